```python
import math
import jax, jax.numpy as jnp
from jax import lax
import numpy as np


D_MODEL = 1024
BATCH = 8
SEQ = 2048
DEPTH = 4

GRID_W = 64
CTX_LEN = 256
HEAD_DIM = 64
N_BRANCH = 3
N_MOD = 9
RMS_EPS = 1e-6

HY_WIDTH = D_MODEL // 2
HY_ORDER = 2
HY_EMB = 33
HY_BANDS = (HY_EMB - 1) // 2
HY_FFN = 64
HY_FAST_DECAY = 0.3
HY_SLOW_DECAY = 1.5
HY_TARGET = 1e-2
HY_FILTER_GAIN = 0.05
HY_COLS = (HY_ORDER + 1) * HY_WIDTH

SWA_HEADS = D_MODEL // 128
SWA_KV_HEADS = SWA_HEADS // 4
SWA_WINDOW = 128
SWA_BLOCK = 128
ROPE_BASE = 10000.0
SWA_COLS = (SWA_HEADS + 2 * SWA_KV_HEADS) * HEAD_DIM

NA_HEADS = D_MODEL // 128
NA_MAX_ROWS = 8
NA_COLS = 16
NA_COL_BLOCK = 16
NA_COL_SPAN = 2 * NA_COLS
NA_TOTAL_COLS = 3 * NA_HEADS * HEAD_DIM

BRANCH_WIDTH = HY_WIDTH
GATE_COLS = N_BRANCH * D_MODEL
IN_COLS = HY_COLS + SWA_COLS + NA_TOTAL_COLS + GATE_COLS
IN_SPLITS = [HY_COLS, HY_COLS + SWA_COLS, HY_COLS + SWA_COLS + NA_TOTAL_COLS]

FFN_HIDDEN = 256 * ((8 * D_MODEL // 3 + 255) // 256)

kernel_name = 'hybrid_hyena_swa_natten_macaron_dit'


def rms_norm(x, gain):
    xf = x.astype(jnp.float32)
    y = xf * lax.rsqrt(jnp.mean(xf * xf, axis=-1, keepdims=True) + RMS_EPS)
    return (y * gain.astype(jnp.float32)).astype(x.dtype)


def modulate(h, shift, scale):
    return h * (1.0 + scale) + shift


def swiglu(h, w_gate, w_up, w_down):
    return (jax.nn.silu(h @ w_gate) * (h @ w_up)) @ w_down


def ffn_half_step(x, mod3, gain, w_gate, w_up, w_down):
    shift, scale, gate = mod3
    h = modulate(rms_norm(x, gain), shift, scale)
    return x + 0.5 * gate * swiglu(h, w_gate, w_up, w_down)


def short_conv3(u, w, b):
    up = jnp.pad(u, ((0, 0), (1, 1), (0, 0)))
    return w[0] * up[:, :-2] + w[1] * up[:, 1:-1] + w[2] * up[:, 2:] + b


def hyena_filters(length, w0, b0, w1, b1, w2, b2, w_out, freq):
    t = jnp.linspace(0.0, 1.0, length, dtype=jnp.float32)[:, None]
    w = (2.0 * math.pi / length) * jnp.arange(length, dtype=jnp.float32)[:, None]
    f = jnp.linspace(1e-4, HY_BANDS - 1, HY_BANDS, dtype=jnp.float32)[None, :]
    z = jnp.concatenate([t, jnp.cos(f * w), -jnp.sin(f * w)], axis=-1)
    a = jnp.sin(freq * (z @ w0 + b0))
    a = jnp.sin(freq * (a @ w1 + b1))
    a = jnp.sin(freq * (a @ w2 + b2))
    h = (a @ w_out).reshape(length, HY_ORDER, 2, HY_WIDTH)
    max_decay = math.log(HY_TARGET) / HY_FAST_DECAY
    min_decay = math.log(HY_TARGET) / HY_SLOW_DECAY
    deltas = jnp.abs(jnp.linspace(min_decay, max_decay, HY_WIDTH, dtype=jnp.float32))
    window = jnp.exp(-t * deltas[None, :])
    return h * window[:, None, None, :]


def two_sided_long_conv(u, h_pos, h_neg, bias):
    length = u.shape[1]
    k = jnp.concatenate([h_pos, jnp.zeros_like(h_pos[:1]), h_neg[:0:-1]], axis=0).astype(jnp.float32)
    uf = u.astype(jnp.float32)
    n = 2 * length
    y = jnp.fft.irfft(jnp.fft.rfft(uf, n=n, axis=1) * jnp.fft.rfft(k, n=n, axis=0)[None], n=n, axis=1)[:, :length]
    return (y + uf * bias.astype(jnp.float32)).astype(u.dtype)


def hyena_branch(z, short_w, short_b, filter_params, bias):
    length = z.shape[1]
    z = short_conv3(z, short_w, short_b)
    v, x1, x2 = jnp.split(z, HY_ORDER + 1, axis=-1)
    h = hyena_filters(length, *filter_params)
    y = x1 * two_sided_long_conv(v, h[:, 0, 0], h[:, 0, 1], bias[0])
    return x2 * two_sided_long_conv(y, h[:, 1, 0], h[:, 1, 1], bias[1])


def split_qkv(z, n_q, n_kv):
    q, k, v = jnp.split(z, [n_q * HEAD_DIM, (n_q + n_kv) * HEAD_DIM], axis=-1)
    lead = z.shape[:-1]
    return (q.reshape(lead + (n_q, HEAD_DIM)), k.reshape(lead + (n_kv, HEAD_DIM)),
            v.reshape(lead + (n_kv, HEAD_DIM)))


def axial_rope_tables(length):
    pos = jnp.arange(length)
    row = (pos // GRID_W).astype(jnp.float32)
    col = (pos % GRID_W).astype(jnp.float32)
    half = HEAD_DIM // 2
    inv = 1.0 / (ROPE_BASE ** (jnp.arange(0, half, 2, dtype=jnp.float32) / half))
    ang_r = row[:, None] * inv[None, :]
    ang_c = col[:, None] * inv[None, :]
    return jnp.cos(ang_r), jnp.sin(ang_r), jnp.cos(ang_c), jnp.sin(ang_c)


def rotate_half_pairs(x, cos, sin):
    x1, x2 = jnp.split(x, 2, axis=-1)
    c = cos[:, None, :]
    s = sin[:, None, :]
    return jnp.concatenate([x1 * c - x2 * s, x2 * c + x1 * s], axis=-1)


def apply_axial_rope(x, tabs):
    cr, sr, cc, sc = tabs
    xr, xcol = jnp.split(x, 2, axis=-1)
    return jnp.concatenate([rotate_half_pairs(xr, cr, sr), rotate_half_pairs(xcol, cc, sc)], axis=-1).astype(x.dtype)


def window_gqa(q, k, v, kc, vc, sink):
    b, length, nh, hd = q.shape
    nkv = k.shape[2]
    g = nh // nkv
    nb = length // SWA_BLOCK
    scale = hd ** -0.5
    qb = q.reshape(b, nb, SWA_BLOCK, nkv, g, hd)
    pad = ((0, 0), (SWA_BLOCK, SWA_BLOCK), (0, 0), (0, 0))
    kp = jnp.pad(k, pad).reshape(b, nb + 2, SWA_BLOCK, nkv, hd)
    vp = jnp.pad(v, pad).reshape(b, nb + 2, SWA_BLOCK, nkv, hd)
    idx = jnp.arange(nb)[:, None] + jnp.arange(3)[None, :]
    span = 3 * SWA_BLOCK
    kb = kp[:, idx].reshape(b, nb, span, nkv, hd)
    vb = vp[:, idx].reshape(b, nb, span, nkv, hd)
    qpos = jnp.arange(length).reshape(nb, SWA_BLOCK)
    kpos = ((idx - 1)[..., None] * SWA_BLOCK + jnp.arange(SWA_BLOCK)).reshape(nb, span)
    valid = ((kpos[:, None, :] >= 0) & (kpos[:, None, :] < length)
             & (jnp.abs(qpos[:, :, None] - kpos[:, None, :]) <= SWA_WINDOW))
    s_loc = jnp.einsum('bnqkgd,bnskd->bnkgqs', qb, kb).astype(jnp.float32) * scale
    s_loc = jnp.where(valid[None, :, None, None], s_loc, -jnp.inf)
    s_ctx = jnp.einsum('bnqkgd,bckd->bnkgqc', qb, kc).astype(jnp.float32) * scale
    s_sink = jnp.broadcast_to(sink.reshape(nkv, g, 1, 1).astype(jnp.float32), s_loc.shape[:-1] + (1,))
    p = jax.nn.softmax(jnp.concatenate([s_loc, s_ctx, s_sink], axis=-1), axis=-1)
    n_ctx = kc.shape[1]
    p_loc = p[..., :span].astype(v.dtype)
    p_ctx = p[..., span:span + n_ctx].astype(v.dtype)
    o = jnp.einsum('bnkgqs,bnskd->bnqkgd', p_loc, vb) + jnp.einsum('bnkgqc,bckd->bnqkgd', p_ctx, vc)
    return o.reshape(b, length, nh * hd)


def neighbourhood_attention(q, k, v, kc, vc, rpb):
    b, length, nh, hd = q.shape
    rows = length // GRID_W
    wr = min(NA_MAX_ROWS, rows)
    ncb = GRID_W // NA_COL_BLOCK
    scale = hd ** -0.5
    qg = q.reshape(b, rows, ncb, NA_COL_BLOCK, nh, hd)
    kg = k.reshape(b, rows, GRID_W, nh, hd)
    vg = v.reshape(b, rows, GRID_W, nh, hd)
    r = jnp.arange(rows)
    row_idx = jnp.clip(r - wr // 2, 0, rows - wr)[:, None] + jnp.arange(wr)[None, :]
    j = jnp.arange(ncb)
    col_idx = (jnp.clip(j * NA_COL_BLOCK - NA_COLS // 2, 0, GRID_W - NA_COL_SPAN)[:, None]
               + jnp.arange(NA_COL_SPAN)[None, :])
    ri = row_idx[:, None, :, None]
    ci = col_idx[None, :, None, :]
    kn = kg[:, ri, ci]
    vn = vg[:, ri, ci]
    s_loc = jnp.einsum('brjqhd,brjwshd->brjhqws', qg, kn).astype(jnp.float32) * scale
    qcol = j[:, None] * NA_COL_BLOCK + jnp.arange(NA_COL_BLOCK)[None, :]
    cstart = jnp.clip(qcol - NA_COLS // 2, 0, GRID_W - NA_COLS)
    kcol = col_idx[:, None, :]
    valid = (kcol >= cstart[..., None]) & (kcol < cstart[..., None] + NA_COLS)
    dr = row_idx - r[:, None] + (NA_MAX_ROWS - 1)
    dc = jnp.clip(kcol - qcol[..., None], -(NA_COLS - 1), NA_COLS - 1) + (NA_COLS - 1)
    bias = rpb[:, dr[:, None, None, :, None], dc[None, :, :, None, :]]
    bias = jnp.moveaxis(bias, 0, 2).astype(jnp.float32)
    s_loc = jnp.where(valid[:, None, :, None, :], s_loc + bias, -jnp.inf)
    n_loc = wr * NA_COL_SPAN
    s_loc = s_loc.reshape(b, rows, ncb, nh, NA_COL_BLOCK, n_loc)
    s_ctx = jnp.einsum('brjqhd,bchd->brjhqc', qg, kc).astype(jnp.float32) * scale
    p = jax.nn.softmax(jnp.concatenate([s_loc, s_ctx], axis=-1), axis=-1)
    p_loc = p[..., :n_loc].reshape(b, rows, ncb, nh, NA_COL_BLOCK, wr, NA_COL_SPAN).astype(v.dtype)
    p_ctx = p[..., n_loc:].astype(v.dtype)
    o = jnp.einsum('brjhqws,brjwshd->brjqhd', p_loc, vn) + jnp.einsum('brjhqc,bchd->brjqhd', p_ctx, vc)
    return o.reshape(b, length, nh * hd)


def context_attention(q, k, v, sink):
    b, n, nh, hd = q.shape
    nkv = k.shape[2]
    g = nh // nkv
    qg = q.reshape(b, n, nkv, g, hd)
    s = jnp.einsum('bqkgd,bckd->bkgqc', qg, k).astype(jnp.float32) * hd ** -0.5
    if sink is not None:
        sk = jnp.broadcast_to(sink.reshape(nkv, g, 1, 1).astype(jnp.float32), s.shape[:-1] + (1,))
        s = jnp.concatenate([s, sk], axis=-1)
    p = jax.nn.softmax(s, axis=-1)[..., :k.shape[1]].astype(v.dtype)
    o = jnp.einsum('bkgqc,bckd->bqkgd', p, v)
    return o.reshape(b, n, nh * hd)


def merge_branches(outs, z_gate, w_branch, w_out):
    proj = jnp.einsum('...ne,ned->...nd', jnp.stack(outs, axis=-2), w_branch)
    gates = jax.nn.sigmoid(z_gate.reshape(z_gate.shape[:-1] + (N_BRANCH, D_MODEL)))
    return jnp.sum(gates * proj, axis=-2) @ w_out


def setup_inputs(seed: int = 0) -> dict:
    key = jax.random.key(seed)
    ks = jax.random.split(key, 32)

    def nrm(k, shape, scale=1.0):
        return jax.random.normal(k, shape, jnp.float32) * scale

    return {
        'x': nrm(ks[0], (BATCH, SEQ, D_MODEL)),
        'c': nrm(ks[1], (BATCH, D_MODEL)),
        'ctx': nrm(ks[2], (BATCH, CTX_LEN, D_MODEL)),
        'c_ctx': nrm(ks[3], (D_MODEL,)),
        'w_ada': nrm(ks[4], (DEPTH, D_MODEL, N_MOD * D_MODEL), 0.5 * D_MODEL ** -0.5),
        'b_ada': nrm(ks[5], (DEPTH, N_MOD * D_MODEL), 0.02),
        'norm_g': 1.0 + nrm(ks[6], (DEPTH, 3, D_MODEL), 0.02),
        'ffn_w_gate': nrm(ks[7], (DEPTH, 2, D_MODEL, FFN_HIDDEN), D_MODEL ** -0.5),
        'ffn_w_up': nrm(ks[8], (DEPTH, 2, D_MODEL, FFN_HIDDEN), D_MODEL ** -0.5),
        'ffn_w_down': nrm(ks[9], (DEPTH, 2, FFN_HIDDEN, D_MODEL), FFN_HIDDEN ** -0.5),
        'w_in': nrm(ks[10], (DEPTH, D_MODEL, IN_COLS), D_MODEL ** -0.5),
        'hy_short_w': nrm(ks[11], (DEPTH, 3, HY_COLS), 3 ** -0.5),
        'hy_short_b': nrm(ks[12], (DEPTH, HY_COLS), 0.02),
        'hy_pe_w0': nrm(ks[13], (DEPTH, HY_EMB, HY_FFN), HY_EMB ** -0.5),
        'hy_pe_b0': nrm(ks[14], (DEPTH, HY_FFN), 0.02),
        'hy_pe_w1': nrm(ks[15], (DEPTH, HY_FFN, HY_FFN), HY_FFN ** -0.5),
        'hy_pe_b1': nrm(ks[16], (DEPTH, HY_FFN), 0.02),
        'hy_pe_w2': nrm(ks[17], (DEPTH, HY_FFN, HY_FFN), HY_FFN ** -0.5),
        'hy_pe_b2': nrm(ks[18], (DEPTH, HY_FFN), 0.02),
        'hy_pe_wout': nrm(ks[19], (DEPTH, HY_FFN, HY_ORDER * 2 * HY_WIDTH), HY_FILTER_GAIN * HY_FFN ** -0.5),
        'hy_sin_freq': 1.0 + nrm(ks[20], (DEPTH, HY_FFN), 0.02),
        'hy_bias': nrm(ks[21], (DEPTH, HY_ORDER, HY_WIDTH), 0.1),
        'swa_q_gain': 1.0 + nrm(ks[22], (DEPTH, HEAD_DIM), 0.02),
        'swa_k_gain': 1.0 + nrm(ks[23], (DEPTH, HEAD_DIM), 0.02),
        'swa_sink': nrm(ks[24], (DEPTH, SWA_HEADS), 0.5),
        'na_q_gain': 1.0 + nrm(ks[25], (DEPTH, HEAD_DIM), 0.02),
        'na_k_gain': 1.0 + nrm(ks[26], (DEPTH, HEAD_DIM), 0.02),
        'na_rpb': nrm(ks[27], (DEPTH, NA_HEADS, 2 * NA_MAX_ROWS - 1, 2 * NA_COLS - 1), 0.1),
        'w_branch': nrm(ks[28], (DEPTH, N_BRANCH, BRANCH_WIDTH, D_MODEL), BRANCH_WIDTH ** -0.5),
        'w_out': nrm(ks[29], (DEPTH, D_MODEL, D_MODEL), D_MODEL ** -0.5),
    }


def reference(x, c, ctx, c_ctx, w_ada, b_ada, norm_g, ffn_w_gate, ffn_w_up, ffn_w_down,
              w_in, hy_short_w, hy_short_b, hy_pe_w0, hy_pe_b0, hy_pe_w1, hy_pe_b1,
              hy_pe_w2, hy_pe_b2, hy_pe_wout, hy_sin_freq, hy_bias,
              swa_q_gain, swa_k_gain, swa_sink, na_q_gain, na_k_gain, na_rpb,
              w_branch, w_out):
    seq = x.shape[1]
    rope = axial_rope_tables(seq)
    xc = ctx
    for i in range(DEPTH):
        update_ctx = i < DEPTH - 1
        mods = jnp.split((jax.nn.silu(c) @ w_ada[i] + b_ada[i])[:, None, :], N_MOD, axis=-1)
        mods_c = jnp.split(jax.nn.silu(c_ctx) @ w_ada[i] + b_ada[i], N_MOD, axis=-1)

        x = ffn_half_step(x, mods[0:3], norm_g[i, 0], ffn_w_gate[i, 0], ffn_w_up[i, 0], ffn_w_down[i, 0])
        xc = ffn_half_step(xc, mods_c[0:3], norm_g[i, 0], ffn_w_gate[i, 0], ffn_w_up[i, 0], ffn_w_down[i, 0])

        h = modulate(rms_norm(x, norm_g[i, 1]), mods[3], mods[4])
        hc = modulate(rms_norm(xc, norm_g[i, 1]), mods_c[3], mods_c[4])
        z_hy, z_swa, z_na, z_gate = jnp.split(h @ w_in[i], IN_SPLITS, axis=-1)
        zc_hy, zc_swa, zc_na, zc_gate = jnp.split(hc @ w_in[i], IN_SPLITS, axis=-1)
        filt = (hy_pe_w0[i], hy_pe_b0[i], hy_pe_w1[i], hy_pe_b1[i], hy_pe_w2[i], hy_pe_b2[i],
                hy_pe_wout[i], hy_sin_freq[i])

        qc_s, kc_s, vc_s = split_qkv(zc_swa, SWA_HEADS, SWA_KV_HEADS)
        kc_s = rms_norm(kc_s, swa_k_gain[i])
        qc_n, kc_n, vc_n = split_qkv(zc_na, NA_HEADS, NA_HEADS)
        kc_n = rms_norm(kc_n, na_k_gain[i])

        q_s, k_s, v_s = split_qkv(z_swa, SWA_HEADS, SWA_KV_HEADS)
        q_s = apply_axial_rope(rms_norm(q_s, swa_q_gain[i]), rope)
        k_s = apply_axial_rope(rms_norm(k_s, swa_k_gain[i]), rope)
        q_n, k_n, v_n = split_qkv(z_na, NA_HEADS, NA_HEADS)
        q_n = rms_norm(q_n, na_q_gain[i])
        k_n = rms_norm(k_n, na_k_gain[i])
        y_hy = hyena_branch(z_hy, hy_short_w[i], hy_short_b[i], filt, hy_bias[i])
        y_swa = window_gqa(q_s, k_s, v_s, kc_s, vc_s, swa_sink[i])
        y_na = neighbourhood_attention(q_n, k_n, v_n, kc_n, vc_n, na_rpb[i])
        x = x + mods[5] * merge_branches((y_hy, y_swa, y_na), z_gate, w_branch[i], w_out[i])

        if update_ctx:
            yc_hy = hyena_branch(zc_hy, hy_short_w[i], hy_short_b[i], filt, hy_bias[i])
            yc_swa = context_attention(rms_norm(qc_s, swa_q_gain[i]), kc_s, vc_s, swa_sink[i])
            yc_na = context_attention(rms_norm(qc_n, na_q_gain[i]), kc_n, vc_n, None)
            xc = xc + mods_c[5] * merge_branches((yc_hy, yc_swa, yc_na), zc_gate, w_branch[i], w_out[i])

        x = ffn_half_step(x, mods[6:9], norm_g[i, 2], ffn_w_gate[i, 1], ffn_w_up[i, 1], ffn_w_down[i, 1])
        if update_ctx:
            xc = ffn_half_step(xc, mods_c[6:9], norm_g[i, 2], ffn_w_gate[i, 1], ffn_w_up[i, 1], ffn_w_down[i, 1])
    return x
```

```python
import functools
import math

import jax
import jax.numpy as jnp
from jax import lax
from jax.experimental import pallas as pl
from jax.experimental.pallas import tpu as pltpu

F32 = jnp.float32
BF16 = jnp.bfloat16

D_MODEL = 1024
DEPTH = 4
GRID_W = 64
HEAD_DIM = 64
N_MOD = 9
RMS_EPS = 1e-6

HY_WIDTH = D_MODEL // 2
HY_EMB = 33
HY_BANDS = (HY_EMB - 1) // 2
HY_FFN = 64
HY_FAST_DECAY = 0.3
HY_SLOW_DECAY = 1.5
HY_TARGET = 1e-2

SWA_HEADS = 8
SWA_KV_HEADS = 2
SWA_WINDOW = 128
SWA_BLOCK = 128
ROPE_BASE = 10000.0

NA_HEADS = 8
NA_MAX_ROWS = 8
NA_COLS = 16
NA_COL_BLOCK = 16

FFN_HIDDEN = 256 * ((8 * D_MODEL // 3 + 255) // 256)

LANES = 128
MXU_DIM = 256
MOD_ROWS = 16
NEG_BIG = -1e30


def _cparams(sem, vmem_mb):
    return pltpu.CompilerParams(dimension_semantics=sem, vmem_limit_bytes=vmem_mb * 1024 * 1024)


def _resident(shape):
    nd = len(shape)
    return pl.BlockSpec(shape, lambda *_: (0,) * nd, pipeline_mode=pl.Buffered(1))


def _split(x):
    hi = x.astype(BF16)
    lo = (x - hi.astype(F32)).astype(BF16)
    return hi, lo


def _dot(a, b):
    return jnp.dot(a, b, preferred_element_type=F32)


def _dot3(a_hi, a_lo, b_hi, b_lo):
    return _dot(a_hi, b_hi) + _dot(a_lo, b_hi) + _dot(a_hi, b_lo)


def _dot3f(a, b):
    a_hi, a_lo = _split(a)
    b_hi, b_lo = _split(b)
    return _dot3(a_hi, a_lo, b_hi, b_lo)


def _rms_mod(x, gain, shift, scale):
    ms = jnp.mean(x * x, axis=-1, keepdims=True)
    y = x * lax.rsqrt(ms + RMS_EPS) * gain
    return y * (1.0 + scale) + shift


def _head_rms(x, bd, gain):
    outs = []
    for j in range(x.shape[1] // MXU_DIM):
        xc = x[:, j * MXU_DIM:(j + 1) * MXU_DIM]
        hi, lo = _split(xc * xc)
        ssum = _dot(hi, bd) + _dot(lo, bd)
        outs.append(xc * lax.rsqrt(ssum * (1.0 / HEAD_DIM) + RMS_EPS) * gain)
    return outs[0] if len(outs) == 1 else jnp.concatenate(outs, axis=1)


def _rope(x, cos, sin):
    lane = lax.broadcasted_iota(jnp.int32, (x.shape[0], LANES), 1)
    first = (lane & 31) < 16
    outs = []
    for j in range(x.shape[1] // LANES):
        xc = x[:, j * LANES:(j + 1) * LANES]
        partner = jnp.where(first, pltpu.roll(xc, LANES - 16, axis=1), pltpu.roll(xc, 16, axis=1))
        outs.append(xc * cos + partner * sin)
    return outs[0] if len(outs) == 1 else jnp.concatenate(outs, axis=1)


def _mods_kernel(c_ref, w_ref, b_ref, o_ref):
    c = c_ref[...]
    a = c * jax.nn.sigmoid(c)
    o_ref[0] = _dot3f(a, w_ref[0]) + b_ref[0]


def _adaln_mods(c16, w_ada, b_ada):
    depth, d, n = w_ada.shape
    nb = 1152
    return pl.pallas_call(
        _mods_kernel,
        grid=(depth, n // nb),
        in_specs=[
            pl.BlockSpec((MOD_ROWS, d), lambda i, j: (0, 0)),
            pl.BlockSpec((1, d, nb), lambda i, j: (i, 0, j)),
            pl.BlockSpec((1, 1, nb), lambda i, j: (i, 0, j)),
        ],
        out_specs=pl.BlockSpec((1, MOD_ROWS, nb), lambda i, j: (i, 0, j)),
        out_shape=jax.ShapeDtypeStruct((depth, MOD_ROWS, n), F32),
        compiler_params=_cparams(("arbitrary", "arbitrary"), 48),
    )(c16, w_ada, b_ada.reshape(depth, 1, n))


def _ffn_kernel(x_ref, sh_ref, sc_ref, gt_ref, g_ref, wg_ref, wu_ref, wd_ref, o_ref, *, fc):
    x = x_ref[0]
    hb = _rms_mod(x, g_ref[...], sh_ref[0], sc_ref[0]).astype(BF16)
    acc = None
    for f0 in range(0, FFN_HIDDEN, fc):
        g = _dot(hb, wg_ref[:, f0:f0 + fc])
        u = _dot(hb, wu_ref[:, f0:f0 + fc])
        a = (g * jax.nn.sigmoid(g) * u).astype(BF16)
        d = _dot(a, wd_ref[f0:f0 + fc, :])
        acc = d if acc is None else acc + d
    o_ref[0] = x + 0.5 * gt_ref[0] * acc


def _mod_spec(mod):
    if mod.shape[0] == 1:
        return pl.BlockSpec((1, 1, D_MODEL), lambda b, t: (0, 0, 0))
    return pl.BlockSpec((1, 1, D_MODEL), lambda b, t: (b, 0, 0))


def _ffn_half_step(x, shift, scale, gate, gain, wg, wu, wd):
    b, l, d = x.shape
    tm = min(512, l)
    return pl.pallas_call(
        functools.partial(_ffn_kernel, fc=FFN_HIDDEN // 2),
        grid=(b, l // tm),
        in_specs=[
            pl.BlockSpec((1, tm, d), lambda b, t: (b, t, 0)),
            _mod_spec(shift), _mod_spec(scale), _mod_spec(gate),
            _resident((1, d)),
            _resident(wg.shape), _resident(wu.shape), _resident(wd.shape),
        ],
        out_specs=pl.BlockSpec((1, tm, d), lambda b, t: (b, t, 0)),
        out_shape=jax.ShapeDtypeStruct(x.shape, F32),
        compiler_params=_cparams(("parallel", "parallel"), 56),
    )(x, shift, scale, gate, gain.reshape(1, d), wg, wu, wd)


def _inproj_kernel(x_ref, sh_ref, sc_ref, g_ref, cos_ref, sin_ref, hg_ref, bd_ref,
                   w_hy, w_qs, w_ks, w_vs, w_qn, w_kn, w_vn, w_gt,
                   zhy_ref, qs_ref, ks_ref, vs_ref, qn_ref, kn_ref, vn_ref, zg_ref):
    hb = _rms_mod(x_ref[0], g_ref[...], sh_ref[0], sc_ref[0]).astype(BF16)
    cos = cos_ref[...]
    sin = sin_ref[...]
    bd = bd_ref[...]
    scale = HEAD_DIM ** -0.5
    zhy_ref[0] = _dot(hb, w_hy[...])
    zg_ref[0] = _dot(hb, w_gt[...])
    qs = _rope(_head_rms(_dot(hb, w_qs[...]), bd, hg_ref[0:1, :]), cos, sin)
    qs_ref[0] = (qs * scale).astype(BF16)
    ks = _rope(_head_rms(_dot(hb, w_ks[...]), bd, hg_ref[1:2, :]), cos, sin)
    ks_ref[0] = ks.astype(BF16)
    vs_ref[0] = _dot(hb, w_vs[...]).astype(BF16)
    qn = _head_rms(_dot(hb, w_qn[...]), bd, hg_ref[2:3, :])
    qn_ref[0] = (qn * scale).astype(BF16)
    kn_ref[0] = _head_rms(_dot(hb, w_kn[...]), bd, hg_ref[3:4, :]).astype(BF16)
    vn_ref[0] = _dot(hb, w_vn[...]).astype(BF16)


def _in_projection(x, shift, scale, gain, cos, sin, head_gains, bd, ws):
    b, l, d = x.shape
    tm = min(256, l)
    widths = [w.shape[1] for w in ws]
    dtypes = [F32, BF16, BF16, BF16, BF16, BF16, BF16, F32]
    return pl.pallas_call(
        _inproj_kernel,
        grid=(b, l // tm),
        in_specs=[
            pl.BlockSpec((1, tm, d), lambda b, t: (b, t, 0)),
            _mod_spec(shift), _mod_spec(scale),
            _resident((1, d)),
            pl.BlockSpec((tm, LANES), lambda b, t: (t, 0)),
            pl.BlockSpec((tm, LANES), lambda b, t: (t, 0)),
            _resident(head_gains.shape), _resident(bd.shape),
        ] + [_resident(w.shape) for w in ws],
        out_specs=[pl.BlockSpec((1, tm, n), lambda b, t: (b, t, 0)) for n in widths],
        out_shape=[jax.ShapeDtypeStruct((b, l, n), dt) for n, dt in zip(widths, dtypes)],
        compiler_params=_cparams(("parallel", "parallel"), 56),
    )(x, shift, scale, gain.reshape(1, d), cos, sin, head_gains, bd, *ws)


def _shortconv_kernel(z_ref, w_ref, b_ref, o_ref):
    z = z_ref[0]
    l = z.shape[0]
    row = lax.broadcasted_iota(jnp.int32, z.shape, 0)
    prev = jnp.where(row == 0, 0.0, pltpu.roll(z, 1, axis=0))
    nxt = jnp.where(row == l - 1, 0.0, pltpu.roll(z, l - 1, axis=0))
    o_ref[0] = w_ref[0:1, :] * prev + w_ref[1:2, :] * z + w_ref[2:3, :] * nxt + b_ref[...]


def _short_conv3(z, w, bias):
    b, l, n = z.shape
    cb = 256
    return pl.pallas_call(
        _shortconv_kernel,
        grid=(b, n // cb),
        in_specs=[
            pl.BlockSpec((1, l, cb), lambda b, j: (b, 0, j)),
            pl.BlockSpec((3, cb), lambda b, j: (0, j)),
            pl.BlockSpec((1, cb), lambda b, j: (0, j)),
        ],
        out_specs=pl.BlockSpec((1, l, cb), lambda b, j: (b, 0, j)),
        out_shape=jax.ShapeDtypeStruct(z.shape, F32),
        compiler_params=_cparams(("parallel", "parallel"), 32),
    )(z, w, bias.reshape(1, n))


def _filter_kernel(z_ref, w0, b0, w1, b1, w2, b2, wo, fr_ref, dl_ref, a_ref, d_ref):
    z = z_ref[...]
    fr = fr_ref[...]
    a = jnp.sin(fr * (_dot3f(z, w0[...]) + b0[...]))
    a = jnp.sin(fr * (_dot3f(a, w1[...]) + b1[...]))
    a = jnp.sin(fr * (_dot3f(a, w2[...]) + b2[...]))
    hh = _dot3f(a, wo[...])
    t = z[:, 0:1]
    win = jnp.exp(-t * dl_ref[...])
    row = lax.broadcasted_iota(jnp.int32, win.shape, 0) + pl.program_id(0) * z.shape[0]
    w = HY_WIDTH
    for o in range(2):
        hp = hh[:, (2 * o) * w:(2 * o + 1) * w] * win
        hn = jnp.where(row == 0, 0.0, hh[:, (2 * o + 1) * w:(2 * o + 2) * w] * win)
        a_ref[:, o * w:(o + 1) * w] = hp + hn
        d_ref[:, o * w:(o + 1) * w] = hp - hn


def _hyena_filter_sums(zfeat, fp, deltas):
    l = zfeat.shape[0]
    tl = min(256, l)
    w0, b0, w1, b1, w2, b2, wo, fr = fp
    n = 2 * HY_WIDTH
    consts = [w0, b0, w1, b1, w2, b2, wo, fr, deltas]
    return pl.pallas_call(
        _filter_kernel,
        grid=(l // tl,),
        in_specs=[pl.BlockSpec((tl, LANES), lambda t: (t, 0))] + [_resident(c.shape) for c in consts],
        out_specs=[pl.BlockSpec((tl, n), lambda t: (t, 0))] * 2,
        out_shape=[jax.ShapeDtypeStruct((l, n), F32)] * 2,
        compiler_params=_cparams(("parallel",), 32),
    )(zfeat, *consts)


def _spectrum_kernel(a_ref, d_ref, ch, cl, sh, sl, kr_ref, ki_ref):
    a_hi, a_lo = _split(a_ref[...])
    d_hi, d_lo = _split(d_ref[...])
    kr_ref[...] = _dot3(ch[...], cl[...], a_hi, a_lo)
    ki_ref[...] = -_dot3(sh[...], sl[...], d_hi, d_lo)


def _filter_spectrum(asum, dsum, tabs):
    l, n = asum.shape
    fp = tabs["c_hi"].shape[0]
    fb = tabs["fb"]
    cb = 256
    tab = pl.BlockSpec((fb, l), lambda j, k: (k, 0))
    col = pl.BlockSpec((l, cb), lambda j, k: (0, j))
    out = pl.BlockSpec((fb, cb), lambda j, k: (k, j))
    return pl.pallas_call(
        _spectrum_kernel,
        grid=(n // cb, fp // fb),
        in_specs=[col, col, tab, tab, tab, tab],
        out_specs=[out, out],
        out_shape=[jax.ShapeDtypeStruct((fp, n), F32)] * 2,
        compiler_params=_cparams(("parallel", "parallel"), 48),
    )(asum, dsum, tabs["c_hi"], tabs["c_lo"], tabs["s_hi"], tabs["s_lo"])


def _longconv_kernel(u_ref, g_ref, kr_ref, ki_ref, bias_ref, wk_ref, ch, cl, sh, sl, cth, ctl, sth, stl,
                     o_ref, uh_ref, ul_ref, acc_ref):
    kb = pl.program_id(2)

    @pl.when(kb == 0)
    def _():
        hi, lo = _split(u_ref[0])
        uh_ref[...] = hi
        ul_ref[...] = lo
        acc_ref[...] = jnp.zeros_like(acc_ref)

    uh = uh_ref[...]
    ul = ul_ref[...]
    xr = _dot3(ch[...], cl[...], uh, ul)
    xi = -_dot3(sh[...], sl[...], uh, ul)
    kr = kr_ref[...]
    ki = ki_ref[...]
    wk = wk_ref[...]
    yr_hi, yr_lo = _split((xr * kr - xi * ki) * wk)
    yi_hi, yi_lo = _split((xr * ki + xi * kr) * wk)
    acc_ref[...] += _dot3(cth[...], ctl[...], yr_hi, yr_lo) - _dot3(sth[...], stl[...], yi_hi, yi_lo)

    @pl.when(kb == pl.num_programs(2) - 1)
    def _():
        u = u_ref[0]
        o_ref[0] = g_ref[0] * (acc_ref[...] + u * bias_ref[...])


def _gated_long_conv(u_arr, u_col, g_arr, g_col, kr, ki, k_col, bias, tabs):
    b, l, _ = u_arr.shape
    cb = 256
    ncb = HY_WIDTH // cb
    fp = tabs["c_hi"].shape[0]
    fb = tabs["fb"]
    tab = pl.BlockSpec((fb, l), lambda b, j, k: (k, 0))
    tabt = pl.BlockSpec((l, fb), lambda b, j, k: (0, k))
    spec = pl.BlockSpec((fb, cb), lambda b, j, k: (k, k_col * ncb + j))
    return pl.pallas_call(
        _longconv_kernel,
        grid=(b, ncb, fp // fb),
        in_specs=[
            pl.BlockSpec((1, l, cb), lambda b, j, k: (b, 0, u_col * ncb + j)),
            pl.BlockSpec((1, l, cb), lambda b, j, k: (b, 0, g_col * ncb + j)),
            spec, spec,
            pl.BlockSpec((1, cb), lambda b, j, k: (0, j)),
            pl.BlockSpec((fb, 1), lambda b, j, k: (k, 0)),
            tab, tab, tab, tab, tabt, tabt, tabt, tabt,
        ],
        out_specs=pl.BlockSpec((1, l, cb), lambda b, j, k: (b, 0, j)),
        out_shape=jax.ShapeDtypeStruct((b, l, HY_WIDTH), F32),
        scratch_shapes=[pltpu.VMEM((l, cb), BF16), pltpu.VMEM((l, cb), BF16), pltpu.VMEM((l, cb), F32)],
        compiler_params=_cparams(("parallel", "parallel", "arbitrary"), 56),
    )(u_arr, g_arr, kr, ki, bias.reshape(1, HY_WIDTH), tabs["wk"],
      tabs["c_hi"], tabs["c_lo"], tabs["s_hi"], tabs["s_lo"],
      tabs["ct_hi"], tabs["ct_lo"], tabs["st_hi"], tabs["st_lo"])


def _softmax_pv(s, v, sink=None):
    m = jnp.max(s, axis=-1, keepdims=True)
    if sink is not None:
        m = jnp.maximum(m, sink)
    p = jnp.exp(s - m)
    den = jnp.sum(p, axis=-1, keepdims=True)
    if sink is not None:
        den = den + jnp.exp(sink - m)
    return _dot(p.astype(BF16), v) / den


def _qk(q, k):
    return lax.dot_general(q, k, (((1,), (1,)), ((), ())), preferred_element_type=F32)


def _half_mask(q_tile, half):
    lane = lax.broadcasted_iota(jnp.int32, q_tile.shape, 1)
    keep = (lane < HEAD_DIM) if half == 0 else (lane >= HEAD_DIM)
    return jnp.where(keep, q_tile, jnp.zeros_like(q_tile))


def _merge_halves(o_even, o_odd):
    lane = lax.broadcasted_iota(jnp.int32, o_even.shape, 1)
    return jnp.where(lane < HEAD_DIM, o_even, o_odd)


def _swa_kernel(sink_ref, q_ref, kp_ref, kc_ref, kn_ref, vp_ref, vc_ref, vn_ref, kx_ref, vx_ref, o_ref):
    n = pl.program_id(1)
    nb = pl.num_programs(1)
    blk = SWA_BLOCK
    g = SWA_HEADS // SWA_KV_HEADS
    n_ctx = kx_ref.shape[1]
    rows = g * blk
    ri = lax.broadcasted_iota(jnp.int32, (rows, 3 * blk + n_ctx), 0) & (blk - 1)
    ci = lax.broadcasted_iota(jnp.int32, (rows, 3 * blk + n_ctx), 1)
    ok_prev = (ci < blk) & (ci >= ri) & (n > 0)
    ok_cur = (ci >= blk) & (ci < 2 * blk)
    ok_next = (ci >= 2 * blk) & (ci < 3 * blk) & (ci - 2 * blk <= ri) & (n < nb - 1)
    ok = ok_prev | ok_cur | ok_next | (ci >= 3 * blk)
    bias = jnp.where(ok, 0.0, NEG_BIG)
    rsel = lax.broadcasted_iota(jnp.int32, (rows, 1), 0) >> (blk.bit_length() - 1)
    outs = [None] * SWA_HEADS
    for kv in range(SWA_KV_HEADS):
        cs = slice(kv * LANES, (kv + 1) * LANES)
        k = jnp.concatenate([kp_ref[0, :, cs], kc_ref[0, :, cs], kn_ref[0, :, cs], kx_ref[0, :, cs]], axis=0)
        v = jnp.concatenate([vp_ref[0, :, cs], vc_ref[0, :, cs], vn_ref[0, :, cs], vx_ref[0, :, cs]], axis=0)
        q_parts = []
        sink = jnp.zeros((rows, 1), F32)
        for gi in range(g):
            h = kv * g + gi
            q_parts.append(_half_mask(q_ref[0, :, (h // 2) * LANES:(h // 2 + 1) * LANES], h % 2))
            sink = jnp.where(rsel == gi, sink_ref[h], sink)
        s = _qk(jnp.concatenate(q_parts, axis=0), k) + bias
        o = _softmax_pv(s, v, sink)
        for gi in range(g):
            outs[kv * g + gi] = o[gi * blk:(gi + 1) * blk]
    for p in range(SWA_HEADS // 2):
        o_ref[0, :, p * LANES:(p + 1) * LANES] = _merge_halves(outs[2 * p], outs[2 * p + 1]).astype(o_ref.dtype)


def _window_gqa(q, k, v, kx, vx, sink):
    b, l, _ = q.shape
    nb = l // SWA_BLOCK
    n_ctx = kx.shape[1]
    kvw = k.shape[2]

    def blk(off):
        return pl.BlockSpec((1, SWA_BLOCK, kvw), lambda b, n: (b, jnp.clip(n + off, 0, nb - 1), 0))

    ctx = pl.BlockSpec((1, n_ctx, kvw), lambda b, n: (b, 0, 0))
    return pl.pallas_call(
        _swa_kernel,
        grid=(b, nb),
        in_specs=[
            pl.BlockSpec(memory_space=pltpu.SMEM),
            pl.BlockSpec((1, SWA_BLOCK, q.shape[2]), lambda b, n: (b, n, 0)),
            blk(-1), blk(0), blk(1), blk(-1), blk(0), blk(1), ctx, ctx,
        ],
        out_specs=pl.BlockSpec((1, SWA_BLOCK, q.shape[2]), lambda b, n: (b, n, 0)),
        out_shape=jax.ShapeDtypeStruct(q.shape, BF16),
        compiler_params=_cparams(("parallel", "parallel"), 32),
    )(sink, q, k, k, k, v, v, v, kx, vx)


def _na_bias_kernel(rpb_ref, o_ref):
    off = pl.program_id(0)
    h = pl.program_id(1)
    nd = 2 * NA_COLS - 1
    q = lax.broadcasted_iota(jnp.int32, (GRID_W, LANES), 0)
    lane = lax.broadcasted_iota(jnp.int32, (GRID_W, LANES), 1)
    kc = lane & (GRID_W - 1)
    upper = lane >= GRID_W
    cstart = jnp.clip(q - NA_COLS // 2, 0, GRID_W - NA_COLS)
    valid = (kc >= cstart) & (kc < cstart + NA_COLS)
    dc = jnp.clip(kc - q, -(NA_COLS - 1), NA_COLS - 1) + (NA_COLS - 1)
    for j in range(NA_MAX_ROWS // 2):
        base0 = (h * (2 * NA_MAX_ROWS - 1) + (2 * j - off + NA_MAX_ROWS - 1)) * nd
        base1 = base0 + nd
        t = jnp.zeros((GRID_W, LANES), F32)
        for d in range(nd):
            val = jnp.where(upper, rpb_ref[base1 + d], rpb_ref[base0 + d])
            t = jnp.where(dc == d, val, t)
        o_ref[0, 0, :, j * LANES:(j + 1) * LANES] = jnp.where(valid, t, NEG_BIG)


def _na_bias_table(rpb):
    return pl.pallas_call(
        _na_bias_kernel,
        grid=(NA_MAX_ROWS, NA_HEADS),
        in_specs=[pl.BlockSpec(memory_space=pltpu.SMEM)],
        out_specs=pl.BlockSpec((1, 1, GRID_W, NA_MAX_ROWS * GRID_W), lambda o, h: (o, h, 0, 0)),
        out_shape=jax.ShapeDtypeStruct((NA_MAX_ROWS, NA_HEADS, GRID_W, NA_MAX_ROWS * GRID_W), F32),
        compiler_params=_cparams(("arbitrary", "arbitrary"), 16),
    )(rpb.reshape(-1))


def _na_kernel(q_ref, k_ref, v_ref, kx_ref, vx_ref, bias_ref, o_ref, *, rows):
    r = pl.program_id(1)
    wr = NA_MAX_ROWS
    start = pl.multiple_of(jnp.clip(r - wr // 2, 0, rows - wr) * GRID_W, GRID_W)
    nloc = wr * GRID_W
    for p in range(NA_HEADS // 2):
        cs = slice(p * LANES, (p + 1) * LANES)
        q_tile = q_ref[0, :, cs]
        q2 = jnp.concatenate([_half_mask(q_tile, 0), _half_mask(q_tile, 1)], axis=0)
        k = jnp.concatenate([k_ref[0, pl.ds(start, nloc), cs], kx_ref[0, :, cs]], axis=0)
        v = jnp.concatenate([v_ref[0, pl.ds(start, nloc), cs], vx_ref[0, :, cs]], axis=0)
        s = _qk(q2, k)
        bias = jnp.concatenate([bias_ref[0, 2 * p], bias_ref[0, 2 * p + 1]], axis=0)
        s = jnp.concatenate([s[:, :nloc] + bias, s[:, nloc:]], axis=1)
        o = _softmax_pv(s, v)
        o_ref[0, :, cs] = _merge_halves(o[:GRID_W], o[GRID_W:]).astype(o_ref.dtype)


def _neighbourhood_attention(q, k, v, kx, vx, bias_tab):
    b, l, w = q.shape
    rows = l // GRID_W
    n_ctx = kx.shape[1]

    def off(r):
        return r - jnp.clip(r - NA_MAX_ROWS // 2, 0, rows - NA_MAX_ROWS)

    full = pl.BlockSpec((1, l, w), lambda b, r: (b, 0, 0))
    ctx = pl.BlockSpec((1, n_ctx, w), lambda b, r: (b, 0, 0))
    return pl.pallas_call(
        functools.partial(_na_kernel, rows=rows),
        grid=(b, rows),
        in_specs=[
            pl.BlockSpec((1, GRID_W, w), lambda b, r: (b, r, 0)),
            full, full, ctx, ctx,
            pl.BlockSpec((1, NA_HEADS, GRID_W, NA_MAX_ROWS * GRID_W), lambda b, r: (off(r), 0, 0, 0)),
        ],
        out_specs=pl.BlockSpec((1, GRID_W, w), lambda b, r: (b, r, 0)),
        out_shape=jax.ShapeDtypeStruct(q.shape, BF16),
        compiler_params=_cparams(("parallel", "arbitrary"), 40),
    )(q, k, v, kx, vx, bias_tab)


def _ctx_attn_kernel(sink_ref, qs_ref, ks_ref, vs_ref, qn_ref, kn_ref, vn_ref, os_ref, on_ref):
    n = qs_ref.shape[1]
    g = SWA_HEADS // SWA_KV_HEADS
    rsel = lax.broadcasted_iota(jnp.int32, (g * n, 1), 0) >> (n.bit_length() - 1)
    outs = [None] * SWA_HEADS
    for kv in range(SWA_KV_HEADS):
        cs = slice(kv * LANES, (kv + 1) * LANES)
        q_parts = []
        sink = jnp.zeros((g * n, 1), F32)
        for gi in range(g):
            h = kv * g + gi
            q_parts.append(_half_mask(qs_ref[0, :, (h // 2) * LANES:(h // 2 + 1) * LANES], h % 2))
            sink = jnp.where(rsel == gi, sink_ref[h], sink)
        o = _softmax_pv(_qk(jnp.concatenate(q_parts, axis=0), ks_ref[0, :, cs]), vs_ref[0, :, cs], sink)
        for gi in range(g):
            outs[kv * g + gi] = o[gi * n:(gi + 1) * n]
    for p in range(SWA_HEADS // 2):
        os_ref[0, :, p * LANES:(p + 1) * LANES] = _merge_halves(outs[2 * p], outs[2 * p + 1]).astype(os_ref.dtype)
    for p in range(NA_HEADS // 2):
        cs = slice(p * LANES, (p + 1) * LANES)
        q_tile = qn_ref[0, :, cs]
        q2 = jnp.concatenate([_half_mask(q_tile, 0), _half_mask(q_tile, 1)], axis=0)
        o = _softmax_pv(_qk(q2, kn_ref[0, :, cs]), vn_ref[0, :, cs])
        on_ref[0, :, cs] = _merge_halves(o[:n], o[n:]).astype(on_ref.dtype)


def _context_attention(qs, ks, vs, qn, kn, vn, sink):
    b, n, w = qs.shape
    spec = lambda a: pl.BlockSpec((1,) + a.shape[1:], lambda b: (b, 0, 0))
    return pl.pallas_call(
        _ctx_attn_kernel,
        grid=(b,),
        in_specs=[pl.BlockSpec(memory_space=pltpu.SMEM)] + [spec(a) for a in (qs, ks, vs, qn, kn, vn)],
        out_specs=[spec(qs), spec(qn)],
        out_shape=[jax.ShapeDtypeStruct(qs.shape, BF16), jax.ShapeDtypeStruct(qn.shape, BF16)],
        compiler_params=_cparams(("parallel",), 32),
    )(sink, qs, ks, vs, qn, kn, vn)


def _merge_kernel(x_ref, gt_ref, yh_ref, ys_ref, yn_ref, zg_ref, wb_ref, wo_ref, o_ref):
    d = D_MODEL
    ys = (yh_ref[0].astype(BF16), ys_ref[0], yn_ref[0])
    m = None
    for n in range(3):
        proj = _dot(ys[n], wb_ref[n])
        term = jax.nn.sigmoid(zg_ref[0, :, n * d:(n + 1) * d]) * proj
        m = term if m is None else m + term
    o_ref[0] = x_ref[0] + gt_ref[0] * _dot(m.astype(BF16), wo_ref[...])


def _merge_branches(x, gate, y_hy, y_swa, y_na, z_gate, w_branch, w_out):
    b, l, d = x.shape
    tm = min(512, l)
    tok = lambda n: pl.BlockSpec((1, tm, n), lambda b, t: (b, t, 0))
    return pl.pallas_call(
        _merge_kernel,
        grid=(b, l // tm),
        in_specs=[tok(d), _mod_spec(gate), tok(HY_WIDTH), tok(HY_WIDTH), tok(HY_WIDTH), tok(3 * d),
                  _resident(w_branch.shape), _resident(w_out.shape)],
        out_specs=tok(d),
        out_shape=jax.ShapeDtypeStruct(x.shape, F32),
        compiler_params=_cparams(("parallel", "parallel"), 48),
    )(x, gate, y_hy, y_swa, y_na, z_gate, w_branch, w_out)


def _dft_tables(l):
    n = 2 * l
    fb = 256 if l >= 1024 else LANES * ((l + 1 + LANES - 1) // LANES)
    fp = fb * ((l + 1 + fb - 1) // fb)
    k = jnp.arange(fp, dtype=jnp.int32)[:, None]
    t = jnp.arange(l, dtype=jnp.int32)[None, :]
    ang = ((k * t) % n).astype(F32) * (2.0 * math.pi / n)
    live = k <= l
    c = jnp.where(live, jnp.cos(ang), 0.0)
    s = jnp.where(live, jnp.sin(ang), 0.0)
    c_hi, c_lo = _split(c)
    s_hi, s_lo = _split(s)
    wk = jnp.where((k == 0) | (k == l), 1.0, 2.0) * jnp.where(live, 1.0 / n, 0.0)
    return dict(fb=fb, c_hi=c_hi, c_lo=c_lo, s_hi=s_hi, s_lo=s_lo,
                ct_hi=c_hi.T, ct_lo=c_lo.T, st_hi=s_hi.T, st_lo=s_lo.T, wk=wk.astype(F32))


def _filter_features(l):
    t = jnp.linspace(0.0, 1.0, l, dtype=F32)[:, None]
    w = (2.0 * math.pi / l) * jnp.arange(l, dtype=F32)[:, None]
    f = jnp.linspace(1e-4, HY_BANDS - 1, HY_BANDS, dtype=F32)[None, :]
    z = jnp.concatenate([t, jnp.cos(f * w), -jnp.sin(f * w)], axis=-1)
    return jnp.pad(z, ((0, 0), (0, LANES - HY_EMB)))


def _decay_rates():
    max_decay = math.log(HY_TARGET) / HY_FAST_DECAY
    min_decay = math.log(HY_TARGET) / HY_SLOW_DECAY
    return jnp.abs(jnp.linspace(min_decay, max_decay, HY_WIDTH, dtype=F32))[None, :]


def _rope_tables(l):
    pos = jnp.arange(l)
    row = (pos // GRID_W).astype(F32)
    col = (pos % GRID_W).astype(F32)
    half = HEAD_DIM // 2
    inv = 1.0 / (ROPE_BASE ** (jnp.arange(0, half, 2, dtype=F32) / half))
    ar = row[:, None] * inv[None, :]
    ac = col[:, None] * inv[None, :]
    cos = jnp.concatenate([jnp.cos(ar), jnp.cos(ar), jnp.cos(ac), jnp.cos(ac)], axis=-1)
    sin = jnp.concatenate([-jnp.sin(ar), jnp.sin(ar), -jnp.sin(ac), jnp.sin(ac)], axis=-1)
    return jnp.tile(cos, (1, 2)), jnp.tile(sin, (1, 2))


def _pad_to(a, shape):
    return jnp.pad(a, [(0, s - d) for d, s in zip(a.shape, shape)])


def _hyena_branch(z_hy, short_w, short_b, kr, ki, bias, tabs):
    zc = _short_conv3(z_hy, short_w, short_b)
    y1 = _gated_long_conv(zc, 0, zc, 1, kr, ki, 0, bias[0], tabs)
    return _gated_long_conv(y1, 0, zc, 2, kr, ki, 1, bias[1], tabs)


def kernel(x, c, ctx, c_ctx, w_ada, b_ada, norm_g, ffn_w_gate, ffn_w_up, ffn_w_down,
           w_in, hy_short_w, hy_short_b, hy_pe_w0, hy_pe_b0, hy_pe_w1, hy_pe_b1,
           hy_pe_w2, hy_pe_b2, hy_pe_wout, hy_sin_freq, hy_bias,
           swa_q_gain, swa_k_gain, swa_sink, na_q_gain, na_k_gain, na_rpb,
           w_branch, w_out):
    bsz, seq, d = x.shape
    n_ctx = ctx.shape[1]
    depth = w_ada.shape[0]

    c16 = _pad_to(jnp.concatenate([c, c_ctx[None, :]], axis=0), (MOD_ROWS, d))
    mods = _adaln_mods(c16, w_ada, b_ada).reshape(depth, MOD_ROWS, N_MOD, 1, d)

    tabs_x = _dft_tables(seq)
    tabs_c = _dft_tables(n_ctx)
    feat_x = _filter_features(seq)
    feat_c = _filter_features(n_ctx)
    deltas = _decay_rates()
    cos_x, sin_x = _rope_tables(seq)
    cos_c = jnp.ones((n_ctx, LANES), F32)
    sin_c = jnp.zeros((n_ctx, LANES), F32)
    eye = jnp.arange(MXU_DIM) // HEAD_DIM
    bd = (eye[:, None] == eye[None, :]).astype(BF16)

    wg = ffn_w_gate.astype(BF16)
    wu = ffn_w_up.astype(BF16)
    wd = ffn_w_down.astype(BF16)
    wb = w_branch.astype(BF16)
    wo = w_out.astype(BF16)
    win = w_in.astype(BF16)
    hyc = 3 * HY_WIDTH
    qw = SWA_HEADS * HEAD_DIM
    o_s = hyc
    o_n = o_s + qw + 2 * SWA_KV_HEADS * HEAD_DIM
    o_g = o_n + 3 * NA_HEADS * HEAD_DIM

    def dup(w2):
        return jnp.concatenate([w2[:, :HEAD_DIM], w2[:, :HEAD_DIM], w2[:, HEAD_DIM:], w2[:, HEAD_DIM:]], axis=1)

    xc = ctx
    for i in range(depth):
        last = i == depth - 1
        mx = lambda j: mods[i, :bsz, j]
        mc = lambda j: mods[i, bsz:bsz + 1, j]
        wi = win[i]
        ws = [wi[:, :hyc], wi[:, o_s:o_s + qw], dup(wi[:, o_s + qw:o_s + qw + 128]),
              dup(wi[:, o_s + qw + 128:o_n]), wi[:, o_n:o_n + 512], wi[:, o_n + 512:o_n + 1024],
              wi[:, o_n + 1024:o_g], wi[:, o_g:]]
        tile2 = lambda g: jnp.tile(g, 2 * MXU_DIM // LANES)
        head_gains = jnp.stack([tile2(swa_q_gain[i]), tile2(swa_k_gain[i]), tile2(na_q_gain[i]), tile2(na_k_gain[i])])
        fparams = (_pad_to(hy_pe_w0[i], (LANES, LANES)), _pad_to(hy_pe_b0[i][None], (1, LANES)),
                   _pad_to(hy_pe_w1[i], (LANES, LANES)), _pad_to(hy_pe_b1[i][None], (1, LANES)),
                   _pad_to(hy_pe_w2[i], (LANES, LANES)), _pad_to(hy_pe_b2[i][None], (1, LANES)),
                   _pad_to(hy_pe_wout[i], (LANES, 4 * HY_WIDTH)), _pad_to(hy_sin_freq[i][None], (1, LANES)))

        x = _ffn_half_step(x, mx(0), mx(1), mx(2), norm_g[i, 0], wg[i, 0], wu[i, 0], wd[i, 0])
        xc = _ffn_half_step(xc, mc(0), mc(1), mc(2), norm_g[i, 0], wg[i, 0], wu[i, 0], wd[i, 0])

        z_hy, q_s, k_s, v_s, q_n, k_n, v_n, z_gate = _in_projection(
            x, mx(3), mx(4), norm_g[i, 1], cos_x, sin_x, head_gains, bd, ws)
        zc_hy, qc_s, kc_s, vc_s, qc_n, kc_n, vc_n, zc_gate = _in_projection(
            xc, mc(3), mc(4), norm_g[i, 1], cos_c, sin_c, head_gains, bd, ws)

        asum, dsum = _hyena_filter_sums(feat_x, fparams, deltas)
        kr, ki = _filter_spectrum(asum, dsum, tabs_x)
        y_hy = _hyena_branch(z_hy, hy_short_w[i], hy_short_b[i], kr, ki, hy_bias[i], tabs_x)
        y_swa = _window_gqa(q_s, k_s, v_s, kc_s, vc_s, swa_sink[i])
        y_na = _neighbourhood_attention(q_n, k_n, v_n, kc_n, vc_n, _na_bias_table(na_rpb[i]))
        x = _merge_branches(x, mx(5), y_hy, y_swa, y_na, z_gate, wb[i], wo[i])

        if not last:
            asum_c, dsum_c = _hyena_filter_sums(feat_c, fparams, deltas)
            krc, kic = _filter_spectrum(asum_c, dsum_c, tabs_c)
            yc_hy = _hyena_branch(zc_hy, hy_short_w[i], hy_short_b[i], krc, kic, hy_bias[i], tabs_c)
            yc_swa, yc_na = _context_attention(qc_s, kc_s, vc_s, qc_n, kc_n, vc_n, swa_sink[i])
            xc = _merge_branches(xc, mc(5), yc_hy, yc_swa, yc_na, zc_gate, wb[i], wo[i])

        x = _ffn_half_step(x, mx(6), mx(7), mx(8), norm_g[i, 2], wg[i, 1], wu[i, 1], wd[i, 1])
        if not last:
            xc = _ffn_half_step(xc, mc(6), mc(7), mc(8), norm_g[i, 2], wg[i, 1], wu[i, 1], wd[i, 1])
    return x
```

```python
import functools
import math

import jax
import jax.numpy as jnp
from jax import lax
from jax.experimental import pallas as pl
from jax.experimental.pallas import tpu as pltpu

F32 = jnp.float32
BF16 = jnp.bfloat16

D_MODEL = 1024
DEPTH = 4
GRID_W = 64
HEAD_DIM = 64
N_MOD = 9
RMS_EPS = 1e-6

HY_WIDTH = D_MODEL // 2
HY_EMB = 33
HY_BANDS = (HY_EMB - 1) // 2
HY_FFN = 64
HY_FAST_DECAY = 0.3
HY_SLOW_DECAY = 1.5
HY_TARGET = 1e-2

SWA_HEADS = 8
SWA_KV_HEADS = 2
SWA_WINDOW = 128
SWA_BLOCK = 128
ROPE_BASE = 10000.0

NA_HEADS = 8
NA_MAX_ROWS = 8
NA_COLS = 16
NA_COL_BLOCK = 16

FFN_HIDDEN = 256 * ((8 * D_MODEL // 3 + 255) // 256)

LANES = 128
MXU_DIM = 256
MOD_ROWS = 16
NEG_BIG = -1e30


def _cparams(sem, vmem_mb):
    return pltpu.CompilerParams(dimension_semantics=sem, vmem_limit_bytes=vmem_mb * 1024 * 1024)


def _resident(shape):
    nd = len(shape)
    return pl.BlockSpec(shape, lambda *_: (0,) * nd, pipeline_mode=pl.Buffered(1))


def _split(x):
    hi = x.astype(BF16)
    lo = (x - hi.astype(F32)).astype(BF16)
    return hi, lo


def _dot(a, b):
    return jnp.dot(a, b, preferred_element_type=F32)


def _dot3(a_hi, a_lo, b_hi, b_lo):
    return _dot(a_hi, b_hi) + _dot(a_lo, b_hi) + _dot(a_hi, b_lo)


def _dot3f(a, b):
    a_hi, a_lo = _split(a)
    b_hi, b_lo = _split(b)
    return _dot3(a_hi, a_lo, b_hi, b_lo)


def _rms_mod(x, gain, shift, scale):
    ms = jnp.mean(x * x, axis=-1, keepdims=True)
    y = x * lax.rsqrt(ms + RMS_EPS) * gain
    return y * (1.0 + scale) + shift


def _head_rms(x, bd, gain):
    outs = []
    for j in range(x.shape[1] // MXU_DIM):
        xc = x[:, j * MXU_DIM:(j + 1) * MXU_DIM]
        hi, lo = _split(xc * xc)
        ssum = _dot(hi, bd) + _dot(lo, bd)
        outs.append(xc * lax.rsqrt(ssum * (1.0 / HEAD_DIM) + RMS_EPS) * gain)
    return outs[0] if len(outs) == 1 else jnp.concatenate(outs, axis=1)


def _rope(x, cos, sin):
    lane = lax.broadcasted_iota(jnp.int32, (x.shape[0], LANES), 1)
    first = (lane & 31) < 16
    outs = []
    for j in range(x.shape[1] // LANES):
        xc = x[:, j * LANES:(j + 1) * LANES]
        partner = jnp.where(first, pltpu.roll(xc, LANES - 16, axis=1), pltpu.roll(xc, 16, axis=1))
        outs.append(xc * cos + partner * sin)
    return outs[0] if len(outs) == 1 else jnp.concatenate(outs, axis=1)


def _mods_kernel(c_ref, w_ref, b_ref, o_ref):
    c = c_ref[...]
    a = c * jax.nn.sigmoid(c)
    o_ref[0] = _dot3f(a, w_ref[0]) + b_ref[0]


def _adaln_mods(c16, w_ada, b_ada):
    depth, d, n = w_ada.shape
    nb = 1152
    return pl.pallas_call(
        _mods_kernel, name="adaln_mods",
        grid=(depth, n // nb),
        in_specs=[
            pl.BlockSpec((MOD_ROWS, d), lambda i, j: (0, 0)),
            pl.BlockSpec((1, d, nb), lambda i, j: (i, 0, j)),
            pl.BlockSpec((1, 1, nb), lambda i, j: (i, 0, j)),
        ],
        out_specs=pl.BlockSpec((1, MOD_ROWS, nb), lambda i, j: (i, 0, j)),
        out_shape=jax.ShapeDtypeStruct((depth, MOD_ROWS, n), F32),
        compiler_params=_cparams(("arbitrary", "arbitrary"), 48),
    )(c16, w_ada, b_ada.reshape(depth, 1, n))


def _ffn_kernel(x_ref, sh_ref, sc_ref, gt_ref, g_ref, wg_ref, wu_ref, wd_ref, o_ref, *, fc):
    x = x_ref[0]
    hb = _rms_mod(x, g_ref[...], sh_ref[0], sc_ref[0]).astype(BF16)
    acc = None
    for f0 in range(0, FFN_HIDDEN, fc):
        g = _dot(hb, wg_ref[:, f0:f0 + fc])
        u = _dot(hb, wu_ref[:, f0:f0 + fc])
        a = (g * jax.nn.sigmoid(g) * u).astype(BF16)
        d = _dot(a, wd_ref[f0:f0 + fc, :])
        acc = d if acc is None else acc + d
    o_ref[0] = x + 0.5 * gt_ref[0] * acc


def _mod_spec(mod):
    if mod.shape[0] == 1:
        return pl.BlockSpec((1, 1, D_MODEL), lambda b, t: (0, 0, 0))
    return pl.BlockSpec((1, 1, D_MODEL), lambda b, t: (b, 0, 0))


def _ffn_half_step(x, shift, scale, gate, gain, wg, wu, wd):
    b, l, d = x.shape
    tm = min(512, l)
    return pl.pallas_call(
        functools.partial(_ffn_kernel, fc=FFN_HIDDEN // 2), name="ffn_half_step",
        grid=(b, l // tm),
        in_specs=[
            pl.BlockSpec((1, tm, d), lambda b, t: (b, t, 0)),
            _mod_spec(shift), _mod_spec(scale), _mod_spec(gate),
            _resident((1, d)),
            _resident(wg.shape), _resident(wu.shape), _resident(wd.shape),
        ],
        out_specs=pl.BlockSpec((1, tm, d), lambda b, t: (b, t, 0)),
        out_shape=jax.ShapeDtypeStruct(x.shape, F32),
        compiler_params=_cparams(("parallel", "parallel"), 56),
    )(x, shift, scale, gate, gain.reshape(1, d), wg, wu, wd)


def _inproj_kernel(x_ref, sh_ref, sc_ref, g_ref, cos_ref, sin_ref, hg_ref, bd_ref,
                   w_hy, w_qs, w_ks, w_vs, w_qn, w_kn, w_vn, w_gt,
                   zhy_ref, qs_ref, ks_ref, vs_ref, qn_ref, kn_ref, vn_ref, zg_ref):
    hb = _rms_mod(x_ref[0], g_ref[...], sh_ref[0], sc_ref[0]).astype(BF16)
    cos = cos_ref[...]
    sin = sin_ref[...]
    bd = bd_ref[...]
    scale = HEAD_DIM ** -0.5
    zhy_ref[0] = _dot(hb, w_hy[...])
    zg_ref[0] = _dot(hb, w_gt[...])
    qs = _rope(_head_rms(_dot(hb, w_qs[...]), bd, hg_ref[0:1, :]), cos, sin)
    qs_ref[0] = (qs * scale).astype(BF16)
    ks = _rope(_head_rms(_dot(hb, w_ks[...]), bd, hg_ref[1:2, :]), cos, sin)
    ks_ref[0] = ks.astype(BF16)
    vs_ref[0] = _dot(hb, w_vs[...]).astype(BF16)
    qn = _head_rms(_dot(hb, w_qn[...]), bd, hg_ref[2:3, :])
    qn_ref[0] = (qn * scale).astype(BF16)
    kn_ref[0] = _head_rms(_dot(hb, w_kn[...]), bd, hg_ref[3:4, :]).astype(BF16)
    vn_ref[0] = _dot(hb, w_vn[...]).astype(BF16)


def _in_projection(x, shift, scale, gain, cos, sin, head_gains, bd, ws):
    b, l, d = x.shape
    tm = min(256, l)
    widths = [w.shape[1] for w in ws]
    dtypes = [F32, BF16, BF16, BF16, BF16, BF16, BF16, F32]
    return pl.pallas_call(
        _inproj_kernel, name="in_projection",
        grid=(b, l // tm),
        in_specs=[
            pl.BlockSpec((1, tm, d), lambda b, t: (b, t, 0)),
            _mod_spec(shift), _mod_spec(scale),
            _resident((1, d)),
            pl.BlockSpec((tm, LANES), lambda b, t: (t, 0)),
            pl.BlockSpec((tm, LANES), lambda b, t: (t, 0)),
            _resident(head_gains.shape), _resident(bd.shape),
        ] + [_resident(w.shape) for w in ws],
        out_specs=[pl.BlockSpec((1, tm, n), lambda b, t: (b, t, 0)) for n in widths],
        out_shape=[jax.ShapeDtypeStruct((b, l, n), dt) for n, dt in zip(widths, dtypes)],
        compiler_params=_cparams(("parallel", "parallel"), 56),
    )(x, shift, scale, gain.reshape(1, d), cos, sin, head_gains, bd, *ws)


def _shortconv_kernel(z0_ref, z1_ref, z2_ref, w_ref, b_ref, o0_ref, o1_ref, o2_ref):
    l = z0_ref.shape[1]
    row = lax.broadcasted_iota(jnp.int32, z0_ref.shape[1:], 0)
    for g, (z_ref, o_ref) in enumerate(((z0_ref, o0_ref), (z1_ref, o1_ref), (z2_ref, o2_ref))):
        z = z_ref[0]
        prev = jnp.where(row == 0, 0.0, pltpu.roll(z, 1, axis=0))
        nxt = jnp.where(row == l - 1, 0.0, pltpu.roll(z, l - 1, axis=0))
        o_ref[0] = w_ref[g, 0:1, :] * prev + w_ref[g, 1:2, :] * z + w_ref[g, 2:3, :] * nxt + b_ref[g]


def _short_conv3(z, w, bias):
    b, l, n = z.shape
    cb = 256
    ncb = HY_WIDTH // cb
    wg = w.reshape(3, 3, HY_WIDTH).transpose(1, 0, 2)
    zin = lambda g: pl.BlockSpec((1, l, cb), lambda b, j: (b, 0, g * ncb + j))
    out = pl.BlockSpec((1, l, cb), lambda b, j: (b, 0, j))
    return pl.pallas_call(
        _shortconv_kernel, name="short_conv3",
        grid=(b, ncb),
        in_specs=[zin(0), zin(1), zin(2),
                  pl.BlockSpec((3, 3, cb), lambda b, j: (0, 0, j)),
                  pl.BlockSpec((3, 1, cb), lambda b, j: (0, 0, j))],
        out_specs=[out, out, out],
        out_shape=[jax.ShapeDtypeStruct((b, l, HY_WIDTH), F32)] * 3,
        compiler_params=_cparams(("parallel", "parallel"), 48),
    )(z, z, z, wg, bias.reshape(3, 1, HY_WIDTH))


def _filter_kernel(z_ref, w0, b0, w1, b1, w2, b2, wo, fr_ref, dl_ref, o_ref):
    z = z_ref[...]
    fr = fr_ref[...]
    a = jnp.sin(fr * (_dot3f(z, w0[...]) + b0[...]))
    a = jnp.sin(fr * (_dot3f(a, w1[...]) + b1[...]))
    a = jnp.sin(fr * (_dot3f(a, w2[...]) + b2[...]))
    hh = _dot3f(a, wo[...])
    t = z[:, 0:1]
    win = jnp.exp(-t * dl_ref[...])
    row = lax.broadcasted_iota(jnp.int32, win.shape, 0) + pl.program_id(0) * z.shape[0]
    w = HY_WIDTH
    for o in range(2):
        hp = hh[:, (2 * o) * w:(2 * o + 1) * w] * win
        hn = jnp.where(row == 0, 0.0, hh[:, (2 * o + 1) * w:(2 * o + 2) * w] * win)
        o_ref[0, :, o * w:(o + 1) * w] = hp + hn
        o_ref[1, :, o * w:(o + 1) * w] = hp - hn


def _hyena_filter_sums(zfeat, fp, deltas):
    l = zfeat.shape[0]
    tl = min(256, l)
    w0, b0, w1, b1, w2, b2, wo, fr = fp
    n = 2 * HY_WIDTH
    consts = [w0, b0, w1, b1, w2, b2, wo, fr, deltas]
    return pl.pallas_call(
        _filter_kernel, name="hyena_filter",
        grid=(l // tl,),
        in_specs=[pl.BlockSpec((tl, LANES), lambda t: (t, 0))] + [_resident(c.shape) for c in consts],
        out_specs=pl.BlockSpec((2, tl, n), lambda t: (0, t, 0)),
        out_shape=jax.ShapeDtypeStruct((2, l, n), F32),
        compiler_params=_cparams(("parallel",), 32),
    )(zfeat, *consts)


def _spectrum_kernel(a_ref, d_ref, ch, cl, sh, sl, kr_ref, ki_ref):
    a_hi, a_lo = _split(a_ref[0])
    d_hi, d_lo = _split(d_ref[0])
    kr_ref[...] = _dot3(ch[...], cl[...], a_hi, a_lo)
    ki_ref[...] = -_dot3(sh[...], sl[...], d_hi, d_lo)


def _filter_spectrum(ad, tabs):
    _, l, n = ad.shape
    fp = tabs["c_hi"].shape[0]
    fb = tabs["fb"]
    cb = 256
    tab = pl.BlockSpec((fb, l), lambda j, k: (k, 0))
    out = pl.BlockSpec((fb, cb), lambda j, k: (k, j))
    return pl.pallas_call(
        _spectrum_kernel, name="dense_spectrum",
        grid=(n // cb, fp // fb),
        in_specs=[pl.BlockSpec((1, l, cb), lambda j, k: (0, 0, j)),
                  pl.BlockSpec((1, l, cb), lambda j, k: (1, 0, j)), tab, tab, tab, tab],
        out_specs=[out, out],
        out_shape=[jax.ShapeDtypeStruct((fp, n), F32)] * 2,
        compiler_params=_cparams(("parallel", "parallel"), 48),
    )(ad, ad, tabs["c_hi"], tabs["c_lo"], tabs["s_hi"], tabs["s_lo"])


def _longconv_kernel(u_ref, g_ref, kr_ref, ki_ref, bias_ref, wk_ref, ch, cl, sh, sl, cth, ctl, sth, stl,
                     o_ref, uh_ref, ul_ref, acc_ref):
    kb = pl.program_id(2)

    @pl.when(kb == 0)
    def _():
        hi, lo = _split(u_ref[0])
        uh_ref[...] = hi
        ul_ref[...] = lo
        acc_ref[...] = jnp.zeros_like(acc_ref)

    uh = uh_ref[...]
    ul = ul_ref[...]
    xr = _dot3(ch[...], cl[...], uh, ul)
    xi = -_dot3(sh[...], sl[...], uh, ul)
    kr = kr_ref[...]
    ki = ki_ref[...]
    wk = wk_ref[...]
    yr_hi, yr_lo = _split((xr * kr - xi * ki) * wk)
    yi_hi, yi_lo = _split((xr * ki + xi * kr) * wk)
    acc_ref[...] += _dot3(cth[...], ctl[...], yr_hi, yr_lo) - _dot3(sth[...], stl[...], yi_hi, yi_lo)

    @pl.when(kb == pl.num_programs(2) - 1)
    def _():
        u = u_ref[0]
        o_ref[0] = g_ref[0] * (acc_ref[...] + u * bias_ref[...])


def _gated_long_conv(u_arr, g_arr, kr, ki, k_col, bias, tabs):
    b, l, _ = u_arr.shape
    cb = 256
    ncb = HY_WIDTH // cb
    fp = tabs["c_hi"].shape[0]
    fb = tabs["fb"]
    tab = pl.BlockSpec((fb, l), lambda b, j, k: (k, 0))
    tabt = pl.BlockSpec((l, fb), lambda b, j, k: (0, k))
    spec = pl.BlockSpec((fb, cb), lambda b, j, k: (k, k_col * ncb + j))
    return pl.pallas_call(
        _longconv_kernel, name="dense_long_conv",
        grid=(b, ncb, fp // fb),
        in_specs=[
            pl.BlockSpec((1, l, cb), lambda b, j, k: (b, 0, j)),
            pl.BlockSpec((1, l, cb), lambda b, j, k: (b, 0, j)),
            spec, spec,
            pl.BlockSpec((1, cb), lambda b, j, k: (0, j)),
            pl.BlockSpec((fb, 1), lambda b, j, k: (k, 0)),
            tab, tab, tab, tab, tabt, tabt, tabt, tabt,
        ],
        out_specs=pl.BlockSpec((1, l, cb), lambda b, j, k: (b, 0, j)),
        out_shape=jax.ShapeDtypeStruct((b, l, HY_WIDTH), F32),
        scratch_shapes=[pltpu.VMEM((l, cb), BF16), pltpu.VMEM((l, cb), BF16), pltpu.VMEM((l, cb), F32)],
        compiler_params=_cparams(("parallel", "parallel", "arbitrary"), 56),
    )(u_arr, g_arr, kr, ki, bias.reshape(1, HY_WIDTH), tabs["wk"],
      tabs["c_hi"], tabs["c_lo"], tabs["s_hi"], tabs["s_lo"],
      tabs["ct_hi"], tabs["ct_lo"], tabs["st_hi"], tabs["st_lo"])


FFT_INNER = 64
FFT_LANE_CHUNK = 4096
FFT_K2_CHUNK = 8


def _fft_tables(l):
    n = 2 * l
    n1 = FFT_INNER
    n2 = n // n1
    h = n2 // 2
    ang = lambda idx, mod: (idx % mod).astype(F32) * (2.0 * math.pi / mod)
    k2 = jnp.arange(n2, dtype=jnp.int32)
    m2 = jnp.arange(h, dtype=jnp.int32)
    a1 = ang(k2[:, None] * m2[None, :], n2)
    c1, s1 = jnp.cos(a1), jnp.sin(a1)
    f1 = jnp.block([[c1, s1], [-s1, c1]])
    k1 = jnp.arange(n1, dtype=jnp.int32)
    a2 = ang(k1[None, None, :] * (k1[None, :, None] * n2 + k2[:, None, None]), n)
    c2, s2 = jnp.cos(a2), jnp.sin(a2)
    m = jnp.concatenate([jnp.concatenate([c2, s2], axis=2), jnp.concatenate([-s2, c2], axis=2)], axis=1)
    g = jnp.block([[c1.T, -s1.T], [s1.T, c1.T]]) * (1.0 / n)
    out = dict(n2=n2)
    for name, t in (("f1", f1), ("f1r", f1[:, :h]), ("m", m), ("mt", jnp.swapaxes(m, 1, 2)), ("g", g)):
        out[name + "_hi"], out[name + "_lo"] = _split(t)
    return out


def _fft1_kernel(u_ref, fh, fl, o_ref, *, paired):
    z = jnp.concatenate([u_ref[0, 0], u_ref[1, 0]], axis=0) if paired else u_ref[0]
    zh, zl = _split(z)
    o_ref[0] = _dot3(fh[...], fl[...], zh, zl)


def _fft_stage1(u, ft, paired):
    n2 = ft["n2"]
    ch = FFT_LANE_CHUNK
    if paired:
        _, r, h, ln = u.shape
        in_spec = pl.BlockSpec((2, 1, h, ch), lambda p, c: (0, p, 0, c))
        fh, fl = ft["f1_hi"], ft["f1_lo"]
    else:
        r, h, ln = u.shape
        in_spec = pl.BlockSpec((1, h, ch), lambda p, c: (p, 0, c))
        fh, fl = ft["f1r_hi"], ft["f1r_lo"]
    return pl.pallas_call(
        functools.partial(_fft1_kernel, paired=paired), name="fft_stage1",
        grid=(r, ln // ch),
        in_specs=[in_spec, _resident(fh.shape), _resident(fl.shape)],
        out_specs=pl.BlockSpec((1, 2 * n2, ch), lambda p, c: (p, 0, c)),
        out_shape=jax.ShapeDtypeStruct((r, 2 * n2, ln), F32),
        compiler_params=_cparams(("parallel", "parallel"), 32),
    )(u, fh, fl)


def _fftspec_kernel(ar_ref, ai_ref, dr_ref, di_ref, mh, ml, kr_ref, ki_ref):
    n1 = FFT_INNER
    for j in range(ar_ref.shape[1]):
        ah, al = _split(jnp.concatenate([ar_ref[0, j], ai_ref[0, j]], axis=0))
        kr_ref[j] = _dot3(mh[j][:n1], ml[j][:n1], ah, al)
        dh, dl = _split(jnp.concatenate([dr_ref[0, j], di_ref[0, j]], axis=0))
        ki_ref[j] = _dot3(mh[j][n1:], ml[j][n1:], dh, dl)


def _fft_filter_spectrum(ad, ft):
    _, l, n = ad.shape
    n1, n2, ks = FFT_INNER, ft["n2"], FFT_K2_CHUNK
    h = n2 // 2
    a = _fft_stage1(ad.reshape(2, h, n1 * n), ft, paired=False).reshape(2, 2 * n2, n1, n)
    cc = HY_WIDTH
    nk = n2 // ks
    blk = lambda r, im: pl.BlockSpec((1, ks, n1, cc), lambda kc, j: (r, im * nk + kc, 0, j))
    tab = pl.BlockSpec((ks, 2 * n1, 2 * n1), lambda kc, j: (kc, 0, 0))
    out = pl.BlockSpec((ks, n1, cc), lambda kc, j: (kc, 0, j))
    return pl.pallas_call(
        _fftspec_kernel, name="fft_filter_spectrum",
        grid=(nk, n // cc),
        in_specs=[blk(0, 0), blk(0, 1), blk(1, 0), blk(1, 1), tab, tab],
        out_specs=[out, out],
        out_shape=[jax.ShapeDtypeStruct((n2, n1, n), F32)] * 2,
        compiler_params=_cparams(("parallel", "parallel"), 32),
    )(a, a, a, a, ft["m_hi"], ft["m_lo"])


def _fftmid_kernel(ar_ref, ai_ref, kr_ref, ki_ref, mh, ml, vh, vl, br_ref, bi_ref):
    n1 = FFT_INNER
    for j in range(ar_ref.shape[1]):
        ah, al = _split(jnp.concatenate([ar_ref[0, j], ai_ref[0, j]], axis=0))
        x = _dot3(mh[j], ml[j], ah, al)
        xr, xi = x[:n1], x[n1:]
        kr, ki = kr_ref[j], ki_ref[j]
        yh, yl = _split(jnp.concatenate([xr * kr - xi * ki, xr * ki + xi * kr], axis=0))
        b = _dot3(vh[j], vl[j], yh, yl)
        br_ref[0, j] = b[:n1]
        bi_ref[0, j] = b[n1:]


def _fft_mid(a, kr, ki, k_col, ft):
    p, _, n1, c = a.shape
    n2, ks = ft["n2"], FFT_K2_CHUNK
    nk = n2 // ks
    slab = lambda im: pl.BlockSpec((1, ks, n1, c), lambda kc, q: (q, im * nk + kc, 0, 0))
    filt = pl.BlockSpec((ks, n1, c), lambda kc, q: (kc, 0, k_col))
    tab = pl.BlockSpec((ks, 2 * n1, 2 * n1), lambda kc, q: (kc, 0, 0))
    out = pl.BlockSpec((1, ks, n1, c), lambda kc, q: (q, kc, 0, 0))
    return pl.pallas_call(
        _fftmid_kernel, name="fft_mid",
        grid=(nk, p),
        in_specs=[slab(0), slab(1), filt, filt, tab, tab, tab, tab],
        out_specs=[out, out],
        out_shape=[jax.ShapeDtypeStruct((p, n2, n1, c), F32)] * 2,
        compiler_params=_cparams(("parallel", "arbitrary"), 40),
    )(a, a, kr, ki, ft["m_hi"], ft["m_lo"], ft["mt_hi"], ft["mt_lo"])


def _ifft1_kernel(br_ref, bi_ref, u_ref, g_ref, bias_ref, gh, gl, o_ref):
    bh, bl = _split(jnp.concatenate([br_ref[0], bi_ref[0]], axis=0))
    y = _dot3(gh[...], gl[...], bh, bl)
    h = y.shape[0] // 2
    for s in range(2):
        o_ref[s, 0] = g_ref[s, 0] * (y[s * h:(s + 1) * h] + u_ref[s, 0] * bias_ref[...])


def _fft_stage1_inv(br, bi, u, gate, bias, ft):
    _, p, h, ln = u.shape
    n2 = ft["n2"]
    ch = FFT_LANE_CHUNK
    spec_b = pl.BlockSpec((1, n2, ch), lambda q, c: (q, 0, c))
    spec_u = pl.BlockSpec((2, 1, h, ch), lambda q, c: (0, q, 0, c))
    bias_t = jnp.tile(bias.reshape(1, -1), (1, ch // bias.shape[-1]))
    return pl.pallas_call(
        _ifft1_kernel, name="fft_stage1_inv",
        grid=(p, ln // ch),
        in_specs=[spec_b, spec_b, spec_u, spec_u, _resident(bias_t.shape),
                  _resident(ft["g_hi"].shape), _resident(ft["g_lo"].shape)],
        out_specs=spec_u,
        out_shape=jax.ShapeDtypeStruct(u.shape, F32),
        compiler_params=_cparams(("parallel", "parallel"), 32),
    )(br, bi, u, gate, bias_t, ft["g_hi"], ft["g_lo"])


def _fft_gated_long_conv(u, gate, kr, ki, k_col, bias, ft):
    b, l, c = u.shape
    n1, n2 = FFT_INNER, ft["n2"]
    p, h, ln = b // 2, n2 // 2, n1 * c
    uv = u.reshape(2, p, h, ln)
    a = _fft_stage1(uv, ft, paired=True)
    br, bi = _fft_mid(a.reshape(p, 2 * n2, n1, c), kr, ki, k_col, ft)
    y = _fft_stage1_inv(br.reshape(p, n2, ln), bi.reshape(p, n2, ln), uv, gate.reshape(2, p, h, ln), bias, ft)
    return y.reshape(b, l, c)


def _softmax_pv(s, v, sink=None):
    m = jnp.max(s, axis=-1, keepdims=True)
    if sink is not None:
        m = jnp.maximum(m, sink)
    p = jnp.exp(s - m)
    den = jnp.sum(p, axis=-1, keepdims=True)
    if sink is not None:
        den = den + jnp.exp(sink - m)
    return _dot(p.astype(BF16), v) / den


def _qk(q, k):
    return lax.dot_general(q, k, (((1,), (1,)), ((), ())), preferred_element_type=F32)


def _half_mask(q_tile, half):
    lane = lax.broadcasted_iota(jnp.int32, q_tile.shape, 1)
    keep = (lane < HEAD_DIM) if half == 0 else (lane >= HEAD_DIM)
    return jnp.where(keep, q_tile, jnp.zeros_like(q_tile))


def _merge_halves(o_even, o_odd):
    lane = lax.broadcasted_iota(jnp.int32, o_even.shape, 1)
    return jnp.where(lane < HEAD_DIM, o_even, o_odd)


def _swa_kernel(sink_ref, q_ref, kp_ref, kc_ref, kn_ref, vp_ref, vc_ref, vn_ref, kx_ref, vx_ref, o_ref):
    n = pl.program_id(1)
    nb = pl.num_programs(1)
    blk = SWA_BLOCK
    g = SWA_HEADS // SWA_KV_HEADS
    n_ctx = kx_ref.shape[1]
    rows = g * blk
    ri = lax.broadcasted_iota(jnp.int32, (rows, 3 * blk + n_ctx), 0) & (blk - 1)
    ci = lax.broadcasted_iota(jnp.int32, (rows, 3 * blk + n_ctx), 1)
    ok_prev = (ci < blk) & (ci >= ri) & (n > 0)
    ok_cur = (ci >= blk) & (ci < 2 * blk)
    ok_next = (ci >= 2 * blk) & (ci < 3 * blk) & (ci - 2 * blk <= ri) & (n < nb - 1)
    ok = ok_prev | ok_cur | ok_next | (ci >= 3 * blk)
    bias = jnp.where(ok, 0.0, NEG_BIG)
    rsel = lax.broadcasted_iota(jnp.int32, (rows, 1), 0) >> (blk.bit_length() - 1)
    outs = [None] * SWA_HEADS
    for kv in range(SWA_KV_HEADS):
        cs = slice(kv * LANES, (kv + 1) * LANES)
        k = jnp.concatenate([kp_ref[0, :, cs], kc_ref[0, :, cs], kn_ref[0, :, cs], kx_ref[0, :, cs]], axis=0)
        v = jnp.concatenate([vp_ref[0, :, cs], vc_ref[0, :, cs], vn_ref[0, :, cs], vx_ref[0, :, cs]], axis=0)
        q_parts = []
        sink = jnp.zeros((rows, 1), F32)
        for gi in range(g):
            h = kv * g + gi
            q_parts.append(_half_mask(q_ref[0, :, (h // 2) * LANES:(h // 2 + 1) * LANES], h % 2))
            sink = jnp.where(rsel == gi, sink_ref[h], sink)
        s = _qk(jnp.concatenate(q_parts, axis=0), k) + bias
        o = _softmax_pv(s, v, sink)
        for gi in range(g):
            outs[kv * g + gi] = o[gi * blk:(gi + 1) * blk]
    for p in range(SWA_HEADS // 2):
        o_ref[0, :, p * LANES:(p + 1) * LANES] = _merge_halves(outs[2 * p], outs[2 * p + 1]).astype(o_ref.dtype)


def _window_gqa(q, k, v, kx, vx, sink):
    b, l, _ = q.shape
    nb = l // SWA_BLOCK
    n_ctx = kx.shape[1]
    kvw = k.shape[2]

    def blk(off):
        return pl.BlockSpec((1, SWA_BLOCK, kvw), lambda b, n: (b, jnp.clip(n + off, 0, nb - 1), 0))

    ctx = pl.BlockSpec((1, n_ctx, kvw), lambda b, n: (b, 0, 0))
    return pl.pallas_call(
        _swa_kernel, name="window_gqa",
        grid=(b, nb),
        in_specs=[
            pl.BlockSpec(memory_space=pltpu.SMEM),
            pl.BlockSpec((1, SWA_BLOCK, q.shape[2]), lambda b, n: (b, n, 0)),
            blk(-1), blk(0), blk(1), blk(-1), blk(0), blk(1), ctx, ctx,
        ],
        out_specs=pl.BlockSpec((1, SWA_BLOCK, q.shape[2]), lambda b, n: (b, n, 0)),
        out_shape=jax.ShapeDtypeStruct(q.shape, BF16),
        compiler_params=_cparams(("parallel", "parallel"), 32),
    )(sink, q, k, k, k, v, v, v, kx, vx)


def _na_bias_kernel(rpb_ref, o_ref):
    off = pl.program_id(0)
    h = pl.program_id(1)
    nd = 2 * NA_COLS - 1
    q = lax.broadcasted_iota(jnp.int32, (GRID_W, LANES), 0)
    lane = lax.broadcasted_iota(jnp.int32, (GRID_W, LANES), 1)
    kc = lane & (GRID_W - 1)
    upper = lane >= GRID_W
    cstart = jnp.clip(q - NA_COLS // 2, 0, GRID_W - NA_COLS)
    valid = (kc >= cstart) & (kc < cstart + NA_COLS)
    dc = jnp.clip(kc - q, -(NA_COLS - 1), NA_COLS - 1) + (NA_COLS - 1)
    for j in range(NA_MAX_ROWS // 2):
        base0 = (h * (2 * NA_MAX_ROWS - 1) + (2 * j - off + NA_MAX_ROWS - 1)) * nd
        base1 = base0 + nd
        t = jnp.zeros((GRID_W, LANES), F32)
        for d in range(nd):
            val = jnp.where(upper, rpb_ref[base1 + d], rpb_ref[base0 + d])
            t = jnp.where(dc == d, val, t)
        o_ref[0, 0, :, j * LANES:(j + 1) * LANES] = jnp.where(valid, t, NEG_BIG)


def _na_bias_table(rpb):
    return pl.pallas_call(
        _na_bias_kernel, name="na_bias_table",
        grid=(NA_MAX_ROWS, NA_HEADS),
        in_specs=[pl.BlockSpec(memory_space=pltpu.SMEM)],
        out_specs=pl.BlockSpec((1, 1, GRID_W, NA_MAX_ROWS * GRID_W), lambda o, h: (o, h, 0, 0)),
        out_shape=jax.ShapeDtypeStruct((NA_MAX_ROWS, NA_HEADS, GRID_W, NA_MAX_ROWS * GRID_W), F32),
        compiler_params=_cparams(("arbitrary", "arbitrary"), 16),
    )(rpb.reshape(-1))


def _na_kernel(q_ref, k_ref, v_ref, kx_ref, vx_ref, bias_ref, o_ref, *, rows):
    r = pl.program_id(1)
    wr = NA_MAX_ROWS
    start = pl.multiple_of(jnp.clip(r - wr // 2, 0, rows - wr) * GRID_W, GRID_W)
    nloc = wr * GRID_W
    for p in range(NA_HEADS // 2):
        cs = slice(p * LANES, (p + 1) * LANES)
        q_tile = q_ref[0, :, cs]
        q2 = jnp.concatenate([_half_mask(q_tile, 0), _half_mask(q_tile, 1)], axis=0)
        k = jnp.concatenate([k_ref[0, pl.ds(start, nloc), cs], kx_ref[0, :, cs]], axis=0)
        v = jnp.concatenate([v_ref[0, pl.ds(start, nloc), cs], vx_ref[0, :, cs]], axis=0)
        s = _qk(q2, k)
        bias = jnp.concatenate([bias_ref[0, 2 * p], bias_ref[0, 2 * p + 1]], axis=0)
        s = jnp.concatenate([s[:, :nloc] + bias, s[:, nloc:]], axis=1)
        o = _softmax_pv(s, v)
        o_ref[0, :, cs] = _merge_halves(o[:GRID_W], o[GRID_W:]).astype(o_ref.dtype)


def _neighbourhood_attention(q, k, v, kx, vx, bias_tab):
    b, l, w = q.shape
    rows = l // GRID_W
    n_ctx = kx.shape[1]

    def off(r):
        return r - jnp.clip(r - NA_MAX_ROWS // 2, 0, rows - NA_MAX_ROWS)

    full = pl.BlockSpec((1, l, w), lambda b, r: (b, 0, 0))
    ctx = pl.BlockSpec((1, n_ctx, w), lambda b, r: (b, 0, 0))
    return pl.pallas_call(
        functools.partial(_na_kernel, rows=rows), name="neighbourhood_attention",
        grid=(b, rows),
        in_specs=[
            pl.BlockSpec((1, GRID_W, w), lambda b, r: (b, r, 0)),
            full, full, ctx, ctx,
            pl.BlockSpec((1, NA_HEADS, GRID_W, NA_MAX_ROWS * GRID_W), lambda b, r: (off(r), 0, 0, 0)),
        ],
        out_specs=pl.BlockSpec((1, GRID_W, w), lambda b, r: (b, r, 0)),
        out_shape=jax.ShapeDtypeStruct(q.shape, BF16),
        compiler_params=_cparams(("parallel", "arbitrary"), 40),
    )(q, k, v, kx, vx, bias_tab)


def _ctx_attn_kernel(sink_ref, qs_ref, ks_ref, vs_ref, qn_ref, kn_ref, vn_ref, os_ref, on_ref):
    n = qs_ref.shape[1]
    g = SWA_HEADS // SWA_KV_HEADS
    rsel = lax.broadcasted_iota(jnp.int32, (g * n, 1), 0) >> (n.bit_length() - 1)
    outs = [None] * SWA_HEADS
    for kv in range(SWA_KV_HEADS):
        cs = slice(kv * LANES, (kv + 1) * LANES)
        q_parts = []
        sink = jnp.zeros((g * n, 1), F32)
        for gi in range(g):
            h = kv * g + gi
            q_parts.append(_half_mask(qs_ref[0, :, (h // 2) * LANES:(h // 2 + 1) * LANES], h % 2))
            sink = jnp.where(rsel == gi, sink_ref[h], sink)
        o = _softmax_pv(_qk(jnp.concatenate(q_parts, axis=0), ks_ref[0, :, cs]), vs_ref[0, :, cs], sink)
        for gi in range(g):
            outs[kv * g + gi] = o[gi * n:(gi + 1) * n]
    for p in range(SWA_HEADS // 2):
        os_ref[0, :, p * LANES:(p + 1) * LANES] = _merge_halves(outs[2 * p], outs[2 * p + 1]).astype(os_ref.dtype)
    for p in range(NA_HEADS // 2):
        cs = slice(p * LANES, (p + 1) * LANES)
        q_tile = qn_ref[0, :, cs]
        q2 = jnp.concatenate([_half_mask(q_tile, 0), _half_mask(q_tile, 1)], axis=0)
        o = _softmax_pv(_qk(q2, kn_ref[0, :, cs]), vn_ref[0, :, cs])
        on_ref[0, :, cs] = _merge_halves(o[:n], o[n:]).astype(on_ref.dtype)


def _context_attention(qs, ks, vs, qn, kn, vn, sink):
    b, n, w = qs.shape
    spec = lambda a: pl.BlockSpec((1,) + a.shape[1:], lambda b: (b, 0, 0))
    return pl.pallas_call(
        _ctx_attn_kernel, name="context_attention",
        grid=(b,),
        in_specs=[pl.BlockSpec(memory_space=pltpu.SMEM)] + [spec(a) for a in (qs, ks, vs, qn, kn, vn)],
        out_specs=[spec(qs), spec(qn)],
        out_shape=[jax.ShapeDtypeStruct(qs.shape, BF16), jax.ShapeDtypeStruct(qn.shape, BF16)],
        compiler_params=_cparams(("parallel",), 32),
    )(sink, qs, ks, vs, qn, kn, vn)


def _merge_kernel(x_ref, gt_ref, yh_ref, ys_ref, yn_ref, zg_ref, wb_ref, wo_ref, o_ref):
    d = D_MODEL
    ys = (yh_ref[0].astype(BF16), ys_ref[0], yn_ref[0])
    m = None
    for n in range(3):
        proj = _dot(ys[n], wb_ref[n])
        term = jax.nn.sigmoid(zg_ref[0, :, n * d:(n + 1) * d]) * proj
        m = term if m is None else m + term
    o_ref[0] = x_ref[0] + gt_ref[0] * _dot(m.astype(BF16), wo_ref[...])


def _merge_branches(x, gate, y_hy, y_swa, y_na, z_gate, w_branch, w_out):
    b, l, d = x.shape
    tm = min(512, l)
    tok = lambda n: pl.BlockSpec((1, tm, n), lambda b, t: (b, t, 0))
    return pl.pallas_call(
        _merge_kernel, name="merge_branches",
        grid=(b, l // tm),
        in_specs=[tok(d), _mod_spec(gate), tok(HY_WIDTH), tok(HY_WIDTH), tok(HY_WIDTH), tok(3 * d),
                  _resident(w_branch.shape), _resident(w_out.shape)],
        out_specs=tok(d),
        out_shape=jax.ShapeDtypeStruct(x.shape, F32),
        compiler_params=_cparams(("parallel", "parallel"), 48),
    )(x, gate, y_hy, y_swa, y_na, z_gate, w_branch, w_out)


def _dft_tables(l):
    n = 2 * l
    fb = 256 if l >= 1024 else LANES * ((l + 1 + LANES - 1) // LANES)
    fp = fb * ((l + 1 + fb - 1) // fb)
    k = jnp.arange(fp, dtype=jnp.int32)[:, None]
    t = jnp.arange(l, dtype=jnp.int32)[None, :]
    ang = ((k * t) % n).astype(F32) * (2.0 * math.pi / n)
    live = k <= l
    c = jnp.where(live, jnp.cos(ang), 0.0)
    s = jnp.where(live, jnp.sin(ang), 0.0)
    c_hi, c_lo = _split(c)
    s_hi, s_lo = _split(s)
    wk = jnp.where((k == 0) | (k == l), 1.0, 2.0) * jnp.where(live, 1.0 / n, 0.0)
    return dict(fb=fb, c_hi=c_hi, c_lo=c_lo, s_hi=s_hi, s_lo=s_lo,
                ct_hi=c_hi.T, ct_lo=c_lo.T, st_hi=s_hi.T, st_lo=s_lo.T, wk=wk.astype(F32))


def _filter_features(l):
    t = jnp.linspace(0.0, 1.0, l, dtype=F32)[:, None]
    w = (2.0 * math.pi / l) * jnp.arange(l, dtype=F32)[:, None]
    f = jnp.linspace(1e-4, HY_BANDS - 1, HY_BANDS, dtype=F32)[None, :]
    z = jnp.concatenate([t, jnp.cos(f * w), -jnp.sin(f * w)], axis=-1)
    return jnp.pad(z, ((0, 0), (0, LANES - HY_EMB)))


def _decay_rates():
    max_decay = math.log(HY_TARGET) / HY_FAST_DECAY
    min_decay = math.log(HY_TARGET) / HY_SLOW_DECAY
    return jnp.abs(jnp.linspace(min_decay, max_decay, HY_WIDTH, dtype=F32))[None, :]


def _rope_tables(l):
    pos = jnp.arange(l)
    row = (pos // GRID_W).astype(F32)
    col = (pos % GRID_W).astype(F32)
    half = HEAD_DIM // 2
    inv = 1.0 / (ROPE_BASE ** (jnp.arange(0, half, 2, dtype=F32) / half))
    ar = row[:, None] * inv[None, :]
    ac = col[:, None] * inv[None, :]
    cos = jnp.concatenate([jnp.cos(ar), jnp.cos(ar), jnp.cos(ac), jnp.cos(ac)], axis=-1)
    sin = jnp.concatenate([-jnp.sin(ar), jnp.sin(ar), -jnp.sin(ac), jnp.sin(ac)], axis=-1)
    return jnp.tile(cos, (1, 2)), jnp.tile(sin, (1, 2))


def _pad_to(a, shape):
    return jnp.pad(a, [(0, s - d) for d, s in zip(a.shape, shape)])


def _use_fft(bsz, length):
    return bsz % 2 == 0 and length % (8 * FFT_INNER) == 0


def _conv_tables(bsz, length):
    return _fft_tables(length) if _use_fft(bsz, length) else _dft_tables(length)


def _hyena_branch(z_hy, short_w, short_b, feat, fparams, deltas, bias, tabs):
    v, x1, x2 = _short_conv3(z_hy, short_w, short_b)
    ad = _hyena_filter_sums(feat, fparams, deltas)
    if "n2" in tabs:
        kr, ki = _fft_filter_spectrum(ad, tabs)
        conv = functools.partial(_fft_gated_long_conv, ft=tabs)
    else:
        kr, ki = _filter_spectrum(ad, tabs)
        conv = functools.partial(_gated_long_conv, tabs=tabs)
    y1 = conv(v, x1, kr, ki, 0, bias[0])
    return conv(y1, x2, kr, ki, 1, bias[1])


def kernel(x, c, ctx, c_ctx, w_ada, b_ada, norm_g, ffn_w_gate, ffn_w_up, ffn_w_down,
           w_in, hy_short_w, hy_short_b, hy_pe_w0, hy_pe_b0, hy_pe_w1, hy_pe_b1,
           hy_pe_w2, hy_pe_b2, hy_pe_wout, hy_sin_freq, hy_bias,
           swa_q_gain, swa_k_gain, swa_sink, na_q_gain, na_k_gain, na_rpb,
           w_branch, w_out):
    bsz, seq, d = x.shape
    n_ctx = ctx.shape[1]
    depth = w_ada.shape[0]

    c16 = _pad_to(jnp.concatenate([c, c_ctx[None, :]], axis=0), (MOD_ROWS, d))
    mods = _adaln_mods(c16, w_ada, b_ada).reshape(depth, MOD_ROWS, N_MOD, 1, d)

    tabs_x = _conv_tables(bsz, seq)
    tabs_c = _conv_tables(bsz, n_ctx)
    feat_x = _filter_features(seq)
    feat_c = _filter_features(n_ctx)
    deltas = _decay_rates()
    cos_x, sin_x = _rope_tables(seq)
    cos_c = jnp.ones((n_ctx, LANES), F32)
    sin_c = jnp.zeros((n_ctx, LANES), F32)
    eye = jnp.arange(MXU_DIM) // HEAD_DIM
    bd = (eye[:, None] == eye[None, :]).astype(BF16)

    wg = ffn_w_gate.astype(BF16)
    wu = ffn_w_up.astype(BF16)
    wd = ffn_w_down.astype(BF16)
    wb = w_branch.astype(BF16)
    wo = w_out.astype(BF16)
    win = w_in.astype(BF16)
    hyc = 3 * HY_WIDTH
    qw = SWA_HEADS * HEAD_DIM
    o_s = hyc
    o_n = o_s + qw + 2 * SWA_KV_HEADS * HEAD_DIM
    o_g = o_n + 3 * NA_HEADS * HEAD_DIM

    def dup(w2):
        return jnp.concatenate([w2[:, :HEAD_DIM], w2[:, :HEAD_DIM], w2[:, HEAD_DIM:], w2[:, HEAD_DIM:]], axis=1)

    xc = ctx
    for i in range(depth):
        last = i == depth - 1
        mx = lambda j: mods[i, :bsz, j]
        mc = lambda j: mods[i, bsz:bsz + 1, j]
        wi = win[i]
        ws = [wi[:, :hyc], wi[:, o_s:o_s + qw], dup(wi[:, o_s + qw:o_s + qw + 128]),
              dup(wi[:, o_s + qw + 128:o_n]), wi[:, o_n:o_n + 512], wi[:, o_n + 512:o_n + 1024],
              wi[:, o_n + 1024:o_g], wi[:, o_g:]]
        tile2 = lambda g: jnp.tile(g, 2 * MXU_DIM // LANES)
        head_gains = jnp.stack([tile2(swa_q_gain[i]), tile2(swa_k_gain[i]), tile2(na_q_gain[i]), tile2(na_k_gain[i])])
        fparams = (_pad_to(hy_pe_w0[i], (LANES, LANES)), _pad_to(hy_pe_b0[i][None], (1, LANES)),
                   _pad_to(hy_pe_w1[i], (LANES, LANES)), _pad_to(hy_pe_b1[i][None], (1, LANES)),
                   _pad_to(hy_pe_w2[i], (LANES, LANES)), _pad_to(hy_pe_b2[i][None], (1, LANES)),
                   _pad_to(hy_pe_wout[i], (LANES, 4 * HY_WIDTH)), _pad_to(hy_sin_freq[i][None], (1, LANES)))

        x = _ffn_half_step(x, mx(0), mx(1), mx(2), norm_g[i, 0], wg[i, 0], wu[i, 0], wd[i, 0])
        xc = _ffn_half_step(xc, mc(0), mc(1), mc(2), norm_g[i, 0], wg[i, 0], wu[i, 0], wd[i, 0])

        z_hy, q_s, k_s, v_s, q_n, k_n, v_n, z_gate = _in_projection(
            x, mx(3), mx(4), norm_g[i, 1], cos_x, sin_x, head_gains, bd, ws)
        zc_hy, qc_s, kc_s, vc_s, qc_n, kc_n, vc_n, zc_gate = _in_projection(
            xc, mc(3), mc(4), norm_g[i, 1], cos_c, sin_c, head_gains, bd, ws)

        y_hy = _hyena_branch(z_hy, hy_short_w[i], hy_short_b[i], feat_x, fparams, deltas, hy_bias[i], tabs_x)
        y_swa = _window_gqa(q_s, k_s, v_s, kc_s, vc_s, swa_sink[i])
        y_na = _neighbourhood_attention(q_n, k_n, v_n, kc_n, vc_n, _na_bias_table(na_rpb[i]))
        x = _merge_branches(x, mx(5), y_hy, y_swa, y_na, z_gate, wb[i], wo[i])

        if not last:
            yc_hy = _hyena_branch(zc_hy, hy_short_w[i], hy_short_b[i], feat_c, fparams, deltas, hy_bias[i], tabs_c)
            yc_swa, yc_na = _context_attention(qc_s, kc_s, vc_s, qc_n, kc_n, vc_n, swa_sink[i])
            xc = _merge_branches(xc, mc(5), yc_hy, yc_swa, yc_na, zc_gate, wb[i], wo[i])

        x = _ffn_half_step(x, mx(6), mx(7), mx(8), norm_g[i, 2], wg[i, 1], wu[i, 1], wd[i, 1])
        if not last:
            xc = _ffn_half_step(xc, mc(6), mc(7), mc(8), norm_g[i, 2], wg[i, 1], wu[i, 1], wd[i, 1])
    return x
```

```python
import functools
import math

import jax
import jax.numpy as jnp
from jax import lax
from jax.experimental import pallas as pl
from jax.experimental.pallas import tpu as pltpu

F32 = jnp.float32
BF16 = jnp.bfloat16

D_MODEL = 1024
DEPTH = 4
GRID_W = 64
HEAD_DIM = 64
N_MOD = 9
RMS_EPS = 1e-6

HY_WIDTH = D_MODEL // 2
HY_EMB = 33
HY_BANDS = (HY_EMB - 1) // 2
HY_FFN = 64
HY_FAST_DECAY = 0.3
HY_SLOW_DECAY = 1.5
HY_TARGET = 1e-2

SWA_HEADS = 8
SWA_KV_HEADS = 2
SWA_WINDOW = 128
SWA_BLOCK = 128
ROPE_BASE = 10000.0

NA_HEADS = 8
NA_MAX_ROWS = 8
NA_COLS = 16
NA_COL_BLOCK = 16

FFN_HIDDEN = 256 * ((8 * D_MODEL // 3 + 255) // 256)

LANES = 128
MXU_DIM = 256
MOD_ROWS = 16
NEG_BIG = -1e30


def _cparams(sem, vmem_mb):
    return pltpu.CompilerParams(dimension_semantics=sem, vmem_limit_bytes=vmem_mb * 1024 * 1024)


def _resident(shape):
    nd = len(shape)
    return pl.BlockSpec(shape, lambda *_: (0,) * nd, pipeline_mode=pl.Buffered(1))


def _split(x):
    hi = x.astype(BF16)
    lo = (x - hi.astype(F32)).astype(BF16)
    return hi, lo


def _dot(a, b):
    return jnp.dot(a, b, preferred_element_type=F32)


def _dot3(a_hi, a_lo, b_hi, b_lo):
    return _dot(a_hi, b_hi) + _dot(a_lo, b_hi) + _dot(a_hi, b_lo)


def _dot3f(a, b):
    a_hi, a_lo = _split(a)
    b_hi, b_lo = _split(b)
    return _dot3(a_hi, a_lo, b_hi, b_lo)


def _rms_mod(x, gain, shift, scale):
    ms = jnp.mean(x * x, axis=-1, keepdims=True)
    y = x * lax.rsqrt(ms + RMS_EPS) * gain
    return y * (1.0 + scale) + shift


def _head_rms(x, bd, gain):
    outs = []
    for j in range(x.shape[1] // MXU_DIM):
        xc = x[:, j * MXU_DIM:(j + 1) * MXU_DIM]
        hi, lo = _split(xc * xc)
        ssum = _dot(hi, bd) + _dot(lo, bd)
        outs.append(xc * lax.rsqrt(ssum * (1.0 / HEAD_DIM) + RMS_EPS) * gain)
    return outs[0] if len(outs) == 1 else jnp.concatenate(outs, axis=1)


def _rope(x, cos, sin):
    lane = lax.broadcasted_iota(jnp.int32, (x.shape[0], LANES), 1)
    first = (lane & 31) < 16
    outs = []
    for j in range(x.shape[1] // LANES):
        xc = x[:, j * LANES:(j + 1) * LANES]
        partner = jnp.where(first, pltpu.roll(xc, LANES - 16, axis=1), pltpu.roll(xc, 16, axis=1))
        outs.append(xc * cos + partner * sin)
    return outs[0] if len(outs) == 1 else jnp.concatenate(outs, axis=1)


def _mods_kernel(c_ref, w_ref, b_ref, o_ref):
    c = c_ref[...]
    a = c * jax.nn.sigmoid(c)
    o_ref[0] = _dot3f(a, w_ref[0]) + b_ref[0]


def _adaln_mods(c16, w_ada, b_ada):
    depth, d, n = w_ada.shape
    nb = 1152
    return pl.pallas_call(
        _mods_kernel, name="adaln_mods",
        grid=(depth, n // nb),
        in_specs=[
            pl.BlockSpec((MOD_ROWS, d), lambda i, j: (0, 0)),
            pl.BlockSpec((1, d, nb), lambda i, j: (i, 0, j)),
            pl.BlockSpec((1, 1, nb), lambda i, j: (i, 0, j)),
        ],
        out_specs=pl.BlockSpec((1, MOD_ROWS, nb), lambda i, j: (i, 0, j)),
        out_shape=jax.ShapeDtypeStruct((depth, MOD_ROWS, n), F32),
        compiler_params=_cparams(("arbitrary", "arbitrary"), 48),
    )(c16, w_ada, b_ada.reshape(depth, 1, n))


def _ffn_kernel(x_ref, sh_ref, sc_ref, gt_ref, g_ref, wg_ref, wu_ref, wd_ref, o_ref, *, fc):
    x = x_ref[0]
    hb = _rms_mod(x, g_ref[...], sh_ref[0], sc_ref[0]).astype(BF16)
    acc = None
    for f0 in range(0, FFN_HIDDEN, fc):
        g = _dot(hb, wg_ref[:, f0:f0 + fc])
        u = _dot(hb, wu_ref[:, f0:f0 + fc])
        a = (g * jax.nn.sigmoid(g) * u).astype(BF16)
        d = _dot(a, wd_ref[f0:f0 + fc, :])
        acc = d if acc is None else acc + d
    o_ref[0] = x + 0.5 * gt_ref[0] * acc


def _mod_spec(mod):
    if mod.shape[0] == 1:
        return pl.BlockSpec((1, 1, D_MODEL), lambda b, t: (0, 0, 0))
    return pl.BlockSpec((1, 1, D_MODEL), lambda b, t: (b, 0, 0))


def _ffn_half_step(x, shift, scale, gate, gain, wg, wu, wd):
    b, l, d = x.shape
    tm = min(512, l)
    return pl.pallas_call(
        functools.partial(_ffn_kernel, fc=FFN_HIDDEN // 2), name="ffn_half_step",
        grid=(b, l // tm),
        in_specs=[
            pl.BlockSpec((1, tm, d), lambda b, t: (b, t, 0)),
            _mod_spec(shift), _mod_spec(scale), _mod_spec(gate),
            _resident((1, d)),
            _resident(wg.shape), _resident(wu.shape), _resident(wd.shape),
        ],
        out_specs=pl.BlockSpec((1, tm, d), lambda b, t: (b, t, 0)),
        out_shape=jax.ShapeDtypeStruct(x.shape, F32),
        compiler_params=_cparams(("parallel", "parallel"), 56),
    )(x, shift, scale, gate, gain.reshape(1, d), wg, wu, wd)


def _inproj_kernel(x_ref, sh_ref, sc_ref, g_ref, cos_ref, sin_ref, hg_ref, bd_ref,
                   w_hy, w_qs, w_ks, w_vs, w_qn, w_kn, w_vn, w_gt,
                   zhy_ref, qs_ref, ks_ref, vs_ref, qn_ref, kn_ref, vn_ref, zg_ref):
    hb = _rms_mod(x_ref[0], g_ref[...], sh_ref[0], sc_ref[0]).astype(BF16)
    cos = cos_ref[...]
    sin = sin_ref[...]
    bd = bd_ref[...]
    scale = HEAD_DIM ** -0.5
    zhy_ref[0] = _dot(hb, w_hy[...])
    zg_ref[0] = _dot(hb, w_gt[...])
    qs = _rope(_head_rms(_dot(hb, w_qs[...]), bd, hg_ref[0:1, :]), cos, sin)
    qs_ref[0] = (qs * scale).astype(BF16)
    ks = _rope(_head_rms(_dot(hb, w_ks[...]), bd, hg_ref[1:2, :]), cos, sin)
    ks_ref[0] = ks.astype(BF16)
    vs_ref[0] = _dot(hb, w_vs[...]).astype(BF16)
    qn = _head_rms(_dot(hb, w_qn[...]), bd, hg_ref[2:3, :])
    qn_ref[0] = (qn * scale).astype(BF16)
    kn_ref[0] = _head_rms(_dot(hb, w_kn[...]), bd, hg_ref[3:4, :]).astype(BF16)
    vn_ref[0] = _dot(hb, w_vn[...]).astype(BF16)


def _in_projection(x, shift, scale, gain, cos, sin, head_gains, bd, ws):
    b, l, d = x.shape
    tm = min(256, l)
    widths = [w.shape[1] for w in ws]
    dtypes = [F32, BF16, BF16, BF16, BF16, BF16, BF16, F32]
    return pl.pallas_call(
        _inproj_kernel, name="in_projection",
        grid=(b, l // tm),
        in_specs=[
            pl.BlockSpec((1, tm, d), lambda b, t: (b, t, 0)),
            _mod_spec(shift), _mod_spec(scale),
            _resident((1, d)),
            pl.BlockSpec((tm, LANES), lambda b, t: (t, 0)),
            pl.BlockSpec((tm, LANES), lambda b, t: (t, 0)),
            _resident(head_gains.shape), _resident(bd.shape),
        ] + [_resident(w.shape) for w in ws],
        out_specs=[pl.BlockSpec((1, tm, n), lambda b, t: (b, t, 0)) for n in widths],
        out_shape=[jax.ShapeDtypeStruct((b, l, n), dt) for n, dt in zip(widths, dtypes)],
        compiler_params=_cparams(("parallel", "parallel"), 56),
    )(x, shift, scale, gain.reshape(1, d), cos, sin, head_gains, bd, *ws)


def _shortconv_kernel(z0_ref, z1_ref, z2_ref, w_ref, b_ref, o0_ref, o1_ref, o2_ref):
    l = z0_ref.shape[1]
    row = lax.broadcasted_iota(jnp.int32, z0_ref.shape[1:], 0)
    for g, (z_ref, o_ref) in enumerate(((z0_ref, o0_ref), (z1_ref, o1_ref), (z2_ref, o2_ref))):
        z = z_ref[0]
        prev = jnp.where(row == 0, 0.0, pltpu.roll(z, 1, axis=0))
        nxt = jnp.where(row == l - 1, 0.0, pltpu.roll(z, l - 1, axis=0))
        o_ref[0] = w_ref[g, 0:1, :] * prev + w_ref[g, 1:2, :] * z + w_ref[g, 2:3, :] * nxt + b_ref[g]


def _short_conv3(z, w, bias):
    b, l, n = z.shape
    cb = 256
    ncb = HY_WIDTH // cb
    wg = w.reshape(3, 3, HY_WIDTH).transpose(1, 0, 2)
    zin = lambda g: pl.BlockSpec((1, l, cb), lambda b, j: (b, 0, g * ncb + j))
    out = pl.BlockSpec((1, l, cb), lambda b, j: (b, 0, j))
    return pl.pallas_call(
        _shortconv_kernel, name="short_conv3",
        grid=(b, ncb),
        in_specs=[zin(0), zin(1), zin(2),
                  pl.BlockSpec((3, 3, cb), lambda b, j: (0, 0, j)),
                  pl.BlockSpec((3, 1, cb), lambda b, j: (0, 0, j))],
        out_specs=[out, out, out],
        out_shape=[jax.ShapeDtypeStruct((b, l, HY_WIDTH), F32)] * 3,
        compiler_params=_cparams(("parallel", "parallel"), 48),
    )(z, z, z, wg, bias.reshape(3, 1, HY_WIDTH))


def _filter_kernel(z_ref, w0, b0, w1, b1, w2, b2, wo, fr_ref, dl_ref, o_ref):
    z = z_ref[...]
    fr = fr_ref[...]
    a = jnp.sin(fr * (_dot3f(z, w0[...]) + b0[...]))
    a = jnp.sin(fr * (_dot3f(a, w1[...]) + b1[...]))
    a = jnp.sin(fr * (_dot3f(a, w2[...]) + b2[...]))
    hh = _dot3f(a, wo[...])
    t = z[:, 0:1]
    win = jnp.exp(-t * dl_ref[...])
    row = lax.broadcasted_iota(jnp.int32, win.shape, 0) + pl.program_id(0) * z.shape[0]
    w = HY_WIDTH
    for o in range(2):
        hp = hh[:, (2 * o) * w:(2 * o + 1) * w] * win
        hn = jnp.where(row == 0, 0.0, hh[:, (2 * o + 1) * w:(2 * o + 2) * w] * win)
        o_ref[0, :, o * w:(o + 1) * w] = hp + hn
        o_ref[1, :, o * w:(o + 1) * w] = hp - hn


def _hyena_filter_sums(zfeat, fp, deltas):
    l = zfeat.shape[0]
    tl = min(256, l)
    w0, b0, w1, b1, w2, b2, wo, fr = fp
    n = 2 * HY_WIDTH
    consts = [w0, b0, w1, b1, w2, b2, wo, fr, deltas]
    return pl.pallas_call(
        _filter_kernel, name="hyena_filter",
        grid=(l // tl,),
        in_specs=[pl.BlockSpec((tl, LANES), lambda t: (t, 0))] + [_resident(c.shape) for c in consts],
        out_specs=pl.BlockSpec((2, tl, n), lambda t: (0, t, 0)),
        out_shape=jax.ShapeDtypeStruct((2, l, n), F32),
        compiler_params=_cparams(("parallel",), 32),
    )(zfeat, *consts)


def _spectrum_kernel(a_ref, d_ref, ch, cl, sh, sl, kr_ref, ki_ref):
    a_hi, a_lo = _split(a_ref[0])
    d_hi, d_lo = _split(d_ref[0])
    kr_ref[...] = _dot3(ch[...], cl[...], a_hi, a_lo)
    ki_ref[...] = -_dot3(sh[...], sl[...], d_hi, d_lo)


def _filter_spectrum(ad, tabs):
    _, l, n = ad.shape
    fp = tabs["c_hi"].shape[0]
    fb = tabs["fb"]
    cb = 256
    tab = pl.BlockSpec((fb, l), lambda j, k: (k, 0))
    out = pl.BlockSpec((fb, cb), lambda j, k: (k, j))
    return pl.pallas_call(
        _spectrum_kernel, name="dense_spectrum",
        grid=(n // cb, fp // fb),
        in_specs=[pl.BlockSpec((1, l, cb), lambda j, k: (0, 0, j)),
                  pl.BlockSpec((1, l, cb), lambda j, k: (1, 0, j)), tab, tab, tab, tab],
        out_specs=[out, out],
        out_shape=[jax.ShapeDtypeStruct((fp, n), F32)] * 2,
        compiler_params=_cparams(("parallel", "parallel"), 48),
    )(ad, ad, tabs["c_hi"], tabs["c_lo"], tabs["s_hi"], tabs["s_lo"])


def _longconv_kernel(u_ref, g_ref, kr_ref, ki_ref, bias_ref, wk_ref, ch, cl, sh, sl, cth, ctl, sth, stl,
                     o_ref, uh_ref, ul_ref, acc_ref):
    kb = pl.program_id(2)

    @pl.when(kb == 0)
    def _():
        hi, lo = _split(u_ref[0])
        uh_ref[...] = hi
        ul_ref[...] = lo
        acc_ref[...] = jnp.zeros_like(acc_ref)

    uh = uh_ref[...]
    ul = ul_ref[...]
    xr = _dot3(ch[...], cl[...], uh, ul)
    xi = -_dot3(sh[...], sl[...], uh, ul)
    kr = kr_ref[...]
    ki = ki_ref[...]
    wk = wk_ref[...]
    yr_hi, yr_lo = _split((xr * kr - xi * ki) * wk)
    yi_hi, yi_lo = _split((xr * ki + xi * kr) * wk)
    acc_ref[...] += _dot3(cth[...], ctl[...], yr_hi, yr_lo) - _dot3(sth[...], stl[...], yi_hi, yi_lo)

    @pl.when(kb == pl.num_programs(2) - 1)
    def _():
        u = u_ref[0]
        o_ref[0] = g_ref[0] * (acc_ref[...] + u * bias_ref[...])


def _gated_long_conv(u_arr, g_arr, kr, ki, k_col, bias, tabs):
    b, l, _ = u_arr.shape
    cb = 256
    ncb = HY_WIDTH // cb
    fp = tabs["c_hi"].shape[0]
    fb = tabs["fb"]
    tab = pl.BlockSpec((fb, l), lambda b, j, k: (k, 0))
    tabt = pl.BlockSpec((l, fb), lambda b, j, k: (0, k))
    spec = pl.BlockSpec((fb, cb), lambda b, j, k: (k, k_col * ncb + j))
    return pl.pallas_call(
        _longconv_kernel, name="dense_long_conv",
        grid=(b, ncb, fp // fb),
        in_specs=[
            pl.BlockSpec((1, l, cb), lambda b, j, k: (b, 0, j)),
            pl.BlockSpec((1, l, cb), lambda b, j, k: (b, 0, j)),
            spec, spec,
            pl.BlockSpec((1, cb), lambda b, j, k: (0, j)),
            pl.BlockSpec((fb, 1), lambda b, j, k: (k, 0)),
            tab, tab, tab, tab, tabt, tabt, tabt, tabt,
        ],
        out_specs=pl.BlockSpec((1, l, cb), lambda b, j, k: (b, 0, j)),
        out_shape=jax.ShapeDtypeStruct((b, l, HY_WIDTH), F32),
        scratch_shapes=[pltpu.VMEM((l, cb), BF16), pltpu.VMEM((l, cb), BF16), pltpu.VMEM((l, cb), F32)],
        compiler_params=_cparams(("parallel", "parallel", "arbitrary"), 56),
    )(u_arr, g_arr, kr, ki, bias.reshape(1, HY_WIDTH), tabs["wk"],
      tabs["c_hi"], tabs["c_lo"], tabs["s_hi"], tabs["s_lo"],
      tabs["ct_hi"], tabs["ct_lo"], tabs["st_hi"], tabs["st_lo"])


FFT_INNER = 64
FFT_COLS = 256
FFT_K2_CHUNK = 8


def _fft_tables(l):
    n = 2 * l
    n1 = FFT_INNER
    n2 = n // n1
    h = n2 // 2
    ang = lambda idx, mod: (idx % mod).astype(F32) * (2.0 * math.pi / mod)
    k2 = jnp.arange(n2, dtype=jnp.int32)
    m2 = jnp.arange(h, dtype=jnp.int32)
    a1 = ang(k2[:, None] * m2[None, :], n2)
    c1, s1 = jnp.cos(a1), jnp.sin(a1)
    f1 = jnp.block([[c1, s1], [-s1, c1]])
    k1 = jnp.arange(n1, dtype=jnp.int32)
    a2 = ang(k1[None, None, :] * (k1[None, :, None] * n2 + k2[:, None, None]), n)
    c2, s2 = jnp.cos(a2), jnp.sin(a2)
    m = jnp.concatenate([jnp.concatenate([c2, s2], axis=2), jnp.concatenate([-s2, c2], axis=2)], axis=1)
    g = jnp.block([[c1.T, -s1.T], [s1.T, c1.T]]) * (1.0 / n)
    out = dict(n2=n2)
    for name, t in (("f1", f1), ("f1r", f1[:, :h]), ("m", m), ("mt", jnp.swapaxes(m, 1, 2)), ("g", g)):
        out[name + "_hi"], out[name + "_lo"] = _split(t)
    return out


def _row_block_matmul(w_hi, w_lo, src_ref, dst_ref, count):
    m, k = w_hi.shape

    def body(i, carry):
        hi, lo = _split(src_ref[pl.ds(pl.multiple_of(i * k, k), k), :])
        dst_ref[pl.ds(pl.multiple_of(i * m, m), m), :] = _dot3(w_hi, w_lo, hi, lo)
        return carry
    lax.fori_loop(0, count, body, 0, unroll=4)


def _fftconv_kernel(u_ref, g_ref, kr_ref, ki_ref, bias_ref, f1h, f1l, mh, ml, vh, vl, gh, gl,
                    o_ref, z_scr, p_scr, q_scr):
    kc = pl.program_id(2)
    n1 = FFT_INNER
    cc = z_scr.shape[1]
    n2 = z_scr.shape[0] // n1
    h = n2 // 2

    @pl.when(kc == 0)
    def _stage1():
        z = [jnp.swapaxes(u_ref[s, 0].reshape(h, n1, cc), 0, 1) for s in range(2)]
        z_scr[...] = jnp.concatenate(z, axis=1).reshape(n1 * n2, cc)
        _row_block_matmul(f1h[...], f1l[...], z_scr, p_scr, n1)
        q_scr[...] = jnp.swapaxes(p_scr[...].reshape(n1, 2 * n2, cc), 0, 1).reshape(2 * n2 * n1, cc)

    ks = kr_ref.shape[0]
    for j in range(ks):
        k2 = kc * ks + j
        r0 = pl.multiple_of(k2 * n1, n1)
        i0 = pl.multiple_of((n2 + k2) * n1, n1)
        ah, al = _split(jnp.concatenate([q_scr[pl.ds(r0, n1), :], q_scr[pl.ds(i0, n1), :]], axis=0))
        x = _dot3(mh[j], ml[j], ah, al)
        xr, xi = x[:n1], x[n1:]
        kr, ki = kr_ref[j], ki_ref[j]
        yh, yl = _split(jnp.concatenate([xr * kr - xi * ki, xr * ki + xi * kr], axis=0))
        b = _dot3(vh[j], vl[j], yh, yl)
        p_scr[pl.ds(r0, n1), :] = b[:n1]
        p_scr[pl.ds(i0, n1), :] = b[n1:]

    @pl.when(kc == pl.num_programs(2) - 1)
    def _stage1_inv():
        q_scr[...] = jnp.swapaxes(p_scr[...].reshape(2 * n2, n1, cc), 0, 1).reshape(2 * n2 * n1, cc)
        _row_block_matmul(gh[...], gl[...], q_scr, z_scr, n1)
        y3 = z_scr[...].reshape(n1, n2, cc)
        for s in range(2):
            y = jnp.swapaxes(y3[:, s * h:(s + 1) * h, :], 0, 1).reshape(h * n1, cc)
            o_ref[s, 0] = g_ref[s, 0] * (y + u_ref[s, 0] * bias_ref[...])


def _fft_gated_long_conv(u, gate, kr, ki, k_col, bias, ft):
    b, l, c = u.shape
    n1, n2, cc, ks = FFT_INNER, ft["n2"], FFT_COLS, FFT_K2_CHUNK
    p, ncc = b // 2, c // cc
    tok = pl.BlockSpec((2, 1, l, cc), lambda q, j, k: (0, q, 0, j))
    filt = pl.BlockSpec((ks, n1, cc), lambda q, j, k: (k, 0, k_col * ncc + j))
    tab = pl.BlockSpec((ks, 2 * n1, 2 * n1), lambda q, j, k: (k, 0, 0))
    res = lambda a: pl.BlockSpec(a.shape, lambda q, j, k: (0,) * a.ndim)
    y = pl.pallas_call(
        _fftconv_kernel, name="fft_long_conv",
        grid=(p, ncc, n2 // ks),
        in_specs=[tok, tok, filt, filt, pl.BlockSpec((1, cc), lambda q, j, k: (0, j)),
                  res(ft["f1_hi"]), res(ft["f1_lo"]), tab, tab, tab, tab, res(ft["g_hi"]), res(ft["g_lo"])],
        out_specs=tok,
        out_shape=jax.ShapeDtypeStruct((2, p, l, c), F32),
        scratch_shapes=[pltpu.VMEM((n1 * n2, cc), F32), pltpu.VMEM((2 * n2 * n1, cc), F32),
                        pltpu.VMEM((2 * n2 * n1, cc), F32)],
        compiler_params=_cparams(("parallel", "parallel", "arbitrary"), 56),
    )(u.reshape(2, p, l, c), gate.reshape(2, p, l, c), kr, ki, bias.reshape(1, c),
      ft["f1_hi"], ft["f1_lo"], ft["m_hi"], ft["m_lo"], ft["mt_hi"], ft["mt_lo"], ft["g_hi"], ft["g_lo"])
    return y.reshape(b, l, c)


def _fftspec_kernel(a_ref, d_ref, f1h, f1l, mh, ml, kr_ref, ki_ref, z_scr, p_scr, q_scr):
    n1 = FFT_INNER
    cc = z_scr.shape[1]
    h = z_scr.shape[0] // n1
    n2 = 2 * h
    for src_ref, out_ref, lo in ((a_ref, kr_ref, 0), (d_ref, ki_ref, n1)):
        z_scr[...] = jnp.swapaxes(src_ref[0].reshape(h, n1, cc), 0, 1).reshape(n1 * h, cc)
        _row_block_matmul(f1h[...], f1l[...], z_scr, p_scr, n1)
        q_scr[...] = jnp.swapaxes(p_scr[...].reshape(n1, 2 * n2, cc), 0, 1).reshape(2 * n2 * n1, cc)

        def body(k2, carry):
            r0 = pl.multiple_of(k2 * n1, n1)
            i0 = pl.multiple_of((n2 + k2) * n1, n1)
            sh, sl = _split(jnp.concatenate([q_scr[pl.ds(r0, n1), :], q_scr[pl.ds(i0, n1), :]], axis=0))
            out_ref[k2] = _dot3(mh[k2][lo:lo + n1], ml[k2][lo:lo + n1], sh, sl)
            return carry
        lax.fori_loop(0, n2, body, 0, unroll=4)


def _fft_filter_spectrum(ad, ft):
    _, l, n = ad.shape
    n1, n2, cc = FFT_INNER, ft["n2"], FFT_COLS
    h = n2 // 2
    res = lambda a: pl.BlockSpec(a.shape, lambda j: (0,) * a.ndim)
    out = pl.BlockSpec((n2, n1, cc), lambda j: (0, 0, j))
    tabs = [ft["f1r_hi"], ft["f1r_lo"], ft["m_hi"], ft["m_lo"]]
    return pl.pallas_call(
        _fftspec_kernel, name="fft_filter_spectrum",
        grid=(n // cc,),
        in_specs=[pl.BlockSpec((1, l, cc), lambda j: (0, 0, j)), pl.BlockSpec((1, l, cc), lambda j: (1, 0, j))]
                 + [res(t) for t in tabs],
        out_specs=[out, out],
        out_shape=[jax.ShapeDtypeStruct((n2, n1, n), F32)] * 2,
        scratch_shapes=[pltpu.VMEM((n1 * h, cc), F32), pltpu.VMEM((2 * n2 * n1, cc), F32),
                        pltpu.VMEM((2 * n2 * n1, cc), F32)],
        compiler_params=_cparams(("parallel",), 56),
    )(ad, ad, *tabs)


def _softmax_pv(s, v, sink=None):
    m = jnp.max(s, axis=-1, keepdims=True)
    if sink is not None:
        m = jnp.maximum(m, sink)
    p = jnp.exp(s - m)
    den = jnp.sum(p, axis=-1, keepdims=True)
    if sink is not None:
        den = den + jnp.exp(sink - m)
    return _dot(p.astype(BF16), v) / den


def _qk(q, k):
    return lax.dot_general(q, k, (((1,), (1,)), ((), ())), preferred_element_type=F32)


def _half_mask(q_tile, half):
    lane = lax.broadcasted_iota(jnp.int32, q_tile.shape, 1)
    keep = (lane < HEAD_DIM) if half == 0 else (lane >= HEAD_DIM)
    return jnp.where(keep, q_tile, jnp.zeros_like(q_tile))


def _merge_halves(o_even, o_odd):
    lane = lax.broadcasted_iota(jnp.int32, o_even.shape, 1)
    return jnp.where(lane < HEAD_DIM, o_even, o_odd)


def _swa_kernel(sink_ref, q_ref, kp_ref, kc_ref, kn_ref, vp_ref, vc_ref, vn_ref, kx_ref, vx_ref, o_ref):
    n = pl.program_id(1)
    nb = pl.num_programs(1)
    blk = SWA_BLOCK
    g = SWA_HEADS // SWA_KV_HEADS
    n_ctx = kx_ref.shape[1]
    rows = g * blk
    ri = lax.broadcasted_iota(jnp.int32, (rows, 3 * blk + n_ctx), 0) & (blk - 1)
    ci = lax.broadcasted_iota(jnp.int32, (rows, 3 * blk + n_ctx), 1)
    ok_prev = (ci < blk) & (ci >= ri) & (n > 0)
    ok_cur = (ci >= blk) & (ci < 2 * blk)
    ok_next = (ci >= 2 * blk) & (ci < 3 * blk) & (ci - 2 * blk <= ri) & (n < nb - 1)
    ok = ok_prev | ok_cur | ok_next | (ci >= 3 * blk)
    bias = jnp.where(ok, 0.0, NEG_BIG)
    rsel = lax.broadcasted_iota(jnp.int32, (rows, 1), 0) >> (blk.bit_length() - 1)
    outs = [None] * SWA_HEADS
    for kv in range(SWA_KV_HEADS):
        cs = slice(kv * LANES, (kv + 1) * LANES)
        k = jnp.concatenate([kp_ref[0, :, cs], kc_ref[0, :, cs], kn_ref[0, :, cs], kx_ref[0, :, cs]], axis=0)
        v = jnp.concatenate([vp_ref[0, :, cs], vc_ref[0, :, cs], vn_ref[0, :, cs], vx_ref[0, :, cs]], axis=0)
        q_parts = []
        sink = jnp.zeros((rows, 1), F32)
        for gi in range(g):
            h = kv * g + gi
            q_parts.append(_half_mask(q_ref[0, :, (h // 2) * LANES:(h // 2 + 1) * LANES], h % 2))
            sink = jnp.where(rsel == gi, sink_ref[h], sink)
        s = _qk(jnp.concatenate(q_parts, axis=0), k) + bias
        o = _softmax_pv(s, v, sink)
        for gi in range(g):
            outs[kv * g + gi] = o[gi * blk:(gi + 1) * blk]
    for p in range(SWA_HEADS // 2):
        o_ref[0, :, p * LANES:(p + 1) * LANES] = _merge_halves(outs[2 * p], outs[2 * p + 1]).astype(o_ref.dtype)


def _window_gqa(q, k, v, kx, vx, sink):
    b, l, _ = q.shape
    nb = l // SWA_BLOCK
    n_ctx = kx.shape[1]
    kvw = k.shape[2]

    def blk(off):
        return pl.BlockSpec((1, SWA_BLOCK, kvw), lambda b, n: (b, jnp.clip(n + off, 0, nb - 1), 0))

    ctx = pl.BlockSpec((1, n_ctx, kvw), lambda b, n: (b, 0, 0))
    return pl.pallas_call(
        _swa_kernel, name="window_gqa",
        grid=(b, nb),
        in_specs=[
            pl.BlockSpec(memory_space=pltpu.SMEM),
            pl.BlockSpec((1, SWA_BLOCK, q.shape[2]), lambda b, n: (b, n, 0)),
            blk(-1), blk(0), blk(1), blk(-1), blk(0), blk(1), ctx, ctx,
        ],
        out_specs=pl.BlockSpec((1, SWA_BLOCK, q.shape[2]), lambda b, n: (b, n, 0)),
        out_shape=jax.ShapeDtypeStruct(q.shape, BF16),
        compiler_params=_cparams(("parallel", "parallel"), 32),
    )(sink, q, k, k, k, v, v, v, kx, vx)


def _na_bias_kernel(rpb_ref, o_ref):
    off = pl.program_id(0)
    h = pl.program_id(1)
    nd = 2 * NA_COLS - 1
    q = lax.broadcasted_iota(jnp.int32, (GRID_W, LANES), 0)
    lane = lax.broadcasted_iota(jnp.int32, (GRID_W, LANES), 1)
    kc = lane & (GRID_W - 1)
    upper = lane >= GRID_W
    cstart = jnp.clip(q - NA_COLS // 2, 0, GRID_W - NA_COLS)
    valid = (kc >= cstart) & (kc < cstart + NA_COLS)
    dc = jnp.clip(kc - q, -(NA_COLS - 1), NA_COLS - 1) + (NA_COLS - 1)
    for j in range(NA_MAX_ROWS // 2):
        base0 = (h * (2 * NA_MAX_ROWS - 1) + (2 * j - off + NA_MAX_ROWS - 1)) * nd
        base1 = base0 + nd
        t = jnp.zeros((GRID_W, LANES), F32)
        for d in range(nd):
            val = jnp.where(upper, rpb_ref[base1 + d], rpb_ref[base0 + d])
            t = jnp.where(dc == d, val, t)
        o_ref[0, 0, :, j * LANES:(j + 1) * LANES] = jnp.where(valid, t, NEG_BIG)


def _na_bias_table(rpb):
    return pl.pallas_call(
        _na_bias_kernel, name="na_bias_table",
        grid=(NA_MAX_ROWS, NA_HEADS),
        in_specs=[pl.BlockSpec(memory_space=pltpu.SMEM)],
        out_specs=pl.BlockSpec((1, 1, GRID_W, NA_MAX_ROWS * GRID_W), lambda o, h: (o, h, 0, 0)),
        out_shape=jax.ShapeDtypeStruct((NA_MAX_ROWS, NA_HEADS, GRID_W, NA_MAX_ROWS * GRID_W), F32),
        compiler_params=_cparams(("arbitrary", "arbitrary"), 16),
    )(rpb.reshape(-1))


NA_ROWS_PER_STEP = 4


def _na_kernel(q_ref, k_ref, v_ref, kx_ref, vx_ref, bias_ref, o_ref, *, rows):
    wr = NA_MAX_ROWS
    nloc = wr * GRID_W
    for i in range(NA_ROWS_PER_STEP):
        r = pl.program_id(1) * NA_ROWS_PER_STEP + i
        first = jnp.clip(r - wr // 2, 0, rows - wr)
        start = pl.multiple_of(first * GRID_W, GRID_W)
        off = r - first
        qr = slice(i * GRID_W, (i + 1) * GRID_W)
        for p in range(NA_HEADS // 2):
            cs = slice(p * LANES, (p + 1) * LANES)
            q_tile = q_ref[0, qr, cs]
            q2 = jnp.concatenate([_half_mask(q_tile, 0), _half_mask(q_tile, 1)], axis=0)
            k = jnp.concatenate([k_ref[0, pl.ds(start, nloc), cs], kx_ref[0, :, cs]], axis=0)
            v = jnp.concatenate([v_ref[0, pl.ds(start, nloc), cs], vx_ref[0, :, cs]], axis=0)
            s = _qk(q2, k)
            bias = jnp.concatenate([bias_ref[off, 2 * p], bias_ref[off, 2 * p + 1]], axis=0)
            s = jnp.concatenate([s[:, :nloc] + bias, s[:, nloc:]], axis=1)
            o = _softmax_pv(s, v)
            o_ref[0, qr, cs] = _merge_halves(o[:GRID_W], o[GRID_W:]).astype(o_ref.dtype)


def _neighbourhood_attention(q, k, v, kx, vx, bias_tab):
    b, l, w = q.shape
    rows = l // GRID_W
    n_ctx = kx.shape[1]
    tq = NA_ROWS_PER_STEP * GRID_W
    full = pl.BlockSpec((1, l, w), lambda b, r: (b, 0, 0))
    ctx = pl.BlockSpec((1, n_ctx, w), lambda b, r: (b, 0, 0))
    return pl.pallas_call(
        functools.partial(_na_kernel, rows=rows), name="neighbourhood_attention",
        grid=(b, rows // NA_ROWS_PER_STEP),
        in_specs=[pl.BlockSpec((1, tq, w), lambda b, r: (b, r, 0)), full, full, ctx, ctx,
                  _resident(bias_tab.shape)],
        out_specs=pl.BlockSpec((1, tq, w), lambda b, r: (b, r, 0)),
        out_shape=jax.ShapeDtypeStruct(q.shape, BF16),
        compiler_params=_cparams(("parallel", "arbitrary"), 48),
    )(q, k, v, kx, vx, bias_tab)


def _ctx_attn_kernel(sink_ref, qs_ref, ks_ref, vs_ref, qn_ref, kn_ref, vn_ref, os_ref, on_ref):
    n = qs_ref.shape[1]
    g = SWA_HEADS // SWA_KV_HEADS
    rsel = lax.broadcasted_iota(jnp.int32, (g * n, 1), 0) >> (n.bit_length() - 1)
    outs = [None] * SWA_HEADS
    for kv in range(SWA_KV_HEADS):
        cs = slice(kv * LANES, (kv + 1) * LANES)
        q_parts = []
        sink = jnp.zeros((g * n, 1), F32)
        for gi in range(g):
            h = kv * g + gi
            q_parts.append(_half_mask(qs_ref[0, :, (h // 2) * LANES:(h // 2 + 1) * LANES], h % 2))
            sink = jnp.where(rsel == gi, sink_ref[h], sink)
        o = _softmax_pv(_qk(jnp.concatenate(q_parts, axis=0), ks_ref[0, :, cs]), vs_ref[0, :, cs], sink)
        for gi in range(g):
            outs[kv * g + gi] = o[gi * n:(gi + 1) * n]
    for p in range(SWA_HEADS // 2):
        os_ref[0, :, p * LANES:(p + 1) * LANES] = _merge_halves(outs[2 * p], outs[2 * p + 1]).astype(os_ref.dtype)
    for p in range(NA_HEADS // 2):
        cs = slice(p * LANES, (p + 1) * LANES)
        q_tile = qn_ref[0, :, cs]
        q2 = jnp.concatenate([_half_mask(q_tile, 0), _half_mask(q_tile, 1)], axis=0)
        o = _softmax_pv(_qk(q2, kn_ref[0, :, cs]), vn_ref[0, :, cs])
        on_ref[0, :, cs] = _merge_halves(o[:n], o[n:]).astype(on_ref.dtype)


def _context_attention(qs, ks, vs, qn, kn, vn, sink):
    b, n, w = qs.shape
    spec = lambda a: pl.BlockSpec((1,) + a.shape[1:], lambda b: (b, 0, 0))
    return pl.pallas_call(
        _ctx_attn_kernel, name="context_attention",
        grid=(b,),
        in_specs=[pl.BlockSpec(memory_space=pltpu.SMEM)] + [spec(a) for a in (qs, ks, vs, qn, kn, vn)],
        out_specs=[spec(qs), spec(qn)],
        out_shape=[jax.ShapeDtypeStruct(qs.shape, BF16), jax.ShapeDtypeStruct(qn.shape, BF16)],
        compiler_params=_cparams(("parallel",), 32),
    )(sink, qs, ks, vs, qn, kn, vn)


def _merge_kernel(x_ref, gt_ref, yh_ref, ys_ref, yn_ref, zg_ref, wb_ref, wo_ref, o_ref):
    d = D_MODEL
    ys = (yh_ref[0].astype(BF16), ys_ref[0], yn_ref[0])
    m = None
    for n in range(3):
        proj = _dot(ys[n], wb_ref[n])
        term = jax.nn.sigmoid(zg_ref[0, :, n * d:(n + 1) * d]) * proj
        m = term if m is None else m + term
    o_ref[0] = x_ref[0] + gt_ref[0] * _dot(m.astype(BF16), wo_ref[...])


def _merge_branches(x, gate, y_hy, y_swa, y_na, z_gate, w_branch, w_out):
    b, l, d = x.shape
    tm = min(512, l)
    tok = lambda n: pl.BlockSpec((1, tm, n), lambda b, t: (b, t, 0))
    return pl.pallas_call(
        _merge_kernel, name="merge_branches",
        grid=(b, l // tm),
        in_specs=[tok(d), _mod_spec(gate), tok(HY_WIDTH), tok(HY_WIDTH), tok(HY_WIDTH), tok(3 * d),
                  _resident(w_branch.shape), _resident(w_out.shape)],
        out_specs=tok(d),
        out_shape=jax.ShapeDtypeStruct(x.shape, F32),
        compiler_params=_cparams(("parallel", "parallel"), 48),
    )(x, gate, y_hy, y_swa, y_na, z_gate, w_branch, w_out)


def _dft_tables(l):
    n = 2 * l
    fb = 256 if l >= 1024 else LANES * ((l + 1 + LANES - 1) // LANES)
    fp = fb * ((l + 1 + fb - 1) // fb)
    k = jnp.arange(fp, dtype=jnp.int32)[:, None]
    t = jnp.arange(l, dtype=jnp.int32)[None, :]
    ang = ((k * t) % n).astype(F32) * (2.0 * math.pi / n)
    live = k <= l
    c = jnp.where(live, jnp.cos(ang), 0.0)
    s = jnp.where(live, jnp.sin(ang), 0.0)
    c_hi, c_lo = _split(c)
    s_hi, s_lo = _split(s)
    wk = jnp.where((k == 0) | (k == l), 1.0, 2.0) * jnp.where(live, 1.0 / n, 0.0)
    return dict(fb=fb, c_hi=c_hi, c_lo=c_lo, s_hi=s_hi, s_lo=s_lo,
                ct_hi=c_hi.T, ct_lo=c_lo.T, st_hi=s_hi.T, st_lo=s_lo.T, wk=wk.astype(F32))


def _filter_features(l):
    t = jnp.linspace(0.0, 1.0, l, dtype=F32)[:, None]
    w = (2.0 * math.pi / l) * jnp.arange(l, dtype=F32)[:, None]
    f = jnp.linspace(1e-4, HY_BANDS - 1, HY_BANDS, dtype=F32)[None, :]
    z = jnp.concatenate([t, jnp.cos(f * w), -jnp.sin(f * w)], axis=-1)
    return jnp.pad(z, ((0, 0), (0, LANES - HY_EMB)))


def _decay_rates():
    max_decay = math.log(HY_TARGET) / HY_FAST_DECAY
    min_decay = math.log(HY_TARGET) / HY_SLOW_DECAY
    return jnp.abs(jnp.linspace(min_decay, max_decay, HY_WIDTH, dtype=F32))[None, :]


def _rope_tables(l):
    pos = jnp.arange(l)
    row = (pos // GRID_W).astype(F32)
    col = (pos % GRID_W).astype(F32)
    half = HEAD_DIM // 2
    inv = 1.0 / (ROPE_BASE ** (jnp.arange(0, half, 2, dtype=F32) / half))
    ar = row[:, None] * inv[None, :]
    ac = col[:, None] * inv[None, :]
    cos = jnp.concatenate([jnp.cos(ar), jnp.cos(ar), jnp.cos(ac), jnp.cos(ac)], axis=-1)
    sin = jnp.concatenate([-jnp.sin(ar), jnp.sin(ar), -jnp.sin(ac), jnp.sin(ac)], axis=-1)
    return jnp.tile(cos, (1, 2)), jnp.tile(sin, (1, 2))


def _pad_to(a, shape):
    return jnp.pad(a, [(0, s - d) for d, s in zip(a.shape, shape)])


def _use_fft(bsz, length):
    return bsz % 2 == 0 and length % (8 * FFT_INNER) == 0


def _conv_tables(bsz, length):
    return _fft_tables(length) if _use_fft(bsz, length) else _dft_tables(length)


def _hyena_branch(z_hy, short_w, short_b, feat, fparams, deltas, bias, tabs):
    v, x1, x2 = _short_conv3(z_hy, short_w, short_b)
    ad = _hyena_filter_sums(feat, fparams, deltas)
    if "n2" in tabs:
        kr, ki = _fft_filter_spectrum(ad, tabs)
        conv = functools.partial(_fft_gated_long_conv, ft=tabs)
    else:
        kr, ki = _filter_spectrum(ad, tabs)
        conv = functools.partial(_gated_long_conv, tabs=tabs)
    y1 = conv(v, x1, kr, ki, 0, bias[0])
    return conv(y1, x2, kr, ki, 1, bias[1])


def kernel(x, c, ctx, c_ctx, w_ada, b_ada, norm_g, ffn_w_gate, ffn_w_up, ffn_w_down,
           w_in, hy_short_w, hy_short_b, hy_pe_w0, hy_pe_b0, hy_pe_w1, hy_pe_b1,
           hy_pe_w2, hy_pe_b2, hy_pe_wout, hy_sin_freq, hy_bias,
           swa_q_gain, swa_k_gain, swa_sink, na_q_gain, na_k_gain, na_rpb,
           w_branch, w_out):
    bsz, seq, d = x.shape
    n_ctx = ctx.shape[1]
    depth = w_ada.shape[0]

    c16 = _pad_to(jnp.concatenate([c, c_ctx[None, :]], axis=0), (MOD_ROWS, d))
    mods = _adaln_mods(c16, w_ada, b_ada).reshape(depth, MOD_ROWS, N_MOD, 1, d)

    tabs_x = _conv_tables(bsz, seq)
    tabs_c = _conv_tables(bsz, n_ctx)
    feat_x = _filter_features(seq)
    feat_c = _filter_features(n_ctx)
    deltas = _decay_rates()
    cos_x, sin_x = _rope_tables(seq)
    cos_c = jnp.ones((n_ctx, LANES), F32)
    sin_c = jnp.zeros((n_ctx, LANES), F32)
    eye = jnp.arange(MXU_DIM) // HEAD_DIM
    bd = (eye[:, None] == eye[None, :]).astype(BF16)

    wg = ffn_w_gate.astype(BF16)
    wu = ffn_w_up.astype(BF16)
    wd = ffn_w_down.astype(BF16)
    wb = w_branch.astype(BF16)
    wo = w_out.astype(BF16)
    win = w_in.astype(BF16)
    hyc = 3 * HY_WIDTH
    qw = SWA_HEADS * HEAD_DIM
    o_s = hyc
    o_n = o_s + qw + 2 * SWA_KV_HEADS * HEAD_DIM
    o_g = o_n + 3 * NA_HEADS * HEAD_DIM

    def dup(w2):
        return jnp.concatenate([w2[:, :HEAD_DIM], w2[:, :HEAD_DIM], w2[:, HEAD_DIM:], w2[:, HEAD_DIM:]], axis=1)

    xc = ctx
    for i in range(depth):
        last = i == depth - 1
        mx = lambda j: mods[i, :bsz, j]
        mc = lambda j: mods[i, bsz:bsz + 1, j]
        wi = win[i]
        ws = [wi[:, :hyc], wi[:, o_s:o_s + qw], dup(wi[:, o_s + qw:o_s + qw + 128]),
              dup(wi[:, o_s + qw + 128:o_n]), wi[:, o_n:o_n + 512], wi[:, o_n + 512:o_n + 1024],
              wi[:, o_n + 1024:o_g], wi[:, o_g:]]
        tile2 = lambda g: jnp.tile(g, 2 * MXU_DIM // LANES)
        head_gains = jnp.stack([tile2(swa_q_gain[i]), tile2(swa_k_gain[i]), tile2(na_q_gain[i]), tile2(na_k_gain[i])])
        fparams = (_pad_to(hy_pe_w0[i], (LANES, LANES)), _pad_to(hy_pe_b0[i][None], (1, LANES)),
                   _pad_to(hy_pe_w1[i], (LANES, LANES)), _pad_to(hy_pe_b1[i][None], (1, LANES)),
                   _pad_to(hy_pe_w2[i], (LANES, LANES)), _pad_to(hy_pe_b2[i][None], (1, LANES)),
                   _pad_to(hy_pe_wout[i], (LANES, 4 * HY_WIDTH)), _pad_to(hy_sin_freq[i][None], (1, LANES)))

        x = _ffn_half_step(x, mx(0), mx(1), mx(2), norm_g[i, 0], wg[i, 0], wu[i, 0], wd[i, 0])
        xc = _ffn_half_step(xc, mc(0), mc(1), mc(2), norm_g[i, 0], wg[i, 0], wu[i, 0], wd[i, 0])

        z_hy, q_s, k_s, v_s, q_n, k_n, v_n, z_gate = _in_projection(
            x, mx(3), mx(4), norm_g[i, 1], cos_x, sin_x, head_gains, bd, ws)
        zc_hy, qc_s, kc_s, vc_s, qc_n, kc_n, vc_n, zc_gate = _in_projection(
            xc, mc(3), mc(4), norm_g[i, 1], cos_c, sin_c, head_gains, bd, ws)

        y_hy = _hyena_branch(z_hy, hy_short_w[i], hy_short_b[i], feat_x, fparams, deltas, hy_bias[i], tabs_x)
        y_swa = _window_gqa(q_s, k_s, v_s, kc_s, vc_s, swa_sink[i])
        y_na = _neighbourhood_attention(q_n, k_n, v_n, kc_n, vc_n, _na_bias_table(na_rpb[i]))
        x = _merge_branches(x, mx(5), y_hy, y_swa, y_na, z_gate, wb[i], wo[i])

        if not last:
            yc_hy = _hyena_branch(zc_hy, hy_short_w[i], hy_short_b[i], feat_c, fparams, deltas, hy_bias[i], tabs_c)
            yc_swa, yc_na = _context_attention(qc_s, kc_s, vc_s, qc_n, kc_n, vc_n, swa_sink[i])
            xc = _merge_branches(xc, mc(5), yc_hy, yc_swa, yc_na, zc_gate, wb[i], wo[i])

        x = _ffn_half_step(x, mx(6), mx(7), mx(8), norm_g[i, 2], wg[i, 1], wu[i, 1], wd[i, 1])
        if not last:
            xc = _ffn_half_step(xc, mc(6), mc(7), mc(8), norm_g[i, 2], wg[i, 1], wu[i, 1], wd[i, 1])
    return x
```

```python
import functools
import math

import jax
import jax.numpy as jnp
from jax import lax
from jax.experimental import pallas as pl
from jax.experimental.pallas import tpu as pltpu

F32 = jnp.float32
BF16 = jnp.bfloat16

D_MODEL = 1024
DEPTH = 4
GRID_W = 64
HEAD_DIM = 64
N_MOD = 9
RMS_EPS = 1e-6

HY_WIDTH = D_MODEL // 2
HY_EMB = 33
HY_BANDS = (HY_EMB - 1) // 2
HY_FFN = 64
HY_FAST_DECAY = 0.3
HY_SLOW_DECAY = 1.5
HY_TARGET = 1e-2

SWA_HEADS = 8
SWA_KV_HEADS = 2
SWA_WINDOW = 128
SWA_BLOCK = 128
ROPE_BASE = 10000.0

NA_HEADS = 8
NA_MAX_ROWS = 8
NA_COLS = 16
NA_COL_BLOCK = 16

FFN_HIDDEN = 256 * ((8 * D_MODEL // 3 + 255) // 256)

LANES = 128
MXU_DIM = 256
MOD_ROWS = 16
NEG_BIG = -1e30


def _cparams(sem, vmem_mb):
    return pltpu.CompilerParams(dimension_semantics=sem, vmem_limit_bytes=vmem_mb * 1024 * 1024)


def _resident(shape):
    nd = len(shape)
    return pl.BlockSpec(shape, lambda *_: (0,) * nd, pipeline_mode=pl.Buffered(1))


def _split(x):
    hi = x.astype(BF16)
    lo = (x - hi.astype(F32)).astype(BF16)
    return hi, lo


def _dot(a, b):
    return jnp.dot(a, b, preferred_element_type=F32)


def _dot3(a_hi, a_lo, b_hi, b_lo):
    return _dot(a_hi, b_hi) + _dot(a_lo, b_hi) + _dot(a_hi, b_lo)


def _dot3f(a, b):
    a_hi, a_lo = _split(a)
    b_hi, b_lo = _split(b)
    return _dot3(a_hi, a_lo, b_hi, b_lo)


def _rms_mod(x, gain, shift, scale):
    ms = jnp.mean(x * x, axis=-1, keepdims=True)
    y = x * lax.rsqrt(ms + RMS_EPS) * gain
    return y * (1.0 + scale) + shift


def _head_rms(x, bd, gain):
    outs = []
    for j in range(x.shape[1] // MXU_DIM):
        xc = x[:, j * MXU_DIM:(j + 1) * MXU_DIM]
        hi, lo = _split(xc * xc)
        ssum = _dot(hi, bd) + _dot(lo, bd)
        outs.append(xc * lax.rsqrt(ssum * (1.0 / HEAD_DIM) + RMS_EPS) * gain)
    return outs[0] if len(outs) == 1 else jnp.concatenate(outs, axis=1)


def _rope(x, cos, sin):
    lane = lax.broadcasted_iota(jnp.int32, (x.shape[0], LANES), 1)
    first = (lane & 31) < 16
    outs = []
    for j in range(x.shape[1] // LANES):
        xc = x[:, j * LANES:(j + 1) * LANES]
        partner = jnp.where(first, pltpu.roll(xc, LANES - 16, axis=1), pltpu.roll(xc, 16, axis=1))
        outs.append(xc * cos + partner * sin)
    return outs[0] if len(outs) == 1 else jnp.concatenate(outs, axis=1)


def _mods_kernel(c_ref, w_ref, b_ref, o_ref):
    c = c_ref[...]
    a = c * jax.nn.sigmoid(c)
    o_ref[0] = _dot3f(a, w_ref[0]) + b_ref[0]


def _adaln_mods(c16, w_ada, b_ada):
    depth, d, n = w_ada.shape
    nb = 1152
    return pl.pallas_call(
        _mods_kernel, name="adaln_mods",
        grid=(depth, n // nb),
        in_specs=[
            pl.BlockSpec((MOD_ROWS, d), lambda i, j: (0, 0)),
            pl.BlockSpec((1, d, nb), lambda i, j: (i, 0, j)),
            pl.BlockSpec((1, 1, nb), lambda i, j: (i, 0, j)),
        ],
        out_specs=pl.BlockSpec((1, MOD_ROWS, nb), lambda i, j: (i, 0, j)),
        out_shape=jax.ShapeDtypeStruct((depth, MOD_ROWS, n), F32),
        compiler_params=_cparams(("arbitrary", "arbitrary"), 48),
    )(c16, w_ada, b_ada.reshape(depth, 1, n))


def _ffn_kernel(x_ref, sh_ref, sc_ref, gt_ref, g_ref, wg_ref, wu_ref, wd_ref, o_ref, *, fc):
    x = x_ref[0]
    hb = _rms_mod(x, g_ref[...], sh_ref[0], sc_ref[0]).astype(BF16)
    acc = None
    for f0 in range(0, FFN_HIDDEN, fc):
        g = _dot(hb, wg_ref[:, f0:f0 + fc])
        u = _dot(hb, wu_ref[:, f0:f0 + fc])
        a = (g * jax.nn.sigmoid(g) * u).astype(BF16)
        d = _dot(a, wd_ref[f0:f0 + fc, :])
        acc = d if acc is None else acc + d
    o_ref[0] = x + 0.5 * gt_ref[0] * acc


def _mod_spec(mod):
    if mod.shape[0] == 1:
        return pl.BlockSpec((1, 1, D_MODEL), lambda b, t: (0, 0, 0))
    return pl.BlockSpec((1, 1, D_MODEL), lambda b, t: (b, 0, 0))


def _ffn_half_step(x, shift, scale, gate, gain, wg, wu, wd):
    b, l, d = x.shape
    tm = min(512, l)
    return pl.pallas_call(
        functools.partial(_ffn_kernel, fc=FFN_HIDDEN // 2), name="ffn_half_step",
        grid=(b, l // tm),
        in_specs=[
            pl.BlockSpec((1, tm, d), lambda b, t: (b, t, 0)),
            _mod_spec(shift), _mod_spec(scale), _mod_spec(gate),
            _resident((1, d)),
            _resident(wg.shape), _resident(wu.shape), _resident(wd.shape),
        ],
        out_specs=pl.BlockSpec((1, tm, d), lambda b, t: (b, t, 0)),
        out_shape=jax.ShapeDtypeStruct(x.shape, F32),
        compiler_params=_cparams(("parallel", "parallel"), 56),
    )(x, shift, scale, gate, gain.reshape(1, d), wg, wu, wd)


_HY_COLS = 3 * HY_WIDTH
_SWA_Q = SWA_HEADS * HEAD_DIM
_SWA_KV = SWA_KV_HEADS * HEAD_DIM
_NA_W = NA_HEADS * HEAD_DIM
_OFF_SWA = _HY_COLS
_OFF_NA = _OFF_SWA + _SWA_Q + 2 * _SWA_KV
_OFF_GATE = _OFF_NA + 3 * _NA_W


def _inproj_kernel(x_ref, sh_ref, sc_ref, g_ref, cos_ref, sin_ref, hg_ref, bd_ref, w_ref, wkv_ref,
                   zhy_ref, qs_ref, ks_ref, vs_ref, qn_ref, kn_ref, vn_ref, gt_ref):
    hb = _rms_mod(x_ref[0], g_ref[...], sh_ref[0], sc_ref[0]).astype(BF16)
    cos = cos_ref[...]
    sin = sin_ref[...]
    bd = bd_ref[...]
    scale = HEAD_DIM ** -0.5
    proj = lambda lo, n: _dot(hb, w_ref[:, lo:lo + n])
    zhy_ref[0] = proj(0, _HY_COLS)
    gt_ref[0] = jax.nn.sigmoid(proj(_OFF_GATE, 3 * D_MODEL)).astype(BF16)
    qs = _rope(_head_rms(proj(_OFF_SWA, _SWA_Q), bd, hg_ref[0:1, :]), cos, sin)
    qs_ref[0] = (qs * scale).astype(BF16)
    kvw = 2 * _SWA_KV
    ks = _rope(_head_rms(_dot(hb, wkv_ref[:, :kvw]), bd, hg_ref[1:2, :]), cos, sin)
    ks_ref[0] = ks.astype(BF16)
    vs_ref[0] = _dot(hb, wkv_ref[:, kvw:]).astype(BF16)
    qn = _head_rms(proj(_OFF_NA, _NA_W), bd, hg_ref[2:3, :])
    qn_ref[0] = (qn * scale).astype(BF16)
    kn_ref[0] = _head_rms(proj(_OFF_NA + _NA_W, _NA_W), bd, hg_ref[3:4, :]).astype(BF16)
    vn_ref[0] = proj(_OFF_NA + 2 * _NA_W, _NA_W).astype(BF16)


def _in_projection(x, shift, scale, gain, cos, sin, head_gains, bd, w, w_kv_dup):
    b, l, d = x.shape
    tm = min(512, l)
    kvw = 2 * _SWA_KV
    widths = [_HY_COLS, _SWA_Q, kvw, kvw, _NA_W, _NA_W, _NA_W, 3 * d]
    dtypes = [F32] + [BF16] * 7
    return pl.pallas_call(
        _inproj_kernel, name="in_projection",
        grid=(b, l // tm),
        in_specs=[
            pl.BlockSpec((1, tm, d), lambda b, t: (b, t, 0)),
            _mod_spec(shift), _mod_spec(scale),
            _resident((1, d)),
            pl.BlockSpec((tm, LANES), lambda b, t: (t, 0)),
            pl.BlockSpec((tm, LANES), lambda b, t: (t, 0)),
            _resident(head_gains.shape), _resident(bd.shape), _resident(w.shape), _resident(w_kv_dup.shape),
        ],
        out_specs=[pl.BlockSpec((1, tm, n), lambda b, t: (b, t, 0)) for n in widths],
        out_shape=[jax.ShapeDtypeStruct((b, l, n), dt) for n, dt in zip(widths, dtypes)],
        compiler_params=_cparams(("parallel", "parallel"), 56),
    )(x, shift, scale, gain.reshape(1, d), cos, sin, head_gains, bd, w, w_kv_dup)


def _shortconv_kernel(z0_ref, z1_ref, z2_ref, w_ref, b_ref, o0_ref, o1_ref, o2_ref):
    l = z0_ref.shape[1]
    row = lax.broadcasted_iota(jnp.int32, z0_ref.shape[1:], 0)
    for g, (z_ref, o_ref) in enumerate(((z0_ref, o0_ref), (z1_ref, o1_ref), (z2_ref, o2_ref))):
        z = z_ref[0]
        prev = jnp.where(row == 0, 0.0, pltpu.roll(z, 1, axis=0))
        nxt = jnp.where(row == l - 1, 0.0, pltpu.roll(z, l - 1, axis=0))
        o_ref[0] = w_ref[g, 0:1, :] * prev + w_ref[g, 1:2, :] * z + w_ref[g, 2:3, :] * nxt + b_ref[g]


def _short_conv3(z, w, bias):
    b, l, n = z.shape
    cb = 256
    ncb = HY_WIDTH // cb
    wg = w.reshape(3, 3, HY_WIDTH).transpose(1, 0, 2)
    zin = lambda g: pl.BlockSpec((1, l, cb), lambda b, j: (b, 0, g * ncb + j))
    out = pl.BlockSpec((1, l, cb), lambda b, j: (b, 0, j))
    return pl.pallas_call(
        _shortconv_kernel, name="short_conv3",
        grid=(b, ncb),
        in_specs=[zin(0), zin(1), zin(2),
                  pl.BlockSpec((3, 3, cb), lambda b, j: (0, 0, j)),
                  pl.BlockSpec((3, 1, cb), lambda b, j: (0, 0, j))],
        out_specs=[out, out, out],
        out_shape=[jax.ShapeDtypeStruct((b, l, HY_WIDTH), F32)] * 3,
        compiler_params=_cparams(("parallel", "parallel"), 48),
    )(z, z, z, wg, bias.reshape(3, 1, HY_WIDTH))


def _filter_kernel(z_ref, w0, b0, w1, b1, w2, b2, wo, fr_ref, dl_ref, o_ref):
    z = z_ref[...]
    fr = fr_ref[...]
    a = jnp.sin(fr * (_dot3f(z, w0[...]) + b0[...]))
    a = jnp.sin(fr * (_dot3f(a, w1[...]) + b1[...]))
    a = jnp.sin(fr * (_dot3f(a, w2[...]) + b2[...]))
    hh = _dot3f(a, wo[...])
    t = z[:, 0:1]
    win = jnp.exp(-t * dl_ref[...])
    row = lax.broadcasted_iota(jnp.int32, win.shape, 0) + pl.program_id(0) * z.shape[0]
    w = HY_WIDTH
    for o in range(2):
        hp = hh[:, (2 * o) * w:(2 * o + 1) * w] * win
        hn = jnp.where(row == 0, 0.0, hh[:, (2 * o + 1) * w:(2 * o + 2) * w] * win)
        o_ref[0, :, o * w:(o + 1) * w] = hp + hn
        o_ref[1, :, o * w:(o + 1) * w] = hp - hn


def _hyena_filter_sums(zfeat, fp, deltas):
    l = zfeat.shape[0]
    tl = min(256, l)
    w0, b0, w1, b1, w2, b2, wo, fr = fp
    n = 2 * HY_WIDTH
    consts = [w0, b0, w1, b1, w2, b2, wo, fr, deltas]
    return pl.pallas_call(
        _filter_kernel, name="hyena_filter",
        grid=(l // tl,),
        in_specs=[pl.BlockSpec((tl, LANES), lambda t: (t, 0))] + [_resident(c.shape) for c in consts],
        out_specs=pl.BlockSpec((2, tl, n), lambda t: (0, t, 0)),
        out_shape=jax.ShapeDtypeStruct((2, l, n), F32),
        compiler_params=_cparams(("parallel",), 32),
    )(zfeat, *consts)


def _spectrum_kernel(a_ref, d_ref, ch, cl, sh, sl, kr_ref, ki_ref):
    a_hi, a_lo = _split(a_ref[0])
    d_hi, d_lo = _split(d_ref[0])
    kr_ref[...] = _dot3(ch[...], cl[...], a_hi, a_lo)
    ki_ref[...] = -_dot3(sh[...], sl[...], d_hi, d_lo)


def _filter_spectrum(ad, tabs):
    _, l, n = ad.shape
    fp = tabs["c_hi"].shape[0]
    fb = tabs["fb"]
    cb = 256
    tab = pl.BlockSpec((fb, l), lambda j, k: (k, 0))
    out = pl.BlockSpec((fb, cb), lambda j, k: (k, j))
    return pl.pallas_call(
        _spectrum_kernel, name="dense_spectrum",
        grid=(n // cb, fp // fb),
        in_specs=[pl.BlockSpec((1, l, cb), lambda j, k: (0, 0, j)),
                  pl.BlockSpec((1, l, cb), lambda j, k: (1, 0, j)), tab, tab, tab, tab],
        out_specs=[out, out],
        out_shape=[jax.ShapeDtypeStruct((fp, n), F32)] * 2,
        compiler_params=_cparams(("parallel", "parallel"), 48),
    )(ad, ad, tabs["c_hi"], tabs["c_lo"], tabs["s_hi"], tabs["s_lo"])


def _longconv_kernel(u_ref, g_ref, kr_ref, ki_ref, bias_ref, wk_ref, ch, cl, sh, sl, cth, ctl, sth, stl,
                     o_ref, uh_ref, ul_ref, acc_ref):
    kb = pl.program_id(2)

    @pl.when(kb == 0)
    def _():
        hi, lo = _split(u_ref[0])
        uh_ref[...] = hi
        ul_ref[...] = lo
        acc_ref[...] = jnp.zeros_like(acc_ref)

    uh = uh_ref[...]
    ul = ul_ref[...]
    xr = _dot3(ch[...], cl[...], uh, ul)
    xi = -_dot3(sh[...], sl[...], uh, ul)
    kr = kr_ref[...]
    ki = ki_ref[...]
    wk = wk_ref[...]
    yr_hi, yr_lo = _split((xr * kr - xi * ki) * wk)
    yi_hi, yi_lo = _split((xr * ki + xi * kr) * wk)
    acc_ref[...] += _dot3(cth[...], ctl[...], yr_hi, yr_lo) - _dot3(sth[...], stl[...], yi_hi, yi_lo)

    @pl.when(kb == pl.num_programs(2) - 1)
    def _():
        u = u_ref[0]
        o_ref[0] = g_ref[0] * (acc_ref[...] + u * bias_ref[...])


def _gated_long_conv(u_arr, g_arr, kr, ki, k_col, bias, tabs):
    b, l, _ = u_arr.shape
    cb = 256
    ncb = HY_WIDTH // cb
    fp = tabs["c_hi"].shape[0]
    fb = tabs["fb"]
    tab = pl.BlockSpec((fb, l), lambda b, j, k: (k, 0))
    tabt = pl.BlockSpec((l, fb), lambda b, j, k: (0, k))
    spec = pl.BlockSpec((fb, cb), lambda b, j, k: (k, k_col * ncb + j))
    return pl.pallas_call(
        _longconv_kernel, name="dense_long_conv",
        grid=(b, ncb, fp // fb),
        in_specs=[
            pl.BlockSpec((1, l, cb), lambda b, j, k: (b, 0, j)),
            pl.BlockSpec((1, l, cb), lambda b, j, k: (b, 0, j)),
            spec, spec,
            pl.BlockSpec((1, cb), lambda b, j, k: (0, j)),
            pl.BlockSpec((fb, 1), lambda b, j, k: (k, 0)),
            tab, tab, tab, tab, tabt, tabt, tabt, tabt,
        ],
        out_specs=pl.BlockSpec((1, l, cb), lambda b, j, k: (b, 0, j)),
        out_shape=jax.ShapeDtypeStruct((b, l, HY_WIDTH), F32),
        scratch_shapes=[pltpu.VMEM((l, cb), BF16), pltpu.VMEM((l, cb), BF16), pltpu.VMEM((l, cb), F32)],
        compiler_params=_cparams(("parallel", "parallel", "arbitrary"), 56),
    )(u_arr, g_arr, kr, ki, bias.reshape(1, HY_WIDTH), tabs["wk"],
      tabs["c_hi"], tabs["c_lo"], tabs["s_hi"], tabs["s_lo"],
      tabs["ct_hi"], tabs["ct_lo"], tabs["st_hi"], tabs["st_lo"])


FFT_INNER = 64
FFT_COLS = 256
FFT_K2_CHUNK = 8
FFT_CONV_PASSES = 1


def _fft_tables(l):
    n = 2 * l
    n1 = FFT_INNER
    n2 = n // n1
    h = n2 // 2
    ang = lambda idx, mod: (idx % mod).astype(F32) * (2.0 * math.pi / mod)
    k2 = jnp.arange(n2, dtype=jnp.int32)
    m2 = jnp.arange(h, dtype=jnp.int32)
    a1 = ang(k2[:, None] * m2[None, :], n2)
    c1, s1 = jnp.cos(a1), jnp.sin(a1)
    f1 = jnp.block([[c1, s1], [-s1, c1]])
    k1 = jnp.arange(n1, dtype=jnp.int32)
    a2 = ang(k1[None, None, :] * (k1[None, :, None] * n2 + k2[:, None, None]), n)
    c2, s2 = jnp.cos(a2), jnp.sin(a2)
    m = jnp.concatenate([jnp.concatenate([c2, s2], axis=2), jnp.concatenate([-s2, c2], axis=2)], axis=1)
    g = jnp.block([[c1.T, -s1.T], [s1.T, c1.T]]) * (1.0 / n)
    out = dict(n2=n2)
    for name, t in (("f1", f1), ("f1r", f1[:, :h]), ("m", m), ("mt", jnp.swapaxes(m, 1, 2)), ("g", g)):
        out[name + "_hi"], out[name + "_lo"] = _split(t)
    return out


def _mm(w_hi, w_lo, x, passes):
    if passes == 1:
        return _dot(w_hi, x.astype(BF16))
    hi, lo = _split(x)
    return _dot3(w_hi, w_lo, hi, lo)


def _row_block_matmul(w_hi, w_lo, src_ref, dst_ref, count, passes=3):
    m, k = w_hi.shape

    def body(i, carry):
        x = src_ref[pl.ds(pl.multiple_of(i * k, k), k), :]
        dst_ref[pl.ds(pl.multiple_of(i * m, m), m), :] = _mm(w_hi, w_lo, x, passes)
        return carry
    lax.fori_loop(0, count, body, 0, unroll=4)


def _short_conv_rows(z, w_ref, b_ref):
    l = z.shape[0]
    row = lax.broadcasted_iota(jnp.int32, z.shape, 0)
    prev = jnp.where(row == 0, 0.0, pltpu.roll(z, 1, axis=0))
    nxt = jnp.where(row == l - 1, 0.0, pltpu.roll(z, l - 1, axis=0))
    return w_ref[0:1, :] * prev + w_ref[1:2, :] * z + w_ref[2:3, :] * nxt + b_ref[...]


def _fftconv_kernel(u_ref, g_ref, uw_ref, ub_ref, gw_ref, gb_ref, kr_ref, ki_ref, bias_ref,
                    f1h, f1l, mh, ml, vh, vl, gh, gl, o_ref, z_scr, p_scr, q_scr, *, u_conv, passes):
    kc = pl.program_id(2)
    n1 = FFT_INNER
    cc = z_scr.shape[1]
    n2 = z_scr.shape[0] // n1
    h = n2 // 2

    def load_u(s):
        return _short_conv_rows(u_ref[s, 0], uw_ref, ub_ref) if u_conv else u_ref[s, 0]

    @pl.when(kc == 0)
    def _stage1():
        z = [jnp.swapaxes(load_u(s).reshape(h, n1, cc), 0, 1) for s in range(2)]
        z_scr[...] = jnp.concatenate(z, axis=1).reshape(n1 * n2, cc)
        _row_block_matmul(f1h[...], f1l[...], z_scr, p_scr, n1, passes)
        q_scr[...] = jnp.swapaxes(p_scr[...].reshape(n1, 2 * n2, cc), 0, 1).reshape(2 * n2 * n1, cc)

    ks = kr_ref.shape[0]
    for j in range(ks):
        k2 = kc * ks + j
        r0 = pl.multiple_of(k2 * n1, n1)
        i0 = pl.multiple_of((n2 + k2) * n1, n1)
        a = jnp.concatenate([q_scr[pl.ds(r0, n1), :], q_scr[pl.ds(i0, n1), :]], axis=0)
        x = _mm(mh[j], ml[j], a, passes)
        xr, xi = x[:n1], x[n1:]
        kr, ki = kr_ref[j], ki_ref[j]
        y = jnp.concatenate([xr * kr - xi * ki, xr * ki + xi * kr], axis=0)
        b = _mm(vh[j], vl[j], y, passes)
        p_scr[pl.ds(r0, n1), :] = b[:n1]
        p_scr[pl.ds(i0, n1), :] = b[n1:]

    @pl.when(kc == pl.num_programs(2) - 1)
    def _stage1_inv():
        q_scr[...] = jnp.swapaxes(p_scr[...].reshape(2 * n2, n1, cc), 0, 1).reshape(2 * n2 * n1, cc)
        _row_block_matmul(gh[...], gl[...], q_scr, z_scr, n1, passes)
        y3 = z_scr[...].reshape(n1, n2, cc)
        for s in range(2):
            y = jnp.swapaxes(y3[:, s * h:(s + 1) * h, :], 0, 1).reshape(h * n1, cc)
            gate = _short_conv_rows(g_ref[s, 0], gw_ref, gb_ref)
            o_ref[s, 0] = (gate * (y + load_u(s) * bias_ref[...])).astype(o_ref.dtype)


def _fft_gated_long_conv(u, u_col, z_hy, g_col, short_w, short_b, kr, ki, k_col, bias, ft, out_dtype):
    b, l, _ = z_hy.shape
    c = HY_WIDTH
    n1, n2, cc, ks = FFT_INNER, ft["n2"], FFT_COLS, FFT_K2_CHUNK
    p, ncc = b // 2, c // cc
    u_conv = u is None
    if u_conv:
        u = z_hy
    tok = lambda col: pl.BlockSpec((2, 1, l, cc), lambda q, j, k: (0, q, 0, col * ncc + j))
    taps = lambda col: pl.BlockSpec((3, cc), lambda q, j, k: (0, col * ncc + j))
    row = lambda col: pl.BlockSpec((1, cc), lambda q, j, k: (0, col * ncc + j))
    filt = pl.BlockSpec((ks, n1, cc), lambda q, j, k: (k, 0, k_col * ncc + j))
    tab = pl.BlockSpec((ks, 2 * n1, 2 * n1), lambda q, j, k: (k, 0, 0))
    res = lambda a: pl.BlockSpec(a.shape, lambda q, j, k: (0,) * a.ndim)
    sb = short_b.reshape(1, -1)
    y = pl.pallas_call(
        functools.partial(_fftconv_kernel, u_conv=u_conv, passes=FFT_CONV_PASSES), name="fft_long_conv",
        grid=(p, ncc, n2 // ks),
        in_specs=[tok(u_col), tok(g_col), taps(u_col), row(u_col), taps(g_col), row(g_col), filt, filt, row(0),
                  res(ft["f1_hi"]), res(ft["f1_lo"]), tab, tab, tab, tab, res(ft["g_hi"]), res(ft["g_lo"])],
        out_specs=tok(0),
        out_shape=jax.ShapeDtypeStruct((2, p, l, c), out_dtype),
        scratch_shapes=[pltpu.VMEM((n1 * n2, cc), F32), pltpu.VMEM((2 * n2 * n1, cc), F32),
                        pltpu.VMEM((2 * n2 * n1, cc), F32)],
        compiler_params=_cparams(("parallel", "parallel", "arbitrary"), 56),
    )(u.reshape(2, p, l, -1), z_hy.reshape(2, p, l, -1), short_w, sb, short_w, sb, kr, ki, bias.reshape(1, c),
      ft["f1_hi"], ft["f1_lo"], ft["m_hi"], ft["m_lo"], ft["mt_hi"], ft["mt_lo"], ft["g_hi"], ft["g_lo"])
    return y.reshape(b, l, c)


def _fftspec_kernel(a_ref, d_ref, f1h, f1l, mh, ml, kr_ref, ki_ref, z_scr, p_scr, q_scr):
    n1 = FFT_INNER
    cc = z_scr.shape[1]
    h = z_scr.shape[0] // n1
    n2 = 2 * h
    for src_ref, out_ref, lo in ((a_ref, kr_ref, 0), (d_ref, ki_ref, n1)):
        z_scr[...] = jnp.swapaxes(src_ref[0].reshape(h, n1, cc), 0, 1).reshape(n1 * h, cc)
        _row_block_matmul(f1h[...], f1l[...], z_scr, p_scr, n1)
        q_scr[...] = jnp.swapaxes(p_scr[...].reshape(n1, 2 * n2, cc), 0, 1).reshape(2 * n2 * n1, cc)

        def body(k2, carry):
            r0 = pl.multiple_of(k2 * n1, n1)
            i0 = pl.multiple_of((n2 + k2) * n1, n1)
            sh, sl = _split(jnp.concatenate([q_scr[pl.ds(r0, n1), :], q_scr[pl.ds(i0, n1), :]], axis=0))
            out_ref[k2] = _dot3(mh[k2][lo:lo + n1], ml[k2][lo:lo + n1], sh, sl)
            return carry
        lax.fori_loop(0, n2, body, 0, unroll=4)


def _fft_filter_spectrum(ad, ft):
    _, l, n = ad.shape
    n1, n2, cc = FFT_INNER, ft["n2"], FFT_COLS
    h = n2 // 2
    res = lambda a: pl.BlockSpec(a.shape, lambda j: (0,) * a.ndim)
    out = pl.BlockSpec((n2, n1, cc), lambda j: (0, 0, j))
    tabs = [ft["f1r_hi"], ft["f1r_lo"], ft["m_hi"], ft["m_lo"]]
    return pl.pallas_call(
        _fftspec_kernel, name="fft_filter_spectrum",
        grid=(n // cc,),
        in_specs=[pl.BlockSpec((1, l, cc), lambda j: (0, 0, j)), pl.BlockSpec((1, l, cc), lambda j: (1, 0, j))]
                 + [res(t) for t in tabs],
        out_specs=[out, out],
        out_shape=[jax.ShapeDtypeStruct((n2, n1, n), F32)] * 2,
        scratch_shapes=[pltpu.VMEM((n1 * h, cc), F32), pltpu.VMEM((2 * n2 * n1, cc), F32),
                        pltpu.VMEM((2 * n2 * n1, cc), F32)],
        compiler_params=_cparams(("parallel",), 56),
    )(ad, ad, *tabs)


def _softmax_pv(s, v, sink=None):
    m = jnp.max(s, axis=-1, keepdims=True)
    if sink is not None:
        m = jnp.maximum(m, sink)
    p = jnp.exp(s - m)
    den = jnp.sum(p, axis=-1, keepdims=True)
    if sink is not None:
        den = den + jnp.exp(sink - m)
    return _dot(p.astype(BF16), v) / den


def _qk(q, k):
    return lax.dot_general(q, k, (((1,), (1,)), ((), ())), preferred_element_type=F32)


def _half_mask(q_tile, half):
    lane = lax.broadcasted_iota(jnp.int32, q_tile.shape, 1)
    keep = (lane < HEAD_DIM) if half == 0 else (lane >= HEAD_DIM)
    return jnp.where(keep, q_tile, jnp.zeros_like(q_tile))


def _merge_halves(o_even, o_odd):
    lane = lax.broadcasted_iota(jnp.int32, o_even.shape, 1)
    return jnp.where(lane < HEAD_DIM, o_even, o_odd)


def _swa_kernel(sink_ref, q_ref, kp_ref, kc_ref, kn_ref, vp_ref, vc_ref, vn_ref, kx_ref, vx_ref, o_ref):
    n = pl.program_id(1)
    nb = pl.num_programs(1)
    blk = SWA_BLOCK
    g = SWA_HEADS // SWA_KV_HEADS
    n_ctx = kx_ref.shape[1]
    rows = g * blk
    ri = lax.broadcasted_iota(jnp.int32, (rows, 3 * blk + n_ctx), 0) & (blk - 1)
    ci = lax.broadcasted_iota(jnp.int32, (rows, 3 * blk + n_ctx), 1)
    ok_prev = (ci < blk) & (ci >= ri) & (n > 0)
    ok_cur = (ci >= blk) & (ci < 2 * blk)
    ok_next = (ci >= 2 * blk) & (ci < 3 * blk) & (ci - 2 * blk <= ri) & (n < nb - 1)
    ok = ok_prev | ok_cur | ok_next | (ci >= 3 * blk)
    bias = jnp.where(ok, 0.0, NEG_BIG)
    rsel = lax.broadcasted_iota(jnp.int32, (rows, 1), 0) >> (blk.bit_length() - 1)
    outs = [None] * SWA_HEADS
    for kv in range(SWA_KV_HEADS):
        cs = slice(kv * LANES, (kv + 1) * LANES)
        k = jnp.concatenate([kp_ref[0, :, cs], kc_ref[0, :, cs], kn_ref[0, :, cs], kx_ref[0, :, cs]], axis=0)
        v = jnp.concatenate([vp_ref[0, :, cs], vc_ref[0, :, cs], vn_ref[0, :, cs], vx_ref[0, :, cs]], axis=0)
        q_parts = []
        sink = jnp.zeros((rows, 1), F32)
        for gi in range(g):
            h = kv * g + gi
            q_parts.append(_half_mask(q_ref[0, :, (h // 2) * LANES:(h // 2 + 1) * LANES], h % 2))
            sink = jnp.where(rsel == gi, sink_ref[h], sink)
        s = _qk(jnp.concatenate(q_parts, axis=0), k) + bias
        o = _softmax_pv(s, v, sink)
        for gi in range(g):
            outs[kv * g + gi] = o[gi * blk:(gi + 1) * blk]
    for p in range(SWA_HEADS // 2):
        o_ref[0, :, p * LANES:(p + 1) * LANES] = _merge_halves(outs[2 * p], outs[2 * p + 1]).astype(o_ref.dtype)


def _window_gqa(q, k, v, kx, vx, sink):
    b, l, _ = q.shape
    nb = l // SWA_BLOCK
    n_ctx = kx.shape[1]
    kvw = k.shape[2]

    def blk(off):
        return pl.BlockSpec((1, SWA_BLOCK, kvw), lambda b, n: (b, jnp.clip(n + off, 0, nb - 1), 0))

    ctx = pl.BlockSpec((1, n_ctx, kvw), lambda b, n: (b, 0, 0))
    return pl.pallas_call(
        _swa_kernel, name="window_gqa",
        grid=(b, nb),
        in_specs=[
            pl.BlockSpec(memory_space=pltpu.SMEM),
            pl.BlockSpec((1, SWA_BLOCK, q.shape[2]), lambda b, n: (b, n, 0)),
            blk(-1), blk(0), blk(1), blk(-1), blk(0), blk(1), ctx, ctx,
        ],
        out_specs=pl.BlockSpec((1, SWA_BLOCK, q.shape[2]), lambda b, n: (b, n, 0)),
        out_shape=jax.ShapeDtypeStruct(q.shape, BF16),
        compiler_params=_cparams(("parallel", "parallel"), 32),
    )(sink, q, k, k, k, v, v, v, kx, vx)


def _na_bias_kernel(rpb_ref, o_ref):
    off = pl.program_id(0)
    h = pl.program_id(1)
    nd = 2 * NA_COLS - 1
    q = lax.broadcasted_iota(jnp.int32, (GRID_W, LANES), 0)
    lane = lax.broadcasted_iota(jnp.int32, (GRID_W, LANES), 1)
    kc = lane & (GRID_W - 1)
    upper = lane >= GRID_W
    cstart = jnp.clip(q - NA_COLS // 2, 0, GRID_W - NA_COLS)
    valid = (kc >= cstart) & (kc < cstart + NA_COLS)
    dc = jnp.clip(kc - q, -(NA_COLS - 1), NA_COLS - 1) + (NA_COLS - 1)
    for j in range(NA_MAX_ROWS // 2):
        base0 = (h * (2 * NA_MAX_ROWS - 1) + (2 * j - off + NA_MAX_ROWS - 1)) * nd
        base1 = base0 + nd
        t = jnp.zeros((GRID_W, LANES), F32)
        for d in range(nd):
            val = jnp.where(upper, rpb_ref[base1 + d], rpb_ref[base0 + d])
            t = jnp.where(dc == d, val, t)
        o_ref[0, 0, :, j * LANES:(j + 1) * LANES] = jnp.where(valid, t, NEG_BIG)


def _na_bias_table(rpb):
    return pl.pallas_call(
        _na_bias_kernel, name="na_bias_table",
        grid=(NA_MAX_ROWS, NA_HEADS),
        in_specs=[pl.BlockSpec(memory_space=pltpu.SMEM)],
        out_specs=pl.BlockSpec((1, 1, GRID_W, NA_MAX_ROWS * GRID_W), lambda o, h: (o, h, 0, 0)),
        out_shape=jax.ShapeDtypeStruct((NA_MAX_ROWS, NA_HEADS, GRID_W, NA_MAX_ROWS * GRID_W), F32),
        compiler_params=_cparams(("arbitrary", "arbitrary"), 16),
    )(rpb.reshape(-1))


NA_ROWS_PER_STEP = 4


def _na_kernel(q_ref, k_ref, v_ref, kx_ref, vx_ref, bias_ref, o_ref, *, rows):
    wr = NA_MAX_ROWS
    nloc = wr * GRID_W
    for i in range(NA_ROWS_PER_STEP):
        r = pl.program_id(1) * NA_ROWS_PER_STEP + i
        first = jnp.clip(r - wr // 2, 0, rows - wr)
        start = pl.multiple_of(first * GRID_W, GRID_W)
        off = r - first
        qr = slice(i * GRID_W, (i + 1) * GRID_W)
        for p in range(NA_HEADS // 2):
            cs = slice(p * LANES, (p + 1) * LANES)
            q_tile = q_ref[0, qr, cs]
            q2 = jnp.concatenate([_half_mask(q_tile, 0), _half_mask(q_tile, 1)], axis=0)
            k = jnp.concatenate([k_ref[0, pl.ds(start, nloc), cs], kx_ref[0, :, cs]], axis=0)
            v = jnp.concatenate([v_ref[0, pl.ds(start, nloc), cs], vx_ref[0, :, cs]], axis=0)
            s = _qk(q2, k)
            bias = jnp.concatenate([bias_ref[off, 2 * p], bias_ref[off, 2 * p + 1]], axis=0)
            s = jnp.concatenate([s[:, :nloc] + bias, s[:, nloc:]], axis=1)
            o = _softmax_pv(s, v)
            o_ref[0, qr, cs] = _merge_halves(o[:GRID_W], o[GRID_W:]).astype(o_ref.dtype)


def _neighbourhood_attention(q, k, v, kx, vx, bias_tab):
    b, l, w = q.shape
    rows = l // GRID_W
    n_ctx = kx.shape[1]
    tq = NA_ROWS_PER_STEP * GRID_W
    full = pl.BlockSpec((1, l, w), lambda b, r: (b, 0, 0))
    ctx = pl.BlockSpec((1, n_ctx, w), lambda b, r: (b, 0, 0))
    return pl.pallas_call(
        functools.partial(_na_kernel, rows=rows), name="neighbourhood_attention",
        grid=(b, rows // NA_ROWS_PER_STEP),
        in_specs=[pl.BlockSpec((1, tq, w), lambda b, r: (b, r, 0)), full, full, ctx, ctx,
                  _resident(bias_tab.shape)],
        out_specs=pl.BlockSpec((1, tq, w), lambda b, r: (b, r, 0)),
        out_shape=jax.ShapeDtypeStruct(q.shape, BF16),
        compiler_params=_cparams(("parallel", "arbitrary"), 48),
    )(q, k, v, kx, vx, bias_tab)


def _ctx_attn_kernel(sink_ref, qs_ref, ks_ref, vs_ref, qn_ref, kn_ref, vn_ref, os_ref, on_ref):
    n = qs_ref.shape[1]
    g = SWA_HEADS // SWA_KV_HEADS
    rsel = lax.broadcasted_iota(jnp.int32, (g * n, 1), 0) >> (n.bit_length() - 1)
    outs = [None] * SWA_HEADS
    for kv in range(SWA_KV_HEADS):
        cs = slice(kv * LANES, (kv + 1) * LANES)
        q_parts = []
        sink = jnp.zeros((g * n, 1), F32)
        for gi in range(g):
            h = kv * g + gi
            q_parts.append(_half_mask(qs_ref[0, :, (h // 2) * LANES:(h // 2 + 1) * LANES], h % 2))
            sink = jnp.where(rsel == gi, sink_ref[h], sink)
        o = _softmax_pv(_qk(jnp.concatenate(q_parts, axis=0), ks_ref[0, :, cs]), vs_ref[0, :, cs], sink)
        for gi in range(g):
            outs[kv * g + gi] = o[gi * n:(gi + 1) * n]
    for p in range(SWA_HEADS // 2):
        os_ref[0, :, p * LANES:(p + 1) * LANES] = _merge_halves(outs[2 * p], outs[2 * p + 1]).astype(os_ref.dtype)
    for p in range(NA_HEADS // 2):
        cs = slice(p * LANES, (p + 1) * LANES)
        q_tile = qn_ref[0, :, cs]
        q2 = jnp.concatenate([_half_mask(q_tile, 0), _half_mask(q_tile, 1)], axis=0)
        o = _softmax_pv(_qk(q2, kn_ref[0, :, cs]), vn_ref[0, :, cs])
        on_ref[0, :, cs] = _merge_halves(o[:n], o[n:]).astype(on_ref.dtype)


def _context_attention(qs, ks, vs, qn, kn, vn, sink):
    b, n, w = qs.shape
    spec = lambda a: pl.BlockSpec((1,) + a.shape[1:], lambda b: (b, 0, 0))
    return pl.pallas_call(
        _ctx_attn_kernel, name="context_attention",
        grid=(b,),
        in_specs=[pl.BlockSpec(memory_space=pltpu.SMEM)] + [spec(a) for a in (qs, ks, vs, qn, kn, vn)],
        out_specs=[spec(qs), spec(qn)],
        out_shape=[jax.ShapeDtypeStruct(qs.shape, BF16), jax.ShapeDtypeStruct(qn.shape, BF16)],
        compiler_params=_cparams(("parallel",), 32),
    )(sink, qs, ks, vs, qn, kn, vn)


def _merge_kernel(x_ref, gt_ref, yh_ref, ys_ref, yn_ref, zg_ref, wb_ref, wo_ref, o_ref):
    d = D_MODEL
    ys = (yh_ref[0].astype(BF16), ys_ref[0], yn_ref[0])
    m = None
    for n in range(3):
        proj = _dot(ys[n], wb_ref[n])
        term = zg_ref[0, :, n * d:(n + 1) * d].astype(F32) * proj
        m = term if m is None else m + term
    o_ref[0] = x_ref[0] + gt_ref[0] * _dot(m.astype(BF16), wo_ref[...])


def _merge_branches(x, gate, y_hy, y_swa, y_na, z_gate, w_branch, w_out):
    b, l, d = x.shape
    tm = min(512, l)
    tok = lambda n: pl.BlockSpec((1, tm, n), lambda b, t: (b, t, 0))
    return pl.pallas_call(
        _merge_kernel, name="merge_branches",
        grid=(b, l // tm),
        in_specs=[tok(d), _mod_spec(gate), tok(HY_WIDTH), tok(HY_WIDTH), tok(HY_WIDTH), tok(3 * d),
                  _resident(w_branch.shape), _resident(w_out.shape)],
        out_specs=tok(d),
        out_shape=jax.ShapeDtypeStruct(x.shape, F32),
        compiler_params=_cparams(("parallel", "parallel"), 48),
    )(x, gate, y_hy, y_swa, y_na, z_gate, w_branch, w_out)


def _dft_tables(l):
    n = 2 * l
    fb = 256 if l >= 1024 else LANES * ((l + 1 + LANES - 1) // LANES)
    fp = fb * ((l + 1 + fb - 1) // fb)
    k = jnp.arange(fp, dtype=jnp.int32)[:, None]
    t = jnp.arange(l, dtype=jnp.int32)[None, :]
    ang = ((k * t) % n).astype(F32) * (2.0 * math.pi / n)
    live = k <= l
    c = jnp.where(live, jnp.cos(ang), 0.0)
    s = jnp.where(live, jnp.sin(ang), 0.0)
    c_hi, c_lo = _split(c)
    s_hi, s_lo = _split(s)
    wk = jnp.where((k == 0) | (k == l), 1.0, 2.0) * jnp.where(live, 1.0 / n, 0.0)
    return dict(fb=fb, c_hi=c_hi, c_lo=c_lo, s_hi=s_hi, s_lo=s_lo,
                ct_hi=c_hi.T, ct_lo=c_lo.T, st_hi=s_hi.T, st_lo=s_lo.T, wk=wk.astype(F32))


def _filter_features(l):
    t = jnp.linspace(0.0, 1.0, l, dtype=F32)[:, None]
    w = (2.0 * math.pi / l) * jnp.arange(l, dtype=F32)[:, None]
    f = jnp.linspace(1e-4, HY_BANDS - 1, HY_BANDS, dtype=F32)[None, :]
    z = jnp.concatenate([t, jnp.cos(f * w), -jnp.sin(f * w)], axis=-1)
    return jnp.pad(z, ((0, 0), (0, LANES - HY_EMB)))


def _decay_rates():
    max_decay = math.log(HY_TARGET) / HY_FAST_DECAY
    min_decay = math.log(HY_TARGET) / HY_SLOW_DECAY
    return jnp.abs(jnp.linspace(min_decay, max_decay, HY_WIDTH, dtype=F32))[None, :]


def _rope_tables(l):
    pos = jnp.arange(l)
    row = (pos // GRID_W).astype(F32)
    col = (pos % GRID_W).astype(F32)
    half = HEAD_DIM // 2
    inv = 1.0 / (ROPE_BASE ** (jnp.arange(0, half, 2, dtype=F32) / half))
    ar = row[:, None] * inv[None, :]
    ac = col[:, None] * inv[None, :]
    cos = jnp.concatenate([jnp.cos(ar), jnp.cos(ar), jnp.cos(ac), jnp.cos(ac)], axis=-1)
    sin = jnp.concatenate([-jnp.sin(ar), jnp.sin(ar), -jnp.sin(ac), jnp.sin(ac)], axis=-1)
    return jnp.tile(cos, (1, 2)), jnp.tile(sin, (1, 2))


def _pad_to(a, shape):
    return jnp.pad(a, [(0, s - d) for d, s in zip(a.shape, shape)])


def _use_fft(bsz, length):
    return bsz % 2 == 0 and length % (8 * FFT_INNER) == 0


def _conv_tables(bsz, length):
    return _fft_tables(length) if _use_fft(bsz, length) else _dft_tables(length)


def _hyena_branch(z_hy, short_w, short_b, feat, fparams, deltas, bias, tabs):
    ad = _hyena_filter_sums(feat, fparams, deltas)
    if "n2" in tabs:
        kr, ki = _fft_filter_spectrum(ad, tabs)
        y1 = _fft_gated_long_conv(None, 0, z_hy, 1, short_w, short_b, kr, ki, 0, bias[0], tabs, F32)
        return _fft_gated_long_conv(y1, 0, z_hy, 2, short_w, short_b, kr, ki, 1, bias[1], tabs, BF16)
    v, x1, x2 = _short_conv3(z_hy, short_w, short_b)
    kr, ki = _filter_spectrum(ad, tabs)
    conv = functools.partial(_gated_long_conv, tabs=tabs)
    y1 = conv(v, x1, kr, ki, 0, bias[0])
    return conv(y1, x2, kr, ki, 1, bias[1])


def kernel(x, c, ctx, c_ctx, w_ada, b_ada, norm_g, ffn_w_gate, ffn_w_up, ffn_w_down,
           w_in, hy_short_w, hy_short_b, hy_pe_w0, hy_pe_b0, hy_pe_w1, hy_pe_b1,
           hy_pe_w2, hy_pe_b2, hy_pe_wout, hy_sin_freq, hy_bias,
           swa_q_gain, swa_k_gain, swa_sink, na_q_gain, na_k_gain, na_rpb,
           w_branch, w_out):
    bsz, seq, d = x.shape
    n_ctx = ctx.shape[1]
    depth = w_ada.shape[0]

    c16 = _pad_to(jnp.concatenate([c, c_ctx[None, :]], axis=0), (MOD_ROWS, d))
    mods = _adaln_mods(c16, w_ada, b_ada).reshape(depth, MOD_ROWS, N_MOD, 1, d)

    tabs_x = _conv_tables(bsz, seq)
    tabs_c = _conv_tables(bsz, n_ctx)
    feat_x = _filter_features(seq)
    feat_c = _filter_features(n_ctx)
    deltas = _decay_rates()
    cos_x, sin_x = _rope_tables(seq)
    cos_c = jnp.ones((n_ctx, LANES), F32)
    sin_c = jnp.zeros((n_ctx, LANES), F32)
    eye = jnp.arange(MXU_DIM) // HEAD_DIM
    bd = (eye[:, None] == eye[None, :]).astype(BF16)

    wg = ffn_w_gate.astype(BF16)
    wu = ffn_w_up.astype(BF16)
    wd = ffn_w_down.astype(BF16)
    wb = w_branch.astype(BF16)
    wo = w_out.astype(BF16)
    win = w_in.astype(BF16)
    kv = win[:, :, _OFF_SWA + _SWA_Q:_OFF_NA].reshape(depth, d, 2, SWA_KV_HEADS, 1, HEAD_DIM)
    win_kv = jnp.broadcast_to(kv, (depth, d, 2, SWA_KV_HEADS, 2, HEAD_DIM)).reshape(depth, d, 4 * _SWA_KV)

    xc = ctx
    for i in range(depth):
        last = i == depth - 1
        mx = lambda j: mods[i, :bsz, j]
        mc = lambda j: mods[i, bsz:bsz + 1, j]
        tile2 = lambda g: jnp.tile(g, 2 * MXU_DIM // LANES)
        head_gains = jnp.stack([tile2(swa_q_gain[i]), tile2(swa_k_gain[i]), tile2(na_q_gain[i]), tile2(na_k_gain[i])])
        fparams = (_pad_to(hy_pe_w0[i], (LANES, LANES)), _pad_to(hy_pe_b0[i][None], (1, LANES)),
                   _pad_to(hy_pe_w1[i], (LANES, LANES)), _pad_to(hy_pe_b1[i][None], (1, LANES)),
                   _pad_to(hy_pe_w2[i], (LANES, LANES)), _pad_to(hy_pe_b2[i][None], (1, LANES)),
                   _pad_to(hy_pe_wout[i], (LANES, 4 * HY_WIDTH)), _pad_to(hy_sin_freq[i][None], (1, LANES)))

        x = _ffn_half_step(x, mx(0), mx(1), mx(2), norm_g[i, 0], wg[i, 0], wu[i, 0], wd[i, 0])
        xc = _ffn_half_step(xc, mc(0), mc(1), mc(2), norm_g[i, 0], wg[i, 0], wu[i, 0], wd[i, 0])

        z_hy, q_s, k_s, v_s, q_n, k_n, v_n, z_gate = _in_projection(
            x, mx(3), mx(4), norm_g[i, 1], cos_x, sin_x, head_gains, bd, win[i], win_kv[i])
        zc_hy, qc_s, kc_s, vc_s, qc_n, kc_n, vc_n, zc_gate = _in_projection(
            xc, mc(3), mc(4), norm_g[i, 1], cos_c, sin_c, head_gains, bd, win[i], win_kv[i])

        y_hy = _hyena_branch(z_hy, hy_short_w[i], hy_short_b[i], feat_x, fparams, deltas, hy_bias[i], tabs_x)
        y_swa = _window_gqa(q_s, k_s, v_s, kc_s, vc_s, swa_sink[i])
        y_na = _neighbourhood_attention(q_n, k_n, v_n, kc_n, vc_n, _na_bias_table(na_rpb[i]))
        x = _merge_branches(x, mx(5), y_hy, y_swa, y_na, z_gate, wb[i], wo[i])

        if not last:
            yc_hy = _hyena_branch(zc_hy, hy_short_w[i], hy_short_b[i], feat_c, fparams, deltas, hy_bias[i], tabs_c)
            yc_swa, yc_na = _context_attention(qc_s, kc_s, vc_s, qc_n, kc_n, vc_n, swa_sink[i])
            xc = _merge_branches(xc, mc(5), yc_hy, yc_swa, yc_na, zc_gate, wb[i], wo[i])

        x = _ffn_half_step(x, mx(6), mx(7), mx(8), norm_g[i, 2], wg[i, 1], wu[i, 1], wd[i, 1])
        if not last:
            xc = _ffn_half_step(xc, mc(6), mc(7), mc(8), norm_g[i, 2], wg[i, 1], wu[i, 1], wd[i, 1])
    return x
```

```python
import functools
import math

import jax
import jax.numpy as jnp
from jax import lax
from jax.experimental import pallas as pl
from jax.experimental.pallas import tpu as pltpu

F32 = jnp.float32
BF16 = jnp.bfloat16

D_MODEL = 1024
DEPTH = 4
GRID_W = 64
HEAD_DIM = 64
N_MOD = 9
RMS_EPS = 1e-6

HY_WIDTH = D_MODEL // 2
HY_EMB = 33
HY_BANDS = (HY_EMB - 1) // 2
HY_FFN = 64
HY_FAST_DECAY = 0.3
HY_SLOW_DECAY = 1.5
HY_TARGET = 1e-2

SWA_HEADS = 8
SWA_KV_HEADS = 2
SWA_WINDOW = 128
SWA_BLOCK = 128
ROPE_BASE = 10000.0

NA_HEADS = 8
NA_MAX_ROWS = 8
NA_COLS = 16
NA_COL_BLOCK = 16

FFN_HIDDEN = 256 * ((8 * D_MODEL // 3 + 255) // 256)

LANES = 128
MXU_DIM = 256
MOD_ROWS = 16
NEG_BIG = -1e30


def _cparams(sem, vmem_mb):
    return pltpu.CompilerParams(dimension_semantics=sem, vmem_limit_bytes=vmem_mb * 1024 * 1024)


def _resident(shape):
    nd = len(shape)
    return pl.BlockSpec(shape, lambda *_: (0,) * nd, pipeline_mode=pl.Buffered(1))


def _split(x):
    hi = x.astype(BF16)
    lo = (x - hi.astype(F32)).astype(BF16)
    return hi, lo


def _dot(a, b):
    return jnp.dot(a, b, preferred_element_type=F32)


def _dot3(a_hi, a_lo, b_hi, b_lo):
    return _dot(a_hi, b_hi) + _dot(a_lo, b_hi) + _dot(a_hi, b_lo)


def _dot3f(a, b):
    a_hi, a_lo = _split(a)
    b_hi, b_lo = _split(b)
    return _dot3(a_hi, a_lo, b_hi, b_lo)


def _rms_mod(x, gain, shift, scale):
    ms = jnp.mean(x * x, axis=-1, keepdims=True)
    y = x * lax.rsqrt(ms + RMS_EPS) * gain
    return y * (1.0 + scale) + shift


def _head_rms(x, bd, gain):
    outs = []
    for j in range(x.shape[1] // MXU_DIM):
        xc = x[:, j * MXU_DIM:(j + 1) * MXU_DIM]
        hi, lo = _split(xc * xc)
        ssum = _dot(hi, bd) + _dot(lo, bd)
        outs.append(xc * lax.rsqrt(ssum * (1.0 / HEAD_DIM) + RMS_EPS) * gain)
    return outs[0] if len(outs) == 1 else jnp.concatenate(outs, axis=1)


def _rope(x, cos, sin):
    lane = lax.broadcasted_iota(jnp.int32, (x.shape[0], LANES), 1)
    first = (lane & 31) < 16
    outs = []
    for j in range(x.shape[1] // LANES):
        xc = x[:, j * LANES:(j + 1) * LANES]
        partner = jnp.where(first, pltpu.roll(xc, LANES - 16, axis=1), pltpu.roll(xc, 16, axis=1))
        outs.append(xc * cos + partner * sin)
    return outs[0] if len(outs) == 1 else jnp.concatenate(outs, axis=1)


def _mods_kernel(c_ref, w_ref, b_ref, o_ref):
    c = c_ref[...]
    a = c * jax.nn.sigmoid(c)
    o_ref[0] = _dot3f(a, w_ref[0]) + b_ref[0]


def _adaln_mods(c16, w_ada, b_ada):
    depth, d, n = w_ada.shape
    nb = 1152
    return pl.pallas_call(
        _mods_kernel, name="adaln_mods",
        grid=(depth, n // nb),
        in_specs=[
            pl.BlockSpec((MOD_ROWS, d), lambda i, j: (0, 0)),
            pl.BlockSpec((1, d, nb), lambda i, j: (i, 0, j)),
            pl.BlockSpec((1, 1, nb), lambda i, j: (i, 0, j)),
        ],
        out_specs=pl.BlockSpec((1, MOD_ROWS, nb), lambda i, j: (i, 0, j)),
        out_shape=jax.ShapeDtypeStruct((depth, MOD_ROWS, n), F32),
        compiler_params=_cparams(("arbitrary", "arbitrary"), 48),
    )(c16, w_ada, b_ada.reshape(depth, 1, n))


def _ffn_kernel(x_ref, sh_ref, sc_ref, gt_ref, g_ref, wg_ref, wu_ref, wd_ref, o_ref, *, fc):
    x = x_ref[0]
    hb = _rms_mod(x, g_ref[...], sh_ref[0], sc_ref[0]).astype(BF16)
    acc = None
    for f0 in range(0, FFN_HIDDEN, fc):
        g = _dot(hb, wg_ref[:, f0:f0 + fc])
        u = _dot(hb, wu_ref[:, f0:f0 + fc])
        a = (g * jax.nn.sigmoid(g) * u).astype(BF16)
        d = _dot(a, wd_ref[f0:f0 + fc, :])
        acc = d if acc is None else acc + d
    o_ref[0] = x + 0.5 * gt_ref[0] * acc


def _mod_spec(mod):
    if mod.shape[0] == 1:
        return pl.BlockSpec((1, 1, D_MODEL), lambda b, t: (0, 0, 0))
    return pl.BlockSpec((1, 1, D_MODEL), lambda b, t: (b, 0, 0))


def _ffn_half_step(x, shift, scale, gate, gain, wg, wu, wd):
    b, l, d = x.shape
    tm = min(512, l)
    return pl.pallas_call(
        functools.partial(_ffn_kernel, fc=FFN_HIDDEN // 2), name="ffn_half_step",
        grid=(b, l // tm),
        in_specs=[
            pl.BlockSpec((1, tm, d), lambda b, t: (b, t, 0)),
            _mod_spec(shift), _mod_spec(scale), _mod_spec(gate),
            _resident((1, d)),
            _resident(wg.shape), _resident(wu.shape), _resident(wd.shape),
        ],
        out_specs=pl.BlockSpec((1, tm, d), lambda b, t: (b, t, 0)),
        out_shape=jax.ShapeDtypeStruct(x.shape, F32),
        compiler_params=_cparams(("parallel", "parallel"), 56),
    )(x, shift, scale, gate, gain.reshape(1, d), wg, wu, wd)


_HY_COLS = 3 * HY_WIDTH
_SWA_Q = SWA_HEADS * HEAD_DIM
_SWA_KV = SWA_KV_HEADS * HEAD_DIM
_NA_W = NA_HEADS * HEAD_DIM
_OFF_SWA = _HY_COLS
_OFF_NA = _OFF_SWA + _SWA_Q + 2 * _SWA_KV
_OFF_GATE = _OFF_NA + 3 * _NA_W


def _inproj_kernel(x_ref, sh_ref, sc_ref, g_ref, cos_ref, sin_ref, hg_ref, bd_ref, w_ref, wkv_ref,
                   zhy_ref, qs_ref, ks_ref, vs_ref, qn_ref, kn_ref, vn_ref, gt_ref):
    hb = _rms_mod(x_ref[0], g_ref[...], sh_ref[0], sc_ref[0]).astype(BF16)
    cos = cos_ref[...]
    sin = sin_ref[...]
    bd = bd_ref[...]
    scale = HEAD_DIM ** -0.5
    proj = lambda lo, n: _dot(hb, w_ref[:, lo:lo + n])
    kvw = 2 * _SWA_KV
    qs = proj(_OFF_SWA, _SWA_Q)
    ks = _dot(hb, wkv_ref[:, :kvw])
    qn = proj(_OFF_NA, _NA_W)
    kn = proj(_OFF_NA + _NA_W, _NA_W)
    vs_ref[0] = _dot(hb, wkv_ref[:, kvw:]).astype(BF16)
    vn_ref[0] = proj(_OFF_NA + 2 * _NA_W, _NA_W).astype(BF16)
    zhy_ref[0] = proj(0, _HY_COLS)
    gt_ref[0] = jax.nn.sigmoid(proj(_OFF_GATE, 3 * D_MODEL)).astype(BF16)
    qs_ref[0] = (_rope(_head_rms(qs, bd, hg_ref[0:1, :]), cos, sin) * scale).astype(BF16)
    ks_ref[0] = _rope(_head_rms(ks, bd, hg_ref[1:2, :]), cos, sin).astype(BF16)
    qn_ref[0] = (_head_rms(qn, bd, hg_ref[2:3, :]) * scale).astype(BF16)
    kn_ref[0] = _head_rms(kn, bd, hg_ref[3:4, :]).astype(BF16)


def _in_projection(x, shift, scale, gain, cos, sin, head_gains, bd, w, w_kv_dup):
    b, l, d = x.shape
    tm = min(512, l)
    kvw = 2 * _SWA_KV
    widths = [_HY_COLS, _SWA_Q, kvw, kvw, _NA_W, _NA_W, _NA_W, 3 * d]
    dtypes = [F32] + [BF16] * 7
    return pl.pallas_call(
        _inproj_kernel, name="in_projection",
        grid=(b, l // tm),
        in_specs=[
            pl.BlockSpec((1, tm, d), lambda b, t: (b, t, 0)),
            _mod_spec(shift), _mod_spec(scale),
            _resident((1, d)),
            pl.BlockSpec((tm, LANES), lambda b, t: (t, 0)),
            pl.BlockSpec((tm, LANES), lambda b, t: (t, 0)),
            _resident(head_gains.shape), _resident(bd.shape), _resident(w.shape), _resident(w_kv_dup.shape),
        ],
        out_specs=[pl.BlockSpec((1, tm, n), lambda b, t: (b, t, 0)) for n in widths],
        out_shape=[jax.ShapeDtypeStruct((b, l, n), dt) for n, dt in zip(widths, dtypes)],
        compiler_params=_cparams(("parallel", "parallel"), 56),
    )(x, shift, scale, gain.reshape(1, d), cos, sin, head_gains, bd, w, w_kv_dup)


def _shortconv_kernel(z0_ref, z1_ref, z2_ref, w_ref, b_ref, o0_ref, o1_ref, o2_ref):
    l = z0_ref.shape[1]
    row = lax.broadcasted_iota(jnp.int32, z0_ref.shape[1:], 0)
    for g, (z_ref, o_ref) in enumerate(((z0_ref, o0_ref), (z1_ref, o1_ref), (z2_ref, o2_ref))):
        z = z_ref[0]
        prev = jnp.where(row == 0, 0.0, pltpu.roll(z, 1, axis=0))
        nxt = jnp.where(row == l - 1, 0.0, pltpu.roll(z, l - 1, axis=0))
        o_ref[0] = w_ref[g, 0:1, :] * prev + w_ref[g, 1:2, :] * z + w_ref[g, 2:3, :] * nxt + b_ref[g]


def _short_conv3(z, w, bias):
    b, l, n = z.shape
    cb = 256
    ncb = HY_WIDTH // cb
    wg = w.reshape(3, 3, HY_WIDTH).transpose(1, 0, 2)
    zin = lambda g: pl.BlockSpec((1, l, cb), lambda b, j: (b, 0, g * ncb + j))
    out = pl.BlockSpec((1, l, cb), lambda b, j: (b, 0, j))
    return pl.pallas_call(
        _shortconv_kernel, name="short_conv3",
        grid=(b, ncb),
        in_specs=[zin(0), zin(1), zin(2),
                  pl.BlockSpec((3, 3, cb), lambda b, j: (0, 0, j)),
                  pl.BlockSpec((3, 1, cb), lambda b, j: (0, 0, j))],
        out_specs=[out, out, out],
        out_shape=[jax.ShapeDtypeStruct((b, l, HY_WIDTH), F32)] * 3,
        compiler_params=_cparams(("parallel", "parallel"), 48),
    )(z, z, z, wg, bias.reshape(3, 1, HY_WIDTH))


def _filter_kernel(z_ref, w0, b0, w1, b1, w2, b2, wo, fr_ref, dl_ref, o_ref):
    z = z_ref[...]
    fr = fr_ref[...]
    a = jnp.sin(fr * (_dot3f(z, w0[...]) + b0[...]))
    a = jnp.sin(fr * (_dot3f(a, w1[...]) + b1[...]))
    a = jnp.sin(fr * (_dot3f(a, w2[...]) + b2[...]))
    hh = _dot3f(a, wo[...])
    t = z[:, 0:1]
    win = jnp.exp(-t * dl_ref[...])
    row = lax.broadcasted_iota(jnp.int32, win.shape, 0) + pl.program_id(0) * z.shape[0]
    w = HY_WIDTH
    for o in range(2):
        hp = hh[:, (2 * o) * w:(2 * o + 1) * w] * win
        hn = jnp.where(row == 0, 0.0, hh[:, (2 * o + 1) * w:(2 * o + 2) * w] * win)
        o_ref[0, :, o * w:(o + 1) * w] = hp + hn
        o_ref[1, :, o * w:(o + 1) * w] = hp - hn


def _hyena_filter_sums(zfeat, fp, deltas):
    l = zfeat.shape[0]
    tl = min(256, l)
    w0, b0, w1, b1, w2, b2, wo, fr = fp
    n = 2 * HY_WIDTH
    consts = [w0, b0, w1, b1, w2, b2, wo, fr, deltas]
    return pl.pallas_call(
        _filter_kernel, name="hyena_filter",
        grid=(l // tl,),
        in_specs=[pl.BlockSpec((tl, LANES), lambda t: (t, 0))] + [_resident(c.shape) for c in consts],
        out_specs=pl.BlockSpec((2, tl, n), lambda t: (0, t, 0)),
        out_shape=jax.ShapeDtypeStruct((2, l, n), F32),
        compiler_params=_cparams(("parallel",), 32),
    )(zfeat, *consts)


def _spectrum_kernel(a_ref, d_ref, ch, cl, sh, sl, kr_ref, ki_ref):
    a_hi, a_lo = _split(a_ref[0])
    d_hi, d_lo = _split(d_ref[0])
    kr_ref[...] = _dot3(ch[...], cl[...], a_hi, a_lo)
    ki_ref[...] = -_dot3(sh[...], sl[...], d_hi, d_lo)


def _filter_spectrum(ad, tabs):
    _, l, n = ad.shape
    fp = tabs["c_hi"].shape[0]
    fb = tabs["fb"]
    cb = 256
    tab = pl.BlockSpec((fb, l), lambda j, k: (k, 0))
    out = pl.BlockSpec((fb, cb), lambda j, k: (k, j))
    return pl.pallas_call(
        _spectrum_kernel, name="dense_spectrum",
        grid=(n // cb, fp // fb),
        in_specs=[pl.BlockSpec((1, l, cb), lambda j, k: (0, 0, j)),
                  pl.BlockSpec((1, l, cb), lambda j, k: (1, 0, j)), tab, tab, tab, tab],
        out_specs=[out, out],
        out_shape=[jax.ShapeDtypeStruct((fp, n), F32)] * 2,
        compiler_params=_cparams(("parallel", "parallel"), 48),
    )(ad, ad, tabs["c_hi"], tabs["c_lo"], tabs["s_hi"], tabs["s_lo"])


def _longconv_kernel(u_ref, g_ref, kr_ref, ki_ref, bias_ref, wk_ref, ch, cl, sh, sl, cth, ctl, sth, stl,
                     o_ref, uh_ref, ul_ref, acc_ref):
    kb = pl.program_id(2)

    @pl.when(kb == 0)
    def _():
        hi, lo = _split(u_ref[0])
        uh_ref[...] = hi
        ul_ref[...] = lo
        acc_ref[...] = jnp.zeros_like(acc_ref)

    uh = uh_ref[...]
    ul = ul_ref[...]
    xr = _dot3(ch[...], cl[...], uh, ul)
    xi = -_dot3(sh[...], sl[...], uh, ul)
    kr = kr_ref[...]
    ki = ki_ref[...]
    wk = wk_ref[...]
    yr_hi, yr_lo = _split((xr * kr - xi * ki) * wk)
    yi_hi, yi_lo = _split((xr * ki + xi * kr) * wk)
    acc_ref[...] += _dot3(cth[...], ctl[...], yr_hi, yr_lo) - _dot3(sth[...], stl[...], yi_hi, yi_lo)

    @pl.when(kb == pl.num_programs(2) - 1)
    def _():
        u = u_ref[0]
        o_ref[0] = g_ref[0] * (acc_ref[...] + u * bias_ref[...])


def _gated_long_conv(u_arr, g_arr, kr, ki, k_col, bias, tabs):
    b, l, _ = u_arr.shape
    cb = 256
    ncb = HY_WIDTH // cb
    fp = tabs["c_hi"].shape[0]
    fb = tabs["fb"]
    tab = pl.BlockSpec((fb, l), lambda b, j, k: (k, 0))
    tabt = pl.BlockSpec((l, fb), lambda b, j, k: (0, k))
    spec = pl.BlockSpec((fb, cb), lambda b, j, k: (k, k_col * ncb + j))
    return pl.pallas_call(
        _longconv_kernel, name="dense_long_conv",
        grid=(b, ncb, fp // fb),
        in_specs=[
            pl.BlockSpec((1, l, cb), lambda b, j, k: (b, 0, j)),
            pl.BlockSpec((1, l, cb), lambda b, j, k: (b, 0, j)),
            spec, spec,
            pl.BlockSpec((1, cb), lambda b, j, k: (0, j)),
            pl.BlockSpec((fb, 1), lambda b, j, k: (k, 0)),
            tab, tab, tab, tab, tabt, tabt, tabt, tabt,
        ],
        out_specs=pl.BlockSpec((1, l, cb), lambda b, j, k: (b, 0, j)),
        out_shape=jax.ShapeDtypeStruct((b, l, HY_WIDTH), F32),
        scratch_shapes=[pltpu.VMEM((l, cb), BF16), pltpu.VMEM((l, cb), BF16), pltpu.VMEM((l, cb), F32)],
        compiler_params=_cparams(("parallel", "parallel", "arbitrary"), 56),
    )(u_arr, g_arr, kr, ki, bias.reshape(1, HY_WIDTH), tabs["wk"],
      tabs["c_hi"], tabs["c_lo"], tabs["s_hi"], tabs["s_lo"],
      tabs["ct_hi"], tabs["ct_lo"], tabs["st_hi"], tabs["st_lo"])


FFT_INNER = 64
FFT_COLS = 256
FFT_K2_CHUNK = 8
FFT_CONV_PASSES = 1


def _fft_tables(l):
    n = 2 * l
    n1 = FFT_INNER
    n2 = n // n1
    h = n2 // 2
    ang = lambda idx, mod: (idx % mod).astype(F32) * (2.0 * math.pi / mod)
    k2 = jnp.arange(n2, dtype=jnp.int32)
    m2 = jnp.arange(h, dtype=jnp.int32)
    a1 = ang(k2[:, None] * m2[None, :], n2)
    c1, s1 = jnp.cos(a1), jnp.sin(a1)
    f1 = jnp.block([[c1, s1], [-s1, c1]])
    k1 = jnp.arange(n1, dtype=jnp.int32)
    a2 = ang(k1[None, None, :] * (k1[None, :, None] * n2 + k2[:, None, None]), n)
    c2, s2 = jnp.cos(a2), jnp.sin(a2)
    m = jnp.concatenate([jnp.concatenate([c2, s2], axis=2), jnp.concatenate([-s2, c2], axis=2)], axis=1)
    g = jnp.block([[c1.T, -s1.T], [s1.T, c1.T]]) * (1.0 / n)
    out = dict(n2=n2)
    for name, t in (("f1", f1), ("f1r", f1[:, :h]), ("m", m), ("mt", jnp.swapaxes(m, 1, 2)), ("g", g)):
        out[name + "_hi"], out[name + "_lo"] = _split(t)
    return out


def _mm(w_hi, w_lo, x, passes):
    if passes == 1:
        return _dot(w_hi, x.astype(BF16))
    hi, lo = _split(x)
    return _dot3(w_hi, w_lo, hi, lo)


def _row_block_matmul(w_hi, w_lo, src_ref, dst_ref, count, passes=3):
    m, k = w_hi.shape

    def body(i, carry):
        x = src_ref[pl.ds(pl.multiple_of(i * k, k), k), :]
        dst_ref[pl.ds(pl.multiple_of(i * m, m), m), :] = _mm(w_hi, w_lo, x, passes)
        return carry
    lax.fori_loop(0, count, body, 0, unroll=4)


def _short_conv_rows(z, w_ref, b_ref):
    l = z.shape[0]
    row = lax.broadcasted_iota(jnp.int32, z.shape, 0)
    prev = jnp.where(row == 0, 0.0, pltpu.roll(z, 1, axis=0))
    nxt = jnp.where(row == l - 1, 0.0, pltpu.roll(z, l - 1, axis=0))
    return w_ref[0:1, :] * prev + w_ref[1:2, :] * z + w_ref[2:3, :] * nxt + b_ref[...]


def _fftconv_kernel(u_ref, g_ref, uw_ref, ub_ref, gw_ref, gb_ref, kr_ref, ki_ref, bias_ref,
                    f1h, f1l, mh, ml, vh, vl, gh, gl, o_ref, z_scr, p_scr, q_scr, *, u_conv, passes):
    kc = pl.program_id(2)
    n1 = FFT_INNER
    cc = z_scr.shape[1]
    n2 = z_scr.shape[0] // n1
    h = n2 // 2

    def load_u(s):
        return _short_conv_rows(u_ref[s, 0], uw_ref, ub_ref) if u_conv else u_ref[s, 0]

    @pl.when(kc == 0)
    def _stage1():
        z = [jnp.swapaxes(load_u(s).reshape(h, n1, cc), 0, 1) for s in range(2)]
        z_scr[...] = jnp.concatenate(z, axis=1).reshape(n1 * n2, cc)
        _row_block_matmul(f1h[...], f1l[...], z_scr, p_scr, n1, passes)
        q_scr[...] = jnp.swapaxes(p_scr[...].reshape(n1, 2 * n2, cc), 0, 1).reshape(2 * n2 * n1, cc)

    ks = kr_ref.shape[0]
    rows = [(pl.multiple_of((kc * ks + j) * n1, n1), pl.multiple_of((n2 + kc * ks + j) * n1, n1)) for j in range(ks)]
    xs = [_mm(mh[j], ml[j], jnp.concatenate([q_scr[pl.ds(r0, n1), :], q_scr[pl.ds(i0, n1), :]], axis=0), passes)
          for j, (r0, i0) in enumerate(rows)]
    ys = []
    for j, x in enumerate(xs):
        xr, xi = x[:n1], x[n1:]
        kr, ki = kr_ref[j], ki_ref[j]
        ys.append(jnp.concatenate([xr * kr - xi * ki, xr * ki + xi * kr], axis=0))
    for j, (r0, i0) in enumerate(rows):
        b = _mm(vh[j], vl[j], ys[j], passes)
        p_scr[pl.ds(r0, n1), :] = b[:n1]
        p_scr[pl.ds(i0, n1), :] = b[n1:]

    @pl.when(kc == pl.num_programs(2) - 1)
    def _stage1_inv():
        q_scr[...] = jnp.swapaxes(p_scr[...].reshape(2 * n2, n1, cc), 0, 1).reshape(2 * n2 * n1, cc)
        _row_block_matmul(gh[...], gl[...], q_scr, z_scr, n1, passes)
        y3 = z_scr[...].reshape(n1, n2, cc)
        for s in range(2):
            y = jnp.swapaxes(y3[:, s * h:(s + 1) * h, :], 0, 1).reshape(h * n1, cc)
            gate = _short_conv_rows(g_ref[s, 0], gw_ref, gb_ref)
            o_ref[s, 0] = (gate * (y + load_u(s) * bias_ref[...])).astype(o_ref.dtype)


def _fft_gated_long_conv(u, u_col, z_hy, g_col, short_w, short_b, kr, ki, k_col, bias, ft, out_dtype):
    b, l, _ = z_hy.shape
    c = HY_WIDTH
    n1, n2, cc, ks = FFT_INNER, ft["n2"], FFT_COLS, FFT_K2_CHUNK
    p, ncc = b // 2, c // cc
    u_conv = u is None
    if u_conv:
        u = z_hy
    tok = lambda col: pl.BlockSpec((2, 1, l, cc), lambda q, j, k: (0, q, 0, col * ncc + j))
    taps = lambda col: pl.BlockSpec((3, cc), lambda q, j, k: (0, col * ncc + j))
    row = lambda col: pl.BlockSpec((1, cc), lambda q, j, k: (0, col * ncc + j))
    filt = pl.BlockSpec((ks, n1, cc), lambda q, j, k: (k, 0, k_col * ncc + j))
    tab = pl.BlockSpec((ks, 2 * n1, 2 * n1), lambda q, j, k: (k, 0, 0))
    res = lambda a: pl.BlockSpec(a.shape, lambda q, j, k: (0,) * a.ndim)
    sb = short_b.reshape(1, -1)
    y = pl.pallas_call(
        functools.partial(_fftconv_kernel, u_conv=u_conv, passes=FFT_CONV_PASSES), name="fft_long_conv",
        grid=(p, ncc, n2 // ks),
        in_specs=[tok(u_col), tok(g_col), taps(u_col), row(u_col), taps(g_col), row(g_col), filt, filt, row(0),
                  res(ft["f1_hi"]), res(ft["f1_lo"]), tab, tab, tab, tab, res(ft["g_hi"]), res(ft["g_lo"])],
        out_specs=tok(0),
        out_shape=jax.ShapeDtypeStruct((2, p, l, c), out_dtype),
        scratch_shapes=[pltpu.VMEM((n1 * n2, cc), F32), pltpu.VMEM((2 * n2 * n1, cc), F32),
                        pltpu.VMEM((2 * n2 * n1, cc), F32)],
        compiler_params=_cparams(("parallel", "parallel", "arbitrary"), 56),
    )(u.reshape(2, p, l, -1), z_hy.reshape(2, p, l, -1), short_w, sb, short_w, sb, kr, ki, bias.reshape(1, c),
      ft["f1_hi"], ft["f1_lo"], ft["m_hi"], ft["m_lo"], ft["mt_hi"], ft["mt_lo"], ft["g_hi"], ft["g_lo"])
    return y.reshape(b, l, c)


def _fftspec_kernel(a_ref, d_ref, f1h, f1l, mh, ml, kr_ref, ki_ref, z_scr, p_scr, q_scr):
    n1 = FFT_INNER
    cc = z_scr.shape[1]
    h = z_scr.shape[0] // n1
    n2 = 2 * h
    for src_ref, out_ref, lo in ((a_ref, kr_ref, 0), (d_ref, ki_ref, n1)):
        z_scr[...] = jnp.swapaxes(src_ref[0].reshape(h, n1, cc), 0, 1).reshape(n1 * h, cc)
        _row_block_matmul(f1h[...], f1l[...], z_scr, p_scr, n1)
        q_scr[...] = jnp.swapaxes(p_scr[...].reshape(n1, 2 * n2, cc), 0, 1).reshape(2 * n2 * n1, cc)

        def body(k2, carry):
            r0 = pl.multiple_of(k2 * n1, n1)
            i0 = pl.multiple_of((n2 + k2) * n1, n1)
            sh, sl = _split(jnp.concatenate([q_scr[pl.ds(r0, n1), :], q_scr[pl.ds(i0, n1), :]], axis=0))
            out_ref[k2] = _dot3(mh[k2][lo:lo + n1], ml[k2][lo:lo + n1], sh, sl)
            return carry
        lax.fori_loop(0, n2, body, 0, unroll=4)


def _fft_filter_spectrum(ad, ft):
    _, l, n = ad.shape
    n1, n2, cc = FFT_INNER, ft["n2"], FFT_COLS
    h = n2 // 2
    res = lambda a: pl.BlockSpec(a.shape, lambda j: (0,) * a.ndim)
    out = pl.BlockSpec((n2, n1, cc), lambda j: (0, 0, j))
    tabs = [ft["f1r_hi"], ft["f1r_lo"], ft["m_hi"], ft["m_lo"]]
    return pl.pallas_call(
        _fftspec_kernel, name="fft_filter_spectrum",
        grid=(n // cc,),
        in_specs=[pl.BlockSpec((1, l, cc), lambda j: (0, 0, j)), pl.BlockSpec((1, l, cc), lambda j: (1, 0, j))]
                 + [res(t) for t in tabs],
        out_specs=[out, out],
        out_shape=[jax.ShapeDtypeStruct((n2, n1, n), F32)] * 2,
        scratch_shapes=[pltpu.VMEM((n1 * h, cc), F32), pltpu.VMEM((2 * n2 * n1, cc), F32),
                        pltpu.VMEM((2 * n2 * n1, cc), F32)],
        compiler_params=_cparams(("parallel",), 56),
    )(ad, ad, *tabs)


def _softmax_weights(s, sink=None):
    m = jnp.max(s, axis=-1, keepdims=True)
    if sink is not None:
        m = jnp.maximum(m, sink)
    p = jnp.exp(s - m)
    den = jnp.sum(p, axis=-1, keepdims=True)
    if sink is not None:
        den = den + jnp.exp(sink - m)
    return p.astype(BF16), den


def _softmax_pv(s, v, sink=None):
    p, den = _softmax_weights(s, sink)
    return _dot(p, v) / den


def _qk(q, k):
    return lax.dot_general(q, k, (((1,), (1,)), ((), ())), preferred_element_type=F32)


def _half_mask(q_tile, half):
    lane = lax.broadcasted_iota(jnp.int32, q_tile.shape, 1)
    keep = (lane < HEAD_DIM) if half == 0 else (lane >= HEAD_DIM)
    return jnp.where(keep, q_tile, jnp.zeros_like(q_tile))


def _merge_halves(o_even, o_odd):
    lane = lax.broadcasted_iota(jnp.int32, o_even.shape, 1)
    return jnp.where(lane < HEAD_DIM, o_even, o_odd)


def _swa_kernel(sink_ref, q_ref, kp_ref, kc_ref, kn_ref, vp_ref, vc_ref, vn_ref, kx_ref, vx_ref, o_ref):
    n = pl.program_id(1)
    nb = pl.num_programs(1)
    blk = SWA_BLOCK
    g = SWA_HEADS // SWA_KV_HEADS
    n_ctx = kx_ref.shape[1]
    rows = g * blk
    ri = lax.broadcasted_iota(jnp.int32, (rows, 3 * blk + n_ctx), 0) & (blk - 1)
    ci = lax.broadcasted_iota(jnp.int32, (rows, 3 * blk + n_ctx), 1)
    ok_prev = (ci < blk) & (ci >= ri) & (n > 0)
    ok_cur = (ci >= blk) & (ci < 2 * blk)
    ok_next = (ci >= 2 * blk) & (ci < 3 * blk) & (ci - 2 * blk <= ri) & (n < nb - 1)
    ok = ok_prev | ok_cur | ok_next | (ci >= 3 * blk)
    bias = jnp.where(ok, 0.0, NEG_BIG)
    rsel = lax.broadcasted_iota(jnp.int32, (rows, 1), 0) >> (blk.bit_length() - 1)
    outs = [None] * SWA_HEADS
    kvs = range(SWA_KV_HEADS)
    lanes = [slice(kv * LANES, (kv + 1) * LANES) for kv in kvs]
    scores, sinks = [], []
    for kv in kvs:
        cs = lanes[kv]
        k = jnp.concatenate([kp_ref[0, :, cs], kc_ref[0, :, cs], kn_ref[0, :, cs], kx_ref[0, :, cs]], axis=0)
        q_parts = []
        sink = jnp.zeros((rows, 1), F32)
        for gi in range(g):
            h = kv * g + gi
            q_parts.append(_half_mask(q_ref[0, :, (h // 2) * LANES:(h // 2 + 1) * LANES], h % 2))
            sink = jnp.where(rsel == gi, sink_ref[h], sink)
        scores.append(_qk(jnp.concatenate(q_parts, axis=0), k))
        sinks.append(sink)
    probs = [_softmax_weights(scores[kv] + bias, sinks[kv]) for kv in kvs]
    for kv in kvs:
        cs = lanes[kv]
        v = jnp.concatenate([vp_ref[0, :, cs], vc_ref[0, :, cs], vn_ref[0, :, cs], vx_ref[0, :, cs]], axis=0)
        pw, den = probs[kv]
        o = _dot(pw, v) / den
        for gi in range(g):
            outs[kv * g + gi] = o[gi * blk:(gi + 1) * blk]
    for p in range(SWA_HEADS // 2):
        o_ref[0, :, p * LANES:(p + 1) * LANES] = _merge_halves(outs[2 * p], outs[2 * p + 1]).astype(o_ref.dtype)


def _window_gqa(q, k, v, kx, vx, sink):
    b, l, _ = q.shape
    nb = l // SWA_BLOCK
    n_ctx = kx.shape[1]
    kvw = k.shape[2]

    def blk(off):
        return pl.BlockSpec((1, SWA_BLOCK, kvw), lambda b, n: (b, jnp.clip(n + off, 0, nb - 1), 0))

    ctx = pl.BlockSpec((1, n_ctx, kvw), lambda b, n: (b, 0, 0))
    return pl.pallas_call(
        _swa_kernel, name="window_gqa",
        grid=(b, nb),
        in_specs=[
            pl.BlockSpec(memory_space=pltpu.SMEM),
            pl.BlockSpec((1, SWA_BLOCK, q.shape[2]), lambda b, n: (b, n, 0)),
            blk(-1), blk(0), blk(1), blk(-1), blk(0), blk(1), ctx, ctx,
        ],
        out_specs=pl.BlockSpec((1, SWA_BLOCK, q.shape[2]), lambda b, n: (b, n, 0)),
        out_shape=jax.ShapeDtypeStruct(q.shape, BF16),
        compiler_params=_cparams(("parallel", "parallel"), 32),
    )(sink, q, k, k, k, v, v, v, kx, vx)


def _na_bias_kernel(rpb_ref, o_ref):
    off = pl.program_id(0)
    h = pl.program_id(1)
    nd = 2 * NA_COLS - 1
    q = lax.broadcasted_iota(jnp.int32, (GRID_W, LANES), 0)
    lane = lax.broadcasted_iota(jnp.int32, (GRID_W, LANES), 1)
    kc = lane & (GRID_W - 1)
    upper = lane >= GRID_W
    cstart = jnp.clip(q - NA_COLS // 2, 0, GRID_W - NA_COLS)
    valid = (kc >= cstart) & (kc < cstart + NA_COLS)
    dc = jnp.clip(kc - q, -(NA_COLS - 1), NA_COLS - 1) + (NA_COLS - 1)
    for j in range(NA_MAX_ROWS // 2):
        base0 = (h * (2 * NA_MAX_ROWS - 1) + (2 * j - off + NA_MAX_ROWS - 1)) * nd
        base1 = base0 + nd
        t = jnp.zeros((GRID_W, LANES), F32)
        for d in range(nd):
            val = jnp.where(upper, rpb_ref[base1 + d], rpb_ref[base0 + d])
            t = jnp.where(dc == d, val, t)
        o_ref[0, 0, :, j * LANES:(j + 1) * LANES] = jnp.where(valid, t, NEG_BIG)


def _na_bias_table(rpb):
    return pl.pallas_call(
        _na_bias_kernel, name="na_bias_table",
        grid=(NA_MAX_ROWS, NA_HEADS),
        in_specs=[pl.BlockSpec(memory_space=pltpu.SMEM)],
        out_specs=pl.BlockSpec((1, 1, GRID_W, NA_MAX_ROWS * GRID_W), lambda o, h: (o, h, 0, 0)),
        out_shape=jax.ShapeDtypeStruct((NA_MAX_ROWS, NA_HEADS, GRID_W, NA_MAX_ROWS * GRID_W), F32),
        compiler_params=_cparams(("arbitrary", "arbitrary"), 16),
    )(rpb.reshape(-1))


NA_ROWS_PER_STEP = 4


def _na_kernel(q_ref, k_ref, v_ref, kx_ref, vx_ref, bias_ref, o_ref, *, rows):
    wr = NA_MAX_ROWS
    nloc = wr * GRID_W
    for i in range(NA_ROWS_PER_STEP):
        r = pl.program_id(1) * NA_ROWS_PER_STEP + i
        first = jnp.clip(r - wr // 2, 0, rows - wr)
        start = pl.multiple_of(first * GRID_W, GRID_W)
        off = r - first
        qr = slice(i * GRID_W, (i + 1) * GRID_W)
        pairs = range(NA_HEADS // 2)
        lanes = [slice(p * LANES, (p + 1) * LANES) for p in pairs]
        scores = []
        for p in pairs:
            q_tile = q_ref[0, qr, lanes[p]]
            q2 = jnp.concatenate([_half_mask(q_tile, 0), _half_mask(q_tile, 1)], axis=0)
            k = jnp.concatenate([k_ref[0, pl.ds(start, nloc), lanes[p]], kx_ref[0, :, lanes[p]]], axis=0)
            scores.append(_qk(q2, k))
        probs = []
        for p in pairs:
            s = scores[p]
            bias = jnp.concatenate([bias_ref[off, 2 * p], bias_ref[off, 2 * p + 1]], axis=0)
            probs.append(_softmax_weights(jnp.concatenate([s[:, :nloc] + bias, s[:, nloc:]], axis=1)))
        for p in pairs:
            pw, den = probs[p]
            v = jnp.concatenate([v_ref[0, pl.ds(start, nloc), lanes[p]], vx_ref[0, :, lanes[p]]], axis=0)
            o = _dot(pw, v) / den
            o_ref[0, qr, lanes[p]] = _merge_halves(o[:GRID_W], o[GRID_W:]).astype(o_ref.dtype)


def _neighbourhood_attention(q, k, v, kx, vx, bias_tab):
    b, l, w = q.shape
    rows = l // GRID_W
    n_ctx = kx.shape[1]
    tq = NA_ROWS_PER_STEP * GRID_W
    full = pl.BlockSpec((1, l, w), lambda b, r: (b, 0, 0))
    ctx = pl.BlockSpec((1, n_ctx, w), lambda b, r: (b, 0, 0))
    return pl.pallas_call(
        functools.partial(_na_kernel, rows=rows), name="neighbourhood_attention",
        grid=(b, rows // NA_ROWS_PER_STEP),
        in_specs=[pl.BlockSpec((1, tq, w), lambda b, r: (b, r, 0)), full, full, ctx, ctx,
                  _resident(bias_tab.shape)],
        out_specs=pl.BlockSpec((1, tq, w), lambda b, r: (b, r, 0)),
        out_shape=jax.ShapeDtypeStruct(q.shape, BF16),
        compiler_params=_cparams(("parallel", "arbitrary"), 48),
    )(q, k, v, kx, vx, bias_tab)


def _ctx_attn_kernel(sink_ref, qs_ref, ks_ref, vs_ref, qn_ref, kn_ref, vn_ref, os_ref, on_ref):
    n = qs_ref.shape[1]
    g = SWA_HEADS // SWA_KV_HEADS
    rsel = lax.broadcasted_iota(jnp.int32, (g * n, 1), 0) >> (n.bit_length() - 1)
    outs = [None] * SWA_HEADS
    for kv in range(SWA_KV_HEADS):
        cs = slice(kv * LANES, (kv + 1) * LANES)
        q_parts = []
        sink = jnp.zeros((g * n, 1), F32)
        for gi in range(g):
            h = kv * g + gi
            q_parts.append(_half_mask(qs_ref[0, :, (h // 2) * LANES:(h // 2 + 1) * LANES], h % 2))
            sink = jnp.where(rsel == gi, sink_ref[h], sink)
        o = _softmax_pv(_qk(jnp.concatenate(q_parts, axis=0), ks_ref[0, :, cs]), vs_ref[0, :, cs], sink)
        for gi in range(g):
            outs[kv * g + gi] = o[gi * n:(gi + 1) * n]
    for p in range(SWA_HEADS // 2):
        os_ref[0, :, p * LANES:(p + 1) * LANES] = _merge_halves(outs[2 * p], outs[2 * p + 1]).astype(os_ref.dtype)
    for p in range(NA_HEADS // 2):
        cs = slice(p * LANES, (p + 1) * LANES)
        q_tile = qn_ref[0, :, cs]
        q2 = jnp.concatenate([_half_mask(q_tile, 0), _half_mask(q_tile, 1)], axis=0)
        o = _softmax_pv(_qk(q2, kn_ref[0, :, cs]), vn_ref[0, :, cs])
        on_ref[0, :, cs] = _merge_halves(o[:n], o[n:]).astype(on_ref.dtype)


def _context_attention(qs, ks, vs, qn, kn, vn, sink):
    b, n, w = qs.shape
    spec = lambda a: pl.BlockSpec((1,) + a.shape[1:], lambda b: (b, 0, 0))
    return pl.pallas_call(
        _ctx_attn_kernel, name="context_attention",
        grid=(b,),
        in_specs=[pl.BlockSpec(memory_space=pltpu.SMEM)] + [spec(a) for a in (qs, ks, vs, qn, kn, vn)],
        out_specs=[spec(qs), spec(qn)],
        out_shape=[jax.ShapeDtypeStruct(qs.shape, BF16), jax.ShapeDtypeStruct(qn.shape, BF16)],
        compiler_params=_cparams(("parallel",), 32),
    )(sink, qs, ks, vs, qn, kn, vn)


def _merge_kernel(x_ref, gt_ref, yh_ref, ys_ref, yn_ref, zg_ref, wb_ref, wo_ref, o_ref):
    d = D_MODEL
    ys = (yh_ref[0].astype(BF16), ys_ref[0], yn_ref[0])
    m = None
    for n in range(3):
        proj = _dot(ys[n], wb_ref[n])
        term = zg_ref[0, :, n * d:(n + 1) * d].astype(F32) * proj
        m = term if m is None else m + term
    o_ref[0] = x_ref[0] + gt_ref[0] * _dot(m.astype(BF16), wo_ref[...])


def _merge_branches(x, gate, y_hy, y_swa, y_na, z_gate, w_branch, w_out):
    b, l, d = x.shape
    tm = min(512, l)
    tok = lambda n: pl.BlockSpec((1, tm, n), lambda b, t: (b, t, 0))
    return pl.pallas_call(
        _merge_kernel, name="merge_branches",
        grid=(b, l // tm),
        in_specs=[tok(d), _mod_spec(gate), tok(HY_WIDTH), tok(HY_WIDTH), tok(HY_WIDTH), tok(3 * d),
                  _resident(w_branch.shape), _resident(w_out.shape)],
        out_specs=tok(d),
        out_shape=jax.ShapeDtypeStruct(x.shape, F32),
        compiler_params=_cparams(("parallel", "parallel"), 48),
    )(x, gate, y_hy, y_swa, y_na, z_gate, w_branch, w_out)


def _dft_tables(l):
    n = 2 * l
    fb = 256 if l >= 1024 else LANES * ((l + 1 + LANES - 1) // LANES)
    fp = fb * ((l + 1 + fb - 1) // fb)
    k = jnp.arange(fp, dtype=jnp.int32)[:, None]
    t = jnp.arange(l, dtype=jnp.int32)[None, :]
    ang = ((k * t) % n).astype(F32) * (2.0 * math.pi / n)
    live = k <= l
    c = jnp.where(live, jnp.cos(ang), 0.0)
    s = jnp.where(live, jnp.sin(ang), 0.0)
    c_hi, c_lo = _split(c)
    s_hi, s_lo = _split(s)
    wk = jnp.where((k == 0) | (k == l), 1.0, 2.0) * jnp.where(live, 1.0 / n, 0.0)
    return dict(fb=fb, c_hi=c_hi, c_lo=c_lo, s_hi=s_hi, s_lo=s_lo,
                ct_hi=c_hi.T, ct_lo=c_lo.T, st_hi=s_hi.T, st_lo=s_lo.T, wk=wk.astype(F32))


def _filter_features(l):
    t = jnp.linspace(0.0, 1.0, l, dtype=F32)[:, None]
    w = (2.0 * math.pi / l) * jnp.arange(l, dtype=F32)[:, None]
    f = jnp.linspace(1e-4, HY_BANDS - 1, HY_BANDS, dtype=F32)[None, :]
    z = jnp.concatenate([t, jnp.cos(f * w), -jnp.sin(f * w)], axis=-1)
    return jnp.pad(z, ((0, 0), (0, LANES - HY_EMB)))


def _decay_rates():
    max_decay = math.log(HY_TARGET) / HY_FAST_DECAY
    min_decay = math.log(HY_TARGET) / HY_SLOW_DECAY
    return jnp.abs(jnp.linspace(min_decay, max_decay, HY_WIDTH, dtype=F32))[None, :]


def _rope_tables(l):
    pos = jnp.arange(l)
    row = (pos // GRID_W).astype(F32)
    col = (pos % GRID_W).astype(F32)
    half = HEAD_DIM // 2
    inv = 1.0 / (ROPE_BASE ** (jnp.arange(0, half, 2, dtype=F32) / half))
    ar = row[:, None] * inv[None, :]
    ac = col[:, None] * inv[None, :]
    cos = jnp.concatenate([jnp.cos(ar), jnp.cos(ar), jnp.cos(ac), jnp.cos(ac)], axis=-1)
    sin = jnp.concatenate([-jnp.sin(ar), jnp.sin(ar), -jnp.sin(ac), jnp.sin(ac)], axis=-1)
    return jnp.tile(cos, (1, 2)), jnp.tile(sin, (1, 2))


def _pad_to(a, shape):
    return jnp.pad(a, [(0, s - d) for d, s in zip(a.shape, shape)])


def _use_fft(bsz, length):
    return bsz % 2 == 0 and length % (8 * FFT_INNER) == 0


def _conv_tables(bsz, length):
    return _fft_tables(length) if _use_fft(bsz, length) else _dft_tables(length)


def _hyena_branch(z_hy, short_w, short_b, feat, fparams, deltas, bias, tabs):
    ad = _hyena_filter_sums(feat, fparams, deltas)
    if "n2" in tabs:
        kr, ki = _fft_filter_spectrum(ad, tabs)
        y1 = _fft_gated_long_conv(None, 0, z_hy, 1, short_w, short_b, kr, ki, 0, bias[0], tabs, F32)
        return _fft_gated_long_conv(y1, 0, z_hy, 2, short_w, short_b, kr, ki, 1, bias[1], tabs, BF16)
    v, x1, x2 = _short_conv3(z_hy, short_w, short_b)
    kr, ki = _filter_spectrum(ad, tabs)
    conv = functools.partial(_gated_long_conv, tabs=tabs)
    y1 = conv(v, x1, kr, ki, 0, bias[0])
    return conv(y1, x2, kr, ki, 1, bias[1])


def kernel(x, c, ctx, c_ctx, w_ada, b_ada, norm_g, ffn_w_gate, ffn_w_up, ffn_w_down,
           w_in, hy_short_w, hy_short_b, hy_pe_w0, hy_pe_b0, hy_pe_w1, hy_pe_b1,
           hy_pe_w2, hy_pe_b2, hy_pe_wout, hy_sin_freq, hy_bias,
           swa_q_gain, swa_k_gain, swa_sink, na_q_gain, na_k_gain, na_rpb,
           w_branch, w_out):
    bsz, seq, d = x.shape
    n_ctx = ctx.shape[1]
    depth = w_ada.shape[0]

    c16 = _pad_to(jnp.concatenate([c, c_ctx[None, :]], axis=0), (MOD_ROWS, d))
    mods = _adaln_mods(c16, w_ada, b_ada).reshape(depth, MOD_ROWS, N_MOD, 1, d)

    tabs_x = _conv_tables(bsz, seq)
    tabs_c = _conv_tables(bsz, n_ctx)
    feat_x = _filter_features(seq)
    feat_c = _filter_features(n_ctx)
    deltas = _decay_rates()
    cos_x, sin_x = _rope_tables(seq)
    cos_c = jnp.ones((n_ctx, LANES), F32)
    sin_c = jnp.zeros((n_ctx, LANES), F32)
    eye = jnp.arange(MXU_DIM) // HEAD_DIM
    bd = (eye[:, None] == eye[None, :]).astype(BF16)

    wg = ffn_w_gate.astype(BF16)
    wu = ffn_w_up.astype(BF16)
    wd = ffn_w_down.astype(BF16)
    wb = w_branch.astype(BF16)
    wo = w_out.astype(BF16)
    win = w_in.astype(BF16)
    kv = win[:, :, _OFF_SWA + _SWA_Q:_OFF_NA].reshape(depth, d, 2, SWA_KV_HEADS, 1, HEAD_DIM)
    win_kv = jnp.broadcast_to(kv, (depth, d, 2, SWA_KV_HEADS, 2, HEAD_DIM)).reshape(depth, d, 4 * _SWA_KV)

    xc = ctx
    for i in range(depth):
        last = i == depth - 1
        mx = lambda j: mods[i, :bsz, j]
        mc = lambda j: mods[i, bsz:bsz + 1, j]
        tile2 = lambda g: jnp.tile(g, 2 * MXU_DIM // LANES)
        head_gains = jnp.stack([tile2(swa_q_gain[i]), tile2(swa_k_gain[i]), tile2(na_q_gain[i]), tile2(na_k_gain[i])])
        fparams = (_pad_to(hy_pe_w0[i], (LANES, LANES)), _pad_to(hy_pe_b0[i][None], (1, LANES)),
                   _pad_to(hy_pe_w1[i], (LANES, LANES)), _pad_to(hy_pe_b1[i][None], (1, LANES)),
                   _pad_to(hy_pe_w2[i], (LANES, LANES)), _pad_to(hy_pe_b2[i][None], (1, LANES)),
                   _pad_to(hy_pe_wout[i], (LANES, 4 * HY_WIDTH)), _pad_to(hy_sin_freq[i][None], (1, LANES)))

        x = _ffn_half_step(x, mx(0), mx(1), mx(2), norm_g[i, 0], wg[i, 0], wu[i, 0], wd[i, 0])
        xc = _ffn_half_step(xc, mc(0), mc(1), mc(2), norm_g[i, 0], wg[i, 0], wu[i, 0], wd[i, 0])

        z_hy, q_s, k_s, v_s, q_n, k_n, v_n, z_gate = _in_projection(
            x, mx(3), mx(4), norm_g[i, 1], cos_x, sin_x, head_gains, bd, win[i], win_kv[i])
        zc_hy, qc_s, kc_s, vc_s, qc_n, kc_n, vc_n, zc_gate = _in_projection(
            xc, mc(3), mc(4), norm_g[i, 1], cos_c, sin_c, head_gains, bd, win[i], win_kv[i])

        y_hy = _hyena_branch(z_hy, hy_short_w[i], hy_short_b[i], feat_x, fparams, deltas, hy_bias[i], tabs_x)
        y_swa = _window_gqa(q_s, k_s, v_s, kc_s, vc_s, swa_sink[i])
        y_na = _neighbourhood_attention(q_n, k_n, v_n, kc_n, vc_n, _na_bias_table(na_rpb[i]))
        x = _merge_branches(x, mx(5), y_hy, y_swa, y_na, z_gate, wb[i], wo[i])

        if not last:
            yc_hy = _hyena_branch(zc_hy, hy_short_w[i], hy_short_b[i], feat_c, fparams, deltas, hy_bias[i], tabs_c)
            yc_swa, yc_na = _context_attention(qc_s, kc_s, vc_s, qc_n, kc_n, vc_n, swa_sink[i])
            xc = _merge_branches(xc, mc(5), yc_hy, yc_swa, yc_na, zc_gate, wb[i], wo[i])

        x = _ffn_half_step(x, mx(6), mx(7), mx(8), norm_g[i, 2], wg[i, 1], wu[i, 1], wd[i, 1])
        if not last:
            xc = _ffn_half_step(xc, mc(6), mc(7), mc(8), norm_g[i, 2], wg[i, 1], wu[i, 1], wd[i, 1])
    return x
```

```python
import functools
import math

import jax
import jax.numpy as jnp
from jax import lax
from jax.experimental import pallas as pl
from jax.experimental.pallas import tpu as pltpu

F32 = jnp.float32
BF16 = jnp.bfloat16

D_MODEL = 1024
DEPTH = 4
GRID_W = 64
HEAD_DIM = 64
N_MOD = 9
RMS_EPS = 1e-6

HY_WIDTH = D_MODEL // 2
HY_EMB = 33
HY_BANDS = (HY_EMB - 1) // 2
HY_FFN = 64
HY_FAST_DECAY = 0.3
HY_SLOW_DECAY = 1.5
HY_TARGET = 1e-2

SWA_HEADS = 8
SWA_KV_HEADS = 2
SWA_WINDOW = 128
SWA_BLOCK = 128
ROPE_BASE = 10000.0

NA_HEADS = 8
NA_MAX_ROWS = 8
NA_COLS = 16
NA_COL_BLOCK = 16

FFN_HIDDEN = 256 * ((8 * D_MODEL // 3 + 255) // 256)

LANES = 128
MXU_DIM = 256
MOD_ROWS = 16
NEG_BIG = -1e30


def _cparams(sem, vmem_mb):
    return pltpu.CompilerParams(dimension_semantics=sem, vmem_limit_bytes=vmem_mb * 1024 * 1024)


def _resident(shape):
    nd = len(shape)
    return pl.BlockSpec(shape, lambda *_: (0,) * nd, pipeline_mode=pl.Buffered(1))


def _layer_slab(arr, idx):
    rest = arr.shape[len(idx):]
    return pl.BlockSpec((None,) * len(idx) + rest, lambda *_: tuple(idx) + (0,) * len(rest),
                        pipeline_mode=pl.Buffered(1))


def _split(x):
    hi = x.astype(BF16)
    lo = (x - hi.astype(F32)).astype(BF16)
    return hi, lo


def _dot(a, b):
    return jnp.dot(a, b, preferred_element_type=F32)


def _dot3(a_hi, a_lo, b_hi, b_lo):
    return _dot(a_hi, b_hi) + _dot(a_lo, b_hi) + _dot(a_hi, b_lo)


def _dot3f(a, b):
    a_hi, a_lo = _split(a)
    b_hi, b_lo = _split(b)
    return _dot3(a_hi, a_lo, b_hi, b_lo)


def _rms_mod(x, gain, shift, scale):
    ms = jnp.mean(x * x, axis=-1, keepdims=True)
    y = x * lax.rsqrt(ms + RMS_EPS) * gain
    return y * (1.0 + scale) + shift


def _head_rms(x, bd, gain):
    outs = []
    for j in range(x.shape[1] // MXU_DIM):
        xc = x[:, j * MXU_DIM:(j + 1) * MXU_DIM]
        hi, lo = _split(xc * xc)
        ssum = _dot(hi, bd) + _dot(lo, bd)
        outs.append(xc * lax.rsqrt(ssum * (1.0 / HEAD_DIM) + RMS_EPS) * gain)
    return outs[0] if len(outs) == 1 else jnp.concatenate(outs, axis=1)


def _rope(x, cos, sin):
    lane = lax.broadcasted_iota(jnp.int32, (x.shape[0], LANES), 1)
    first = (lane & 31) < 16
    outs = []
    for j in range(x.shape[1] // LANES):
        xc = x[:, j * LANES:(j + 1) * LANES]
        partner = jnp.where(first, pltpu.roll(xc, LANES - 16, axis=1), pltpu.roll(xc, 16, axis=1))
        outs.append(xc * cos + partner * sin)
    return outs[0] if len(outs) == 1 else jnp.concatenate(outs, axis=1)


def _mods_kernel(c_ref, w_ref, b_ref, o_ref):
    c = c_ref[...]
    a = c * jax.nn.sigmoid(c)
    o_ref[0] = _dot3f(a, w_ref[0]) + b_ref[0]


def _adaln_mods(c16, w_ada, b_ada):
    depth, d, n = w_ada.shape
    nb = 1152
    return pl.pallas_call(
        _mods_kernel, name="adaln_mods",
        grid=(depth, n // nb),
        in_specs=[
            pl.BlockSpec((MOD_ROWS, d), lambda i, j: (0, 0)),
            pl.BlockSpec((1, d, nb), lambda i, j: (i, 0, j)),
            pl.BlockSpec((1, 1, nb), lambda i, j: (i, 0, j)),
        ],
        out_specs=pl.BlockSpec((1, MOD_ROWS, nb), lambda i, j: (i, 0, j)),
        out_shape=jax.ShapeDtypeStruct((depth, MOD_ROWS, n), F32),
        compiler_params=_cparams(("arbitrary", "arbitrary"), 48),
    )(c16, w_ada, b_ada.reshape(depth, 1, n))


def _swiglu_residual(x, sh_ref, sc_ref, gt_ref, g_ref, wg_ref, wu_ref, wd_ref, fc):
    hb = _rms_mod(x, g_ref[...], sh_ref[0], sc_ref[0]).astype(BF16)
    acc = None
    for f0 in range(0, FFN_HIDDEN, fc):
        g = _dot(hb, wg_ref[:, f0:f0 + fc])
        u = _dot(hb, wu_ref[:, f0:f0 + fc])
        a = (g * jax.nn.sigmoid(g) * u).astype(BF16)
        d = _dot(a, wd_ref[f0:f0 + fc, :])
        acc = d if acc is None else acc + d
    return x + 0.5 * gt_ref[0] * acc


def _ffn_kernel(x_ref, sh_ref, sc_ref, gt_ref, g_ref, wg_ref, wu_ref, wd_ref, o_ref, *, fc):
    o_ref[0] = _swiglu_residual(x_ref[0], sh_ref, sc_ref, gt_ref, g_ref, wg_ref, wu_ref, wd_ref, fc)


def _mod_spec(mod):
    if mod.shape[0] == 1:
        return pl.BlockSpec((1, 1, D_MODEL), lambda b, t: (0, 0, 0))
    return pl.BlockSpec((1, 1, D_MODEL), lambda b, t: (b, 0, 0))


def _ffn_half_step(x, shift, scale, gate, gain, wg, wu, wd, idx):
    b, l, d = x.shape
    tm = min(512, l)
    return pl.pallas_call(
        functools.partial(_ffn_kernel, fc=FFN_HIDDEN // 2), name="ffn_half_step",
        grid=(b, l // tm),
        in_specs=[
            pl.BlockSpec((1, tm, d), lambda b, t: (b, t, 0)),
            _mod_spec(shift), _mod_spec(scale), _mod_spec(gate),
            _resident((1, d)),
            _layer_slab(wg, idx), _layer_slab(wu, idx), _layer_slab(wd, idx),
        ],
        out_specs=pl.BlockSpec((1, tm, d), lambda b, t: (b, t, 0)),
        out_shape=jax.ShapeDtypeStruct(x.shape, F32),
        compiler_params=_cparams(("parallel", "parallel"), 56),
    )(x, shift, scale, gate, gain.reshape(1, d), wg, wu, wd)


_HY_COLS = 3 * HY_WIDTH
_SWA_Q = SWA_HEADS * HEAD_DIM
_SWA_KV = SWA_KV_HEADS * HEAD_DIM
_NA_W = NA_HEADS * HEAD_DIM
_OFF_SWA = _HY_COLS
_OFF_NA = _OFF_SWA + _SWA_Q + 2 * _SWA_KV
_OFF_GATE = _OFF_NA + 3 * _NA_W


def _inproj_kernel(x_ref, sh_ref, sc_ref, g_ref, cos_ref, sin_ref, hg_ref, bd_ref, w_ref, wkv_ref,
                   zhy_ref, qs_ref, ks_ref, vs_ref, qn_ref, kn_ref, vn_ref, gt_ref):
    hb = _rms_mod(x_ref[0], g_ref[...], sh_ref[0], sc_ref[0]).astype(BF16)
    cos = cos_ref[...]
    sin = sin_ref[...]
    bd = bd_ref[...]
    scale = HEAD_DIM ** -0.5
    proj = lambda lo, n: _dot(hb, w_ref[:, lo:lo + n])
    kvw = 2 * _SWA_KV
    qs = proj(_OFF_SWA, _SWA_Q)
    ks = _dot(hb, wkv_ref[:, :kvw])
    qn = proj(_OFF_NA, _NA_W)
    kn = proj(_OFF_NA + _NA_W, _NA_W)
    vs_ref[0] = _dot(hb, wkv_ref[:, kvw:]).astype(BF16)
    vn_ref[0] = proj(_OFF_NA + 2 * _NA_W, _NA_W).astype(BF16)
    zhy_ref[0] = proj(0, _HY_COLS)
    gt_ref[0] = jax.nn.sigmoid(proj(_OFF_GATE, 3 * D_MODEL)).astype(BF16)
    qs_ref[0] = (_rope(_head_rms(qs, bd, hg_ref[0:1, :]), cos, sin) * scale).astype(BF16)
    ks_ref[0] = _rope(_head_rms(ks, bd, hg_ref[1:2, :]), cos, sin).astype(BF16)
    qn_ref[0] = (_head_rms(qn, bd, hg_ref[2:3, :]) * scale).astype(BF16)
    kn_ref[0] = _head_rms(kn, bd, hg_ref[3:4, :]).astype(BF16)


def _in_projection(x, shift, scale, gain, cos, sin, head_gains, bd, w, w_kv_dup, layer):
    b, l, d = x.shape
    tm = min(512, l)
    kvw = 2 * _SWA_KV
    widths = [_HY_COLS, _SWA_Q, kvw, kvw, _NA_W, _NA_W, _NA_W, 3 * d]
    dtypes = [F32] + [BF16] * 7
    return pl.pallas_call(
        _inproj_kernel, name="in_projection",
        grid=(b, l // tm),
        in_specs=[
            pl.BlockSpec((1, tm, d), lambda b, t: (b, t, 0)),
            _mod_spec(shift), _mod_spec(scale),
            _resident((1, d)),
            pl.BlockSpec((tm, LANES), lambda b, t: (t, 0)),
            pl.BlockSpec((tm, LANES), lambda b, t: (t, 0)),
            _resident(head_gains.shape), _resident(bd.shape), _layer_slab(w, (layer,)), _layer_slab(w_kv_dup, (layer,)),
        ],
        out_specs=[pl.BlockSpec((1, tm, n), lambda b, t: (b, t, 0)) for n in widths],
        out_shape=[jax.ShapeDtypeStruct((b, l, n), dt) for n, dt in zip(widths, dtypes)],
        compiler_params=_cparams(("parallel", "parallel"), 56),
    )(x, shift, scale, gain.reshape(1, d), cos, sin, head_gains, bd, w, w_kv_dup)


def _shortconv_kernel(z0_ref, z1_ref, z2_ref, w_ref, b_ref, o0_ref, o1_ref, o2_ref):
    l = z0_ref.shape[1]
    row = lax.broadcasted_iota(jnp.int32, z0_ref.shape[1:], 0)
    for g, (z_ref, o_ref) in enumerate(((z0_ref, o0_ref), (z1_ref, o1_ref), (z2_ref, o2_ref))):
        z = z_ref[0]
        prev = jnp.where(row == 0, 0.0, pltpu.roll(z, 1, axis=0))
        nxt = jnp.where(row == l - 1, 0.0, pltpu.roll(z, l - 1, axis=0))
        o_ref[0] = w_ref[g, 0:1, :] * prev + w_ref[g, 1:2, :] * z + w_ref[g, 2:3, :] * nxt + b_ref[g]


def _short_conv3(z, w, bias):
    b, l, n = z.shape
    cb = 256
    ncb = HY_WIDTH // cb
    wg = w.reshape(3, 3, HY_WIDTH).transpose(1, 0, 2)
    zin = lambda g: pl.BlockSpec((1, l, cb), lambda b, j: (b, 0, g * ncb + j))
    out = pl.BlockSpec((1, l, cb), lambda b, j: (b, 0, j))
    return pl.pallas_call(
        _shortconv_kernel, name="short_conv3",
        grid=(b, ncb),
        in_specs=[zin(0), zin(1), zin(2),
                  pl.BlockSpec((3, 3, cb), lambda b, j: (0, 0, j)),
                  pl.BlockSpec((3, 1, cb), lambda b, j: (0, 0, j))],
        out_specs=[out, out, out],
        out_shape=[jax.ShapeDtypeStruct((b, l, HY_WIDTH), F32)] * 3,
        compiler_params=_cparams(("parallel", "parallel"), 48),
    )(z, z, z, wg, bias.reshape(3, 1, HY_WIDTH))


def _filter_kernel(z_ref, w0, b0, w1, b1, w2, b2, wo, fr_ref, dl_ref, o_ref):
    z = z_ref[...]
    fr = fr_ref[...]
    a = jnp.sin(fr * (_dot3f(z, w0[...]) + b0[...]))
    a = jnp.sin(fr * (_dot3f(a, w1[...]) + b1[...]))
    a = jnp.sin(fr * (_dot3f(a, w2[...]) + b2[...]))
    hh = _dot3f(a, wo[...])
    t = z[:, 0:1]
    win = jnp.exp(-t * dl_ref[...])
    row = lax.broadcasted_iota(jnp.int32, win.shape, 0) + pl.program_id(0) * z.shape[0]
    w = HY_WIDTH
    for o in range(2):
        hp = hh[:, (2 * o) * w:(2 * o + 1) * w] * win
        hn = jnp.where(row == 0, 0.0, hh[:, (2 * o + 1) * w:(2 * o + 2) * w] * win)
        o_ref[0, :, o * w:(o + 1) * w] = hp + hn
        o_ref[1, :, o * w:(o + 1) * w] = hp - hn


def _hyena_filter_sums(zfeat, fp, deltas):
    l = zfeat.shape[0]
    tl = min(256, l)
    w0, b0, w1, b1, w2, b2, wo, fr = fp
    n = 2 * HY_WIDTH
    consts = [w0, b0, w1, b1, w2, b2, wo, fr, deltas]
    return pl.pallas_call(
        _filter_kernel, name="hyena_filter",
        grid=(l // tl,),
        in_specs=[pl.BlockSpec((tl, LANES), lambda t: (t, 0))] + [_resident(c.shape) for c in consts],
        out_specs=pl.BlockSpec((2, tl, n), lambda t: (0, t, 0)),
        out_shape=jax.ShapeDtypeStruct((2, l, n), F32),
        compiler_params=_cparams(("parallel",), 32),
    )(zfeat, *consts)


def _spectrum_kernel(a_ref, d_ref, ch, cl, sh, sl, kr_ref, ki_ref):
    a_hi, a_lo = _split(a_ref[0])
    d_hi, d_lo = _split(d_ref[0])
    kr_ref[...] = _dot3(ch[...], cl[...], a_hi, a_lo)
    ki_ref[...] = -_dot3(sh[...], sl[...], d_hi, d_lo)


def _filter_spectrum(ad, tabs):
    _, l, n = ad.shape
    fp = tabs["c_hi"].shape[0]
    fb = tabs["fb"]
    cb = 256
    tab = pl.BlockSpec((fb, l), lambda j, k: (k, 0))
    out = pl.BlockSpec((fb, cb), lambda j, k: (k, j))
    return pl.pallas_call(
        _spectrum_kernel, name="dense_spectrum",
        grid=(n // cb, fp // fb),
        in_specs=[pl.BlockSpec((1, l, cb), lambda j, k: (0, 0, j)),
                  pl.BlockSpec((1, l, cb), lambda j, k: (1, 0, j)), tab, tab, tab, tab],
        out_specs=[out, out],
        out_shape=[jax.ShapeDtypeStruct((fp, n), F32)] * 2,
        compiler_params=_cparams(("parallel", "parallel"), 48),
    )(ad, ad, tabs["c_hi"], tabs["c_lo"], tabs["s_hi"], tabs["s_lo"])


def _longconv_kernel(u_ref, g_ref, kr_ref, ki_ref, bias_ref, wk_ref, ch, cl, sh, sl, cth, ctl, sth, stl,
                     o_ref, uh_ref, ul_ref, acc_ref):
    kb = pl.program_id(2)

    @pl.when(kb == 0)
    def _():
        hi, lo = _split(u_ref[0])
        uh_ref[...] = hi
        ul_ref[...] = lo
        acc_ref[...] = jnp.zeros_like(acc_ref)

    uh = uh_ref[...]
    ul = ul_ref[...]
    xr = _dot3(ch[...], cl[...], uh, ul)
    xi = -_dot3(sh[...], sl[...], uh, ul)
    kr = kr_ref[...]
    ki = ki_ref[...]
    wk = wk_ref[...]
    yr_hi, yr_lo = _split((xr * kr - xi * ki) * wk)
    yi_hi, yi_lo = _split((xr * ki + xi * kr) * wk)
    acc_ref[...] += _dot3(cth[...], ctl[...], yr_hi, yr_lo) - _dot3(sth[...], stl[...], yi_hi, yi_lo)

    @pl.when(kb == pl.num_programs(2) - 1)
    def _():
        u = u_ref[0]
        o_ref[0] = g_ref[0] * (acc_ref[...] + u * bias_ref[...])


def _gated_long_conv(u_arr, g_arr, kr, ki, k_col, bias, tabs):
    b, l, _ = u_arr.shape
    cb = 256
    ncb = HY_WIDTH // cb
    fp = tabs["c_hi"].shape[0]
    fb = tabs["fb"]
    tab = pl.BlockSpec((fb, l), lambda b, j, k: (k, 0))
    tabt = pl.BlockSpec((l, fb), lambda b, j, k: (0, k))
    spec = pl.BlockSpec((fb, cb), lambda b, j, k: (k, k_col * ncb + j))
    return pl.pallas_call(
        _longconv_kernel, name="dense_long_conv",
        grid=(b, ncb, fp // fb),
        in_specs=[
            pl.BlockSpec((1, l, cb), lambda b, j, k: (b, 0, j)),
            pl.BlockSpec((1, l, cb), lambda b, j, k: (b, 0, j)),
            spec, spec,
            pl.BlockSpec((1, cb), lambda b, j, k: (0, j)),
            pl.BlockSpec((fb, 1), lambda b, j, k: (k, 0)),
            tab, tab, tab, tab, tabt, tabt, tabt, tabt,
        ],
        out_specs=pl.BlockSpec((1, l, cb), lambda b, j, k: (b, 0, j)),
        out_shape=jax.ShapeDtypeStruct((b, l, HY_WIDTH), F32),
        scratch_shapes=[pltpu.VMEM((l, cb), BF16), pltpu.VMEM((l, cb), BF16), pltpu.VMEM((l, cb), F32)],
        compiler_params=_cparams(("parallel", "parallel", "arbitrary"), 56),
    )(u_arr, g_arr, kr, ki, bias.reshape(1, HY_WIDTH), tabs["wk"],
      tabs["c_hi"], tabs["c_lo"], tabs["s_hi"], tabs["s_lo"],
      tabs["ct_hi"], tabs["ct_lo"], tabs["st_hi"], tabs["st_lo"])


FFT_INNER = 64
FFT_COLS = 256
FFT_K2_CHUNK = 8
FFT_CONV_PASSES = 1


def _fft_tables(l):
    n = 2 * l
    n1 = FFT_INNER
    n2 = n // n1
    h = n2 // 2
    ang = lambda idx, mod: (idx % mod).astype(F32) * (2.0 * math.pi / mod)
    k2 = jnp.arange(n2, dtype=jnp.int32)
    m2 = jnp.arange(h, dtype=jnp.int32)
    a1 = ang(k2[:, None] * m2[None, :], n2)
    c1, s1 = jnp.cos(a1), jnp.sin(a1)
    f1 = jnp.block([[c1, s1], [-s1, c1]])
    k1 = jnp.arange(n1, dtype=jnp.int32)
    a2 = ang(k1[None, None, :] * (k1[None, :, None] * n2 + k2[:, None, None]), n)
    c2, s2 = jnp.cos(a2), jnp.sin(a2)
    m = jnp.concatenate([jnp.concatenate([c2, s2], axis=2), jnp.concatenate([-s2, c2], axis=2)], axis=1)
    g = jnp.block([[c1.T, -s1.T], [s1.T, c1.T]]) * (1.0 / n)
    out = dict(n2=n2)
    for name, t in (("f1", f1), ("f1r", f1[:, :h]), ("m", m), ("mt", jnp.swapaxes(m, 1, 2)), ("g", g)):
        out[name + "_hi"], out[name + "_lo"] = _split(t)
    return out


def _mm(w_hi, w_lo, x, passes):
    if passes == 1:
        return _dot(w_hi, x.astype(BF16))
    hi, lo = _split(x)
    return _dot3(w_hi, w_lo, hi, lo)


def _row_block_matmul(w_hi, w_lo, src_ref, dst_ref, count, passes=3):
    m, k = w_hi.shape

    def body(i, carry):
        x = src_ref[pl.ds(pl.multiple_of(i * k, k), k), :]
        dst_ref[pl.ds(pl.multiple_of(i * m, m), m), :] = _mm(w_hi, w_lo, x, passes)
        return carry
    lax.fori_loop(0, count, body, 0, unroll=4)


def _short_conv_rows(z, w_ref, b_ref):
    l = z.shape[0]
    row = lax.broadcasted_iota(jnp.int32, z.shape, 0)
    prev = jnp.where(row == 0, 0.0, pltpu.roll(z, 1, axis=0))
    nxt = jnp.where(row == l - 1, 0.0, pltpu.roll(z, l - 1, axis=0))
    return w_ref[0:1, :] * prev + w_ref[1:2, :] * z + w_ref[2:3, :] * nxt + b_ref[...]


def _fftconv_kernel(u_ref, g_ref, uw_ref, ub_ref, gw_ref, gb_ref, kr_ref, ki_ref, bias_ref,
                    f1h, f1l, mh, ml, vh, vl, gh, gl, o_ref, z_scr, p_scr, q_scr, *, u_conv, passes):
    kc = pl.program_id(2)
    n1 = FFT_INNER
    cc = z_scr.shape[1]
    n2 = z_scr.shape[0] // n1
    h = n2 // 2

    def load_u(s):
        return _short_conv_rows(u_ref[s, 0], uw_ref, ub_ref) if u_conv else u_ref[s, 0]

    @pl.when(kc == 0)
    def _stage1():
        z = [jnp.swapaxes(load_u(s).reshape(h, n1, cc), 0, 1) for s in range(2)]
        z_scr[...] = jnp.concatenate(z, axis=1).reshape(n1 * n2, cc)
        _row_block_matmul(f1h[...], f1l[...], z_scr, p_scr, n1, passes)
        q_scr[...] = jnp.swapaxes(p_scr[...].reshape(n1, 2 * n2, cc), 0, 1).reshape(2 * n2 * n1, cc)

    ks = kr_ref.shape[0]
    rows = [(pl.multiple_of((kc * ks + j) * n1, n1), pl.multiple_of((n2 + kc * ks + j) * n1, n1)) for j in range(ks)]
    xs = [_mm(mh[j], ml[j], jnp.concatenate([q_scr[pl.ds(r0, n1), :], q_scr[pl.ds(i0, n1), :]], axis=0), passes)
          for j, (r0, i0) in enumerate(rows)]
    ys = []
    for j, x in enumerate(xs):
        xr, xi = x[:n1], x[n1:]
        kr, ki = kr_ref[j], ki_ref[j]
        ys.append(jnp.concatenate([xr * kr - xi * ki, xr * ki + xi * kr], axis=0))
    for j, (r0, i0) in enumerate(rows):
        b = _mm(vh[j], vl[j], ys[j], passes)
        p_scr[pl.ds(r0, n1), :] = b[:n1]
        p_scr[pl.ds(i0, n1), :] = b[n1:]

    @pl.when(kc == pl.num_programs(2) - 1)
    def _stage1_inv():
        q_scr[...] = jnp.swapaxes(p_scr[...].reshape(2 * n2, n1, cc), 0, 1).reshape(2 * n2 * n1, cc)
        _row_block_matmul(gh[...], gl[...], q_scr, z_scr, n1, passes)
        y3 = z_scr[...].reshape(n1, n2, cc)
        for s in range(2):
            y = jnp.swapaxes(y3[:, s * h:(s + 1) * h, :], 0, 1).reshape(h * n1, cc)
            gate = _short_conv_rows(g_ref[s, 0], gw_ref, gb_ref)
            o_ref[s, 0] = (gate * (y + load_u(s) * bias_ref[...])).astype(o_ref.dtype)


def _fft_gated_long_conv(u, u_col, z_hy, g_col, short_w, short_b, kr, ki, k_col, bias, ft, out_dtype):
    b, l, _ = z_hy.shape
    c = HY_WIDTH
    n1, n2, cc, ks = FFT_INNER, ft["n2"], FFT_COLS, FFT_K2_CHUNK
    p, ncc = b // 2, c // cc
    u_conv = u is None
    if u_conv:
        u = z_hy
    tok = lambda col: pl.BlockSpec((2, 1, l, cc), lambda q, j, k: (0, q, 0, col * ncc + j))
    taps = lambda col: pl.BlockSpec((3, cc), lambda q, j, k: (0, col * ncc + j))
    row = lambda col: pl.BlockSpec((1, cc), lambda q, j, k: (0, col * ncc + j))
    filt = pl.BlockSpec((ks, n1, cc), lambda q, j, k: (k, 0, k_col * ncc + j))
    tab = pl.BlockSpec((ks, 2 * n1, 2 * n1), lambda q, j, k: (k, 0, 0))
    res = lambda a: pl.BlockSpec(a.shape, lambda q, j, k: (0,) * a.ndim)
    sb = short_b.reshape(1, -1)
    y = pl.pallas_call(
        functools.partial(_fftconv_kernel, u_conv=u_conv, passes=FFT_CONV_PASSES), name="fft_long_conv",
        grid=(p, ncc, n2 // ks),
        in_specs=[tok(u_col), tok(g_col), taps(u_col), row(u_col), taps(g_col), row(g_col), filt, filt, row(0),
                  res(ft["f1_hi"]), res(ft["f1_lo"]), tab, tab, tab, tab, res(ft["g_hi"]), res(ft["g_lo"])],
        out_specs=tok(0),
        out_shape=jax.ShapeDtypeStruct((2, p, l, c), out_dtype),
        scratch_shapes=[pltpu.VMEM((n1 * n2, cc), F32), pltpu.VMEM((2 * n2 * n1, cc), F32),
                        pltpu.VMEM((2 * n2 * n1, cc), F32)],
        compiler_params=_cparams(("parallel", "parallel", "arbitrary"), 56),
    )(u.reshape(2, p, l, -1), z_hy.reshape(2, p, l, -1), short_w, sb, short_w, sb, kr, ki, bias.reshape(1, c),
      ft["f1_hi"], ft["f1_lo"], ft["m_hi"], ft["m_lo"], ft["mt_hi"], ft["mt_lo"], ft["g_hi"], ft["g_lo"])
    return y.reshape(b, l, c)


def _fftspec_kernel(a_ref, d_ref, f1h, f1l, mh, ml, kr_ref, ki_ref, z_scr, p_scr, q_scr):
    n1 = FFT_INNER
    cc = z_scr.shape[1]
    h = z_scr.shape[0] // n1
    n2 = 2 * h
    for src_ref, out_ref, lo in ((a_ref, kr_ref, 0), (d_ref, ki_ref, n1)):
        z_scr[...] = jnp.swapaxes(src_ref[0].reshape(h, n1, cc), 0, 1).reshape(n1 * h, cc)
        _row_block_matmul(f1h[...], f1l[...], z_scr, p_scr, n1)
        q_scr[...] = jnp.swapaxes(p_scr[...].reshape(n1, 2 * n2, cc), 0, 1).reshape(2 * n2 * n1, cc)

        def body(k2, carry):
            r0 = pl.multiple_of(k2 * n1, n1)
            i0 = pl.multiple_of((n2 + k2) * n1, n1)
            sh, sl = _split(jnp.concatenate([q_scr[pl.ds(r0, n1), :], q_scr[pl.ds(i0, n1), :]], axis=0))
            out_ref[k2] = _dot3(mh[k2][lo:lo + n1], ml[k2][lo:lo + n1], sh, sl)
            return carry
        lax.fori_loop(0, n2, body, 0, unroll=4)


def _fft_filter_spectrum(ad, ft):
    _, l, n = ad.shape
    n1, n2, cc = FFT_INNER, ft["n2"], FFT_COLS
    h = n2 // 2
    res = lambda a: pl.BlockSpec(a.shape, lambda j: (0,) * a.ndim)
    out = pl.BlockSpec((n2, n1, cc), lambda j: (0, 0, j))
    tabs = [ft["f1r_hi"], ft["f1r_lo"], ft["m_hi"], ft["m_lo"]]
    return pl.pallas_call(
        _fftspec_kernel, name="fft_filter_spectrum",
        grid=(n // cc,),
        in_specs=[pl.BlockSpec((1, l, cc), lambda j: (0, 0, j)), pl.BlockSpec((1, l, cc), lambda j: (1, 0, j))]
                 + [res(t) for t in tabs],
        out_specs=[out, out],
        out_shape=[jax.ShapeDtypeStruct((n2, n1, n), F32)] * 2,
        scratch_shapes=[pltpu.VMEM((n1 * h, cc), F32), pltpu.VMEM((2 * n2 * n1, cc), F32),
                        pltpu.VMEM((2 * n2 * n1, cc), F32)],
        compiler_params=_cparams(("parallel",), 56),
    )(ad, ad, *tabs)


def _softmax_weights(s, sink=None):
    m = jnp.max(s, axis=-1, keepdims=True)
    if sink is not None:
        m = jnp.maximum(m, sink)
    p = jnp.exp(s - m)
    den = jnp.sum(p, axis=-1, keepdims=True)
    if sink is not None:
        den = den + jnp.exp(sink - m)
    return p.astype(BF16), den


def _softmax_pv(s, v, sink=None):
    p, den = _softmax_weights(s, sink)
    return _dot(p, v) / den


def _qk(q, k):
    return lax.dot_general(q, k, (((1,), (1,)), ((), ())), preferred_element_type=F32)


def _half_mask(q_tile, half):
    lane = lax.broadcasted_iota(jnp.int32, q_tile.shape, 1)
    keep = (lane < HEAD_DIM) if half == 0 else (lane >= HEAD_DIM)
    return jnp.where(keep, q_tile, jnp.zeros_like(q_tile))


def _merge_halves(o_even, o_odd):
    lane = lax.broadcasted_iota(jnp.int32, o_even.shape, 1)
    return jnp.where(lane < HEAD_DIM, o_even, o_odd)


def _swa_kernel(sink_ref, q_ref, kp_ref, kc_ref, kn_ref, vp_ref, vc_ref, vn_ref, kx_ref, vx_ref, o_ref):
    n = pl.program_id(1)
    nb = pl.num_programs(1)
    blk = SWA_BLOCK
    g = SWA_HEADS // SWA_KV_HEADS
    n_ctx = kx_ref.shape[1]
    rows = g * blk
    ri = lax.broadcasted_iota(jnp.int32, (rows, 3 * blk + n_ctx), 0) & (blk - 1)
    ci = lax.broadcasted_iota(jnp.int32, (rows, 3 * blk + n_ctx), 1)
    ok_prev = (ci < blk) & (ci >= ri) & (n > 0)
    ok_cur = (ci >= blk) & (ci < 2 * blk)
    ok_next = (ci >= 2 * blk) & (ci < 3 * blk) & (ci - 2 * blk <= ri) & (n < nb - 1)
    ok = ok_prev | ok_cur | ok_next | (ci >= 3 * blk)
    bias = jnp.where(ok, 0.0, NEG_BIG)
    rsel = lax.broadcasted_iota(jnp.int32, (rows, 1), 0) >> (blk.bit_length() - 1)
    outs = [None] * SWA_HEADS
    kvs = range(SWA_KV_HEADS)
    lanes = [slice(kv * LANES, (kv + 1) * LANES) for kv in kvs]
    scores, sinks = [], []
    for kv in kvs:
        cs = lanes[kv]
        k = jnp.concatenate([kp_ref[0, :, cs], kc_ref[0, :, cs], kn_ref[0, :, cs], kx_ref[0, :, cs]], axis=0)
        q_parts = []
        sink = jnp.zeros((rows, 1), F32)
        for gi in range(g):
            h = kv * g + gi
            q_parts.append(_half_mask(q_ref[0, :, (h // 2) * LANES:(h // 2 + 1) * LANES], h % 2))
            sink = jnp.where(rsel == gi, sink_ref[h], sink)
        scores.append(_qk(jnp.concatenate(q_parts, axis=0), k))
        sinks.append(sink)
    probs = [_softmax_weights(scores[kv] + bias, sinks[kv]) for kv in kvs]
    for kv in kvs:
        cs = lanes[kv]
        v = jnp.concatenate([vp_ref[0, :, cs], vc_ref[0, :, cs], vn_ref[0, :, cs], vx_ref[0, :, cs]], axis=0)
        pw, den = probs[kv]
        o = _dot(pw, v) / den
        for gi in range(g):
            outs[kv * g + gi] = o[gi * blk:(gi + 1) * blk]
    for p in range(SWA_HEADS // 2):
        o_ref[0, :, p * LANES:(p + 1) * LANES] = _merge_halves(outs[2 * p], outs[2 * p + 1]).astype(o_ref.dtype)


def _window_gqa(q, k, v, kx, vx, sink):
    b, l, _ = q.shape
    nb = l // SWA_BLOCK
    n_ctx = kx.shape[1]
    kvw = k.shape[2]

    def blk(off):
        return pl.BlockSpec((1, SWA_BLOCK, kvw), lambda b, n: (b, jnp.clip(n + off, 0, nb - 1), 0))

    ctx = pl.BlockSpec((1, n_ctx, kvw), lambda b, n: (b, 0, 0))
    return pl.pallas_call(
        _swa_kernel, name="window_gqa",
        grid=(b, nb),
        in_specs=[
            pl.BlockSpec(memory_space=pltpu.SMEM),
            pl.BlockSpec((1, SWA_BLOCK, q.shape[2]), lambda b, n: (b, n, 0)),
            blk(-1), blk(0), blk(1), blk(-1), blk(0), blk(1), ctx, ctx,
        ],
        out_specs=pl.BlockSpec((1, SWA_BLOCK, q.shape[2]), lambda b, n: (b, n, 0)),
        out_shape=jax.ShapeDtypeStruct(q.shape, BF16),
        compiler_params=_cparams(("parallel", "parallel"), 32),
    )(sink, q, k, k, k, v, v, v, kx, vx)


def _na_bias_kernel(rpb_ref, o_ref):
    off = pl.program_id(0)
    h = pl.program_id(1)
    nd = 2 * NA_COLS - 1
    q = lax.broadcasted_iota(jnp.int32, (GRID_W, LANES), 0)
    lane = lax.broadcasted_iota(jnp.int32, (GRID_W, LANES), 1)
    kc = lane & (GRID_W - 1)
    upper = lane >= GRID_W
    cstart = jnp.clip(q - NA_COLS // 2, 0, GRID_W - NA_COLS)
    valid = (kc >= cstart) & (kc < cstart + NA_COLS)
    dc = jnp.clip(kc - q, -(NA_COLS - 1), NA_COLS - 1) + (NA_COLS - 1)
    for j in range(NA_MAX_ROWS // 2):
        base0 = (h * (2 * NA_MAX_ROWS - 1) + (2 * j - off + NA_MAX_ROWS - 1)) * nd
        base1 = base0 + nd
        t = jnp.zeros((GRID_W, LANES), F32)
        for d in range(nd):
            val = jnp.where(upper, rpb_ref[base1 + d], rpb_ref[base0 + d])
            t = jnp.where(dc == d, val, t)
        o_ref[0, 0, :, j * LANES:(j + 1) * LANES] = jnp.where(valid, t, NEG_BIG)


def _na_bias_table(rpb):
    return pl.pallas_call(
        _na_bias_kernel, name="na_bias_table",
        grid=(NA_MAX_ROWS, NA_HEADS),
        in_specs=[pl.BlockSpec(memory_space=pltpu.SMEM)],
        out_specs=pl.BlockSpec((1, 1, GRID_W, NA_MAX_ROWS * GRID_W), lambda o, h: (o, h, 0, 0)),
        out_shape=jax.ShapeDtypeStruct((NA_MAX_ROWS, NA_HEADS, GRID_W, NA_MAX_ROWS * GRID_W), F32),
        compiler_params=_cparams(("arbitrary", "arbitrary"), 16),
    )(rpb.reshape(-1))


NA_ROWS_PER_STEP = 4


def _na_kernel(q_ref, k_ref, v_ref, kx_ref, vx_ref, bias_ref, o_ref, *, rows):
    wr = NA_MAX_ROWS
    nloc = wr * GRID_W
    for i in range(NA_ROWS_PER_STEP):
        r = pl.program_id(1) * NA_ROWS_PER_STEP + i
        first = jnp.clip(r - wr // 2, 0, rows - wr)
        start = pl.multiple_of(first * GRID_W, GRID_W)
        off = r - first
        qr = slice(i * GRID_W, (i + 1) * GRID_W)
        pairs = range(NA_HEADS // 2)
        lanes = [slice(p * LANES, (p + 1) * LANES) for p in pairs]
        scores = []
        for p in pairs:
            q_tile = q_ref[0, qr, lanes[p]]
            q2 = jnp.concatenate([_half_mask(q_tile, 0), _half_mask(q_tile, 1)], axis=0)
            k = jnp.concatenate([k_ref[0, pl.ds(start, nloc), lanes[p]], kx_ref[0, :, lanes[p]]], axis=0)
            scores.append(_qk(q2, k))
        probs = []
        for p in pairs:
            s = scores[p]
            bias = jnp.concatenate([bias_ref[off, 2 * p], bias_ref[off, 2 * p + 1]], axis=0)
            probs.append(_softmax_weights(jnp.concatenate([s[:, :nloc] + bias, s[:, nloc:]], axis=1)))
        for p in pairs:
            pw, den = probs[p]
            v = jnp.concatenate([v_ref[0, pl.ds(start, nloc), lanes[p]], vx_ref[0, :, lanes[p]]], axis=0)
            o = _dot(pw, v) / den
            o_ref[0, qr, lanes[p]] = _merge_halves(o[:GRID_W], o[GRID_W:]).astype(o_ref.dtype)


def _neighbourhood_attention(q, k, v, kx, vx, bias_tab):
    b, l, w = q.shape
    rows = l // GRID_W
    n_ctx = kx.shape[1]
    tq = NA_ROWS_PER_STEP * GRID_W
    full = pl.BlockSpec((1, l, w), lambda b, r: (b, 0, 0))
    ctx = pl.BlockSpec((1, n_ctx, w), lambda b, r: (b, 0, 0))
    return pl.pallas_call(
        functools.partial(_na_kernel, rows=rows), name="neighbourhood_attention",
        grid=(b, rows // NA_ROWS_PER_STEP),
        in_specs=[pl.BlockSpec((1, tq, w), lambda b, r: (b, r, 0)), full, full, ctx, ctx,
                  _resident(bias_tab.shape)],
        out_specs=pl.BlockSpec((1, tq, w), lambda b, r: (b, r, 0)),
        out_shape=jax.ShapeDtypeStruct(q.shape, BF16),
        compiler_params=_cparams(("parallel", "arbitrary"), 48),
    )(q, k, v, kx, vx, bias_tab)


def _ctx_attn_kernel(sink_ref, qs_ref, ks_ref, vs_ref, qn_ref, kn_ref, vn_ref, os_ref, on_ref):
    n = qs_ref.shape[1]
    g = SWA_HEADS // SWA_KV_HEADS
    rsel = lax.broadcasted_iota(jnp.int32, (g * n, 1), 0) >> (n.bit_length() - 1)
    outs = [None] * SWA_HEADS
    for kv in range(SWA_KV_HEADS):
        cs = slice(kv * LANES, (kv + 1) * LANES)
        q_parts = []
        sink = jnp.zeros((g * n, 1), F32)
        for gi in range(g):
            h = kv * g + gi
            q_parts.append(_half_mask(qs_ref[0, :, (h // 2) * LANES:(h // 2 + 1) * LANES], h % 2))
            sink = jnp.where(rsel == gi, sink_ref[h], sink)
        o = _softmax_pv(_qk(jnp.concatenate(q_parts, axis=0), ks_ref[0, :, cs]), vs_ref[0, :, cs], sink)
        for gi in range(g):
            outs[kv * g + gi] = o[gi * n:(gi + 1) * n]
    for p in range(SWA_HEADS // 2):
        os_ref[0, :, p * LANES:(p + 1) * LANES] = _merge_halves(outs[2 * p], outs[2 * p + 1]).astype(os_ref.dtype)
    for p in range(NA_HEADS // 2):
        cs = slice(p * LANES, (p + 1) * LANES)
        q_tile = qn_ref[0, :, cs]
        q2 = jnp.concatenate([_half_mask(q_tile, 0), _half_mask(q_tile, 1)], axis=0)
        o = _softmax_pv(_qk(q2, kn_ref[0, :, cs]), vn_ref[0, :, cs])
        on_ref[0, :, cs] = _merge_halves(o[:n], o[n:]).astype(on_ref.dtype)


def _context_attention(qs, ks, vs, qn, kn, vn, sink):
    b, n, w = qs.shape
    spec = lambda a: pl.BlockSpec((1,) + a.shape[1:], lambda b: (b, 0, 0))
    return pl.pallas_call(
        _ctx_attn_kernel, name="context_attention",
        grid=(b,),
        in_specs=[pl.BlockSpec(memory_space=pltpu.SMEM)] + [spec(a) for a in (qs, ks, vs, qn, kn, vn)],
        out_specs=[spec(qs), spec(qn)],
        out_shape=[jax.ShapeDtypeStruct(qs.shape, BF16), jax.ShapeDtypeStruct(qn.shape, BF16)],
        compiler_params=_cparams(("parallel",), 32),
    )(sink, qs, ks, vs, qn, kn, vn)


def _merge_ffn_kernel(x_ref, mg_ref, yh_ref, ys_ref, yn_ref, zg_ref, wb_ref, wo_ref,
                      sh_ref, sc_ref, gt_ref, g_ref, wg_ref, wu_ref, wd_ref, o_ref, *, fc):
    d = D_MODEL
    ys = (yh_ref[0].astype(BF16), ys_ref[0], yn_ref[0])
    m = None
    for n in range(3):
        proj = _dot(ys[n], wb_ref[n])
        term = zg_ref[0, :, n * d:(n + 1) * d].astype(F32) * proj
        m = term if m is None else m + term
    x = x_ref[0] + mg_ref[0] * _dot(m.astype(BF16), wo_ref[...])
    o_ref[0] = _swiglu_residual(x, sh_ref, sc_ref, gt_ref, g_ref, wg_ref, wu_ref, wd_ref, fc)


def _merge_and_ffn(x, mix_gate, y_hy, y_swa, y_na, gates, w_branch, w_out, layer,
                   shift, scale, gate, gain, wg, wu, wd):
    b, l, d = x.shape
    tm = min(512, l)
    tok = lambda n: pl.BlockSpec((1, tm, n), lambda b, t: (b, t, 0))
    idx = (layer, 1)
    return pl.pallas_call(
        functools.partial(_merge_ffn_kernel, fc=FFN_HIDDEN // 2), name="merge_ffn",
        grid=(b, l // tm),
        in_specs=[tok(d), _mod_spec(mix_gate), tok(HY_WIDTH), tok(HY_WIDTH), tok(HY_WIDTH), tok(3 * d),
                  _layer_slab(w_branch, (layer,)), _layer_slab(w_out, (layer,)),
                  _mod_spec(shift), _mod_spec(scale), _mod_spec(gate), _resident((1, d)),
                  _layer_slab(wg, idx), _layer_slab(wu, idx), _layer_slab(wd, idx)],
        out_specs=tok(d),
        out_shape=jax.ShapeDtypeStruct(x.shape, F32),
        compiler_params=_cparams(("parallel", "parallel"), 58),
    )(x, mix_gate, y_hy, y_swa, y_na, gates, w_branch, w_out, shift, scale, gate, gain.reshape(1, d), wg, wu, wd)


def _dft_tables(l):
    n = 2 * l
    fb = 256 if l >= 1024 else LANES * ((l + 1 + LANES - 1) // LANES)
    fp = fb * ((l + 1 + fb - 1) // fb)
    k = jnp.arange(fp, dtype=jnp.int32)[:, None]
    t = jnp.arange(l, dtype=jnp.int32)[None, :]
    ang = ((k * t) % n).astype(F32) * (2.0 * math.pi / n)
    live = k <= l
    c = jnp.where(live, jnp.cos(ang), 0.0)
    s = jnp.where(live, jnp.sin(ang), 0.0)
    c_hi, c_lo = _split(c)
    s_hi, s_lo = _split(s)
    wk = jnp.where((k == 0) | (k == l), 1.0, 2.0) * jnp.where(live, 1.0 / n, 0.0)
    return dict(fb=fb, c_hi=c_hi, c_lo=c_lo, s_hi=s_hi, s_lo=s_lo,
                ct_hi=c_hi.T, ct_lo=c_lo.T, st_hi=s_hi.T, st_lo=s_lo.T, wk=wk.astype(F32))


def _filter_features(l):
    t = jnp.linspace(0.0, 1.0, l, dtype=F32)[:, None]
    w = (2.0 * math.pi / l) * jnp.arange(l, dtype=F32)[:, None]
    f = jnp.linspace(1e-4, HY_BANDS - 1, HY_BANDS, dtype=F32)[None, :]
    z = jnp.concatenate([t, jnp.cos(f * w), -jnp.sin(f * w)], axis=-1)
    return jnp.pad(z, ((0, 0), (0, LANES - HY_EMB)))


def _decay_rates():
    max_decay = math.log(HY_TARGET) / HY_FAST_DECAY
    min_decay = math.log(HY_TARGET) / HY_SLOW_DECAY
    return jnp.abs(jnp.linspace(min_decay, max_decay, HY_WIDTH, dtype=F32))[None, :]


def _rope_tables(l):
    pos = jnp.arange(l)
    row = (pos // GRID_W).astype(F32)
    col = (pos % GRID_W).astype(F32)
    half = HEAD_DIM // 2
    inv = 1.0 / (ROPE_BASE ** (jnp.arange(0, half, 2, dtype=F32) / half))
    ar = row[:, None] * inv[None, :]
    ac = col[:, None] * inv[None, :]
    cos = jnp.concatenate([jnp.cos(ar), jnp.cos(ar), jnp.cos(ac), jnp.cos(ac)], axis=-1)
    sin = jnp.concatenate([-jnp.sin(ar), jnp.sin(ar), -jnp.sin(ac), jnp.sin(ac)], axis=-1)
    return jnp.tile(cos, (1, 2)), jnp.tile(sin, (1, 2))


def _pad_to(a, shape):
    return jnp.pad(a, [(0, s - d) for d, s in zip(a.shape, shape)])


def _use_fft(bsz, length):
    return bsz % 2 == 0 and length % (8 * FFT_INNER) == 0


def _conv_tables(bsz, length):
    return _fft_tables(length) if _use_fft(bsz, length) else _dft_tables(length)


def _hyena_branch(z_hy, short_w, short_b, feat, fparams, deltas, bias, tabs):
    ad = _hyena_filter_sums(feat, fparams, deltas)
    if "n2" in tabs:
        kr, ki = _fft_filter_spectrum(ad, tabs)
        y1 = _fft_gated_long_conv(None, 0, z_hy, 1, short_w, short_b, kr, ki, 0, bias[0], tabs, F32)
        return _fft_gated_long_conv(y1, 0, z_hy, 2, short_w, short_b, kr, ki, 1, bias[1], tabs, BF16)
    v, x1, x2 = _short_conv3(z_hy, short_w, short_b)
    kr, ki = _filter_spectrum(ad, tabs)
    conv = functools.partial(_gated_long_conv, tabs=tabs)
    y1 = conv(v, x1, kr, ki, 0, bias[0])
    return conv(y1, x2, kr, ki, 1, bias[1])


def kernel(x, c, ctx, c_ctx, w_ada, b_ada, norm_g, ffn_w_gate, ffn_w_up, ffn_w_down,
           w_in, hy_short_w, hy_short_b, hy_pe_w0, hy_pe_b0, hy_pe_w1, hy_pe_b1,
           hy_pe_w2, hy_pe_b2, hy_pe_wout, hy_sin_freq, hy_bias,
           swa_q_gain, swa_k_gain, swa_sink, na_q_gain, na_k_gain, na_rpb,
           w_branch, w_out):
    bsz, seq, d = x.shape
    n_ctx = ctx.shape[1]
    depth = w_ada.shape[0]

    c16 = _pad_to(jnp.concatenate([c, c_ctx[None, :]], axis=0), (MOD_ROWS, d))
    mods = _adaln_mods(c16, w_ada, b_ada).reshape(depth, MOD_ROWS, N_MOD, 1, d)

    tabs_x = _conv_tables(bsz, seq)
    tabs_c = _conv_tables(bsz, n_ctx)
    feat_x = _filter_features(seq)
    feat_c = _filter_features(n_ctx)
    deltas = _decay_rates()
    cos_x, sin_x = _rope_tables(seq)
    cos_c = jnp.ones((n_ctx, LANES), F32)
    sin_c = jnp.zeros((n_ctx, LANES), F32)
    eye = jnp.arange(MXU_DIM) // HEAD_DIM
    bd = (eye[:, None] == eye[None, :]).astype(BF16)

    wg = ffn_w_gate.astype(BF16)
    wu = ffn_w_up.astype(BF16)
    wd = ffn_w_down.astype(BF16)
    wb = w_branch.astype(BF16)
    wo = w_out.astype(BF16)
    win = w_in.astype(BF16)
    kv = win[:, :, _OFF_SWA + _SWA_Q:_OFF_NA].reshape(depth, d, 2, SWA_KV_HEADS, 1, HEAD_DIM)
    win_kv = jnp.broadcast_to(kv, (depth, d, 2, SWA_KV_HEADS, 2, HEAD_DIM)).reshape(depth, d, 4 * _SWA_KV)

    xc = ctx
    for i in range(depth):
        last = i == depth - 1
        mx = lambda j: mods[i, :bsz, j]
        mc = lambda j: mods[i, bsz:bsz + 1, j]
        tile2 = lambda g: jnp.tile(g, 2 * MXU_DIM // LANES)
        head_gains = jnp.stack([tile2(swa_q_gain[i]), tile2(swa_k_gain[i]), tile2(na_q_gain[i]), tile2(na_k_gain[i])])
        fparams = (_pad_to(hy_pe_w0[i], (LANES, LANES)), _pad_to(hy_pe_b0[i][None], (1, LANES)),
                   _pad_to(hy_pe_w1[i], (LANES, LANES)), _pad_to(hy_pe_b1[i][None], (1, LANES)),
                   _pad_to(hy_pe_w2[i], (LANES, LANES)), _pad_to(hy_pe_b2[i][None], (1, LANES)),
                   _pad_to(hy_pe_wout[i], (LANES, 4 * HY_WIDTH)), _pad_to(hy_sin_freq[i][None], (1, LANES)))

        x = _ffn_half_step(x, mx(0), mx(1), mx(2), norm_g[i, 0], wg, wu, wd, (i, 0))
        xc = _ffn_half_step(xc, mc(0), mc(1), mc(2), norm_g[i, 0], wg, wu, wd, (i, 0))

        z_hy, q_s, k_s, v_s, q_n, k_n, v_n, gates = _in_projection(
            x, mx(3), mx(4), norm_g[i, 1], cos_x, sin_x, head_gains, bd, win, win_kv, i)
        zc_hy, qc_s, kc_s, vc_s, qc_n, kc_n, vc_n, gates_c = _in_projection(
            xc, mc(3), mc(4), norm_g[i, 1], cos_c, sin_c, head_gains, bd, win, win_kv, i)

        y_hy = _hyena_branch(z_hy, hy_short_w[i], hy_short_b[i], feat_x, fparams, deltas, hy_bias[i], tabs_x)
        y_swa = _window_gqa(q_s, k_s, v_s, kc_s, vc_s, swa_sink[i])
        y_na = _neighbourhood_attention(q_n, k_n, v_n, kc_n, vc_n, _na_bias_table(na_rpb[i]))
        x = _merge_and_ffn(x, mx(5), y_hy, y_swa, y_na, gates, wb, wo, i,
                           mx(6), mx(7), mx(8), norm_g[i, 2], wg, wu, wd)

        if not last:
            yc_hy = _hyena_branch(zc_hy, hy_short_w[i], hy_short_b[i], feat_c, fparams, deltas, hy_bias[i], tabs_c)
            yc_swa, yc_na = _context_attention(qc_s, kc_s, vc_s, qc_n, kc_n, vc_n, swa_sink[i])
            xc = _merge_and_ffn(xc, mc(5), yc_hy, yc_swa, yc_na, gates_c, wb, wo, i,
                                mc(6), mc(7), mc(8), norm_g[i, 2], wg, wu, wd)
    return x
```

```python
import functools
import math

import jax
import jax.numpy as jnp
from jax import lax
from jax.experimental import pallas as pl
from jax.experimental.pallas import tpu as pltpu

F32 = jnp.float32
BF16 = jnp.bfloat16

D_MODEL = 1024
DEPTH = 4
GRID_W = 64
HEAD_DIM = 64
N_MOD = 9
RMS_EPS = 1e-6

HY_WIDTH = D_MODEL // 2
HY_EMB = 33
HY_BANDS = (HY_EMB - 1) // 2
HY_FFN = 64
HY_FAST_DECAY = 0.3
HY_SLOW_DECAY = 1.5
HY_TARGET = 1e-2

SWA_HEADS = 8
SWA_KV_HEADS = 2
SWA_WINDOW = 128
SWA_BLOCK = 128
ROPE_BASE = 10000.0

NA_HEADS = 8
NA_MAX_ROWS = 8
NA_COLS = 16
NA_COL_BLOCK = 16

FFN_HIDDEN = 256 * ((8 * D_MODEL // 3 + 255) // 256)
FFN_CHUNK = 1536

LANES = 128
MXU_DIM = 256
MOD_ROWS = 16
NEG_BIG = -1e30


def _cparams(sem, vmem_mb):
    return pltpu.CompilerParams(dimension_semantics=sem, vmem_limit_bytes=vmem_mb * 1024 * 1024)


def _resident(shape):
    nd = len(shape)
    return pl.BlockSpec(shape, lambda *_: (0,) * nd, pipeline_mode=pl.Buffered(1))


def _layer_slab(arr, idx):
    rest = arr.shape[len(idx):]
    return pl.BlockSpec((None,) * len(idx) + rest, lambda *_: tuple(idx) + (0,) * len(rest),
                        pipeline_mode=pl.Buffered(1))


def _split(x):
    hi = x.astype(BF16)
    lo = (x - hi.astype(F32)).astype(BF16)
    return hi, lo


def _dot(a, b):
    return jnp.dot(a, b, preferred_element_type=F32)


def _dot3(a_hi, a_lo, b_hi, b_lo):
    return _dot(a_hi, b_hi) + _dot(a_lo, b_hi) + _dot(a_hi, b_lo)


def _dot3f(a, b):
    a_hi, a_lo = _split(a)
    b_hi, b_lo = _split(b)
    return _dot3(a_hi, a_lo, b_hi, b_lo)


def _rms_mod(x, gain, shift, scale):
    ms = jnp.mean(x * x, axis=-1, keepdims=True)
    y = x * lax.rsqrt(ms + RMS_EPS) * gain
    return y * (1.0 + scale) + shift


def _head_rms(x, bd, gain):
    outs = []
    for j in range(x.shape[1] // MXU_DIM):
        xc = x[:, j * MXU_DIM:(j + 1) * MXU_DIM]
        hi, lo = _split(xc * xc)
        ssum = _dot(hi, bd) + _dot(lo, bd)
        outs.append(xc * lax.rsqrt(ssum * (1.0 / HEAD_DIM) + RMS_EPS) * gain)
    return outs[0] if len(outs) == 1 else jnp.concatenate(outs, axis=1)


def _rope(x, cos, sin):
    lane = lax.broadcasted_iota(jnp.int32, (x.shape[0], LANES), 1)
    first = (lane & 31) < 16
    outs = []
    for j in range(x.shape[1] // LANES):
        xc = x[:, j * LANES:(j + 1) * LANES]
        partner = jnp.where(first, pltpu.roll(xc, LANES - 16, axis=1), pltpu.roll(xc, 16, axis=1))
        outs.append(xc * cos + partner * sin)
    return outs[0] if len(outs) == 1 else jnp.concatenate(outs, axis=1)


def _mods_kernel(c_ref, w_ref, b_ref, o_ref):
    c = c_ref[...]
    a = c * jax.nn.sigmoid(c)
    o_ref[0] = _dot3f(a, w_ref[0]) + b_ref[0]


def _adaln_mods(c16, w_ada, b_ada):
    depth, d, n = w_ada.shape
    nb = 1152
    return pl.pallas_call(
        _mods_kernel, name="adaln_mods",
        grid=(depth, n // nb),
        in_specs=[
            pl.BlockSpec((MOD_ROWS, d), lambda i, j: (0, 0)),
            pl.BlockSpec((1, d, nb), lambda i, j: (i, 0, j)),
            pl.BlockSpec((1, 1, nb), lambda i, j: (i, 0, j)),
        ],
        out_specs=pl.BlockSpec((1, MOD_ROWS, nb), lambda i, j: (i, 0, j)),
        out_shape=jax.ShapeDtypeStruct((depth, MOD_ROWS, n), F32),
        compiler_params=_cparams(("arbitrary", "arbitrary"), 48),
    )(c16, w_ada, b_ada.reshape(depth, 1, n))


def _swiglu_residual(x, sh_ref, sc_ref, gt_ref, g_ref, wg_ref, wu_ref, wd_ref, fc):
    hb = _rms_mod(x, g_ref[...], sh_ref[0], sc_ref[0]).astype(BF16)
    acc = None
    for f0 in range(0, FFN_HIDDEN, fc):
        f1 = min(f0 + fc, FFN_HIDDEN)
        g = _dot(hb, wg_ref[:, f0:f1])
        u = _dot(hb, wu_ref[:, f0:f1])
        a = (g * jax.nn.sigmoid(g) * u).astype(BF16)
        d = _dot(a, wd_ref[f0:f1, :])
        acc = d if acc is None else acc + d
    return x + 0.5 * gt_ref[0] * acc


def _ffn_kernel(x_ref, sh_ref, sc_ref, gt_ref, g_ref, wg_ref, wu_ref, wd_ref, o_ref, *, fc):
    o_ref[0] = _swiglu_residual(x_ref[0], sh_ref, sc_ref, gt_ref, g_ref, wg_ref, wu_ref, wd_ref, fc)


def _mod_spec(mod):
    if mod.shape[0] == 1:
        return pl.BlockSpec((1, 1, D_MODEL), lambda b, t: (0, 0, 0))
    return pl.BlockSpec((1, 1, D_MODEL), lambda b, t: (b, 0, 0))


def _ffn_half_step(x, shift, scale, gate, gain, wg, wu, wd, idx):
    b, l, d = x.shape
    tm = min(512, l)
    return pl.pallas_call(
        functools.partial(_ffn_kernel, fc=FFN_CHUNK), name="ffn_half_step",
        grid=(b, l // tm),
        in_specs=[
            pl.BlockSpec((1, tm, d), lambda b, t: (b, t, 0)),
            _mod_spec(shift), _mod_spec(scale), _mod_spec(gate),
            _resident((1, d)),
            _layer_slab(wg, idx), _layer_slab(wu, idx), _layer_slab(wd, idx),
        ],
        out_specs=pl.BlockSpec((1, tm, d), lambda b, t: (b, t, 0)),
        out_shape=jax.ShapeDtypeStruct(x.shape, F32),
        compiler_params=_cparams(("parallel", "parallel"), 56),
    )(x, shift, scale, gate, gain.reshape(1, d), wg, wu, wd)


_HY_COLS = 3 * HY_WIDTH
_SWA_Q = SWA_HEADS * HEAD_DIM
_SWA_KV = SWA_KV_HEADS * HEAD_DIM
_NA_W = NA_HEADS * HEAD_DIM
_OFF_SWA = _HY_COLS
_OFF_NA = _OFF_SWA + _SWA_Q + 2 * _SWA_KV
_OFF_GATE = _OFF_NA + 3 * _NA_W


def _inproj_kernel(x_ref, sh_ref, sc_ref, g_ref, cos_ref, sin_ref, hg_ref, bd_ref, w_ref, wkv_ref,
                   zhy_ref, qs_ref, ks_ref, vs_ref, qn_ref, kn_ref, vn_ref, gt_ref):
    hb = _rms_mod(x_ref[0], g_ref[...], sh_ref[0], sc_ref[0]).astype(BF16)
    cos = cos_ref[...]
    sin = sin_ref[...]
    bd = bd_ref[...]
    scale = HEAD_DIM ** -0.5
    proj = lambda lo, n: _dot(hb, w_ref[:, lo:lo + n])
    kvw = 2 * _SWA_KV
    qs = proj(_OFF_SWA, _SWA_Q)
    ks = _dot(hb, wkv_ref[:, :kvw])
    qn = proj(_OFF_NA, _NA_W)
    kn = proj(_OFF_NA + _NA_W, _NA_W)
    vs_ref[0] = _dot(hb, wkv_ref[:, kvw:]).astype(BF16)
    vn_ref[0] = proj(_OFF_NA + 2 * _NA_W, _NA_W).astype(BF16)
    zhy_ref[0] = proj(0, _HY_COLS)
    gt_ref[0] = jax.nn.sigmoid(proj(_OFF_GATE, 3 * D_MODEL)).astype(BF16)
    qs_ref[0] = (_rope(_head_rms(qs, bd, hg_ref[0:1, :]), cos, sin) * scale).astype(BF16)
    ks_ref[0] = _rope(_head_rms(ks, bd, hg_ref[1:2, :]), cos, sin).astype(BF16)
    qn_ref[0] = (_head_rms(qn, bd, hg_ref[2:3, :]) * scale).astype(BF16)
    kn_ref[0] = _head_rms(kn, bd, hg_ref[3:4, :]).astype(BF16)


def _in_projection(x, shift, scale, gain, cos, sin, head_gains, bd, w, w_kv_dup, layer):
    b, l, d = x.shape
    tm = min(512, l)
    kvw = 2 * _SWA_KV
    widths = [_HY_COLS, _SWA_Q, kvw, kvw, _NA_W, _NA_W, _NA_W, 3 * d]
    dtypes = [F32] + [BF16] * 7
    return pl.pallas_call(
        _inproj_kernel, name="in_projection",
        grid=(b, l // tm),
        in_specs=[
            pl.BlockSpec((1, tm, d), lambda b, t: (b, t, 0)),
            _mod_spec(shift), _mod_spec(scale),
            _resident((1, d)),
            pl.BlockSpec((tm, LANES), lambda b, t: (t, 0)),
            pl.BlockSpec((tm, LANES), lambda b, t: (t, 0)),
            _resident(head_gains.shape), _resident(bd.shape), _layer_slab(w, (layer,)), _layer_slab(w_kv_dup, (layer,)),
        ],
        out_specs=[pl.BlockSpec((1, tm, n), lambda b, t: (b, t, 0)) for n in widths],
        out_shape=[jax.ShapeDtypeStruct((b, l, n), dt) for n, dt in zip(widths, dtypes)],
        compiler_params=_cparams(("parallel", "parallel"), 56),
    )(x, shift, scale, gain.reshape(1, d), cos, sin, head_gains, bd, w, w_kv_dup)


def _shortconv_kernel(z0_ref, z1_ref, z2_ref, w_ref, b_ref, o0_ref, o1_ref, o2_ref):
    l = z0_ref.shape[1]
    row = lax.broadcasted_iota(jnp.int32, z0_ref.shape[1:], 0)
    for g, (z_ref, o_ref) in enumerate(((z0_ref, o0_ref), (z1_ref, o1_ref), (z2_ref, o2_ref))):
        z = z_ref[0]
        prev = jnp.where(row == 0, 0.0, pltpu.roll(z, 1, axis=0))
        nxt = jnp.where(row == l - 1, 0.0, pltpu.roll(z, l - 1, axis=0))
        o_ref[0] = w_ref[g, 0:1, :] * prev + w_ref[g, 1:2, :] * z + w_ref[g, 2:3, :] * nxt + b_ref[g]


def _short_conv3(z, w, bias):
    b, l, n = z.shape
    cb = 256
    ncb = HY_WIDTH // cb
    wg = w.reshape(3, 3, HY_WIDTH).transpose(1, 0, 2)
    zin = lambda g: pl.BlockSpec((1, l, cb), lambda b, j: (b, 0, g * ncb + j))
    out = pl.BlockSpec((1, l, cb), lambda b, j: (b, 0, j))
    return pl.pallas_call(
        _shortconv_kernel, name="short_conv3",
        grid=(b, ncb),
        in_specs=[zin(0), zin(1), zin(2),
                  pl.BlockSpec((3, 3, cb), lambda b, j: (0, 0, j)),
                  pl.BlockSpec((3, 1, cb), lambda b, j: (0, 0, j))],
        out_specs=[out, out, out],
        out_shape=[jax.ShapeDtypeStruct((b, l, HY_WIDTH), F32)] * 3,
        compiler_params=_cparams(("parallel", "parallel"), 48),
    )(z, z, z, wg, bias.reshape(3, 1, HY_WIDTH))


def _filter_kernel(z_ref, w0, b0, w1, b1, w2, b2, wo, fr_ref, dl_ref, o_ref):
    z = z_ref[...]
    fr = fr_ref[...]
    a = jnp.sin(fr * (_dot3f(z, w0[...]) + b0[...]))
    a = jnp.sin(fr * (_dot3f(a, w1[...]) + b1[...]))
    a = jnp.sin(fr * (_dot3f(a, w2[...]) + b2[...]))
    hh = _dot3f(a, wo[...])
    t = z[:, 0:1]
    win = jnp.exp(-t * dl_ref[...])
    row = lax.broadcasted_iota(jnp.int32, win.shape, 0) + pl.program_id(0) * z.shape[0]
    w = HY_WIDTH
    for o in range(2):
        hp = hh[:, (2 * o) * w:(2 * o + 1) * w] * win
        hn = jnp.where(row == 0, 0.0, hh[:, (2 * o + 1) * w:(2 * o + 2) * w] * win)
        o_ref[0, :, o * w:(o + 1) * w] = hp + hn
        o_ref[1, :, o * w:(o + 1) * w] = hp - hn


def _hyena_filter_sums(zfeat, fp, deltas):
    l = zfeat.shape[0]
    tl = min(256, l)
    w0, b0, w1, b1, w2, b2, wo, fr = fp
    n = 2 * HY_WIDTH
    consts = [w0, b0, w1, b1, w2, b2, wo, fr, deltas]
    return pl.pallas_call(
        _filter_kernel, name="hyena_filter",
        grid=(l // tl,),
        in_specs=[pl.BlockSpec((tl, LANES), lambda t: (t, 0))] + [_resident(c.shape) for c in consts],
        out_specs=pl.BlockSpec((2, tl, n), lambda t: (0, t, 0)),
        out_shape=jax.ShapeDtypeStruct((2, l, n), F32),
        compiler_params=_cparams(("parallel",), 32),
    )(zfeat, *consts)


def _spectrum_kernel(a_ref, d_ref, ch, cl, sh, sl, kr_ref, ki_ref):
    a_hi, a_lo = _split(a_ref[0])
    d_hi, d_lo = _split(d_ref[0])
    kr_ref[...] = _dot3(ch[...], cl[...], a_hi, a_lo)
    ki_ref[...] = -_dot3(sh[...], sl[...], d_hi, d_lo)


def _filter_spectrum(ad, tabs):
    _, l, n = ad.shape
    fp = tabs["c_hi"].shape[0]
    fb = tabs["fb"]
    cb = 256
    tab = pl.BlockSpec((fb, l), lambda j, k: (k, 0))
    out = pl.BlockSpec((fb, cb), lambda j, k: (k, j))
    return pl.pallas_call(
        _spectrum_kernel, name="dense_spectrum",
        grid=(n // cb, fp // fb),
        in_specs=[pl.BlockSpec((1, l, cb), lambda j, k: (0, 0, j)),
                  pl.BlockSpec((1, l, cb), lambda j, k: (1, 0, j)), tab, tab, tab, tab],
        out_specs=[out, out],
        out_shape=[jax.ShapeDtypeStruct((fp, n), F32)] * 2,
        compiler_params=_cparams(("parallel", "parallel"), 48),
    )(ad, ad, tabs["c_hi"], tabs["c_lo"], tabs["s_hi"], tabs["s_lo"])


def _longconv_kernel(u_ref, g_ref, kr_ref, ki_ref, bias_ref, wk_ref, ch, cl, sh, sl, cth, ctl, sth, stl,
                     o_ref, uh_ref, ul_ref, acc_ref):
    kb = pl.program_id(2)

    @pl.when(kb == 0)
    def _():
        hi, lo = _split(u_ref[0])
        uh_ref[...] = hi
        ul_ref[...] = lo
        acc_ref[...] = jnp.zeros_like(acc_ref)

    uh = uh_ref[...]
    ul = ul_ref[...]
    xr = _dot3(ch[...], cl[...], uh, ul)
    xi = -_dot3(sh[...], sl[...], uh, ul)
    kr = kr_ref[...]
    ki = ki_ref[...]
    wk = wk_ref[...]
    yr_hi, yr_lo = _split((xr * kr - xi * ki) * wk)
    yi_hi, yi_lo = _split((xr * ki + xi * kr) * wk)
    acc_ref[...] += _dot3(cth[...], ctl[...], yr_hi, yr_lo) - _dot3(sth[...], stl[...], yi_hi, yi_lo)

    @pl.when(kb == pl.num_programs(2) - 1)
    def _():
        u = u_ref[0]
        o_ref[0] = g_ref[0] * (acc_ref[...] + u * bias_ref[...])


def _gated_long_conv(u_arr, g_arr, kr, ki, k_col, bias, tabs):
    b, l, _ = u_arr.shape
    cb = 256
    ncb = HY_WIDTH // cb
    fp = tabs["c_hi"].shape[0]
    fb = tabs["fb"]
    tab = pl.BlockSpec((fb, l), lambda b, j, k: (k, 0))
    tabt = pl.BlockSpec((l, fb), lambda b, j, k: (0, k))
    spec = pl.BlockSpec((fb, cb), lambda b, j, k: (k, k_col * ncb + j))
    return pl.pallas_call(
        _longconv_kernel, name="dense_long_conv",
        grid=(b, ncb, fp // fb),
        in_specs=[
            pl.BlockSpec((1, l, cb), lambda b, j, k: (b, 0, j)),
            pl.BlockSpec((1, l, cb), lambda b, j, k: (b, 0, j)),
            spec, spec,
            pl.BlockSpec((1, cb), lambda b, j, k: (0, j)),
            pl.BlockSpec((fb, 1), lambda b, j, k: (k, 0)),
            tab, tab, tab, tab, tabt, tabt, tabt, tabt,
        ],
        out_specs=pl.BlockSpec((1, l, cb), lambda b, j, k: (b, 0, j)),
        out_shape=jax.ShapeDtypeStruct((b, l, HY_WIDTH), F32),
        scratch_shapes=[pltpu.VMEM((l, cb), BF16), pltpu.VMEM((l, cb), BF16), pltpu.VMEM((l, cb), F32)],
        compiler_params=_cparams(("parallel", "parallel", "arbitrary"), 56),
    )(u_arr, g_arr, kr, ki, bias.reshape(1, HY_WIDTH), tabs["wk"],
      tabs["c_hi"], tabs["c_lo"], tabs["s_hi"], tabs["s_lo"],
      tabs["ct_hi"], tabs["ct_lo"], tabs["st_hi"], tabs["st_lo"])


FFT_INNER = 64
FFT_COLS = 256
FFT_K2_CHUNK = 16


def _fft_tables(l):
    n = 2 * l
    n1 = FFT_INNER
    n2 = n // n1
    h = n2 // 2
    ang = lambda idx, mod: (idx % mod).astype(F32) * (2.0 * math.pi / mod)
    k2 = jnp.arange(n2, dtype=jnp.int32)
    m2 = jnp.arange(h, dtype=jnp.int32)
    a1 = ang(k2[:, None] * m2[None, :], n2)
    c1, s1 = jnp.cos(a1), jnp.sin(a1)
    f1 = jnp.block([[c1, s1], [-s1, c1]])
    k1 = jnp.arange(n1, dtype=jnp.int32)
    a2 = ang(k1[None, None, :] * (k1[None, :, None] * n2 + k2[:, None, None]), n)
    c2, s2 = jnp.cos(a2), jnp.sin(a2)
    m = jnp.concatenate([jnp.concatenate([c2, s2], axis=2), jnp.concatenate([-s2, c2], axis=2)], axis=1)
    g = jnp.block([[c1.T, -s1.T], [s1.T, c1.T]]) * (1.0 / n)
    out = dict(n2=n2)
    for name, t in (("f1", f1), ("f1r", f1[:, :h]), ("m", m), ("mt", jnp.swapaxes(m, 1, 2)), ("g", g)):
        out[name + "_hi"] = t.astype(BF16)
    return out


def _mm(w_hi, w_lo, x):
    if w_lo is None:
        return _dot(w_hi, x.astype(BF16))
    hi, lo = _split(x)
    return _dot3(w_hi, w_lo, hi, lo)


def _row_block_matmul(w_hi, w_lo, src_ref, dst_ref, count):
    m, k = w_hi.shape

    def body(i, carry):
        x = src_ref[pl.ds(pl.multiple_of(i * k, k), k), :]
        dst_ref[pl.ds(pl.multiple_of(i * m, m), m), :] = _mm(w_hi, w_lo, x)
        return carry
    lax.fori_loop(0, count, body, 0, unroll=4)


def _short_conv_rows(z, w_ref, b_ref):
    l = z.shape[0]
    row = lax.broadcasted_iota(jnp.int32, z.shape, 0)
    prev = jnp.where(row == 0, 0.0, pltpu.roll(z, 1, axis=0))
    nxt = jnp.where(row == l - 1, 0.0, pltpu.roll(z, l - 1, axis=0))
    return w_ref[0:1, :] * prev + w_ref[1:2, :] * z + w_ref[2:3, :] * nxt + b_ref[...]


def _fftconv_kernel(u_ref, g_ref, uw_ref, ub_ref, gw_ref, gb_ref, kr_ref, ki_ref, bias_ref,
                    f1_ref, m_ref, mt_ref, gm_ref, o_ref, z_scr, p_scr, q_scr, *, u_conv):
    kc = pl.program_id(2)
    n1 = FFT_INNER
    cc = z_scr.shape[1]
    n2 = z_scr.shape[0] // n1
    h = n2 // 2

    def load_u(s):
        return _short_conv_rows(u_ref[s, 0], uw_ref, ub_ref) if u_conv else u_ref[s, 0]

    @pl.when(kc == 0)
    def _stage1():
        z = [jnp.swapaxes(load_u(s).reshape(h, n1, cc), 0, 1) for s in range(2)]
        z_scr[...] = jnp.concatenate(z, axis=1).reshape(n1 * n2, cc)
        _row_block_matmul(f1_ref[...], None, z_scr, p_scr, n1)
        q_scr[...] = jnp.swapaxes(p_scr[...].reshape(n1, 2 * n2, cc), 0, 1).reshape(2 * n2 * n1, cc)

    ks = kr_ref.shape[0]
    rows = [(pl.multiple_of((kc * ks + j) * n1, n1), pl.multiple_of((n2 + kc * ks + j) * n1, n1)) for j in range(ks)]
    xs = [_mm(m_ref[j], None, jnp.concatenate([q_scr[pl.ds(r0, n1), :], q_scr[pl.ds(i0, n1), :]], axis=0))
          for j, (r0, i0) in enumerate(rows)]
    ys = []
    for j, x in enumerate(xs):
        xr, xi = x[:n1], x[n1:]
        kr, ki = kr_ref[j], ki_ref[j]
        ys.append(jnp.concatenate([xr * kr - xi * ki, xr * ki + xi * kr], axis=0))
    for j, (r0, i0) in enumerate(rows):
        b = _mm(mt_ref[j], None, ys[j])
        p_scr[pl.ds(r0, n1), :] = b[:n1]
        p_scr[pl.ds(i0, n1), :] = b[n1:]

    @pl.when(kc == pl.num_programs(2) - 1)
    def _stage1_inv():
        q_scr[...] = jnp.swapaxes(p_scr[...].reshape(2 * n2, n1, cc), 0, 1).reshape(2 * n2 * n1, cc)
        _row_block_matmul(gm_ref[...], None, q_scr, z_scr, n1)
        y3 = z_scr[...].reshape(n1, n2, cc)
        for s in range(2):
            y = jnp.swapaxes(y3[:, s * h:(s + 1) * h, :], 0, 1).reshape(h * n1, cc)
            gate = _short_conv_rows(g_ref[s, 0], gw_ref, gb_ref)
            o_ref[s, 0] = (gate * (y + load_u(s) * bias_ref[...])).astype(o_ref.dtype)


def _fft_gated_long_conv(u, u_col, z_hy, g_col, short_w, short_b, kr, ki, k_col, bias, ft, out_dtype):
    b, l, _ = z_hy.shape
    c = HY_WIDTH
    n1, n2, cc, ks = FFT_INNER, ft["n2"], FFT_COLS, FFT_K2_CHUNK
    p, ncc = b // 2, c // cc
    u_conv = u is None
    if u_conv:
        u = z_hy
    tok = lambda col: pl.BlockSpec((2, 1, l, cc), lambda q, j, k: (0, q, 0, col * ncc + j))
    taps = lambda col: pl.BlockSpec((3, cc), lambda q, j, k: (0, col * ncc + j))
    row = lambda col: pl.BlockSpec((1, cc), lambda q, j, k: (0, col * ncc + j))
    filt = pl.BlockSpec((ks, n1, cc), lambda q, j, k: (k, 0, k_col * ncc + j))
    tab = pl.BlockSpec((ks, 2 * n1, 2 * n1), lambda q, j, k: (k, 0, 0))
    res = lambda a: pl.BlockSpec(a.shape, lambda q, j, k: (0,) * a.ndim)
    sb = short_b.reshape(1, -1)
    y = pl.pallas_call(
        functools.partial(_fftconv_kernel, u_conv=u_conv), name="fft_long_conv",
        grid=(p, ncc, n2 // ks),
        in_specs=[tok(u_col), tok(g_col), taps(u_col), row(u_col), taps(g_col), row(g_col), filt, filt, row(0),
                  res(ft["f1_hi"]), tab, tab, res(ft["g_hi"])],
        out_specs=tok(0),
        out_shape=jax.ShapeDtypeStruct((2, p, l, c), out_dtype),
        scratch_shapes=[pltpu.VMEM((n1 * n2, cc), F32), pltpu.VMEM((2 * n2 * n1, cc), F32),
                        pltpu.VMEM((2 * n2 * n1, cc), F32)],
        compiler_params=_cparams(("parallel", "parallel", "arbitrary"), 56),
    )(u.reshape(2, p, l, -1), z_hy.reshape(2, p, l, -1), short_w, sb, short_w, sb, kr, ki, bias.reshape(1, c),
      ft["f1_hi"], ft["m_hi"], ft["mt_hi"], ft["g_hi"])
    return y.reshape(b, l, c)


def _fftspec_kernel(a_ref, d_ref, f1_ref, m_ref, kr_ref, ki_ref, z_scr, p_scr, q_scr):
    n1 = FFT_INNER
    cc = z_scr.shape[1]
    h = z_scr.shape[0] // n1
    n2 = 2 * h
    for src_ref, out_ref, lo in ((a_ref, kr_ref, 0), (d_ref, ki_ref, n1)):
        z_scr[...] = jnp.swapaxes(src_ref[0].reshape(h, n1, cc), 0, 1).reshape(n1 * h, cc)
        _row_block_matmul(f1_ref[...], None, z_scr, p_scr, n1)
        q_scr[...] = jnp.swapaxes(p_scr[...].reshape(n1, 2 * n2, cc), 0, 1).reshape(2 * n2 * n1, cc)

        def body(k2, carry):
            r0 = pl.multiple_of(k2 * n1, n1)
            i0 = pl.multiple_of((n2 + k2) * n1, n1)
            slab = jnp.concatenate([q_scr[pl.ds(r0, n1), :], q_scr[pl.ds(i0, n1), :]], axis=0)
            out_ref[k2] = _mm(m_ref[k2][lo:lo + n1], None, slab)
            return carry
        lax.fori_loop(0, n2, body, 0, unroll=4)


def _fft_filter_spectrum(ad, ft):
    _, l, n = ad.shape
    n1, n2, cc = FFT_INNER, ft["n2"], FFT_COLS
    h = n2 // 2
    res = lambda a: pl.BlockSpec(a.shape, lambda j: (0,) * a.ndim)
    out = pl.BlockSpec((n2, n1, cc), lambda j: (0, 0, j))
    tabs = [ft["f1r_hi"], ft["m_hi"]]
    return pl.pallas_call(
        _fftspec_kernel, name="fft_filter_spectrum",
        grid=(n // cc,),
        in_specs=[pl.BlockSpec((1, l, cc), lambda j: (0, 0, j)), pl.BlockSpec((1, l, cc), lambda j: (1, 0, j))]
                 + [res(t) for t in tabs],
        out_specs=[out, out],
        out_shape=[jax.ShapeDtypeStruct((n2, n1, n), F32)] * 2,
        scratch_shapes=[pltpu.VMEM((n1 * h, cc), F32), pltpu.VMEM((2 * n2 * n1, cc), F32),
                        pltpu.VMEM((2 * n2 * n1, cc), F32)],
        compiler_params=_cparams(("parallel",), 56),
    )(ad, ad, *tabs)


def _softmax_weights(s, sink=None):
    m = jnp.max(s, axis=-1, keepdims=True)
    if sink is not None:
        m = jnp.maximum(m, sink)
    p = jnp.exp(s - m)
    den = jnp.sum(p, axis=-1, keepdims=True)
    if sink is not None:
        den = den + jnp.exp(sink - m)
    return p.astype(BF16), den


def _softmax_pv(s, v, sink=None):
    p, den = _softmax_weights(s, sink)
    return _dot(p, v) / den


def _qk(q, k):
    return lax.dot_general(q, k, (((1,), (1,)), ((), ())), preferred_element_type=F32)


def _half_mask(q_tile, half):
    lane = lax.broadcasted_iota(jnp.int32, q_tile.shape, 1)
    keep = (lane < HEAD_DIM) if half == 0 else (lane >= HEAD_DIM)
    return jnp.where(keep, q_tile, jnp.zeros_like(q_tile))


def _merge_halves(o_even, o_odd):
    lane = lax.broadcasted_iota(jnp.int32, o_even.shape, 1)
    return jnp.where(lane < HEAD_DIM, o_even, o_odd)


def _swa_kernel(sink_ref, q_ref, kp_ref, kc_ref, kn_ref, vp_ref, vc_ref, vn_ref, kx_ref, vx_ref, o_ref):
    n = pl.program_id(1)
    nb = pl.num_programs(1)
    blk = SWA_BLOCK
    g = SWA_HEADS // SWA_KV_HEADS
    n_ctx = kx_ref.shape[1]
    rows = g * blk
    ri = lax.broadcasted_iota(jnp.int32, (rows, 3 * blk + n_ctx), 0) & (blk - 1)
    ci = lax.broadcasted_iota(jnp.int32, (rows, 3 * blk + n_ctx), 1)
    ok_prev = (ci < blk) & (ci >= ri) & (n > 0)
    ok_cur = (ci >= blk) & (ci < 2 * blk)
    ok_next = (ci >= 2 * blk) & (ci < 3 * blk) & (ci - 2 * blk <= ri) & (n < nb - 1)
    ok = ok_prev | ok_cur | ok_next | (ci >= 3 * blk)
    bias = jnp.where(ok, 0.0, NEG_BIG)
    rsel = lax.broadcasted_iota(jnp.int32, (rows, 1), 0) >> (blk.bit_length() - 1)
    outs = [None] * SWA_HEADS
    kvs = range(SWA_KV_HEADS)
    lanes = [slice(kv * LANES, (kv + 1) * LANES) for kv in kvs]
    scores, sinks = [], []
    for kv in kvs:
        cs = lanes[kv]
        k = jnp.concatenate([kp_ref[0, :, cs], kc_ref[0, :, cs], kn_ref[0, :, cs], kx_ref[0, :, cs]], axis=0)
        q_parts = []
        sink = jnp.zeros((rows, 1), F32)
        for gi in range(g):
            h = kv * g + gi
            q_parts.append(_half_mask(q_ref[0, :, (h // 2) * LANES:(h // 2 + 1) * LANES], h % 2))
            sink = jnp.where(rsel == gi, sink_ref[h], sink)
        scores.append(_qk(jnp.concatenate(q_parts, axis=0), k))
        sinks.append(sink)
    probs = [_softmax_weights(scores[kv] + bias, sinks[kv]) for kv in kvs]
    for kv in kvs:
        cs = lanes[kv]
        v = jnp.concatenate([vp_ref[0, :, cs], vc_ref[0, :, cs], vn_ref[0, :, cs], vx_ref[0, :, cs]], axis=0)
        pw, den = probs[kv]
        o = _dot(pw, v) / den
        for gi in range(g):
            outs[kv * g + gi] = o[gi * blk:(gi + 1) * blk]
    for p in range(SWA_HEADS // 2):
        o_ref[0, :, p * LANES:(p + 1) * LANES] = _merge_halves(outs[2 * p], outs[2 * p + 1]).astype(o_ref.dtype)


def _window_gqa(q, k, v, kx, vx, sink):
    b, l, _ = q.shape
    nb = l // SWA_BLOCK
    n_ctx = kx.shape[1]
    kvw = k.shape[2]

    def blk(off):
        return pl.BlockSpec((1, SWA_BLOCK, kvw), lambda b, n: (b, jnp.clip(n + off, 0, nb - 1), 0))

    ctx = pl.BlockSpec((1, n_ctx, kvw), lambda b, n: (b, 0, 0))
    return pl.pallas_call(
        _swa_kernel, name="window_gqa",
        grid=(b, nb),
        in_specs=[
            pl.BlockSpec(memory_space=pltpu.SMEM),
            pl.BlockSpec((1, SWA_BLOCK, q.shape[2]), lambda b, n: (b, n, 0)),
            blk(-1), blk(0), blk(1), blk(-1), blk(0), blk(1), ctx, ctx,
        ],
        out_specs=pl.BlockSpec((1, SWA_BLOCK, q.shape[2]), lambda b, n: (b, n, 0)),
        out_shape=jax.ShapeDtypeStruct(q.shape, BF16),
        compiler_params=_cparams(("parallel", "parallel"), 32),
    )(sink, q, k, k, k, v, v, v, kx, vx)


def _na_bias_kernel(rpb_ref, o_ref):
    off = pl.program_id(0)
    h = pl.program_id(1)
    nd = 2 * NA_COLS - 1
    q = lax.broadcasted_iota(jnp.int32, (GRID_W, LANES), 0)
    lane = lax.broadcasted_iota(jnp.int32, (GRID_W, LANES), 1)
    kc = lane & (GRID_W - 1)
    upper = lane >= GRID_W
    cstart = jnp.clip(q - NA_COLS // 2, 0, GRID_W - NA_COLS)
    valid = (kc >= cstart) & (kc < cstart + NA_COLS)
    dc = jnp.clip(kc - q, -(NA_COLS - 1), NA_COLS - 1) + (NA_COLS - 1)
    for j in range(NA_MAX_ROWS // 2):
        base0 = (h * (2 * NA_MAX_ROWS - 1) + (2 * j - off + NA_MAX_ROWS - 1)) * nd
        base1 = base0 + nd
        t = jnp.zeros((GRID_W, LANES), F32)
        for d in range(nd):
            val = jnp.where(upper, rpb_ref[base1 + d], rpb_ref[base0 + d])
            t = jnp.where(dc == d, val, t)
        o_ref[0, 0, :, j * LANES:(j + 1) * LANES] = jnp.where(valid, t, NEG_BIG)


def _na_bias_table(rpb):
    return pl.pallas_call(
        _na_bias_kernel, name="na_bias_table",
        grid=(NA_MAX_ROWS, NA_HEADS),
        in_specs=[pl.BlockSpec(memory_space=pltpu.SMEM)],
        out_specs=pl.BlockSpec((1, 1, GRID_W, NA_MAX_ROWS * GRID_W), lambda o, h: (o, h, 0, 0)),
        out_shape=jax.ShapeDtypeStruct((NA_MAX_ROWS, NA_HEADS, GRID_W, NA_MAX_ROWS * GRID_W), F32),
        compiler_params=_cparams(("arbitrary", "arbitrary"), 16),
    )(rpb.reshape(-1))


NA_ROWS_PER_STEP = 4


def _na_kernel(q_ref, k_ref, v_ref, kx_ref, vx_ref, bias_ref, o_ref, *, rows):
    wr = NA_MAX_ROWS
    nloc = wr * GRID_W
    for i in range(NA_ROWS_PER_STEP):
        r = pl.program_id(1) * NA_ROWS_PER_STEP + i
        first = jnp.clip(r - wr // 2, 0, rows - wr)
        start = pl.multiple_of(first * GRID_W, GRID_W)
        off = r - first
        qr = slice(i * GRID_W, (i + 1) * GRID_W)
        pairs = range(NA_HEADS // 2)
        lanes = [slice(p * LANES, (p + 1) * LANES) for p in pairs]
        scores = []
        for p in pairs:
            q_tile = q_ref[0, qr, lanes[p]]
            q2 = jnp.concatenate([_half_mask(q_tile, 0), _half_mask(q_tile, 1)], axis=0)
            k = jnp.concatenate([k_ref[0, pl.ds(start, nloc), lanes[p]], kx_ref[0, :, lanes[p]]], axis=0)
            scores.append(_qk(q2, k))
        probs = []
        for p in pairs:
            s = scores[p]
            bias = jnp.concatenate([bias_ref[off, 2 * p], bias_ref[off, 2 * p + 1]], axis=0)
            probs.append(_softmax_weights(jnp.concatenate([s[:, :nloc] + bias, s[:, nloc:]], axis=1)))
        for p in pairs:
            pw, den = probs[p]
            v = jnp.concatenate([v_ref[0, pl.ds(start, nloc), lanes[p]], vx_ref[0, :, lanes[p]]], axis=0)
            o = _dot(pw, v) / den
            o_ref[0, qr, lanes[p]] = _merge_halves(o[:GRID_W], o[GRID_W:]).astype(o_ref.dtype)


def _neighbourhood_attention(q, k, v, kx, vx, bias_tab):
    b, l, w = q.shape
    rows = l // GRID_W
    n_ctx = kx.shape[1]
    tq = NA_ROWS_PER_STEP * GRID_W
    full = pl.BlockSpec((1, l, w), lambda b, r: (b, 0, 0))
    ctx = pl.BlockSpec((1, n_ctx, w), lambda b, r: (b, 0, 0))
    return pl.pallas_call(
        functools.partial(_na_kernel, rows=rows), name="neighbourhood_attention",
        grid=(b, rows // NA_ROWS_PER_STEP),
        in_specs=[pl.BlockSpec((1, tq, w), lambda b, r: (b, r, 0)), full, full, ctx, ctx,
                  _resident(bias_tab.shape)],
        out_specs=pl.BlockSpec((1, tq, w), lambda b, r: (b, r, 0)),
        out_shape=jax.ShapeDtypeStruct(q.shape, BF16),
        compiler_params=_cparams(("parallel", "arbitrary"), 48),
    )(q, k, v, kx, vx, bias_tab)


def _ctx_attn_kernel(sink_ref, qs_ref, ks_ref, vs_ref, qn_ref, kn_ref, vn_ref, os_ref, on_ref):
    n = qs_ref.shape[1]
    g = SWA_HEADS // SWA_KV_HEADS
    rsel = lax.broadcasted_iota(jnp.int32, (g * n, 1), 0) >> (n.bit_length() - 1)
    outs = [None] * SWA_HEADS
    for kv in range(SWA_KV_HEADS):
        cs = slice(kv * LANES, (kv + 1) * LANES)
        q_parts = []
        sink = jnp.zeros((g * n, 1), F32)
        for gi in range(g):
            h = kv * g + gi
            q_parts.append(_half_mask(qs_ref[0, :, (h // 2) * LANES:(h // 2 + 1) * LANES], h % 2))
            sink = jnp.where(rsel == gi, sink_ref[h], sink)
        o = _softmax_pv(_qk(jnp.concatenate(q_parts, axis=0), ks_ref[0, :, cs]), vs_ref[0, :, cs], sink)
        for gi in range(g):
            outs[kv * g + gi] = o[gi * n:(gi + 1) * n]
    for p in range(SWA_HEADS // 2):
        os_ref[0, :, p * LANES:(p + 1) * LANES] = _merge_halves(outs[2 * p], outs[2 * p + 1]).astype(os_ref.dtype)
    for p in range(NA_HEADS // 2):
        cs = slice(p * LANES, (p + 1) * LANES)
        q_tile = qn_ref[0, :, cs]
        q2 = jnp.concatenate([_half_mask(q_tile, 0), _half_mask(q_tile, 1)], axis=0)
        o = _softmax_pv(_qk(q2, kn_ref[0, :, cs]), vn_ref[0, :, cs])
        on_ref[0, :, cs] = _merge_halves(o[:n], o[n:]).astype(on_ref.dtype)


def _context_attention(qs, ks, vs, qn, kn, vn, sink):
    b, n, w = qs.shape
    spec = lambda a: pl.BlockSpec((1,) + a.shape[1:], lambda b: (b, 0, 0))
    return pl.pallas_call(
        _ctx_attn_kernel, name="context_attention",
        grid=(b,),
        in_specs=[pl.BlockSpec(memory_space=pltpu.SMEM)] + [spec(a) for a in (qs, ks, vs, qn, kn, vn)],
        out_specs=[spec(qs), spec(qn)],
        out_shape=[jax.ShapeDtypeStruct(qs.shape, BF16), jax.ShapeDtypeStruct(qn.shape, BF16)],
        compiler_params=_cparams(("parallel",), 32),
    )(sink, qs, ks, vs, qn, kn, vn)


def _merge_ffn_kernel(x_ref, mg_ref, yh_ref, ys_ref, yn_ref, zg_ref, wb_ref, wo_ref,
                      sh_ref, sc_ref, gt_ref, g_ref, wg_ref, wu_ref, wd_ref, o_ref, *, fc):
    d = D_MODEL
    ys = (yh_ref[0].astype(BF16), ys_ref[0], yn_ref[0])
    m = None
    for n in range(3):
        proj = _dot(ys[n], wb_ref[n])
        term = zg_ref[0, :, n * d:(n + 1) * d].astype(F32) * proj
        m = term if m is None else m + term
    x = x_ref[0] + mg_ref[0] * _dot(m.astype(BF16), wo_ref[...])
    o_ref[0] = _swiglu_residual(x, sh_ref, sc_ref, gt_ref, g_ref, wg_ref, wu_ref, wd_ref, fc)


def _merge_and_ffn(x, mix_gate, y_hy, y_swa, y_na, gates, w_branch, w_out, layer,
                   shift, scale, gate, gain, wg, wu, wd):
    b, l, d = x.shape
    tm = min(512, l)
    tok = lambda n: pl.BlockSpec((1, tm, n), lambda b, t: (b, t, 0))
    idx = (layer, 1)
    return pl.pallas_call(
        functools.partial(_merge_ffn_kernel, fc=FFN_CHUNK), name="merge_ffn",
        grid=(b, l // tm),
        in_specs=[tok(d), _mod_spec(mix_gate), tok(HY_WIDTH), tok(HY_WIDTH), tok(HY_WIDTH), tok(3 * d),
                  _layer_slab(w_branch, (layer,)), _layer_slab(w_out, (layer,)),
                  _mod_spec(shift), _mod_spec(scale), _mod_spec(gate), _resident((1, d)),
                  _layer_slab(wg, idx), _layer_slab(wu, idx), _layer_slab(wd, idx)],
        out_specs=tok(d),
        out_shape=jax.ShapeDtypeStruct(x.shape, F32),
        compiler_params=_cparams(("parallel", "parallel"), 58),
    )(x, mix_gate, y_hy, y_swa, y_na, gates, w_branch, w_out, shift, scale, gate, gain.reshape(1, d), wg, wu, wd)


def _dft_tables(l):
    n = 2 * l
    fb = 256 if l >= 1024 else LANES * ((l + 1 + LANES - 1) // LANES)
    fp = fb * ((l + 1 + fb - 1) // fb)
    k = jnp.arange(fp, dtype=jnp.int32)[:, None]
    t = jnp.arange(l, dtype=jnp.int32)[None, :]
    ang = ((k * t) % n).astype(F32) * (2.0 * math.pi / n)
    live = k <= l
    c = jnp.where(live, jnp.cos(ang), 0.0)
    s = jnp.where(live, jnp.sin(ang), 0.0)
    c_hi, c_lo = _split(c)
    s_hi, s_lo = _split(s)
    wk = jnp.where((k == 0) | (k == l), 1.0, 2.0) * jnp.where(live, 1.0 / n, 0.0)
    return dict(fb=fb, c_hi=c_hi, c_lo=c_lo, s_hi=s_hi, s_lo=s_lo,
                ct_hi=c_hi.T, ct_lo=c_lo.T, st_hi=s_hi.T, st_lo=s_lo.T, wk=wk.astype(F32))


def _filter_features(l):
    t = jnp.linspace(0.0, 1.0, l, dtype=F32)[:, None]
    w = (2.0 * math.pi / l) * jnp.arange(l, dtype=F32)[:, None]
    f = jnp.linspace(1e-4, HY_BANDS - 1, HY_BANDS, dtype=F32)[None, :]
    z = jnp.concatenate([t, jnp.cos(f * w), -jnp.sin(f * w)], axis=-1)
    return jnp.pad(z, ((0, 0), (0, LANES - HY_EMB)))


def _decay_rates():
    max_decay = math.log(HY_TARGET) / HY_FAST_DECAY
    min_decay = math.log(HY_TARGET) / HY_SLOW_DECAY
    return jnp.abs(jnp.linspace(min_decay, max_decay, HY_WIDTH, dtype=F32))[None, :]


def _rope_tables(l):
    pos = jnp.arange(l)
    row = (pos // GRID_W).astype(F32)
    col = (pos % GRID_W).astype(F32)
    half = HEAD_DIM // 2
    inv = 1.0 / (ROPE_BASE ** (jnp.arange(0, half, 2, dtype=F32) / half))
    ar = row[:, None] * inv[None, :]
    ac = col[:, None] * inv[None, :]
    cos = jnp.concatenate([jnp.cos(ar), jnp.cos(ar), jnp.cos(ac), jnp.cos(ac)], axis=-1)
    sin = jnp.concatenate([-jnp.sin(ar), jnp.sin(ar), -jnp.sin(ac), jnp.sin(ac)], axis=-1)
    return jnp.tile(cos, (1, 2)), jnp.tile(sin, (1, 2))


def _pad_to(a, shape):
    return jnp.pad(a, [(0, s - d) for d, s in zip(a.shape, shape)])


def _use_fft(bsz, length):
    return bsz % 2 == 0 and length % (8 * FFT_INNER) == 0


def _conv_tables(bsz, length):
    return _fft_tables(length) if _use_fft(bsz, length) else _dft_tables(length)


def _hyena_branch(z_hy, short_w, short_b, feat, fparams, deltas, bias, tabs):
    ad = _hyena_filter_sums(feat, fparams, deltas)
    if "n2" in tabs:
        kr, ki = _fft_filter_spectrum(ad, tabs)
        y1 = _fft_gated_long_conv(None, 0, z_hy, 1, short_w, short_b, kr, ki, 0, bias[0], tabs, F32)
        return _fft_gated_long_conv(y1, 0, z_hy, 2, short_w, short_b, kr, ki, 1, bias[1], tabs, BF16)
    v, x1, x2 = _short_conv3(z_hy, short_w, short_b)
    kr, ki = _filter_spectrum(ad, tabs)
    conv = functools.partial(_gated_long_conv, tabs=tabs)
    y1 = conv(v, x1, kr, ki, 0, bias[0])
    return conv(y1, x2, kr, ki, 1, bias[1])


def kernel(x, c, ctx, c_ctx, w_ada, b_ada, norm_g, ffn_w_gate, ffn_w_up, ffn_w_down,
           w_in, hy_short_w, hy_short_b, hy_pe_w0, hy_pe_b0, hy_pe_w1, hy_pe_b1,
           hy_pe_w2, hy_pe_b2, hy_pe_wout, hy_sin_freq, hy_bias,
           swa_q_gain, swa_k_gain, swa_sink, na_q_gain, na_k_gain, na_rpb,
           w_branch, w_out):
    bsz, seq, d = x.shape
    n_ctx = ctx.shape[1]
    depth = w_ada.shape[0]

    c16 = _pad_to(jnp.concatenate([c, c_ctx[None, :]], axis=0), (MOD_ROWS, d))
    mods = _adaln_mods(c16, w_ada, b_ada).reshape(depth, MOD_ROWS, N_MOD, 1, d)

    tabs_x = _conv_tables(bsz, seq)
    tabs_c = _conv_tables(bsz, n_ctx)
    feat_x = _filter_features(seq)
    feat_c = _filter_features(n_ctx)
    deltas = _decay_rates()
    cos_x, sin_x = _rope_tables(seq)
    cos_c = jnp.ones((n_ctx, LANES), F32)
    sin_c = jnp.zeros((n_ctx, LANES), F32)
    eye = jnp.arange(MXU_DIM) // HEAD_DIM
    bd = (eye[:, None] == eye[None, :]).astype(BF16)

    wg = ffn_w_gate.astype(BF16)
    wu = ffn_w_up.astype(BF16)
    wd = ffn_w_down.astype(BF16)
    wb = w_branch.astype(BF16)
    wo = w_out.astype(BF16)
    win = w_in.astype(BF16)
    kv = win[:, :, _OFF_SWA + _SWA_Q:_OFF_NA].reshape(depth, d, 2, SWA_KV_HEADS, 1, HEAD_DIM)
    win_kv = jnp.broadcast_to(kv, (depth, d, 2, SWA_KV_HEADS, 2, HEAD_DIM)).reshape(depth, d, 4 * _SWA_KV)

    xc = ctx
    for i in range(depth):
        last = i == depth - 1
        mx = lambda j: mods[i, :bsz, j]
        mc = lambda j: mods[i, bsz:bsz + 1, j]
        tile2 = lambda g: jnp.tile(g, 2 * MXU_DIM // LANES)
        head_gains = jnp.stack([tile2(swa_q_gain[i]), tile2(swa_k_gain[i]), tile2(na_q_gain[i]), tile2(na_k_gain[i])])
        fparams = (_pad_to(hy_pe_w0[i], (LANES, LANES)), _pad_to(hy_pe_b0[i][None], (1, LANES)),
                   _pad_to(hy_pe_w1[i], (LANES, LANES)), _pad_to(hy_pe_b1[i][None], (1, LANES)),
                   _pad_to(hy_pe_w2[i], (LANES, LANES)), _pad_to(hy_pe_b2[i][None], (1, LANES)),
                   _pad_to(hy_pe_wout[i], (LANES, 4 * HY_WIDTH)), _pad_to(hy_sin_freq[i][None], (1, LANES)))

        x = _ffn_half_step(x, mx(0), mx(1), mx(2), norm_g[i, 0], wg, wu, wd, (i, 0))
        xc = _ffn_half_step(xc, mc(0), mc(1), mc(2), norm_g[i, 0], wg, wu, wd, (i, 0))

        z_hy, q_s, k_s, v_s, q_n, k_n, v_n, gates = _in_projection(
            x, mx(3), mx(4), norm_g[i, 1], cos_x, sin_x, head_gains, bd, win, win_kv, i)
        zc_hy, qc_s, kc_s, vc_s, qc_n, kc_n, vc_n, gates_c = _in_projection(
            xc, mc(3), mc(4), norm_g[i, 1], cos_c, sin_c, head_gains, bd, win, win_kv, i)

        y_hy = _hyena_branch(z_hy, hy_short_w[i], hy_short_b[i], feat_x, fparams, deltas, hy_bias[i], tabs_x)
        y_swa = _window_gqa(q_s, k_s, v_s, kc_s, vc_s, swa_sink[i])
        y_na = _neighbourhood_attention(q_n, k_n, v_n, kc_n, vc_n, _na_bias_table(na_rpb[i]))
        x = _merge_and_ffn(x, mx(5), y_hy, y_swa, y_na, gates, wb, wo, i,
                           mx(6), mx(7), mx(8), norm_g[i, 2], wg, wu, wd)

        if not last:
            yc_hy = _hyena_branch(zc_hy, hy_short_w[i], hy_short_b[i], feat_c, fparams, deltas, hy_bias[i], tabs_c)
            yc_swa, yc_na = _context_attention(qc_s, kc_s, vc_s, qc_n, kc_n, vc_n, swa_sink[i])
            xc = _merge_and_ffn(xc, mc(5), yc_hy, yc_swa, yc_na, gates_c, wb, wo, i,
                                mc(6), mc(7), mc(8), norm_g[i, 2], wg, wu, wd)
    return x
```

```python
import functools
import math

import jax
import jax.numpy as jnp
from jax import lax
from jax.experimental import pallas as pl
from jax.experimental.pallas import tpu as pltpu

F32 = jnp.float32
BF16 = jnp.bfloat16

D_MODEL = 1024
DEPTH = 4
GRID_W = 64
HEAD_DIM = 64
N_MOD = 9
RMS_EPS = 1e-6

HY_WIDTH = D_MODEL // 2
HY_EMB = 33
HY_BANDS = (HY_EMB - 1) // 2
HY_FFN = 64
HY_FAST_DECAY = 0.3
HY_SLOW_DECAY = 1.5
HY_TARGET = 1e-2

SWA_HEADS = 8
SWA_KV_HEADS = 2
SWA_WINDOW = 128
SWA_BLOCK = 128
ROPE_BASE = 10000.0

NA_HEADS = 8
NA_MAX_ROWS = 8
NA_COLS = 16
NA_COL_BLOCK = 16

FFN_HIDDEN = 256 * ((8 * D_MODEL // 3 + 255) // 256)
FFN_CHUNK = 1536

LANES = 128
MXU_DIM = 256
MOD_ROWS = 16
NEG_BIG = -1e30


def _cparams(sem, vmem_mb):
    return pltpu.CompilerParams(dimension_semantics=sem, vmem_limit_bytes=vmem_mb * 1024 * 1024)


def _resident(shape):
    nd = len(shape)
    return pl.BlockSpec(shape, lambda *_: (0,) * nd, pipeline_mode=pl.Buffered(1))


def _layer_slab(arr, idx):
    rest = arr.shape[len(idx):]
    return pl.BlockSpec((None,) * len(idx) + rest, lambda *_: tuple(idx) + (0,) * len(rest),
                        pipeline_mode=pl.Buffered(1))


def _split(x):
    hi = x.astype(BF16)
    lo = (x - hi.astype(F32)).astype(BF16)
    return hi, lo


def _dot(a, b):
    return jnp.dot(a, b, preferred_element_type=F32)


def _dot3(a_hi, a_lo, b_hi, b_lo):
    return _dot(a_hi, b_hi) + _dot(a_lo, b_hi) + _dot(a_hi, b_lo)


def _dot3f(a, b):
    a_hi, a_lo = _split(a)
    b_hi, b_lo = _split(b)
    return _dot3(a_hi, a_lo, b_hi, b_lo)


def _rms_mod(x, gain, shift, scale):
    ms = jnp.mean(x * x, axis=-1, keepdims=True)
    y = x * lax.rsqrt(ms + RMS_EPS) * gain
    return y * (1.0 + scale) + shift


def _head_rms(x, bd, gain):
    outs = []
    for j in range(x.shape[1] // MXU_DIM):
        xc = x[:, j * MXU_DIM:(j + 1) * MXU_DIM]
        ssum = _dot((xc * xc).astype(BF16), bd)
        outs.append(xc * lax.rsqrt(ssum * (1.0 / HEAD_DIM) + RMS_EPS) * gain)
    return outs[0] if len(outs) == 1 else jnp.concatenate(outs, axis=1)


def _rope(x, cos, sin):
    lane = lax.broadcasted_iota(jnp.int32, (x.shape[0], LANES), 1)
    first = (lane & 31) < 16
    outs = []
    for j in range(x.shape[1] // LANES):
        xc = x[:, j * LANES:(j + 1) * LANES]
        partner = jnp.where(first, pltpu.roll(xc, LANES - 16, axis=1), pltpu.roll(xc, 16, axis=1))
        outs.append(xc * cos + partner * sin)
    return outs[0] if len(outs) == 1 else jnp.concatenate(outs, axis=1)


def _mods_kernel(c_ref, w_ref, b_ref, o_ref):
    c = c_ref[...]
    a = c * jax.nn.sigmoid(c)
    o_ref[0] = _dot3f(a, w_ref[0]) + b_ref[0]


def _adaln_mods(c16, w_ada, b_ada):
    depth, d, n = w_ada.shape
    nb = 1152
    return pl.pallas_call(
        _mods_kernel, name="adaln_mods",
        grid=(depth, n // nb),
        in_specs=[
            pl.BlockSpec((MOD_ROWS, d), lambda i, j: (0, 0)),
            pl.BlockSpec((1, d, nb), lambda i, j: (i, 0, j)),
            pl.BlockSpec((1, 1, nb), lambda i, j: (i, 0, j)),
        ],
        out_specs=pl.BlockSpec((1, MOD_ROWS, nb), lambda i, j: (i, 0, j)),
        out_shape=jax.ShapeDtypeStruct((depth, MOD_ROWS, n), F32),
        compiler_params=_cparams(("arbitrary", "arbitrary"), 48),
    )(c16, w_ada, b_ada.reshape(depth, 1, n))


def _swiglu_residual(x, sh_ref, sc_ref, gt_ref, g_ref, wg_ref, wu_ref, wd_ref, fc):
    hb = _rms_mod(x, g_ref[...], sh_ref[0], sc_ref[0]).astype(BF16)
    acc = None
    for f0 in range(0, FFN_HIDDEN, fc):
        f1 = min(f0 + fc, FFN_HIDDEN)
        g = _dot(hb, wg_ref[:, f0:f1])
        u = _dot(hb, wu_ref[:, f0:f1])
        a = (g * jax.nn.sigmoid(g) * u).astype(BF16)
        d = _dot(a, wd_ref[f0:f1, :])
        acc = d if acc is None else acc + d
    return x + 0.5 * gt_ref[0] * acc


def _ffn_kernel(x_ref, sh_ref, sc_ref, gt_ref, g_ref, wg_ref, wu_ref, wd_ref, o_ref, *, fc):
    o_ref[0] = _swiglu_residual(x_ref[0], sh_ref, sc_ref, gt_ref, g_ref, wg_ref, wu_ref, wd_ref, fc)


def _mod_spec(sel):
    _, layer, j, row = sel
    if row is None:
        return pl.BlockSpec((None, 1, None, 1, D_MODEL), lambda b, t: (layer, b, j, 0, 0))
    return pl.BlockSpec((None, 1, None, 1, D_MODEL), lambda b, t: (layer, row, j, 0, 0))


def _ffn_half_step(x, shift, scale, gate, gain, wg, wu, wd, idx):
    b, l, d = x.shape
    tm = min(512, l)
    return pl.pallas_call(
        functools.partial(_ffn_kernel, fc=FFN_CHUNK), name="ffn_half_step",
        grid=(b, l // tm),
        in_specs=[
            pl.BlockSpec((1, tm, d), lambda b, t: (b, t, 0)),
            _mod_spec(shift), _mod_spec(scale), _mod_spec(gate),
            _resident((1, d)),
            _layer_slab(wg, idx), _layer_slab(wu, idx), _layer_slab(wd, idx),
        ],
        out_specs=pl.BlockSpec((1, tm, d), lambda b, t: (b, t, 0)),
        out_shape=jax.ShapeDtypeStruct(x.shape, F32),
        compiler_params=_cparams(("parallel", "parallel"), 56),
    )(x, shift[0], scale[0], gate[0], gain.reshape(1, d), wg, wu, wd)


_HY_COLS = 3 * HY_WIDTH
_SWA_Q = SWA_HEADS * HEAD_DIM
_SWA_KV = SWA_KV_HEADS * HEAD_DIM
_NA_W = NA_HEADS * HEAD_DIM
_OFF_SWA = _HY_COLS
_OFF_NA = _OFF_SWA + _SWA_Q + 2 * _SWA_KV
_OFF_GATE = _OFF_NA + 3 * _NA_W


def _inproj_kernel(x_ref, sh_ref, sc_ref, g_ref, cos_ref, sin_ref, hg_ref, bd_ref, w_ref, wkv_ref,
                   zhy_ref, qs_ref, ks_ref, vs_ref, qn_ref, kn_ref, vn_ref, gt_ref):
    hb = _rms_mod(x_ref[0], g_ref[...], sh_ref[0], sc_ref[0]).astype(BF16)
    cos = cos_ref[...]
    sin = sin_ref[...]
    bd = bd_ref[...]
    scale = HEAD_DIM ** -0.5
    proj = lambda lo, n: _dot(hb, w_ref[:, lo:lo + n])
    kvw = 2 * _SWA_KV
    qs = proj(_OFF_SWA, _SWA_Q)
    ks = _dot(hb, wkv_ref[:, :kvw])
    qn = proj(_OFF_NA, _NA_W)
    kn = proj(_OFF_NA + _NA_W, _NA_W)
    vs_ref[0] = _dot(hb, wkv_ref[:, kvw:]).astype(BF16)
    vn_ref[0] = proj(_OFF_NA + 2 * _NA_W, _NA_W).astype(BF16)
    zhy_ref[0] = proj(0, _HY_COLS)
    gt_ref[0] = jax.nn.sigmoid(proj(_OFF_GATE, 3 * D_MODEL)).astype(BF16)
    qs_ref[0] = (_rope(_head_rms(qs, bd, hg_ref[0:1, :]), cos, sin) * scale).astype(BF16)
    ks_ref[0] = _rope(_head_rms(ks, bd, hg_ref[1:2, :]), cos, sin).astype(BF16)
    qn_ref[0] = (_head_rms(qn, bd, hg_ref[2:3, :]) * scale).astype(BF16)
    kn_ref[0] = _head_rms(kn, bd, hg_ref[3:4, :]).astype(BF16)


def _in_projection(x, shift, scale, gain, cos, sin, head_gains, bd, w, w_kv_dup, layer):
    b, l, d = x.shape
    tm = min(512, l)
    kvw = 2 * _SWA_KV
    widths = [_HY_COLS, _SWA_Q, kvw, kvw, _NA_W, _NA_W, _NA_W, 3 * d]
    dtypes = [F32] + [BF16] * 7
    return pl.pallas_call(
        _inproj_kernel, name="in_projection",
        grid=(b, l // tm),
        in_specs=[
            pl.BlockSpec((1, tm, d), lambda b, t: (b, t, 0)),
            _mod_spec(shift), _mod_spec(scale),
            _resident((1, d)),
            pl.BlockSpec((tm, LANES), lambda b, t: (t, 0)),
            pl.BlockSpec((tm, LANES), lambda b, t: (t, 0)),
            _resident(head_gains.shape), _resident(bd.shape), _layer_slab(w, (layer,)), _layer_slab(w_kv_dup, (layer,)),
        ],
        out_specs=[pl.BlockSpec((1, tm, n), lambda b, t: (b, t, 0)) for n in widths],
        out_shape=[jax.ShapeDtypeStruct((b, l, n), dt) for n, dt in zip(widths, dtypes)],
        compiler_params=_cparams(("parallel", "parallel"), 56),
    )(x, shift[0], scale[0], gain.reshape(1, d), cos, sin, head_gains, bd, w, w_kv_dup)


def _shortconv_kernel(z0_ref, z1_ref, z2_ref, w_ref, b_ref, o0_ref, o1_ref, o2_ref):
    l = z0_ref.shape[1]
    row = lax.broadcasted_iota(jnp.int32, z0_ref.shape[1:], 0)
    for g, (z_ref, o_ref) in enumerate(((z0_ref, o0_ref), (z1_ref, o1_ref), (z2_ref, o2_ref))):
        z = z_ref[0]
        prev = jnp.where(row == 0, 0.0, pltpu.roll(z, 1, axis=0))
        nxt = jnp.where(row == l - 1, 0.0, pltpu.roll(z, l - 1, axis=0))
        o_ref[0] = w_ref[g, 0:1, :] * prev + w_ref[g, 1:2, :] * z + w_ref[g, 2:3, :] * nxt + b_ref[g]


def _short_conv3(z, w, bias):
    b, l, n = z.shape
    cb = 256
    ncb = HY_WIDTH // cb
    wg = w.reshape(3, 3, HY_WIDTH).transpose(1, 0, 2)
    zin = lambda g: pl.BlockSpec((1, l, cb), lambda b, j: (b, 0, g * ncb + j))
    out = pl.BlockSpec((1, l, cb), lambda b, j: (b, 0, j))
    return pl.pallas_call(
        _shortconv_kernel, name="short_conv3",
        grid=(b, ncb),
        in_specs=[zin(0), zin(1), zin(2),
                  pl.BlockSpec((3, 3, cb), lambda b, j: (0, 0, j)),
                  pl.BlockSpec((3, 1, cb), lambda b, j: (0, 0, j))],
        out_specs=[out, out, out],
        out_shape=[jax.ShapeDtypeStruct((b, l, HY_WIDTH), F32)] * 3,
        compiler_params=_cparams(("parallel", "parallel"), 48),
    )(z, z, z, wg, bias.reshape(3, 1, HY_WIDTH))


def _filter_kernel(z_ref, w0, b0, w1, b1, w2, b2, wo, fr_ref, dl_ref, o_ref):
    z = z_ref[...]
    fr = fr_ref[...]
    a = jnp.sin(fr * (_dot3f(z, w0[...]) + b0[...]))
    a = jnp.sin(fr * (_dot3f(a, w1[...]) + b1[...]))
    a = jnp.sin(fr * (_dot3f(a, w2[...]) + b2[...]))
    hh = _dot3f(a, wo[...])
    t = z[:, 0:1]
    win = jnp.exp(-t * dl_ref[...])
    row = lax.broadcasted_iota(jnp.int32, win.shape, 0) + pl.program_id(0) * z.shape[0]
    w = HY_WIDTH
    for o in range(2):
        hp = hh[:, (2 * o) * w:(2 * o + 1) * w] * win
        hn = jnp.where(row == 0, 0.0, hh[:, (2 * o + 1) * w:(2 * o + 2) * w] * win)
        o_ref[0, :, o * w:(o + 1) * w] = hp + hn
        o_ref[1, :, o * w:(o + 1) * w] = hp - hn


def _hyena_filter_sums(zfeat, fp, deltas):
    l = zfeat.shape[0]
    tl = min(256, l)
    w0, b0, w1, b1, w2, b2, wo, fr = fp
    n = 2 * HY_WIDTH
    consts = [w0, b0, w1, b1, w2, b2, wo, fr, deltas]
    return pl.pallas_call(
        _filter_kernel, name="hyena_filter",
        grid=(l // tl,),
        in_specs=[pl.BlockSpec((tl, LANES), lambda t: (t, 0))] + [_resident(c.shape) for c in consts],
        out_specs=pl.BlockSpec((2, tl, n), lambda t: (0, t, 0)),
        out_shape=jax.ShapeDtypeStruct((2, l, n), F32),
        compiler_params=_cparams(("parallel",), 32),
    )(zfeat, *consts)


def _spectrum_kernel(a_ref, d_ref, c_ref, s_ref, kr_ref, ki_ref):
    kr_ref[...] = _dot(c_ref[...], a_ref[0].astype(BF16))
    ki_ref[...] = -_dot(s_ref[...], d_ref[0].astype(BF16))


def _filter_spectrum(ad, tabs):
    _, l, n = ad.shape
    fp = tabs["c_hi"].shape[0]
    fb = tabs["fb"]
    cb = 256
    tab = pl.BlockSpec((fb, l), lambda j, k: (k, 0))
    out = pl.BlockSpec((fb, cb), lambda j, k: (k, j))
    return pl.pallas_call(
        _spectrum_kernel, name="dense_spectrum",
        grid=(n // cb, fp // fb),
        in_specs=[pl.BlockSpec((1, l, cb), lambda j, k: (0, 0, j)),
                  pl.BlockSpec((1, l, cb), lambda j, k: (1, 0, j)), tab, tab],
        out_specs=[out, out],
        out_shape=[jax.ShapeDtypeStruct((fp, n), F32)] * 2,
        compiler_params=_cparams(("parallel", "parallel"), 48),
    )(ad, ad, tabs["c_hi"], tabs["s_hi"])


def _longconv_kernel(u_ref, g_ref, kr_ref, ki_ref, bias_ref, wk_ref, c_ref, s_ref, ct_ref, st_ref,
                     o_ref, ub_ref, acc_ref):
    kb = pl.program_id(2)

    @pl.when(kb == 0)
    def _():
        ub_ref[...] = u_ref[0].astype(BF16)
        acc_ref[...] = jnp.zeros_like(acc_ref)

    ub = ub_ref[...]
    xr = _dot(c_ref[...], ub)
    xi = -_dot(s_ref[...], ub)
    kr = kr_ref[...]
    ki = ki_ref[...]
    wk = wk_ref[...]
    yr = ((xr * kr - xi * ki) * wk).astype(BF16)
    yi = ((xr * ki + xi * kr) * wk).astype(BF16)
    acc_ref[...] += _dot(ct_ref[...], yr) - _dot(st_ref[...], yi)

    @pl.when(kb == pl.num_programs(2) - 1)
    def _():
        u = u_ref[0]
        o_ref[0] = g_ref[0] * (acc_ref[...] + u * bias_ref[...])


def _gated_long_conv(u_arr, g_arr, kr, ki, k_col, bias, tabs):
    b, l, _ = u_arr.shape
    cb = 256
    ncb = HY_WIDTH // cb
    fp = tabs["c_hi"].shape[0]
    fb = tabs["fb"]
    tab = pl.BlockSpec((fb, l), lambda b, j, k: (k, 0))
    tabt = pl.BlockSpec((l, fb), lambda b, j, k: (0, k))
    spec = pl.BlockSpec((fb, cb), lambda b, j, k: (k, k_col * ncb + j))
    return pl.pallas_call(
        _longconv_kernel, name="dense_long_conv",
        grid=(b, ncb, fp // fb),
        in_specs=[
            pl.BlockSpec((1, l, cb), lambda b, j, k: (b, 0, j)),
            pl.BlockSpec((1, l, cb), lambda b, j, k: (b, 0, j)),
            spec, spec,
            pl.BlockSpec((1, cb), lambda b, j, k: (0, j)),
            pl.BlockSpec((fb, 1), lambda b, j, k: (k, 0)),
            tab, tab, tabt, tabt,
        ],
        out_specs=pl.BlockSpec((1, l, cb), lambda b, j, k: (b, 0, j)),
        out_shape=jax.ShapeDtypeStruct((b, l, HY_WIDTH), F32),
        scratch_shapes=[pltpu.VMEM((l, cb), BF16), pltpu.VMEM((l, cb), F32)],
        compiler_params=_cparams(("parallel", "parallel", "arbitrary"), 56),
    )(u_arr, g_arr, kr, ki, bias.reshape(1, HY_WIDTH), tabs["wk"],
      tabs["c_hi"], tabs["s_hi"], tabs["ct_hi"], tabs["st_hi"])


FFT_INNER = 64
FFT_COLS = 256
FFT_K2_CHUNK = 16


def _fft_tables(l):
    n = 2 * l
    n1 = FFT_INNER
    n2 = n // n1
    h = n2 // 2
    ang = lambda idx, mod: (idx % mod).astype(F32) * (2.0 * math.pi / mod)
    k2 = jnp.arange(n2, dtype=jnp.int32)
    m2 = jnp.arange(h, dtype=jnp.int32)
    a1 = ang(k2[:, None] * m2[None, :], n2)
    c1, s1 = jnp.cos(a1), jnp.sin(a1)
    f1 = jnp.block([[c1, s1], [-s1, c1]])
    k1 = jnp.arange(n1, dtype=jnp.int32)
    a2 = ang(k1[None, None, :] * (k1[None, :, None] * n2 + k2[:, None, None]), n)
    c2, s2 = jnp.cos(a2), jnp.sin(a2)
    m = jnp.concatenate([jnp.concatenate([c2, s2], axis=2), jnp.concatenate([-s2, c2], axis=2)], axis=1)
    g = jnp.block([[c1.T, -s1.T], [s1.T, c1.T]]) * (1.0 / n)
    out = dict(n2=n2)
    for name, t in (("f1", f1), ("f1r", f1[:, :h]), ("m", m), ("mt", jnp.swapaxes(m, 1, 2)), ("g", g)):
        out[name + "_hi"] = t.astype(BF16)
    return out


def _mm(w_hi, w_lo, x):
    if w_lo is None:
        return _dot(w_hi, x.astype(BF16))
    hi, lo = _split(x)
    return _dot3(w_hi, w_lo, hi, lo)


def _row_block_matmul(w_hi, w_lo, src_ref, dst_ref, count):
    m, k = w_hi.shape

    def body(i, carry):
        x = src_ref[pl.ds(pl.multiple_of(i * k, k), k), :]
        dst_ref[pl.ds(pl.multiple_of(i * m, m), m), :] = _mm(w_hi, w_lo, x).astype(dst_ref.dtype)
        return carry
    lax.fori_loop(0, count, body, 0, unroll=4)


def _short_conv_rows(z, w_ref, b_ref):
    l = z.shape[0]
    row = lax.broadcasted_iota(jnp.int32, z.shape, 0)
    prev = jnp.where(row == 0, 0.0, pltpu.roll(z, 1, axis=0))
    nxt = jnp.where(row == l - 1, 0.0, pltpu.roll(z, l - 1, axis=0))
    return w_ref[0:1, :] * prev + w_ref[1:2, :] * z + w_ref[2:3, :] * nxt + b_ref[...]


def _fftconv_kernel(u_ref, g_ref, uw_ref, ub_ref, gw_ref, gb_ref, kr_ref, ki_ref, bias_ref,
                    f1_ref, m_ref, mt_ref, gm_ref, o_ref, z_scr, y_scr, p_scr, q_scr, *, u_conv):
    kc = pl.program_id(2)
    n1 = FFT_INNER
    cc = z_scr.shape[1]
    n2 = z_scr.shape[0] // n1
    h = n2 // 2

    def load_u(s):
        return _short_conv_rows(u_ref[s, 0], uw_ref, ub_ref) if u_conv else u_ref[s, 0]

    @pl.when(kc == 0)
    def _stage1():
        z = [jnp.swapaxes(load_u(s).astype(BF16).reshape(h, n1, cc), 0, 1) for s in range(2)]
        z_scr[...] = jnp.concatenate(z, axis=1).reshape(n1 * n2, cc)
        _row_block_matmul(f1_ref[...], None, z_scr, p_scr, n1)
        q_scr[...] = jnp.swapaxes(p_scr[...].reshape(n1, 2 * n2, cc), 0, 1).reshape(2 * n2 * n1, cc)

    ks = kr_ref.shape[0]
    rows = [(pl.multiple_of((kc * ks + j) * n1, n1), pl.multiple_of((n2 + kc * ks + j) * n1, n1)) for j in range(ks)]
    xs = [_mm(m_ref[j], None, jnp.concatenate([q_scr[pl.ds(r0, n1), :], q_scr[pl.ds(i0, n1), :]], axis=0))
          for j, (r0, i0) in enumerate(rows)]
    ys = []
    for j, x in enumerate(xs):
        xr, xi = x[:n1], x[n1:]
        kr, ki = kr_ref[j], ki_ref[j]
        ys.append(jnp.concatenate([xr * kr - xi * ki, xr * ki + xi * kr], axis=0))
    for j, (r0, i0) in enumerate(rows):
        b = _mm(mt_ref[j], None, ys[j]).astype(p_scr.dtype)
        p_scr[pl.ds(r0, n1), :] = b[:n1]
        p_scr[pl.ds(i0, n1), :] = b[n1:]

    @pl.when(kc == pl.num_programs(2) - 1)
    def _stage1_inv():
        q_scr[...] = jnp.swapaxes(p_scr[...].reshape(2 * n2, n1, cc), 0, 1).reshape(2 * n2 * n1, cc)
        _row_block_matmul(gm_ref[...], None, q_scr, y_scr, n1)
        y3 = y_scr[...].reshape(n1, n2, cc)
        for s in range(2):
            y = jnp.swapaxes(y3[:, s * h:(s + 1) * h, :], 0, 1).reshape(h * n1, cc)
            gate = _short_conv_rows(g_ref[s, 0], gw_ref, gb_ref)
            o_ref[s, 0] = (gate * (y + load_u(s) * bias_ref[...])).astype(o_ref.dtype)


def _fft_gated_long_conv(u, u_col, z_hy, g_col, short_w, short_b, kr, ki, k_col, bias, ft, out_dtype):
    b, l, _ = z_hy.shape
    c = HY_WIDTH
    n1, n2, cc, ks = FFT_INNER, ft["n2"], FFT_COLS, FFT_K2_CHUNK
    p, ncc = b // 2, c // cc
    u_conv = u is None
    if u_conv:
        u = z_hy
    tok = lambda col: pl.BlockSpec((2, 1, l, cc), lambda q, j, k: (0, q, 0, col * ncc + j))
    taps = lambda col: pl.BlockSpec((3, cc), lambda q, j, k: (0, col * ncc + j))
    row = lambda col: pl.BlockSpec((1, cc), lambda q, j, k: (0, col * ncc + j))
    filt = pl.BlockSpec((ks, n1, cc), lambda q, j, k: (k, 0, k_col * ncc + j))
    tab = pl.BlockSpec((ks, 2 * n1, 2 * n1), lambda q, j, k: (k, 0, 0))
    res = lambda a: pl.BlockSpec(a.shape, lambda q, j, k: (0,) * a.ndim)
    sb = short_b.reshape(1, -1)
    y = pl.pallas_call(
        functools.partial(_fftconv_kernel, u_conv=u_conv), name="fft_long_conv",
        grid=(p, ncc, n2 // ks),
        in_specs=[tok(u_col), tok(g_col), taps(u_col), row(u_col), taps(g_col), row(g_col), filt, filt, row(0),
                  res(ft["f1_hi"]), tab, tab, res(ft["g_hi"])],
        out_specs=tok(0),
        out_shape=jax.ShapeDtypeStruct((2, p, l, c), out_dtype),
        scratch_shapes=[pltpu.VMEM((n1 * n2, cc), BF16), pltpu.VMEM((n1 * n2, cc), F32),
                        pltpu.VMEM((2 * n2 * n1, cc), BF16), pltpu.VMEM((2 * n2 * n1, cc), BF16)],
        compiler_params=_cparams(("parallel", "parallel", "arbitrary"), 56),
    )(u.reshape(2, p, l, -1), z_hy.reshape(2, p, l, -1), short_w, sb, short_w, sb, kr, ki, bias.reshape(1, c),
      ft["f1_hi"], ft["m_hi"], ft["mt_hi"], ft["g_hi"])
    return y.reshape(b, l, c)


def _fftspec_kernel(a_ref, d_ref, f1_ref, m_ref, kr_ref, ki_ref, z_scr, p_scr, q_scr):
    n1 = FFT_INNER
    cc = z_scr.shape[1]
    h = z_scr.shape[0] // n1
    n2 = 2 * h
    for src_ref, out_ref, lo in ((a_ref, kr_ref, 0), (d_ref, ki_ref, n1)):
        z_scr[...] = jnp.swapaxes(src_ref[0].reshape(h, n1, cc), 0, 1).reshape(n1 * h, cc)
        _row_block_matmul(f1_ref[...], None, z_scr, p_scr, n1)
        q_scr[...] = jnp.swapaxes(p_scr[...].reshape(n1, 2 * n2, cc), 0, 1).reshape(2 * n2 * n1, cc)

        def body(k2, carry):
            r0 = pl.multiple_of(k2 * n1, n1)
            i0 = pl.multiple_of((n2 + k2) * n1, n1)
            slab = jnp.concatenate([q_scr[pl.ds(r0, n1), :], q_scr[pl.ds(i0, n1), :]], axis=0)
            out_ref[k2] = _mm(m_ref[k2][lo:lo + n1], None, slab)
            return carry
        lax.fori_loop(0, n2, body, 0, unroll=4)


def _fft_filter_spectrum(ad, ft):
    _, l, n = ad.shape
    n1, n2, cc = FFT_INNER, ft["n2"], FFT_COLS
    h = n2 // 2
    res = lambda a: pl.BlockSpec(a.shape, lambda j: (0,) * a.ndim)
    out = pl.BlockSpec((n2, n1, cc), lambda j: (0, 0, j))
    tabs = [ft["f1r_hi"], ft["m_hi"]]
    return pl.pallas_call(
        _fftspec_kernel, name="fft_filter_spectrum",
        grid=(n // cc,),
        in_specs=[pl.BlockSpec((1, l, cc), lambda j: (0, 0, j)), pl.BlockSpec((1, l, cc), lambda j: (1, 0, j))]
                 + [res(t) for t in tabs],
        out_specs=[out, out],
        out_shape=[jax.ShapeDtypeStruct((n2, n1, n), F32)] * 2,
        scratch_shapes=[pltpu.VMEM((n1 * h, cc), F32), pltpu.VMEM((2 * n2 * n1, cc), F32),
                        pltpu.VMEM((2 * n2 * n1, cc), F32)],
        compiler_params=_cparams(("parallel",), 56),
    )(ad, ad, *tabs)


def _softmax_weights(s, sink=None):
    m = jnp.max(s, axis=-1, keepdims=True)
    if sink is not None:
        m = jnp.maximum(m, sink)
    p = jnp.exp(s - m)
    den = jnp.sum(p, axis=-1, keepdims=True)
    if sink is not None:
        den = den + jnp.exp(sink - m)
    return p.astype(BF16), den


def _softmax_pv(s, v, sink=None):
    p, den = _softmax_weights(s, sink)
    return _dot(p, v) / den


def _qk(q, k):
    return lax.dot_general(q, k, (((1,), (1,)), ((), ())), preferred_element_type=F32)


def _half_mask(q_tile, half):
    lane = lax.broadcasted_iota(jnp.int32, q_tile.shape, 1)
    keep = (lane < HEAD_DIM) if half == 0 else (lane >= HEAD_DIM)
    return jnp.where(keep, q_tile, jnp.zeros_like(q_tile))


def _merge_halves(o_even, o_odd):
    lane = lax.broadcasted_iota(jnp.int32, o_even.shape, 1)
    return jnp.where(lane < HEAD_DIM, o_even, o_odd)


def _swa_kernel(sink_ref, q_ref, kp_ref, kc_ref, kn_ref, vp_ref, vc_ref, vn_ref, kx_ref, vx_ref, o_ref):
    n = pl.program_id(1)
    nb = pl.num_programs(1)
    blk = SWA_BLOCK
    g = SWA_HEADS // SWA_KV_HEADS
    n_ctx = kx_ref.shape[1]
    rows = g * blk
    ri = lax.broadcasted_iota(jnp.int32, (rows, 3 * blk + n_ctx), 0) & (blk - 1)
    ci = lax.broadcasted_iota(jnp.int32, (rows, 3 * blk + n_ctx), 1)
    ok_prev = (ci < blk) & (ci >= ri) & (n > 0)
    ok_cur = (ci >= blk) & (ci < 2 * blk)
    ok_next = (ci >= 2 * blk) & (ci < 3 * blk) & (ci - 2 * blk <= ri) & (n < nb - 1)
    ok = ok_prev | ok_cur | ok_next | (ci >= 3 * blk)
    bias = jnp.where(ok, 0.0, NEG_BIG)
    rsel = lax.broadcasted_iota(jnp.int32, (rows, 1), 0) >> (blk.bit_length() - 1)
    outs = [None] * SWA_HEADS
    kvs = range(SWA_KV_HEADS)
    lanes = [slice(kv * LANES, (kv + 1) * LANES) for kv in kvs]
    scores, sinks = [], []
    for kv in kvs:
        cs = lanes[kv]
        k = jnp.concatenate([kp_ref[0, :, cs], kc_ref[0, :, cs], kn_ref[0, :, cs], kx_ref[0, :, cs]], axis=0)
        q_parts = []
        sink = jnp.zeros((rows, 1), F32)
        for gi in range(g):
            h = kv * g + gi
            q_parts.append(_half_mask(q_ref[0, :, (h // 2) * LANES:(h // 2 + 1) * LANES], h % 2))
            sink = jnp.where(rsel == gi, sink_ref[h], sink)
        scores.append(_qk(jnp.concatenate(q_parts, axis=0), k))
        sinks.append(sink)
    probs = [_softmax_weights(scores[kv] + bias, sinks[kv]) for kv in kvs]
    for kv in kvs:
        cs = lanes[kv]
        v = jnp.concatenate([vp_ref[0, :, cs], vc_ref[0, :, cs], vn_ref[0, :, cs], vx_ref[0, :, cs]], axis=0)
        pw, den = probs[kv]
        o = _dot(pw, v) / den
        for gi in range(g):
            outs[kv * g + gi] = o[gi * blk:(gi + 1) * blk]
    for p in range(SWA_HEADS // 2):
        o_ref[0, :, p * LANES:(p + 1) * LANES] = _merge_halves(outs[2 * p], outs[2 * p + 1]).astype(o_ref.dtype)


def _window_gqa(q, k, v, kx, vx, sink):
    b, l, _ = q.shape
    nb = l // SWA_BLOCK
    n_ctx = kx.shape[1]
    kvw = k.shape[2]

    def blk(off):
        return pl.BlockSpec((1, SWA_BLOCK, kvw), lambda b, n: (b, jnp.clip(n + off, 0, nb - 1), 0))

    ctx = pl.BlockSpec((1, n_ctx, kvw), lambda b, n: (b, 0, 0))
    return pl.pallas_call(
        _swa_kernel, name="window_gqa",
        grid=(b, nb),
        in_specs=[
            pl.BlockSpec(memory_space=pltpu.SMEM),
            pl.BlockSpec((1, SWA_BLOCK, q.shape[2]), lambda b, n: (b, n, 0)),
            blk(-1), blk(0), blk(1), blk(-1), blk(0), blk(1), ctx, ctx,
        ],
        out_specs=pl.BlockSpec((1, SWA_BLOCK, q.shape[2]), lambda b, n: (b, n, 0)),
        out_shape=jax.ShapeDtypeStruct(q.shape, BF16),
        compiler_params=_cparams(("parallel", "parallel"), 32),
    )(sink, q, k, k, k, v, v, v, kx, vx)


def _na_bias_kernel(rpb_ref, o_ref):
    off = pl.program_id(0)
    h = pl.program_id(1)
    nd = 2 * NA_COLS - 1
    q = lax.broadcasted_iota(jnp.int32, (GRID_W, LANES), 0)
    lane = lax.broadcasted_iota(jnp.int32, (GRID_W, LANES), 1)
    kc = lane & (GRID_W - 1)
    upper = lane >= GRID_W
    cstart = jnp.clip(q - NA_COLS // 2, 0, GRID_W - NA_COLS)
    valid = (kc >= cstart) & (kc < cstart + NA_COLS)
    dc = jnp.clip(kc - q, -(NA_COLS - 1), NA_COLS - 1) + (NA_COLS - 1)
    for j in range(NA_MAX_ROWS // 2):
        base0 = (h * (2 * NA_MAX_ROWS - 1) + (2 * j - off + NA_MAX_ROWS - 1)) * nd
        base1 = base0 + nd
        t = jnp.zeros((GRID_W, LANES), F32)
        for d in range(nd):
            val = jnp.where(upper, rpb_ref[base1 + d], rpb_ref[base0 + d])
            t = jnp.where(dc == d, val, t)
        o_ref[0, 0, :, j * LANES:(j + 1) * LANES] = jnp.where(valid, t, NEG_BIG)


def _na_bias_table(rpb):
    return pl.pallas_call(
        _na_bias_kernel, name="na_bias_table",
        grid=(NA_MAX_ROWS, NA_HEADS),
        in_specs=[pl.BlockSpec(memory_space=pltpu.SMEM)],
        out_specs=pl.BlockSpec((1, 1, GRID_W, NA_MAX_ROWS * GRID_W), lambda o, h: (o, h, 0, 0)),
        out_shape=jax.ShapeDtypeStruct((NA_MAX_ROWS, NA_HEADS, GRID_W, NA_MAX_ROWS * GRID_W), F32),
        compiler_params=_cparams(("arbitrary", "arbitrary"), 16),
    )(rpb.reshape(-1))


NA_ROWS_PER_STEP = 4


def _na_kernel(q_ref, k_ref, v_ref, kx_ref, vx_ref, bias_ref, o_ref, *, rows):
    wr = NA_MAX_ROWS
    nloc = wr * GRID_W
    for i in range(NA_ROWS_PER_STEP):
        r = pl.program_id(1) * NA_ROWS_PER_STEP + i
        first = jnp.clip(r - wr // 2, 0, rows - wr)
        start = pl.multiple_of(first * GRID_W, GRID_W)
        off = r - first
        qr = slice(i * GRID_W, (i + 1) * GRID_W)
        pairs = range(NA_HEADS // 2)
        lanes = [slice(p * LANES, (p + 1) * LANES) for p in pairs]
        scores = []
        for p in pairs:
            q_tile = q_ref[0, qr, lanes[p]]
            q2 = jnp.concatenate([_half_mask(q_tile, 0), _half_mask(q_tile, 1)], axis=0)
            k = jnp.concatenate([k_ref[0, pl.ds(start, nloc), lanes[p]], kx_ref[0, :, lanes[p]]], axis=0)
            scores.append(_qk(q2, k))
        probs = []
        for p in pairs:
            s = scores[p]
            bias = jnp.concatenate([bias_ref[off, 2 * p], bias_ref[off, 2 * p + 1]], axis=0)
            probs.append(_softmax_weights(jnp.concatenate([s[:, :nloc] + bias, s[:, nloc:]], axis=1)))
        for p in pairs:
            pw, den = probs[p]
            v = jnp.concatenate([v_ref[0, pl.ds(start, nloc), lanes[p]], vx_ref[0, :, lanes[p]]], axis=0)
            o = _dot(pw, v) / den
            o_ref[0, qr, lanes[p]] = _merge_halves(o[:GRID_W], o[GRID_W:]).astype(o_ref.dtype)


def _neighbourhood_attention(q, k, v, kx, vx, bias_tab):
    b, l, w = q.shape
    rows = l // GRID_W
    n_ctx = kx.shape[1]
    tq = NA_ROWS_PER_STEP * GRID_W
    full = pl.BlockSpec((1, l, w), lambda b, r: (b, 0, 0))
    ctx = pl.BlockSpec((1, n_ctx, w), lambda b, r: (b, 0, 0))
    return pl.pallas_call(
        functools.partial(_na_kernel, rows=rows), name="neighbourhood_attention",
        grid=(b, rows // NA_ROWS_PER_STEP),
        in_specs=[pl.BlockSpec((1, tq, w), lambda b, r: (b, r, 0)), full, full, ctx, ctx,
                  _resident(bias_tab.shape)],
        out_specs=pl.BlockSpec((1, tq, w), lambda b, r: (b, r, 0)),
        out_shape=jax.ShapeDtypeStruct(q.shape, BF16),
        compiler_params=_cparams(("parallel", "arbitrary"), 48),
    )(q, k, v, kx, vx, bias_tab)


def _ctx_attn_kernel(sink_ref, qs_ref, ks_ref, vs_ref, qn_ref, kn_ref, vn_ref, os_ref, on_ref):
    n = qs_ref.shape[1]
    g = SWA_HEADS // SWA_KV_HEADS
    rsel = lax.broadcasted_iota(jnp.int32, (g * n, 1), 0) >> (n.bit_length() - 1)
    outs = [None] * SWA_HEADS
    for kv in range(SWA_KV_HEADS):
        cs = slice(kv * LANES, (kv + 1) * LANES)
        q_parts = []
        sink = jnp.zeros((g * n, 1), F32)
        for gi in range(g):
            h = kv * g + gi
            q_parts.append(_half_mask(qs_ref[0, :, (h // 2) * LANES:(h // 2 + 1) * LANES], h % 2))
            sink = jnp.where(rsel == gi, sink_ref[h], sink)
        o = _softmax_pv(_qk(jnp.concatenate(q_parts, axis=0), ks_ref[0, :, cs]), vs_ref[0, :, cs], sink)
        for gi in range(g):
            outs[kv * g + gi] = o[gi * n:(gi + 1) * n]
    for p in range(SWA_HEADS // 2):
        os_ref[0, :, p * LANES:(p + 1) * LANES] = _merge_halves(outs[2 * p], outs[2 * p + 1]).astype(os_ref.dtype)
    for p in range(NA_HEADS // 2):
        cs = slice(p * LANES, (p + 1) * LANES)
        q_tile = qn_ref[0, :, cs]
        q2 = jnp.concatenate([_half_mask(q_tile, 0), _half_mask(q_tile, 1)], axis=0)
        o = _softmax_pv(_qk(q2, kn_ref[0, :, cs]), vn_ref[0, :, cs])
        on_ref[0, :, cs] = _merge_halves(o[:n], o[n:]).astype(on_ref.dtype)


def _context_attention(qs, ks, vs, qn, kn, vn, sink):
    b, n, w = qs.shape
    spec = lambda a: pl.BlockSpec((1,) + a.shape[1:], lambda b: (b, 0, 0))
    return pl.pallas_call(
        _ctx_attn_kernel, name="context_attention",
        grid=(b,),
        in_specs=[pl.BlockSpec(memory_space=pltpu.SMEM)] + [spec(a) for a in (qs, ks, vs, qn, kn, vn)],
        out_specs=[spec(qs), spec(qn)],
        out_shape=[jax.ShapeDtypeStruct(qs.shape, BF16), jax.ShapeDtypeStruct(qn.shape, BF16)],
        compiler_params=_cparams(("parallel",), 32),
    )(sink, qs, ks, vs, qn, kn, vn)


def _merge_ffn_kernel(x_ref, mg_ref, yh_ref, ys_ref, yn_ref, zg_ref, wb_ref, wo_ref,
                      sh_ref, sc_ref, gt_ref, g_ref, wg_ref, wu_ref, wd_ref, o_ref, *, fc):
    d = D_MODEL
    ys = (yh_ref[0].astype(BF16), ys_ref[0], yn_ref[0])
    m = None
    for n in range(3):
        proj = _dot(ys[n], wb_ref[n])
        term = zg_ref[0, :, n * d:(n + 1) * d].astype(F32) * proj
        m = term if m is None else m + term
    x = x_ref[0] + mg_ref[0] * _dot(m.astype(BF16), wo_ref[...])
    o_ref[0] = _swiglu_residual(x, sh_ref, sc_ref, gt_ref, g_ref, wg_ref, wu_ref, wd_ref, fc)


def _merge_and_ffn(x, mix_gate, y_hy, y_swa, y_na, gates, w_branch, w_out, layer,
                   shift, scale, gate, gain, wg, wu, wd):
    b, l, d = x.shape
    tm = min(512, l)
    tok = lambda n: pl.BlockSpec((1, tm, n), lambda b, t: (b, t, 0))
    idx = (layer, 1)
    return pl.pallas_call(
        functools.partial(_merge_ffn_kernel, fc=FFN_CHUNK), name="merge_ffn",
        grid=(b, l // tm),
        in_specs=[tok(d), _mod_spec(mix_gate), tok(HY_WIDTH), tok(HY_WIDTH), tok(HY_WIDTH), tok(3 * d),
                  _layer_slab(w_branch, (layer,)), _layer_slab(w_out, (layer,)),
                  _mod_spec(shift), _mod_spec(scale), _mod_spec(gate), _resident((1, d)),
                  _layer_slab(wg, idx), _layer_slab(wu, idx), _layer_slab(wd, idx)],
        out_specs=tok(d),
        out_shape=jax.ShapeDtypeStruct(x.shape, F32),
        compiler_params=_cparams(("parallel", "parallel"), 58),
    )(x, mix_gate[0], y_hy, y_swa, y_na, gates, w_branch, w_out, shift[0], scale[0], gate[0], gain.reshape(1, d),
      wg, wu, wd)


def _dft_tables(l):
    n = 2 * l
    fb = 256 if l >= 1024 else LANES * ((l + 1 + LANES - 1) // LANES)
    fp = fb * ((l + 1 + fb - 1) // fb)
    k = jnp.arange(fp, dtype=jnp.int32)[:, None]
    t = jnp.arange(l, dtype=jnp.int32)[None, :]
    ang = ((k * t) % n).astype(F32) * (2.0 * math.pi / n)
    live = k <= l
    c = jnp.where(live, jnp.cos(ang), 0.0)
    s = jnp.where(live, jnp.sin(ang), 0.0)
    c_hi = c.astype(BF16)
    s_hi = s.astype(BF16)
    wk = jnp.where((k == 0) | (k == l), 1.0, 2.0) * jnp.where(live, 1.0 / n, 0.0)
    return dict(fb=fb, c_hi=c_hi, s_hi=s_hi, ct_hi=c_hi.T, st_hi=s_hi.T, wk=wk.astype(F32))


def _filter_features(l):
    t = jnp.linspace(0.0, 1.0, l, dtype=F32)[:, None]
    w = (2.0 * math.pi / l) * jnp.arange(l, dtype=F32)[:, None]
    f = jnp.linspace(1e-4, HY_BANDS - 1, HY_BANDS, dtype=F32)[None, :]
    z = jnp.concatenate([t, jnp.cos(f * w), -jnp.sin(f * w)], axis=-1)
    return jnp.pad(z, ((0, 0), (0, LANES - HY_EMB)))


def _decay_rates():
    max_decay = math.log(HY_TARGET) / HY_FAST_DECAY
    min_decay = math.log(HY_TARGET) / HY_SLOW_DECAY
    return jnp.abs(jnp.linspace(min_decay, max_decay, HY_WIDTH, dtype=F32))[None, :]


def _rope_tables(l):
    pos = jnp.arange(l)
    row = (pos // GRID_W).astype(F32)
    col = (pos % GRID_W).astype(F32)
    half = HEAD_DIM // 2
    inv = 1.0 / (ROPE_BASE ** (jnp.arange(0, half, 2, dtype=F32) / half))
    ar = row[:, None] * inv[None, :]
    ac = col[:, None] * inv[None, :]
    cos = jnp.concatenate([jnp.cos(ar), jnp.cos(ar), jnp.cos(ac), jnp.cos(ac)], axis=-1)
    sin = jnp.concatenate([-jnp.sin(ar), jnp.sin(ar), -jnp.sin(ac), jnp.sin(ac)], axis=-1)
    return jnp.tile(cos, (1, 2)), jnp.tile(sin, (1, 2))


def _pad_to(a, shape):
    return jnp.pad(a, [(0, s - d) for d, s in zip(a.shape, shape)])


def _use_fft(bsz, length):
    return bsz % 2 == 0 and length % (8 * FFT_INNER) == 0


def _conv_tables(bsz, length):
    return _fft_tables(length) if _use_fft(bsz, length) else _dft_tables(length)


def _hyena_branch(z_hy, short_w, short_b, feat, fparams, deltas, bias, tabs):
    ad = _hyena_filter_sums(feat, fparams, deltas)
    if "n2" in tabs:
        kr, ki = _fft_filter_spectrum(ad, tabs)
        y1 = _fft_gated_long_conv(None, 0, z_hy, 1, short_w, short_b, kr, ki, 0, bias[0], tabs, F32)
        return _fft_gated_long_conv(y1, 0, z_hy, 2, short_w, short_b, kr, ki, 1, bias[1], tabs, BF16)
    v, x1, x2 = _short_conv3(z_hy, short_w, short_b)
    kr, ki = _filter_spectrum(ad, tabs)
    conv = functools.partial(_gated_long_conv, tabs=tabs)
    y1 = conv(v, x1, kr, ki, 0, bias[0])
    return conv(y1, x2, kr, ki, 1, bias[1])


def kernel(x, c, ctx, c_ctx, w_ada, b_ada, norm_g, ffn_w_gate, ffn_w_up, ffn_w_down,
           w_in, hy_short_w, hy_short_b, hy_pe_w0, hy_pe_b0, hy_pe_w1, hy_pe_b1,
           hy_pe_w2, hy_pe_b2, hy_pe_wout, hy_sin_freq, hy_bias,
           swa_q_gain, swa_k_gain, swa_sink, na_q_gain, na_k_gain, na_rpb,
           w_branch, w_out):
    bsz, seq, d = x.shape
    n_ctx = ctx.shape[1]
    depth = w_ada.shape[0]

    c16 = _pad_to(jnp.concatenate([c, c_ctx[None, :]], axis=0), (MOD_ROWS, d))
    mods = _adaln_mods(c16, w_ada, b_ada).reshape(depth, MOD_ROWS, N_MOD, 1, d)

    tabs_x = _conv_tables(bsz, seq)
    tabs_c = _conv_tables(bsz, n_ctx)
    feat_x = _filter_features(seq)
    feat_c = _filter_features(n_ctx)
    deltas = _decay_rates()
    cos_x, sin_x = _rope_tables(seq)
    cos_c = jnp.ones((n_ctx, LANES), F32)
    sin_c = jnp.zeros((n_ctx, LANES), F32)
    eye = jnp.arange(MXU_DIM) // HEAD_DIM
    bd = (eye[:, None] == eye[None, :]).astype(BF16)

    wg = ffn_w_gate.astype(BF16)
    wu = ffn_w_up.astype(BF16)
    wd = ffn_w_down.astype(BF16)
    wb = w_branch.astype(BF16)
    wo = w_out.astype(BF16)
    win = w_in.astype(BF16)
    kv = win[:, :, _OFF_SWA + _SWA_Q:_OFF_NA].reshape(depth, d, 2, SWA_KV_HEADS, 1, HEAD_DIM)
    win_kv = jnp.broadcast_to(kv, (depth, d, 2, SWA_KV_HEADS, 2, HEAD_DIM)).reshape(depth, d, 4 * _SWA_KV)

    xc = ctx
    for i in range(depth):
        last = i == depth - 1
        mx = lambda j: (mods, i, j, None)
        mc = lambda j: (mods, i, j, bsz)
        tile2 = lambda g: jnp.tile(g, 2 * MXU_DIM // LANES)
        head_gains = jnp.stack([tile2(swa_q_gain[i]), tile2(swa_k_gain[i]), tile2(na_q_gain[i]), tile2(na_k_gain[i])])
        fparams = (_pad_to(hy_pe_w0[i], (LANES, LANES)), _pad_to(hy_pe_b0[i][None], (1, LANES)),
                   _pad_to(hy_pe_w1[i], (LANES, LANES)), _pad_to(hy_pe_b1[i][None], (1, LANES)),
                   _pad_to(hy_pe_w2[i], (LANES, LANES)), _pad_to(hy_pe_b2[i][None], (1, LANES)),
                   _pad_to(hy_pe_wout[i], (LANES, 4 * HY_WIDTH)), _pad_to(hy_sin_freq[i][None], (1, LANES)))

        x = _ffn_half_step(x, mx(0), mx(1), mx(2), norm_g[i, 0], wg, wu, wd, (i, 0))
        xc = _ffn_half_step(xc, mc(0), mc(1), mc(2), norm_g[i, 0], wg, wu, wd, (i, 0))

        z_hy, q_s, k_s, v_s, q_n, k_n, v_n, gates = _in_projection(
            x, mx(3), mx(4), norm_g[i, 1], cos_x, sin_x, head_gains, bd, win, win_kv, i)
        zc_hy, qc_s, kc_s, vc_s, qc_n, kc_n, vc_n, gates_c = _in_projection(
            xc, mc(3), mc(4), norm_g[i, 1], cos_c, sin_c, head_gains, bd, win, win_kv, i)

        y_hy = _hyena_branch(z_hy, hy_short_w[i], hy_short_b[i], feat_x, fparams, deltas, hy_bias[i], tabs_x)
        y_swa = _window_gqa(q_s, k_s, v_s, kc_s, vc_s, swa_sink[i])
        y_na = _neighbourhood_attention(q_n, k_n, v_n, kc_n, vc_n, _na_bias_table(na_rpb[i]))
        x = _merge_and_ffn(x, mx(5), y_hy, y_swa, y_na, gates, wb, wo, i,
                           mx(6), mx(7), mx(8), norm_g[i, 2], wg, wu, wd)

        if not last:
            yc_hy = _hyena_branch(zc_hy, hy_short_w[i], hy_short_b[i], feat_c, fparams, deltas, hy_bias[i], tabs_c)
            yc_swa, yc_na = _context_attention(qc_s, kc_s, vc_s, qc_n, kc_n, vc_n, swa_sink[i])
            xc = _merge_and_ffn(xc, mc(5), yc_hy, yc_swa, yc_na, gates_c, wb, wo, i,
                                mc(6), mc(7), mc(8), norm_g[i, 2], wg, wu, wd)
    return x
```

```python
import functools
import math

import jax
import jax.numpy as jnp
from jax import lax
from jax.experimental import pallas as pl
from jax.experimental.pallas import tpu as pltpu

F32 = jnp.float32
BF16 = jnp.bfloat16

D_MODEL = 1024
DEPTH = 4
GRID_W = 64
HEAD_DIM = 64
N_MOD = 9
RMS_EPS = 1e-6

HY_WIDTH = D_MODEL // 2
HY_EMB = 33
HY_BANDS = (HY_EMB - 1) // 2
HY_FFN = 64
HY_FAST_DECAY = 0.3
HY_SLOW_DECAY = 1.5
HY_TARGET = 1e-2

SWA_HEADS = 8
SWA_KV_HEADS = 2
SWA_WINDOW = 128
SWA_BLOCK = 128
ROPE_BASE = 10000.0

NA_HEADS = 8
NA_MAX_ROWS = 8
NA_COLS = 16
NA_COL_BLOCK = 16

FFN_HIDDEN = 256 * ((8 * D_MODEL // 3 + 255) // 256)
FFN_CHUNK = 1536

LANES = 128
MXU_DIM = 256
MOD_ROWS = 16
NEG_BIG = -1e30


def _cparams(sem, vmem_mb):
    return pltpu.CompilerParams(dimension_semantics=sem, vmem_limit_bytes=vmem_mb * 1024 * 1024)


def _resident(shape):
    nd = len(shape)
    return pl.BlockSpec(shape, lambda *_: (0,) * nd, pipeline_mode=pl.Buffered(1))


def _layer_slab(arr, idx):
    rest = arr.shape[len(idx):]
    return pl.BlockSpec((None,) * len(idx) + rest, lambda *_: tuple(idx) + (0,) * len(rest),
                        pipeline_mode=pl.Buffered(1))


def _split(x):
    hi = x.astype(BF16)
    lo = (x - hi.astype(F32)).astype(BF16)
    return hi, lo


def _dot(a, b):
    return jnp.dot(a, b, preferred_element_type=F32)


def _dot3(a_hi, a_lo, b_hi, b_lo):
    return _dot(a_hi, b_hi) + _dot(a_lo, b_hi) + _dot(a_hi, b_lo)


def _dot3f(a, b):
    a_hi, a_lo = _split(a)
    b_hi, b_lo = _split(b)
    return _dot3(a_hi, a_lo, b_hi, b_lo)


def _rms_mod(x, gain, shift, scale):
    ms = jnp.mean(x * x, axis=-1, keepdims=True)
    y = x * lax.rsqrt(ms + RMS_EPS) * gain
    return y * (1.0 + scale) + shift


def _head_sumsq(x, bd):
    return [_dot(jnp.square(x[:, j * MXU_DIM:(j + 1) * MXU_DIM]).astype(BF16), bd) for j in range(x.shape[1] // MXU_DIM)]


def _head_normalize(x, sumsq, gain):
    outs = [x[:, j * MXU_DIM:(j + 1) * MXU_DIM] * lax.rsqrt(s * (1.0 / HEAD_DIM) + RMS_EPS) * gain
            for j, s in enumerate(sumsq)]
    return outs[0] if len(outs) == 1 else jnp.concatenate(outs, axis=1)


def _rope(x, cos, sin):
    lane = lax.broadcasted_iota(jnp.int32, (x.shape[0], LANES), 1)
    first = (lane & 31) < 16
    outs = []
    for j in range(x.shape[1] // LANES):
        xc = x[:, j * LANES:(j + 1) * LANES]
        partner = jnp.where(first, pltpu.roll(xc, LANES - 16, axis=1), pltpu.roll(xc, 16, axis=1))
        outs.append(xc * cos + partner * sin)
    return outs[0] if len(outs) == 1 else jnp.concatenate(outs, axis=1)


def _mods_kernel(c_ref, w_ref, b_ref, o_ref):
    c = c_ref[...]
    a = c * jax.nn.sigmoid(c)
    o_ref[0] = _dot3f(a, w_ref[0]) + b_ref[0]


def _adaln_mods(c16, w_ada, b_ada):
    depth, d, n = w_ada.shape
    nb = 1152
    return pl.pallas_call(
        _mods_kernel, name="adaln_mods",
        grid=(depth, n // nb),
        in_specs=[
            pl.BlockSpec((MOD_ROWS, d), lambda i, j: (0, 0)),
            pl.BlockSpec((1, d, nb), lambda i, j: (i, 0, j)),
            pl.BlockSpec((1, 1, nb), lambda i, j: (i, 0, j)),
        ],
        out_specs=pl.BlockSpec((1, MOD_ROWS, nb), lambda i, j: (i, 0, j)),
        out_shape=jax.ShapeDtypeStruct((depth, MOD_ROWS, n), F32),
        compiler_params=_cparams(("arbitrary", "arbitrary"), 48),
    )(c16, w_ada, b_ada.reshape(depth, 1, n))


def _swiglu_residual(x, sh_ref, sc_ref, gt_ref, g_ref, wg_ref, wu_ref, wd_ref, fc):
    hb = _rms_mod(x, g_ref[...], sh_ref[0], sc_ref[0]).astype(BF16)
    acc = None
    for f0 in range(0, FFN_HIDDEN, fc):
        f1 = min(f0 + fc, FFN_HIDDEN)
        g = _dot(hb, wg_ref[:, f0:f1])
        u = _dot(hb, wu_ref[:, f0:f1])
        a = (g * jax.nn.sigmoid(g) * u).astype(BF16)
        d = _dot(a, wd_ref[f0:f1, :])
        acc = d if acc is None else acc + d
    return x + 0.5 * gt_ref[0] * acc


def _ffn_kernel(x_ref, sh_ref, sc_ref, gt_ref, g_ref, wg_ref, wu_ref, wd_ref, o_ref, *, fc):
    o_ref[0] = _swiglu_residual(x_ref[0], sh_ref, sc_ref, gt_ref, g_ref, wg_ref, wu_ref, wd_ref, fc)


def _mod_spec(sel):
    _, layer, j, row = sel
    if row is None:
        return pl.BlockSpec((None, 1, None, 1, D_MODEL), lambda b, t: (layer, b, j, 0, 0))
    return pl.BlockSpec((None, 1, None, 1, D_MODEL), lambda b, t: (layer, row, j, 0, 0))


def _ffn_half_step(x, shift, scale, gate, gain, wg, wu, wd, idx):
    b, l, d = x.shape
    tm = min(512, l)
    return pl.pallas_call(
        functools.partial(_ffn_kernel, fc=FFN_CHUNK), name="ffn_half_step",
        grid=(b, l // tm),
        in_specs=[
            pl.BlockSpec((1, tm, d), lambda b, t: (b, t, 0)),
            _mod_spec(shift), _mod_spec(scale), _mod_spec(gate),
            _resident((1, d)),
            _layer_slab(wg, idx), _layer_slab(wu, idx), _layer_slab(wd, idx),
        ],
        out_specs=pl.BlockSpec((1, tm, d), lambda b, t: (b, t, 0)),
        out_shape=jax.ShapeDtypeStruct(x.shape, F32),
        compiler_params=_cparams(("parallel", "parallel"), 56),
    )(x, shift[0], scale[0], gate[0], gain.reshape(1, d), wg, wu, wd)


_HY_COLS = 3 * HY_WIDTH
_SWA_Q = SWA_HEADS * HEAD_DIM
_SWA_KV = SWA_KV_HEADS * HEAD_DIM
_NA_W = NA_HEADS * HEAD_DIM
_OFF_SWA = _HY_COLS
_OFF_NA = _OFF_SWA + _SWA_Q + 2 * _SWA_KV
_OFF_GATE = _OFF_NA + 3 * _NA_W


def _inproj_kernel(x_ref, sh_ref, sc_ref, g_ref, cos_ref, sin_ref, hg_ref, bd_ref, w_ref, wkv_ref,
                   zhy_ref, qs_ref, ks_ref, vs_ref, qn_ref, kn_ref, vn_ref, gt_ref):
    hb = _rms_mod(x_ref[0], g_ref[...], sh_ref[0], sc_ref[0]).astype(BF16)
    cos = cos_ref[...]
    sin = sin_ref[...]
    bd = bd_ref[...]
    scale = HEAD_DIM ** -0.5
    proj = lambda lo, n: _dot(hb, w_ref[:, lo:lo + n])
    kvw = 2 * _SWA_KV
    raw = [proj(_OFF_SWA, _SWA_Q), _dot(hb, wkv_ref[:, :kvw]), proj(_OFF_NA, _NA_W), proj(_OFF_NA + _NA_W, _NA_W)]
    gates = proj(_OFF_GATE, 3 * D_MODEL)
    sums = [_head_sumsq(r, bd) for r in raw]
    vs_ref[0] = _dot(hb, wkv_ref[:, kvw:]).astype(BF16)
    vn_ref[0] = proj(_OFF_NA + 2 * _NA_W, _NA_W).astype(BF16)
    zhy_ref[0] = proj(0, _HY_COLS).astype(BF16)
    gt_ref[0] = jax.nn.sigmoid(gates).astype(BF16)
    qs, ks, qn, kn = [_head_normalize(r, s, hg_ref[i:i + 1, :]) for i, (r, s) in enumerate(zip(raw, sums))]
    qs_ref[0] = (_rope(qs, cos, sin) * scale).astype(BF16)
    ks_ref[0] = _rope(ks, cos, sin).astype(BF16)
    qn_ref[0] = (qn * scale).astype(BF16)
    kn_ref[0] = kn.astype(BF16)


def _in_projection(x, shift, scale, gain, cos, sin, head_gains, bd, w, w_kv_dup, layer):
    b, l, d = x.shape
    tm = min(512, l)
    kvw = 2 * _SWA_KV
    widths = [_HY_COLS, _SWA_Q, kvw, kvw, _NA_W, _NA_W, _NA_W, 3 * d]
    dtypes = [BF16] * 8
    return pl.pallas_call(
        _inproj_kernel, name="in_projection",
        grid=(b, l // tm),
        in_specs=[
            pl.BlockSpec((1, tm, d), lambda b, t: (b, t, 0)),
            _mod_spec(shift), _mod_spec(scale),
            _resident((1, d)),
            pl.BlockSpec((tm, LANES), lambda b, t: (t, 0)),
            pl.BlockSpec((tm, LANES), lambda b, t: (t, 0)),
            _resident(head_gains.shape), _resident(bd.shape), _layer_slab(w, (layer,)), _layer_slab(w_kv_dup, (layer,)),
        ],
        out_specs=[pl.BlockSpec((1, tm, n), lambda b, t: (b, t, 0)) for n in widths],
        out_shape=[jax.ShapeDtypeStruct((b, l, n), dt) for n, dt in zip(widths, dtypes)],
        compiler_params=_cparams(("parallel", "parallel"), 56),
    )(x, shift[0], scale[0], gain.reshape(1, d), cos, sin, head_gains, bd, w, w_kv_dup)


def _shortconv_kernel(z0_ref, z1_ref, z2_ref, w_ref, b_ref, o0_ref, o1_ref, o2_ref):
    l = z0_ref.shape[1]
    row = lax.broadcasted_iota(jnp.int32, z0_ref.shape[1:], 0)
    for g, (z_ref, o_ref) in enumerate(((z0_ref, o0_ref), (z1_ref, o1_ref), (z2_ref, o2_ref))):
        z = z_ref[0].astype(F32)
        prev = jnp.where(row == 0, 0.0, pltpu.roll(z, 1, axis=0))
        nxt = jnp.where(row == l - 1, 0.0, pltpu.roll(z, l - 1, axis=0))
        o_ref[0] = w_ref[g, 0:1, :] * prev + w_ref[g, 1:2, :] * z + w_ref[g, 2:3, :] * nxt + b_ref[g]


def _short_conv3(z, w, bias):
    b, l, n = z.shape
    cb = 256
    ncb = HY_WIDTH // cb
    wg = w.reshape(3, 3, HY_WIDTH).transpose(1, 0, 2)
    zin = lambda g: pl.BlockSpec((1, l, cb), lambda b, j: (b, 0, g * ncb + j))
    out = pl.BlockSpec((1, l, cb), lambda b, j: (b, 0, j))
    return pl.pallas_call(
        _shortconv_kernel, name="short_conv3",
        grid=(b, ncb),
        in_specs=[zin(0), zin(1), zin(2),
                  pl.BlockSpec((3, 3, cb), lambda b, j: (0, 0, j)),
                  pl.BlockSpec((3, 1, cb), lambda b, j: (0, 0, j))],
        out_specs=[out, out, out],
        out_shape=[jax.ShapeDtypeStruct((b, l, HY_WIDTH), F32)] * 3,
        compiler_params=_cparams(("parallel", "parallel"), 48),
    )(z, z, z, wg, bias.reshape(3, 1, HY_WIDTH))


def _filter_kernel(z_ref, w0, b0, w1, b1, w2, b2, wo, fr_ref, dl_ref, o_ref):
    z = z_ref[...]
    fr = fr_ref[...]
    a = jnp.sin(fr * (_dot3f(z, w0[...]) + b0[...]))
    a = jnp.sin(fr * (_dot3f(a, w1[...]) + b1[...]))
    a = jnp.sin(fr * (_dot3f(a, w2[...]) + b2[...]))
    hh = _dot3f(a, wo[...])
    t = z[:, 0:1]
    win = jnp.exp(-t * dl_ref[...])
    row = lax.broadcasted_iota(jnp.int32, win.shape, 0) + pl.program_id(0) * z.shape[0]
    w = HY_WIDTH
    for o in range(2):
        hp = hh[:, (2 * o) * w:(2 * o + 1) * w] * win
        hn = jnp.where(row == 0, 0.0, hh[:, (2 * o + 1) * w:(2 * o + 2) * w] * win)
        o_ref[0, :, o * w:(o + 1) * w] = hp + hn
        o_ref[1, :, o * w:(o + 1) * w] = hp - hn


def _hyena_filter_sums(zfeat, fp, deltas):
    l = zfeat.shape[0]
    tl = min(256, l)
    w0, b0, w1, b1, w2, b2, wo, fr = fp
    n = 2 * HY_WIDTH
    consts = [w0, b0, w1, b1, w2, b2, wo, fr, deltas]
    return pl.pallas_call(
        _filter_kernel, name="hyena_filter",
        grid=(l // tl,),
        in_specs=[pl.BlockSpec((tl, LANES), lambda t: (t, 0))] + [_resident(c.shape) for c in consts],
        out_specs=pl.BlockSpec((2, tl, n), lambda t: (0, t, 0)),
        out_shape=jax.ShapeDtypeStruct((2, l, n), F32),
        compiler_params=_cparams(("parallel",), 32),
    )(zfeat, *consts)


def _spectrum_kernel(a_ref, d_ref, c_ref, s_ref, kr_ref, ki_ref):
    kr_ref[...] = _dot(c_ref[...], a_ref[0].astype(BF16))
    ki_ref[...] = -_dot(s_ref[...], d_ref[0].astype(BF16))


def _filter_spectrum(ad, tabs):
    _, l, n = ad.shape
    fp = tabs["c_hi"].shape[0]
    fb = tabs["fb"]
    cb = 256
    tab = pl.BlockSpec((fb, l), lambda j, k: (k, 0))
    out = pl.BlockSpec((fb, cb), lambda j, k: (k, j))
    return pl.pallas_call(
        _spectrum_kernel, name="dense_spectrum",
        grid=(n // cb, fp // fb),
        in_specs=[pl.BlockSpec((1, l, cb), lambda j, k: (0, 0, j)),
                  pl.BlockSpec((1, l, cb), lambda j, k: (1, 0, j)), tab, tab],
        out_specs=[out, out],
        out_shape=[jax.ShapeDtypeStruct((fp, n), F32)] * 2,
        compiler_params=_cparams(("parallel", "parallel"), 48),
    )(ad, ad, tabs["c_hi"], tabs["s_hi"])


def _longconv_kernel(u_ref, g_ref, kr_ref, ki_ref, bias_ref, wk_ref, c_ref, s_ref, ct_ref, st_ref,
                     o_ref, ub_ref, acc_ref):
    kb = pl.program_id(2)

    @pl.when(kb == 0)
    def _():
        ub_ref[...] = u_ref[0].astype(BF16)
        acc_ref[...] = jnp.zeros_like(acc_ref)

    ub = ub_ref[...]
    xr = _dot(c_ref[...], ub)
    xi = -_dot(s_ref[...], ub)
    kr = kr_ref[...]
    ki = ki_ref[...]
    wk = wk_ref[...]
    yr = ((xr * kr - xi * ki) * wk).astype(BF16)
    yi = ((xr * ki + xi * kr) * wk).astype(BF16)
    acc_ref[...] += _dot(ct_ref[...], yr) - _dot(st_ref[...], yi)

    @pl.when(kb == pl.num_programs(2) - 1)
    def _():
        u = u_ref[0]
        o_ref[0] = g_ref[0] * (acc_ref[...] + u * bias_ref[...])


def _gated_long_conv(u_arr, g_arr, kr, ki, k_col, bias, tabs):
    b, l, _ = u_arr.shape
    cb = 256
    ncb = HY_WIDTH // cb
    fp = tabs["c_hi"].shape[0]
    fb = tabs["fb"]
    tab = pl.BlockSpec((fb, l), lambda b, j, k: (k, 0))
    tabt = pl.BlockSpec((l, fb), lambda b, j, k: (0, k))
    spec = pl.BlockSpec((fb, cb), lambda b, j, k: (k, k_col * ncb + j))
    return pl.pallas_call(
        _longconv_kernel, name="dense_long_conv",
        grid=(b, ncb, fp // fb),
        in_specs=[
            pl.BlockSpec((1, l, cb), lambda b, j, k: (b, 0, j)),
            pl.BlockSpec((1, l, cb), lambda b, j, k: (b, 0, j)),
            spec, spec,
            pl.BlockSpec((1, cb), lambda b, j, k: (0, j)),
            pl.BlockSpec((fb, 1), lambda b, j, k: (k, 0)),
            tab, tab, tabt, tabt,
        ],
        out_specs=pl.BlockSpec((1, l, cb), lambda b, j, k: (b, 0, j)),
        out_shape=jax.ShapeDtypeStruct((b, l, HY_WIDTH), F32),
        scratch_shapes=[pltpu.VMEM((l, cb), BF16), pltpu.VMEM((l, cb), F32)],
        compiler_params=_cparams(("parallel", "parallel", "arbitrary"), 56),
    )(u_arr, g_arr, kr, ki, bias.reshape(1, HY_WIDTH), tabs["wk"],
      tabs["c_hi"], tabs["s_hi"], tabs["ct_hi"], tabs["st_hi"])


FFT_INNER = 64
FFT_COLS = 256
FFT_K2_CHUNK = 16


def _fft_tables(l):
    n = 2 * l
    n1 = FFT_INNER
    n2 = n // n1
    h = n2 // 2
    ang = lambda idx, mod: (idx % mod).astype(F32) * (2.0 * math.pi / mod)
    k2 = jnp.arange(n2, dtype=jnp.int32)
    m2 = jnp.arange(h, dtype=jnp.int32)
    a1 = ang(k2[:, None] * m2[None, :], n2)
    c1, s1 = jnp.cos(a1), jnp.sin(a1)
    f1 = jnp.block([[c1, s1], [-s1, c1]])
    k1 = jnp.arange(n1, dtype=jnp.int32)
    a2 = ang(k1[None, None, :] * (k1[None, :, None] * n2 + k2[:, None, None]), n)
    c2, s2 = jnp.cos(a2), jnp.sin(a2)
    m = jnp.concatenate([jnp.concatenate([c2, s2], axis=2), jnp.concatenate([-s2, c2], axis=2)], axis=1)
    g = jnp.block([[c1.T, -s1.T], [s1.T, c1.T]]) * (1.0 / n)
    out = dict(n2=n2)
    for name, t in (("f1", f1), ("f1r", f1[:, :h]), ("m", m), ("mt", jnp.swapaxes(m, 1, 2)), ("g", g)):
        out[name + "_hi"] = t.astype(BF16)
    return out


def _mm(w_hi, w_lo, x):
    if w_lo is None:
        return _dot(w_hi, x.astype(BF16))
    hi, lo = _split(x)
    return _dot3(w_hi, w_lo, hi, lo)


def _row_block_matmul(w_hi, w_lo, src_ref, dst_ref, count):
    m, k = w_hi.shape

    def body(i, carry):
        x = src_ref[pl.ds(pl.multiple_of(i * k, k), k), :]
        dst_ref[pl.ds(pl.multiple_of(i * m, m), m), :] = _mm(w_hi, w_lo, x).astype(dst_ref.dtype)
        return carry
    lax.fori_loop(0, count, body, 0, unroll=4)


def _short_conv_rows(z, w_ref, b_ref):
    l = z.shape[0]
    row = lax.broadcasted_iota(jnp.int32, z.shape, 0)
    prev = jnp.where(row == 0, 0.0, pltpu.roll(z, 1, axis=0))
    nxt = jnp.where(row == l - 1, 0.0, pltpu.roll(z, l - 1, axis=0))
    return w_ref[0:1, :] * prev + w_ref[1:2, :] * z + w_ref[2:3, :] * nxt + b_ref[...]


def _fftconv_kernel(u_ref, g_ref, uw_ref, ub_ref, gw_ref, gb_ref, kr_ref, ki_ref, bias_ref,
                    f1_ref, m_ref, mt_ref, gm_ref, o_ref, z_scr, y_scr, p_scr, q_scr, *, u_conv):
    kc = pl.program_id(2)
    n1 = FFT_INNER
    cc = z_scr.shape[1]
    n2 = z_scr.shape[0] // n1
    h = n2 // 2

    def load_u(s):
        return _short_conv_rows(u_ref[s, 0].astype(F32), uw_ref, ub_ref) if u_conv else u_ref[s, 0]

    @pl.when(kc == 0)
    def _stage1():
        z = [jnp.swapaxes(load_u(s).astype(BF16).reshape(h, n1, cc), 0, 1) for s in range(2)]
        z_scr[...] = jnp.concatenate(z, axis=1).reshape(n1 * n2, cc)
        _row_block_matmul(f1_ref[...], None, z_scr, p_scr, n1)
        q_scr[...] = jnp.swapaxes(p_scr[...].reshape(n1, 2 * n2, cc), 0, 1).reshape(2 * n2 * n1, cc)

    ks = kr_ref.shape[0]
    rows = [(pl.multiple_of((kc * ks + j) * n1, n1), pl.multiple_of((n2 + kc * ks + j) * n1, n1)) for j in range(ks)]
    xs = [_mm(m_ref[j], None, jnp.concatenate([q_scr[pl.ds(r0, n1), :], q_scr[pl.ds(i0, n1), :]], axis=0))
          for j, (r0, i0) in enumerate(rows)]
    ys = []
    for j, x in enumerate(xs):
        xr, xi = x[:n1], x[n1:]
        kr, ki = kr_ref[j], ki_ref[j]
        ys.append(jnp.concatenate([xr * kr - xi * ki, xr * ki + xi * kr], axis=0))
    for j, (r0, i0) in enumerate(rows):
        b = _mm(mt_ref[j], None, ys[j]).astype(p_scr.dtype)
        p_scr[pl.ds(r0, n1), :] = b[:n1]
        p_scr[pl.ds(i0, n1), :] = b[n1:]

    @pl.when(kc == pl.num_programs(2) - 1)
    def _stage1_inv():
        q_scr[...] = jnp.swapaxes(p_scr[...].reshape(2 * n2, n1, cc), 0, 1).reshape(2 * n2 * n1, cc)
        _row_block_matmul(gm_ref[...], None, q_scr, y_scr, n1)
        y3 = y_scr[...].reshape(n1, n2, cc)
        for s in range(2):
            y = jnp.swapaxes(y3[:, s * h:(s + 1) * h, :], 0, 1).reshape(h * n1, cc)
            gate = _short_conv_rows(g_ref[s, 0].astype(F32), gw_ref, gb_ref)
            o_ref[s, 0] = (gate * (y + load_u(s) * bias_ref[...])).astype(o_ref.dtype)


def _fft_gated_long_conv(u, u_col, z_hy, g_col, short_w, short_b, kr, ki, k_col, bias, ft, out_dtype):
    b, l, _ = z_hy.shape
    c = HY_WIDTH
    n1, n2, cc, ks = FFT_INNER, ft["n2"], FFT_COLS, FFT_K2_CHUNK
    p, ncc = b // 2, c // cc
    u_conv = u is None
    if u_conv:
        u = z_hy
    tok = lambda col: pl.BlockSpec((2, 1, l, cc), lambda q, j, k: (0, q, 0, col * ncc + j))
    taps = lambda col: pl.BlockSpec((3, cc), lambda q, j, k: (0, col * ncc + j))
    row = lambda col: pl.BlockSpec((1, cc), lambda q, j, k: (0, col * ncc + j))
    filt = pl.BlockSpec((ks, n1, cc), lambda q, j, k: (k, 0, k_col * ncc + j))
    tab = pl.BlockSpec((ks, 2 * n1, 2 * n1), lambda q, j, k: (k, 0, 0))
    res = lambda a: pl.BlockSpec(a.shape, lambda q, j, k: (0,) * a.ndim)
    sb = short_b.reshape(1, -1)
    y = pl.pallas_call(
        functools.partial(_fftconv_kernel, u_conv=u_conv), name="fft_long_conv",
        grid=(p, ncc, n2 // ks),
        in_specs=[tok(u_col), tok(g_col), taps(u_col), row(u_col), taps(g_col), row(g_col), filt, filt, row(0),
                  res(ft["f1_hi"]), tab, tab, res(ft["g_hi"])],
        out_specs=tok(0),
        out_shape=jax.ShapeDtypeStruct((2, p, l, c), out_dtype),
        scratch_shapes=[pltpu.VMEM((n1 * n2, cc), BF16), pltpu.VMEM((n1 * n2, cc), F32),
                        pltpu.VMEM((2 * n2 * n1, cc), BF16), pltpu.VMEM((2 * n2 * n1, cc), BF16)],
        compiler_params=_cparams(("parallel", "parallel", "arbitrary"), 56),
    )(u.reshape(2, p, l, -1), z_hy.reshape(2, p, l, -1), short_w, sb, short_w, sb, kr, ki, bias.reshape(1, c),
      ft["f1_hi"], ft["m_hi"], ft["mt_hi"], ft["g_hi"])
    return y.reshape(b, l, c)


def _fftspec_kernel(a_ref, d_ref, f1_ref, m_ref, kr_ref, ki_ref, z_scr, p_scr, q_scr):
    n1 = FFT_INNER
    cc = z_scr.shape[1]
    h = z_scr.shape[0] // n1
    n2 = 2 * h
    for src_ref, out_ref, lo in ((a_ref, kr_ref, 0), (d_ref, ki_ref, n1)):
        z_scr[...] = jnp.swapaxes(src_ref[0].reshape(h, n1, cc), 0, 1).reshape(n1 * h, cc)
        _row_block_matmul(f1_ref[...], None, z_scr, p_scr, n1)
        q_scr[...] = jnp.swapaxes(p_scr[...].reshape(n1, 2 * n2, cc), 0, 1).reshape(2 * n2 * n1, cc)

        def body(k2, carry):
            r0 = pl.multiple_of(k2 * n1, n1)
            i0 = pl.multiple_of((n2 + k2) * n1, n1)
            slab = jnp.concatenate([q_scr[pl.ds(r0, n1), :], q_scr[pl.ds(i0, n1), :]], axis=0)
            out_ref[k2] = _mm(m_ref[k2][lo:lo + n1], None, slab)
            return carry
        lax.fori_loop(0, n2, body, 0, unroll=4)


def _fft_filter_spectrum(ad, ft):
    _, l, n = ad.shape
    n1, n2, cc = FFT_INNER, ft["n2"], FFT_COLS
    h = n2 // 2
    res = lambda a: pl.BlockSpec(a.shape, lambda j: (0,) * a.ndim)
    out = pl.BlockSpec((n2, n1, cc), lambda j: (0, 0, j))
    tabs = [ft["f1r_hi"], ft["m_hi"]]
    return pl.pallas_call(
        _fftspec_kernel, name="fft_filter_spectrum",
        grid=(n // cc,),
        in_specs=[pl.BlockSpec((1, l, cc), lambda j: (0, 0, j)), pl.BlockSpec((1, l, cc), lambda j: (1, 0, j))]
                 + [res(t) for t in tabs],
        out_specs=[out, out],
        out_shape=[jax.ShapeDtypeStruct((n2, n1, n), F32)] * 2,
        scratch_shapes=[pltpu.VMEM((n1 * h, cc), F32), pltpu.VMEM((2 * n2 * n1, cc), F32),
                        pltpu.VMEM((2 * n2 * n1, cc), F32)],
        compiler_params=_cparams(("parallel",), 56),
    )(ad, ad, *tabs)


def _softmax_weights(s, sink=None):
    m = jnp.max(s, axis=-1, keepdims=True)
    if sink is not None:
        m = jnp.maximum(m, sink)
    p = jnp.exp(s - m)
    den = jnp.sum(p, axis=-1, keepdims=True)
    if sink is not None:
        den = den + jnp.exp(sink - m)
    return p.astype(BF16), den


def _softmax_pv(s, v, sink=None):
    p, den = _softmax_weights(s, sink)
    return _dot(p, v) / den


def _qk(q, k):
    return lax.dot_general(q, k, (((1,), (1,)), ((), ())), preferred_element_type=F32)


def _half_mask(q_tile, half):
    lane = lax.broadcasted_iota(jnp.int32, q_tile.shape, 1)
    keep = (lane < HEAD_DIM) if half == 0 else (lane >= HEAD_DIM)
    return jnp.where(keep, q_tile, jnp.zeros_like(q_tile))


def _merge_halves(o_even, o_odd):
    lane = lax.broadcasted_iota(jnp.int32, o_even.shape, 1)
    return jnp.where(lane < HEAD_DIM, o_even, o_odd)


def _swa_kernel(sink_ref, q_ref, kp_ref, kc_ref, kn_ref, vp_ref, vc_ref, vn_ref, kx_ref, vx_ref, o_ref):
    n = pl.program_id(1)
    nb = pl.num_programs(1)
    blk = SWA_BLOCK
    g = SWA_HEADS // SWA_KV_HEADS
    n_ctx = kx_ref.shape[1]
    rows = g * blk
    ri = lax.broadcasted_iota(jnp.int32, (rows, 3 * blk + n_ctx), 0) & (blk - 1)
    ci = lax.broadcasted_iota(jnp.int32, (rows, 3 * blk + n_ctx), 1)
    ok_prev = (ci < blk) & (ci >= ri) & (n > 0)
    ok_cur = (ci >= blk) & (ci < 2 * blk)
    ok_next = (ci >= 2 * blk) & (ci < 3 * blk) & (ci - 2 * blk <= ri) & (n < nb - 1)
    ok = ok_prev | ok_cur | ok_next | (ci >= 3 * blk)
    bias = jnp.where(ok, 0.0, NEG_BIG)
    rsel = lax.broadcasted_iota(jnp.int32, (rows, 1), 0) >> (blk.bit_length() - 1)
    outs = [None] * SWA_HEADS
    kvs = range(SWA_KV_HEADS)
    lanes = [slice(kv * LANES, (kv + 1) * LANES) for kv in kvs]
    scores, sinks = [], []
    for kv in kvs:
        cs = lanes[kv]
        k = jnp.concatenate([kp_ref[0, :, cs], kc_ref[0, :, cs], kn_ref[0, :, cs], kx_ref[0, :, cs]], axis=0)
        q_parts = []
        sink = jnp.zeros((rows, 1), F32)
        for gi in range(g):
            h = kv * g + gi
            q_parts.append(_half_mask(q_ref[0, :, (h // 2) * LANES:(h // 2 + 1) * LANES], h % 2))
            sink = jnp.where(rsel == gi, sink_ref[h], sink)
        scores.append(_qk(jnp.concatenate(q_parts, axis=0), k))
        sinks.append(sink)
    probs = [_softmax_weights(scores[kv] + bias, sinks[kv]) for kv in kvs]
    for kv in kvs:
        cs = lanes[kv]
        v = jnp.concatenate([vp_ref[0, :, cs], vc_ref[0, :, cs], vn_ref[0, :, cs], vx_ref[0, :, cs]], axis=0)
        pw, den = probs[kv]
        o = _dot(pw, v) / den
        for gi in range(g):
            outs[kv * g + gi] = o[gi * blk:(gi + 1) * blk]
    for p in range(SWA_HEADS // 2):
        o_ref[0, :, p * LANES:(p + 1) * LANES] = _merge_halves(outs[2 * p], outs[2 * p + 1]).astype(o_ref.dtype)


def _window_gqa(q, k, v, kx, vx, sink):
    b, l, _ = q.shape
    nb = l // SWA_BLOCK
    n_ctx = kx.shape[1]
    kvw = k.shape[2]

    def blk(off):
        return pl.BlockSpec((1, SWA_BLOCK, kvw), lambda b, n: (b, jnp.clip(n + off, 0, nb - 1), 0))

    ctx = pl.BlockSpec((1, n_ctx, kvw), lambda b, n: (b, 0, 0))
    return pl.pallas_call(
        _swa_kernel, name="window_gqa",
        grid=(b, nb),
        in_specs=[
            pl.BlockSpec(memory_space=pltpu.SMEM),
            pl.BlockSpec((1, SWA_BLOCK, q.shape[2]), lambda b, n: (b, n, 0)),
            blk(-1), blk(0), blk(1), blk(-1), blk(0), blk(1), ctx, ctx,
        ],
        out_specs=pl.BlockSpec((1, SWA_BLOCK, q.shape[2]), lambda b, n: (b, n, 0)),
        out_shape=jax.ShapeDtypeStruct(q.shape, BF16),
        compiler_params=_cparams(("parallel", "parallel"), 32),
    )(sink, q, k, k, k, v, v, v, kx, vx)


def _na_bias_kernel(rpb_ref, o_ref):
    off = pl.program_id(0)
    h = pl.program_id(1)
    nd = 2 * NA_COLS - 1
    q = lax.broadcasted_iota(jnp.int32, (GRID_W, LANES), 0)
    lane = lax.broadcasted_iota(jnp.int32, (GRID_W, LANES), 1)
    kc = lane & (GRID_W - 1)
    upper = lane >= GRID_W
    cstart = jnp.clip(q - NA_COLS // 2, 0, GRID_W - NA_COLS)
    valid = (kc >= cstart) & (kc < cstart + NA_COLS)
    dc = jnp.clip(kc - q, -(NA_COLS - 1), NA_COLS - 1) + (NA_COLS - 1)
    for j in range(NA_MAX_ROWS // 2):
        base0 = (h * (2 * NA_MAX_ROWS - 1) + (2 * j - off + NA_MAX_ROWS - 1)) * nd
        base1 = base0 + nd
        t = jnp.zeros((GRID_W, LANES), F32)
        for d in range(nd):
            val = jnp.where(upper, rpb_ref[base1 + d], rpb_ref[base0 + d])
            t = jnp.where(dc == d, val, t)
        o_ref[0, 0, :, j * LANES:(j + 1) * LANES] = jnp.where(valid, t, NEG_BIG)


def _na_bias_table(rpb):
    return pl.pallas_call(
        _na_bias_kernel, name="na_bias_table",
        grid=(NA_MAX_ROWS, NA_HEADS),
        in_specs=[pl.BlockSpec(memory_space=pltpu.SMEM)],
        out_specs=pl.BlockSpec((1, 1, GRID_W, NA_MAX_ROWS * GRID_W), lambda o, h: (o, h, 0, 0)),
        out_shape=jax.ShapeDtypeStruct((NA_MAX_ROWS, NA_HEADS, GRID_W, NA_MAX_ROWS * GRID_W), F32),
        compiler_params=_cparams(("arbitrary", "arbitrary"), 16),
    )(rpb.reshape(-1))


NA_ROWS_PER_STEP = 4


def _na_kernel(q_ref, k_ref, v_ref, kx_ref, vx_ref, bias_ref, o_ref, *, rows):
    wr = NA_MAX_ROWS
    nloc = wr * GRID_W
    for i in range(NA_ROWS_PER_STEP):
        r = pl.program_id(1) * NA_ROWS_PER_STEP + i
        first = jnp.clip(r - wr // 2, 0, rows - wr)
        start = pl.multiple_of(first * GRID_W, GRID_W)
        off = r - first
        qr = slice(i * GRID_W, (i + 1) * GRID_W)
        pairs = range(NA_HEADS // 2)
        lanes = [slice(p * LANES, (p + 1) * LANES) for p in pairs]
        scores = []
        for p in pairs:
            q_tile = q_ref[0, qr, lanes[p]]
            q2 = jnp.concatenate([_half_mask(q_tile, 0), _half_mask(q_tile, 1)], axis=0)
            k = jnp.concatenate([k_ref[0, pl.ds(start, nloc), lanes[p]], kx_ref[0, :, lanes[p]]], axis=0)
            scores.append(_qk(q2, k))
        probs = []
        for p in pairs:
            s = scores[p]
            bias = jnp.concatenate([bias_ref[off, 2 * p], bias_ref[off, 2 * p + 1]], axis=0)
            probs.append(_softmax_weights(jnp.concatenate([s[:, :nloc] + bias, s[:, nloc:]], axis=1)))
        for p in pairs:
            pw, den = probs[p]
            v = jnp.concatenate([v_ref[0, pl.ds(start, nloc), lanes[p]], vx_ref[0, :, lanes[p]]], axis=0)
            o = _dot(pw, v) / den
            o_ref[0, qr, lanes[p]] = _merge_halves(o[:GRID_W], o[GRID_W:]).astype(o_ref.dtype)


def _neighbourhood_attention(q, k, v, kx, vx, bias_tab):
    b, l, w = q.shape
    rows = l // GRID_W
    n_ctx = kx.shape[1]
    tq = NA_ROWS_PER_STEP * GRID_W
    full = pl.BlockSpec((1, l, w), lambda b, r: (b, 0, 0))
    ctx = pl.BlockSpec((1, n_ctx, w), lambda b, r: (b, 0, 0))
    return pl.pallas_call(
        functools.partial(_na_kernel, rows=rows), name="neighbourhood_attention",
        grid=(b, rows // NA_ROWS_PER_STEP),
        in_specs=[pl.BlockSpec((1, tq, w), lambda b, r: (b, r, 0)), full, full, ctx, ctx,
                  _resident(bias_tab.shape)],
        out_specs=pl.BlockSpec((1, tq, w), lambda b, r: (b, r, 0)),
        out_shape=jax.ShapeDtypeStruct(q.shape, BF16),
        compiler_params=_cparams(("parallel", "arbitrary"), 48),
    )(q, k, v, kx, vx, bias_tab)


def _ctx_attn_kernel(sink_ref, qs_ref, ks_ref, vs_ref, qn_ref, kn_ref, vn_ref, os_ref, on_ref):
    n = qs_ref.shape[1]
    g = SWA_HEADS // SWA_KV_HEADS
    rsel = lax.broadcasted_iota(jnp.int32, (g * n, 1), 0) >> (n.bit_length() - 1)
    outs = [None] * SWA_HEADS
    for kv in range(SWA_KV_HEADS):
        cs = slice(kv * LANES, (kv + 1) * LANES)
        q_parts = []
        sink = jnp.zeros((g * n, 1), F32)
        for gi in range(g):
            h = kv * g + gi
            q_parts.append(_half_mask(qs_ref[0, :, (h // 2) * LANES:(h // 2 + 1) * LANES], h % 2))
            sink = jnp.where(rsel == gi, sink_ref[h], sink)
        o = _softmax_pv(_qk(jnp.concatenate(q_parts, axis=0), ks_ref[0, :, cs]), vs_ref[0, :, cs], sink)
        for gi in range(g):
            outs[kv * g + gi] = o[gi * n:(gi + 1) * n]
    for p in range(SWA_HEADS // 2):
        os_ref[0, :, p * LANES:(p + 1) * LANES] = _merge_halves(outs[2 * p], outs[2 * p + 1]).astype(os_ref.dtype)
    for p in range(NA_HEADS // 2):
        cs = slice(p * LANES, (p + 1) * LANES)
        q_tile = qn_ref[0, :, cs]
        q2 = jnp.concatenate([_half_mask(q_tile, 0), _half_mask(q_tile, 1)], axis=0)
        o = _softmax_pv(_qk(q2, kn_ref[0, :, cs]), vn_ref[0, :, cs])
        on_ref[0, :, cs] = _merge_halves(o[:n], o[n:]).astype(on_ref.dtype)


def _context_attention(qs, ks, vs, qn, kn, vn, sink):
    b, n, w = qs.shape
    spec = lambda a: pl.BlockSpec((1,) + a.shape[1:], lambda b: (b, 0, 0))
    return pl.pallas_call(
        _ctx_attn_kernel, name="context_attention",
        grid=(b,),
        in_specs=[pl.BlockSpec(memory_space=pltpu.SMEM)] + [spec(a) for a in (qs, ks, vs, qn, kn, vn)],
        out_specs=[spec(qs), spec(qn)],
        out_shape=[jax.ShapeDtypeStruct(qs.shape, BF16), jax.ShapeDtypeStruct(qn.shape, BF16)],
        compiler_params=_cparams(("parallel",), 32),
    )(sink, qs, ks, vs, qn, kn, vn)


def _merge_ffn_kernel(x_ref, mg_ref, yh_ref, ys_ref, yn_ref, zg_ref, wb_ref, wo_ref,
                      sh_ref, sc_ref, gt_ref, g_ref, wg_ref, wu_ref, wd_ref, o_ref, *, fc):
    d = D_MODEL
    ys = (yh_ref[0].astype(BF16), ys_ref[0], yn_ref[0])
    m = None
    for n in range(3):
        proj = _dot(ys[n], wb_ref[n])
        term = zg_ref[0, :, n * d:(n + 1) * d].astype(F32) * proj
        m = term if m is None else m + term
    x = x_ref[0] + mg_ref[0] * _dot(m.astype(BF16), wo_ref[...])
    o_ref[0] = _swiglu_residual(x, sh_ref, sc_ref, gt_ref, g_ref, wg_ref, wu_ref, wd_ref, fc)


def _merge_and_ffn(x, mix_gate, y_hy, y_swa, y_na, gates, w_branch, w_out, layer,
                   shift, scale, gate, gain, wg, wu, wd):
    b, l, d = x.shape
    tm = min(512, l)
    tok = lambda n: pl.BlockSpec((1, tm, n), lambda b, t: (b, t, 0))
    idx = (layer, 1)
    return pl.pallas_call(
        functools.partial(_merge_ffn_kernel, fc=FFN_CHUNK), name="merge_ffn",
        grid=(b, l // tm),
        in_specs=[tok(d), _mod_spec(mix_gate), tok(HY_WIDTH), tok(HY_WIDTH), tok(HY_WIDTH), tok(3 * d),
                  _layer_slab(w_branch, (layer,)), _layer_slab(w_out, (layer,)),
                  _mod_spec(shift), _mod_spec(scale), _mod_spec(gate), _resident((1, d)),
                  _layer_slab(wg, idx), _layer_slab(wu, idx), _layer_slab(wd, idx)],
        out_specs=tok(d),
        out_shape=jax.ShapeDtypeStruct(x.shape, F32),
        compiler_params=_cparams(("parallel", "parallel"), 58),
    )(x, mix_gate[0], y_hy, y_swa, y_na, gates, w_branch, w_out, shift[0], scale[0], gate[0], gain.reshape(1, d),
      wg, wu, wd)


def _dft_tables(l):
    n = 2 * l
    fb = 256 if l >= 1024 else LANES * ((l + 1 + LANES - 1) // LANES)
    fp = fb * ((l + 1 + fb - 1) // fb)
    k = jnp.arange(fp, dtype=jnp.int32)[:, None]
    t = jnp.arange(l, dtype=jnp.int32)[None, :]
    ang = ((k * t) % n).astype(F32) * (2.0 * math.pi / n)
    live = k <= l
    c = jnp.where(live, jnp.cos(ang), 0.0)
    s = jnp.where(live, jnp.sin(ang), 0.0)
    c_hi = c.astype(BF16)
    s_hi = s.astype(BF16)
    wk = jnp.where((k == 0) | (k == l), 1.0, 2.0) * jnp.where(live, 1.0 / n, 0.0)
    return dict(fb=fb, c_hi=c_hi, s_hi=s_hi, ct_hi=c_hi.T, st_hi=s_hi.T, wk=wk.astype(F32))


def _filter_features(l):
    t = jnp.linspace(0.0, 1.0, l, dtype=F32)[:, None]
    w = (2.0 * math.pi / l) * jnp.arange(l, dtype=F32)[:, None]
    f = jnp.linspace(1e-4, HY_BANDS - 1, HY_BANDS, dtype=F32)[None, :]
    z = jnp.concatenate([t, jnp.cos(f * w), -jnp.sin(f * w)], axis=-1)
    return jnp.pad(z, ((0, 0), (0, LANES - HY_EMB)))


def _decay_rates():
    max_decay = math.log(HY_TARGET) / HY_FAST_DECAY
    min_decay = math.log(HY_TARGET) / HY_SLOW_DECAY
    return jnp.abs(jnp.linspace(min_decay, max_decay, HY_WIDTH, dtype=F32))[None, :]


def _rope_tables(l):
    pos = jnp.arange(l)
    row = (pos // GRID_W).astype(F32)
    col = (pos % GRID_W).astype(F32)
    half = HEAD_DIM // 2
    inv = 1.0 / (ROPE_BASE ** (jnp.arange(0, half, 2, dtype=F32) / half))
    ar = row[:, None] * inv[None, :]
    ac = col[:, None] * inv[None, :]
    cos = jnp.concatenate([jnp.cos(ar), jnp.cos(ar), jnp.cos(ac), jnp.cos(ac)], axis=-1)
    sin = jnp.concatenate([-jnp.sin(ar), jnp.sin(ar), -jnp.sin(ac), jnp.sin(ac)], axis=-1)
    return jnp.tile(cos, (1, 2)), jnp.tile(sin, (1, 2))


def _pad_to(a, shape):
    return jnp.pad(a, [(0, s - d) for d, s in zip(a.shape, shape)])


def _use_fft(bsz, length):
    return bsz % 2 == 0 and length % (8 * FFT_INNER) == 0


def _conv_tables(bsz, length):
    return _fft_tables(length) if _use_fft(bsz, length) else _dft_tables(length)


def _hyena_branch(z_hy, short_w, short_b, feat, fparams, deltas, bias, tabs):
    ad = _hyena_filter_sums(feat, fparams, deltas)
    if "n2" in tabs:
        kr, ki = _fft_filter_spectrum(ad, tabs)
        y1 = _fft_gated_long_conv(None, 0, z_hy, 1, short_w, short_b, kr, ki, 0, bias[0], tabs, F32)
        return _fft_gated_long_conv(y1, 0, z_hy, 2, short_w, short_b, kr, ki, 1, bias[1], tabs, BF16)
    v, x1, x2 = _short_conv3(z_hy, short_w, short_b)
    kr, ki = _filter_spectrum(ad, tabs)
    conv = functools.partial(_gated_long_conv, tabs=tabs)
    y1 = conv(v, x1, kr, ki, 0, bias[0])
    return conv(y1, x2, kr, ki, 1, bias[1])


def kernel(x, c, ctx, c_ctx, w_ada, b_ada, norm_g, ffn_w_gate, ffn_w_up, ffn_w_down,
           w_in, hy_short_w, hy_short_b, hy_pe_w0, hy_pe_b0, hy_pe_w1, hy_pe_b1,
           hy_pe_w2, hy_pe_b2, hy_pe_wout, hy_sin_freq, hy_bias,
           swa_q_gain, swa_k_gain, swa_sink, na_q_gain, na_k_gain, na_rpb,
           w_branch, w_out):
    bsz, seq, d = x.shape
    n_ctx = ctx.shape[1]
    depth = w_ada.shape[0]

    c16 = _pad_to(jnp.concatenate([c, c_ctx[None, :]], axis=0), (MOD_ROWS, d))
    mods = _adaln_mods(c16, w_ada, b_ada).reshape(depth, MOD_ROWS, N_MOD, 1, d)

    tabs_x = _conv_tables(bsz, seq)
    tabs_c = _conv_tables(bsz, n_ctx)
    feat_x = _filter_features(seq)
    feat_c = _filter_features(n_ctx)
    deltas = _decay_rates()
    cos_x, sin_x = _rope_tables(seq)
    cos_c = jnp.ones((n_ctx, LANES), F32)
    sin_c = jnp.zeros((n_ctx, LANES), F32)
    eye = jnp.arange(MXU_DIM) // HEAD_DIM
    bd = (eye[:, None] == eye[None, :]).astype(BF16)

    wg = ffn_w_gate.astype(BF16)
    wu = ffn_w_up.astype(BF16)
    wd = ffn_w_down.astype(BF16)
    wb = w_branch.astype(BF16)
    wo = w_out.astype(BF16)
    win = w_in.astype(BF16)
    kv = win[:, :, _OFF_SWA + _SWA_Q:_OFF_NA].reshape(depth, d, 2, SWA_KV_HEADS, 1, HEAD_DIM)
    win_kv = jnp.broadcast_to(kv, (depth, d, 2, SWA_KV_HEADS, 2, HEAD_DIM)).reshape(depth, d, 4 * _SWA_KV)

    xc = ctx
    for i in range(depth):
        last = i == depth - 1
        mx = lambda j: (mods, i, j, None)
        mc = lambda j: (mods, i, j, bsz)
        tile2 = lambda g: jnp.tile(g, 2 * MXU_DIM // LANES)
        head_gains = jnp.stack([tile2(swa_q_gain[i]), tile2(swa_k_gain[i]), tile2(na_q_gain[i]), tile2(na_k_gain[i])])
        fparams = (_pad_to(hy_pe_w0[i], (LANES, LANES)), _pad_to(hy_pe_b0[i][None], (1, LANES)),
                   _pad_to(hy_pe_w1[i], (LANES, LANES)), _pad_to(hy_pe_b1[i][None], (1, LANES)),
                   _pad_to(hy_pe_w2[i], (LANES, LANES)), _pad_to(hy_pe_b2[i][None], (1, LANES)),
                   _pad_to(hy_pe_wout[i], (LANES, 4 * HY_WIDTH)), _pad_to(hy_sin_freq[i][None], (1, LANES)))

        x = _ffn_half_step(x, mx(0), mx(1), mx(2), norm_g[i, 0], wg, wu, wd, (i, 0))
        xc = _ffn_half_step(xc, mc(0), mc(1), mc(2), norm_g[i, 0], wg, wu, wd, (i, 0))

        z_hy, q_s, k_s, v_s, q_n, k_n, v_n, gates = _in_projection(
            x, mx(3), mx(4), norm_g[i, 1], cos_x, sin_x, head_gains, bd, win, win_kv, i)
        zc_hy, qc_s, kc_s, vc_s, qc_n, kc_n, vc_n, gates_c = _in_projection(
            xc, mc(3), mc(4), norm_g[i, 1], cos_c, sin_c, head_gains, bd, win, win_kv, i)

        y_hy = _hyena_branch(z_hy, hy_short_w[i], hy_short_b[i], feat_x, fparams, deltas, hy_bias[i], tabs_x)
        y_swa = _window_gqa(q_s, k_s, v_s, kc_s, vc_s, swa_sink[i])
        y_na = _neighbourhood_attention(q_n, k_n, v_n, kc_n, vc_n, _na_bias_table(na_rpb[i]))
        x = _merge_and_ffn(x, mx(5), y_hy, y_swa, y_na, gates, wb, wo, i,
                           mx(6), mx(7), mx(8), norm_g[i, 2], wg, wu, wd)

        if not last:
            yc_hy = _hyena_branch(zc_hy, hy_short_w[i], hy_short_b[i], feat_c, fparams, deltas, hy_bias[i], tabs_c)
            yc_swa, yc_na = _context_attention(qc_s, kc_s, vc_s, qc_n, kc_n, vc_n, swa_sink[i])
            xc = _merge_and_ffn(xc, mc(5), yc_hy, yc_swa, yc_na, gates_c, wb, wo, i,
                                mc(6), mc(7), mc(8), norm_g[i, 2], wg, wu, wd)
    return x
```

```python
import functools
import math

import jax
import jax.numpy as jnp
from jax import lax
from jax.experimental import pallas as pl
from jax.experimental.pallas import tpu as pltpu

F32 = jnp.float32
BF16 = jnp.bfloat16

D_MODEL = 1024
DEPTH = 4
GRID_W = 64
HEAD_DIM = 64
N_MOD = 9
RMS_EPS = 1e-6

HY_WIDTH = D_MODEL // 2
HY_CHUNK = 256
HY_EMB = 33
HY_BANDS = (HY_EMB - 1) // 2
HY_FFN = 64
HY_FAST_DECAY = 0.3
HY_SLOW_DECAY = 1.5
HY_TARGET = 1e-2

SWA_HEADS = 8
SWA_KV_HEADS = 2
SWA_WINDOW = 128
SWA_BLOCK = 128
ROPE_BASE = 10000.0

NA_HEADS = 8
NA_MAX_ROWS = 8
NA_COLS = 16
NA_COL_BLOCK = 16

FFN_HIDDEN = 256 * ((8 * D_MODEL // 3 + 255) // 256)
FFN_CHUNK = 1536

LANES = 128
MXU_DIM = 256
MOD_ROWS = 16
NEG_BIG = -1e30


def _cparams(sem, vmem_mb):
    return pltpu.CompilerParams(dimension_semantics=sem, vmem_limit_bytes=vmem_mb * 1024 * 1024)


def _resident(shape):
    nd = len(shape)
    return pl.BlockSpec(shape, lambda *_: (0,) * nd, pipeline_mode=pl.Buffered(1))


def _layer_slab(arr, idx):
    rest = arr.shape[len(idx):]
    return pl.BlockSpec((None,) * len(idx) + rest, lambda *_: tuple(idx) + (0,) * len(rest),
                        pipeline_mode=pl.Buffered(1))


def _split(x):
    hi = x.astype(BF16)
    lo = (x - hi.astype(F32)).astype(BF16)
    return hi, lo


def _dot(a, b):
    return jnp.dot(a, b, preferred_element_type=F32)


def _dot3(a_hi, a_lo, b_hi, b_lo):
    return _dot(a_hi, b_hi) + _dot(a_lo, b_hi) + _dot(a_hi, b_lo)


def _dot3f(a, b):
    a_hi, a_lo = _split(a)
    b_hi, b_lo = _split(b)
    return _dot3(a_hi, a_lo, b_hi, b_lo)


def _rms_mod(x, gain, shift, scale):
    ms = jnp.mean(x * x, axis=-1, keepdims=True)
    y = x * lax.rsqrt(ms + RMS_EPS) * gain
    return y * (1.0 + scale) + shift


def _head_sumsq(x, bd):
    return [_dot(jnp.square(x[:, j * MXU_DIM:(j + 1) * MXU_DIM]).astype(BF16), bd) for j in range(x.shape[1] // MXU_DIM)]


def _head_normalize(x, sumsq, gain):
    outs = [x[:, j * MXU_DIM:(j + 1) * MXU_DIM] * lax.rsqrt(s * (1.0 / HEAD_DIM) + RMS_EPS) * gain
            for j, s in enumerate(sumsq)]
    return outs[0] if len(outs) == 1 else jnp.concatenate(outs, axis=1)


def _rope(x, cos, sin):
    lane = lax.broadcasted_iota(jnp.int32, (x.shape[0], LANES), 1)
    first = (lane & 31) < 16
    outs = []
    for j in range(x.shape[1] // LANES):
        xc = x[:, j * LANES:(j + 1) * LANES]
        partner = jnp.where(first, pltpu.roll(xc, LANES - 16, axis=1), pltpu.roll(xc, 16, axis=1))
        outs.append(xc * cos + partner * sin)
    return outs[0] if len(outs) == 1 else jnp.concatenate(outs, axis=1)


def _mods_kernel(c_ref, w_ref, b_ref, o_ref):
    c = c_ref[...]
    a = c * jax.nn.sigmoid(c)
    o_ref[0] = _dot3f(a, w_ref[0]) + b_ref[0]


def _adaln_mods(c16, w_ada, b_ada):
    depth, d, n = w_ada.shape
    nb = 1152
    return pl.pallas_call(
        _mods_kernel, name="adaln_mods",
        grid=(depth, n // nb),
        in_specs=[
            pl.BlockSpec((MOD_ROWS, d), lambda i, j: (0, 0)),
            pl.BlockSpec((1, d, nb), lambda i, j: (i, 0, j)),
            pl.BlockSpec((1, 1, nb), lambda i, j: (i, 0, j)),
        ],
        out_specs=pl.BlockSpec((1, MOD_ROWS, nb), lambda i, j: (i, 0, j)),
        out_shape=jax.ShapeDtypeStruct((depth, MOD_ROWS, n), F32),
        compiler_params=_cparams(("arbitrary", "arbitrary"), 48),
    )(c16, w_ada, b_ada.reshape(depth, 1, n))


def _swiglu_residual(x, sh_ref, sc_ref, gt_ref, g_ref, wg_ref, wu_ref, wd_ref, fc):
    hb = _rms_mod(x, g_ref[...], sh_ref[0], sc_ref[0]).astype(BF16)
    acc = None
    for f0 in range(0, FFN_HIDDEN, fc):
        f1 = min(f0 + fc, FFN_HIDDEN)
        g = _dot(hb, wg_ref[:, f0:f1])
        u = _dot(hb, wu_ref[:, f0:f1])
        a = (g * jax.nn.sigmoid(g) * u).astype(BF16)
        d = _dot(a, wd_ref[f0:f1, :])
        acc = d if acc is None else acc + d
    return x + 0.5 * gt_ref[0] * acc


def _ffn_kernel(x_ref, sh_ref, sc_ref, gt_ref, g_ref, wg_ref, wu_ref, wd_ref, o_ref, *, fc):
    o_ref[0] = _swiglu_residual(x_ref[0], sh_ref, sc_ref, gt_ref, g_ref, wg_ref, wu_ref, wd_ref, fc)


def _mod_spec(sel):
    _, layer, j, row = sel
    if row is None:
        return pl.BlockSpec((None, 1, None, 1, D_MODEL), lambda b, t: (layer, b, j, 0, 0))
    return pl.BlockSpec((None, 1, None, 1, D_MODEL), lambda b, t: (layer, row, j, 0, 0))


def _ffn_half_step(x, shift, scale, gate, gain, wg, wu, wd, idx):
    b, l, d = x.shape
    tm = min(512, l)
    return pl.pallas_call(
        functools.partial(_ffn_kernel, fc=FFN_CHUNK), name="ffn_half_step",
        grid=(b, l // tm),
        in_specs=[
            pl.BlockSpec((1, tm, d), lambda b, t: (b, t, 0)),
            _mod_spec(shift), _mod_spec(scale), _mod_spec(gate),
            _resident((1, d)),
            _layer_slab(wg, idx), _layer_slab(wu, idx), _layer_slab(wd, idx),
        ],
        out_specs=pl.BlockSpec((1, tm, d), lambda b, t: (b, t, 0)),
        out_shape=jax.ShapeDtypeStruct(x.shape, F32),
        compiler_params=_cparams(("parallel", "parallel"), 56),
    )(x, shift[0], scale[0], gate[0], gain.reshape(1, d), wg, wu, wd)


_HY_COLS = 3 * HY_WIDTH
_SWA_Q = SWA_HEADS * HEAD_DIM
_SWA_KV = SWA_KV_HEADS * HEAD_DIM
_NA_W = NA_HEADS * HEAD_DIM
_OFF_SWA = _HY_COLS
_OFF_NA = _OFF_SWA + _SWA_Q + 2 * _SWA_KV
_OFF_GATE = _OFF_NA + 3 * _NA_W


def _inproj_kernel(x_ref, sh_ref, sc_ref, g_ref, cos_ref, sin_ref, hg_ref, bd_ref, w_ref, wkv_ref,
                   zhy_ref, qs_ref, ks_ref, vs_ref, qn_ref, kn_ref, vn_ref, gt_ref):
    hb = _rms_mod(x_ref[0], g_ref[...], sh_ref[0], sc_ref[0]).astype(BF16)
    cos = cos_ref[...]
    sin = sin_ref[...]
    bd = bd_ref[...]
    scale = HEAD_DIM ** -0.5
    proj = lambda lo, n: _dot(hb, w_ref[:, lo:lo + n])
    kvw = 2 * _SWA_KV
    raw = [proj(_OFF_SWA, _SWA_Q), _dot(hb, wkv_ref[:, :kvw]), proj(_OFF_NA, _NA_W), proj(_OFF_NA + _NA_W, _NA_W)]
    gates = proj(_OFF_GATE, 3 * D_MODEL)
    sums = [_head_sumsq(r, bd) for r in raw]
    vs_ref[0] = _dot(hb, wkv_ref[:, kvw:]).astype(BF16)
    vn_ref[0] = proj(_OFF_NA + 2 * _NA_W, _NA_W).astype(BF16)
    zhy = proj(0, _HY_COLS).astype(BF16)
    for c in range(_HY_COLS // HY_CHUNK):
        zhy_ref[0, c] = zhy[:, c * HY_CHUNK:(c + 1) * HY_CHUNK]
    gt_ref[0] = jax.nn.sigmoid(gates).astype(BF16)
    qs, ks, qn, kn = [_head_normalize(r, s, hg_ref[i:i + 1, :]) for i, (r, s) in enumerate(zip(raw, sums))]
    qs_ref[0] = (_rope(qs, cos, sin) * scale).astype(BF16)
    ks_ref[0] = _rope(ks, cos, sin).astype(BF16)
    qn_ref[0] = (qn * scale).astype(BF16)
    kn_ref[0] = kn.astype(BF16)


def _in_projection(x, shift, scale, gain, cos, sin, head_gains, bd, w, w_kv_dup, layer):
    b, l, d = x.shape
    tm = min(512, l)
    kvw = 2 * _SWA_KV
    widths = [_HY_COLS, _SWA_Q, kvw, kvw, _NA_W, _NA_W, _NA_W, 3 * d]
    dtypes = [BF16] * 8
    return pl.pallas_call(
        _inproj_kernel, name="in_projection",
        grid=(b, l // tm),
        in_specs=[
            pl.BlockSpec((1, tm, d), lambda b, t: (b, t, 0)),
            _mod_spec(shift), _mod_spec(scale),
            _resident((1, d)),
            pl.BlockSpec((tm, LANES), lambda b, t: (t, 0)),
            pl.BlockSpec((tm, LANES), lambda b, t: (t, 0)),
            _resident(head_gains.shape), _resident(bd.shape), _layer_slab(w, (layer,)), _layer_slab(w_kv_dup, (layer,)),
        ],
        out_specs=[pl.BlockSpec((1, _HY_COLS // HY_CHUNK, tm, HY_CHUNK), lambda b, t: (b, 0, t, 0))]
                  + [pl.BlockSpec((1, tm, n), lambda b, t: (b, t, 0)) for n in widths[1:]],
        out_shape=[jax.ShapeDtypeStruct((b, _HY_COLS // HY_CHUNK, l, HY_CHUNK), BF16)]
                  + [jax.ShapeDtypeStruct((b, l, n), dt) for n, dt in zip(widths[1:], dtypes[1:])],
        compiler_params=_cparams(("parallel", "parallel"), 56),
    )(x, shift[0], scale[0], gain.reshape(1, d), cos, sin, head_gains, bd, w, w_kv_dup)


def _shortconv_kernel(z0_ref, z1_ref, z2_ref, w_ref, b_ref, o0_ref, o1_ref, o2_ref):
    l = z0_ref.shape[1]
    row = lax.broadcasted_iota(jnp.int32, z0_ref.shape[1:], 0)
    for g, (z_ref, o_ref) in enumerate(((z0_ref, o0_ref), (z1_ref, o1_ref), (z2_ref, o2_ref))):
        z = z_ref[0].astype(F32)
        prev = jnp.where(row == 0, 0.0, pltpu.roll(z, 1, axis=0))
        nxt = jnp.where(row == l - 1, 0.0, pltpu.roll(z, l - 1, axis=0))
        o_ref[0] = w_ref[g, 0:1, :] * prev + w_ref[g, 1:2, :] * z + w_ref[g, 2:3, :] * nxt + b_ref[g]


def _short_conv3(z, w, bias):
    b, _, l, cb = z.shape
    ncb = HY_WIDTH // cb
    wg = w.reshape(3, 3, HY_WIDTH).transpose(1, 0, 2)
    zin = lambda g: pl.BlockSpec((None, 1, l, cb), lambda b, j: (b, g * ncb + j, 0, 0))
    out = pl.BlockSpec((1, l, cb), lambda b, j: (b, 0, j))
    return pl.pallas_call(
        _shortconv_kernel, name="short_conv3",
        grid=(b, ncb),
        in_specs=[zin(0), zin(1), zin(2),
                  pl.BlockSpec((3, 3, cb), lambda b, j: (0, 0, j)),
                  pl.BlockSpec((3, 1, cb), lambda b, j: (0, 0, j))],
        out_specs=[out, out, out],
        out_shape=[jax.ShapeDtypeStruct((b, l, HY_WIDTH), F32)] * 3,
        compiler_params=_cparams(("parallel", "parallel"), 48),
    )(z, z, z, wg, bias.reshape(3, 1, HY_WIDTH))


def _filter_kernel(z_ref, w0, b0, w1, b1, w2, b2, wo, fr_ref, dl_ref, o_ref):
    z = z_ref[...]
    fr = fr_ref[...]
    a = jnp.sin(fr * (_dot3f(z, w0[...]) + b0[...]))
    a = jnp.sin(fr * (_dot3f(a, w1[...]) + b1[...]))
    a = jnp.sin(fr * (_dot3f(a, w2[...]) + b2[...]))
    hh = _dot3f(a, wo[...])
    t = z[:, 0:1]
    win = jnp.exp(-t * dl_ref[...])
    row = lax.broadcasted_iota(jnp.int32, win.shape, 0) + pl.program_id(0) * z.shape[0]
    w = HY_WIDTH
    for o in range(2):
        hp = hh[:, (2 * o) * w:(2 * o + 1) * w] * win
        hn = jnp.where(row == 0, 0.0, hh[:, (2 * o + 1) * w:(2 * o + 2) * w] * win)
        o_ref[0, :, o * w:(o + 1) * w] = hp + hn
        o_ref[1, :, o * w:(o + 1) * w] = hp - hn


def _hyena_filter_sums(zfeat, fp, deltas):
    l = zfeat.shape[0]
    tl = min(256, l)
    w0, b0, w1, b1, w2, b2, wo, fr = fp
    n = 2 * HY_WIDTH
    consts = [w0, b0, w1, b1, w2, b2, wo, fr, deltas]
    return pl.pallas_call(
        _filter_kernel, name="hyena_filter",
        grid=(l // tl,),
        in_specs=[pl.BlockSpec((tl, LANES), lambda t: (t, 0))] + [_resident(c.shape) for c in consts],
        out_specs=pl.BlockSpec((2, tl, n), lambda t: (0, t, 0)),
        out_shape=jax.ShapeDtypeStruct((2, l, n), F32),
        compiler_params=_cparams(("parallel",), 32),
    )(zfeat, *consts)


def _spectrum_kernel(a_ref, d_ref, c_ref, s_ref, kr_ref, ki_ref):
    kr_ref[...] = _dot(c_ref[...], a_ref[0].astype(BF16))
    ki_ref[...] = -_dot(s_ref[...], d_ref[0].astype(BF16))


def _filter_spectrum(ad, tabs):
    _, l, n = ad.shape
    fp = tabs["c_hi"].shape[0]
    fb = tabs["fb"]
    cb = 256
    tab = pl.BlockSpec((fb, l), lambda j, k: (k, 0))
    out = pl.BlockSpec((fb, cb), lambda j, k: (k, j))
    return pl.pallas_call(
        _spectrum_kernel, name="dense_spectrum",
        grid=(n // cb, fp // fb),
        in_specs=[pl.BlockSpec((1, l, cb), lambda j, k: (0, 0, j)),
                  pl.BlockSpec((1, l, cb), lambda j, k: (1, 0, j)), tab, tab],
        out_specs=[out, out],
        out_shape=[jax.ShapeDtypeStruct((fp, n), F32)] * 2,
        compiler_params=_cparams(("parallel", "parallel"), 48),
    )(ad, ad, tabs["c_hi"], tabs["s_hi"])


def _longconv_kernel(u_ref, g_ref, kr_ref, ki_ref, bias_ref, wk_ref, c_ref, s_ref, ct_ref, st_ref,
                     o_ref, ub_ref, acc_ref):
    kb = pl.program_id(2)

    @pl.when(kb == 0)
    def _():
        ub_ref[...] = u_ref[0].astype(BF16)
        acc_ref[...] = jnp.zeros_like(acc_ref)

    ub = ub_ref[...]
    xr = _dot(c_ref[...], ub)
    xi = -_dot(s_ref[...], ub)
    kr = kr_ref[...]
    ki = ki_ref[...]
    wk = wk_ref[...]
    yr = ((xr * kr - xi * ki) * wk).astype(BF16)
    yi = ((xr * ki + xi * kr) * wk).astype(BF16)
    acc_ref[...] += _dot(ct_ref[...], yr) - _dot(st_ref[...], yi)

    @pl.when(kb == pl.num_programs(2) - 1)
    def _():
        u = u_ref[0]
        o_ref[0] = g_ref[0] * (acc_ref[...] + u * bias_ref[...])


def _gated_long_conv(u_arr, g_arr, kr, ki, k_col, bias, tabs):
    b, l, _ = u_arr.shape
    cb = 256
    ncb = HY_WIDTH // cb
    fp = tabs["c_hi"].shape[0]
    fb = tabs["fb"]
    tab = pl.BlockSpec((fb, l), lambda b, j, k: (k, 0))
    tabt = pl.BlockSpec((l, fb), lambda b, j, k: (0, k))
    spec = pl.BlockSpec((fb, cb), lambda b, j, k: (k, k_col * ncb + j))
    return pl.pallas_call(
        _longconv_kernel, name="dense_long_conv",
        grid=(b, ncb, fp // fb),
        in_specs=[
            pl.BlockSpec((1, l, cb), lambda b, j, k: (b, 0, j)),
            pl.BlockSpec((1, l, cb), lambda b, j, k: (b, 0, j)),
            spec, spec,
            pl.BlockSpec((1, cb), lambda b, j, k: (0, j)),
            pl.BlockSpec((fb, 1), lambda b, j, k: (k, 0)),
            tab, tab, tabt, tabt,
        ],
        out_specs=pl.BlockSpec((1, l, cb), lambda b, j, k: (b, 0, j)),
        out_shape=jax.ShapeDtypeStruct((b, l, HY_WIDTH), F32),
        scratch_shapes=[pltpu.VMEM((l, cb), BF16), pltpu.VMEM((l, cb), F32)],
        compiler_params=_cparams(("parallel", "parallel", "arbitrary"), 56),
    )(u_arr, g_arr, kr, ki, bias.reshape(1, HY_WIDTH), tabs["wk"],
      tabs["c_hi"], tabs["s_hi"], tabs["ct_hi"], tabs["st_hi"])


FFT_INNER = 64
FFT_COLS = HY_CHUNK
FFT_K2_CHUNK = 16


def _fft_tables(l):
    n = 2 * l
    n1 = FFT_INNER
    n2 = n // n1
    h = n2 // 2
    ang = lambda idx, mod: (idx % mod).astype(F32) * (2.0 * math.pi / mod)
    k2 = jnp.arange(n2, dtype=jnp.int32)
    m2 = jnp.arange(h, dtype=jnp.int32)
    a1 = ang(k2[:, None] * m2[None, :], n2)
    c1, s1 = jnp.cos(a1), jnp.sin(a1)
    f1 = jnp.block([[c1, s1], [-s1, c1]])
    k1 = jnp.arange(n1, dtype=jnp.int32)
    a2 = ang(k1[None, None, :] * (k1[None, :, None] * n2 + k2[:, None, None]), n)
    c2, s2 = jnp.cos(a2), jnp.sin(a2)
    m = jnp.concatenate([jnp.concatenate([c2, s2], axis=2), jnp.concatenate([-s2, c2], axis=2)], axis=1)
    g = jnp.block([[c1.T, -s1.T], [s1.T, c1.T]]) * (1.0 / n)
    out = dict(n2=n2)
    for name, t in (("f1", f1), ("f1r", f1[:, :h]), ("m", m), ("mt", jnp.swapaxes(m, 1, 2)), ("g", g)):
        out[name + "_hi"] = t.astype(BF16)
    return out


def _mm(w_hi, w_lo, x):
    if w_lo is None:
        return _dot(w_hi, x.astype(BF16))
    hi, lo = _split(x)
    return _dot3(w_hi, w_lo, hi, lo)


def _row_block_matmul(w_hi, w_lo, src_ref, dst_ref, count):
    m, k = w_hi.shape

    def body(i, carry):
        x = src_ref[pl.ds(pl.multiple_of(i * k, k), k), :]
        dst_ref[pl.ds(pl.multiple_of(i * m, m), m), :] = _mm(w_hi, w_lo, x).astype(dst_ref.dtype)
        return carry
    lax.fori_loop(0, count, body, 0, unroll=4)


def _short_conv_rows(z, w_ref, b_ref):
    l = z.shape[0]
    row = lax.broadcasted_iota(jnp.int32, z.shape, 0)
    prev = jnp.where(row == 0, 0.0, pltpu.roll(z, 1, axis=0))
    nxt = jnp.where(row == l - 1, 0.0, pltpu.roll(z, l - 1, axis=0))
    return w_ref[0:1, :] * prev + w_ref[1:2, :] * z + w_ref[2:3, :] * nxt + b_ref[...]


def _fftconv_kernel(u_ref, g_ref, uw_ref, ub_ref, gw_ref, gb_ref, kr_ref, ki_ref, bias_ref,
                    f1_ref, m_ref, mt_ref, gm_ref, o_ref, z_scr, y_scr, p_scr, q_scr, *, u_conv):
    kc = pl.program_id(2)
    n1 = FFT_INNER
    cc = z_scr.shape[1]
    n2 = z_scr.shape[0] // n1
    h = n2 // 2

    def load_u(s):
        return _short_conv_rows(u_ref[s, 0].astype(F32), uw_ref, ub_ref) if u_conv else u_ref[s, 0]

    @pl.when(kc == 0)
    def _stage1():
        z = [jnp.swapaxes(load_u(s).astype(BF16).reshape(h, n1, cc), 0, 1) for s in range(2)]
        z_scr[...] = jnp.concatenate(z, axis=1).reshape(n1 * n2, cc)
        _row_block_matmul(f1_ref[...], None, z_scr, p_scr, n1)
        q_scr[...] = jnp.swapaxes(p_scr[...].reshape(n1, 2 * n2, cc), 0, 1).reshape(2 * n2 * n1, cc)

    ks = kr_ref.shape[0]
    rows = [(pl.multiple_of((kc * ks + j) * n1, n1), pl.multiple_of((n2 + kc * ks + j) * n1, n1)) for j in range(ks)]
    xs = [_mm(m_ref[j], None, jnp.concatenate([q_scr[pl.ds(r0, n1), :], q_scr[pl.ds(i0, n1), :]], axis=0))
          for j, (r0, i0) in enumerate(rows)]
    ys = []
    for j, x in enumerate(xs):
        xr, xi = x[:n1], x[n1:]
        kr, ki = kr_ref[j], ki_ref[j]
        ys.append(jnp.concatenate([xr * kr - xi * ki, xr * ki + xi * kr], axis=0))
    for j, (r0, i0) in enumerate(rows):
        b = _mm(mt_ref[j], None, ys[j]).astype(p_scr.dtype)
        p_scr[pl.ds(r0, n1), :] = b[:n1]
        p_scr[pl.ds(i0, n1), :] = b[n1:]

    @pl.when(kc == pl.num_programs(2) - 1)
    def _stage1_inv():
        q_scr[...] = jnp.swapaxes(p_scr[...].reshape(2 * n2, n1, cc), 0, 1).reshape(2 * n2 * n1, cc)
        _row_block_matmul(gm_ref[...], None, q_scr, y_scr, n1)
        y3 = y_scr[...].reshape(n1, n2, cc)
        for s in range(2):
            y = jnp.swapaxes(y3[:, s * h:(s + 1) * h, :], 0, 1).reshape(h * n1, cc)
            gate = _short_conv_rows(g_ref[s, 0].astype(F32), gw_ref, gb_ref)
            o_ref[s, 0] = (gate * (y + load_u(s) * bias_ref[...])).astype(o_ref.dtype)


def _fft_gated_long_conv(u, u_col, z_hy, g_col, short_w, short_b, kr, ki, k_col, bias, ft, out_dtype):
    b, _, l, cc = z_hy.shape
    c = HY_WIDTH
    n1, n2, ks = FFT_INNER, ft["n2"], FFT_K2_CHUNK
    p, ncc = b // 2, c // cc
    u_conv = u is None
    if u_conv:
        u = z_hy
    tok = lambda col: pl.BlockSpec((2, 1, None, l, cc), lambda q, j, k: (0, q, col * ncc + j, 0, 0))
    taps = lambda col: pl.BlockSpec((3, cc), lambda q, j, k: (0, col * ncc + j))
    row = lambda col: pl.BlockSpec((1, cc), lambda q, j, k: (0, col * ncc + j))
    filt = pl.BlockSpec((ks, n1, cc), lambda q, j, k: (k, 0, k_col * ncc + j))
    tab = pl.BlockSpec((ks, 2 * n1, 2 * n1), lambda q, j, k: (k, 0, 0))
    res = lambda a: pl.BlockSpec(a.shape, lambda q, j, k: (0,) * a.ndim)
    sb = short_b.reshape(1, -1)
    y = pl.pallas_call(
        functools.partial(_fftconv_kernel, u_conv=u_conv), name="fft_long_conv",
        grid=(p, ncc, n2 // ks),
        in_specs=[tok(u_col), tok(g_col), taps(u_col), row(u_col), taps(g_col), row(g_col), filt, filt, row(0),
                  res(ft["f1_hi"]), tab, tab, res(ft["g_hi"])],
        out_specs=tok(0),
        out_shape=jax.ShapeDtypeStruct((2, p, ncc, l, cc), out_dtype),
        scratch_shapes=[pltpu.VMEM((n1 * n2, cc), BF16), pltpu.VMEM((n1 * n2, cc), F32),
                        pltpu.VMEM((2 * n2 * n1, cc), BF16), pltpu.VMEM((2 * n2 * n1, cc), BF16)],
        compiler_params=_cparams(("parallel", "parallel", "arbitrary"), 56),
    )(u.reshape((2, p) + u.shape[1:]), z_hy.reshape((2, p) + z_hy.shape[1:]), short_w, sb, short_w, sb, kr, ki,
      bias.reshape(1, c), ft["f1_hi"], ft["m_hi"], ft["mt_hi"], ft["g_hi"])
    return y.reshape(b, ncc, l, cc)


def _fftspec_kernel(a_ref, d_ref, f1_ref, m_ref, kr_ref, ki_ref, z_scr, p_scr, q_scr):
    n1 = FFT_INNER
    cc = z_scr.shape[1]
    h = z_scr.shape[0] // n1
    n2 = 2 * h
    for src_ref, out_ref, lo in ((a_ref, kr_ref, 0), (d_ref, ki_ref, n1)):
        z_scr[...] = jnp.swapaxes(src_ref[0].reshape(h, n1, cc), 0, 1).reshape(n1 * h, cc)
        _row_block_matmul(f1_ref[...], None, z_scr, p_scr, n1)
        q_scr[...] = jnp.swapaxes(p_scr[...].reshape(n1, 2 * n2, cc), 0, 1).reshape(2 * n2 * n1, cc)

        def body(k2, carry):
            r0 = pl.multiple_of(k2 * n1, n1)
            i0 = pl.multiple_of((n2 + k2) * n1, n1)
            slab = jnp.concatenate([q_scr[pl.ds(r0, n1), :], q_scr[pl.ds(i0, n1), :]], axis=0)
            out_ref[k2] = _mm(m_ref[k2][lo:lo + n1], None, slab)
            return carry
        lax.fori_loop(0, n2, body, 0, unroll=4)


def _fft_filter_spectrum(ad, ft):
    _, l, n = ad.shape
    n1, n2, cc = FFT_INNER, ft["n2"], FFT_COLS
    h = n2 // 2
    res = lambda a: pl.BlockSpec(a.shape, lambda j: (0,) * a.ndim)
    out = pl.BlockSpec((n2, n1, cc), lambda j: (0, 0, j))
    tabs = [ft["f1r_hi"], ft["m_hi"]]
    return pl.pallas_call(
        _fftspec_kernel, name="fft_filter_spectrum",
        grid=(n // cc,),
        in_specs=[pl.BlockSpec((1, l, cc), lambda j: (0, 0, j)), pl.BlockSpec((1, l, cc), lambda j: (1, 0, j))]
                 + [res(t) for t in tabs],
        out_specs=[out, out],
        out_shape=[jax.ShapeDtypeStruct((n2, n1, n), F32)] * 2,
        scratch_shapes=[pltpu.VMEM((n1 * h, cc), F32), pltpu.VMEM((2 * n2 * n1, cc), F32),
                        pltpu.VMEM((2 * n2 * n1, cc), F32)],
        compiler_params=_cparams(("parallel",), 56),
    )(ad, ad, *tabs)


def _softmax_weights(s, sink=None):
    m = jnp.max(s, axis=-1, keepdims=True)
    if sink is not None:
        m = jnp.maximum(m, sink)
    p = jnp.exp(s - m)
    den = jnp.sum(p, axis=-1, keepdims=True)
    if sink is not None:
        den = den + jnp.exp(sink - m)
    return p.astype(BF16), den


def _softmax_pv(s, v, sink=None):
    p, den = _softmax_weights(s, sink)
    return _dot(p, v) / den


def _qk(q, k):
    return lax.dot_general(q, k, (((1,), (1,)), ((), ())), preferred_element_type=F32)


def _half_mask(q_tile, half):
    lane = lax.broadcasted_iota(jnp.int32, q_tile.shape, 1)
    keep = (lane < HEAD_DIM) if half == 0 else (lane >= HEAD_DIM)
    return jnp.where(keep, q_tile, jnp.zeros_like(q_tile))


def _merge_halves(o_even, o_odd):
    lane = lax.broadcasted_iota(jnp.int32, o_even.shape, 1)
    return jnp.where(lane < HEAD_DIM, o_even, o_odd)


def _swa_kernel(sink_ref, q_ref, kp_ref, kc_ref, kn_ref, vp_ref, vc_ref, vn_ref, kx_ref, vx_ref, o_ref):
    n = pl.program_id(1)
    nb = pl.num_programs(1)
    blk = SWA_BLOCK
    g = SWA_HEADS // SWA_KV_HEADS
    n_ctx = kx_ref.shape[1]
    rows = g * blk
    ri = lax.broadcasted_iota(jnp.int32, (rows, 3 * blk + n_ctx), 0) & (blk - 1)
    ci = lax.broadcasted_iota(jnp.int32, (rows, 3 * blk + n_ctx), 1)
    ok_prev = (ci < blk) & (ci >= ri) & (n > 0)
    ok_cur = (ci >= blk) & (ci < 2 * blk)
    ok_next = (ci >= 2 * blk) & (ci < 3 * blk) & (ci - 2 * blk <= ri) & (n < nb - 1)
    ok = ok_prev | ok_cur | ok_next | (ci >= 3 * blk)
    bias = jnp.where(ok, 0.0, NEG_BIG)
    rsel = lax.broadcasted_iota(jnp.int32, (rows, 1), 0) >> (blk.bit_length() - 1)
    outs = [None] * SWA_HEADS
    kvs = range(SWA_KV_HEADS)
    lanes = [slice(kv * LANES, (kv + 1) * LANES) for kv in kvs]
    scores, sinks = [], []
    for kv in kvs:
        cs = lanes[kv]
        k = jnp.concatenate([kp_ref[0, :, cs], kc_ref[0, :, cs], kn_ref[0, :, cs], kx_ref[0, :, cs]], axis=0)
        q_parts = []
        sink = jnp.zeros((rows, 1), F32)
        for gi in range(g):
            h = kv * g + gi
            q_parts.append(_half_mask(q_ref[0, :, (h // 2) * LANES:(h // 2 + 1) * LANES], h % 2))
            sink = jnp.where(rsel == gi, sink_ref[h], sink)
        scores.append(_qk(jnp.concatenate(q_parts, axis=0), k))
        sinks.append(sink)
    probs = [_softmax_weights(scores[kv] + bias, sinks[kv]) for kv in kvs]
    for kv in kvs:
        cs = lanes[kv]
        v = jnp.concatenate([vp_ref[0, :, cs], vc_ref[0, :, cs], vn_ref[0, :, cs], vx_ref[0, :, cs]], axis=0)
        pw, den = probs[kv]
        o = _dot(pw, v) / den
        for gi in range(g):
            outs[kv * g + gi] = o[gi * blk:(gi + 1) * blk]
    for p in range(SWA_HEADS // 2):
        o_ref[0, :, p * LANES:(p + 1) * LANES] = _merge_halves(outs[2 * p], outs[2 * p + 1]).astype(o_ref.dtype)


def _window_gqa(q, k, v, kx, vx, sink):
    b, l, _ = q.shape
    nb = l // SWA_BLOCK
    n_ctx = kx.shape[1]
    kvw = k.shape[2]

    def blk(off):
        return pl.BlockSpec((1, SWA_BLOCK, kvw), lambda b, n: (b, jnp.clip(n + off, 0, nb - 1), 0))

    ctx = pl.BlockSpec((1, n_ctx, kvw), lambda b, n: (b, 0, 0))
    return pl.pallas_call(
        _swa_kernel, name="window_gqa",
        grid=(b, nb),
        in_specs=[
            pl.BlockSpec(memory_space=pltpu.SMEM),
            pl.BlockSpec((1, SWA_BLOCK, q.shape[2]), lambda b, n: (b, n, 0)),
            blk(-1), blk(0), blk(1), blk(-1), blk(0), blk(1), ctx, ctx,
        ],
        out_specs=pl.BlockSpec((1, SWA_BLOCK, q.shape[2]), lambda b, n: (b, n, 0)),
        out_shape=jax.ShapeDtypeStruct(q.shape, BF16),
        compiler_params=_cparams(("parallel", "parallel"), 32),
    )(sink, q, k, k, k, v, v, v, kx, vx)


def _na_bias_kernel(rpb_ref, o_ref):
    off = pl.program_id(0)
    h = pl.program_id(1)
    nd = 2 * NA_COLS - 1
    q = lax.broadcasted_iota(jnp.int32, (GRID_W, LANES), 0)
    lane = lax.broadcasted_iota(jnp.int32, (GRID_W, LANES), 1)
    kc = lane & (GRID_W - 1)
    upper = lane >= GRID_W
    cstart = jnp.clip(q - NA_COLS // 2, 0, GRID_W - NA_COLS)
    valid = (kc >= cstart) & (kc < cstart + NA_COLS)
    dc = jnp.clip(kc - q, -(NA_COLS - 1), NA_COLS - 1) + (NA_COLS - 1)
    for j in range(NA_MAX_ROWS // 2):
        base0 = (h * (2 * NA_MAX_ROWS - 1) + (2 * j - off + NA_MAX_ROWS - 1)) * nd
        base1 = base0 + nd
        t = jnp.zeros((GRID_W, LANES), F32)
        for d in range(nd):
            val = jnp.where(upper, rpb_ref[base1 + d], rpb_ref[base0 + d])
            t = jnp.where(dc == d, val, t)
        o_ref[0, 0, :, j * LANES:(j + 1) * LANES] = jnp.where(valid, t, NEG_BIG)


def _na_bias_table(rpb):
    return pl.pallas_call(
        _na_bias_kernel, name="na_bias_table",
        grid=(NA_MAX_ROWS, NA_HEADS),
        in_specs=[pl.BlockSpec(memory_space=pltpu.SMEM)],
        out_specs=pl.BlockSpec((1, 1, GRID_W, NA_MAX_ROWS * GRID_W), lambda o, h: (o, h, 0, 0)),
        out_shape=jax.ShapeDtypeStruct((NA_MAX_ROWS, NA_HEADS, GRID_W, NA_MAX_ROWS * GRID_W), F32),
        compiler_params=_cparams(("arbitrary", "arbitrary"), 16),
    )(rpb.reshape(-1))


NA_ROWS_PER_STEP = 4


def _na_kernel(q_ref, k_ref, v_ref, kx_ref, vx_ref, bias_ref, o_ref, *, rows):
    wr = NA_MAX_ROWS
    nloc = wr * GRID_W
    for i in range(NA_ROWS_PER_STEP):
        r = pl.program_id(1) * NA_ROWS_PER_STEP + i
        first = jnp.clip(r - wr // 2, 0, rows - wr)
        start = pl.multiple_of(first * GRID_W, GRID_W)
        off = r - first
        qr = slice(i * GRID_W, (i + 1) * GRID_W)
        pairs = range(NA_HEADS // 2)
        lanes = [slice(p * LANES, (p + 1) * LANES) for p in pairs]
        scores = []
        for p in pairs:
            q_tile = q_ref[0, qr, lanes[p]]
            q2 = jnp.concatenate([_half_mask(q_tile, 0), _half_mask(q_tile, 1)], axis=0)
            k = jnp.concatenate([k_ref[0, pl.ds(start, nloc), lanes[p]], kx_ref[0, :, lanes[p]]], axis=0)
            scores.append(_qk(q2, k))
        probs = []
        for p in pairs:
            s = scores[p]
            bias = jnp.concatenate([bias_ref[off, 2 * p], bias_ref[off, 2 * p + 1]], axis=0)
            probs.append(_softmax_weights(jnp.concatenate([s[:, :nloc] + bias, s[:, nloc:]], axis=1)))
        for p in pairs:
            pw, den = probs[p]
            v = jnp.concatenate([v_ref[0, pl.ds(start, nloc), lanes[p]], vx_ref[0, :, lanes[p]]], axis=0)
            o = _dot(pw, v) / den
            o_ref[0, qr, lanes[p]] = _merge_halves(o[:GRID_W], o[GRID_W:]).astype(o_ref.dtype)


def _neighbourhood_attention(q, k, v, kx, vx, bias_tab):
    b, l, w = q.shape
    rows = l // GRID_W
    n_ctx = kx.shape[1]
    tq = NA_ROWS_PER_STEP * GRID_W
    full = pl.BlockSpec((1, l, w), lambda b, r: (b, 0, 0))
    ctx = pl.BlockSpec((1, n_ctx, w), lambda b, r: (b, 0, 0))
    return pl.pallas_call(
        functools.partial(_na_kernel, rows=rows), name="neighbourhood_attention",
        grid=(b, rows // NA_ROWS_PER_STEP),
        in_specs=[pl.BlockSpec((1, tq, w), lambda b, r: (b, r, 0)), full, full, ctx, ctx,
                  _resident(bias_tab.shape)],
        out_specs=pl.BlockSpec((1, tq, w), lambda b, r: (b, r, 0)),
        out_shape=jax.ShapeDtypeStruct(q.shape, BF16),
        compiler_params=_cparams(("parallel", "arbitrary"), 48),
    )(q, k, v, kx, vx, bias_tab)


def _ctx_attn_kernel(sink_ref, qs_ref, ks_ref, vs_ref, qn_ref, kn_ref, vn_ref, os_ref, on_ref):
    n = qs_ref.shape[1]
    g = SWA_HEADS // SWA_KV_HEADS
    rsel = lax.broadcasted_iota(jnp.int32, (g * n, 1), 0) >> (n.bit_length() - 1)
    outs = [None] * SWA_HEADS
    for kv in range(SWA_KV_HEADS):
        cs = slice(kv * LANES, (kv + 1) * LANES)
        q_parts = []
        sink = jnp.zeros((g * n, 1), F32)
        for gi in range(g):
            h = kv * g + gi
            q_parts.append(_half_mask(qs_ref[0, :, (h // 2) * LANES:(h // 2 + 1) * LANES], h % 2))
            sink = jnp.where(rsel == gi, sink_ref[h], sink)
        o = _softmax_pv(_qk(jnp.concatenate(q_parts, axis=0), ks_ref[0, :, cs]), vs_ref[0, :, cs], sink)
        for gi in range(g):
            outs[kv * g + gi] = o[gi * n:(gi + 1) * n]
    for p in range(SWA_HEADS // 2):
        os_ref[0, :, p * LANES:(p + 1) * LANES] = _merge_halves(outs[2 * p], outs[2 * p + 1]).astype(os_ref.dtype)
    for p in range(NA_HEADS // 2):
        cs = slice(p * LANES, (p + 1) * LANES)
        q_tile = qn_ref[0, :, cs]
        q2 = jnp.concatenate([_half_mask(q_tile, 0), _half_mask(q_tile, 1)], axis=0)
        o = _softmax_pv(_qk(q2, kn_ref[0, :, cs]), vn_ref[0, :, cs])
        on_ref[0, :, cs] = _merge_halves(o[:n], o[n:]).astype(on_ref.dtype)


def _context_attention(qs, ks, vs, qn, kn, vn, sink):
    b, n, w = qs.shape
    spec = lambda a: pl.BlockSpec((1,) + a.shape[1:], lambda b: (b, 0, 0))
    return pl.pallas_call(
        _ctx_attn_kernel, name="context_attention",
        grid=(b,),
        in_specs=[pl.BlockSpec(memory_space=pltpu.SMEM)] + [spec(a) for a in (qs, ks, vs, qn, kn, vn)],
        out_specs=[spec(qs), spec(qn)],
        out_shape=[jax.ShapeDtypeStruct(qs.shape, BF16), jax.ShapeDtypeStruct(qn.shape, BF16)],
        compiler_params=_cparams(("parallel",), 32),
    )(sink, qs, ks, vs, qn, kn, vn)


def _merge_ffn_kernel(x_ref, mg_ref, yh_ref, ys_ref, yn_ref, zg_ref, wb_ref, wo_ref,
                      sh_ref, sc_ref, gt_ref, g_ref, wg_ref, wu_ref, wd_ref, o_ref, *, fc):
    d = D_MODEL
    yh = jnp.concatenate([yh_ref[0, c] for c in range(yh_ref.shape[1])], axis=1)
    ys = (yh.astype(BF16), ys_ref[0], yn_ref[0])
    m = None
    for n in range(3):
        proj = _dot(ys[n], wb_ref[n])
        term = zg_ref[0, :, n * d:(n + 1) * d].astype(F32) * proj
        m = term if m is None else m + term
    x = x_ref[0] + mg_ref[0] * _dot(m.astype(BF16), wo_ref[...])
    o_ref[0] = _swiglu_residual(x, sh_ref, sc_ref, gt_ref, g_ref, wg_ref, wu_ref, wd_ref, fc)


def _merge_and_ffn(x, mix_gate, y_hy, y_swa, y_na, gates, w_branch, w_out, layer,
                   shift, scale, gate, gain, wg, wu, wd):
    b, l, d = x.shape
    tm = min(512, l)
    tok = lambda n: pl.BlockSpec((1, tm, n), lambda b, t: (b, t, 0))
    idx = (layer, 1)
    return pl.pallas_call(
        functools.partial(_merge_ffn_kernel, fc=FFN_CHUNK), name="merge_ffn",
        grid=(b, l // tm),
        in_specs=[tok(d), _mod_spec(mix_gate),
                  pl.BlockSpec((1, HY_WIDTH // HY_CHUNK, tm, HY_CHUNK), lambda b, t: (b, 0, t, 0)),
                  tok(HY_WIDTH), tok(HY_WIDTH), tok(3 * d),
                  _layer_slab(w_branch, (layer,)), _layer_slab(w_out, (layer,)),
                  _mod_spec(shift), _mod_spec(scale), _mod_spec(gate), _resident((1, d)),
                  _layer_slab(wg, idx), _layer_slab(wu, idx), _layer_slab(wd, idx)],
        out_specs=tok(d),
        out_shape=jax.ShapeDtypeStruct(x.shape, F32),
        compiler_params=_cparams(("parallel", "parallel"), 58),
    )(x, mix_gate[0], y_hy, y_swa, y_na, gates, w_branch, w_out, shift[0], scale[0], gate[0], gain.reshape(1, d),
      wg, wu, wd)


def _dft_tables(l):
    n = 2 * l
    fb = 256 if l >= 1024 else LANES * ((l + 1 + LANES - 1) // LANES)
    fp = fb * ((l + 1 + fb - 1) // fb)
    k = jnp.arange(fp, dtype=jnp.int32)[:, None]
    t = jnp.arange(l, dtype=jnp.int32)[None, :]
    ang = ((k * t) % n).astype(F32) * (2.0 * math.pi / n)
    live = k <= l
    c = jnp.where(live, jnp.cos(ang), 0.0)
    s = jnp.where(live, jnp.sin(ang), 0.0)
    c_hi = c.astype(BF16)
    s_hi = s.astype(BF16)
    wk = jnp.where((k == 0) | (k == l), 1.0, 2.0) * jnp.where(live, 1.0 / n, 0.0)
    return dict(fb=fb, c_hi=c_hi, s_hi=s_hi, ct_hi=c_hi.T, st_hi=s_hi.T, wk=wk.astype(F32))


def _filter_features(l):
    t = jnp.linspace(0.0, 1.0, l, dtype=F32)[:, None]
    w = (2.0 * math.pi / l) * jnp.arange(l, dtype=F32)[:, None]
    f = jnp.linspace(1e-4, HY_BANDS - 1, HY_BANDS, dtype=F32)[None, :]
    z = jnp.concatenate([t, jnp.cos(f * w), -jnp.sin(f * w)], axis=-1)
    return jnp.pad(z, ((0, 0), (0, LANES - HY_EMB)))


def _decay_rates():
    max_decay = math.log(HY_TARGET) / HY_FAST_DECAY
    min_decay = math.log(HY_TARGET) / HY_SLOW_DECAY
    return jnp.abs(jnp.linspace(min_decay, max_decay, HY_WIDTH, dtype=F32))[None, :]


def _rope_tables(l):
    pos = jnp.arange(l)
    row = (pos // GRID_W).astype(F32)
    col = (pos % GRID_W).astype(F32)
    half = HEAD_DIM // 2
    inv = 1.0 / (ROPE_BASE ** (jnp.arange(0, half, 2, dtype=F32) / half))
    ar = row[:, None] * inv[None, :]
    ac = col[:, None] * inv[None, :]
    cos = jnp.concatenate([jnp.cos(ar), jnp.cos(ar), jnp.cos(ac), jnp.cos(ac)], axis=-1)
    sin = jnp.concatenate([-jnp.sin(ar), jnp.sin(ar), -jnp.sin(ac), jnp.sin(ac)], axis=-1)
    return jnp.tile(cos, (1, 2)), jnp.tile(sin, (1, 2))


def _pad_to(a, shape):
    return jnp.pad(a, [(0, s - d) for d, s in zip(a.shape, shape)])


def _use_fft(bsz, length):
    return bsz % 2 == 0 and length % (8 * FFT_INNER) == 0


def _conv_tables(bsz, length):
    return _fft_tables(length) if _use_fft(bsz, length) else _dft_tables(length)


def _hyena_branch(z_hy, short_w, short_b, feat, fparams, deltas, bias, tabs):
    ad = _hyena_filter_sums(feat, fparams, deltas)
    if "n2" in tabs:
        kr, ki = _fft_filter_spectrum(ad, tabs)
        y1 = _fft_gated_long_conv(None, 0, z_hy, 1, short_w, short_b, kr, ki, 0, bias[0], tabs, F32)
        return _fft_gated_long_conv(y1, 0, z_hy, 2, short_w, short_b, kr, ki, 1, bias[1], tabs, BF16)
    v, x1, x2 = _short_conv3(z_hy, short_w, short_b)
    kr, ki = _filter_spectrum(ad, tabs)
    conv = functools.partial(_gated_long_conv, tabs=tabs)
    y1 = conv(v, x1, kr, ki, 0, bias[0])
    y = conv(y1, x2, kr, ki, 1, bias[1])
    bsz, length, _ = y.shape
    return y.reshape(bsz, length, HY_WIDTH // HY_CHUNK, HY_CHUNK).transpose(0, 2, 1, 3)


def kernel(x, c, ctx, c_ctx, w_ada, b_ada, norm_g, ffn_w_gate, ffn_w_up, ffn_w_down,
           w_in, hy_short_w, hy_short_b, hy_pe_w0, hy_pe_b0, hy_pe_w1, hy_pe_b1,
           hy_pe_w2, hy_pe_b2, hy_pe_wout, hy_sin_freq, hy_bias,
           swa_q_gain, swa_k_gain, swa_sink, na_q_gain, na_k_gain, na_rpb,
           w_branch, w_out):
    bsz, seq, d = x.shape
    n_ctx = ctx.shape[1]
    depth = w_ada.shape[0]

    c16 = _pad_to(jnp.concatenate([c, c_ctx[None, :]], axis=0), (MOD_ROWS, d))
    mods = _adaln_mods(c16, w_ada, b_ada).reshape(depth, MOD_ROWS, N_MOD, 1, d)

    tabs_x = _conv_tables(bsz, seq)
    tabs_c = _conv_tables(bsz, n_ctx)
    feat_x = _filter_features(seq)
    feat_c = _filter_features(n_ctx)
    deltas = _decay_rates()
    cos_x, sin_x = _rope_tables(seq)
    cos_c = jnp.ones((n_ctx, LANES), F32)
    sin_c = jnp.zeros((n_ctx, LANES), F32)
    eye = jnp.arange(MXU_DIM) // HEAD_DIM
    bd = (eye[:, None] == eye[None, :]).astype(BF16)

    wg = ffn_w_gate.astype(BF16)
    wu = ffn_w_up.astype(BF16)
    wd = ffn_w_down.astype(BF16)
    wb = w_branch.astype(BF16)
    wo = w_out.astype(BF16)
    win = w_in.astype(BF16)
    kv = win[:, :, _OFF_SWA + _SWA_Q:_OFF_NA].reshape(depth, d, 2, SWA_KV_HEADS, 1, HEAD_DIM)
    win_kv = jnp.broadcast_to(kv, (depth, d, 2, SWA_KV_HEADS, 2, HEAD_DIM)).reshape(depth, d, 4 * _SWA_KV)

    xc = ctx
    for i in range(depth):
        last = i == depth - 1
        mx = lambda j: (mods, i, j, None)
        mc = lambda j: (mods, i, j, bsz)
        tile2 = lambda g: jnp.tile(g, 2 * MXU_DIM // LANES)
        head_gains = jnp.stack([tile2(swa_q_gain[i]), tile2(swa_k_gain[i]), tile2(na_q_gain[i]), tile2(na_k_gain[i])])
        fparams = (_pad_to(hy_pe_w0[i], (LANES, LANES)), _pad_to(hy_pe_b0[i][None], (1, LANES)),
                   _pad_to(hy_pe_w1[i], (LANES, LANES)), _pad_to(hy_pe_b1[i][None], (1, LANES)),
                   _pad_to(hy_pe_w2[i], (LANES, LANES)), _pad_to(hy_pe_b2[i][None], (1, LANES)),
                   _pad_to(hy_pe_wout[i], (LANES, 4 * HY_WIDTH)), _pad_to(hy_sin_freq[i][None], (1, LANES)))

        x = _ffn_half_step(x, mx(0), mx(1), mx(2), norm_g[i, 0], wg, wu, wd, (i, 0))
        xc = _ffn_half_step(xc, mc(0), mc(1), mc(2), norm_g[i, 0], wg, wu, wd, (i, 0))

        z_hy, q_s, k_s, v_s, q_n, k_n, v_n, gates = _in_projection(
            x, mx(3), mx(4), norm_g[i, 1], cos_x, sin_x, head_gains, bd, win, win_kv, i)
        zc_hy, qc_s, kc_s, vc_s, qc_n, kc_n, vc_n, gates_c = _in_projection(
            xc, mc(3), mc(4), norm_g[i, 1], cos_c, sin_c, head_gains, bd, win, win_kv, i)

        y_hy = _hyena_branch(z_hy, hy_short_w[i], hy_short_b[i], feat_x, fparams, deltas, hy_bias[i], tabs_x)
        y_swa = _window_gqa(q_s, k_s, v_s, kc_s, vc_s, swa_sink[i])
        y_na = _neighbourhood_attention(q_n, k_n, v_n, kc_n, vc_n, _na_bias_table(na_rpb[i]))
        x = _merge_and_ffn(x, mx(5), y_hy, y_swa, y_na, gates, wb, wo, i,
                           mx(6), mx(7), mx(8), norm_g[i, 2], wg, wu, wd)

        if not last:
            yc_hy = _hyena_branch(zc_hy, hy_short_w[i], hy_short_b[i], feat_c, fparams, deltas, hy_bias[i], tabs_c)
            yc_swa, yc_na = _context_attention(qc_s, kc_s, vc_s, qc_n, kc_n, vc_n, swa_sink[i])
            xc = _merge_and_ffn(xc, mc(5), yc_hy, yc_swa, yc_na, gates_c, wb, wo, i,
                                mc(6), mc(7), mc(8), norm_g[i, 2], wg, wu, wd)
    return x
```

```python
import functools
import math

import jax
import jax.numpy as jnp
from jax import lax
from jax.experimental import pallas as pl
from jax.experimental.pallas import tpu as pltpu

F32 = jnp.float32
BF16 = jnp.bfloat16

D_MODEL = 1024
DEPTH = 4
GRID_W = 64
HEAD_DIM = 64
N_MOD = 9
RMS_EPS = 1e-6

HY_WIDTH = D_MODEL // 2
HY_CHUNK = 256
HY_EMB = 33
HY_BANDS = (HY_EMB - 1) // 2
HY_FFN = 64
HY_FAST_DECAY = 0.3
HY_SLOW_DECAY = 1.5
HY_TARGET = 1e-2

SWA_HEADS = 8
SWA_KV_HEADS = 2
SWA_WINDOW = 128
SWA_BLOCK = 128
ROPE_BASE = 10000.0

NA_HEADS = 8
NA_MAX_ROWS = 8
NA_COLS = 16
NA_COL_BLOCK = 16

FFN_HIDDEN = 256 * ((8 * D_MODEL // 3 + 255) // 256)
FFN_CHUNK = 1536

LANES = 128
MXU_DIM = 256
MOD_ROWS = 16
NEG_BIG = -1e30


def _cparams(sem, vmem_mb):
    return pltpu.CompilerParams(dimension_semantics=sem, vmem_limit_bytes=vmem_mb * 1024 * 1024)


def _resident(shape):
    nd = len(shape)
    return pl.BlockSpec(shape, lambda *_: (0,) * nd, pipeline_mode=pl.Buffered(1))


def _layer_slab(arr, idx):
    rest = arr.shape[len(idx):]
    return pl.BlockSpec((None,) * len(idx) + rest, lambda *_: tuple(idx) + (0,) * len(rest),
                        pipeline_mode=pl.Buffered(1))


def _split(x):
    hi = x.astype(BF16)
    lo = (x - hi.astype(F32)).astype(BF16)
    return hi, lo


def _dot(a, b):
    return jnp.dot(a, b, preferred_element_type=F32)


def _dot3(a_hi, a_lo, b_hi, b_lo):
    return _dot(a_hi, b_hi) + _dot(a_lo, b_hi) + _dot(a_hi, b_lo)


def _dot3f(a, b):
    a_hi, a_lo = _split(a)
    b_hi, b_lo = _split(b)
    return _dot3(a_hi, a_lo, b_hi, b_lo)


def _rms_mod(x, gain, shift, scale):
    ms = jnp.mean(x * x, axis=-1, keepdims=True)
    y = x * lax.rsqrt(ms + RMS_EPS) * gain
    return y * (1.0 + scale) + shift


def _head_sumsq(x, bd):
    return [_dot(jnp.square(x[:, j * MXU_DIM:(j + 1) * MXU_DIM]).astype(BF16), bd) for j in range(x.shape[1] // MXU_DIM)]


def _head_normalize(x, sumsq, gain):
    outs = [x[:, j * MXU_DIM:(j + 1) * MXU_DIM] * lax.rsqrt(s * (1.0 / HEAD_DIM) + RMS_EPS) * gain
            for j, s in enumerate(sumsq)]
    return outs[0] if len(outs) == 1 else jnp.concatenate(outs, axis=1)


def _rope(x, cos, sin):
    lane = lax.broadcasted_iota(jnp.int32, (x.shape[0], LANES), 1)
    first = (lane & 31) < 16
    outs = []
    for j in range(x.shape[1] // LANES):
        xc = x[:, j * LANES:(j + 1) * LANES]
        partner = jnp.where(first, pltpu.roll(xc, LANES - 16, axis=1), pltpu.roll(xc, 16, axis=1))
        outs.append(xc * cos + partner * sin)
    return outs[0] if len(outs) == 1 else jnp.concatenate(outs, axis=1)


def _mods_kernel(c_ref, w_ref, b_ref, o_ref):
    c = c_ref[...]
    a = c * jax.nn.sigmoid(c)
    o_ref[0] = _dot3f(a, w_ref[0]) + b_ref[0]


def _adaln_mods(c16, w_ada, b_ada):
    depth, d, n = w_ada.shape
    nb = 1152
    return pl.pallas_call(
        _mods_kernel, name="adaln_mods",
        grid=(depth, n // nb),
        in_specs=[
            pl.BlockSpec((MOD_ROWS, d), lambda i, j: (0, 0)),
            pl.BlockSpec((1, d, nb), lambda i, j: (i, 0, j)),
            pl.BlockSpec((1, 1, nb), lambda i, j: (i, 0, j)),
        ],
        out_specs=pl.BlockSpec((1, MOD_ROWS, nb), lambda i, j: (i, 0, j)),
        out_shape=jax.ShapeDtypeStruct((depth, MOD_ROWS, n), F32),
        compiler_params=_cparams(("arbitrary", "arbitrary"), 48),
    )(c16, w_ada, b_ada.reshape(depth, 1, n))


def _swiglu_residual(x, sh_ref, sc_ref, gt_ref, g_ref, wg_ref, wu_ref, wd_ref, fc):
    hb = _rms_mod(x, g_ref[...], sh_ref[0], sc_ref[0]).astype(BF16)
    acc = None
    for f0 in range(0, FFN_HIDDEN, fc):
        f1 = min(f0 + fc, FFN_HIDDEN)
        g = _dot(hb, wg_ref[:, f0:f1])
        u = _dot(hb, wu_ref[:, f0:f1])
        a = (g * jax.nn.sigmoid(g) * u).astype(BF16)
        d = _dot(a, wd_ref[f0:f1, :])
        acc = d if acc is None else acc + d
    return x + 0.5 * gt_ref[0] * acc


def _ffn_kernel(x_ref, sh_ref, sc_ref, gt_ref, g_ref, wg_ref, wu_ref, wd_ref, o_ref, *, fc):
    o_ref[0] = _swiglu_residual(x_ref[0], sh_ref, sc_ref, gt_ref, g_ref, wg_ref, wu_ref, wd_ref, fc)


def _mod_spec(sel):
    _, layer, j, row = sel
    if row is None:
        return pl.BlockSpec((None, 1, None, 1, D_MODEL), lambda b, t: (layer, b, j, 0, 0))
    return pl.BlockSpec((None, 1, None, 1, D_MODEL), lambda b, t: (layer, row, j, 0, 0))


def _ffn_half_step(x, shift, scale, gate, gain, wg, wu, wd, idx):
    b, l, d = x.shape
    tm = min(512, l)
    return pl.pallas_call(
        functools.partial(_ffn_kernel, fc=FFN_CHUNK), name="ffn_half_step",
        grid=(b, l // tm),
        in_specs=[
            pl.BlockSpec((1, tm, d), lambda b, t: (b, t, 0)),
            _mod_spec(shift), _mod_spec(scale), _mod_spec(gate),
            _resident((1, d)),
            _layer_slab(wg, idx), _layer_slab(wu, idx), _layer_slab(wd, idx),
        ],
        out_specs=pl.BlockSpec((1, tm, d), lambda b, t: (b, t, 0)),
        out_shape=jax.ShapeDtypeStruct(x.shape, F32),
        compiler_params=_cparams(("parallel", "parallel"), 56),
    )(x, shift[0], scale[0], gate[0], gain.reshape(1, d), wg, wu, wd)


_HY_COLS = 3 * HY_WIDTH
_SWA_Q = SWA_HEADS * HEAD_DIM
_SWA_KV = SWA_KV_HEADS * HEAD_DIM
_NA_W = NA_HEADS * HEAD_DIM
_OFF_SWA = _HY_COLS
_OFF_NA = _OFF_SWA + _SWA_Q + 2 * _SWA_KV
_OFF_GATE = _OFF_NA + 3 * _NA_W


def _inproj_kernel(x_ref, sh_ref, sc_ref, g_ref, cos_ref, sin_ref, hg_ref, bd_ref, w_ref, wkv_ref,
                   zhy_ref, qs_ref, ks_ref, vs_ref, qn_ref, kn_ref, vn_ref, gt_ref):
    hb = _rms_mod(x_ref[0], g_ref[...], sh_ref[0], sc_ref[0]).astype(BF16)
    cos = cos_ref[...]
    sin = sin_ref[...]
    bd = bd_ref[...]
    scale = HEAD_DIM ** -0.5
    proj = lambda lo, n: _dot(hb, w_ref[:, lo:lo + n])
    kvw = 2 * _SWA_KV
    raw = [proj(_OFF_SWA, _SWA_Q), _dot(hb, wkv_ref[:, :kvw]), proj(_OFF_NA, _NA_W), proj(_OFF_NA + _NA_W, _NA_W)]
    gates = proj(_OFF_GATE, 3 * D_MODEL)
    sums = [_head_sumsq(r, bd) for r in raw]
    vs_ref[0] = _dot(hb, wkv_ref[:, kvw:]).astype(BF16)
    vn_ref[0] = proj(_OFF_NA + 2 * _NA_W, _NA_W).astype(BF16)
    zhy = proj(0, _HY_COLS).astype(BF16)
    for c in range(_HY_COLS // HY_CHUNK):
        zhy_ref[0, c] = zhy[:, c * HY_CHUNK:(c + 1) * HY_CHUNK]
    gt_ref[0] = jax.nn.sigmoid(gates).astype(BF16)
    qs, ks, qn, kn = [_head_normalize(r, s, hg_ref[i:i + 1, :]) for i, (r, s) in enumerate(zip(raw, sums))]
    qs_ref[0] = (_rope(qs, cos, sin) * scale).astype(BF16)
    ks_ref[0] = _rope(ks, cos, sin).astype(BF16)
    qn_ref[0] = (qn * scale).astype(BF16)
    kn_ref[0] = kn.astype(BF16)


def _in_projection(x, shift, scale, gain, cos, sin, head_gains, bd, w, w_kv_dup, layer):
    b, l, d = x.shape
    tm = min(512, l)
    kvw = 2 * _SWA_KV
    widths = [_HY_COLS, _SWA_Q, kvw, kvw, _NA_W, _NA_W, _NA_W, 3 * d]
    dtypes = [BF16] * 8
    return pl.pallas_call(
        _inproj_kernel, name="in_projection",
        grid=(b, l // tm),
        in_specs=[
            pl.BlockSpec((1, tm, d), lambda b, t: (b, t, 0)),
            _mod_spec(shift), _mod_spec(scale),
            _resident((1, d)),
            pl.BlockSpec((tm, LANES), lambda b, t: (t, 0)),
            pl.BlockSpec((tm, LANES), lambda b, t: (t, 0)),
            _resident(head_gains.shape), _resident(bd.shape), _layer_slab(w, (layer,)), _layer_slab(w_kv_dup, (layer,)),
        ],
        out_specs=[pl.BlockSpec((1, _HY_COLS // HY_CHUNK, tm, HY_CHUNK), lambda b, t: (b, 0, t, 0))]
                  + [pl.BlockSpec((1, tm, n), lambda b, t: (b, t, 0)) for n in widths[1:]],
        out_shape=[jax.ShapeDtypeStruct((b, _HY_COLS // HY_CHUNK, l, HY_CHUNK), BF16)]
                  + [jax.ShapeDtypeStruct((b, l, n), dt) for n, dt in zip(widths[1:], dtypes[1:])],
        compiler_params=_cparams(("parallel", "parallel"), 56),
    )(x, shift[0], scale[0], gain.reshape(1, d), cos, sin, head_gains, bd, w, w_kv_dup)


def _shortconv_kernel(z0_ref, z1_ref, z2_ref, w_ref, b_ref, o0_ref, o1_ref, o2_ref):
    l = z0_ref.shape[1]
    row = lax.broadcasted_iota(jnp.int32, z0_ref.shape[1:], 0)
    for g, (z_ref, o_ref) in enumerate(((z0_ref, o0_ref), (z1_ref, o1_ref), (z2_ref, o2_ref))):
        z = z_ref[0].astype(F32)
        prev = jnp.where(row == 0, 0.0, pltpu.roll(z, 1, axis=0))
        nxt = jnp.where(row == l - 1, 0.0, pltpu.roll(z, l - 1, axis=0))
        o_ref[0] = w_ref[g, 0:1, :] * prev + w_ref[g, 1:2, :] * z + w_ref[g, 2:3, :] * nxt + b_ref[g]


def _short_conv3(z, w, bias):
    b, _, l, cb = z.shape
    ncb = HY_WIDTH // cb
    wg = w.reshape(3, 3, HY_WIDTH).transpose(1, 0, 2)
    zin = lambda g: pl.BlockSpec((None, 1, l, cb), lambda b, j: (b, g * ncb + j, 0, 0))
    out = pl.BlockSpec((1, l, cb), lambda b, j: (b, 0, j))
    return pl.pallas_call(
        _shortconv_kernel, name="short_conv3",
        grid=(b, ncb),
        in_specs=[zin(0), zin(1), zin(2),
                  pl.BlockSpec((3, 3, cb), lambda b, j: (0, 0, j)),
                  pl.BlockSpec((3, 1, cb), lambda b, j: (0, 0, j))],
        out_specs=[out, out, out],
        out_shape=[jax.ShapeDtypeStruct((b, l, HY_WIDTH), F32)] * 3,
        compiler_params=_cparams(("parallel", "parallel"), 48),
    )(z, z, z, wg, bias.reshape(3, 1, HY_WIDTH))


def _filter_kernel(z_ref, w0, b0, w1, b1, w2, b2, wo, fr_ref, dl_ref, o_ref):
    z = z_ref[...]
    fr = fr_ref[...]
    a = jnp.sin(fr * (_dot3f(z, w0[...]) + b0[...]))
    a = jnp.sin(fr * (_dot3f(a, w1[...]) + b1[...]))
    a = jnp.sin(fr * (_dot3f(a, w2[...]) + b2[...]))
    hh = _dot3f(a, wo[...])
    t = z[:, 0:1]
    win = jnp.exp(-t * dl_ref[...])
    row = lax.broadcasted_iota(jnp.int32, win.shape, 0) + pl.program_id(0) * z.shape[0]
    w = HY_WIDTH
    for o in range(2):
        hp = hh[:, (2 * o) * w:(2 * o + 1) * w] * win
        hn = jnp.where(row == 0, 0.0, hh[:, (2 * o + 1) * w:(2 * o + 2) * w] * win)
        o_ref[0, :, o * w:(o + 1) * w] = hp + hn
        o_ref[1, :, o * w:(o + 1) * w] = hp - hn


def _hyena_filter_sums(zfeat, fp, deltas):
    l = zfeat.shape[0]
    tl = min(256, l)
    w0, b0, w1, b1, w2, b2, wo, fr = fp
    n = 2 * HY_WIDTH
    consts = [w0, b0, w1, b1, w2, b2, wo, fr, deltas]
    return pl.pallas_call(
        _filter_kernel, name="hyena_filter",
        grid=(l // tl,),
        in_specs=[pl.BlockSpec((tl, LANES), lambda t: (t, 0))] + [_resident(c.shape) for c in consts],
        out_specs=pl.BlockSpec((2, tl, n), lambda t: (0, t, 0)),
        out_shape=jax.ShapeDtypeStruct((2, l, n), F32),
        compiler_params=_cparams(("parallel",), 32),
    )(zfeat, *consts)


def _spectrum_kernel(a_ref, d_ref, c_ref, s_ref, kr_ref, ki_ref):
    kr_ref[...] = _dot(c_ref[...], a_ref[0].astype(BF16))
    ki_ref[...] = -_dot(s_ref[...], d_ref[0].astype(BF16))


def _filter_spectrum(ad, tabs):
    _, l, n = ad.shape
    fp = tabs["c_hi"].shape[0]
    fb = tabs["fb"]
    cb = 256
    tab = pl.BlockSpec((fb, l), lambda j, k: (k, 0))
    out = pl.BlockSpec((fb, cb), lambda j, k: (k, j))
    return pl.pallas_call(
        _spectrum_kernel, name="dense_spectrum",
        grid=(n // cb, fp // fb),
        in_specs=[pl.BlockSpec((1, l, cb), lambda j, k: (0, 0, j)),
                  pl.BlockSpec((1, l, cb), lambda j, k: (1, 0, j)), tab, tab],
        out_specs=[out, out],
        out_shape=[jax.ShapeDtypeStruct((fp, n), F32)] * 2,
        compiler_params=_cparams(("parallel", "parallel"), 48),
    )(ad, ad, tabs["c_hi"], tabs["s_hi"])


def _longconv_kernel(u_ref, g_ref, kr_ref, ki_ref, bias_ref, wk_ref, c_ref, s_ref, ct_ref, st_ref,
                     o_ref, ub_ref, acc_ref):
    kb = pl.program_id(2)

    @pl.when(kb == 0)
    def _():
        ub_ref[...] = u_ref[0].astype(BF16)
        acc_ref[...] = jnp.zeros_like(acc_ref)

    ub = ub_ref[...]
    xr = _dot(c_ref[...], ub)
    xi = -_dot(s_ref[...], ub)
    kr = kr_ref[...]
    ki = ki_ref[...]
    wk = wk_ref[...]
    yr = ((xr * kr - xi * ki) * wk).astype(BF16)
    yi = ((xr * ki + xi * kr) * wk).astype(BF16)
    acc_ref[...] += _dot(ct_ref[...], yr) - _dot(st_ref[...], yi)

    @pl.when(kb == pl.num_programs(2) - 1)
    def _():
        u = u_ref[0]
        o_ref[0] = g_ref[0] * (acc_ref[...] + u * bias_ref[...])


def _gated_long_conv(u_arr, g_arr, kr, ki, k_col, bias, tabs):
    b, l, _ = u_arr.shape
    cb = 256
    ncb = HY_WIDTH // cb
    fp = tabs["c_hi"].shape[0]
    fb = tabs["fb"]
    tab = pl.BlockSpec((fb, l), lambda b, j, k: (k, 0))
    tabt = pl.BlockSpec((l, fb), lambda b, j, k: (0, k))
    spec = pl.BlockSpec((fb, cb), lambda b, j, k: (k, k_col * ncb + j))
    return pl.pallas_call(
        _longconv_kernel, name="dense_long_conv",
        grid=(b, ncb, fp // fb),
        in_specs=[
            pl.BlockSpec((1, l, cb), lambda b, j, k: (b, 0, j)),
            pl.BlockSpec((1, l, cb), lambda b, j, k: (b, 0, j)),
            spec, spec,
            pl.BlockSpec((1, cb), lambda b, j, k: (0, j)),
            pl.BlockSpec((fb, 1), lambda b, j, k: (k, 0)),
            tab, tab, tabt, tabt,
        ],
        out_specs=pl.BlockSpec((1, l, cb), lambda b, j, k: (b, 0, j)),
        out_shape=jax.ShapeDtypeStruct((b, l, HY_WIDTH), F32),
        scratch_shapes=[pltpu.VMEM((l, cb), BF16), pltpu.VMEM((l, cb), F32)],
        compiler_params=_cparams(("parallel", "parallel", "arbitrary"), 56),
    )(u_arr, g_arr, kr, ki, bias.reshape(1, HY_WIDTH), tabs["wk"],
      tabs["c_hi"], tabs["s_hi"], tabs["ct_hi"], tabs["st_hi"])


FFT_INNER = 64
FFT_COLS = HY_CHUNK
FFT_K2_CHUNK = 16
FFT_UNROLL = 16


def _fft_tables(l):
    n = 2 * l
    n1 = FFT_INNER
    n2 = n // n1
    h = n2 // 2
    ang = lambda idx, mod: (idx % mod).astype(F32) * (2.0 * math.pi / mod)
    k2 = jnp.arange(n2, dtype=jnp.int32)
    m2 = jnp.arange(h, dtype=jnp.int32)
    a1 = ang(k2[:, None] * m2[None, :], n2)
    c1, s1 = jnp.cos(a1), jnp.sin(a1)
    f1 = jnp.block([[c1, s1], [-s1, c1]])
    k1 = jnp.arange(n1, dtype=jnp.int32)
    a2 = ang(k1[None, None, :] * (k1[None, :, None] * n2 + k2[:, None, None]), n)
    c2, s2 = jnp.cos(a2), jnp.sin(a2)
    m = jnp.concatenate([jnp.concatenate([c2, s2], axis=2), jnp.concatenate([-s2, c2], axis=2)], axis=1)
    g = jnp.block([[c1.T, -s1.T], [s1.T, c1.T]]) * (1.0 / n)
    out = dict(n2=n2)
    for name, t in (("f1", f1), ("f1r", f1[:, :h]), ("m", m), ("mt", jnp.swapaxes(m, 1, 2)), ("g", g)):
        out[name + "_hi"] = t.astype(BF16)
    return out


def _mm(w_hi, w_lo, x):
    if w_lo is None:
        return _dot(w_hi, x.astype(BF16))
    hi, lo = _split(x)
    return _dot3(w_hi, w_lo, hi, lo)


def _row_block_matmul(w_hi, w_lo, src_ref, dst_ref, count):
    m, k = w_hi.shape

    def body(i, carry):
        x = src_ref[pl.ds(pl.multiple_of(i * k, k), k), :]
        dst_ref[pl.ds(pl.multiple_of(i * m, m), m), :] = _mm(w_hi, w_lo, x).astype(dst_ref.dtype)
        return carry
    lax.fori_loop(0, count, body, 0, unroll=FFT_UNROLL)


def _short_conv_rows(z, w_ref, b_ref):
    l = z.shape[0]
    row = lax.broadcasted_iota(jnp.int32, z.shape, 0)
    prev = jnp.where(row == 0, 0.0, pltpu.roll(z, 1, axis=0))
    nxt = jnp.where(row == l - 1, 0.0, pltpu.roll(z, l - 1, axis=0))
    return w_ref[0:1, :] * prev + w_ref[1:2, :] * z + w_ref[2:3, :] * nxt + b_ref[...]


def _fftconv_kernel(u_ref, g_ref, uw_ref, ub_ref, gw_ref, gb_ref, kr_ref, ki_ref, bias_ref,
                    f1_ref, m_ref, mt_ref, gm_ref, o_ref, z_scr, y_scr, p_scr, q_scr, *, u_conv):
    kc = pl.program_id(2)
    n1 = FFT_INNER
    cc = z_scr.shape[1]
    n2 = z_scr.shape[0] // n1
    h = n2 // 2

    def load_u(s):
        return _short_conv_rows(u_ref[s, 0].astype(F32), uw_ref, ub_ref) if u_conv else u_ref[s, 0]

    @pl.when(kc == 0)
    def _stage1():
        z = [jnp.swapaxes(load_u(s).astype(BF16).reshape(h, n1, cc), 0, 1) for s in range(2)]
        z_scr[...] = jnp.concatenate(z, axis=1).reshape(n1 * n2, cc)
        _row_block_matmul(f1_ref[...], None, z_scr, p_scr, n1)
        q_scr[...] = jnp.swapaxes(p_scr[...].reshape(n1, 2 * n2, cc), 0, 1).reshape(2 * n2 * n1, cc)

    ks = kr_ref.shape[0]
    rows = [(pl.multiple_of((kc * ks + j) * n1, n1), pl.multiple_of((n2 + kc * ks + j) * n1, n1)) for j in range(ks)]
    xs = [_mm(m_ref[j], None, jnp.concatenate([q_scr[pl.ds(r0, n1), :], q_scr[pl.ds(i0, n1), :]], axis=0))
          for j, (r0, i0) in enumerate(rows)]
    ys = []
    for j, x in enumerate(xs):
        xr, xi = x[:n1], x[n1:]
        kr, ki = kr_ref[j], ki_ref[j]
        ys.append(jnp.concatenate([xr * kr - xi * ki, xr * ki + xi * kr], axis=0))
    for j, (r0, i0) in enumerate(rows):
        b = _mm(mt_ref[j], None, ys[j]).astype(p_scr.dtype)
        p_scr[pl.ds(r0, n1), :] = b[:n1]
        p_scr[pl.ds(i0, n1), :] = b[n1:]

    @pl.when(kc == pl.num_programs(2) - 1)
    def _stage1_inv():
        q_scr[...] = jnp.swapaxes(p_scr[...].reshape(2 * n2, n1, cc), 0, 1).reshape(2 * n2 * n1, cc)
        _row_block_matmul(gm_ref[...], None, q_scr, y_scr, n1)
        y3 = y_scr[...].reshape(n1, n2, cc)
        for s in range(2):
            y = jnp.swapaxes(y3[:, s * h:(s + 1) * h, :], 0, 1).reshape(h * n1, cc)
            gate = _short_conv_rows(g_ref[s, 0].astype(F32), gw_ref, gb_ref)
            o_ref[s, 0] = (gate * (y + load_u(s) * bias_ref[...])).astype(o_ref.dtype)


def _fft_gated_long_conv(u, u_col, z_hy, g_col, short_w, short_b, kr, ki, k_col, bias, ft, out_dtype):
    b, _, l, cc = z_hy.shape
    c = HY_WIDTH
    n1, n2, ks = FFT_INNER, ft["n2"], FFT_K2_CHUNK
    p, ncc = b // 2, c // cc
    u_conv = u is None
    if u_conv:
        u = z_hy
    tok = lambda col: pl.BlockSpec((2, 1, None, l, cc), lambda q, j, k: (0, q, col * ncc + j, 0, 0))
    taps = lambda col: pl.BlockSpec((3, cc), lambda q, j, k: (0, col * ncc + j))
    row = lambda col: pl.BlockSpec((1, cc), lambda q, j, k: (0, col * ncc + j))
    filt = pl.BlockSpec((ks, n1, cc), lambda q, j, k: (k, 0, k_col * ncc + j))
    tab = pl.BlockSpec((ks, 2 * n1, 2 * n1), lambda q, j, k: (k, 0, 0))
    res = lambda a: pl.BlockSpec(a.shape, lambda q, j, k: (0,) * a.ndim)
    sb = short_b.reshape(1, -1)
    y = pl.pallas_call(
        functools.partial(_fftconv_kernel, u_conv=u_conv), name="fft_long_conv",
        grid=(p, ncc, n2 // ks),
        in_specs=[tok(u_col), tok(g_col), taps(u_col), row(u_col), taps(g_col), row(g_col), filt, filt, row(0),
                  res(ft["f1_hi"]), tab, tab, res(ft["g_hi"])],
        out_specs=tok(0),
        out_shape=jax.ShapeDtypeStruct((2, p, ncc, l, cc), out_dtype),
        scratch_shapes=[pltpu.VMEM((n1 * n2, cc), BF16), pltpu.VMEM((n1 * n2, cc), F32),
                        pltpu.VMEM((2 * n2 * n1, cc), BF16), pltpu.VMEM((2 * n2 * n1, cc), BF16)],
        compiler_params=_cparams(("parallel", "parallel", "arbitrary"), 56),
    )(u.reshape((2, p) + u.shape[1:]), z_hy.reshape((2, p) + z_hy.shape[1:]), short_w, sb, short_w, sb, kr, ki,
      bias.reshape(1, c), ft["f1_hi"], ft["m_hi"], ft["mt_hi"], ft["g_hi"])
    return y.reshape(b, ncc, l, cc)


def _fftspec_kernel(a_ref, d_ref, f1_ref, m_ref, kr_ref, ki_ref, z_scr, p_scr, q_scr):
    n1 = FFT_INNER
    cc = z_scr.shape[1]
    h = z_scr.shape[0] // n1
    n2 = 2 * h
    for src_ref, out_ref, lo in ((a_ref, kr_ref, 0), (d_ref, ki_ref, n1)):
        z_scr[...] = jnp.swapaxes(src_ref[0].reshape(h, n1, cc), 0, 1).reshape(n1 * h, cc)
        _row_block_matmul(f1_ref[...], None, z_scr, p_scr, n1)
        q_scr[...] = jnp.swapaxes(p_scr[...].reshape(n1, 2 * n2, cc), 0, 1).reshape(2 * n2 * n1, cc)

        def body(k2, carry):
            r0 = pl.multiple_of(k2 * n1, n1)
            i0 = pl.multiple_of((n2 + k2) * n1, n1)
            slab = jnp.concatenate([q_scr[pl.ds(r0, n1), :], q_scr[pl.ds(i0, n1), :]], axis=0)
            out_ref[k2] = _mm(m_ref[k2][lo:lo + n1], None, slab)
            return carry
        lax.fori_loop(0, n2, body, 0, unroll=FFT_UNROLL)


def _fft_filter_spectrum(ad, ft):
    _, l, n = ad.shape
    n1, n2, cc = FFT_INNER, ft["n2"], FFT_COLS
    h = n2 // 2
    res = lambda a: pl.BlockSpec(a.shape, lambda j: (0,) * a.ndim)
    out = pl.BlockSpec((n2, n1, cc), lambda j: (0, 0, j))
    tabs = [ft["f1r_hi"], ft["m_hi"]]
    return pl.pallas_call(
        _fftspec_kernel, name="fft_filter_spectrum",
        grid=(n // cc,),
        in_specs=[pl.BlockSpec((1, l, cc), lambda j: (0, 0, j)), pl.BlockSpec((1, l, cc), lambda j: (1, 0, j))]
                 + [res(t) for t in tabs],
        out_specs=[out, out],
        out_shape=[jax.ShapeDtypeStruct((n2, n1, n), F32)] * 2,
        scratch_shapes=[pltpu.VMEM((n1 * h, cc), F32), pltpu.VMEM((2 * n2 * n1, cc), F32),
                        pltpu.VMEM((2 * n2 * n1, cc), F32)],
        compiler_params=_cparams(("parallel",), 56),
    )(ad, ad, *tabs)


def _softmax_weights(s, sink=None):
    m = jnp.max(s, axis=-1, keepdims=True)
    if sink is not None:
        m = jnp.maximum(m, sink)
    p = jnp.exp(s - m)
    den = jnp.sum(p, axis=-1, keepdims=True)
    if sink is not None:
        den = den + jnp.exp(sink - m)
    return p.astype(BF16), den


def _softmax_pv(s, v, sink=None):
    p, den = _softmax_weights(s, sink)
    return _dot(p, v) / den


def _qk(q, k):
    return lax.dot_general(q, k, (((1,), (1,)), ((), ())), preferred_element_type=F32)


def _half_mask(q_tile, half):
    lane = lax.broadcasted_iota(jnp.int32, q_tile.shape, 1)
    keep = (lane < HEAD_DIM) if half == 0 else (lane >= HEAD_DIM)
    return jnp.where(keep, q_tile, jnp.zeros_like(q_tile))


def _merge_halves(o_even, o_odd):
    lane = lax.broadcasted_iota(jnp.int32, o_even.shape, 1)
    return jnp.where(lane < HEAD_DIM, o_even, o_odd)


def _swa_kernel(sink_ref, q_ref, kp_ref, kc_ref, kn_ref, vp_ref, vc_ref, vn_ref, kx_ref, vx_ref, o_ref):
    n = pl.program_id(1)
    nb = pl.num_programs(1)
    blk = SWA_BLOCK
    g = SWA_HEADS // SWA_KV_HEADS
    n_ctx = kx_ref.shape[1]
    rows = g * blk
    ri = lax.broadcasted_iota(jnp.int32, (rows, 3 * blk + n_ctx), 0) & (blk - 1)
    ci = lax.broadcasted_iota(jnp.int32, (rows, 3 * blk + n_ctx), 1)
    ok_prev = (ci < blk) & (ci >= ri) & (n > 0)
    ok_cur = (ci >= blk) & (ci < 2 * blk)
    ok_next = (ci >= 2 * blk) & (ci < 3 * blk) & (ci - 2 * blk <= ri) & (n < nb - 1)
    ok = ok_prev | ok_cur | ok_next | (ci >= 3 * blk)
    bias = jnp.where(ok, 0.0, NEG_BIG)
    rsel = lax.broadcasted_iota(jnp.int32, (rows, 1), 0) >> (blk.bit_length() - 1)
    outs = [None] * SWA_HEADS
    kvs = range(SWA_KV_HEADS)
    lanes = [slice(kv * LANES, (kv + 1) * LANES) for kv in kvs]
    scores, sinks = [], []
    for kv in kvs:
        cs = lanes[kv]
        k = jnp.concatenate([kp_ref[0, :, cs], kc_ref[0, :, cs], kn_ref[0, :, cs], kx_ref[0, :, cs]], axis=0)
        q_parts = []
        sink = jnp.zeros((rows, 1), F32)
        for gi in range(g):
            h = kv * g + gi
            q_parts.append(_half_mask(q_ref[0, :, (h // 2) * LANES:(h // 2 + 1) * LANES], h % 2))
            sink = jnp.where(rsel == gi, sink_ref[h], sink)
        scores.append(_qk(jnp.concatenate(q_parts, axis=0), k))
        sinks.append(sink)
    probs = [_softmax_weights(scores[kv] + bias, sinks[kv]) for kv in kvs]
    for kv in kvs:
        cs = lanes[kv]
        v = jnp.concatenate([vp_ref[0, :, cs], vc_ref[0, :, cs], vn_ref[0, :, cs], vx_ref[0, :, cs]], axis=0)
        pw, den = probs[kv]
        o = _dot(pw, v) / den
        for gi in range(g):
            outs[kv * g + gi] = o[gi * blk:(gi + 1) * blk]
    for p in range(SWA_HEADS // 2):
        o_ref[0, :, p * LANES:(p + 1) * LANES] = _merge_halves(outs[2 * p], outs[2 * p + 1]).astype(o_ref.dtype)


def _window_gqa(q, k, v, kx, vx, sink):
    b, l, _ = q.shape
    nb = l // SWA_BLOCK
    n_ctx = kx.shape[1]
    kvw = k.shape[2]

    def blk(off):
        return pl.BlockSpec((1, SWA_BLOCK, kvw), lambda b, n: (b, jnp.clip(n + off, 0, nb - 1), 0))

    ctx = pl.BlockSpec((1, n_ctx, kvw), lambda b, n: (b, 0, 0))
    return pl.pallas_call(
        _swa_kernel, name="window_gqa",
        grid=(b, nb),
        in_specs=[
            pl.BlockSpec(memory_space=pltpu.SMEM),
            pl.BlockSpec((1, SWA_BLOCK, q.shape[2]), lambda b, n: (b, n, 0)),
            blk(-1), blk(0), blk(1), blk(-1), blk(0), blk(1), ctx, ctx,
        ],
        out_specs=pl.BlockSpec((1, SWA_BLOCK, q.shape[2]), lambda b, n: (b, n, 0)),
        out_shape=jax.ShapeDtypeStruct(q.shape, BF16),
        compiler_params=_cparams(("parallel", "parallel"), 32),
    )(sink, q, k, k, k, v, v, v, kx, vx)


def _na_bias_kernel(rpb_ref, o_ref):
    off = pl.program_id(0)
    h = pl.program_id(1)
    nd = 2 * NA_COLS - 1
    q = lax.broadcasted_iota(jnp.int32, (GRID_W, LANES), 0)
    lane = lax.broadcasted_iota(jnp.int32, (GRID_W, LANES), 1)
    kc = lane & (GRID_W - 1)
    upper = lane >= GRID_W
    cstart = jnp.clip(q - NA_COLS // 2, 0, GRID_W - NA_COLS)
    valid = (kc >= cstart) & (kc < cstart + NA_COLS)
    dc = jnp.clip(kc - q, -(NA_COLS - 1), NA_COLS - 1) + (NA_COLS - 1)
    for j in range(NA_MAX_ROWS // 2):
        base0 = (h * (2 * NA_MAX_ROWS - 1) + (2 * j - off + NA_MAX_ROWS - 1)) * nd
        base1 = base0 + nd
        t = jnp.zeros((GRID_W, LANES), F32)
        for d in range(nd):
            val = jnp.where(upper, rpb_ref[base1 + d], rpb_ref[base0 + d])
            t = jnp.where(dc == d, val, t)
        o_ref[0, 0, :, j * LANES:(j + 1) * LANES] = jnp.where(valid, t, NEG_BIG)


def _na_bias_table(rpb):
    return pl.pallas_call(
        _na_bias_kernel, name="na_bias_table",
        grid=(NA_MAX_ROWS, NA_HEADS),
        in_specs=[pl.BlockSpec(memory_space=pltpu.SMEM)],
        out_specs=pl.BlockSpec((1, 1, GRID_W, NA_MAX_ROWS * GRID_W), lambda o, h: (o, h, 0, 0)),
        out_shape=jax.ShapeDtypeStruct((NA_MAX_ROWS, NA_HEADS, GRID_W, NA_MAX_ROWS * GRID_W), F32),
        compiler_params=_cparams(("arbitrary", "arbitrary"), 16),
    )(rpb.reshape(-1))


NA_ROWS_PER_STEP = 4


def _na_kernel(q_ref, k_ref, v_ref, kx_ref, vx_ref, bias_ref, o_ref, *, rows):
    wr = NA_MAX_ROWS
    nloc = wr * GRID_W
    for i in range(NA_ROWS_PER_STEP):
        r = pl.program_id(1) * NA_ROWS_PER_STEP + i
        first = jnp.clip(r - wr // 2, 0, rows - wr)
        start = pl.multiple_of(first * GRID_W, GRID_W)
        off = r - first
        qr = slice(i * GRID_W, (i + 1) * GRID_W)
        pairs = range(NA_HEADS // 2)
        lanes = [slice(p * LANES, (p + 1) * LANES) for p in pairs]
        scores = []
        for p in pairs:
            q_tile = q_ref[0, qr, lanes[p]]
            q2 = jnp.concatenate([_half_mask(q_tile, 0), _half_mask(q_tile, 1)], axis=0)
            k = jnp.concatenate([k_ref[0, pl.ds(start, nloc), lanes[p]], kx_ref[0, :, lanes[p]]], axis=0)
            scores.append(_qk(q2, k))
        probs = []
        for p in pairs:
            s = scores[p]
            bias = jnp.concatenate([bias_ref[off, 2 * p], bias_ref[off, 2 * p + 1]], axis=0)
            probs.append(_softmax_weights(jnp.concatenate([s[:, :nloc] + bias, s[:, nloc:]], axis=1)))
        for p in pairs:
            pw, den = probs[p]
            v = jnp.concatenate([v_ref[0, pl.ds(start, nloc), lanes[p]], vx_ref[0, :, lanes[p]]], axis=0)
            o = _dot(pw, v) / den
            o_ref[0, qr, lanes[p]] = _merge_halves(o[:GRID_W], o[GRID_W:]).astype(o_ref.dtype)


def _neighbourhood_attention(q, k, v, kx, vx, bias_tab):
    b, l, w = q.shape
    rows = l // GRID_W
    n_ctx = kx.shape[1]
    tq = NA_ROWS_PER_STEP * GRID_W
    full = pl.BlockSpec((1, l, w), lambda b, r: (b, 0, 0))
    ctx = pl.BlockSpec((1, n_ctx, w), lambda b, r: (b, 0, 0))
    return pl.pallas_call(
        functools.partial(_na_kernel, rows=rows), name="neighbourhood_attention",
        grid=(b, rows // NA_ROWS_PER_STEP),
        in_specs=[pl.BlockSpec((1, tq, w), lambda b, r: (b, r, 0)), full, full, ctx, ctx,
                  _resident(bias_tab.shape)],
        out_specs=pl.BlockSpec((1, tq, w), lambda b, r: (b, r, 0)),
        out_shape=jax.ShapeDtypeStruct(q.shape, BF16),
        compiler_params=_cparams(("parallel", "arbitrary"), 48),
    )(q, k, v, kx, vx, bias_tab)


def _ctx_attn_kernel(sink_ref, qs_ref, ks_ref, vs_ref, qn_ref, kn_ref, vn_ref, os_ref, on_ref):
    n = qs_ref.shape[1]
    g = SWA_HEADS // SWA_KV_HEADS
    rsel = lax.broadcasted_iota(jnp.int32, (g * n, 1), 0) >> (n.bit_length() - 1)
    outs = [None] * SWA_HEADS
    for kv in range(SWA_KV_HEADS):
        cs = slice(kv * LANES, (kv + 1) * LANES)
        q_parts = []
        sink = jnp.zeros((g * n, 1), F32)
        for gi in range(g):
            h = kv * g + gi
            q_parts.append(_half_mask(qs_ref[0, :, (h // 2) * LANES:(h // 2 + 1) * LANES], h % 2))
            sink = jnp.where(rsel == gi, sink_ref[h], sink)
        o = _softmax_pv(_qk(jnp.concatenate(q_parts, axis=0), ks_ref[0, :, cs]), vs_ref[0, :, cs], sink)
        for gi in range(g):
            outs[kv * g + gi] = o[gi * n:(gi + 1) * n]
    for p in range(SWA_HEADS // 2):
        os_ref[0, :, p * LANES:(p + 1) * LANES] = _merge_halves(outs[2 * p], outs[2 * p + 1]).astype(os_ref.dtype)
    for p in range(NA_HEADS // 2):
        cs = slice(p * LANES, (p + 1) * LANES)
        q_tile = qn_ref[0, :, cs]
        q2 = jnp.concatenate([_half_mask(q_tile, 0), _half_mask(q_tile, 1)], axis=0)
        o = _softmax_pv(_qk(q2, kn_ref[0, :, cs]), vn_ref[0, :, cs])
        on_ref[0, :, cs] = _merge_halves(o[:n], o[n:]).astype(on_ref.dtype)


def _context_attention(qs, ks, vs, qn, kn, vn, sink):
    b, n, w = qs.shape
    spec = lambda a: pl.BlockSpec((1,) + a.shape[1:], lambda b: (b, 0, 0))
    return pl.pallas_call(
        _ctx_attn_kernel, name="context_attention",
        grid=(b,),
        in_specs=[pl.BlockSpec(memory_space=pltpu.SMEM)] + [spec(a) for a in (qs, ks, vs, qn, kn, vn)],
        out_specs=[spec(qs), spec(qn)],
        out_shape=[jax.ShapeDtypeStruct(qs.shape, BF16), jax.ShapeDtypeStruct(qn.shape, BF16)],
        compiler_params=_cparams(("parallel",), 32),
    )(sink, qs, ks, vs, qn, kn, vn)


def _merge_ffn_kernel(x_ref, mg_ref, yh_ref, ys_ref, yn_ref, zg_ref, wb_ref, wo_ref,
                      sh_ref, sc_ref, gt_ref, g_ref, wg_ref, wu_ref, wd_ref, o_ref, *, fc):
    d = D_MODEL
    yh = jnp.concatenate([yh_ref[0, c] for c in range(yh_ref.shape[1])], axis=1)
    ys = (yh.astype(BF16), ys_ref[0], yn_ref[0])
    m = None
    for n in range(3):
        proj = _dot(ys[n], wb_ref[n])
        term = zg_ref[0, :, n * d:(n + 1) * d].astype(F32) * proj
        m = term if m is None else m + term
    x = x_ref[0] + mg_ref[0] * _dot(m.astype(BF16), wo_ref[...])
    o_ref[0] = _swiglu_residual(x, sh_ref, sc_ref, gt_ref, g_ref, wg_ref, wu_ref, wd_ref, fc)


def _merge_and_ffn(x, mix_gate, y_hy, y_swa, y_na, gates, w_branch, w_out, layer,
                   shift, scale, gate, gain, wg, wu, wd):
    b, l, d = x.shape
    tm = min(512, l)
    tok = lambda n: pl.BlockSpec((1, tm, n), lambda b, t: (b, t, 0))
    idx = (layer, 1)
    return pl.pallas_call(
        functools.partial(_merge_ffn_kernel, fc=FFN_CHUNK), name="merge_ffn",
        grid=(b, l // tm),
        in_specs=[tok(d), _mod_spec(mix_gate),
                  pl.BlockSpec((1, HY_WIDTH // HY_CHUNK, tm, HY_CHUNK), lambda b, t: (b, 0, t, 0)),
                  tok(HY_WIDTH), tok(HY_WIDTH), tok(3 * d),
                  _layer_slab(w_branch, (layer,)), _layer_slab(w_out, (layer,)),
                  _mod_spec(shift), _mod_spec(scale), _mod_spec(gate), _resident((1, d)),
                  _layer_slab(wg, idx), _layer_slab(wu, idx), _layer_slab(wd, idx)],
        out_specs=tok(d),
        out_shape=jax.ShapeDtypeStruct(x.shape, F32),
        compiler_params=_cparams(("parallel", "parallel"), 58),
    )(x, mix_gate[0], y_hy, y_swa, y_na, gates, w_branch, w_out, shift[0], scale[0], gate[0], gain.reshape(1, d),
      wg, wu, wd)


def _dft_tables(l):
    n = 2 * l
    fb = 256 if l >= 1024 else LANES * ((l + 1 + LANES - 1) // LANES)
    fp = fb * ((l + 1 + fb - 1) // fb)
    k = jnp.arange(fp, dtype=jnp.int32)[:, None]
    t = jnp.arange(l, dtype=jnp.int32)[None, :]
    ang = ((k * t) % n).astype(F32) * (2.0 * math.pi / n)
    live = k <= l
    c = jnp.where(live, jnp.cos(ang), 0.0)
    s = jnp.where(live, jnp.sin(ang), 0.0)
    c_hi = c.astype(BF16)
    s_hi = s.astype(BF16)
    wk = jnp.where((k == 0) | (k == l), 1.0, 2.0) * jnp.where(live, 1.0 / n, 0.0)
    return dict(fb=fb, c_hi=c_hi, s_hi=s_hi, ct_hi=c_hi.T, st_hi=s_hi.T, wk=wk.astype(F32))


def _filter_features(l):
    t = jnp.linspace(0.0, 1.0, l, dtype=F32)[:, None]
    w = (2.0 * math.pi / l) * jnp.arange(l, dtype=F32)[:, None]
    f = jnp.linspace(1e-4, HY_BANDS - 1, HY_BANDS, dtype=F32)[None, :]
    z = jnp.concatenate([t, jnp.cos(f * w), -jnp.sin(f * w)], axis=-1)
    return jnp.pad(z, ((0, 0), (0, LANES - HY_EMB)))


def _decay_rates():
    max_decay = math.log(HY_TARGET) / HY_FAST_DECAY
    min_decay = math.log(HY_TARGET) / HY_SLOW_DECAY
    return jnp.abs(jnp.linspace(min_decay, max_decay, HY_WIDTH, dtype=F32))[None, :]


def _rope_tables(l):
    pos = jnp.arange(l)
    row = (pos // GRID_W).astype(F32)
    col = (pos % GRID_W).astype(F32)
    half = HEAD_DIM // 2
    inv = 1.0 / (ROPE_BASE ** (jnp.arange(0, half, 2, dtype=F32) / half))
    ar = row[:, None] * inv[None, :]
    ac = col[:, None] * inv[None, :]
    cos = jnp.concatenate([jnp.cos(ar), jnp.cos(ar), jnp.cos(ac), jnp.cos(ac)], axis=-1)
    sin = jnp.concatenate([-jnp.sin(ar), jnp.sin(ar), -jnp.sin(ac), jnp.sin(ac)], axis=-1)
    return jnp.tile(cos, (1, 2)), jnp.tile(sin, (1, 2))


def _pad_to(a, shape):
    return jnp.pad(a, [(0, s - d) for d, s in zip(a.shape, shape)])


def _use_fft(bsz, length):
    return bsz % 2 == 0 and length % (8 * FFT_INNER) == 0


def _conv_tables(bsz, length):
    return _fft_tables(length) if _use_fft(bsz, length) else _dft_tables(length)


def _hyena_branch(z_hy, short_w, short_b, feat, fparams, deltas, bias, tabs):
    ad = _hyena_filter_sums(feat, fparams, deltas)
    if "n2" in tabs:
        kr, ki = _fft_filter_spectrum(ad, tabs)
        y1 = _fft_gated_long_conv(None, 0, z_hy, 1, short_w, short_b, kr, ki, 0, bias[0], tabs, F32)
        return _fft_gated_long_conv(y1, 0, z_hy, 2, short_w, short_b, kr, ki, 1, bias[1], tabs, BF16)
    v, x1, x2 = _short_conv3(z_hy, short_w, short_b)
    kr, ki = _filter_spectrum(ad, tabs)
    conv = functools.partial(_gated_long_conv, tabs=tabs)
    y1 = conv(v, x1, kr, ki, 0, bias[0])
    y = conv(y1, x2, kr, ki, 1, bias[1])
    bsz, length, _ = y.shape
    return y.reshape(bsz, length, HY_WIDTH // HY_CHUNK, HY_CHUNK).transpose(0, 2, 1, 3)


def kernel(x, c, ctx, c_ctx, w_ada, b_ada, norm_g, ffn_w_gate, ffn_w_up, ffn_w_down,
           w_in, hy_short_w, hy_short_b, hy_pe_w0, hy_pe_b0, hy_pe_w1, hy_pe_b1,
           hy_pe_w2, hy_pe_b2, hy_pe_wout, hy_sin_freq, hy_bias,
           swa_q_gain, swa_k_gain, swa_sink, na_q_gain, na_k_gain, na_rpb,
           w_branch, w_out):
    bsz, seq, d = x.shape
    n_ctx = ctx.shape[1]
    depth = w_ada.shape[0]

    c16 = _pad_to(jnp.concatenate([c, c_ctx[None, :]], axis=0), (MOD_ROWS, d))
    mods = _adaln_mods(c16, w_ada, b_ada).reshape(depth, MOD_ROWS, N_MOD, 1, d)

    tabs_x = _conv_tables(bsz, seq)
    tabs_c = _conv_tables(bsz, n_ctx)
    feat_x = _filter_features(seq)
    feat_c = _filter_features(n_ctx)
    deltas = _decay_rates()
    cos_x, sin_x = _rope_tables(seq)
    cos_c = jnp.ones((n_ctx, LANES), F32)
    sin_c = jnp.zeros((n_ctx, LANES), F32)
    eye = jnp.arange(MXU_DIM) // HEAD_DIM
    bd = (eye[:, None] == eye[None, :]).astype(BF16)

    wg = ffn_w_gate.astype(BF16)
    wu = ffn_w_up.astype(BF16)
    wd = ffn_w_down.astype(BF16)
    wb = w_branch.astype(BF16)
    wo = w_out.astype(BF16)
    win = w_in.astype(BF16)
    kv = win[:, :, _OFF_SWA + _SWA_Q:_OFF_NA].reshape(depth, d, 2, SWA_KV_HEADS, 1, HEAD_DIM)
    win_kv = jnp.broadcast_to(kv, (depth, d, 2, SWA_KV_HEADS, 2, HEAD_DIM)).reshape(depth, d, 4 * _SWA_KV)

    xc = ctx
    for i in range(depth):
        last = i == depth - 1
        mx = lambda j: (mods, i, j, None)
        mc = lambda j: (mods, i, j, bsz)
        tile2 = lambda g: jnp.tile(g, 2 * MXU_DIM // LANES)
        head_gains = jnp.stack([tile2(swa_q_gain[i]), tile2(swa_k_gain[i]), tile2(na_q_gain[i]), tile2(na_k_gain[i])])
        fparams = (_pad_to(hy_pe_w0[i], (LANES, LANES)), _pad_to(hy_pe_b0[i][None], (1, LANES)),
                   _pad_to(hy_pe_w1[i], (LANES, LANES)), _pad_to(hy_pe_b1[i][None], (1, LANES)),
                   _pad_to(hy_pe_w2[i], (LANES, LANES)), _pad_to(hy_pe_b2[i][None], (1, LANES)),
                   _pad_to(hy_pe_wout[i], (LANES, 4 * HY_WIDTH)), _pad_to(hy_sin_freq[i][None], (1, LANES)))

        x = _ffn_half_step(x, mx(0), mx(1), mx(2), norm_g[i, 0], wg, wu, wd, (i, 0))
        xc = _ffn_half_step(xc, mc(0), mc(1), mc(2), norm_g[i, 0], wg, wu, wd, (i, 0))

        z_hy, q_s, k_s, v_s, q_n, k_n, v_n, gates = _in_projection(
            x, mx(3), mx(4), norm_g[i, 1], cos_x, sin_x, head_gains, bd, win, win_kv, i)
        zc_hy, qc_s, kc_s, vc_s, qc_n, kc_n, vc_n, gates_c = _in_projection(
            xc, mc(3), mc(4), norm_g[i, 1], cos_c, sin_c, head_gains, bd, win, win_kv, i)

        y_hy = _hyena_branch(z_hy, hy_short_w[i], hy_short_b[i], feat_x, fparams, deltas, hy_bias[i], tabs_x)
        y_swa = _window_gqa(q_s, k_s, v_s, kc_s, vc_s, swa_sink[i])
        y_na = _neighbourhood_attention(q_n, k_n, v_n, kc_n, vc_n, _na_bias_table(na_rpb[i]))
        x = _merge_and_ffn(x, mx(5), y_hy, y_swa, y_na, gates, wb, wo, i,
                           mx(6), mx(7), mx(8), norm_g[i, 2], wg, wu, wd)

        if not last:
            yc_hy = _hyena_branch(zc_hy, hy_short_w[i], hy_short_b[i], feat_c, fparams, deltas, hy_bias[i], tabs_c)
            yc_swa, yc_na = _context_attention(qc_s, kc_s, vc_s, qc_n, kc_n, vc_n, swa_sink[i])
            xc = _merge_and_ffn(xc, mc(5), yc_hy, yc_swa, yc_na, gates_c, wb, wo, i,
                                mc(6), mc(7), mc(8), norm_g[i, 2], wg, wu, wd)
    return x
```

```python
import functools
import math

import jax
import jax.numpy as jnp
from jax import lax
from jax.experimental import pallas as pl
from jax.experimental.pallas import tpu as pltpu

F32 = jnp.float32
BF16 = jnp.bfloat16

D_MODEL = 1024
DEPTH = 4
GRID_W = 64
HEAD_DIM = 64
N_MOD = 9
RMS_EPS = 1e-6

HY_WIDTH = D_MODEL // 2
HY_CHUNK = 256
HY_EMB = 33
HY_BANDS = (HY_EMB - 1) // 2
HY_FFN = 64
HY_FAST_DECAY = 0.3
HY_SLOW_DECAY = 1.5
HY_TARGET = 1e-2

SWA_HEADS = 8
SWA_KV_HEADS = 2
SWA_WINDOW = 128
SWA_BLOCK = 128
ROPE_BASE = 10000.0

NA_HEADS = 8
NA_MAX_ROWS = 8
NA_COLS = 16
NA_COL_BLOCK = 16

FFN_HIDDEN = 256 * ((8 * D_MODEL // 3 + 255) // 256)
FFN_CHUNK = 1536

LANES = 128
MXU_DIM = 256
MOD_ROWS = 16
NEG_BIG = -1e30


def _cparams(sem, vmem_mb):
    return pltpu.CompilerParams(dimension_semantics=sem, vmem_limit_bytes=vmem_mb * 1024 * 1024)


def _resident(shape):
    nd = len(shape)
    return pl.BlockSpec(shape, lambda *_: (0,) * nd, pipeline_mode=pl.Buffered(1))


def _layer_slab(arr, idx):
    rest = arr.shape[len(idx):]
    return pl.BlockSpec((None,) * len(idx) + rest, lambda *_: tuple(idx) + (0,) * len(rest),
                        pipeline_mode=pl.Buffered(1))


def _split(x):
    hi = x.astype(BF16)
    lo = (x - hi.astype(F32)).astype(BF16)
    return hi, lo


def _dot(a, b):
    return jnp.dot(a, b, preferred_element_type=F32)


def _dot3(a_hi, a_lo, b_hi, b_lo):
    return _dot(a_hi, b_hi) + _dot(a_lo, b_hi) + _dot(a_hi, b_lo)


def _dot3f(a, b):
    a_hi, a_lo = _split(a)
    b_hi, b_lo = _split(b)
    return _dot3(a_hi, a_lo, b_hi, b_lo)


def _rms_mod(x, gain, shift, scale):
    ms = jnp.mean(x * x, axis=-1, keepdims=True)
    y = x * lax.rsqrt(ms + RMS_EPS) * gain
    return y * (1.0 + scale) + shift


def _head_sumsq(x, bd):
    return [_dot(jnp.square(x[:, j * MXU_DIM:(j + 1) * MXU_DIM]).astype(BF16), bd) for j in range(x.shape[1] // MXU_DIM)]


def _head_normalize(x, sumsq, gain):
    outs = [x[:, j * MXU_DIM:(j + 1) * MXU_DIM] * lax.rsqrt(s * (1.0 / HEAD_DIM) + RMS_EPS) * gain
            for j, s in enumerate(sumsq)]
    return outs[0] if len(outs) == 1 else jnp.concatenate(outs, axis=1)


def _rope(x, cos, sin):
    lane = lax.broadcasted_iota(jnp.int32, (x.shape[0], LANES), 1)
    first = (lane & 31) < 16
    outs = []
    for j in range(x.shape[1] // LANES):
        xc = x[:, j * LANES:(j + 1) * LANES]
        partner = jnp.where(first, pltpu.roll(xc, LANES - 16, axis=1), pltpu.roll(xc, 16, axis=1))
        outs.append(xc * cos + partner * sin)
    return outs[0] if len(outs) == 1 else jnp.concatenate(outs, axis=1)


def _mods_kernel(c_ref, w_ref, b_ref, o_ref):
    c = c_ref[...]
    a = c * jax.nn.sigmoid(c)
    o_ref[0] = _dot3f(a, w_ref[0]) + b_ref[0]


def _adaln_mods(c16, w_ada, b_ada):
    depth, d, n = w_ada.shape
    nb = 1152
    return pl.pallas_call(
        _mods_kernel, name="adaln_mods",
        grid=(depth, n // nb),
        in_specs=[
            pl.BlockSpec((MOD_ROWS, d), lambda i, j: (0, 0)),
            pl.BlockSpec((1, d, nb), lambda i, j: (i, 0, j)),
            pl.BlockSpec((1, 1, nb), lambda i, j: (i, 0, j)),
        ],
        out_specs=pl.BlockSpec((1, MOD_ROWS, nb), lambda i, j: (i, 0, j)),
        out_shape=jax.ShapeDtypeStruct((depth, MOD_ROWS, n), F32),
        compiler_params=_cparams(("arbitrary", "arbitrary"), 48),
    )(c16, w_ada, b_ada.reshape(depth, 1, n))


def _swiglu_residual(x, sh_ref, sc_ref, gt_ref, g_ref, wg_ref, wu_ref, wd_ref, fc):
    hb = _rms_mod(x, g_ref[...], sh_ref[0], sc_ref[0]).astype(BF16)
    acc = None
    for f0 in range(0, FFN_HIDDEN, fc):
        f1 = min(f0 + fc, FFN_HIDDEN)
        g = _dot(hb, wg_ref[:, f0:f1])
        u = _dot(hb, wu_ref[:, f0:f1])
        a = (g * jax.nn.sigmoid(g) * u).astype(BF16)
        d = _dot(a, wd_ref[f0:f1, :])
        acc = d if acc is None else acc + d
    return x + 0.5 * gt_ref[0] * acc


def _ffn_kernel(x_ref, sh_ref, sc_ref, gt_ref, g_ref, wg_ref, wu_ref, wd_ref, o_ref, *, fc):
    o_ref[0] = _swiglu_residual(x_ref[0], sh_ref, sc_ref, gt_ref, g_ref, wg_ref, wu_ref, wd_ref, fc)


def _mod_spec(sel):
    _, layer, j, row = sel
    if row is None:
        return pl.BlockSpec((None, 1, None, 1, D_MODEL), lambda b, t: (layer, b, j, 0, 0))
    return pl.BlockSpec((None, 1, None, 1, D_MODEL), lambda b, t: (layer, row, j, 0, 0))


def _ffn_half_step(x, shift, scale, gate, gain, wg, wu, wd, idx):
    b, l, d = x.shape
    tm = min(512, l)
    return pl.pallas_call(
        functools.partial(_ffn_kernel, fc=FFN_CHUNK), name="ffn_half_step",
        grid=(b, l // tm),
        in_specs=[
            pl.BlockSpec((1, tm, d), lambda b, t: (b, t, 0)),
            _mod_spec(shift), _mod_spec(scale), _mod_spec(gate),
            _resident((1, d)),
            _layer_slab(wg, idx), _layer_slab(wu, idx), _layer_slab(wd, idx),
        ],
        out_specs=pl.BlockSpec((1, tm, d), lambda b, t: (b, t, 0)),
        out_shape=jax.ShapeDtypeStruct(x.shape, F32),
        compiler_params=_cparams(("parallel", "parallel"), 56),
    )(x, shift[0], scale[0], gate[0], gain.reshape(1, d), wg, wu, wd)


_HY_COLS = 3 * HY_WIDTH
_SWA_Q = SWA_HEADS * HEAD_DIM
_SWA_KV = SWA_KV_HEADS * HEAD_DIM
_NA_W = NA_HEADS * HEAD_DIM
_OFF_SWA = _HY_COLS
_OFF_NA = _OFF_SWA + _SWA_Q + 2 * _SWA_KV
_OFF_GATE = _OFF_NA + 3 * _NA_W


def _inproj_kernel(x_ref, sh_ref, sc_ref, g_ref, cos_ref, sin_ref, hg_ref, bd_ref, w_ref, wkv_ref,
                   zhy_ref, qs_ref, ks_ref, vs_ref, qn_ref, kn_ref, vn_ref, gt_ref):
    hb = _rms_mod(x_ref[0], g_ref[...], sh_ref[0], sc_ref[0]).astype(BF16)
    cos = cos_ref[...]
    sin = sin_ref[...]
    bd = bd_ref[...]
    scale = HEAD_DIM ** -0.5
    proj = lambda lo, n: _dot(hb, w_ref[:, lo:lo + n])
    kvw = 2 * _SWA_KV
    raw = [proj(_OFF_SWA, _SWA_Q), _dot(hb, wkv_ref[:, :kvw]), proj(_OFF_NA, _NA_W), proj(_OFF_NA + _NA_W, _NA_W)]
    gates = proj(_OFF_GATE, 3 * D_MODEL)
    sums = [_head_sumsq(r, bd) for r in raw]
    vs_ref[0] = _dot(hb, wkv_ref[:, kvw:]).astype(BF16)
    vn_ref[0] = proj(_OFF_NA + 2 * _NA_W, _NA_W).astype(BF16)
    zhy = proj(0, _HY_COLS).astype(BF16)
    for c in range(_HY_COLS // HY_CHUNK):
        zhy_ref[0, c] = zhy[:, c * HY_CHUNK:(c + 1) * HY_CHUNK]
    gt_ref[0] = jax.nn.sigmoid(gates).astype(BF16)
    qs, ks, qn, kn = [_head_normalize(r, s, hg_ref[i:i + 1, :]) for i, (r, s) in enumerate(zip(raw, sums))]
    qs_ref[0] = (_rope(qs, cos, sin) * scale).astype(BF16)
    ks_ref[0] = _rope(ks, cos, sin).astype(BF16)
    qn_ref[0] = (qn * scale).astype(BF16)
    kn_ref[0] = kn.astype(BF16)


def _in_projection(x, shift, scale, gain, cos, sin, head_gains, bd, w, w_kv_dup, layer):
    b, l, d = x.shape
    tm = min(512, l)
    kvw = 2 * _SWA_KV
    widths = [_HY_COLS, _SWA_Q, kvw, kvw, _NA_W, _NA_W, _NA_W, 3 * d]
    dtypes = [BF16] * 8
    return pl.pallas_call(
        _inproj_kernel, name="in_projection",
        grid=(b, l // tm),
        in_specs=[
            pl.BlockSpec((1, tm, d), lambda b, t: (b, t, 0)),
            _mod_spec(shift), _mod_spec(scale),
            _resident((1, d)),
            pl.BlockSpec((tm, LANES), lambda b, t: (t, 0)),
            pl.BlockSpec((tm, LANES), lambda b, t: (t, 0)),
            _resident(head_gains.shape), _resident(bd.shape), _layer_slab(w, (layer,)), _layer_slab(w_kv_dup, (layer,)),
        ],
        out_specs=[pl.BlockSpec((1, _HY_COLS // HY_CHUNK, tm, HY_CHUNK), lambda b, t: (b, 0, t, 0))]
                  + [pl.BlockSpec((1, tm, n), lambda b, t: (b, t, 0)) for n in widths[1:]],
        out_shape=[jax.ShapeDtypeStruct((b, _HY_COLS // HY_CHUNK, l, HY_CHUNK), BF16)]
                  + [jax.ShapeDtypeStruct((b, l, n), dt) for n, dt in zip(widths[1:], dtypes[1:])],
        compiler_params=_cparams(("parallel", "parallel"), 56),
    )(x, shift[0], scale[0], gain.reshape(1, d), cos, sin, head_gains, bd, w, w_kv_dup)


def _shortconv_kernel(z0_ref, z1_ref, z2_ref, w_ref, b_ref, o0_ref, o1_ref, o2_ref):
    l = z0_ref.shape[1]
    row = lax.broadcasted_iota(jnp.int32, z0_ref.shape[1:], 0)
    for g, (z_ref, o_ref) in enumerate(((z0_ref, o0_ref), (z1_ref, o1_ref), (z2_ref, o2_ref))):
        z = z_ref[0].astype(F32)
        prev = jnp.where(row == 0, 0.0, pltpu.roll(z, 1, axis=0))
        nxt = jnp.where(row == l - 1, 0.0, pltpu.roll(z, l - 1, axis=0))
        o_ref[0] = w_ref[g, 0:1, :] * prev + w_ref[g, 1:2, :] * z + w_ref[g, 2:3, :] * nxt + b_ref[g]


def _short_conv3(z, w, bias):
    b, _, l, cb = z.shape
    ncb = HY_WIDTH // cb
    wg = w.reshape(3, 3, HY_WIDTH).transpose(1, 0, 2)
    zin = lambda g: pl.BlockSpec((None, 1, l, cb), lambda b, j: (b, g * ncb + j, 0, 0))
    out = pl.BlockSpec((1, l, cb), lambda b, j: (b, 0, j))
    return pl.pallas_call(
        _shortconv_kernel, name="short_conv3",
        grid=(b, ncb),
        in_specs=[zin(0), zin(1), zin(2),
                  pl.BlockSpec((3, 3, cb), lambda b, j: (0, 0, j)),
                  pl.BlockSpec((3, 1, cb), lambda b, j: (0, 0, j))],
        out_specs=[out, out, out],
        out_shape=[jax.ShapeDtypeStruct((b, l, HY_WIDTH), F32)] * 3,
        compiler_params=_cparams(("parallel", "parallel"), 48),
    )(z, z, z, wg, bias.reshape(3, 1, HY_WIDTH))


def _filter_kernel(z_ref, w0, b0, w1, b1, w2, b2, wo, fr_ref, dl_ref, o_ref):
    z = z_ref[...]
    fr = fr_ref[...]
    a = jnp.sin(fr * (_dot3f(z, w0[...]) + b0[...]))
    a = jnp.sin(fr * (_dot3f(a, w1[...]) + b1[...]))
    a = jnp.sin(fr * (_dot3f(a, w2[...]) + b2[...]))
    hh = _dot3f(a, wo[...])
    t = z[:, 0:1]
    win = jnp.exp(-t * dl_ref[...])
    row = lax.broadcasted_iota(jnp.int32, win.shape, 0) + pl.program_id(0) * z.shape[0]
    w = HY_WIDTH
    for o in range(2):
        hp = hh[:, (2 * o) * w:(2 * o + 1) * w] * win
        hn = jnp.where(row == 0, 0.0, hh[:, (2 * o + 1) * w:(2 * o + 2) * w] * win)
        o_ref[0, :, o * w:(o + 1) * w] = hp + hn
        o_ref[1, :, o * w:(o + 1) * w] = hp - hn


def _hyena_filter_sums(zfeat, fp, deltas):
    l = zfeat.shape[0]
    tl = min(256, l)
    w0, b0, w1, b1, w2, b2, wo, fr = fp
    n = 2 * HY_WIDTH
    consts = [w0, b0, w1, b1, w2, b2, wo, fr, deltas]
    return pl.pallas_call(
        _filter_kernel, name="hyena_filter",
        grid=(l // tl,),
        in_specs=[pl.BlockSpec((tl, LANES), lambda t: (t, 0))] + [_resident(c.shape) for c in consts],
        out_specs=pl.BlockSpec((2, tl, n), lambda t: (0, t, 0)),
        out_shape=jax.ShapeDtypeStruct((2, l, n), F32),
        compiler_params=_cparams(("parallel",), 32),
    )(zfeat, *consts)


def _spectrum_kernel(a_ref, d_ref, c_ref, s_ref, kr_ref, ki_ref):
    kr_ref[...] = _dot(c_ref[...], a_ref[0].astype(BF16))
    ki_ref[...] = -_dot(s_ref[...], d_ref[0].astype(BF16))


def _filter_spectrum(ad, tabs):
    _, l, n = ad.shape
    fp = tabs["c_hi"].shape[0]
    fb = tabs["fb"]
    cb = 256
    tab = pl.BlockSpec((fb, l), lambda j, k: (k, 0))
    out = pl.BlockSpec((fb, cb), lambda j, k: (k, j))
    return pl.pallas_call(
        _spectrum_kernel, name="dense_spectrum",
        grid=(n // cb, fp // fb),
        in_specs=[pl.BlockSpec((1, l, cb), lambda j, k: (0, 0, j)),
                  pl.BlockSpec((1, l, cb), lambda j, k: (1, 0, j)), tab, tab],
        out_specs=[out, out],
        out_shape=[jax.ShapeDtypeStruct((fp, n), F32)] * 2,
        compiler_params=_cparams(("parallel", "parallel"), 48),
    )(ad, ad, tabs["c_hi"], tabs["s_hi"])


def _longconv_kernel(u_ref, g_ref, kr_ref, ki_ref, bias_ref, wk_ref, c_ref, s_ref, ct_ref, st_ref,
                     o_ref, ub_ref, acc_ref):
    kb = pl.program_id(2)

    @pl.when(kb == 0)
    def _():
        ub_ref[...] = u_ref[0].astype(BF16)
        acc_ref[...] = jnp.zeros_like(acc_ref)

    ub = ub_ref[...]
    xr = _dot(c_ref[...], ub)
    xi = -_dot(s_ref[...], ub)
    kr = kr_ref[...]
    ki = ki_ref[...]
    wk = wk_ref[...]
    yr = ((xr * kr - xi * ki) * wk).astype(BF16)
    yi = ((xr * ki + xi * kr) * wk).astype(BF16)
    acc_ref[...] += _dot(ct_ref[...], yr) - _dot(st_ref[...], yi)

    @pl.when(kb == pl.num_programs(2) - 1)
    def _():
        u = u_ref[0]
        o_ref[0] = g_ref[0] * (acc_ref[...] + u * bias_ref[...])


def _gated_long_conv(u_arr, g_arr, kr, ki, k_col, bias, tabs):
    b, l, _ = u_arr.shape
    cb = 256
    ncb = HY_WIDTH // cb
    fp = tabs["c_hi"].shape[0]
    fb = tabs["fb"]
    tab = pl.BlockSpec((fb, l), lambda b, j, k: (k, 0))
    tabt = pl.BlockSpec((l, fb), lambda b, j, k: (0, k))
    spec = pl.BlockSpec((fb, cb), lambda b, j, k: (k, k_col * ncb + j))
    return pl.pallas_call(
        _longconv_kernel, name="dense_long_conv",
        grid=(b, ncb, fp // fb),
        in_specs=[
            pl.BlockSpec((1, l, cb), lambda b, j, k: (b, 0, j)),
            pl.BlockSpec((1, l, cb), lambda b, j, k: (b, 0, j)),
            spec, spec,
            pl.BlockSpec((1, cb), lambda b, j, k: (0, j)),
            pl.BlockSpec((fb, 1), lambda b, j, k: (k, 0)),
            tab, tab, tabt, tabt,
        ],
        out_specs=pl.BlockSpec((1, l, cb), lambda b, j, k: (b, 0, j)),
        out_shape=jax.ShapeDtypeStruct((b, l, HY_WIDTH), F32),
        scratch_shapes=[pltpu.VMEM((l, cb), BF16), pltpu.VMEM((l, cb), F32)],
        compiler_params=_cparams(("parallel", "parallel", "arbitrary"), 56),
    )(u_arr, g_arr, kr, ki, bias.reshape(1, HY_WIDTH), tabs["wk"],
      tabs["c_hi"], tabs["s_hi"], tabs["ct_hi"], tabs["st_hi"])


FFT_INNER = 64
FFT_COLS = HY_CHUNK
FFT_K2_CHUNK = 16
FFT_UNROLL = 16


def _fft_tables(l):
    n = 2 * l
    n1 = FFT_INNER
    n2 = n // n1
    h = n2 // 2
    ang = lambda idx, mod: (idx % mod).astype(F32) * (2.0 * math.pi / mod)
    k2 = jnp.arange(n2, dtype=jnp.int32)
    m2 = jnp.arange(h, dtype=jnp.int32)
    a1 = ang(k2[:, None] * m2[None, :], n2)
    c1, s1 = jnp.cos(a1), jnp.sin(a1)
    f1 = jnp.block([[c1, s1], [-s1, c1]])
    k1 = jnp.arange(n1, dtype=jnp.int32)
    a2 = ang(k1[None, None, :] * (k1[None, :, None] * n2 + k2[:, None, None]), n)
    c2, s2 = jnp.cos(a2), jnp.sin(a2)
    m = jnp.concatenate([jnp.concatenate([c2, s2], axis=2), jnp.concatenate([-s2, c2], axis=2)], axis=1)
    g = jnp.block([[c1.T, -s1.T], [s1.T, c1.T]]) * (1.0 / n)
    out = dict(n2=n2)
    for name, t in (("f1", f1), ("f1r", f1[:, :h]), ("m", m), ("mt", jnp.swapaxes(m, 1, 2)), ("g", g)):
        out[name + "_hi"] = t.astype(BF16)
    return out


def _mm(w_hi, w_lo, x):
    if w_lo is None:
        return _dot(w_hi, x.astype(BF16))
    hi, lo = _split(x)
    return _dot3(w_hi, w_lo, hi, lo)


def _row_block_matmul(w_hi, w_lo, src_ref, dst_ref, count):
    m, k = w_hi.shape

    def body(i, carry):
        x = src_ref[pl.ds(pl.multiple_of(i * k, k), k), :]
        dst_ref[pl.ds(pl.multiple_of(i * m, m), m), :] = _mm(w_hi, w_lo, x).astype(dst_ref.dtype)
        return carry
    lax.fori_loop(0, count, body, 0, unroll=FFT_UNROLL)


def _short_conv_rows(z, w_ref, b_ref):
    l = z.shape[0]
    row = lax.broadcasted_iota(jnp.int32, z.shape, 0)
    prev = jnp.where(row == 0, 0.0, pltpu.roll(z, 1, axis=0))
    nxt = jnp.where(row == l - 1, 0.0, pltpu.roll(z, l - 1, axis=0))
    return w_ref[0:1, :] * prev + w_ref[1:2, :] * z + w_ref[2:3, :] * nxt + b_ref[...]


def _fftconv_kernel(u_ref, g_ref, uw_ref, ub_ref, gw_ref, gb_ref, kr_ref, ki_ref, bias_ref,
                    f1_ref, m_ref, mt_ref, gm_ref, o_ref, z_scr, y_scr, p_scr, q_scr, u_scr, *, u_conv):
    kc = pl.program_id(2)
    n1 = FFT_INNER
    cc = z_scr.shape[1]
    n2 = z_scr.shape[0] // n1
    h = n2 // 2

    @pl.when(kc == 0)
    def _stage1():
        z = []
        for s in range(2):
            u = u_ref[s, 0]
            if u_conv:
                u = _short_conv_rows(u.astype(F32), uw_ref, ub_ref)
                u_scr[s] = u
            z.append(jnp.swapaxes(u.astype(BF16).reshape(h, n1, cc), 0, 1))
        z_scr[...] = jnp.concatenate(z, axis=1).reshape(n1 * n2, cc)
        _row_block_matmul(f1_ref[...], None, z_scr, p_scr, n1)
        q_scr[...] = jnp.swapaxes(p_scr[...].reshape(n1, 2 * n2, cc), 0, 1).reshape(2 * n2 * n1, cc)

    ks = kr_ref.shape[0]
    rows = [(pl.multiple_of((kc * ks + j) * n1, n1), pl.multiple_of((n2 + kc * ks + j) * n1, n1)) for j in range(ks)]
    xs = [_mm(m_ref[j], None, jnp.concatenate([q_scr[pl.ds(r0, n1), :], q_scr[pl.ds(i0, n1), :]], axis=0))
          for j, (r0, i0) in enumerate(rows)]
    ys = []
    for j, x in enumerate(xs):
        xr, xi = x[:n1], x[n1:]
        kr, ki = kr_ref[j], ki_ref[j]
        ys.append(jnp.concatenate([xr * kr - xi * ki, xr * ki + xi * kr], axis=0))
    for j, (r0, i0) in enumerate(rows):
        b = _mm(mt_ref[j], None, ys[j]).astype(p_scr.dtype)
        p_scr[pl.ds(r0, n1), :] = b[:n1]
        p_scr[pl.ds(i0, n1), :] = b[n1:]

    @pl.when(kc == pl.num_programs(2) - 1)
    def _stage1_inv():
        q_scr[...] = jnp.swapaxes(p_scr[...].reshape(2 * n2, n1, cc), 0, 1).reshape(2 * n2 * n1, cc)
        _row_block_matmul(gm_ref[...], None, q_scr, y_scr, n1)
        y3 = y_scr[...].reshape(n1, n2, cc)
        for s in range(2):
            y = jnp.swapaxes(y3[:, s * h:(s + 1) * h, :], 0, 1).reshape(h * n1, cc)
            gate = _short_conv_rows(g_ref[s, 0].astype(F32), gw_ref, gb_ref)
            u = u_scr[s] if u_conv else u_ref[s, 0]
            o_ref[s, 0] = (gate * (y + u * bias_ref[...])).astype(o_ref.dtype)


def _fft_gated_long_conv(u, u_col, z_hy, g_col, short_w, short_b, kr, ki, k_col, bias, ft, out_dtype):
    b, _, l, cc = z_hy.shape
    c = HY_WIDTH
    n1, n2, ks = FFT_INNER, ft["n2"], FFT_K2_CHUNK
    p, ncc = b // 2, c // cc
    u_conv = u is None
    if u_conv:
        u = z_hy
    tok = lambda col: pl.BlockSpec((2, 1, None, l, cc), lambda q, j, k: (0, q, col * ncc + j, 0, 0))
    taps = lambda col: pl.BlockSpec((3, cc), lambda q, j, k: (0, col * ncc + j))
    row = lambda col: pl.BlockSpec((1, cc), lambda q, j, k: (0, col * ncc + j))
    filt = pl.BlockSpec((ks, n1, cc), lambda q, j, k: (k, 0, k_col * ncc + j))
    tab = pl.BlockSpec((ks, 2 * n1, 2 * n1), lambda q, j, k: (k, 0, 0))
    res = lambda a: pl.BlockSpec(a.shape, lambda q, j, k: (0,) * a.ndim)
    sb = short_b.reshape(1, -1)
    y = pl.pallas_call(
        functools.partial(_fftconv_kernel, u_conv=u_conv), name="fft_long_conv",
        grid=(p, ncc, n2 // ks),
        in_specs=[tok(u_col), tok(g_col), taps(u_col), row(u_col), taps(g_col), row(g_col), filt, filt, row(0),
                  res(ft["f1_hi"]), tab, tab, res(ft["g_hi"])],
        out_specs=tok(0),
        out_shape=jax.ShapeDtypeStruct((2, p, ncc, l, cc), out_dtype),
        scratch_shapes=[pltpu.VMEM((n1 * n2, cc), BF16), pltpu.VMEM((n1 * n2, cc), F32),
                        pltpu.VMEM((2 * n2 * n1, cc), BF16), pltpu.VMEM((2 * n2 * n1, cc), BF16),
                        pltpu.VMEM((2, l, cc) if u_conv else (2, 8, LANES), F32)],
        compiler_params=_cparams(("parallel", "parallel", "arbitrary"), 56),
    )(u.reshape((2, p) + u.shape[1:]), z_hy.reshape((2, p) + z_hy.shape[1:]), short_w, sb, short_w, sb, kr, ki,
      bias.reshape(1, c), ft["f1_hi"], ft["m_hi"], ft["mt_hi"], ft["g_hi"])
    return y.reshape(b, ncc, l, cc)


def _fftspec_kernel(a_ref, d_ref, f1_ref, m_ref, kr_ref, ki_ref, z_scr, p_scr, q_scr):
    n1 = FFT_INNER
    cc = z_scr.shape[1]
    h = z_scr.shape[0] // n1
    n2 = 2 * h
    for src_ref, out_ref, lo in ((a_ref, kr_ref, 0), (d_ref, ki_ref, n1)):
        z_scr[...] = jnp.swapaxes(src_ref[0].reshape(h, n1, cc), 0, 1).reshape(n1 * h, cc)
        _row_block_matmul(f1_ref[...], None, z_scr, p_scr, n1)
        q_scr[...] = jnp.swapaxes(p_scr[...].reshape(n1, 2 * n2, cc), 0, 1).reshape(2 * n2 * n1, cc)

        def body(k2, carry):
            r0 = pl.multiple_of(k2 * n1, n1)
            i0 = pl.multiple_of((n2 + k2) * n1, n1)
            slab = jnp.concatenate([q_scr[pl.ds(r0, n1), :], q_scr[pl.ds(i0, n1), :]], axis=0)
            out_ref[k2] = _mm(m_ref[k2][lo:lo + n1], None, slab)
            return carry
        lax.fori_loop(0, n2, body, 0, unroll=FFT_UNROLL)


def _fft_filter_spectrum(ad, ft):
    _, l, n = ad.shape
    n1, n2, cc = FFT_INNER, ft["n2"], FFT_COLS
    h = n2 // 2
    res = lambda a: pl.BlockSpec(a.shape, lambda j: (0,) * a.ndim)
    out = pl.BlockSpec((n2, n1, cc), lambda j: (0, 0, j))
    tabs = [ft["f1r_hi"], ft["m_hi"]]
    return pl.pallas_call(
        _fftspec_kernel, name="fft_filter_spectrum",
        grid=(n // cc,),
        in_specs=[pl.BlockSpec((1, l, cc), lambda j: (0, 0, j)), pl.BlockSpec((1, l, cc), lambda j: (1, 0, j))]
                 + [res(t) for t in tabs],
        out_specs=[out, out],
        out_shape=[jax.ShapeDtypeStruct((n2, n1, n), F32)] * 2,
        scratch_shapes=[pltpu.VMEM((n1 * h, cc), F32), pltpu.VMEM((2 * n2 * n1, cc), F32),
                        pltpu.VMEM((2 * n2 * n1, cc), F32)],
        compiler_params=_cparams(("parallel",), 56),
    )(ad, ad, *tabs)


def _softmax_weights(s, sink=None):
    m = jnp.max(s, axis=-1, keepdims=True)
    if sink is not None:
        m = jnp.maximum(m, sink)
    p = jnp.exp(s - m)
    den = jnp.sum(p, axis=-1, keepdims=True)
    if sink is not None:
        den = den + jnp.exp(sink - m)
    return p.astype(BF16), den


def _softmax_pv(s, v, sink=None):
    p, den = _softmax_weights(s, sink)
    return _dot(p, v) / den


def _qk(q, k):
    return lax.dot_general(q, k, (((1,), (1,)), ((), ())), preferred_element_type=F32)


def _half_mask(q_tile, half):
    lane = lax.broadcasted_iota(jnp.int32, q_tile.shape, 1)
    keep = (lane < HEAD_DIM) if half == 0 else (lane >= HEAD_DIM)
    return jnp.where(keep, q_tile, jnp.zeros_like(q_tile))


def _merge_halves(o_even, o_odd):
    lane = lax.broadcasted_iota(jnp.int32, o_even.shape, 1)
    return jnp.where(lane < HEAD_DIM, o_even, o_odd)


def _swa_kernel(sink_ref, q_ref, kp_ref, kc_ref, kn_ref, vp_ref, vc_ref, vn_ref, kx_ref, vx_ref, o_ref):
    n = pl.program_id(1)
    nb = pl.num_programs(1)
    blk = SWA_BLOCK
    g = SWA_HEADS // SWA_KV_HEADS
    n_ctx = kx_ref.shape[1]
    rows = g * blk
    ri = lax.broadcasted_iota(jnp.int32, (rows, 3 * blk + n_ctx), 0) & (blk - 1)
    ci = lax.broadcasted_iota(jnp.int32, (rows, 3 * blk + n_ctx), 1)
    ok_prev = (ci < blk) & (ci >= ri) & (n > 0)
    ok_cur = (ci >= blk) & (ci < 2 * blk)
    ok_next = (ci >= 2 * blk) & (ci < 3 * blk) & (ci - 2 * blk <= ri) & (n < nb - 1)
    ok = ok_prev | ok_cur | ok_next | (ci >= 3 * blk)
    bias = jnp.where(ok, 0.0, NEG_BIG)
    rsel = lax.broadcasted_iota(jnp.int32, (rows, 1), 0) >> (blk.bit_length() - 1)
    outs = [None] * SWA_HEADS
    kvs = range(SWA_KV_HEADS)
    lanes = [slice(kv * LANES, (kv + 1) * LANES) for kv in kvs]
    scores, sinks = [], []
    for kv in kvs:
        cs = lanes[kv]
        k = jnp.concatenate([kp_ref[0, :, cs], kc_ref[0, :, cs], kn_ref[0, :, cs], kx_ref[0, :, cs]], axis=0)
        q_parts = []
        sink = jnp.zeros((rows, 1), F32)
        for gi in range(g):
            h = kv * g + gi
            q_parts.append(_half_mask(q_ref[0, :, (h // 2) * LANES:(h // 2 + 1) * LANES], h % 2))
            sink = jnp.where(rsel == gi, sink_ref[h], sink)
        scores.append(_qk(jnp.concatenate(q_parts, axis=0), k))
        sinks.append(sink)
    probs = [_softmax_weights(scores[kv] + bias, sinks[kv]) for kv in kvs]
    for kv in kvs:
        cs = lanes[kv]
        v = jnp.concatenate([vp_ref[0, :, cs], vc_ref[0, :, cs], vn_ref[0, :, cs], vx_ref[0, :, cs]], axis=0)
        pw, den = probs[kv]
        o = _dot(pw, v) / den
        for gi in range(g):
            outs[kv * g + gi] = o[gi * blk:(gi + 1) * blk]
    for p in range(SWA_HEADS // 2):
        o_ref[0, :, p * LANES:(p + 1) * LANES] = _merge_halves(outs[2 * p], outs[2 * p + 1]).astype(o_ref.dtype)


def _window_gqa(q, k, v, kx, vx, sink):
    b, l, _ = q.shape
    nb = l // SWA_BLOCK
    n_ctx = kx.shape[1]
    kvw = k.shape[2]

    def blk(off):
        return pl.BlockSpec((1, SWA_BLOCK, kvw), lambda b, n: (b, jnp.clip(n + off, 0, nb - 1), 0))

    ctx = pl.BlockSpec((1, n_ctx, kvw), lambda b, n: (b, 0, 0))
    return pl.pallas_call(
        _swa_kernel, name="window_gqa",
        grid=(b, nb),
        in_specs=[
            pl.BlockSpec(memory_space=pltpu.SMEM),
            pl.BlockSpec((1, SWA_BLOCK, q.shape[2]), lambda b, n: (b, n, 0)),
            blk(-1), blk(0), blk(1), blk(-1), blk(0), blk(1), ctx, ctx,
        ],
        out_specs=pl.BlockSpec((1, SWA_BLOCK, q.shape[2]), lambda b, n: (b, n, 0)),
        out_shape=jax.ShapeDtypeStruct(q.shape, BF16),
        compiler_params=_cparams(("parallel", "parallel"), 32),
    )(sink, q, k, k, k, v, v, v, kx, vx)


def _na_bias_kernel(rpb_ref, o_ref):
    h = pl.program_id(0)
    nd = 2 * NA_COLS - 1
    nr = 2 * NA_MAX_ROWS - 1
    q = lax.broadcasted_iota(jnp.int32, (GRID_W, LANES), 0)
    lane = lax.broadcasted_iota(jnp.int32, (GRID_W, LANES), 1)
    kc = lane & (GRID_W - 1)
    upper = lane >= GRID_W
    cstart = jnp.clip(q - NA_COLS // 2, 0, GRID_W - NA_COLS)
    valid = (kc >= cstart) & (kc < cstart + NA_COLS)
    dc = jnp.clip(kc - q, -(NA_COLS - 1), NA_COLS - 1) + (NA_COLS - 1)
    tiles = []
    for dr in range(nr - 1):
        base0 = (h * nr + dr) * nd
        base1 = base0 + nd
        t = jnp.zeros((GRID_W, LANES), F32)
        for d in range(nd):
            val = jnp.where(upper, rpb_ref[base1 + d], rpb_ref[base0 + d])
            t = jnp.where(dc == d, val, t)
        tiles.append(jnp.where(valid, t, NEG_BIG))
    for off in range(NA_MAX_ROWS):
        for j in range(NA_MAX_ROWS // 2):
            o_ref[off, 0, :, j * LANES:(j + 1) * LANES] = tiles[2 * j - off + NA_MAX_ROWS - 1]


def _na_bias_table(rpb):
    return pl.pallas_call(
        _na_bias_kernel, name="na_bias_table",
        grid=(NA_HEADS,),
        in_specs=[pl.BlockSpec(memory_space=pltpu.SMEM)],
        out_specs=pl.BlockSpec((NA_MAX_ROWS, 1, GRID_W, NA_MAX_ROWS * GRID_W), lambda h: (0, h, 0, 0)),
        out_shape=jax.ShapeDtypeStruct((NA_MAX_ROWS, NA_HEADS, GRID_W, NA_MAX_ROWS * GRID_W), F32),
        compiler_params=_cparams(("arbitrary",), 16),
    )(rpb.reshape(-1))


NA_ROWS_PER_STEP = 4


def _na_kernel(q_ref, k_ref, v_ref, kx_ref, vx_ref, bias_ref, o_ref, *, rows):
    wr = NA_MAX_ROWS
    nloc = wr * GRID_W
    pairs = range(NA_HEADS // 2)
    lanes = [slice(p * LANES, (p + 1) * LANES) for p in pairs]
    for i in range(NA_ROWS_PER_STEP):
        r = pl.program_id(1) * NA_ROWS_PER_STEP + i
        first = jnp.clip(r - wr // 2, 0, rows - wr)
        start = pl.multiple_of(first * GRID_W, GRID_W)
        off = r - first
        qr = slice(i * GRID_W, (i + 1) * GRID_W)
        scores = []
        for p in pairs:
            q_tile = q_ref[0, qr, lanes[p]]
            q2 = jnp.concatenate([_half_mask(q_tile, 0), _half_mask(q_tile, 1)], axis=0)
            k = jnp.concatenate([k_ref[0, pl.ds(start, nloc), lanes[p]], kx_ref[0, :, lanes[p]]], axis=0)
            scores.append(_qk(q2, k))
        probs = []
        for p in pairs:
            s = scores[p]
            bias = jnp.concatenate([bias_ref[off, 2 * p], bias_ref[off, 2 * p + 1]], axis=0)
            probs.append(_softmax_weights(jnp.concatenate([s[:, :nloc] + bias, s[:, nloc:]], axis=1)))
        for p in pairs:
            pw, den = probs[p]
            v = jnp.concatenate([v_ref[0, pl.ds(start, nloc), lanes[p]], vx_ref[0, :, lanes[p]]], axis=0)
            o = _dot(pw, v) / den
            o_ref[0, qr, lanes[p]] = _merge_halves(o[:GRID_W], o[GRID_W:]).astype(o_ref.dtype)


def _neighbourhood_attention(q, k, v, kx, vx, bias_tab):
    b, l, w = q.shape
    rows = l // GRID_W
    n_ctx = kx.shape[1]
    tq = NA_ROWS_PER_STEP * GRID_W
    full = pl.BlockSpec((1, l, w), lambda b, r: (b, 0, 0))
    ctx = pl.BlockSpec((1, n_ctx, w), lambda b, r: (b, 0, 0))
    return pl.pallas_call(
        functools.partial(_na_kernel, rows=rows), name="neighbourhood_attention",
        grid=(b, rows // NA_ROWS_PER_STEP),
        in_specs=[pl.BlockSpec((1, tq, w), lambda b, r: (b, r, 0)), full, full, ctx, ctx,
                  _resident(bias_tab.shape)],
        out_specs=pl.BlockSpec((1, tq, w), lambda b, r: (b, r, 0)),
        out_shape=jax.ShapeDtypeStruct(q.shape, BF16),
        compiler_params=_cparams(("parallel", "arbitrary"), 48),
    )(q, k, v, kx, vx, bias_tab)


def _ctx_attn_kernel(sink_ref, qs_ref, ks_ref, vs_ref, qn_ref, kn_ref, vn_ref, os_ref, on_ref):
    n = qs_ref.shape[1]
    g = SWA_HEADS // SWA_KV_HEADS
    rsel = lax.broadcasted_iota(jnp.int32, (g * n, 1), 0) >> (n.bit_length() - 1)
    outs = [None] * SWA_HEADS
    for kv in range(SWA_KV_HEADS):
        cs = slice(kv * LANES, (kv + 1) * LANES)
        q_parts = []
        sink = jnp.zeros((g * n, 1), F32)
        for gi in range(g):
            h = kv * g + gi
            q_parts.append(_half_mask(qs_ref[0, :, (h // 2) * LANES:(h // 2 + 1) * LANES], h % 2))
            sink = jnp.where(rsel == gi, sink_ref[h], sink)
        o = _softmax_pv(_qk(jnp.concatenate(q_parts, axis=0), ks_ref[0, :, cs]), vs_ref[0, :, cs], sink)
        for gi in range(g):
            outs[kv * g + gi] = o[gi * n:(gi + 1) * n]
    for p in range(SWA_HEADS // 2):
        os_ref[0, :, p * LANES:(p + 1) * LANES] = _merge_halves(outs[2 * p], outs[2 * p + 1]).astype(os_ref.dtype)
    for p in range(NA_HEADS // 2):
        cs = slice(p * LANES, (p + 1) * LANES)
        q_tile = qn_ref[0, :, cs]
        q2 = jnp.concatenate([_half_mask(q_tile, 0), _half_mask(q_tile, 1)], axis=0)
        o = _softmax_pv(_qk(q2, kn_ref[0, :, cs]), vn_ref[0, :, cs])
        on_ref[0, :, cs] = _merge_halves(o[:n], o[n:]).astype(on_ref.dtype)


def _context_attention(qs, ks, vs, qn, kn, vn, sink):
    b, n, w = qs.shape
    spec = lambda a: pl.BlockSpec((1,) + a.shape[1:], lambda b: (b, 0, 0))
    return pl.pallas_call(
        _ctx_attn_kernel, name="context_attention",
        grid=(b,),
        in_specs=[pl.BlockSpec(memory_space=pltpu.SMEM)] + [spec(a) for a in (qs, ks, vs, qn, kn, vn)],
        out_specs=[spec(qs), spec(qn)],
        out_shape=[jax.ShapeDtypeStruct(qs.shape, BF16), jax.ShapeDtypeStruct(qn.shape, BF16)],
        compiler_params=_cparams(("parallel",), 32),
    )(sink, qs, ks, vs, qn, kn, vn)


def _merge_ffn_kernel(x_ref, mg_ref, yh_ref, ys_ref, yn_ref, zg_ref, wb_ref, wo_ref,
                      sh_ref, sc_ref, gt_ref, g_ref, wg_ref, wu_ref, wd_ref, o_ref, *, fc):
    d = D_MODEL
    yh = jnp.concatenate([yh_ref[0, c] for c in range(yh_ref.shape[1])], axis=1)
    ys = (yh.astype(BF16), ys_ref[0], yn_ref[0])
    m = None
    for n in range(3):
        proj = _dot(ys[n], wb_ref[n])
        term = zg_ref[0, :, n * d:(n + 1) * d].astype(F32) * proj
        m = term if m is None else m + term
    x = x_ref[0] + mg_ref[0] * _dot(m.astype(BF16), wo_ref[...])
    o_ref[0] = _swiglu_residual(x, sh_ref, sc_ref, gt_ref, g_ref, wg_ref, wu_ref, wd_ref, fc)


def _merge_and_ffn(x, mix_gate, y_hy, y_swa, y_na, gates, w_branch, w_out, layer,
                   shift, scale, gate, gain, wg, wu, wd):
    b, l, d = x.shape
    tm = min(512, l)
    tok = lambda n: pl.BlockSpec((1, tm, n), lambda b, t: (b, t, 0))
    idx = (layer, 1)
    return pl.pallas_call(
        functools.partial(_merge_ffn_kernel, fc=FFN_CHUNK), name="merge_ffn",
        grid=(b, l // tm),
        in_specs=[tok(d), _mod_spec(mix_gate),
                  pl.BlockSpec((1, HY_WIDTH // HY_CHUNK, tm, HY_CHUNK), lambda b, t: (b, 0, t, 0)),
                  tok(HY_WIDTH), tok(HY_WIDTH), tok(3 * d),
                  _layer_slab(w_branch, (layer,)), _layer_slab(w_out, (layer,)),
                  _mod_spec(shift), _mod_spec(scale), _mod_spec(gate), _resident((1, d)),
                  _layer_slab(wg, idx), _layer_slab(wu, idx), _layer_slab(wd, idx)],
        out_specs=tok(d),
        out_shape=jax.ShapeDtypeStruct(x.shape, F32),
        compiler_params=_cparams(("parallel", "parallel"), 58),
    )(x, mix_gate[0], y_hy, y_swa, y_na, gates, w_branch, w_out, shift[0], scale[0], gate[0], gain.reshape(1, d),
      wg, wu, wd)


def _dft_tables(l):
    n = 2 * l
    fb = 256 if l >= 1024 else LANES * ((l + 1 + LANES - 1) // LANES)
    fp = fb * ((l + 1 + fb - 1) // fb)
    k = jnp.arange(fp, dtype=jnp.int32)[:, None]
    t = jnp.arange(l, dtype=jnp.int32)[None, :]
    ang = ((k * t) % n).astype(F32) * (2.0 * math.pi / n)
    live = k <= l
    c = jnp.where(live, jnp.cos(ang), 0.0)
    s = jnp.where(live, jnp.sin(ang), 0.0)
    c_hi = c.astype(BF16)
    s_hi = s.astype(BF16)
    wk = jnp.where((k == 0) | (k == l), 1.0, 2.0) * jnp.where(live, 1.0 / n, 0.0)
    return dict(fb=fb, c_hi=c_hi, s_hi=s_hi, ct_hi=c_hi.T, st_hi=s_hi.T, wk=wk.astype(F32))


def _filter_features(l):
    t = jnp.linspace(0.0, 1.0, l, dtype=F32)[:, None]
    w = (2.0 * math.pi / l) * jnp.arange(l, dtype=F32)[:, None]
    f = jnp.linspace(1e-4, HY_BANDS - 1, HY_BANDS, dtype=F32)[None, :]
    z = jnp.concatenate([t, jnp.cos(f * w), -jnp.sin(f * w)], axis=-1)
    return jnp.pad(z, ((0, 0), (0, LANES - HY_EMB)))


def _decay_rates():
    max_decay = math.log(HY_TARGET) / HY_FAST_DECAY
    min_decay = math.log(HY_TARGET) / HY_SLOW_DECAY
    return jnp.abs(jnp.linspace(min_decay, max_decay, HY_WIDTH, dtype=F32))[None, :]


def _rope_tables(l):
    pos = jnp.arange(l)
    row = (pos // GRID_W).astype(F32)
    col = (pos % GRID_W).astype(F32)
    half = HEAD_DIM // 2
    inv = 1.0 / (ROPE_BASE ** (jnp.arange(0, half, 2, dtype=F32) / half))
    ar = row[:, None] * inv[None, :]
    ac = col[:, None] * inv[None, :]
    cos = jnp.concatenate([jnp.cos(ar), jnp.cos(ar), jnp.cos(ac), jnp.cos(ac)], axis=-1)
    sin = jnp.concatenate([-jnp.sin(ar), jnp.sin(ar), -jnp.sin(ac), jnp.sin(ac)], axis=-1)
    return jnp.tile(cos, (1, 2)), jnp.tile(sin, (1, 2))


def _pad_to(a, shape):
    return jnp.pad(a, [(0, s - d) for d, s in zip(a.shape, shape)])


def _use_fft(bsz, length):
    return bsz % 2 == 0 and length % (8 * FFT_INNER) == 0


def _conv_tables(bsz, length):
    return _fft_tables(length) if _use_fft(bsz, length) else _dft_tables(length)


def _hyena_branch(z_hy, short_w, short_b, feat, fparams, deltas, bias, tabs):
    ad = _hyena_filter_sums(feat, fparams, deltas)
    if "n2" in tabs:
        kr, ki = _fft_filter_spectrum(ad, tabs)
        y1 = _fft_gated_long_conv(None, 0, z_hy, 1, short_w, short_b, kr, ki, 0, bias[0], tabs, F32)
        return _fft_gated_long_conv(y1, 0, z_hy, 2, short_w, short_b, kr, ki, 1, bias[1], tabs, BF16)
    v, x1, x2 = _short_conv3(z_hy, short_w, short_b)
    kr, ki = _filter_spectrum(ad, tabs)
    conv = functools.partial(_gated_long_conv, tabs=tabs)
    y1 = conv(v, x1, kr, ki, 0, bias[0])
    y = conv(y1, x2, kr, ki, 1, bias[1])
    bsz, length, _ = y.shape
    return y.reshape(bsz, length, HY_WIDTH // HY_CHUNK, HY_CHUNK).transpose(0, 2, 1, 3)


def kernel(x, c, ctx, c_ctx, w_ada, b_ada, norm_g, ffn_w_gate, ffn_w_up, ffn_w_down,
           w_in, hy_short_w, hy_short_b, hy_pe_w0, hy_pe_b0, hy_pe_w1, hy_pe_b1,
           hy_pe_w2, hy_pe_b2, hy_pe_wout, hy_sin_freq, hy_bias,
           swa_q_gain, swa_k_gain, swa_sink, na_q_gain, na_k_gain, na_rpb,
           w_branch, w_out):
    bsz, seq, d = x.shape
    n_ctx = ctx.shape[1]
    depth = w_ada.shape[0]

    c16 = _pad_to(jnp.concatenate([c, c_ctx[None, :]], axis=0), (MOD_ROWS, d))
    mods = _adaln_mods(c16, w_ada, b_ada).reshape(depth, MOD_ROWS, N_MOD, 1, d)

    tabs_x = _conv_tables(bsz, seq)
    tabs_c = _conv_tables(bsz, n_ctx)
    feat_x = _filter_features(seq)
    feat_c = _filter_features(n_ctx)
    deltas = _decay_rates()
    cos_x, sin_x = _rope_tables(seq)
    cos_c = jnp.ones((n_ctx, LANES), F32)
    sin_c = jnp.zeros((n_ctx, LANES), F32)
    eye = jnp.arange(MXU_DIM) // HEAD_DIM
    bd = (eye[:, None] == eye[None, :]).astype(BF16)

    wg = ffn_w_gate.astype(BF16)
    wu = ffn_w_up.astype(BF16)
    wd = ffn_w_down.astype(BF16)
    wb = w_branch.astype(BF16)
    wo = w_out.astype(BF16)
    win = w_in.astype(BF16)
    kv = win[:, :, _OFF_SWA + _SWA_Q:_OFF_NA].reshape(depth, d, 2, SWA_KV_HEADS, 1, HEAD_DIM)
    win_kv = jnp.broadcast_to(kv, (depth, d, 2, SWA_KV_HEADS, 2, HEAD_DIM)).reshape(depth, d, 4 * _SWA_KV)

    xc = ctx
    for i in range(depth):
        last = i == depth - 1
        mx = lambda j: (mods, i, j, None)
        mc = lambda j: (mods, i, j, bsz)
        tile2 = lambda g: jnp.tile(g, 2 * MXU_DIM // LANES)
        head_gains = jnp.stack([tile2(swa_q_gain[i]), tile2(swa_k_gain[i]), tile2(na_q_gain[i]), tile2(na_k_gain[i])])
        fparams = (_pad_to(hy_pe_w0[i], (LANES, LANES)), _pad_to(hy_pe_b0[i][None], (1, LANES)),
                   _pad_to(hy_pe_w1[i], (LANES, LANES)), _pad_to(hy_pe_b1[i][None], (1, LANES)),
                   _pad_to(hy_pe_w2[i], (LANES, LANES)), _pad_to(hy_pe_b2[i][None], (1, LANES)),
                   _pad_to(hy_pe_wout[i], (LANES, 4 * HY_WIDTH)), _pad_to(hy_sin_freq[i][None], (1, LANES)))

        x = _ffn_half_step(x, mx(0), mx(1), mx(2), norm_g[i, 0], wg, wu, wd, (i, 0))
        xc = _ffn_half_step(xc, mc(0), mc(1), mc(2), norm_g[i, 0], wg, wu, wd, (i, 0))

        z_hy, q_s, k_s, v_s, q_n, k_n, v_n, gates = _in_projection(
            x, mx(3), mx(4), norm_g[i, 1], cos_x, sin_x, head_gains, bd, win, win_kv, i)
        zc_hy, qc_s, kc_s, vc_s, qc_n, kc_n, vc_n, gates_c = _in_projection(
            xc, mc(3), mc(4), norm_g[i, 1], cos_c, sin_c, head_gains, bd, win, win_kv, i)

        y_hy = _hyena_branch(z_hy, hy_short_w[i], hy_short_b[i], feat_x, fparams, deltas, hy_bias[i], tabs_x)
        y_swa = _window_gqa(q_s, k_s, v_s, kc_s, vc_s, swa_sink[i])
        y_na = _neighbourhood_attention(q_n, k_n, v_n, kc_n, vc_n, _na_bias_table(na_rpb[i]))
        x = _merge_and_ffn(x, mx(5), y_hy, y_swa, y_na, gates, wb, wo, i,
                           mx(6), mx(7), mx(8), norm_g[i, 2], wg, wu, wd)

        if not last:
            yc_hy = _hyena_branch(zc_hy, hy_short_w[i], hy_short_b[i], feat_c, fparams, deltas, hy_bias[i], tabs_c)
            yc_swa, yc_na = _context_attention(qc_s, kc_s, vc_s, qc_n, kc_n, vc_n, swa_sink[i])
            xc = _merge_and_ffn(xc, mc(5), yc_hy, yc_swa, yc_na, gates_c, wb, wo, i,
                                mc(6), mc(7), mc(8), norm_g[i, 2], wg, wu, wd)
    return x
```

```python
import functools
import math

import jax
import jax.numpy as jnp
from jax import lax
from jax.experimental import pallas as pl
from jax.experimental.pallas import tpu as pltpu

F32 = jnp.float32
BF16 = jnp.bfloat16

D_MODEL = 1024
DEPTH = 4
GRID_W = 64
HEAD_DIM = 64
N_MOD = 9
RMS_EPS = 1e-6

HY_WIDTH = D_MODEL // 2
HY_CHUNK = 256
HY_EMB = 33
HY_BANDS = (HY_EMB - 1) // 2
HY_FFN = 64
HY_FAST_DECAY = 0.3
HY_SLOW_DECAY = 1.5
HY_TARGET = 1e-2

SWA_HEADS = 8
SWA_KV_HEADS = 2
SWA_WINDOW = 128
SWA_BLOCK = 128
ROPE_BASE = 10000.0

NA_HEADS = 8
NA_MAX_ROWS = 8
NA_COLS = 16
NA_COL_BLOCK = 16

FFN_HIDDEN = 256 * ((8 * D_MODEL // 3 + 255) // 256)
FFN_CHUNK = 1536

LANES = 128
MXU_DIM = 256
MOD_ROWS = 16
NEG_BIG = -1e30


def _cparams(sem, vmem_mb):
    return pltpu.CompilerParams(dimension_semantics=sem, vmem_limit_bytes=vmem_mb * 1024 * 1024)


def _resident(shape):
    nd = len(shape)
    return pl.BlockSpec(shape, lambda *_: (0,) * nd, pipeline_mode=pl.Buffered(1))


def _layer_slab(arr, idx):
    rest = arr.shape[len(idx):]
    return pl.BlockSpec((None,) * len(idx) + rest, lambda *_: tuple(idx) + (0,) * len(rest),
                        pipeline_mode=pl.Buffered(1))


def _split(x):
    hi = x.astype(BF16)
    lo = (x - hi.astype(F32)).astype(BF16)
    return hi, lo


def _dot(a, b):
    return jnp.dot(a, b, preferred_element_type=F32)


def _dot3(a_hi, a_lo, b_hi, b_lo):
    return _dot(a_hi, b_hi) + _dot(a_lo, b_hi) + _dot(a_hi, b_lo)


def _dot3f(a, b):
    a_hi, a_lo = _split(a)
    b_hi, b_lo = _split(b)
    return _dot3(a_hi, a_lo, b_hi, b_lo)


def _rms_mod(x, gain, shift, scale):
    ms = jnp.mean(x * x, axis=-1, keepdims=True)
    y = x * lax.rsqrt(ms + RMS_EPS) * gain
    return y * (1.0 + scale) + shift


def _head_sumsq(x, bd):
    return [_dot(jnp.square(x[:, j * MXU_DIM:(j + 1) * MXU_DIM]).astype(BF16), bd) for j in range(x.shape[1] // MXU_DIM)]


def _head_normalize(x, sumsq, gain):
    outs = [x[:, j * MXU_DIM:(j + 1) * MXU_DIM] * lax.rsqrt(s * (1.0 / HEAD_DIM) + RMS_EPS) * gain
            for j, s in enumerate(sumsq)]
    return outs[0] if len(outs) == 1 else jnp.concatenate(outs, axis=1)


def _rope(x, cos, sin):
    lane = lax.broadcasted_iota(jnp.int32, (x.shape[0], LANES), 1)
    first = (lane & 31) < 16
    outs = []
    for j in range(x.shape[1] // LANES):
        xc = x[:, j * LANES:(j + 1) * LANES]
        partner = jnp.where(first, pltpu.roll(xc, LANES - 16, axis=1), pltpu.roll(xc, 16, axis=1))
        outs.append(xc * cos + partner * sin)
    return outs[0] if len(outs) == 1 else jnp.concatenate(outs, axis=1)


def _mods_kernel(c_ref, w_ref, b_ref, o_ref):
    c = c_ref[...]
    a = c * jax.nn.sigmoid(c)
    o_ref[0] = _dot3f(a, w_ref[0]) + b_ref[0]


def _adaln_mods(c16, w_ada, b_ada):
    depth, d, n = w_ada.shape
    nb = 1152
    return pl.pallas_call(
        _mods_kernel, name="adaln_mods",
        grid=(depth, n // nb),
        in_specs=[
            pl.BlockSpec((MOD_ROWS, d), lambda i, j: (0, 0)),
            pl.BlockSpec((1, d, nb), lambda i, j: (i, 0, j)),
            pl.BlockSpec((1, 1, nb), lambda i, j: (i, 0, j)),
        ],
        out_specs=pl.BlockSpec((1, MOD_ROWS, nb), lambda i, j: (i, 0, j)),
        out_shape=jax.ShapeDtypeStruct((depth, MOD_ROWS, n), F32),
        compiler_params=_cparams(("arbitrary", "arbitrary"), 48),
    )(c16, w_ada, b_ada.reshape(depth, 1, n))


def _swiglu_residual(x, sh_ref, sc_ref, gt_ref, g_ref, wg_ref, wu_ref, wd_ref, fc):
    hb = _rms_mod(x, g_ref[...], sh_ref[0], sc_ref[0]).astype(BF16)
    acc = None
    for f0 in range(0, FFN_HIDDEN, fc):
        f1 = min(f0 + fc, FFN_HIDDEN)
        g = _dot(hb, wg_ref[:, f0:f1])
        u = _dot(hb, wu_ref[:, f0:f1])
        a = (g * jax.nn.sigmoid(g) * u).astype(BF16)
        d = _dot(a, wd_ref[f0:f1, :])
        acc = d if acc is None else acc + d
    return x + 0.5 * gt_ref[0] * acc


def _ffn_kernel(x_ref, sh_ref, sc_ref, gt_ref, g_ref, wg_ref, wu_ref, wd_ref, o_ref, *, fc):
    o_ref[0] = _swiglu_residual(x_ref[0], sh_ref, sc_ref, gt_ref, g_ref, wg_ref, wu_ref, wd_ref, fc)


def _mod_spec(sel):
    _, layer, j, row = sel
    if row is None:
        return pl.BlockSpec((None, 1, None, 1, D_MODEL), lambda b, t: (layer, b, j, 0, 0))
    return pl.BlockSpec((None, 1, None, 1, D_MODEL), lambda b, t: (layer, row, j, 0, 0))


def _ffn_half_step(x, shift, scale, gate, gain, wg, wu, wd, idx):
    b, l, d = x.shape
    tm = min(512, l)
    return pl.pallas_call(
        functools.partial(_ffn_kernel, fc=FFN_CHUNK), name="ffn_half_step",
        grid=(b, l // tm),
        in_specs=[
            pl.BlockSpec((1, tm, d), lambda b, t: (b, t, 0)),
            _mod_spec(shift), _mod_spec(scale), _mod_spec(gate),
            _resident((1, d)),
            _layer_slab(wg, idx), _layer_slab(wu, idx), _layer_slab(wd, idx),
        ],
        out_specs=pl.BlockSpec((1, tm, d), lambda b, t: (b, t, 0)),
        out_shape=jax.ShapeDtypeStruct(x.shape, F32),
        compiler_params=_cparams(("parallel", "parallel"), 56),
    )(x, shift[0], scale[0], gate[0], gain.reshape(1, d), wg, wu, wd)


_HY_COLS = 3 * HY_WIDTH
_SWA_Q = SWA_HEADS * HEAD_DIM
_SWA_KV = SWA_KV_HEADS * HEAD_DIM
_NA_W = NA_HEADS * HEAD_DIM
_OFF_SWA = _HY_COLS
_OFF_NA = _OFF_SWA + _SWA_Q + 2 * _SWA_KV
_OFF_GATE = _OFF_NA + 3 * _NA_W


def _inproj_kernel(x_ref, sh_ref, sc_ref, g_ref, cos_ref, sin_ref, hg_ref, bd_ref, w_ref, wkv_ref,
                   zhy_ref, qs_ref, ks_ref, vs_ref, qn_ref, kn_ref, vn_ref, gt_ref):
    hb = _rms_mod(x_ref[0], g_ref[...], sh_ref[0], sc_ref[0]).astype(BF16)
    cos = cos_ref[...]
    sin = sin_ref[...]
    bd = bd_ref[...]
    scale = HEAD_DIM ** -0.5
    proj = lambda lo, n: _dot(hb, w_ref[:, lo:lo + n])
    kvw = 2 * _SWA_KV
    raw = [proj(_OFF_SWA, _SWA_Q), _dot(hb, wkv_ref[:, :kvw]), proj(_OFF_NA, _NA_W), proj(_OFF_NA + _NA_W, _NA_W)]
    gates = proj(_OFF_GATE, 3 * D_MODEL)
    sums = [_head_sumsq(r, bd) for r in raw]
    vs_ref[0] = _dot(hb, wkv_ref[:, kvw:]).astype(BF16)
    vn_ref[0] = proj(_OFF_NA + 2 * _NA_W, _NA_W).astype(BF16)
    zhy = proj(0, _HY_COLS).astype(BF16)
    for c in range(_HY_COLS // HY_CHUNK):
        zhy_ref[0, c] = zhy[:, c * HY_CHUNK:(c + 1) * HY_CHUNK]
    gt_ref[0] = jax.nn.sigmoid(gates).astype(BF16)
    qs, ks, qn, kn = [_head_normalize(r, s, hg_ref[i:i + 1, :]) for i, (r, s) in enumerate(zip(raw, sums))]
    qs_ref[0] = (_rope(qs, cos, sin) * scale).astype(BF16)
    ks_ref[0] = _rope(ks, cos, sin).astype(BF16)
    qn_ref[0] = (qn * scale).astype(BF16)
    kn_ref[0] = kn.astype(BF16)


def _in_projection(x, shift, scale, gain, cos, sin, head_gains, bd, w, w_kv_dup, layer):
    b, l, d = x.shape
    tm = min(512, l)
    kvw = 2 * _SWA_KV
    widths = [_HY_COLS, _SWA_Q, kvw, kvw, _NA_W, _NA_W, _NA_W, 3 * d]
    dtypes = [BF16] * 8
    return pl.pallas_call(
        _inproj_kernel, name="in_projection",
        grid=(b, l // tm),
        in_specs=[
            pl.BlockSpec((1, tm, d), lambda b, t: (b, t, 0)),
            _mod_spec(shift), _mod_spec(scale),
            _resident((1, d)),
            pl.BlockSpec((tm, LANES), lambda b, t: (t, 0)),
            pl.BlockSpec((tm, LANES), lambda b, t: (t, 0)),
            _resident(head_gains.shape), _resident(bd.shape), _layer_slab(w, (layer,)), _layer_slab(w_kv_dup, (layer,)),
        ],
        out_specs=[pl.BlockSpec((1, _HY_COLS // HY_CHUNK, tm, HY_CHUNK), lambda b, t: (b, 0, t, 0))]
                  + [pl.BlockSpec((1, tm, n), lambda b, t: (b, t, 0)) for n in widths[1:]],
        out_shape=[jax.ShapeDtypeStruct((b, _HY_COLS // HY_CHUNK, l, HY_CHUNK), BF16)]
                  + [jax.ShapeDtypeStruct((b, l, n), dt) for n, dt in zip(widths[1:], dtypes[1:])],
        compiler_params=_cparams(("parallel", "parallel"), 56),
    )(x, shift[0], scale[0], gain.reshape(1, d), cos, sin, head_gains, bd, w, w_kv_dup)


def _shortconv_kernel(z0_ref, z1_ref, z2_ref, w_ref, b_ref, o0_ref, o1_ref, o2_ref):
    l = z0_ref.shape[1]
    row = lax.broadcasted_iota(jnp.int32, z0_ref.shape[1:], 0)
    for g, (z_ref, o_ref) in enumerate(((z0_ref, o0_ref), (z1_ref, o1_ref), (z2_ref, o2_ref))):
        z = z_ref[0].astype(F32)
        prev = jnp.where(row == 0, 0.0, pltpu.roll(z, 1, axis=0))
        nxt = jnp.where(row == l - 1, 0.0, pltpu.roll(z, l - 1, axis=0))
        o_ref[0] = w_ref[g, 0:1, :] * prev + w_ref[g, 1:2, :] * z + w_ref[g, 2:3, :] * nxt + b_ref[g]


def _short_conv3(z, w, bias):
    b, _, l, cb = z.shape
    ncb = HY_WIDTH // cb
    wg = w.reshape(3, 3, HY_WIDTH).transpose(1, 0, 2)
    zin = lambda g: pl.BlockSpec((None, 1, l, cb), lambda b, j: (b, g * ncb + j, 0, 0))
    out = pl.BlockSpec((1, l, cb), lambda b, j: (b, 0, j))
    return pl.pallas_call(
        _shortconv_kernel, name="short_conv3",
        grid=(b, ncb),
        in_specs=[zin(0), zin(1), zin(2),
                  pl.BlockSpec((3, 3, cb), lambda b, j: (0, 0, j)),
                  pl.BlockSpec((3, 1, cb), lambda b, j: (0, 0, j))],
        out_specs=[out, out, out],
        out_shape=[jax.ShapeDtypeStruct((b, l, HY_WIDTH), F32)] * 3,
        compiler_params=_cparams(("parallel", "parallel"), 48),
    )(z, z, z, wg, bias.reshape(3, 1, HY_WIDTH))


def _filter_kernel(z_ref, w0, b0, w1, b1, w2, b2, wo, fr_ref, dl_ref, o_ref):
    z = z_ref[...]
    fr = fr_ref[...]
    a = jnp.sin(fr * (_dot3f(z, w0[...]) + b0[...]))
    a = jnp.sin(fr * (_dot3f(a, w1[...]) + b1[...]))
    a = jnp.sin(fr * (_dot3f(a, w2[...]) + b2[...]))
    hh = _dot3f(a, wo[...])
    t = z[:, 0:1]
    win = jnp.exp(-t * dl_ref[...])
    row = lax.broadcasted_iota(jnp.int32, win.shape, 0) + pl.program_id(0) * z.shape[0]
    w = HY_WIDTH
    for o in range(2):
        hp = hh[:, (2 * o) * w:(2 * o + 1) * w] * win
        hn = jnp.where(row == 0, 0.0, hh[:, (2 * o + 1) * w:(2 * o + 2) * w] * win)
        o_ref[0, :, o * w:(o + 1) * w] = hp + hn
        o_ref[1, :, o * w:(o + 1) * w] = hp - hn


def _hyena_filter_sums(zfeat, fp, deltas):
    l = zfeat.shape[0]
    tl = min(256, l)
    w0, b0, w1, b1, w2, b2, wo, fr = fp
    n = 2 * HY_WIDTH
    consts = [w0, b0, w1, b1, w2, b2, wo, fr, deltas]
    return pl.pallas_call(
        _filter_kernel, name="hyena_filter",
        grid=(l // tl,),
        in_specs=[pl.BlockSpec((tl, LANES), lambda t: (t, 0))] + [_resident(c.shape) for c in consts],
        out_specs=pl.BlockSpec((2, tl, n), lambda t: (0, t, 0)),
        out_shape=jax.ShapeDtypeStruct((2, l, n), F32),
        compiler_params=_cparams(("parallel",), 32),
    )(zfeat, *consts)


def _spectrum_kernel(a_ref, d_ref, c_ref, s_ref, kr_ref, ki_ref):
    kr_ref[...] = _dot(c_ref[...], a_ref[0].astype(BF16))
    ki_ref[...] = -_dot(s_ref[...], d_ref[0].astype(BF16))


def _filter_spectrum(ad, tabs):
    _, l, n = ad.shape
    fp = tabs["c_hi"].shape[0]
    fb = tabs["fb"]
    cb = 256
    tab = pl.BlockSpec((fb, l), lambda j, k: (k, 0))
    out = pl.BlockSpec((fb, cb), lambda j, k: (k, j))
    return pl.pallas_call(
        _spectrum_kernel, name="dense_spectrum",
        grid=(n // cb, fp // fb),
        in_specs=[pl.BlockSpec((1, l, cb), lambda j, k: (0, 0, j)),
                  pl.BlockSpec((1, l, cb), lambda j, k: (1, 0, j)), tab, tab],
        out_specs=[out, out],
        out_shape=[jax.ShapeDtypeStruct((fp, n), F32)] * 2,
        compiler_params=_cparams(("parallel", "parallel"), 48),
    )(ad, ad, tabs["c_hi"], tabs["s_hi"])


def _longconv_kernel(u_ref, g_ref, kr_ref, ki_ref, bias_ref, wk_ref, c_ref, s_ref, ct_ref, st_ref,
                     o_ref, ub_ref, acc_ref):
    kb = pl.program_id(2)

    @pl.when(kb == 0)
    def _():
        ub_ref[...] = u_ref[0].astype(BF16)
        acc_ref[...] = jnp.zeros_like(acc_ref)

    ub = ub_ref[...]
    xr = _dot(c_ref[...], ub)
    xi = -_dot(s_ref[...], ub)
    kr = kr_ref[...]
    ki = ki_ref[...]
    wk = wk_ref[...]
    yr = ((xr * kr - xi * ki) * wk).astype(BF16)
    yi = ((xr * ki + xi * kr) * wk).astype(BF16)
    acc_ref[...] += _dot(ct_ref[...], yr) - _dot(st_ref[...], yi)

    @pl.when(kb == pl.num_programs(2) - 1)
    def _():
        u = u_ref[0]
        o_ref[0] = g_ref[0] * (acc_ref[...] + u * bias_ref[...])


def _gated_long_conv(u_arr, g_arr, kr, ki, k_col, bias, tabs):
    b, l, _ = u_arr.shape
    cb = 256
    ncb = HY_WIDTH // cb
    fp = tabs["c_hi"].shape[0]
    fb = tabs["fb"]
    tab = pl.BlockSpec((fb, l), lambda b, j, k: (k, 0))
    tabt = pl.BlockSpec((l, fb), lambda b, j, k: (0, k))
    spec = pl.BlockSpec((fb, cb), lambda b, j, k: (k, k_col * ncb + j))
    return pl.pallas_call(
        _longconv_kernel, name="dense_long_conv",
        grid=(b, ncb, fp // fb),
        in_specs=[
            pl.BlockSpec((1, l, cb), lambda b, j, k: (b, 0, j)),
            pl.BlockSpec((1, l, cb), lambda b, j, k: (b, 0, j)),
            spec, spec,
            pl.BlockSpec((1, cb), lambda b, j, k: (0, j)),
            pl.BlockSpec((fb, 1), lambda b, j, k: (k, 0)),
            tab, tab, tabt, tabt,
        ],
        out_specs=pl.BlockSpec((1, l, cb), lambda b, j, k: (b, 0, j)),
        out_shape=jax.ShapeDtypeStruct((b, l, HY_WIDTH), F32),
        scratch_shapes=[pltpu.VMEM((l, cb), BF16), pltpu.VMEM((l, cb), F32)],
        compiler_params=_cparams(("parallel", "parallel", "arbitrary"), 56),
    )(u_arr, g_arr, kr, ki, bias.reshape(1, HY_WIDTH), tabs["wk"],
      tabs["c_hi"], tabs["s_hi"], tabs["ct_hi"], tabs["st_hi"])


FFT_INNER = 64
FFT_COLS = HY_CHUNK
FFT_K2_CHUNK = 16
FFT_UNROLL = 16


def _fft_tables(l):
    n = 2 * l
    n1 = FFT_INNER
    n2 = n // n1
    h = n2 // 2
    ang = lambda idx, mod: (idx % mod).astype(F32) * (2.0 * math.pi / mod)
    k2 = jnp.arange(n2, dtype=jnp.int32)
    m2 = jnp.arange(h, dtype=jnp.int32)
    a1 = ang(k2[:, None] * m2[None, :], n2)
    c1, s1 = jnp.cos(a1), jnp.sin(a1)
    f1 = jnp.block([[c1, s1], [-s1, c1]])
    k1 = jnp.arange(n1, dtype=jnp.int32)
    a2 = ang(k1[None, None, :] * (k1[None, :, None] * n2 + k2[:, None, None]), n)
    c2, s2 = jnp.cos(a2), jnp.sin(a2)
    m = jnp.concatenate([jnp.concatenate([c2, s2], axis=2), jnp.concatenate([-s2, c2], axis=2)], axis=1)
    g = jnp.block([[c1.T, -s1.T], [s1.T, c1.T]]) * (1.0 / n)
    out = dict(n2=n2)
    for name, t in (("f1", f1), ("f1r", f1[:, :h]), ("m", m), ("mt", jnp.swapaxes(m, 1, 2)), ("g", g)):
        out[name + "_hi"] = t.astype(BF16)
    return out


def _mm(w_hi, w_lo, x):
    if w_lo is None:
        return _dot(w_hi, x.astype(BF16))
    hi, lo = _split(x)
    return _dot3(w_hi, w_lo, hi, lo)


def _row_block_matmul(w_hi, w_lo, src_ref, dst_ref, count):
    m, k = w_hi.shape

    def body(i, carry):
        x = src_ref[pl.ds(pl.multiple_of(i * k, k), k), :]
        dst_ref[pl.ds(pl.multiple_of(i * m, m), m), :] = _mm(w_hi, w_lo, x).astype(dst_ref.dtype)
        return carry
    lax.fori_loop(0, count, body, 0, unroll=FFT_UNROLL)


def _short_conv_rows(z, w_ref, b_ref):
    l = z.shape[0]
    row = lax.broadcasted_iota(jnp.int32, z.shape, 0)
    prev = jnp.where(row == 0, 0.0, pltpu.roll(z, 1, axis=0))
    nxt = jnp.where(row == l - 1, 0.0, pltpu.roll(z, l - 1, axis=0))
    return w_ref[0:1, :] * prev + w_ref[1:2, :] * z + w_ref[2:3, :] * nxt + b_ref[...]


def _fftconv_kernel(u_ref, g_ref, uw_ref, ub_ref, gw_ref, gb_ref, kr_ref, ki_ref, bias_ref,
                    f1_ref, m_ref, mt_ref, gm_ref, o_ref, z_scr, y_scr, p_scr, q_scr, u_scr, *, u_conv):
    kc = pl.program_id(2)
    n1 = FFT_INNER
    cc = z_scr.shape[1]
    n2 = z_scr.shape[0] // n1
    h = n2 // 2

    @pl.when(kc == 0)
    def _stage1():
        z = []
        for s in range(2):
            u = u_ref[s, 0]
            if u_conv:
                u = _short_conv_rows(u.astype(F32), uw_ref, ub_ref)
                u_scr[s] = u
            z.append(jnp.swapaxes(u.astype(BF16).reshape(h, n1, cc), 0, 1))
        z_scr[...] = jnp.concatenate(z, axis=1).reshape(n1 * n2, cc)
        _row_block_matmul(f1_ref[...], None, z_scr, p_scr, n1)
        q_scr[...] = jnp.swapaxes(p_scr[...].reshape(n1, 2 * n2, cc), 0, 1).reshape(2 * n2 * n1, cc)

    ks = kr_ref.shape[0]
    rows = [(pl.multiple_of((kc * ks + j) * n1, n1), pl.multiple_of((n2 + kc * ks + j) * n1, n1)) for j in range(ks)]
    xs = [_mm(m_ref[j], None, jnp.concatenate([q_scr[pl.ds(r0, n1), :], q_scr[pl.ds(i0, n1), :]], axis=0))
          for j, (r0, i0) in enumerate(rows)]
    ys = []
    for j, x in enumerate(xs):
        xr, xi = x[:n1], x[n1:]
        kr, ki = kr_ref[j], ki_ref[j]
        ys.append(jnp.concatenate([xr * kr - xi * ki, xr * ki + xi * kr], axis=0))
    for j, (r0, i0) in enumerate(rows):
        b = _mm(mt_ref[j], None, ys[j]).astype(p_scr.dtype)
        p_scr[pl.ds(r0, n1), :] = b[:n1]
        p_scr[pl.ds(i0, n1), :] = b[n1:]

    @pl.when(kc == pl.num_programs(2) - 1)
    def _stage1_inv():
        q_scr[...] = jnp.swapaxes(p_scr[...].reshape(2 * n2, n1, cc), 0, 1).reshape(2 * n2 * n1, cc)
        _row_block_matmul(gm_ref[...], None, q_scr, y_scr, n1)
        y3 = y_scr[...].reshape(n1, n2, cc)
        for s in range(2):
            y = jnp.swapaxes(y3[:, s * h:(s + 1) * h, :], 0, 1).reshape(h * n1, cc)
            gate = _short_conv_rows(g_ref[s, 0].astype(F32), gw_ref, gb_ref)
            u = u_scr[s] if u_conv else u_ref[s, 0]
            o_ref[s, 0] = (gate * (y + u * bias_ref[...])).astype(o_ref.dtype)


def _fft_gated_long_conv(u, u_col, z_hy, g_col, short_w, short_b, kr, ki, k_col, bias, ft, out_dtype):
    b, _, l, cc = z_hy.shape
    c = HY_WIDTH
    n1, n2, ks = FFT_INNER, ft["n2"], FFT_K2_CHUNK
    p, ncc = b // 2, c // cc
    u_conv = u is None
    if u_conv:
        u = z_hy
    tok = lambda col: pl.BlockSpec((2, 1, None, l, cc), lambda q, j, k: (0, q, col * ncc + j, 0, 0))
    taps = lambda col: pl.BlockSpec((3, cc), lambda q, j, k: (0, col * ncc + j))
    row = lambda col: pl.BlockSpec((1, cc), lambda q, j, k: (0, col * ncc + j))
    filt = pl.BlockSpec((ks, n1, cc), lambda q, j, k: (k, 0, k_col * ncc + j))
    tab = pl.BlockSpec((ks, 2 * n1, 2 * n1), lambda q, j, k: (k, 0, 0))
    res = lambda a: pl.BlockSpec(a.shape, lambda q, j, k: (0,) * a.ndim)
    sb = short_b.reshape(1, -1)
    y = pl.pallas_call(
        functools.partial(_fftconv_kernel, u_conv=u_conv), name="fft_long_conv",
        grid=(p, ncc, n2 // ks),
        in_specs=[tok(u_col), tok(g_col), taps(u_col), row(u_col), taps(g_col), row(g_col), filt, filt, row(0),
                  res(ft["f1_hi"]), tab, tab, res(ft["g_hi"])],
        out_specs=tok(0),
        out_shape=jax.ShapeDtypeStruct((2, p, ncc, l, cc), out_dtype),
        scratch_shapes=[pltpu.VMEM((n1 * n2, cc), BF16), pltpu.VMEM((n1 * n2, cc), F32),
                        pltpu.VMEM((2 * n2 * n1, cc), BF16), pltpu.VMEM((2 * n2 * n1, cc), BF16),
                        pltpu.VMEM((2, l, cc) if u_conv else (2, 8, LANES), F32)],
        compiler_params=_cparams(("parallel", "parallel", "arbitrary"), 56),
    )(u.reshape((2, p) + u.shape[1:]), z_hy.reshape((2, p) + z_hy.shape[1:]), short_w, sb, short_w, sb, kr, ki,
      bias.reshape(1, c), ft["f1_hi"], ft["m_hi"], ft["mt_hi"], ft["g_hi"])
    return y.reshape(b, ncc, l, cc)


def _fftspec_kernel(a_ref, d_ref, f1_ref, m_ref, kr_ref, ki_ref, z_scr, p_scr, q_scr):
    n1 = FFT_INNER
    cc = z_scr.shape[1]
    h = z_scr.shape[0] // n1
    n2 = 2 * h
    for src_ref, out_ref, lo in ((a_ref, kr_ref, 0), (d_ref, ki_ref, n1)):
        z_scr[...] = jnp.swapaxes(src_ref[0].astype(BF16).reshape(h, n1, cc), 0, 1).reshape(n1 * h, cc)
        _row_block_matmul(f1_ref[...], None, z_scr, p_scr, n1)
        q_scr[...] = jnp.swapaxes(p_scr[...].reshape(n1, 2 * n2, cc), 0, 1).reshape(2 * n2 * n1, cc)

        def body(k2, carry):
            r0 = pl.multiple_of(k2 * n1, n1)
            i0 = pl.multiple_of((n2 + k2) * n1, n1)
            slab = jnp.concatenate([q_scr[pl.ds(r0, n1), :], q_scr[pl.ds(i0, n1), :]], axis=0)
            out_ref[k2] = _mm(m_ref[k2][lo:lo + n1], None, slab)
            return carry
        lax.fori_loop(0, n2, body, 0, unroll=FFT_UNROLL)


def _fft_filter_spectrum(ad, ft):
    _, l, n = ad.shape
    n1, n2, cc = FFT_INNER, ft["n2"], FFT_COLS
    h = n2 // 2
    res = lambda a: pl.BlockSpec(a.shape, lambda j: (0,) * a.ndim)
    out = pl.BlockSpec((n2, n1, cc), lambda j: (0, 0, j))
    tabs = [ft["f1r_hi"], ft["m_hi"]]
    return pl.pallas_call(
        _fftspec_kernel, name="fft_filter_spectrum",
        grid=(n // cc,),
        in_specs=[pl.BlockSpec((1, l, cc), lambda j: (0, 0, j)), pl.BlockSpec((1, l, cc), lambda j: (1, 0, j))]
                 + [res(t) for t in tabs],
        out_specs=[out, out],
        out_shape=[jax.ShapeDtypeStruct((n2, n1, n), F32)] * 2,
        scratch_shapes=[pltpu.VMEM((n1 * h, cc), BF16), pltpu.VMEM((2 * n2 * n1, cc), BF16),
                        pltpu.VMEM((2 * n2 * n1, cc), BF16)],
        compiler_params=_cparams(("parallel",), 56),
    )(ad, ad, *tabs)


def _softmax_weights(s, sink=None):
    m = jnp.max(s, axis=-1, keepdims=True)
    if sink is not None:
        m = jnp.maximum(m, sink)
    p = jnp.exp(s - m)
    den = jnp.sum(p, axis=-1, keepdims=True)
    if sink is not None:
        den = den + jnp.exp(sink - m)
    return p.astype(BF16), den


def _softmax_pv(s, v, sink=None):
    p, den = _softmax_weights(s, sink)
    return _dot(p, v) / den


def _qk(q, k):
    return lax.dot_general(q, k, (((1,), (1,)), ((), ())), preferred_element_type=F32)


def _half_mask(q_tile, half):
    lane = lax.broadcasted_iota(jnp.int32, q_tile.shape, 1)
    keep = (lane < HEAD_DIM) if half == 0 else (lane >= HEAD_DIM)
    return jnp.where(keep, q_tile, jnp.zeros_like(q_tile))


def _merge_halves(o_even, o_odd):
    lane = lax.broadcasted_iota(jnp.int32, o_even.shape, 1)
    return jnp.where(lane < HEAD_DIM, o_even, o_odd)


SWA_CHAIN_HEADS = 2

def _swa_kernel(sink_ref, q_ref, kp_ref, kc_ref, kn_ref, vp_ref, vc_ref, vn_ref, kx_ref, vx_ref, o_ref):
    n = pl.program_id(1)
    nb = pl.num_programs(1)
    blk = SWA_BLOCK
    g = SWA_CHAIN_HEADS
    n_ctx = kx_ref.shape[1]
    rows = g * blk
    ri = lax.broadcasted_iota(jnp.int32, (rows, 3 * blk + n_ctx), 0) & (blk - 1)
    ci = lax.broadcasted_iota(jnp.int32, (rows, 3 * blk + n_ctx), 1)
    ok_prev = (ci < blk) & (ci >= ri) & (n > 0)
    ok_cur = (ci >= blk) & (ci < 2 * blk)
    ok_next = (ci >= 2 * blk) & (ci < 3 * blk) & (ci - 2 * blk <= ri) & (n < nb - 1)
    ok = ok_prev | ok_cur | ok_next | (ci >= 3 * blk)
    bias = jnp.where(ok, 0.0, NEG_BIG)
    rsel = lax.broadcasted_iota(jnp.int32, (rows, 1), 0) >> (blk.bit_length() - 1)
    outs = [None] * SWA_HEADS
    chains = range(SWA_HEADS // g)
    group = SWA_HEADS // SWA_KV_HEADS
    lanes = [slice((c * g // group) * LANES, (c * g // group + 1) * LANES) for c in chains]
    scores, sinks = [], []
    for c in chains:
        cs = lanes[c]
        k = jnp.concatenate([kp_ref[0, :, cs], kc_ref[0, :, cs], kn_ref[0, :, cs], kx_ref[0, :, cs]], axis=0)
        q_parts = []
        sink = jnp.zeros((rows, 1), F32)
        for gi in range(g):
            h = c * g + gi
            q_parts.append(_half_mask(q_ref[0, :, (h // 2) * LANES:(h // 2 + 1) * LANES], h % 2))
            sink = jnp.where(rsel == gi, sink_ref[h], sink)
        scores.append(_qk(jnp.concatenate(q_parts, axis=0), k))
        sinks.append(sink)
    probs = [_softmax_weights(scores[c] + bias, sinks[c]) for c in chains]
    for c in chains:
        cs = lanes[c]
        v = jnp.concatenate([vp_ref[0, :, cs], vc_ref[0, :, cs], vn_ref[0, :, cs], vx_ref[0, :, cs]], axis=0)
        pw, den = probs[c]
        o = _dot(pw, v) / den
        for gi in range(g):
            outs[c * g + gi] = o[gi * blk:(gi + 1) * blk]
    for p in range(SWA_HEADS // 2):
        o_ref[0, :, p * LANES:(p + 1) * LANES] = _merge_halves(outs[2 * p], outs[2 * p + 1]).astype(o_ref.dtype)


def _window_gqa(q, k, v, kx, vx, sink):
    b, l, _ = q.shape
    nb = l // SWA_BLOCK
    n_ctx = kx.shape[1]
    kvw = k.shape[2]

    def blk(off):
        return pl.BlockSpec((1, SWA_BLOCK, kvw), lambda b, n: (b, jnp.clip(n + off, 0, nb - 1), 0))

    ctx = pl.BlockSpec((1, n_ctx, kvw), lambda b, n: (b, 0, 0))
    return pl.pallas_call(
        _swa_kernel, name="window_gqa",
        grid=(b, nb),
        in_specs=[
            pl.BlockSpec(memory_space=pltpu.SMEM),
            pl.BlockSpec((1, SWA_BLOCK, q.shape[2]), lambda b, n: (b, n, 0)),
            blk(-1), blk(0), blk(1), blk(-1), blk(0), blk(1), ctx, ctx,
        ],
        out_specs=pl.BlockSpec((1, SWA_BLOCK, q.shape[2]), lambda b, n: (b, n, 0)),
        out_shape=jax.ShapeDtypeStruct(q.shape, BF16),
        compiler_params=_cparams(("parallel", "parallel"), 32),
    )(sink, q, k, k, k, v, v, v, kx, vx)


def _na_bias_kernel(rpb_ref, o_ref):
    h = pl.program_id(0)
    nd = 2 * NA_COLS - 1
    nr = 2 * NA_MAX_ROWS - 1
    q = lax.broadcasted_iota(jnp.int32, (GRID_W, LANES), 0)
    lane = lax.broadcasted_iota(jnp.int32, (GRID_W, LANES), 1)
    kc = lane & (GRID_W - 1)
    upper = lane >= GRID_W
    cstart = jnp.clip(q - NA_COLS // 2, 0, GRID_W - NA_COLS)
    valid = (kc >= cstart) & (kc < cstart + NA_COLS)
    dc = jnp.clip(kc - q, -(NA_COLS - 1), NA_COLS - 1) + (NA_COLS - 1)
    tiles = []
    for dr in range(nr - 1):
        base0 = (h * nr + dr) * nd
        base1 = base0 + nd
        t = jnp.zeros((GRID_W, LANES), F32)
        for d in range(nd):
            val = jnp.where(upper, rpb_ref[base1 + d], rpb_ref[base0 + d])
            t = jnp.where(dc == d, val, t)
        tiles.append(jnp.where(valid, t, NEG_BIG))
    for off in range(NA_MAX_ROWS):
        for j in range(NA_MAX_ROWS // 2):
            o_ref[off, 0, :, j * LANES:(j + 1) * LANES] = tiles[2 * j - off + NA_MAX_ROWS - 1]


def _na_bias_table(rpb):
    return pl.pallas_call(
        _na_bias_kernel, name="na_bias_table",
        grid=(NA_HEADS,),
        in_specs=[pl.BlockSpec(memory_space=pltpu.SMEM)],
        out_specs=pl.BlockSpec((NA_MAX_ROWS, 1, GRID_W, NA_MAX_ROWS * GRID_W), lambda h: (0, h, 0, 0)),
        out_shape=jax.ShapeDtypeStruct((NA_MAX_ROWS, NA_HEADS, GRID_W, NA_MAX_ROWS * GRID_W), F32),
        compiler_params=_cparams(("arbitrary",), 16),
    )(rpb.reshape(-1))


NA_ROWS_PER_STEP = 4


def _na_kernel(q_ref, k_ref, v_ref, kx_ref, vx_ref, bias_ref, o_ref, *, rows):
    wr = NA_MAX_ROWS
    nloc = wr * GRID_W
    pairs = range(NA_HEADS // 2)
    lanes = [slice(p * LANES, (p + 1) * LANES) for p in pairs]
    for i in range(NA_ROWS_PER_STEP):
        r = pl.program_id(1) * NA_ROWS_PER_STEP + i
        first = jnp.clip(r - wr // 2, 0, rows - wr)
        start = pl.multiple_of(first * GRID_W, GRID_W)
        off = r - first
        qr = slice(i * GRID_W, (i + 1) * GRID_W)
        scores = []
        for p in pairs:
            q_tile = q_ref[0, qr, lanes[p]]
            q2 = jnp.concatenate([_half_mask(q_tile, 0), _half_mask(q_tile, 1)], axis=0)
            k = jnp.concatenate([k_ref[0, pl.ds(start, nloc), lanes[p]], kx_ref[0, :, lanes[p]]], axis=0)
            scores.append(_qk(q2, k))
        probs = []
        for p in pairs:
            s = scores[p]
            bias = jnp.concatenate([bias_ref[off, 2 * p], bias_ref[off, 2 * p + 1]], axis=0)
            probs.append(_softmax_weights(jnp.concatenate([s[:, :nloc] + bias, s[:, nloc:]], axis=1)))
        for p in pairs:
            pw, den = probs[p]
            v = jnp.concatenate([v_ref[0, pl.ds(start, nloc), lanes[p]], vx_ref[0, :, lanes[p]]], axis=0)
            o = _dot(pw, v) / den
            o_ref[0, qr, lanes[p]] = _merge_halves(o[:GRID_W], o[GRID_W:]).astype(o_ref.dtype)


def _neighbourhood_attention(q, k, v, kx, vx, bias_tab):
    b, l, w = q.shape
    rows = l // GRID_W
    n_ctx = kx.shape[1]
    tq = NA_ROWS_PER_STEP * GRID_W
    full = pl.BlockSpec((1, l, w), lambda b, r: (b, 0, 0))
    ctx = pl.BlockSpec((1, n_ctx, w), lambda b, r: (b, 0, 0))
    return pl.pallas_call(
        functools.partial(_na_kernel, rows=rows), name="neighbourhood_attention",
        grid=(b, rows // NA_ROWS_PER_STEP),
        in_specs=[pl.BlockSpec((1, tq, w), lambda b, r: (b, r, 0)), full, full, ctx, ctx,
                  _resident(bias_tab.shape)],
        out_specs=pl.BlockSpec((1, tq, w), lambda b, r: (b, r, 0)),
        out_shape=jax.ShapeDtypeStruct(q.shape, BF16),
        compiler_params=_cparams(("parallel", "arbitrary"), 48),
    )(q, k, v, kx, vx, bias_tab)


def _ctx_attn_kernel(sink_ref, qs_ref, ks_ref, vs_ref, qn_ref, kn_ref, vn_ref, os_ref, on_ref):
    n = qs_ref.shape[1]
    g = SWA_HEADS // SWA_KV_HEADS
    rsel = lax.broadcasted_iota(jnp.int32, (g * n, 1), 0) >> (n.bit_length() - 1)
    outs = [None] * SWA_HEADS
    for kv in range(SWA_KV_HEADS):
        cs = slice(kv * LANES, (kv + 1) * LANES)
        q_parts = []
        sink = jnp.zeros((g * n, 1), F32)
        for gi in range(g):
            h = kv * g + gi
            q_parts.append(_half_mask(qs_ref[0, :, (h // 2) * LANES:(h // 2 + 1) * LANES], h % 2))
            sink = jnp.where(rsel == gi, sink_ref[h], sink)
        o = _softmax_pv(_qk(jnp.concatenate(q_parts, axis=0), ks_ref[0, :, cs]), vs_ref[0, :, cs], sink)
        for gi in range(g):
            outs[kv * g + gi] = o[gi * n:(gi + 1) * n]
    for p in range(SWA_HEADS // 2):
        os_ref[0, :, p * LANES:(p + 1) * LANES] = _merge_halves(outs[2 * p], outs[2 * p + 1]).astype(os_ref.dtype)
    for p in range(NA_HEADS // 2):
        cs = slice(p * LANES, (p + 1) * LANES)
        q_tile = qn_ref[0, :, cs]
        q2 = jnp.concatenate([_half_mask(q_tile, 0), _half_mask(q_tile, 1)], axis=0)
        o = _softmax_pv(_qk(q2, kn_ref[0, :, cs]), vn_ref[0, :, cs])
        on_ref[0, :, cs] = _merge_halves(o[:n], o[n:]).astype(on_ref.dtype)


def _context_attention(qs, ks, vs, qn, kn, vn, sink):
    b, n, w = qs.shape
    spec = lambda a: pl.BlockSpec((1,) + a.shape[1:], lambda b: (b, 0, 0))
    return pl.pallas_call(
        _ctx_attn_kernel, name="context_attention",
        grid=(b,),
        in_specs=[pl.BlockSpec(memory_space=pltpu.SMEM)] + [spec(a) for a in (qs, ks, vs, qn, kn, vn)],
        out_specs=[spec(qs), spec(qn)],
        out_shape=[jax.ShapeDtypeStruct(qs.shape, BF16), jax.ShapeDtypeStruct(qn.shape, BF16)],
        compiler_params=_cparams(("parallel",), 32),
    )(sink, qs, ks, vs, qn, kn, vn)


def _merge_ffn_kernel(x_ref, mg_ref, yh_ref, ys_ref, yn_ref, zg_ref, wb_ref, wo_ref,
                      sh_ref, sc_ref, gt_ref, g_ref, wg_ref, wu_ref, wd_ref, o_ref, *, fc):
    d = D_MODEL
    yh = jnp.concatenate([yh_ref[0, c] for c in range(yh_ref.shape[1])], axis=1)
    ys = (yh.astype(BF16), ys_ref[0], yn_ref[0])
    m = None
    for n in range(3):
        proj = _dot(ys[n], wb_ref[n])
        term = zg_ref[0, :, n * d:(n + 1) * d].astype(F32) * proj
        m = term if m is None else m + term
    x = x_ref[0] + mg_ref[0] * _dot(m.astype(BF16), wo_ref[...])
    o_ref[0] = _swiglu_residual(x, sh_ref, sc_ref, gt_ref, g_ref, wg_ref, wu_ref, wd_ref, fc)


def _merge_and_ffn(x, mix_gate, y_hy, y_swa, y_na, gates, w_branch, w_out, layer,
                   shift, scale, gate, gain, wg, wu, wd):
    b, l, d = x.shape
    tm = min(512, l)
    tok = lambda n: pl.BlockSpec((1, tm, n), lambda b, t: (b, t, 0))
    idx = (layer, 1)
    return pl.pallas_call(
        functools.partial(_merge_ffn_kernel, fc=FFN_CHUNK), name="merge_ffn",
        grid=(b, l // tm),
        in_specs=[tok(d), _mod_spec(mix_gate),
                  pl.BlockSpec((1, HY_WIDTH // HY_CHUNK, tm, HY_CHUNK), lambda b, t: (b, 0, t, 0)),
                  tok(HY_WIDTH), tok(HY_WIDTH), tok(3 * d),
                  _layer_slab(w_branch, (layer,)), _layer_slab(w_out, (layer,)),
                  _mod_spec(shift), _mod_spec(scale), _mod_spec(gate), _resident((1, d)),
                  _layer_slab(wg, idx), _layer_slab(wu, idx), _layer_slab(wd, idx)],
        out_specs=tok(d),
        out_shape=jax.ShapeDtypeStruct(x.shape, F32),
        compiler_params=_cparams(("parallel", "parallel"), 58),
    )(x, mix_gate[0], y_hy, y_swa, y_na, gates, w_branch, w_out, shift[0], scale[0], gate[0], gain.reshape(1, d),
      wg, wu, wd)


def _dft_tables(l):
    n = 2 * l
    fb = 256 if l >= 1024 else LANES * ((l + 1 + LANES - 1) // LANES)
    fp = fb * ((l + 1 + fb - 1) // fb)
    k = jnp.arange(fp, dtype=jnp.int32)[:, None]
    t = jnp.arange(l, dtype=jnp.int32)[None, :]
    ang = ((k * t) % n).astype(F32) * (2.0 * math.pi / n)
    live = k <= l
    c = jnp.where(live, jnp.cos(ang), 0.0)
    s = jnp.where(live, jnp.sin(ang), 0.0)
    c_hi = c.astype(BF16)
    s_hi = s.astype(BF16)
    wk = jnp.where((k == 0) | (k == l), 1.0, 2.0) * jnp.where(live, 1.0 / n, 0.0)
    return dict(fb=fb, c_hi=c_hi, s_hi=s_hi, ct_hi=c_hi.T, st_hi=s_hi.T, wk=wk.astype(F32))


def _filter_features(l):
    t = jnp.linspace(0.0, 1.0, l, dtype=F32)[:, None]
    w = (2.0 * math.pi / l) * jnp.arange(l, dtype=F32)[:, None]
    f = jnp.linspace(1e-4, HY_BANDS - 1, HY_BANDS, dtype=F32)[None, :]
    z = jnp.concatenate([t, jnp.cos(f * w), -jnp.sin(f * w)], axis=-1)
    return jnp.pad(z, ((0, 0), (0, LANES - HY_EMB)))


def _decay_rates():
    max_decay = math.log(HY_TARGET) / HY_FAST_DECAY
    min_decay = math.log(HY_TARGET) / HY_SLOW_DECAY
    return jnp.abs(jnp.linspace(min_decay, max_decay, HY_WIDTH, dtype=F32))[None, :]


def _rope_tables(l):
    pos = jnp.arange(l)
    row = (pos // GRID_W).astype(F32)
    col = (pos % GRID_W).astype(F32)
    half = HEAD_DIM // 2
    inv = 1.0 / (ROPE_BASE ** (jnp.arange(0, half, 2, dtype=F32) / half))
    ar = row[:, None] * inv[None, :]
    ac = col[:, None] * inv[None, :]
    cos = jnp.concatenate([jnp.cos(ar), jnp.cos(ar), jnp.cos(ac), jnp.cos(ac)], axis=-1)
    sin = jnp.concatenate([-jnp.sin(ar), jnp.sin(ar), -jnp.sin(ac), jnp.sin(ac)], axis=-1)
    return jnp.tile(cos, (1, 2)), jnp.tile(sin, (1, 2))


def _pad_to(a, shape):
    return jnp.pad(a, [(0, s - d) for d, s in zip(a.shape, shape)])


def _use_fft(bsz, length):
    return bsz % 2 == 0 and length % (8 * FFT_INNER) == 0


def _conv_tables(bsz, length):
    return _fft_tables(length) if _use_fft(bsz, length) else _dft_tables(length)


def _hyena_branch(z_hy, short_w, short_b, feat, fparams, deltas, bias, tabs):
    ad = _hyena_filter_sums(feat, fparams, deltas)
    if "n2" in tabs:
        kr, ki = _fft_filter_spectrum(ad, tabs)
        y1 = _fft_gated_long_conv(None, 0, z_hy, 1, short_w, short_b, kr, ki, 0, bias[0], tabs, F32)
        return _fft_gated_long_conv(y1, 0, z_hy, 2, short_w, short_b, kr, ki, 1, bias[1], tabs, BF16)
    v, x1, x2 = _short_conv3(z_hy, short_w, short_b)
    kr, ki = _filter_spectrum(ad, tabs)
    conv = functools.partial(_gated_long_conv, tabs=tabs)
    y1 = conv(v, x1, kr, ki, 0, bias[0])
    y = conv(y1, x2, kr, ki, 1, bias[1])
    bsz, length, _ = y.shape
    return y.reshape(bsz, length, HY_WIDTH // HY_CHUNK, HY_CHUNK).transpose(0, 2, 1, 3)


def kernel(x, c, ctx, c_ctx, w_ada, b_ada, norm_g, ffn_w_gate, ffn_w_up, ffn_w_down,
           w_in, hy_short_w, hy_short_b, hy_pe_w0, hy_pe_b0, hy_pe_w1, hy_pe_b1,
           hy_pe_w2, hy_pe_b2, hy_pe_wout, hy_sin_freq, hy_bias,
           swa_q_gain, swa_k_gain, swa_sink, na_q_gain, na_k_gain, na_rpb,
           w_branch, w_out):
    bsz, seq, d = x.shape
    n_ctx = ctx.shape[1]
    depth = w_ada.shape[0]

    c16 = _pad_to(jnp.concatenate([c, c_ctx[None, :]], axis=0), (MOD_ROWS, d))
    mods = _adaln_mods(c16, w_ada, b_ada).reshape(depth, MOD_ROWS, N_MOD, 1, d)

    tabs_x = _conv_tables(bsz, seq)
    tabs_c = _conv_tables(bsz, n_ctx)
    feat_x = _filter_features(seq)
    feat_c = _filter_features(n_ctx)
    deltas = _decay_rates()
    cos_x, sin_x = _rope_tables(seq)
    cos_c = jnp.ones((n_ctx, LANES), F32)
    sin_c = jnp.zeros((n_ctx, LANES), F32)
    eye = jnp.arange(MXU_DIM) // HEAD_DIM
    bd = (eye[:, None] == eye[None, :]).astype(BF16)

    wg = ffn_w_gate.astype(BF16)
    wu = ffn_w_up.astype(BF16)
    wd = ffn_w_down.astype(BF16)
    wb = w_branch.astype(BF16)
    wo = w_out.astype(BF16)
    win = w_in.astype(BF16)
    kv = win[:, :, _OFF_SWA + _SWA_Q:_OFF_NA].reshape(depth, d, 2, SWA_KV_HEADS, 1, HEAD_DIM)
    win_kv = jnp.broadcast_to(kv, (depth, d, 2, SWA_KV_HEADS, 2, HEAD_DIM)).reshape(depth, d, 4 * _SWA_KV)

    xc = ctx
    for i in range(depth):
        last = i == depth - 1
        mx = lambda j: (mods, i, j, None)
        mc = lambda j: (mods, i, j, bsz)
        tile2 = lambda g: jnp.tile(g, 2 * MXU_DIM // LANES)
        head_gains = jnp.stack([tile2(swa_q_gain[i]), tile2(swa_k_gain[i]), tile2(na_q_gain[i]), tile2(na_k_gain[i])])
        fparams = (_pad_to(hy_pe_w0[i], (LANES, LANES)), _pad_to(hy_pe_b0[i][None], (1, LANES)),
                   _pad_to(hy_pe_w1[i], (LANES, LANES)), _pad_to(hy_pe_b1[i][None], (1, LANES)),
                   _pad_to(hy_pe_w2[i], (LANES, LANES)), _pad_to(hy_pe_b2[i][None], (1, LANES)),
                   _pad_to(hy_pe_wout[i], (LANES, 4 * HY_WIDTH)), _pad_to(hy_sin_freq[i][None], (1, LANES)))

        x = _ffn_half_step(x, mx(0), mx(1), mx(2), norm_g[i, 0], wg, wu, wd, (i, 0))
        xc = _ffn_half_step(xc, mc(0), mc(1), mc(2), norm_g[i, 0], wg, wu, wd, (i, 0))

        z_hy, q_s, k_s, v_s, q_n, k_n, v_n, gates = _in_projection(
            x, mx(3), mx(4), norm_g[i, 1], cos_x, sin_x, head_gains, bd, win, win_kv, i)
        zc_hy, qc_s, kc_s, vc_s, qc_n, kc_n, vc_n, gates_c = _in_projection(
            xc, mc(3), mc(4), norm_g[i, 1], cos_c, sin_c, head_gains, bd, win, win_kv, i)

        y_hy = _hyena_branch(z_hy, hy_short_w[i], hy_short_b[i], feat_x, fparams, deltas, hy_bias[i], tabs_x)
        y_swa = _window_gqa(q_s, k_s, v_s, kc_s, vc_s, swa_sink[i])
        y_na = _neighbourhood_attention(q_n, k_n, v_n, kc_n, vc_n, _na_bias_table(na_rpb[i]))
        x = _merge_and_ffn(x, mx(5), y_hy, y_swa, y_na, gates, wb, wo, i,
                           mx(6), mx(7), mx(8), norm_g[i, 2], wg, wu, wd)

        if not last:
            yc_hy = _hyena_branch(zc_hy, hy_short_w[i], hy_short_b[i], feat_c, fparams, deltas, hy_bias[i], tabs_c)
            yc_swa, yc_na = _context_attention(qc_s, kc_s, vc_s, qc_n, kc_n, vc_n, swa_sink[i])
            xc = _merge_and_ffn(xc, mc(5), yc_hy, yc_swa, yc_na, gates_c, wb, wo, i,
                                mc(6), mc(7), mc(8), norm_g[i, 2], wg, wu, wd)
    return x
```

```python
import functools
import math

import jax
import jax.numpy as jnp
from jax import lax
from jax.experimental import pallas as pl
from jax.experimental.pallas import tpu as pltpu

F32 = jnp.float32
BF16 = jnp.bfloat16

D_MODEL = 1024
DEPTH = 4
GRID_W = 64
HEAD_DIM = 64
N_MOD = 9
RMS_EPS = 1e-6

HY_WIDTH = D_MODEL // 2
HY_CHUNK = 256
HY_EMB = 33
HY_BANDS = (HY_EMB - 1) // 2
HY_FFN = 64
HY_FAST_DECAY = 0.3
HY_SLOW_DECAY = 1.5
HY_TARGET = 1e-2

SWA_HEADS = 8
SWA_KV_HEADS = 2
SWA_WINDOW = 128
SWA_BLOCK = 128
ROPE_BASE = 10000.0

NA_HEADS = 8
NA_MAX_ROWS = 8
NA_COLS = 16
NA_COL_BLOCK = 16

FFN_HIDDEN = 256 * ((8 * D_MODEL // 3 + 255) // 256)
FFN_CHUNK = 1536

LANES = 128
MXU_DIM = 256
MOD_ROWS = 16
NEG_BIG = -1e30


def _cparams(sem, vmem_mb):
    return pltpu.CompilerParams(dimension_semantics=sem, vmem_limit_bytes=vmem_mb * 1024 * 1024)


def _resident(shape):
    nd = len(shape)
    return pl.BlockSpec(shape, lambda *_: (0,) * nd, pipeline_mode=pl.Buffered(1))


def _layer_slab(arr, idx):
    rest = arr.shape[len(idx):]
    return pl.BlockSpec((None,) * len(idx) + rest, lambda *_: tuple(idx) + (0,) * len(rest),
                        pipeline_mode=pl.Buffered(1))


def _split(x):
    hi = x.astype(BF16)
    lo = (x - hi.astype(F32)).astype(BF16)
    return hi, lo


def _dot(a, b):
    return jnp.dot(a, b, preferred_element_type=F32)


def _dot3(a_hi, a_lo, b_hi, b_lo):
    return _dot(a_hi, b_hi) + _dot(a_lo, b_hi) + _dot(a_hi, b_lo)


def _dot3f(a, b):
    a_hi, a_lo = _split(a)
    b_hi, b_lo = _split(b)
    return _dot3(a_hi, a_lo, b_hi, b_lo)


def _rms_mod(x, gain, shift, scale):
    ms = jnp.mean(x * x, axis=-1, keepdims=True)
    y = x * lax.rsqrt(ms + RMS_EPS) * gain
    return y * (1.0 + scale) + shift


def _head_sumsq(x, bd):
    return [_dot(jnp.square(x[:, j * MXU_DIM:(j + 1) * MXU_DIM]).astype(BF16), bd) for j in range(x.shape[1] // MXU_DIM)]


def _head_normalize(x, sumsq, gain):
    outs = [x[:, j * MXU_DIM:(j + 1) * MXU_DIM] * lax.rsqrt(s * (1.0 / HEAD_DIM) + RMS_EPS) * gain
            for j, s in enumerate(sumsq)]
    return outs[0] if len(outs) == 1 else jnp.concatenate(outs, axis=1)


def _rope(x, cos, sin):
    lane = lax.broadcasted_iota(jnp.int32, (x.shape[0], LANES), 1)
    first = (lane & 31) < 16
    outs = []
    for j in range(x.shape[1] // LANES):
        xc = x[:, j * LANES:(j + 1) * LANES]
        partner = jnp.where(first, pltpu.roll(xc, LANES - 16, axis=1), pltpu.roll(xc, 16, axis=1))
        outs.append(xc * cos + partner * sin)
    return outs[0] if len(outs) == 1 else jnp.concatenate(outs, axis=1)


def _mods_kernel(c_ref, w_ref, b_ref, o_ref):
    c = c_ref[...]
    a = c * jax.nn.sigmoid(c)
    o_ref[0] = _dot3f(a, w_ref[0]) + b_ref[0]


def _adaln_mods(c16, w_ada, b_ada):
    depth, d, n = w_ada.shape
    nb = 1152
    return pl.pallas_call(
        _mods_kernel, name="adaln_mods",
        grid=(depth, n // nb),
        in_specs=[
            pl.BlockSpec((MOD_ROWS, d), lambda i, j: (0, 0)),
            pl.BlockSpec((1, d, nb), lambda i, j: (i, 0, j)),
            pl.BlockSpec((1, 1, nb), lambda i, j: (i, 0, j)),
        ],
        out_specs=pl.BlockSpec((1, MOD_ROWS, nb), lambda i, j: (i, 0, j)),
        out_shape=jax.ShapeDtypeStruct((depth, MOD_ROWS, n), F32),
        compiler_params=_cparams(("arbitrary", "arbitrary"), 48),
    )(c16, w_ada, b_ada.reshape(depth, 1, n))


def _swiglu_residual(x, sh_ref, sc_ref, gt_ref, g_ref, wg_ref, wu_ref, wd_ref, fc):
    hb = _rms_mod(x, g_ref[...], sh_ref[0], sc_ref[0]).astype(BF16)
    acc = None
    for f0 in range(0, FFN_HIDDEN, fc):
        f1 = min(f0 + fc, FFN_HIDDEN)
        g = _dot(hb, wg_ref[:, f0:f1])
        u = _dot(hb, wu_ref[:, f0:f1])
        a = (g * jax.nn.sigmoid(g) * u).astype(BF16)
        d = _dot(a, wd_ref[f0:f1, :])
        acc = d if acc is None else acc + d
    return x + 0.5 * gt_ref[0] * acc


def _ffn_kernel(x_ref, sh_ref, sc_ref, gt_ref, g_ref, wg_ref, wu_ref, wd_ref, o_ref, *, fc):
    o_ref[0] = _swiglu_residual(x_ref[0], sh_ref, sc_ref, gt_ref, g_ref, wg_ref, wu_ref, wd_ref, fc)


def _mod_spec(sel):
    _, layer, j, row = sel
    if row is None:
        return pl.BlockSpec((None, 1, None, 1, D_MODEL), lambda b, t: (layer, b, j, 0, 0))
    return pl.BlockSpec((None, 1, None, 1, D_MODEL), lambda b, t: (layer, row, j, 0, 0))


def _ffn_half_step(x, shift, scale, gate, gain, wg, wu, wd, idx):
    b, l, d = x.shape
    tm = min(512, l)
    return pl.pallas_call(
        functools.partial(_ffn_kernel, fc=FFN_CHUNK), name="ffn_half_step",
        grid=(b, l // tm),
        in_specs=[
            pl.BlockSpec((1, tm, d), lambda b, t: (b, t, 0)),
            _mod_spec(shift), _mod_spec(scale), _mod_spec(gate),
            _resident((1, d)),
            _layer_slab(wg, idx), _layer_slab(wu, idx), _layer_slab(wd, idx),
        ],
        out_specs=pl.BlockSpec((1, tm, d), lambda b, t: (b, t, 0)),
        out_shape=jax.ShapeDtypeStruct(x.shape, F32),
        compiler_params=_cparams(("parallel", "parallel"), 56),
    )(x, shift[0], scale[0], gate[0], gain.reshape(1, d), wg, wu, wd)


_HY_COLS = 3 * HY_WIDTH
_SWA_Q = SWA_HEADS * HEAD_DIM
_SWA_KV = SWA_KV_HEADS * HEAD_DIM
_NA_W = NA_HEADS * HEAD_DIM
_OFF_SWA = _HY_COLS
_OFF_NA = _OFF_SWA + _SWA_Q + 2 * _SWA_KV
_OFF_GATE = _OFF_NA + 3 * _NA_W


def _inproj_kernel(x_ref, sh_ref, sc_ref, g_ref, cos_ref, sin_ref, hg_ref, bd_ref, w_ref, wkv_ref,
                   zhy_ref, qs_ref, ks_ref, vs_ref, qn_ref, kn_ref, vn_ref, gt_ref):
    hb = _rms_mod(x_ref[0], g_ref[...], sh_ref[0], sc_ref[0]).astype(BF16)
    cos = cos_ref[...]
    sin = sin_ref[...]
    bd = bd_ref[...]
    scale = HEAD_DIM ** -0.5
    proj = lambda lo, n: _dot(hb, w_ref[:, lo:lo + n])
    kvw = 2 * _SWA_KV
    raw = [proj(_OFF_SWA, _SWA_Q), _dot(hb, wkv_ref[:, :kvw]), proj(_OFF_NA, _NA_W), proj(_OFF_NA + _NA_W, _NA_W)]
    gates = proj(_OFF_GATE, 3 * D_MODEL)
    sums = [_head_sumsq(r, bd) for r in raw]
    vs_ref[0] = _dot(hb, wkv_ref[:, kvw:]).astype(BF16)
    vn_ref[0] = proj(_OFF_NA + 2 * _NA_W, _NA_W).astype(BF16)
    zhy = proj(0, _HY_COLS).astype(BF16)
    for c in range(_HY_COLS // HY_CHUNK):
        zhy_ref[0, c] = zhy[:, c * HY_CHUNK:(c + 1) * HY_CHUNK]
    gt_ref[0] = jax.nn.sigmoid(gates).astype(BF16)
    qs, ks, qn, kn = [_head_normalize(r, s, hg_ref[i:i + 1, :]) for i, (r, s) in enumerate(zip(raw, sums))]
    qs_ref[0] = (_rope(qs, cos, sin) * scale).astype(BF16)
    ks_ref[0] = _rope(ks, cos, sin).astype(BF16)
    qn_ref[0] = (qn * scale).astype(BF16)
    kn_ref[0] = kn.astype(BF16)


def _in_projection(x, shift, scale, gain, cos, sin, head_gains, bd, w, w_kv_dup, layer):
    b, l, d = x.shape
    tm = min(512, l)
    kvw = 2 * _SWA_KV
    widths = [_HY_COLS, _SWA_Q, kvw, kvw, _NA_W, _NA_W, _NA_W, 3 * d]
    dtypes = [BF16] * 8
    return pl.pallas_call(
        _inproj_kernel, name="in_projection",
        grid=(b, l // tm),
        in_specs=[
            pl.BlockSpec((1, tm, d), lambda b, t: (b, t, 0)),
            _mod_spec(shift), _mod_spec(scale),
            _resident((1, d)),
            pl.BlockSpec((tm, LANES), lambda b, t: (t, 0)),
            pl.BlockSpec((tm, LANES), lambda b, t: (t, 0)),
            _resident(head_gains.shape), _resident(bd.shape), _layer_slab(w, (layer,)), _layer_slab(w_kv_dup, (layer,)),
        ],
        out_specs=[pl.BlockSpec((1, _HY_COLS // HY_CHUNK, tm, HY_CHUNK), lambda b, t: (b, 0, t, 0))]
                  + [pl.BlockSpec((1, tm, n), lambda b, t: (b, t, 0)) for n in widths[1:]],
        out_shape=[jax.ShapeDtypeStruct((b, _HY_COLS // HY_CHUNK, l, HY_CHUNK), BF16)]
                  + [jax.ShapeDtypeStruct((b, l, n), dt) for n, dt in zip(widths[1:], dtypes[1:])],
        compiler_params=_cparams(("parallel", "parallel"), 56),
    )(x, shift[0], scale[0], gain.reshape(1, d), cos, sin, head_gains, bd, w, w_kv_dup)


def _shortconv_kernel(z0_ref, z1_ref, z2_ref, w_ref, b_ref, o0_ref, o1_ref, o2_ref):
    nb, _, l, _ = z0_ref.shape
    row = lax.broadcasted_iota(jnp.int32, z0_ref.shape[2:], 0)
    for g, (z_ref, o_ref) in enumerate(((z0_ref, o0_ref), (z1_ref, o1_ref), (z2_ref, o2_ref))):
        for i in range(nb):
            z = z_ref[i, 0].astype(F32)
            prev = jnp.where(row == 0, 0.0, pltpu.roll(z, 1, axis=0))
            nxt = jnp.where(row == l - 1, 0.0, pltpu.roll(z, l - 1, axis=0))
            o_ref[i] = w_ref[g, 0:1, :] * prev + w_ref[g, 1:2, :] * z + w_ref[g, 2:3, :] * nxt + b_ref[g]


def _short_conv3(z, w, bias):
    b, _, l, cb = z.shape
    ncb = HY_WIDTH // cb
    wg = w.reshape(3, 3, HY_WIDTH).transpose(1, 0, 2)
    zin = lambda g: pl.BlockSpec((b, 1, l, cb), lambda j: (0, g * ncb + j, 0, 0))
    out = pl.BlockSpec((b, l, cb), lambda j: (0, 0, j))
    return pl.pallas_call(
        _shortconv_kernel, name="short_conv3",
        grid=(ncb,),
        in_specs=[zin(0), zin(1), zin(2),
                  pl.BlockSpec((3, 3, cb), lambda j: (0, 0, j)),
                  pl.BlockSpec((3, 1, cb), lambda j: (0, 0, j))],
        out_specs=[out, out, out],
        out_shape=[jax.ShapeDtypeStruct((b, l, HY_WIDTH), F32)] * 3,
        compiler_params=_cparams(("parallel",), 48),
    )(z, z, z, wg, bias.reshape(3, 1, HY_WIDTH))


def _filter_kernel(z_ref, w0, b0, w1, b1, w2, b2, wo, fr_ref, dl_ref, o_ref):
    z = z_ref[...]
    fr = fr_ref[...]
    a = jnp.sin(fr * (_dot3f(z, w0[...]) + b0[...]))
    a = jnp.sin(fr * (_dot3f(a, w1[...]) + b1[...]))
    a = jnp.sin(fr * (_dot3f(a, w2[...]) + b2[...]))
    hh = _dot3f(a, wo[...])
    t = z[:, 0:1]
    win = jnp.exp(-t * dl_ref[...])
    row = lax.broadcasted_iota(jnp.int32, win.shape, 0) + pl.program_id(0) * z.shape[0]
    w = HY_WIDTH
    for o in range(2):
        hp = hh[:, (2 * o) * w:(2 * o + 1) * w] * win
        hn = jnp.where(row == 0, 0.0, hh[:, (2 * o + 1) * w:(2 * o + 2) * w] * win)
        o_ref[0, :, o * w:(o + 1) * w] = hp + hn
        o_ref[1, :, o * w:(o + 1) * w] = hp - hn


def _hyena_filter_sums(zfeat, fp, deltas):
    l = zfeat.shape[0]
    tl = min(256, l)
    w0, b0, w1, b1, w2, b2, wo, fr = fp
    n = 2 * HY_WIDTH
    consts = [w0, b0, w1, b1, w2, b2, wo, fr, deltas]
    return pl.pallas_call(
        _filter_kernel, name="hyena_filter",
        grid=(l // tl,),
        in_specs=[pl.BlockSpec((tl, LANES), lambda t: (t, 0))] + [_resident(c.shape) for c in consts],
        out_specs=pl.BlockSpec((2, tl, n), lambda t: (0, t, 0)),
        out_shape=jax.ShapeDtypeStruct((2, l, n), F32),
        compiler_params=_cparams(("parallel",), 32),
    )(zfeat, *consts)


def _spectrum_kernel(a_ref, d_ref, c_ref, s_ref, kr_ref, ki_ref):
    kr_ref[...] = _dot(c_ref[...], a_ref[0].astype(BF16))
    ki_ref[...] = -_dot(s_ref[...], d_ref[0].astype(BF16))


def _filter_spectrum(ad, tabs):
    _, l, n = ad.shape
    fp = tabs["c_hi"].shape[0]
    fb = tabs["fb"]
    cb = 256
    tab = pl.BlockSpec((fb, l), lambda j, k: (k, 0))
    out = pl.BlockSpec((fb, cb), lambda j, k: (k, j))
    return pl.pallas_call(
        _spectrum_kernel, name="dense_spectrum",
        grid=(n // cb, fp // fb),
        in_specs=[pl.BlockSpec((1, l, cb), lambda j, k: (0, 0, j)),
                  pl.BlockSpec((1, l, cb), lambda j, k: (1, 0, j)), tab, tab],
        out_specs=[out, out],
        out_shape=[jax.ShapeDtypeStruct((fp, n), F32)] * 2,
        compiler_params=_cparams(("parallel", "parallel"), 48),
    )(ad, ad, tabs["c_hi"], tabs["s_hi"])


def _longconv_kernel(u_ref, g_ref, kr_ref, ki_ref, bias_ref, wk_ref, c_ref, s_ref, ct_ref, st_ref,
                     o_ref, ub_ref, acc_ref):
    kb = pl.program_id(1)
    nb, _, cb = u_ref.shape

    @pl.when(kb == 0)
    def _():
        ub_ref[...] = jnp.concatenate([u_ref[i].astype(BF16) for i in range(nb)], axis=1)
        acc_ref[...] = jnp.zeros_like(acc_ref)

    ub = ub_ref[...]
    xr = _dot(c_ref[...], ub)
    xi = -_dot(s_ref[...], ub)
    kr = jnp.tile(kr_ref[...], (1, nb))
    ki = jnp.tile(ki_ref[...], (1, nb))
    wk = wk_ref[...]
    yr = ((xr * kr - xi * ki) * wk).astype(BF16)
    yi = ((xr * ki + xi * kr) * wk).astype(BF16)
    acc_ref[...] += _dot(ct_ref[...], yr) - _dot(st_ref[...], yi)

    @pl.when(kb == pl.num_programs(1) - 1)
    def _():
        for i in range(nb):
            o_ref[i] = g_ref[i] * (acc_ref[:, i * cb:(i + 1) * cb] + u_ref[i] * bias_ref[...])


def _gated_long_conv(u_arr, g_arr, kr, ki, k_col, bias, tabs):
    b, l, _ = u_arr.shape
    cb = 256
    ncb = HY_WIDTH // cb
    fp = tabs["c_hi"].shape[0]
    fb = tabs["fb"]
    tab = pl.BlockSpec((fb, l), lambda j, k: (k, 0))
    tabt = pl.BlockSpec((l, fb), lambda j, k: (0, k))
    spec = pl.BlockSpec((fb, cb), lambda j, k: (k, k_col * ncb + j))
    tok = pl.BlockSpec((b, l, cb), lambda j, k: (0, 0, j))
    return pl.pallas_call(
        _longconv_kernel, name="dense_long_conv",
        grid=(ncb, fp // fb),
        in_specs=[tok, tok, spec, spec,
                  pl.BlockSpec((1, cb), lambda j, k: (0, j)),
                  pl.BlockSpec((fb, 1), lambda j, k: (k, 0)),
                  tab, tab, tabt, tabt],
        out_specs=tok,
        out_shape=jax.ShapeDtypeStruct((b, l, HY_WIDTH), F32),
        scratch_shapes=[pltpu.VMEM((l, b * cb), BF16), pltpu.VMEM((l, b * cb), F32)],
        compiler_params=_cparams(("parallel", "arbitrary"), 56),
    )(u_arr, g_arr, kr, ki, bias.reshape(1, HY_WIDTH), tabs["wk"],
      tabs["c_hi"], tabs["s_hi"], tabs["ct_hi"], tabs["st_hi"])


FFT_INNER = 64
FFT_COLS = HY_CHUNK
FFT_K2_CHUNK = 16
FFT_UNROLL = 16


def _fft_tables(l):
    n = 2 * l
    n1 = FFT_INNER
    n2 = n // n1
    h = n2 // 2
    ang = lambda idx, mod: (idx % mod).astype(F32) * (2.0 * math.pi / mod)
    k2 = jnp.arange(n2, dtype=jnp.int32)
    m2 = jnp.arange(h, dtype=jnp.int32)
    a1 = ang(k2[:, None] * m2[None, :], n2)
    c1, s1 = jnp.cos(a1), jnp.sin(a1)
    f1 = jnp.block([[c1, s1], [-s1, c1]])
    k1 = jnp.arange(n1, dtype=jnp.int32)
    a2 = ang(k1[None, None, :] * (k1[None, :, None] * n2 + k2[:, None, None]), n)
    c2, s2 = jnp.cos(a2), jnp.sin(a2)
    m = jnp.concatenate([jnp.concatenate([c2, s2], axis=2), jnp.concatenate([-s2, c2], axis=2)], axis=1)
    g = jnp.block([[c1.T, -s1.T], [s1.T, c1.T]]) * (1.0 / n)
    out = dict(n2=n2)
    for name, t in (("f1", f1), ("f1r", f1[:, :h]), ("m", m), ("mt", jnp.swapaxes(m, 1, 2)), ("g", g)):
        out[name + "_hi"] = t.astype(BF16)
    return out


def _mm(w_hi, w_lo, x):
    if w_lo is None:
        return _dot(w_hi, x.astype(BF16))
    hi, lo = _split(x)
    return _dot3(w_hi, w_lo, hi, lo)


def _row_block_matmul(w_hi, w_lo, src_ref, dst_ref, count):
    m, k = w_hi.shape

    def body(i, carry):
        x = src_ref[pl.ds(pl.multiple_of(i * k, k), k), :]
        dst_ref[pl.ds(pl.multiple_of(i * m, m), m), :] = _mm(w_hi, w_lo, x).astype(dst_ref.dtype)
        return carry
    lax.fori_loop(0, count, body, 0, unroll=FFT_UNROLL)


def _short_conv_rows(z, w_ref, b_ref):
    l = z.shape[0]
    row = lax.broadcasted_iota(jnp.int32, z.shape, 0)
    prev = jnp.where(row == 0, 0.0, pltpu.roll(z, 1, axis=0))
    nxt = jnp.where(row == l - 1, 0.0, pltpu.roll(z, l - 1, axis=0))
    return w_ref[0:1, :] * prev + w_ref[1:2, :] * z + w_ref[2:3, :] * nxt + b_ref[...]


def _fftconv_kernel(u_ref, g_ref, uw_ref, ub_ref, gw_ref, gb_ref, kr_ref, ki_ref, bias_ref,
                    f1_ref, m_ref, mt_ref, gm_ref, o_ref, z_scr, y_scr, p_scr, q_scr, u_scr, *, u_conv):
    kc = pl.program_id(2)
    n1 = FFT_INNER
    cc = z_scr.shape[1]
    n2 = z_scr.shape[0] // n1
    h = n2 // 2

    @pl.when(kc == 0)
    def _stage1():
        z = []
        for s in range(2):
            u = u_ref[s, 0]
            if u_conv:
                u = _short_conv_rows(u.astype(F32), uw_ref, ub_ref)
                u_scr[s] = u
            z.append(jnp.swapaxes(u.astype(BF16).reshape(h, n1, cc), 0, 1))
        z_scr[...] = jnp.concatenate(z, axis=1).reshape(n1 * n2, cc)
        _row_block_matmul(f1_ref[...], None, z_scr, p_scr, n1)
        q_scr[...] = jnp.swapaxes(p_scr[...].reshape(n1, 2 * n2, cc), 0, 1).reshape(2 * n2 * n1, cc)

    ks = kr_ref.shape[0]
    rows = [(pl.multiple_of((kc * ks + j) * n1, n1), pl.multiple_of((n2 + kc * ks + j) * n1, n1)) for j in range(ks)]
    xs = [_mm(m_ref[j], None, jnp.concatenate([q_scr[pl.ds(r0, n1), :], q_scr[pl.ds(i0, n1), :]], axis=0))
          for j, (r0, i0) in enumerate(rows)]
    ys = []
    for j, x in enumerate(xs):
        xr, xi = x[:n1], x[n1:]
        kr, ki = kr_ref[j], ki_ref[j]
        ys.append(jnp.concatenate([xr * kr - xi * ki, xr * ki + xi * kr], axis=0))
    for j, (r0, i0) in enumerate(rows):
        b = _mm(mt_ref[j], None, ys[j]).astype(p_scr.dtype)
        p_scr[pl.ds(r0, n1), :] = b[:n1]
        p_scr[pl.ds(i0, n1), :] = b[n1:]

    @pl.when(kc == pl.num_programs(2) - 1)
    def _stage1_inv():
        q_scr[...] = jnp.swapaxes(p_scr[...].reshape(2 * n2, n1, cc), 0, 1).reshape(2 * n2 * n1, cc)
        _row_block_matmul(gm_ref[...], None, q_scr, y_scr, n1)
        y3 = y_scr[...].reshape(n1, n2, cc)
        for s in range(2):
            y = jnp.swapaxes(y3[:, s * h:(s + 1) * h, :], 0, 1).reshape(h * n1, cc)
            gate = _short_conv_rows(g_ref[s, 0].astype(F32), gw_ref, gb_ref)
            u = u_scr[s] if u_conv else u_ref[s, 0]
            o_ref[s, 0] = (gate * (y + u * bias_ref[...])).astype(o_ref.dtype)


def _fft_gated_long_conv(u, u_col, z_hy, g_col, short_w, short_b, kr, ki, k_col, bias, ft, out_dtype):
    b, _, l, cc = z_hy.shape
    c = HY_WIDTH
    n1, n2, ks = FFT_INNER, ft["n2"], FFT_K2_CHUNK
    p, ncc = b // 2, c // cc
    u_conv = u is None
    if u_conv:
        u = z_hy
    tok = lambda col: pl.BlockSpec((2, 1, None, l, cc), lambda q, j, k: (0, q, col * ncc + j, 0, 0))
    taps = lambda col: pl.BlockSpec((3, cc), lambda q, j, k: (0, col * ncc + j))
    row = lambda col: pl.BlockSpec((1, cc), lambda q, j, k: (0, col * ncc + j))
    filt = pl.BlockSpec((ks, n1, cc), lambda q, j, k: (k, 0, k_col * ncc + j))
    tab = pl.BlockSpec((ks, 2 * n1, 2 * n1), lambda q, j, k: (k, 0, 0))
    res = lambda a: pl.BlockSpec(a.shape, lambda q, j, k: (0,) * a.ndim)
    sb = short_b.reshape(1, -1)
    y = pl.pallas_call(
        functools.partial(_fftconv_kernel, u_conv=u_conv), name="fft_long_conv",
        grid=(p, ncc, n2 // ks),
        in_specs=[tok(u_col), tok(g_col), taps(u_col), row(u_col), taps(g_col), row(g_col), filt, filt, row(0),
                  res(ft["f1_hi"]), tab, tab, res(ft["g_hi"])],
        out_specs=tok(0),
        out_shape=jax.ShapeDtypeStruct((2, p, ncc, l, cc), out_dtype),
        scratch_shapes=[pltpu.VMEM((n1 * n2, cc), BF16), pltpu.VMEM((n1 * n2, cc), F32),
                        pltpu.VMEM((2 * n2 * n1, cc), BF16), pltpu.VMEM((2 * n2 * n1, cc), BF16),
                        pltpu.VMEM((2, l, cc) if u_conv else (2, 8, LANES), F32)],
        compiler_params=_cparams(("parallel", "parallel", "arbitrary"), 56),
    )(u.reshape((2, p) + u.shape[1:]), z_hy.reshape((2, p) + z_hy.shape[1:]), short_w, sb, short_w, sb, kr, ki,
      bias.reshape(1, c), ft["f1_hi"], ft["m_hi"], ft["mt_hi"], ft["g_hi"])
    return y.reshape(b, ncc, l, cc)


def _fftspec_kernel(a_ref, d_ref, f1_ref, m_ref, kr_ref, ki_ref, z_scr, p_scr, q_scr):
    n1 = FFT_INNER
    cc = z_scr.shape[1]
    h = z_scr.shape[0] // n1
    n2 = 2 * h
    for src_ref, out_ref, lo in ((a_ref, kr_ref, 0), (d_ref, ki_ref, n1)):
        z_scr[...] = jnp.swapaxes(src_ref[0].astype(BF16).reshape(h, n1, cc), 0, 1).reshape(n1 * h, cc)
        _row_block_matmul(f1_ref[...], None, z_scr, p_scr, n1)
        q_scr[...] = jnp.swapaxes(p_scr[...].reshape(n1, 2 * n2, cc), 0, 1).reshape(2 * n2 * n1, cc)

        def body(k2, carry):
            r0 = pl.multiple_of(k2 * n1, n1)
            i0 = pl.multiple_of((n2 + k2) * n1, n1)
            slab = jnp.concatenate([q_scr[pl.ds(r0, n1), :], q_scr[pl.ds(i0, n1), :]], axis=0)
            out_ref[k2] = _mm(m_ref[k2][lo:lo + n1], None, slab)
            return carry
        lax.fori_loop(0, n2, body, 0, unroll=FFT_UNROLL)


def _fft_filter_spectrum(ad, ft):
    _, l, n = ad.shape
    n1, n2, cc = FFT_INNER, ft["n2"], FFT_COLS
    h = n2 // 2
    res = lambda a: pl.BlockSpec(a.shape, lambda j: (0,) * a.ndim)
    out = pl.BlockSpec((n2, n1, cc), lambda j: (0, 0, j))
    tabs = [ft["f1r_hi"], ft["m_hi"]]
    return pl.pallas_call(
        _fftspec_kernel, name="fft_filter_spectrum",
        grid=(n // cc,),
        in_specs=[pl.BlockSpec((1, l, cc), lambda j: (0, 0, j)), pl.BlockSpec((1, l, cc), lambda j: (1, 0, j))]
                 + [res(t) for t in tabs],
        out_specs=[out, out],
        out_shape=[jax.ShapeDtypeStruct((n2, n1, n), F32)] * 2,
        scratch_shapes=[pltpu.VMEM((n1 * h, cc), BF16), pltpu.VMEM((2 * n2 * n1, cc), BF16),
                        pltpu.VMEM((2 * n2 * n1, cc), BF16)],
        compiler_params=_cparams(("parallel",), 56),
    )(ad, ad, *tabs)


def _softmax_weights(s, sink=None):
    m = jnp.max(s, axis=-1, keepdims=True)
    if sink is not None:
        m = jnp.maximum(m, sink)
    p = jnp.exp(s - m)
    den = jnp.sum(p, axis=-1, keepdims=True)
    if sink is not None:
        den = den + jnp.exp(sink - m)
    return p.astype(BF16), den


def _softmax_pv(s, v, sink=None):
    p, den = _softmax_weights(s, sink)
    return _dot(p, v) / den


def _qk(q, k):
    return lax.dot_general(q, k, (((1,), (1,)), ((), ())), preferred_element_type=F32)


def _half_mask(q_tile, half):
    lane = lax.broadcasted_iota(jnp.int32, q_tile.shape, 1)
    keep = (lane < HEAD_DIM) if half == 0 else (lane >= HEAD_DIM)
    return jnp.where(keep, q_tile, jnp.zeros_like(q_tile))


def _merge_halves(o_even, o_odd):
    lane = lax.broadcasted_iota(jnp.int32, o_even.shape, 1)
    return jnp.where(lane < HEAD_DIM, o_even, o_odd)


SWA_CHAIN_HEADS = 2

def _swa_kernel(sink_ref, q_ref, kp_ref, kc_ref, kn_ref, vp_ref, vc_ref, vn_ref, kx_ref, vx_ref, o_ref):
    n = pl.program_id(1)
    nb = pl.num_programs(1)
    blk = SWA_BLOCK
    g = SWA_CHAIN_HEADS
    n_ctx = kx_ref.shape[1]
    rows = g * blk
    ri = lax.broadcasted_iota(jnp.int32, (rows, 3 * blk + n_ctx), 0) & (blk - 1)
    ci = lax.broadcasted_iota(jnp.int32, (rows, 3 * blk + n_ctx), 1)
    ok_prev = (ci < blk) & (ci >= ri) & (n > 0)
    ok_cur = (ci >= blk) & (ci < 2 * blk)
    ok_next = (ci >= 2 * blk) & (ci < 3 * blk) & (ci - 2 * blk <= ri) & (n < nb - 1)
    ok = ok_prev | ok_cur | ok_next | (ci >= 3 * blk)
    bias = jnp.where(ok, 0.0, NEG_BIG)
    rsel = lax.broadcasted_iota(jnp.int32, (rows, 1), 0) >> (blk.bit_length() - 1)
    outs = [None] * SWA_HEADS
    chains = range(SWA_HEADS // g)
    group = SWA_HEADS // SWA_KV_HEADS
    lanes = [slice((c * g // group) * LANES, (c * g // group + 1) * LANES) for c in chains]
    scores, sinks = [], []
    for c in chains:
        cs = lanes[c]
        k = jnp.concatenate([kp_ref[0, :, cs], kc_ref[0, :, cs], kn_ref[0, :, cs], kx_ref[0, :, cs]], axis=0)
        q_parts = []
        sink = jnp.zeros((rows, 1), F32)
        for gi in range(g):
            h = c * g + gi
            q_parts.append(_half_mask(q_ref[0, :, (h // 2) * LANES:(h // 2 + 1) * LANES], h % 2))
            sink = jnp.where(rsel == gi, sink_ref[h], sink)
        scores.append(_qk(jnp.concatenate(q_parts, axis=0), k))
        sinks.append(sink)
    probs = [_softmax_weights(scores[c] + bias, sinks[c]) for c in chains]
    for c in chains:
        cs = lanes[c]
        v = jnp.concatenate([vp_ref[0, :, cs], vc_ref[0, :, cs], vn_ref[0, :, cs], vx_ref[0, :, cs]], axis=0)
        pw, den = probs[c]
        o = _dot(pw, v) / den
        for gi in range(g):
            outs[c * g + gi] = o[gi * blk:(gi + 1) * blk]
    for p in range(SWA_HEADS // 2):
        o_ref[0, :, p * LANES:(p + 1) * LANES] = _merge_halves(outs[2 * p], outs[2 * p + 1]).astype(o_ref.dtype)


def _window_gqa(q, k, v, kx, vx, sink):
    b, l, _ = q.shape
    nb = l // SWA_BLOCK
    n_ctx = kx.shape[1]
    kvw = k.shape[2]

    def blk(off):
        return pl.BlockSpec((1, SWA_BLOCK, kvw), lambda b, n: (b, jnp.clip(n + off, 0, nb - 1), 0))

    ctx = pl.BlockSpec((1, n_ctx, kvw), lambda b, n: (b, 0, 0))
    return pl.pallas_call(
        _swa_kernel, name="window_gqa",
        grid=(b, nb),
        in_specs=[
            pl.BlockSpec(memory_space=pltpu.SMEM),
            pl.BlockSpec((1, SWA_BLOCK, q.shape[2]), lambda b, n: (b, n, 0)),
            blk(-1), blk(0), blk(1), blk(-1), blk(0), blk(1), ctx, ctx,
        ],
        out_specs=pl.BlockSpec((1, SWA_BLOCK, q.shape[2]), lambda b, n: (b, n, 0)),
        out_shape=jax.ShapeDtypeStruct(q.shape, BF16),
        compiler_params=_cparams(("parallel", "parallel"), 32),
    )(sink, q, k, k, k, v, v, v, kx, vx)


def _na_bias_kernel(rpb_ref, o_ref):
    h = pl.program_id(0)
    nd = 2 * NA_COLS - 1
    nr = 2 * NA_MAX_ROWS - 1
    q = lax.broadcasted_iota(jnp.int32, (GRID_W, LANES), 0)
    lane = lax.broadcasted_iota(jnp.int32, (GRID_W, LANES), 1)
    kc = lane & (GRID_W - 1)
    upper = lane >= GRID_W
    cstart = jnp.clip(q - NA_COLS // 2, 0, GRID_W - NA_COLS)
    valid = (kc >= cstart) & (kc < cstart + NA_COLS)
    dc = jnp.clip(kc - q, -(NA_COLS - 1), NA_COLS - 1) + (NA_COLS - 1)
    tiles = []
    for dr in range(nr - 1):
        base0 = (h * nr + dr) * nd
        base1 = base0 + nd
        t = jnp.zeros((GRID_W, LANES), F32)
        for d in range(nd):
            val = jnp.where(upper, rpb_ref[base1 + d], rpb_ref[base0 + d])
            t = jnp.where(dc == d, val, t)
        tiles.append(jnp.where(valid, t, NEG_BIG))
    for off in range(NA_MAX_ROWS):
        for j in range(NA_MAX_ROWS // 2):
            o_ref[off, 0, :, j * LANES:(j + 1) * LANES] = tiles[2 * j - off + NA_MAX_ROWS - 1]


def _na_bias_table(rpb):
    return pl.pallas_call(
        _na_bias_kernel, name="na_bias_table",
        grid=(NA_HEADS,),
        in_specs=[pl.BlockSpec(memory_space=pltpu.SMEM)],
        out_specs=pl.BlockSpec((NA_MAX_ROWS, 1, GRID_W, NA_MAX_ROWS * GRID_W), lambda h: (0, h, 0, 0)),
        out_shape=jax.ShapeDtypeStruct((NA_MAX_ROWS, NA_HEADS, GRID_W, NA_MAX_ROWS * GRID_W), F32),
        compiler_params=_cparams(("arbitrary",), 16),
    )(rpb.reshape(-1))


NA_ROWS_PER_STEP = 4


def _na_kernel(q_ref, k_ref, v_ref, kx_ref, vx_ref, bias_ref, o_ref, *, rows):
    wr = NA_MAX_ROWS
    nloc = wr * GRID_W
    pairs = range(NA_HEADS // 2)
    lanes = [slice(p * LANES, (p + 1) * LANES) for p in pairs]
    for i in range(NA_ROWS_PER_STEP):
        r = pl.program_id(1) * NA_ROWS_PER_STEP + i
        first = jnp.clip(r - wr // 2, 0, rows - wr)
        start = pl.multiple_of(first * GRID_W, GRID_W)
        off = r - first
        qr = slice(i * GRID_W, (i + 1) * GRID_W)
        scores = []
        for p in pairs:
            q_tile = q_ref[0, qr, lanes[p]]
            q2 = jnp.concatenate([_half_mask(q_tile, 0), _half_mask(q_tile, 1)], axis=0)
            k = jnp.concatenate([k_ref[0, pl.ds(start, nloc), lanes[p]], kx_ref[0, :, lanes[p]]], axis=0)
            scores.append(_qk(q2, k))
        probs = []
        for p in pairs:
            s = scores[p]
            bias = jnp.concatenate([bias_ref[off, 2 * p], bias_ref[off, 2 * p + 1]], axis=0)
            probs.append(_softmax_weights(jnp.concatenate([s[:, :nloc] + bias, s[:, nloc:]], axis=1)))
        for p in pairs:
            pw, den = probs[p]
            v = jnp.concatenate([v_ref[0, pl.ds(start, nloc), lanes[p]], vx_ref[0, :, lanes[p]]], axis=0)
            o = _dot(pw, v) / den
            o_ref[0, qr, lanes[p]] = _merge_halves(o[:GRID_W], o[GRID_W:]).astype(o_ref.dtype)


def _neighbourhood_attention(q, k, v, kx, vx, bias_tab):
    b, l, w = q.shape
    rows = l // GRID_W
    n_ctx = kx.shape[1]
    tq = NA_ROWS_PER_STEP * GRID_W
    full = pl.BlockSpec((1, l, w), lambda b, r: (b, 0, 0))
    ctx = pl.BlockSpec((1, n_ctx, w), lambda b, r: (b, 0, 0))
    return pl.pallas_call(
        functools.partial(_na_kernel, rows=rows), name="neighbourhood_attention",
        grid=(b, rows // NA_ROWS_PER_STEP),
        in_specs=[pl.BlockSpec((1, tq, w), lambda b, r: (b, r, 0)), full, full, ctx, ctx,
                  _resident(bias_tab.shape)],
        out_specs=pl.BlockSpec((1, tq, w), lambda b, r: (b, r, 0)),
        out_shape=jax.ShapeDtypeStruct(q.shape, BF16),
        compiler_params=_cparams(("parallel", "arbitrary"), 48),
    )(q, k, v, kx, vx, bias_tab)


def _ctx_attn_kernel(sink_ref, qs_ref, ks_ref, vs_ref, qn_ref, kn_ref, vn_ref, os_ref, on_ref):
    n = qs_ref.shape[1]
    g = SWA_HEADS // SWA_KV_HEADS
    rsel = lax.broadcasted_iota(jnp.int32, (g * n, 1), 0) >> (n.bit_length() - 1)
    outs = [None] * SWA_HEADS
    for kv in range(SWA_KV_HEADS):
        cs = slice(kv * LANES, (kv + 1) * LANES)
        q_parts = []
        sink = jnp.zeros((g * n, 1), F32)
        for gi in range(g):
            h = kv * g + gi
            q_parts.append(_half_mask(qs_ref[0, :, (h // 2) * LANES:(h // 2 + 1) * LANES], h % 2))
            sink = jnp.where(rsel == gi, sink_ref[h], sink)
        o = _softmax_pv(_qk(jnp.concatenate(q_parts, axis=0), ks_ref[0, :, cs]), vs_ref[0, :, cs], sink)
        for gi in range(g):
            outs[kv * g + gi] = o[gi * n:(gi + 1) * n]
    for p in range(SWA_HEADS // 2):
        os_ref[0, :, p * LANES:(p + 1) * LANES] = _merge_halves(outs[2 * p], outs[2 * p + 1]).astype(os_ref.dtype)
    for p in range(NA_HEADS // 2):
        cs = slice(p * LANES, (p + 1) * LANES)
        q_tile = qn_ref[0, :, cs]
        q2 = jnp.concatenate([_half_mask(q_tile, 0), _half_mask(q_tile, 1)], axis=0)
        o = _softmax_pv(_qk(q2, kn_ref[0, :, cs]), vn_ref[0, :, cs])
        on_ref[0, :, cs] = _merge_halves(o[:n], o[n:]).astype(on_ref.dtype)


def _context_attention(qs, ks, vs, qn, kn, vn, sink):
    b, n, w = qs.shape
    spec = lambda a: pl.BlockSpec((1,) + a.shape[1:], lambda b: (b, 0, 0))
    return pl.pallas_call(
        _ctx_attn_kernel, name="context_attention",
        grid=(b,),
        in_specs=[pl.BlockSpec(memory_space=pltpu.SMEM)] + [spec(a) for a in (qs, ks, vs, qn, kn, vn)],
        out_specs=[spec(qs), spec(qn)],
        out_shape=[jax.ShapeDtypeStruct(qs.shape, BF16), jax.ShapeDtypeStruct(qn.shape, BF16)],
        compiler_params=_cparams(("parallel",), 32),
    )(sink, qs, ks, vs, qn, kn, vn)


def _merge_ffn_kernel(x_ref, mg_ref, yh_ref, ys_ref, yn_ref, zg_ref, wb_ref, wo_ref,
                      sh_ref, sc_ref, gt_ref, g_ref, wg_ref, wu_ref, wd_ref, o_ref, *, fc):
    d = D_MODEL
    yh = jnp.concatenate([yh_ref[0, c] for c in range(yh_ref.shape[1])], axis=1)
    ys = (yh.astype(BF16), ys_ref[0], yn_ref[0])
    m = None
    for n in range(3):
        proj = _dot(ys[n], wb_ref[n])
        term = zg_ref[0, :, n * d:(n + 1) * d].astype(F32) * proj
        m = term if m is None else m + term
    x = x_ref[0] + mg_ref[0] * _dot(m.astype(BF16), wo_ref[...])
    o_ref[0] = _swiglu_residual(x, sh_ref, sc_ref, gt_ref, g_ref, wg_ref, wu_ref, wd_ref, fc)


def _merge_and_ffn(x, mix_gate, y_hy, y_swa, y_na, gates, w_branch, w_out, layer,
                   shift, scale, gate, gain, wg, wu, wd):
    b, l, d = x.shape
    tm = min(512, l)
    tok = lambda n: pl.BlockSpec((1, tm, n), lambda b, t: (b, t, 0))
    idx = (layer, 1)
    return pl.pallas_call(
        functools.partial(_merge_ffn_kernel, fc=FFN_CHUNK), name="merge_ffn",
        grid=(b, l // tm),
        in_specs=[tok(d), _mod_spec(mix_gate),
                  pl.BlockSpec((1, HY_WIDTH // HY_CHUNK, tm, HY_CHUNK), lambda b, t: (b, 0, t, 0)),
                  tok(HY_WIDTH), tok(HY_WIDTH), tok(3 * d),
                  _layer_slab(w_branch, (layer,)), _layer_slab(w_out, (layer,)),
                  _mod_spec(shift), _mod_spec(scale), _mod_spec(gate), _resident((1, d)),
                  _layer_slab(wg, idx), _layer_slab(wu, idx), _layer_slab(wd, idx)],
        out_specs=tok(d),
        out_shape=jax.ShapeDtypeStruct(x.shape, F32),
        compiler_params=_cparams(("parallel", "parallel"), 58),
    )(x, mix_gate[0], y_hy, y_swa, y_na, gates, w_branch, w_out, shift[0], scale[0], gate[0], gain.reshape(1, d),
      wg, wu, wd)


def _dft_tables(l):
    n = 2 * l
    fb = 256 if l >= 1024 else LANES * ((l + 1 + LANES - 1) // LANES)
    fp = fb * ((l + 1 + fb - 1) // fb)
    k = jnp.arange(fp, dtype=jnp.int32)[:, None]
    t = jnp.arange(l, dtype=jnp.int32)[None, :]
    ang = ((k * t) % n).astype(F32) * (2.0 * math.pi / n)
    live = k <= l
    c = jnp.where(live, jnp.cos(ang), 0.0)
    s = jnp.where(live, jnp.sin(ang), 0.0)
    c_hi = c.astype(BF16)
    s_hi = s.astype(BF16)
    wk = jnp.where((k == 0) | (k == l), 1.0, 2.0) * jnp.where(live, 1.0 / n, 0.0)
    return dict(fb=fb, c_hi=c_hi, s_hi=s_hi, ct_hi=c_hi.T, st_hi=s_hi.T, wk=wk.astype(F32))


def _filter_features(l):
    t = jnp.linspace(0.0, 1.0, l, dtype=F32)[:, None]
    w = (2.0 * math.pi / l) * jnp.arange(l, dtype=F32)[:, None]
    f = jnp.linspace(1e-4, HY_BANDS - 1, HY_BANDS, dtype=F32)[None, :]
    z = jnp.concatenate([t, jnp.cos(f * w), -jnp.sin(f * w)], axis=-1)
    return jnp.pad(z, ((0, 0), (0, LANES - HY_EMB)))


def _decay_rates():
    max_decay = math.log(HY_TARGET) / HY_FAST_DECAY
    min_decay = math.log(HY_TARGET) / HY_SLOW_DECAY
    return jnp.abs(jnp.linspace(min_decay, max_decay, HY_WIDTH, dtype=F32))[None, :]


def _rope_tables(l):
    pos = jnp.arange(l)
    row = (pos // GRID_W).astype(F32)
    col = (pos % GRID_W).astype(F32)
    half = HEAD_DIM // 2
    inv = 1.0 / (ROPE_BASE ** (jnp.arange(0, half, 2, dtype=F32) / half))
    ar = row[:, None] * inv[None, :]
    ac = col[:, None] * inv[None, :]
    cos = jnp.concatenate([jnp.cos(ar), jnp.cos(ar), jnp.cos(ac), jnp.cos(ac)], axis=-1)
    sin = jnp.concatenate([-jnp.sin(ar), jnp.sin(ar), -jnp.sin(ac), jnp.sin(ac)], axis=-1)
    return jnp.tile(cos, (1, 2)), jnp.tile(sin, (1, 2))


def _pad_to(a, shape):
    return jnp.pad(a, [(0, s - d) for d, s in zip(a.shape, shape)])


def _use_fft(bsz, length):
    return bsz % 2 == 0 and length % (8 * FFT_INNER) == 0


def _conv_tables(bsz, length):
    return _fft_tables(length) if _use_fft(bsz, length) else _dft_tables(length)


def _hyena_branch(z_hy, short_w, short_b, feat, fparams, deltas, bias, tabs):
    ad = _hyena_filter_sums(feat, fparams, deltas)
    if "n2" in tabs:
        kr, ki = _fft_filter_spectrum(ad, tabs)
        y1 = _fft_gated_long_conv(None, 0, z_hy, 1, short_w, short_b, kr, ki, 0, bias[0], tabs, F32)
        return _fft_gated_long_conv(y1, 0, z_hy, 2, short_w, short_b, kr, ki, 1, bias[1], tabs, BF16)
    v, x1, x2 = _short_conv3(z_hy, short_w, short_b)
    kr, ki = _filter_spectrum(ad, tabs)
    conv = functools.partial(_gated_long_conv, tabs=tabs)
    y1 = conv(v, x1, kr, ki, 0, bias[0])
    y = conv(y1, x2, kr, ki, 1, bias[1])
    bsz, length, _ = y.shape
    return y.reshape(bsz, length, HY_WIDTH // HY_CHUNK, HY_CHUNK).transpose(0, 2, 1, 3)


def kernel(x, c, ctx, c_ctx, w_ada, b_ada, norm_g, ffn_w_gate, ffn_w_up, ffn_w_down,
           w_in, hy_short_w, hy_short_b, hy_pe_w0, hy_pe_b0, hy_pe_w1, hy_pe_b1,
           hy_pe_w2, hy_pe_b2, hy_pe_wout, hy_sin_freq, hy_bias,
           swa_q_gain, swa_k_gain, swa_sink, na_q_gain, na_k_gain, na_rpb,
           w_branch, w_out):
    bsz, seq, d = x.shape
    n_ctx = ctx.shape[1]
    depth = w_ada.shape[0]

    c16 = _pad_to(jnp.concatenate([c, c_ctx[None, :]], axis=0), (MOD_ROWS, d))
    mods = _adaln_mods(c16, w_ada, b_ada).reshape(depth, MOD_ROWS, N_MOD, 1, d)

    tabs_x = _conv_tables(bsz, seq)
    tabs_c = _conv_tables(bsz, n_ctx)
    feat_x = _filter_features(seq)
    feat_c = _filter_features(n_ctx)
    deltas = _decay_rates()
    cos_x, sin_x = _rope_tables(seq)
    cos_c = jnp.ones((n_ctx, LANES), F32)
    sin_c = jnp.zeros((n_ctx, LANES), F32)
    eye = jnp.arange(MXU_DIM) // HEAD_DIM
    bd = (eye[:, None] == eye[None, :]).astype(BF16)

    wg = ffn_w_gate.astype(BF16)
    wu = ffn_w_up.astype(BF16)
    wd = ffn_w_down.astype(BF16)
    wb = w_branch.astype(BF16)
    wo = w_out.astype(BF16)
    win = w_in.astype(BF16)
    kv = win[:, :, _OFF_SWA + _SWA_Q:_OFF_NA].reshape(depth, d, 2, SWA_KV_HEADS, 1, HEAD_DIM)
    win_kv = jnp.broadcast_to(kv, (depth, d, 2, SWA_KV_HEADS, 2, HEAD_DIM)).reshape(depth, d, 4 * _SWA_KV)

    xc = ctx
    for i in range(depth):
        last = i == depth - 1
        mx = lambda j: (mods, i, j, None)
        mc = lambda j: (mods, i, j, bsz)
        tile2 = lambda g: jnp.tile(g, 2 * MXU_DIM // LANES)
        head_gains = jnp.stack([tile2(swa_q_gain[i]), tile2(swa_k_gain[i]), tile2(na_q_gain[i]), tile2(na_k_gain[i])])
        fparams = (_pad_to(hy_pe_w0[i], (LANES, LANES)), _pad_to(hy_pe_b0[i][None], (1, LANES)),
                   _pad_to(hy_pe_w1[i], (LANES, LANES)), _pad_to(hy_pe_b1[i][None], (1, LANES)),
                   _pad_to(hy_pe_w2[i], (LANES, LANES)), _pad_to(hy_pe_b2[i][None], (1, LANES)),
                   _pad_to(hy_pe_wout[i], (LANES, 4 * HY_WIDTH)), _pad_to(hy_sin_freq[i][None], (1, LANES)))

        x = _ffn_half_step(x, mx(0), mx(1), mx(2), norm_g[i, 0], wg, wu, wd, (i, 0))
        xc = _ffn_half_step(xc, mc(0), mc(1), mc(2), norm_g[i, 0], wg, wu, wd, (i, 0))

        z_hy, q_s, k_s, v_s, q_n, k_n, v_n, gates = _in_projection(
            x, mx(3), mx(4), norm_g[i, 1], cos_x, sin_x, head_gains, bd, win, win_kv, i)
        zc_hy, qc_s, kc_s, vc_s, qc_n, kc_n, vc_n, gates_c = _in_projection(
            xc, mc(3), mc(4), norm_g[i, 1], cos_c, sin_c, head_gains, bd, win, win_kv, i)

        y_hy = _hyena_branch(z_hy, hy_short_w[i], hy_short_b[i], feat_x, fparams, deltas, hy_bias[i], tabs_x)
        y_swa = _window_gqa(q_s, k_s, v_s, kc_s, vc_s, swa_sink[i])
        y_na = _neighbourhood_attention(q_n, k_n, v_n, kc_n, vc_n, _na_bias_table(na_rpb[i]))
        x = _merge_and_ffn(x, mx(5), y_hy, y_swa, y_na, gates, wb, wo, i,
                           mx(6), mx(7), mx(8), norm_g[i, 2], wg, wu, wd)

        if not last:
            yc_hy = _hyena_branch(zc_hy, hy_short_w[i], hy_short_b[i], feat_c, fparams, deltas, hy_bias[i], tabs_c)
            yc_swa, yc_na = _context_attention(qc_s, kc_s, vc_s, qc_n, kc_n, vc_n, swa_sink[i])
            xc = _merge_and_ffn(xc, mc(5), yc_hy, yc_swa, yc_na, gates_c, wb, wo, i,
                                mc(6), mc(7), mc(8), norm_g[i, 2], wg, wu, wd)
    return x
```

```python
import functools
import math

import jax
import jax.numpy as jnp
from jax import lax
from jax.experimental import pallas as pl
from jax.experimental.pallas import tpu as pltpu

F32 = jnp.float32
BF16 = jnp.bfloat16

D_MODEL = 1024
DEPTH = 4
GRID_W = 64
HEAD_DIM = 64
N_MOD = 9
RMS_EPS = 1e-6

HY_WIDTH = D_MODEL // 2
HY_CHUNK = 256
HY_EMB = 33
HY_BANDS = (HY_EMB - 1) // 2
HY_FFN = 64
HY_FAST_DECAY = 0.3
HY_SLOW_DECAY = 1.5
HY_TARGET = 1e-2

SWA_HEADS = 8
SWA_KV_HEADS = 2
SWA_WINDOW = 128
SWA_BLOCK = 128
ROPE_BASE = 10000.0

NA_HEADS = 8
NA_MAX_ROWS = 8
NA_COLS = 16
NA_COL_BLOCK = 16

FFN_HIDDEN = 256 * ((8 * D_MODEL // 3 + 255) // 256)
FFN_CHUNK = 1536

LANES = 128
MXU_DIM = 256
MOD_ROWS = 16
NEG_BIG = -1e30


def _cparams(sem, vmem_mb):
    return pltpu.CompilerParams(dimension_semantics=sem, vmem_limit_bytes=vmem_mb * 1024 * 1024)


def _resident(shape):
    nd = len(shape)
    return pl.BlockSpec(shape, lambda *_: (0,) * nd, pipeline_mode=pl.Buffered(1))


def _layer_slab(arr, idx):
    rest = arr.shape[len(idx):]
    return pl.BlockSpec((None,) * len(idx) + rest, lambda *_: tuple(idx) + (0,) * len(rest),
                        pipeline_mode=pl.Buffered(1))


def _split(x):
    hi = x.astype(BF16)
    lo = (x - hi.astype(F32)).astype(BF16)
    return hi, lo


def _dot(a, b):
    return jnp.dot(a, b, preferred_element_type=F32)


def _dot3(a_hi, a_lo, b_hi, b_lo):
    return _dot(a_hi, b_hi) + _dot(a_lo, b_hi) + _dot(a_hi, b_lo)


def _dot3f(a, b):
    a_hi, a_lo = _split(a)
    b_hi, b_lo = _split(b)
    return _dot3(a_hi, a_lo, b_hi, b_lo)


def _rms_mod(x, gain, shift, scale):
    ms = jnp.mean(x * x, axis=-1, keepdims=True)
    y = x * lax.rsqrt(ms + RMS_EPS) * gain
    return y * (1.0 + scale) + shift


def _head_sumsq(x, bd):
    return [_dot(jnp.square(x[:, j * MXU_DIM:(j + 1) * MXU_DIM]).astype(BF16), bd) for j in range(x.shape[1] // MXU_DIM)]


def _head_normalize(x, sumsq, gain):
    outs = [x[:, j * MXU_DIM:(j + 1) * MXU_DIM] * lax.rsqrt(s * (1.0 / HEAD_DIM) + RMS_EPS) * gain
            for j, s in enumerate(sumsq)]
    return outs[0] if len(outs) == 1 else jnp.concatenate(outs, axis=1)


def _rope(x, cos, sin):
    lane = lax.broadcasted_iota(jnp.int32, (x.shape[0], LANES), 1)
    first = (lane & 31) < 16
    outs = []
    for j in range(x.shape[1] // LANES):
        xc = x[:, j * LANES:(j + 1) * LANES]
        partner = jnp.where(first, pltpu.roll(xc, LANES - 16, axis=1), pltpu.roll(xc, 16, axis=1))
        outs.append(xc * cos + partner * sin)
    return outs[0] if len(outs) == 1 else jnp.concatenate(outs, axis=1)


def _mods_kernel(c_ref, w_ref, b_ref, o_ref):
    c = c_ref[...]
    a = c * jax.nn.sigmoid(c)
    o_ref[0] = _dot3f(a, w_ref[0]) + b_ref[0]


def _adaln_mods(c16, w_ada, b_ada):
    depth, d, n = w_ada.shape
    nb = 1152
    return pl.pallas_call(
        _mods_kernel, name="adaln_mods",
        grid=(depth, n // nb),
        in_specs=[
            pl.BlockSpec((MOD_ROWS, d), lambda i, j: (0, 0)),
            pl.BlockSpec((1, d, nb), lambda i, j: (i, 0, j)),
            pl.BlockSpec((1, 1, nb), lambda i, j: (i, 0, j)),
        ],
        out_specs=pl.BlockSpec((1, MOD_ROWS, nb), lambda i, j: (i, 0, j)),
        out_shape=jax.ShapeDtypeStruct((depth, MOD_ROWS, n), F32),
        compiler_params=_cparams(("arbitrary", "arbitrary"), 48),
    )(c16, w_ada, b_ada.reshape(depth, 1, n))


def _swiglu_residual(x, sh_ref, sc_ref, gt_ref, g_ref, wg_ref, wu_ref, wd_ref, fc):
    hb = _rms_mod(x, g_ref[...], sh_ref[0], sc_ref[0]).astype(BF16)
    acc = None
    for f0 in range(0, FFN_HIDDEN, fc):
        f1 = min(f0 + fc, FFN_HIDDEN)
        g = _dot(hb, wg_ref[:, f0:f1])
        u = _dot(hb, wu_ref[:, f0:f1])
        a = (g * jax.nn.sigmoid(g) * u).astype(BF16)
        d = _dot(a, wd_ref[f0:f1, :])
        acc = d if acc is None else acc + d
    return x + 0.5 * gt_ref[0] * acc


def _ffn_kernel(x_ref, sh_ref, sc_ref, gt_ref, g_ref, wg_ref, wu_ref, wd_ref, o_ref, *, fc):
    o_ref[0] = _swiglu_residual(x_ref[0], sh_ref, sc_ref, gt_ref, g_ref, wg_ref, wu_ref, wd_ref, fc)


def _mod_spec(sel):
    _, layer, j, row = sel
    if row is None:
        return pl.BlockSpec((None, 1, None, 1, D_MODEL), lambda b, t: (layer, b, j, 0, 0))
    return pl.BlockSpec((None, 1, None, 1, D_MODEL), lambda b, t: (layer, row, j, 0, 0))


def _ffn_half_step(x, shift, scale, gate, gain, wg, wu, wd, idx):
    b, l, d = x.shape
    tm = min(512, l)
    return pl.pallas_call(
        functools.partial(_ffn_kernel, fc=FFN_CHUNK), name="ffn_half_step",
        grid=(b, l // tm),
        in_specs=[
            pl.BlockSpec((1, tm, d), lambda b, t: (b, t, 0)),
            _mod_spec(shift), _mod_spec(scale), _mod_spec(gate),
            _resident((1, d)),
            _layer_slab(wg, idx), _layer_slab(wu, idx), _layer_slab(wd, idx),
        ],
        out_specs=pl.BlockSpec((1, tm, d), lambda b, t: (b, t, 0)),
        out_shape=jax.ShapeDtypeStruct(x.shape, F32),
        compiler_params=_cparams(("parallel", "parallel"), 56),
    )(x, shift[0], scale[0], gate[0], gain.reshape(1, d), wg, wu, wd)


_HY_COLS = 3 * HY_WIDTH
_SWA_Q = SWA_HEADS * HEAD_DIM
_SWA_KV = SWA_KV_HEADS * HEAD_DIM
_NA_W = NA_HEADS * HEAD_DIM
_OFF_SWA = _HY_COLS
_OFF_NA = _OFF_SWA + _SWA_Q + 2 * _SWA_KV
_OFF_GATE = _OFF_NA + 3 * _NA_W


def _inproj_kernel(x_ref, sh_ref, sc_ref, g_ref, cos_ref, sin_ref, hg_ref, bd_ref, w_ref, wkv_ref,
                   zhy_ref, qs_ref, ks_ref, vs_ref, qn_ref, kn_ref, vn_ref, gt_ref):
    hb = _rms_mod(x_ref[0], g_ref[...], sh_ref[0], sc_ref[0]).astype(BF16)
    cos = cos_ref[...]
    sin = sin_ref[...]
    bd = bd_ref[...]
    scale = HEAD_DIM ** -0.5
    proj = lambda lo, n: _dot(hb, w_ref[:, lo:lo + n])
    kvw = 2 * _SWA_KV
    raw = [proj(_OFF_SWA, _SWA_Q), _dot(hb, wkv_ref[:, :kvw]), proj(_OFF_NA, _NA_W), proj(_OFF_NA + _NA_W, _NA_W)]
    gates = proj(_OFF_GATE, 3 * D_MODEL)
    sums = [_head_sumsq(r, bd) for r in raw]
    vs_ref[0] = _dot(hb, wkv_ref[:, kvw:]).astype(BF16)
    vn_ref[0] = proj(_OFF_NA + 2 * _NA_W, _NA_W).astype(BF16)
    zhy = proj(0, _HY_COLS).astype(BF16)
    for c in range(_HY_COLS // HY_CHUNK):
        zhy_ref[0, c] = zhy[:, c * HY_CHUNK:(c + 1) * HY_CHUNK]
    gt_ref[0] = jax.nn.sigmoid(gates).astype(BF16)
    qs, ks, qn, kn = [_head_normalize(r, s, hg_ref[i:i + 1, :]) for i, (r, s) in enumerate(zip(raw, sums))]
    qs_ref[0] = (_rope(qs, cos, sin) * scale).astype(BF16)
    ks_ref[0] = _rope(ks, cos, sin).astype(BF16)
    qn_ref[0] = (qn * scale).astype(BF16)
    kn_ref[0] = kn.astype(BF16)


def _in_projection(x, shift, scale, gain, cos, sin, head_gains, bd, w, w_kv_dup, layer):
    b, l, d = x.shape
    tm = min(512, l)
    kvw = 2 * _SWA_KV
    widths = [_HY_COLS, _SWA_Q, kvw, kvw, _NA_W, _NA_W, _NA_W, 3 * d]
    dtypes = [BF16] * 8
    return pl.pallas_call(
        _inproj_kernel, name="in_projection",
        grid=(b, l // tm),
        in_specs=[
            pl.BlockSpec((1, tm, d), lambda b, t: (b, t, 0)),
            _mod_spec(shift), _mod_spec(scale),
            _resident((1, d)),
            pl.BlockSpec((tm, LANES), lambda b, t: (t, 0)),
            pl.BlockSpec((tm, LANES), lambda b, t: (t, 0)),
            _resident(head_gains.shape), _resident(bd.shape), _layer_slab(w, (layer,)), _layer_slab(w_kv_dup, (layer,)),
        ],
        out_specs=[pl.BlockSpec((1, _HY_COLS // HY_CHUNK, tm, HY_CHUNK), lambda b, t: (b, 0, t, 0))]
                  + [pl.BlockSpec((1, tm, n), lambda b, t: (b, t, 0)) for n in widths[1:]],
        out_shape=[jax.ShapeDtypeStruct((b, _HY_COLS // HY_CHUNK, l, HY_CHUNK), BF16)]
                  + [jax.ShapeDtypeStruct((b, l, n), dt) for n, dt in zip(widths[1:], dtypes[1:])],
        compiler_params=_cparams(("parallel", "parallel"), 56),
    )(x, shift[0], scale[0], gain.reshape(1, d), cos, sin, head_gains, bd, w, w_kv_dup)


def _shortconv_kernel(z0_ref, z1_ref, z2_ref, w_ref, b_ref, o0_ref, o1_ref, o2_ref):
    nb, _, l, _ = z0_ref.shape
    row = lax.broadcasted_iota(jnp.int32, z0_ref.shape[2:], 0)
    for g, (z_ref, o_ref) in enumerate(((z0_ref, o0_ref), (z1_ref, o1_ref), (z2_ref, o2_ref))):
        for i in range(nb):
            z = z_ref[i, 0].astype(F32)
            prev = jnp.where(row == 0, 0.0, pltpu.roll(z, 1, axis=0))
            nxt = jnp.where(row == l - 1, 0.0, pltpu.roll(z, l - 1, axis=0))
            o_ref[i] = w_ref[g, 0:1, :] * prev + w_ref[g, 1:2, :] * z + w_ref[g, 2:3, :] * nxt + b_ref[g]


def _short_conv3(z, w, bias):
    b, _, l, cb = z.shape
    ncb = HY_WIDTH // cb
    wg = w.reshape(3, 3, HY_WIDTH).transpose(1, 0, 2)
    zin = lambda g: pl.BlockSpec((b, 1, l, cb), lambda j: (0, g * ncb + j, 0, 0))
    out = pl.BlockSpec((b, l, cb), lambda j: (0, 0, j))
    return pl.pallas_call(
        _shortconv_kernel, name="short_conv3",
        grid=(ncb,),
        in_specs=[zin(0), zin(1), zin(2),
                  pl.BlockSpec((3, 3, cb), lambda j: (0, 0, j)),
                  pl.BlockSpec((3, 1, cb), lambda j: (0, 0, j))],
        out_specs=[out, out, out],
        out_shape=[jax.ShapeDtypeStruct((b, l, HY_WIDTH), F32)] * 3,
        compiler_params=_cparams(("parallel",), 48),
    )(z, z, z, wg, bias.reshape(3, 1, HY_WIDTH))


def _filter_kernel(z_ref, w0, b0, w1, b1, w2, b2, wo, fr_ref, dl_ref, o_ref):
    z = z_ref[...]
    fr = fr_ref[...]
    a = jnp.sin(fr * (_dot3f(z, w0[...]) + b0[...]))
    a = jnp.sin(fr * (_dot3f(a, w1[...]) + b1[...]))
    a = jnp.sin(fr * (_dot3f(a, w2[...]) + b2[...]))
    hh = _dot3f(a, wo[...])
    t = z[:, 0:1]
    win = jnp.exp(-t * dl_ref[...])
    row = lax.broadcasted_iota(jnp.int32, win.shape, 0) + pl.program_id(0) * z.shape[0]
    w = HY_WIDTH
    for o in range(2):
        hp = hh[:, (2 * o) * w:(2 * o + 1) * w] * win
        hn = jnp.where(row == 0, 0.0, hh[:, (2 * o + 1) * w:(2 * o + 2) * w] * win)
        o_ref[0, :, o * w:(o + 1) * w] = hp + hn
        o_ref[1, :, o * w:(o + 1) * w] = hp - hn


def _hyena_filter_sums(zfeat, fp, deltas):
    l = zfeat.shape[0]
    tl = min(256, l)
    w0, b0, w1, b1, w2, b2, wo, fr = fp
    n = 2 * HY_WIDTH
    consts = [w0, b0, w1, b1, w2, b2, wo, fr, deltas]
    return pl.pallas_call(
        _filter_kernel, name="hyena_filter",
        grid=(l // tl,),
        in_specs=[pl.BlockSpec((tl, LANES), lambda t: (t, 0))] + [_resident(c.shape) for c in consts],
        out_specs=pl.BlockSpec((2, tl, n), lambda t: (0, t, 0)),
        out_shape=jax.ShapeDtypeStruct((2, l, n), F32),
        compiler_params=_cparams(("parallel",), 32),
    )(zfeat, *consts)


def _spectrum_kernel(a_ref, d_ref, c_ref, s_ref, kr_ref, ki_ref):
    kr_ref[...] = _dot(c_ref[...], a_ref[0].astype(BF16))
    ki_ref[...] = -_dot(s_ref[...], d_ref[0].astype(BF16))


def _filter_spectrum(ad, tabs):
    _, l, n = ad.shape
    fp = tabs["c_hi"].shape[0]
    fb = tabs["fb"]
    cb = 256
    tab = pl.BlockSpec((fb, l), lambda j, k: (k, 0))
    out = pl.BlockSpec((fb, cb), lambda j, k: (k, j))
    return pl.pallas_call(
        _spectrum_kernel, name="dense_spectrum",
        grid=(n // cb, fp // fb),
        in_specs=[pl.BlockSpec((1, l, cb), lambda j, k: (0, 0, j)),
                  pl.BlockSpec((1, l, cb), lambda j, k: (1, 0, j)), tab, tab],
        out_specs=[out, out],
        out_shape=[jax.ShapeDtypeStruct((fp, n), F32)] * 2,
        compiler_params=_cparams(("parallel", "parallel"), 48),
    )(ad, ad, tabs["c_hi"], tabs["s_hi"])


def _longconv_kernel(u_ref, g_ref, kr_ref, ki_ref, bias_ref, wk_ref, c_ref, s_ref, ct_ref, st_ref,
                     o_ref, ub_ref, acc_ref):
    kb = pl.program_id(1)
    nb, _, cb = u_ref.shape

    @pl.when(kb == 0)
    def _():
        ub_ref[...] = jnp.concatenate([u_ref[i].astype(BF16) for i in range(nb)], axis=1)
        acc_ref[...] = jnp.zeros_like(acc_ref)

    ub = ub_ref[...]
    xr = _dot(c_ref[...], ub)
    xi = -_dot(s_ref[...], ub)
    kr = jnp.tile(kr_ref[...], (1, nb))
    ki = jnp.tile(ki_ref[...], (1, nb))
    wk = wk_ref[...]
    yr = ((xr * kr - xi * ki) * wk).astype(BF16)
    yi = ((xr * ki + xi * kr) * wk).astype(BF16)
    acc_ref[...] += _dot(ct_ref[...], yr) - _dot(st_ref[...], yi)

    @pl.when(kb == pl.num_programs(1) - 1)
    def _():
        for i in range(nb):
            o_ref[i] = g_ref[i] * (acc_ref[:, i * cb:(i + 1) * cb] + u_ref[i] * bias_ref[...])


def _gated_long_conv(u_arr, g_arr, kr, ki, k_col, bias, tabs):
    b, l, _ = u_arr.shape
    cb = 256
    ncb = HY_WIDTH // cb
    fp = tabs["c_hi"].shape[0]
    fb = tabs["fb"]
    tab = pl.BlockSpec((fb, l), lambda j, k: (k, 0))
    tabt = pl.BlockSpec((l, fb), lambda j, k: (0, k))
    spec = pl.BlockSpec((fb, cb), lambda j, k: (k, k_col * ncb + j))
    tok = pl.BlockSpec((b, l, cb), lambda j, k: (0, 0, j))
    return pl.pallas_call(
        _longconv_kernel, name="dense_long_conv",
        grid=(ncb, fp // fb),
        in_specs=[tok, tok, spec, spec,
                  pl.BlockSpec((1, cb), lambda j, k: (0, j)),
                  pl.BlockSpec((fb, 1), lambda j, k: (k, 0)),
                  tab, tab, tabt, tabt],
        out_specs=tok,
        out_shape=jax.ShapeDtypeStruct((b, l, HY_WIDTH), F32),
        scratch_shapes=[pltpu.VMEM((l, b * cb), BF16), pltpu.VMEM((l, b * cb), F32)],
        compiler_params=_cparams(("parallel", "arbitrary"), 56),
    )(u_arr, g_arr, kr, ki, bias.reshape(1, HY_WIDTH), tabs["wk"],
      tabs["c_hi"], tabs["s_hi"], tabs["ct_hi"], tabs["st_hi"])


FFT_INNER = 64
FFT_COLS = HY_CHUNK
FFT_K2_CHUNK = 16
FFT_UNROLL = 16


def _fft_tables(l):
    n = 2 * l
    n1 = FFT_INNER
    n2 = n // n1
    h = n2 // 2
    ang = lambda idx, mod: (idx % mod).astype(F32) * (2.0 * math.pi / mod)
    k2 = jnp.arange(n2, dtype=jnp.int32)
    m2 = jnp.arange(h, dtype=jnp.int32)
    a1 = ang(k2[:, None] * m2[None, :], n2)
    c1, s1 = jnp.cos(a1), jnp.sin(a1)
    f1 = jnp.block([[c1, s1], [-s1, c1]])
    k1 = jnp.arange(n1, dtype=jnp.int32)
    a2 = ang(k1[None, None, :] * (k1[None, :, None] * n2 + k2[:, None, None]), n)
    c2, s2 = jnp.cos(a2), jnp.sin(a2)
    m = jnp.concatenate([jnp.concatenate([c2, s2], axis=2), jnp.concatenate([-s2, c2], axis=2)], axis=1)
    g = jnp.block([[c1.T, -s1.T], [s1.T, c1.T]]) * (1.0 / n)
    out = dict(n2=n2)
    for name, t in (("f1", f1), ("f1r", f1[:, :h]), ("m", m), ("mt", jnp.swapaxes(m, 1, 2)), ("g", g)):
        out[name + "_hi"] = t.astype(BF16)
    return out


def _mm(w, x):
    return _dot(w, x.astype(BF16))


def _row_block_matmul(w, src_ref, dst_ref, count):
    m, k = w.shape

    def body(i, carry):
        x = src_ref[pl.ds(pl.multiple_of(i * k, k), k), :]
        dst_ref[pl.ds(pl.multiple_of(i * m, m), m), :] = _mm(w, x).astype(dst_ref.dtype)
        return carry
    lax.fori_loop(0, count, body, 0, unroll=FFT_UNROLL)


def _short_conv_rows(z, w_ref, b_ref):
    l = z.shape[0]
    row = lax.broadcasted_iota(jnp.int32, z.shape, 0)
    prev = jnp.where(row == 0, 0.0, pltpu.roll(z, 1, axis=0))
    nxt = jnp.where(row == l - 1, 0.0, pltpu.roll(z, l - 1, axis=0))
    return w_ref[0:1, :] * prev + w_ref[1:2, :] * z + w_ref[2:3, :] * nxt + b_ref[...]


def _fftconv_kernel(u_ref, g_ref, uw_ref, ub_ref, gw_ref, gb_ref, kr_ref, ki_ref, bias_ref,
                    f1_ref, m_ref, mt_ref, gm_ref, o_ref, z_scr, y_scr, p_scr, q_scr, u_scr, *, u_conv):
    kc = pl.program_id(2)
    n1 = FFT_INNER
    cc = z_scr.shape[1]
    n2 = z_scr.shape[0] // n1
    h = n2 // 2

    @pl.when(kc == 0)
    def _stage1():
        z = []
        for s in range(2):
            u = u_ref[s, 0]
            if u_conv:
                u = _short_conv_rows(u.astype(F32), uw_ref, ub_ref)
                u_scr[s] = u
            z.append(jnp.swapaxes(u.astype(BF16).reshape(h, n1, cc), 0, 1))
        z_scr[...] = jnp.concatenate(z, axis=1).reshape(n1 * n2, cc)
        _row_block_matmul(f1_ref[...], z_scr, p_scr, n1)
        q_scr[...] = jnp.swapaxes(p_scr[...].reshape(n1, 2 * n2, cc), 0, 1).reshape(2 * n2 * n1, cc)

    ks = kr_ref.shape[0]
    rows = [(pl.multiple_of((kc * ks + j) * n1, n1), pl.multiple_of((n2 + kc * ks + j) * n1, n1)) for j in range(ks)]
    xs = [_mm(m_ref[j], jnp.concatenate([q_scr[pl.ds(r0, n1), :], q_scr[pl.ds(i0, n1), :]], axis=0))
          for j, (r0, i0) in enumerate(rows)]
    ys = []
    for j, x in enumerate(xs):
        xr, xi = x[:n1], x[n1:]
        kr, ki = kr_ref[j], ki_ref[j]
        ys.append(jnp.concatenate([xr * kr - xi * ki, xr * ki + xi * kr], axis=0))
    for j, (r0, i0) in enumerate(rows):
        b = _mm(mt_ref[j], ys[j]).astype(p_scr.dtype)
        p_scr[pl.ds(r0, n1), :] = b[:n1]
        p_scr[pl.ds(i0, n1), :] = b[n1:]

    @pl.when(kc == pl.num_programs(2) - 1)
    def _stage1_inv():
        q_scr[...] = jnp.swapaxes(p_scr[...].reshape(2 * n2, n1, cc), 0, 1).reshape(2 * n2 * n1, cc)
        _row_block_matmul(gm_ref[...], q_scr, y_scr, n1)
        y3 = y_scr[...].reshape(n1, n2, cc)
        for s in range(2):
            y = jnp.swapaxes(y3[:, s * h:(s + 1) * h, :], 0, 1).reshape(h * n1, cc)
            gate = _short_conv_rows(g_ref[s, 0].astype(F32), gw_ref, gb_ref)
            u = u_scr[s] if u_conv else u_ref[s, 0]
            o_ref[s, 0] = (gate * (y + u * bias_ref[...])).astype(o_ref.dtype)


def _fft_gated_long_conv(u, u_col, z_hy, g_col, short_w, short_b, kr, ki, k_col, bias, ft, out_dtype):
    b, _, l, cc = z_hy.shape
    c = HY_WIDTH
    n1, n2, ks = FFT_INNER, ft["n2"], FFT_K2_CHUNK
    p, ncc = b // 2, c // cc
    u_conv = u is None
    if u_conv:
        u = z_hy
    tok = lambda col: pl.BlockSpec((2, 1, None, l, cc), lambda q, j, k: (0, q, col * ncc + j, 0, 0))
    taps = lambda col: pl.BlockSpec((3, cc), lambda q, j, k: (0, col * ncc + j))
    row = lambda col: pl.BlockSpec((1, cc), lambda q, j, k: (0, col * ncc + j))
    filt = pl.BlockSpec((ks, n1, cc), lambda q, j, k: (k, 0, k_col * ncc + j))
    tab = pl.BlockSpec((ks, 2 * n1, 2 * n1), lambda q, j, k: (k, 0, 0))
    res = lambda a: pl.BlockSpec(a.shape, lambda q, j, k: (0,) * a.ndim)
    sb = short_b.reshape(1, -1)
    y = pl.pallas_call(
        functools.partial(_fftconv_kernel, u_conv=u_conv), name="fft_long_conv",
        grid=(p, ncc, n2 // ks),
        in_specs=[tok(u_col), tok(g_col), taps(u_col), row(u_col), taps(g_col), row(g_col), filt, filt, row(0),
                  res(ft["f1_hi"]), tab, tab, res(ft["g_hi"])],
        out_specs=tok(0),
        out_shape=jax.ShapeDtypeStruct((2, p, ncc, l, cc), out_dtype),
        scratch_shapes=[pltpu.VMEM((n1 * n2, cc), BF16), pltpu.VMEM((n1 * n2, cc), F32),
                        pltpu.VMEM((2 * n2 * n1, cc), BF16), pltpu.VMEM((2 * n2 * n1, cc), BF16),
                        pltpu.VMEM((2, l, cc) if u_conv else (2, 8, LANES), F32)],
        compiler_params=_cparams(("parallel", "parallel", "arbitrary"), 56),
    )(u.reshape((2, p) + u.shape[1:]), z_hy.reshape((2, p) + z_hy.shape[1:]), short_w, sb, short_w, sb, kr, ki,
      bias.reshape(1, c), ft["f1_hi"], ft["m_hi"], ft["mt_hi"], ft["g_hi"])
    return y.reshape(b, ncc, l, cc)


def _fftspec_kernel(a_ref, d_ref, f1_ref, m_ref, kr_ref, ki_ref, z_scr, p_scr, q_scr):
    n1 = FFT_INNER
    cc = z_scr.shape[1]
    h = z_scr.shape[0] // n1
    n2 = 2 * h
    for src_ref, out_ref, lo in ((a_ref, kr_ref, 0), (d_ref, ki_ref, n1)):
        z_scr[...] = jnp.swapaxes(src_ref[0].astype(BF16).reshape(h, n1, cc), 0, 1).reshape(n1 * h, cc)
        _row_block_matmul(f1_ref[...], z_scr, p_scr, n1)
        q_scr[...] = jnp.swapaxes(p_scr[...].reshape(n1, 2 * n2, cc), 0, 1).reshape(2 * n2 * n1, cc)

        def body(k2, carry):
            r0 = pl.multiple_of(k2 * n1, n1)
            i0 = pl.multiple_of((n2 + k2) * n1, n1)
            slab = jnp.concatenate([q_scr[pl.ds(r0, n1), :], q_scr[pl.ds(i0, n1), :]], axis=0)
            out_ref[k2] = _mm(m_ref[k2][lo:lo + n1], slab)
            return carry
        lax.fori_loop(0, n2, body, 0, unroll=FFT_UNROLL)


def _fft_filter_spectrum(ad, ft):
    _, l, n = ad.shape
    n1, n2, cc = FFT_INNER, ft["n2"], FFT_COLS
    h = n2 // 2
    res = lambda a: pl.BlockSpec(a.shape, lambda j: (0,) * a.ndim)
    out = pl.BlockSpec((n2, n1, cc), lambda j: (0, 0, j))
    tabs = [ft["f1r_hi"], ft["m_hi"]]
    return pl.pallas_call(
        _fftspec_kernel, name="fft_filter_spectrum",
        grid=(n // cc,),
        in_specs=[pl.BlockSpec((1, l, cc), lambda j: (0, 0, j)), pl.BlockSpec((1, l, cc), lambda j: (1, 0, j))]
                 + [res(t) for t in tabs],
        out_specs=[out, out],
        out_shape=[jax.ShapeDtypeStruct((n2, n1, n), F32)] * 2,
        scratch_shapes=[pltpu.VMEM((n1 * h, cc), BF16), pltpu.VMEM((2 * n2 * n1, cc), BF16),
                        pltpu.VMEM((2 * n2 * n1, cc), BF16)],
        compiler_params=_cparams(("parallel",), 56),
    )(ad, ad, *tabs)


def _softmax_weights(s, sink=None):
    m = jnp.max(s, axis=-1, keepdims=True)
    if sink is not None:
        m = jnp.maximum(m, sink)
    p = jnp.exp(s - m)
    den = jnp.sum(p, axis=-1, keepdims=True)
    if sink is not None:
        den = den + jnp.exp(sink - m)
    return p.astype(BF16), den


def _softmax_pv(s, v, sink=None):
    p, den = _softmax_weights(s, sink)
    return _dot(p, v) / den


def _qk(q, k):
    return lax.dot_general(q, k, (((1,), (1,)), ((), ())), preferred_element_type=F32)


def _half_mask(q_tile, half):
    lane = lax.broadcasted_iota(jnp.int32, q_tile.shape, 1)
    keep = (lane < HEAD_DIM) if half == 0 else (lane >= HEAD_DIM)
    return jnp.where(keep, q_tile, jnp.zeros_like(q_tile))


def _merge_halves(o_even, o_odd):
    lane = lax.broadcasted_iota(jnp.int32, o_even.shape, 1)
    return jnp.where(lane < HEAD_DIM, o_even, o_odd)


SWA_CHAIN_HEADS = 2

def _swa_kernel(sink_ref, q_ref, kp_ref, kc_ref, kn_ref, vp_ref, vc_ref, vn_ref, kx_ref, vx_ref, o_ref):
    n = pl.program_id(1)
    nb = pl.num_programs(1)
    blk = SWA_BLOCK
    g = SWA_CHAIN_HEADS
    n_ctx = kx_ref.shape[1]
    rows = g * blk
    ri = lax.broadcasted_iota(jnp.int32, (rows, 3 * blk + n_ctx), 0) & (blk - 1)
    ci = lax.broadcasted_iota(jnp.int32, (rows, 3 * blk + n_ctx), 1)
    ok_prev = (ci < blk) & (ci >= ri) & (n > 0)
    ok_cur = (ci >= blk) & (ci < 2 * blk)
    ok_next = (ci >= 2 * blk) & (ci < 3 * blk) & (ci - 2 * blk <= ri) & (n < nb - 1)
    ok = ok_prev | ok_cur | ok_next | (ci >= 3 * blk)
    bias = jnp.where(ok, 0.0, NEG_BIG)
    rsel = lax.broadcasted_iota(jnp.int32, (rows, 1), 0) >> (blk.bit_length() - 1)
    outs = [None] * SWA_HEADS
    chains = range(SWA_HEADS // g)
    group = SWA_HEADS // SWA_KV_HEADS
    lanes = [slice((c * g // group) * LANES, (c * g // group + 1) * LANES) for c in chains]
    scores, sinks = [], []
    for c in chains:
        cs = lanes[c]
        k = jnp.concatenate([kp_ref[0, :, cs], kc_ref[0, :, cs], kn_ref[0, :, cs], kx_ref[0, :, cs]], axis=0)
        q_parts = []
        sink = jnp.zeros((rows, 1), F32)
        for gi in range(g):
            h = c * g + gi
            q_parts.append(_half_mask(q_ref[0, :, (h // 2) * LANES:(h // 2 + 1) * LANES], h % 2))
            sink = jnp.where(rsel == gi, sink_ref[h], sink)
        scores.append(_qk(jnp.concatenate(q_parts, axis=0), k))
        sinks.append(sink)
    probs = [_softmax_weights(scores[c] + bias, sinks[c]) for c in chains]
    for c in chains:
        cs = lanes[c]
        v = jnp.concatenate([vp_ref[0, :, cs], vc_ref[0, :, cs], vn_ref[0, :, cs], vx_ref[0, :, cs]], axis=0)
        pw, den = probs[c]
        o = _dot(pw, v) / den
        for gi in range(g):
            outs[c * g + gi] = o[gi * blk:(gi + 1) * blk]
    for p in range(SWA_HEADS // 2):
        o_ref[0, :, p * LANES:(p + 1) * LANES] = _merge_halves(outs[2 * p], outs[2 * p + 1]).astype(o_ref.dtype)


def _window_gqa(q, k, v, kx, vx, sink):
    b, l, _ = q.shape
    nb = l // SWA_BLOCK
    n_ctx = kx.shape[1]
    kvw = k.shape[2]

    def blk(off):
        return pl.BlockSpec((1, SWA_BLOCK, kvw), lambda b, n: (b, jnp.clip(n + off, 0, nb - 1), 0))

    ctx = pl.BlockSpec((1, n_ctx, kvw), lambda b, n: (b, 0, 0))
    return pl.pallas_call(
        _swa_kernel, name="window_gqa",
        grid=(b, nb),
        in_specs=[
            pl.BlockSpec(memory_space=pltpu.SMEM),
            pl.BlockSpec((1, SWA_BLOCK, q.shape[2]), lambda b, n: (b, n, 0)),
            blk(-1), blk(0), blk(1), blk(-1), blk(0), blk(1), ctx, ctx,
        ],
        out_specs=pl.BlockSpec((1, SWA_BLOCK, q.shape[2]), lambda b, n: (b, n, 0)),
        out_shape=jax.ShapeDtypeStruct(q.shape, BF16),
        compiler_params=_cparams(("parallel", "parallel"), 32),
    )(sink, q, k, k, k, v, v, v, kx, vx)


def _na_bias_kernel(rpb_ref, o_ref):
    h = pl.program_id(0)
    nd = 2 * NA_COLS - 1
    nr = 2 * NA_MAX_ROWS - 1
    q = lax.broadcasted_iota(jnp.int32, (GRID_W, LANES), 0)
    lane = lax.broadcasted_iota(jnp.int32, (GRID_W, LANES), 1)
    kc = lane & (GRID_W - 1)
    upper = lane >= GRID_W
    cstart = jnp.clip(q - NA_COLS // 2, 0, GRID_W - NA_COLS)
    valid = (kc >= cstart) & (kc < cstart + NA_COLS)
    dc = jnp.clip(kc - q, -(NA_COLS - 1), NA_COLS - 1) + (NA_COLS - 1)
    tiles = []
    for dr in range(nr - 1):
        base0 = (h * nr + dr) * nd
        base1 = base0 + nd
        t = jnp.zeros((GRID_W, LANES), F32)
        for d in range(nd):
            val = jnp.where(upper, rpb_ref[base1 + d], rpb_ref[base0 + d])
            t = jnp.where(dc == d, val, t)
        tiles.append(jnp.where(valid, t, NEG_BIG))
    for off in range(NA_MAX_ROWS):
        for j in range(NA_MAX_ROWS // 2):
            o_ref[off, 0, :, j * LANES:(j + 1) * LANES] = tiles[2 * j - off + NA_MAX_ROWS - 1]


def _na_bias_table(rpb):
    return pl.pallas_call(
        _na_bias_kernel, name="na_bias_table",
        grid=(NA_HEADS,),
        in_specs=[pl.BlockSpec(memory_space=pltpu.SMEM)],
        out_specs=pl.BlockSpec((NA_MAX_ROWS, 1, GRID_W, NA_MAX_ROWS * GRID_W), lambda h: (0, h, 0, 0)),
        out_shape=jax.ShapeDtypeStruct((NA_MAX_ROWS, NA_HEADS, GRID_W, NA_MAX_ROWS * GRID_W), F32),
        compiler_params=_cparams(("arbitrary",), 16),
    )(rpb.reshape(-1))


NA_ROWS_PER_STEP = 8


def _na_kernel(q_ref, k_ref, v_ref, kx_ref, vx_ref, bias_ref, o_ref, *, rows):
    wr = NA_MAX_ROWS
    nloc = wr * GRID_W
    pairs = range(NA_HEADS // 2)
    lanes = [slice(p * LANES, (p + 1) * LANES) for p in pairs]
    for i in range(NA_ROWS_PER_STEP):
        r = pl.program_id(1) * NA_ROWS_PER_STEP + i
        first = jnp.clip(r - wr // 2, 0, rows - wr)
        start = pl.multiple_of(first * GRID_W, GRID_W)
        off = r - first
        qr = slice(i * GRID_W, (i + 1) * GRID_W)
        scores = []
        for p in pairs:
            q_tile = q_ref[0, qr, lanes[p]]
            q2 = jnp.concatenate([_half_mask(q_tile, 0), _half_mask(q_tile, 1)], axis=0)
            k = jnp.concatenate([k_ref[0, pl.ds(start, nloc), lanes[p]], kx_ref[0, :, lanes[p]]], axis=0)
            scores.append(_qk(q2, k))
        probs = []
        for p in pairs:
            s = scores[p]
            bias = jnp.concatenate([bias_ref[off, 2 * p], bias_ref[off, 2 * p + 1]], axis=0)
            probs.append(_softmax_weights(jnp.concatenate([s[:, :nloc] + bias, s[:, nloc:]], axis=1)))
        for p in pairs:
            pw, den = probs[p]
            v = jnp.concatenate([v_ref[0, pl.ds(start, nloc), lanes[p]], vx_ref[0, :, lanes[p]]], axis=0)
            o = _dot(pw, v) / den
            o_ref[0, qr, lanes[p]] = _merge_halves(o[:GRID_W], o[GRID_W:]).astype(o_ref.dtype)


def _neighbourhood_attention(q, k, v, kx, vx, bias_tab):
    b, l, w = q.shape
    rows = l // GRID_W
    n_ctx = kx.shape[1]
    tq = NA_ROWS_PER_STEP * GRID_W
    full = pl.BlockSpec((1, l, w), lambda b, r: (b, 0, 0))
    ctx = pl.BlockSpec((1, n_ctx, w), lambda b, r: (b, 0, 0))
    return pl.pallas_call(
        functools.partial(_na_kernel, rows=rows), name="neighbourhood_attention",
        grid=(b, rows // NA_ROWS_PER_STEP),
        in_specs=[pl.BlockSpec((1, tq, w), lambda b, r: (b, r, 0)), full, full, ctx, ctx,
                  _resident(bias_tab.shape)],
        out_specs=pl.BlockSpec((1, tq, w), lambda b, r: (b, r, 0)),
        out_shape=jax.ShapeDtypeStruct(q.shape, BF16),
        compiler_params=_cparams(("parallel", "arbitrary"), 48),
    )(q, k, v, kx, vx, bias_tab)


def _ctx_attn_kernel(sink_ref, qs_ref, ks_ref, vs_ref, qn_ref, kn_ref, vn_ref, os_ref, on_ref):
    n = qs_ref.shape[1]
    g = SWA_HEADS // SWA_KV_HEADS
    rsel = lax.broadcasted_iota(jnp.int32, (g * n, 1), 0) >> (n.bit_length() - 1)
    outs = [None] * SWA_HEADS
    for kv in range(SWA_KV_HEADS):
        cs = slice(kv * LANES, (kv + 1) * LANES)
        q_parts = []
        sink = jnp.zeros((g * n, 1), F32)
        for gi in range(g):
            h = kv * g + gi
            q_parts.append(_half_mask(qs_ref[0, :, (h // 2) * LANES:(h // 2 + 1) * LANES], h % 2))
            sink = jnp.where(rsel == gi, sink_ref[h], sink)
        o = _softmax_pv(_qk(jnp.concatenate(q_parts, axis=0), ks_ref[0, :, cs]), vs_ref[0, :, cs], sink)
        for gi in range(g):
            outs[kv * g + gi] = o[gi * n:(gi + 1) * n]
    for p in range(SWA_HEADS // 2):
        os_ref[0, :, p * LANES:(p + 1) * LANES] = _merge_halves(outs[2 * p], outs[2 * p + 1]).astype(os_ref.dtype)
    for p in range(NA_HEADS // 2):
        cs = slice(p * LANES, (p + 1) * LANES)
        q_tile = qn_ref[0, :, cs]
        q2 = jnp.concatenate([_half_mask(q_tile, 0), _half_mask(q_tile, 1)], axis=0)
        o = _softmax_pv(_qk(q2, kn_ref[0, :, cs]), vn_ref[0, :, cs])
        on_ref[0, :, cs] = _merge_halves(o[:n], o[n:]).astype(on_ref.dtype)


def _context_attention(qs, ks, vs, qn, kn, vn, sink):
    b, n, w = qs.shape
    spec = lambda a: pl.BlockSpec((1,) + a.shape[1:], lambda b: (b, 0, 0))
    return pl.pallas_call(
        _ctx_attn_kernel, name="context_attention",
        grid=(b,),
        in_specs=[pl.BlockSpec(memory_space=pltpu.SMEM)] + [spec(a) for a in (qs, ks, vs, qn, kn, vn)],
        out_specs=[spec(qs), spec(qn)],
        out_shape=[jax.ShapeDtypeStruct(qs.shape, BF16), jax.ShapeDtypeStruct(qn.shape, BF16)],
        compiler_params=_cparams(("parallel",), 32),
    )(sink, qs, ks, vs, qn, kn, vn)


def _merge_ffn_kernel(x_ref, mg_ref, yh_ref, ys_ref, yn_ref, zg_ref, wb_ref, wo_ref,
                      sh_ref, sc_ref, gt_ref, g_ref, wg_ref, wu_ref, wd_ref, o_ref, *, fc):
    d = D_MODEL
    yh = jnp.concatenate([yh_ref[0, c] for c in range(yh_ref.shape[1])], axis=1)
    ys = (yh.astype(BF16), ys_ref[0], yn_ref[0])
    m = None
    for n in range(3):
        proj = _dot(ys[n], wb_ref[n])
        term = zg_ref[0, :, n * d:(n + 1) * d].astype(F32) * proj
        m = term if m is None else m + term
    x = x_ref[0] + mg_ref[0] * _dot(m.astype(BF16), wo_ref[...])
    o_ref[0] = _swiglu_residual(x, sh_ref, sc_ref, gt_ref, g_ref, wg_ref, wu_ref, wd_ref, fc)


def _merge_and_ffn(x, mix_gate, y_hy, y_swa, y_na, gates, w_branch, w_out, layer,
                   shift, scale, gate, gain, wg, wu, wd):
    b, l, d = x.shape
    tm = min(512, l)
    tok = lambda n: pl.BlockSpec((1, tm, n), lambda b, t: (b, t, 0))
    idx = (layer, 1)
    return pl.pallas_call(
        functools.partial(_merge_ffn_kernel, fc=FFN_CHUNK), name="merge_ffn",
        grid=(b, l // tm),
        in_specs=[tok(d), _mod_spec(mix_gate),
                  pl.BlockSpec((1, HY_WIDTH // HY_CHUNK, tm, HY_CHUNK), lambda b, t: (b, 0, t, 0)),
                  tok(HY_WIDTH), tok(HY_WIDTH), tok(3 * d),
                  _layer_slab(w_branch, (layer,)), _layer_slab(w_out, (layer,)),
                  _mod_spec(shift), _mod_spec(scale), _mod_spec(gate), _resident((1, d)),
                  _layer_slab(wg, idx), _layer_slab(wu, idx), _layer_slab(wd, idx)],
        out_specs=tok(d),
        out_shape=jax.ShapeDtypeStruct(x.shape, F32),
        compiler_params=_cparams(("parallel", "parallel"), 58),
    )(x, mix_gate[0], y_hy, y_swa, y_na, gates, w_branch, w_out, shift[0], scale[0], gate[0], gain.reshape(1, d),
      wg, wu, wd)


def _dft_tables(l):
    n = 2 * l
    fb = 256 if l >= 1024 else LANES * ((l + 1 + LANES - 1) // LANES)
    fp = fb * ((l + 1 + fb - 1) // fb)
    k = jnp.arange(fp, dtype=jnp.int32)[:, None]
    t = jnp.arange(l, dtype=jnp.int32)[None, :]
    ang = ((k * t) % n).astype(F32) * (2.0 * math.pi / n)
    live = k <= l
    c = jnp.where(live, jnp.cos(ang), 0.0)
    s = jnp.where(live, jnp.sin(ang), 0.0)
    c_hi = c.astype(BF16)
    s_hi = s.astype(BF16)
    wk = jnp.where((k == 0) | (k == l), 1.0, 2.0) * jnp.where(live, 1.0 / n, 0.0)
    return dict(fb=fb, c_hi=c_hi, s_hi=s_hi, ct_hi=c_hi.T, st_hi=s_hi.T, wk=wk.astype(F32))


def _filter_features(l):
    t = jnp.linspace(0.0, 1.0, l, dtype=F32)[:, None]
    w = (2.0 * math.pi / l) * jnp.arange(l, dtype=F32)[:, None]
    f = jnp.linspace(1e-4, HY_BANDS - 1, HY_BANDS, dtype=F32)[None, :]
    z = jnp.concatenate([t, jnp.cos(f * w), -jnp.sin(f * w)], axis=-1)
    return jnp.pad(z, ((0, 0), (0, LANES - HY_EMB)))


def _decay_rates():
    max_decay = math.log(HY_TARGET) / HY_FAST_DECAY
    min_decay = math.log(HY_TARGET) / HY_SLOW_DECAY
    return jnp.abs(jnp.linspace(min_decay, max_decay, HY_WIDTH, dtype=F32))[None, :]


def _rope_tables(l):
    pos = jnp.arange(l)
    row = (pos // GRID_W).astype(F32)
    col = (pos % GRID_W).astype(F32)
    half = HEAD_DIM // 2
    inv = 1.0 / (ROPE_BASE ** (jnp.arange(0, half, 2, dtype=F32) / half))
    ar = row[:, None] * inv[None, :]
    ac = col[:, None] * inv[None, :]
    cos = jnp.concatenate([jnp.cos(ar), jnp.cos(ar), jnp.cos(ac), jnp.cos(ac)], axis=-1)
    sin = jnp.concatenate([-jnp.sin(ar), jnp.sin(ar), -jnp.sin(ac), jnp.sin(ac)], axis=-1)
    return jnp.tile(cos, (1, 2)), jnp.tile(sin, (1, 2))


def _pad_to(a, shape):
    return jnp.pad(a, [(0, s - d) for d, s in zip(a.shape, shape)])


def _use_fft(bsz, length):
    return bsz % 2 == 0 and length % (8 * FFT_INNER) == 0


def _conv_tables(bsz, length):
    return _fft_tables(length) if _use_fft(bsz, length) else _dft_tables(length)


def _hyena_branch(z_hy, short_w, short_b, feat, fparams, deltas, bias, tabs):
    ad = _hyena_filter_sums(feat, fparams, deltas)
    if "n2" in tabs:
        kr, ki = _fft_filter_spectrum(ad, tabs)
        y1 = _fft_gated_long_conv(None, 0, z_hy, 1, short_w, short_b, kr, ki, 0, bias[0], tabs, F32)
        return _fft_gated_long_conv(y1, 0, z_hy, 2, short_w, short_b, kr, ki, 1, bias[1], tabs, BF16)
    v, x1, x2 = _short_conv3(z_hy, short_w, short_b)
    kr, ki = _filter_spectrum(ad, tabs)
    conv = functools.partial(_gated_long_conv, tabs=tabs)
    y1 = conv(v, x1, kr, ki, 0, bias[0])
    y = conv(y1, x2, kr, ki, 1, bias[1])
    bsz, length, _ = y.shape
    return y.reshape(bsz, length, HY_WIDTH // HY_CHUNK, HY_CHUNK).transpose(0, 2, 1, 3)


def kernel(x, c, ctx, c_ctx, w_ada, b_ada, norm_g, ffn_w_gate, ffn_w_up, ffn_w_down,
           w_in, hy_short_w, hy_short_b, hy_pe_w0, hy_pe_b0, hy_pe_w1, hy_pe_b1,
           hy_pe_w2, hy_pe_b2, hy_pe_wout, hy_sin_freq, hy_bias,
           swa_q_gain, swa_k_gain, swa_sink, na_q_gain, na_k_gain, na_rpb,
           w_branch, w_out):
    bsz, seq, d = x.shape
    n_ctx = ctx.shape[1]
    depth = w_ada.shape[0]

    c16 = _pad_to(jnp.concatenate([c, c_ctx[None, :]], axis=0), (MOD_ROWS, d))
    mods = _adaln_mods(c16, w_ada, b_ada).reshape(depth, MOD_ROWS, N_MOD, 1, d)

    tabs_x = _conv_tables(bsz, seq)
    tabs_c = _conv_tables(bsz, n_ctx)
    feat_x = _filter_features(seq)
    feat_c = _filter_features(n_ctx)
    deltas = _decay_rates()
    cos_x, sin_x = _rope_tables(seq)
    cos_c = jnp.ones((n_ctx, LANES), F32)
    sin_c = jnp.zeros((n_ctx, LANES), F32)
    eye = jnp.arange(MXU_DIM) // HEAD_DIM
    bd = (eye[:, None] == eye[None, :]).astype(BF16)

    wg = ffn_w_gate.astype(BF16)
    wu = ffn_w_up.astype(BF16)
    wd = ffn_w_down.astype(BF16)
    wb = w_branch.astype(BF16)
    wo = w_out.astype(BF16)
    win = w_in.astype(BF16)
    kv = win[:, :, _OFF_SWA + _SWA_Q:_OFF_NA].reshape(depth, d, 2, SWA_KV_HEADS, 1, HEAD_DIM)
    win_kv = jnp.broadcast_to(kv, (depth, d, 2, SWA_KV_HEADS, 2, HEAD_DIM)).reshape(depth, d, 4 * _SWA_KV)

    xc = ctx
    for i in range(depth):
        last = i == depth - 1
        mx = lambda j: (mods, i, j, None)
        mc = lambda j: (mods, i, j, bsz)
        tile2 = lambda g: jnp.tile(g, 2 * MXU_DIM // LANES)
        head_gains = jnp.stack([tile2(swa_q_gain[i]), tile2(swa_k_gain[i]), tile2(na_q_gain[i]), tile2(na_k_gain[i])])
        fparams = (_pad_to(hy_pe_w0[i], (LANES, LANES)), _pad_to(hy_pe_b0[i][None], (1, LANES)),
                   _pad_to(hy_pe_w1[i], (LANES, LANES)), _pad_to(hy_pe_b1[i][None], (1, LANES)),
                   _pad_to(hy_pe_w2[i], (LANES, LANES)), _pad_to(hy_pe_b2[i][None], (1, LANES)),
                   _pad_to(hy_pe_wout[i], (LANES, 4 * HY_WIDTH)), _pad_to(hy_sin_freq[i][None], (1, LANES)))

        x = _ffn_half_step(x, mx(0), mx(1), mx(2), norm_g[i, 0], wg, wu, wd, (i, 0))
        xc = _ffn_half_step(xc, mc(0), mc(1), mc(2), norm_g[i, 0], wg, wu, wd, (i, 0))

        z_hy, q_s, k_s, v_s, q_n, k_n, v_n, gates = _in_projection(
            x, mx(3), mx(4), norm_g[i, 1], cos_x, sin_x, head_gains, bd, win, win_kv, i)
        zc_hy, qc_s, kc_s, vc_s, qc_n, kc_n, vc_n, gates_c = _in_projection(
            xc, mc(3), mc(4), norm_g[i, 1], cos_c, sin_c, head_gains, bd, win, win_kv, i)

        y_hy = _hyena_branch(z_hy, hy_short_w[i], hy_short_b[i], feat_x, fparams, deltas, hy_bias[i], tabs_x)
        y_swa = _window_gqa(q_s, k_s, v_s, kc_s, vc_s, swa_sink[i])
        y_na = _neighbourhood_attention(q_n, k_n, v_n, kc_n, vc_n, _na_bias_table(na_rpb[i]))
        x = _merge_and_ffn(x, mx(5), y_hy, y_swa, y_na, gates, wb, wo, i,
                           mx(6), mx(7), mx(8), norm_g[i, 2], wg, wu, wd)

        if not last:
            yc_hy = _hyena_branch(zc_hy, hy_short_w[i], hy_short_b[i], feat_c, fparams, deltas, hy_bias[i], tabs_c)
            yc_swa, yc_na = _context_attention(qc_s, kc_s, vc_s, qc_n, kc_n, vc_n, swa_sink[i])
            xc = _merge_and_ffn(xc, mc(5), yc_hy, yc_swa, yc_na, gates_c, wb, wo, i,
                                mc(6), mc(7), mc(8), norm_g[i, 2], wg, wu, wd)
    return x
```

```python
import functools
import math

import jax
import jax.numpy as jnp
from jax import lax
from jax.experimental import pallas as pl
from jax.experimental.pallas import tpu as pltpu

F32 = jnp.float32
BF16 = jnp.bfloat16

D_MODEL = 1024
DEPTH = 4
GRID_W = 64
HEAD_DIM = 64
N_MOD = 9
RMS_EPS = 1e-6

HY_WIDTH = D_MODEL // 2
HY_CHUNK = 256
HY_EMB = 33
HY_BANDS = (HY_EMB - 1) // 2
HY_FFN = 64
HY_FAST_DECAY = 0.3
HY_SLOW_DECAY = 1.5
HY_TARGET = 1e-2

SWA_HEADS = 8
SWA_KV_HEADS = 2
SWA_WINDOW = 128
SWA_BLOCK = 128
ROPE_BASE = 10000.0

NA_HEADS = 8
NA_MAX_ROWS = 8
NA_COLS = 16
NA_COL_BLOCK = 16

FFN_HIDDEN = 256 * ((8 * D_MODEL // 3 + 255) // 256)
FFN_CHUNK = 1536

LANES = 128
MXU_DIM = 256
MOD_ROWS = 16
NEG_BIG = -1e30


def _cparams(sem, vmem_mb):
    return pltpu.CompilerParams(dimension_semantics=sem, vmem_limit_bytes=vmem_mb * 1024 * 1024)


def _resident(shape):
    nd = len(shape)
    return pl.BlockSpec(shape, lambda *_: (0,) * nd, pipeline_mode=pl.Buffered(1))


def _layer_slab(arr, idx):
    rest = arr.shape[len(idx):]
    return pl.BlockSpec((None,) * len(idx) + rest, lambda *_: tuple(idx) + (0,) * len(rest),
                        pipeline_mode=pl.Buffered(1))


def _split(x):
    hi = x.astype(BF16)
    lo = (x - hi.astype(F32)).astype(BF16)
    return hi, lo


def _dot(a, b):
    return jnp.dot(a, b, preferred_element_type=F32)


def _dot3(a_hi, a_lo, b_hi, b_lo):
    return _dot(a_hi, b_hi) + _dot(a_lo, b_hi) + _dot(a_hi, b_lo)


def _dot3f(a, b):
    a_hi, a_lo = _split(a)
    b_hi, b_lo = _split(b)
    return _dot3(a_hi, a_lo, b_hi, b_lo)


def _rms_mod(x, gain, shift, scale):
    ms = jnp.mean(x * x, axis=-1, keepdims=True)
    y = x * lax.rsqrt(ms + RMS_EPS) * gain
    return y * (1.0 + scale) + shift


def _head_sumsq(x, bd):
    return [_dot(jnp.square(x[:, j * MXU_DIM:(j + 1) * MXU_DIM]).astype(BF16), bd) for j in range(x.shape[1] // MXU_DIM)]


def _head_normalize(x, sumsq, gain):
    outs = [x[:, j * MXU_DIM:(j + 1) * MXU_DIM] * lax.rsqrt(s * (1.0 / HEAD_DIM) + RMS_EPS) * gain
            for j, s in enumerate(sumsq)]
    return outs[0] if len(outs) == 1 else jnp.concatenate(outs, axis=1)


def _rope(x, cos, sin):
    lane = lax.broadcasted_iota(jnp.int32, (x.shape[0], LANES), 1)
    first = (lane & 31) < 16
    outs = []
    for j in range(x.shape[1] // LANES):
        xc = x[:, j * LANES:(j + 1) * LANES]
        partner = jnp.where(first, pltpu.roll(xc, LANES - 16, axis=1), pltpu.roll(xc, 16, axis=1))
        outs.append(xc * cos + partner * sin)
    return outs[0] if len(outs) == 1 else jnp.concatenate(outs, axis=1)


def _mods_kernel(c_ref, w_ref, b_ref, o_ref):
    c = c_ref[...]
    a = c * jax.nn.sigmoid(c)
    o_ref[0] = _dot3f(a, w_ref[0]) + b_ref[0]


def _adaln_mods(c16, w_ada, b_ada):
    depth, d, n = w_ada.shape
    nb = 1152
    return pl.pallas_call(
        _mods_kernel, name="adaln_mods",
        grid=(depth, n // nb),
        in_specs=[
            pl.BlockSpec((MOD_ROWS, d), lambda i, j: (0, 0)),
            pl.BlockSpec((1, d, nb), lambda i, j: (i, 0, j)),
            pl.BlockSpec((1, 1, nb), lambda i, j: (i, 0, j)),
        ],
        out_specs=pl.BlockSpec((1, MOD_ROWS, nb), lambda i, j: (i, 0, j)),
        out_shape=jax.ShapeDtypeStruct((depth, MOD_ROWS, n), F32),
        compiler_params=_cparams(("arbitrary", "arbitrary"), 48),
    )(c16, w_ada, b_ada.reshape(depth, 1, n))


def _swiglu_residual(x, sh_ref, sc_ref, gt_ref, g_ref, wg_ref, wu_ref, wd_ref, fc):
    hb = _rms_mod(x, g_ref[...], sh_ref[0], sc_ref[0]).astype(BF16)
    acc = None
    for f0 in range(0, FFN_HIDDEN, fc):
        f1 = min(f0 + fc, FFN_HIDDEN)
        g = _dot(hb, wg_ref[:, f0:f1])
        u = _dot(hb, wu_ref[:, f0:f1])
        a = (g * jax.nn.sigmoid(g) * u).astype(BF16)
        d = _dot(a, wd_ref[f0:f1, :])
        acc = d if acc is None else acc + d
    return x + 0.5 * gt_ref[0] * acc


def _ffn_kernel(x_ref, sh_ref, sc_ref, gt_ref, g_ref, wg_ref, wu_ref, wd_ref, o_ref, *, fc):
    o_ref[0] = _swiglu_residual(x_ref[0], sh_ref, sc_ref, gt_ref, g_ref, wg_ref, wu_ref, wd_ref, fc)


def _mod_spec(sel):
    _, layer, j, row = sel
    if row is None:
        return pl.BlockSpec((None, 1, None, 1, D_MODEL), lambda b, t: (layer, b, j, 0, 0))
    return pl.BlockSpec((None, 1, None, 1, D_MODEL), lambda b, t: (layer, row, j, 0, 0))


def _ffn_half_step(x, shift, scale, gate, gain, wg, wu, wd, idx):
    b, l, d = x.shape
    tm = min(512, l)
    return pl.pallas_call(
        functools.partial(_ffn_kernel, fc=FFN_CHUNK), name="ffn_half_step",
        grid=(b, l // tm),
        in_specs=[
            pl.BlockSpec((1, tm, d), lambda b, t: (b, t, 0)),
            _mod_spec(shift), _mod_spec(scale), _mod_spec(gate),
            _resident((1, d)),
            _layer_slab(wg, idx), _layer_slab(wu, idx), _layer_slab(wd, idx),
        ],
        out_specs=pl.BlockSpec((1, tm, d), lambda b, t: (b, t, 0)),
        out_shape=jax.ShapeDtypeStruct(x.shape, F32),
        compiler_params=_cparams(("parallel", "parallel"), 56),
    )(x, shift[0], scale[0], gate[0], gain.reshape(1, d), wg, wu, wd)


_HY_COLS = 3 * HY_WIDTH
_SWA_Q = SWA_HEADS * HEAD_DIM
_SWA_KV = SWA_KV_HEADS * HEAD_DIM
_NA_W = NA_HEADS * HEAD_DIM
_OFF_SWA = _HY_COLS
_OFF_NA = _OFF_SWA + _SWA_Q + 2 * _SWA_KV
_OFF_GATE = _OFF_NA + 3 * _NA_W


def _inproj_kernel(x_ref, sh_ref, sc_ref, g_ref, cos_ref, sin_ref, hg_ref, bd_ref, w_ref, wkv_ref,
                   zhy_ref, qs_ref, ks_ref, vs_ref, qn_ref, kn_ref, vn_ref, gt_ref):
    hb = _rms_mod(x_ref[0], g_ref[...], sh_ref[0], sc_ref[0]).astype(BF16)
    cos = cos_ref[...]
    sin = sin_ref[...]
    bd = bd_ref[...]
    scale = HEAD_DIM ** -0.5
    proj = lambda lo, n: _dot(hb, w_ref[:, lo:lo + n])
    kvw = 2 * _SWA_KV
    raw = [proj(_OFF_SWA, _SWA_Q), _dot(hb, wkv_ref[:, :kvw]), proj(_OFF_NA, _NA_W), proj(_OFF_NA + _NA_W, _NA_W)]
    gates = proj(_OFF_GATE, 3 * D_MODEL)
    sums = [_head_sumsq(r, bd) for r in raw]
    vs_ref[0] = _dot(hb, wkv_ref[:, kvw:]).astype(BF16)
    vn_ref[0] = proj(_OFF_NA + 2 * _NA_W, _NA_W).astype(BF16)
    zhy = proj(0, _HY_COLS).astype(BF16)
    for c in range(_HY_COLS // HY_CHUNK):
        zhy_ref[0, c] = zhy[:, c * HY_CHUNK:(c + 1) * HY_CHUNK]
    gt_ref[0] = jax.nn.sigmoid(gates).astype(BF16)
    qs, ks, qn, kn = [_head_normalize(r, s, hg_ref[i:i + 1, :]) for i, (r, s) in enumerate(zip(raw, sums))]
    qs_ref[0] = (_rope(qs, cos, sin) * scale).astype(BF16)
    ks_ref[0] = _rope(ks, cos, sin).astype(BF16)
    qn_ref[0] = (qn * scale).astype(BF16)
    kn_ref[0] = kn.astype(BF16)


def _in_projection(x, shift, scale, gain, cos, sin, head_gains, bd, w, w_kv_dup, layer):
    b, l, d = x.shape
    tm = min(512, l)
    kvw = 2 * _SWA_KV
    widths = [_HY_COLS, _SWA_Q, kvw, kvw, _NA_W, _NA_W, _NA_W, 3 * d]
    dtypes = [BF16] * 8
    return pl.pallas_call(
        _inproj_kernel, name="in_projection",
        grid=(b, l // tm),
        in_specs=[
            pl.BlockSpec((1, tm, d), lambda b, t: (b, t, 0)),
            _mod_spec(shift), _mod_spec(scale),
            _resident((1, d)),
            pl.BlockSpec((tm, LANES), lambda b, t: (t, 0)),
            pl.BlockSpec((tm, LANES), lambda b, t: (t, 0)),
            _resident(head_gains.shape), _resident(bd.shape), _layer_slab(w, (layer,)), _layer_slab(w_kv_dup, (layer,)),
        ],
        out_specs=[pl.BlockSpec((1, _HY_COLS // HY_CHUNK, tm, HY_CHUNK), lambda b, t: (b, 0, t, 0))]
                  + [pl.BlockSpec((1, tm, n), lambda b, t: (b, t, 0)) for n in widths[1:]],
        out_shape=[jax.ShapeDtypeStruct((b, _HY_COLS // HY_CHUNK, l, HY_CHUNK), BF16)]
                  + [jax.ShapeDtypeStruct((b, l, n), dt) for n, dt in zip(widths[1:], dtypes[1:])],
        compiler_params=_cparams(("parallel", "parallel"), 56),
    )(x, shift[0], scale[0], gain.reshape(1, d), cos, sin, head_gains, bd, w, w_kv_dup)


def _shortconv_kernel(z0_ref, z1_ref, z2_ref, w_ref, b_ref, o0_ref, o1_ref, o2_ref):
    nb, _, l, _ = z0_ref.shape
    row = lax.broadcasted_iota(jnp.int32, z0_ref.shape[2:], 0)
    for g, (z_ref, o_ref) in enumerate(((z0_ref, o0_ref), (z1_ref, o1_ref), (z2_ref, o2_ref))):
        for i in range(nb):
            z = z_ref[i, 0].astype(F32)
            prev = jnp.where(row == 0, 0.0, pltpu.roll(z, 1, axis=0))
            nxt = jnp.where(row == l - 1, 0.0, pltpu.roll(z, l - 1, axis=0))
            o_ref[i] = w_ref[g, 0:1, :] * prev + w_ref[g, 1:2, :] * z + w_ref[g, 2:3, :] * nxt + b_ref[g]


def _short_conv3(z, w, bias):
    b, _, l, cb = z.shape
    ncb = HY_WIDTH // cb
    wg = w.reshape(3, 3, HY_WIDTH).transpose(1, 0, 2)
    zin = lambda g: pl.BlockSpec((b, 1, l, cb), lambda j: (0, g * ncb + j, 0, 0))
    out = pl.BlockSpec((b, l, cb), lambda j: (0, 0, j))
    return pl.pallas_call(
        _shortconv_kernel, name="short_conv3",
        grid=(ncb,),
        in_specs=[zin(0), zin(1), zin(2),
                  pl.BlockSpec((3, 3, cb), lambda j: (0, 0, j)),
                  pl.BlockSpec((3, 1, cb), lambda j: (0, 0, j))],
        out_specs=[out, out, out],
        out_shape=[jax.ShapeDtypeStruct((b, l, HY_WIDTH), F32)] * 3,
        compiler_params=_cparams(("parallel",), 48),
    )(z, z, z, wg, bias.reshape(3, 1, HY_WIDTH))


def _filter_kernel(z_ref, w0, b0, w1, b1, w2, b2, wo, fr_ref, dl_ref, o_ref):
    z = z_ref[...]
    fr = fr_ref[...]
    a = jnp.sin(fr * (_dot3f(z, w0[...]) + b0[...]))
    a = jnp.sin(fr * (_dot3f(a, w1[...]) + b1[...]))
    a = jnp.sin(fr * (_dot3f(a, w2[...]) + b2[...]))
    hh = _dot3f(a, wo[...])
    t = z[:, 0:1]
    win = jnp.exp(-t * dl_ref[...])
    row = lax.broadcasted_iota(jnp.int32, win.shape, 0) + pl.program_id(0) * z.shape[0]
    w = HY_WIDTH
    for o in range(2):
        hp = hh[:, (2 * o) * w:(2 * o + 1) * w] * win
        hn = jnp.where(row == 0, 0.0, hh[:, (2 * o + 1) * w:(2 * o + 2) * w] * win)
        o_ref[0, :, o * w:(o + 1) * w] = hp + hn
        o_ref[1, :, o * w:(o + 1) * w] = hp - hn


def _hyena_filter_sums(zfeat, fp, deltas):
    l = zfeat.shape[0]
    tl = min(256, l)
    w0, b0, w1, b1, w2, b2, wo, fr = fp
    n = 2 * HY_WIDTH
    consts = [w0, b0, w1, b1, w2, b2, wo, fr, deltas]
    return pl.pallas_call(
        _filter_kernel, name="hyena_filter",
        grid=(l // tl,),
        in_specs=[pl.BlockSpec((tl, LANES), lambda t: (t, 0))] + [_resident(c.shape) for c in consts],
        out_specs=pl.BlockSpec((2, tl, n), lambda t: (0, t, 0)),
        out_shape=jax.ShapeDtypeStruct((2, l, n), F32),
        compiler_params=_cparams(("parallel",), 32),
    )(zfeat, *consts)


def _spectrum_kernel(a_ref, d_ref, c_ref, s_ref, kr_ref, ki_ref):
    kr_ref[...] = _dot(c_ref[...], a_ref[0].astype(BF16))
    ki_ref[...] = -_dot(s_ref[...], d_ref[0].astype(BF16))


def _filter_spectrum(ad, tabs):
    _, l, n = ad.shape
    fp = tabs["c_hi"].shape[0]
    fb = tabs["fb"]
    cb = 256
    tab = pl.BlockSpec((fb, l), lambda j, k: (k, 0))
    out = pl.BlockSpec((fb, cb), lambda j, k: (k, j))
    return pl.pallas_call(
        _spectrum_kernel, name="dense_spectrum",
        grid=(n // cb, fp // fb),
        in_specs=[pl.BlockSpec((1, l, cb), lambda j, k: (0, 0, j)),
                  pl.BlockSpec((1, l, cb), lambda j, k: (1, 0, j)), tab, tab],
        out_specs=[out, out],
        out_shape=[jax.ShapeDtypeStruct((fp, n), F32)] * 2,
        compiler_params=_cparams(("parallel", "parallel"), 48),
    )(ad, ad, tabs["c_hi"], tabs["s_hi"])


def _longconv_kernel(u_ref, g_ref, kr_ref, ki_ref, bias_ref, wk_ref, c_ref, s_ref, ct_ref, st_ref,
                     o_ref, ub_ref, acc_ref):
    kb = pl.program_id(1)
    nb, _, cb = u_ref.shape

    @pl.when(kb == 0)
    def _():
        ub_ref[...] = jnp.concatenate([u_ref[i].astype(BF16) for i in range(nb)], axis=1)
        acc_ref[...] = jnp.zeros_like(acc_ref)

    ub = ub_ref[...]
    xr = _dot(c_ref[...], ub)
    xi = -_dot(s_ref[...], ub)
    kr = jnp.tile(kr_ref[...], (1, nb))
    ki = jnp.tile(ki_ref[...], (1, nb))
    wk = wk_ref[...]
    yr = ((xr * kr - xi * ki) * wk).astype(BF16)
    yi = ((xr * ki + xi * kr) * wk).astype(BF16)
    acc_ref[...] += _dot(ct_ref[...], yr) - _dot(st_ref[...], yi)

    @pl.when(kb == pl.num_programs(1) - 1)
    def _():
        for i in range(nb):
            o_ref[i] = g_ref[i] * (acc_ref[:, i * cb:(i + 1) * cb] + u_ref[i] * bias_ref[...])


def _gated_long_conv(u_arr, g_arr, kr, ki, k_col, bias, tabs):
    b, l, _ = u_arr.shape
    cb = 256
    ncb = HY_WIDTH // cb
    fp = tabs["c_hi"].shape[0]
    fb = tabs["fb"]
    tab = pl.BlockSpec((fb, l), lambda j, k: (k, 0))
    tabt = pl.BlockSpec((l, fb), lambda j, k: (0, k))
    spec = pl.BlockSpec((fb, cb), lambda j, k: (k, k_col * ncb + j))
    tok = pl.BlockSpec((b, l, cb), lambda j, k: (0, 0, j))
    return pl.pallas_call(
        _longconv_kernel, name="dense_long_conv",
        grid=(ncb, fp // fb),
        in_specs=[tok, tok, spec, spec,
                  pl.BlockSpec((1, cb), lambda j, k: (0, j)),
                  pl.BlockSpec((fb, 1), lambda j, k: (k, 0)),
                  tab, tab, tabt, tabt],
        out_specs=tok,
        out_shape=jax.ShapeDtypeStruct((b, l, HY_WIDTH), F32),
        scratch_shapes=[pltpu.VMEM((l, b * cb), BF16), pltpu.VMEM((l, b * cb), F32)],
        compiler_params=_cparams(("parallel", "arbitrary"), 56),
    )(u_arr, g_arr, kr, ki, bias.reshape(1, HY_WIDTH), tabs["wk"],
      tabs["c_hi"], tabs["s_hi"], tabs["ct_hi"], tabs["st_hi"])


FFT_INNER = 64
FFT_COLS = HY_CHUNK
FFT_K2_CHUNK = 16
FFT_UNROLL = 64


def _fft_tables(l):
    n = 2 * l
    n1 = FFT_INNER
    n2 = n // n1
    h = n2 // 2
    ang = lambda idx, mod: (idx % mod).astype(F32) * (2.0 * math.pi / mod)
    k2 = jnp.arange(n2, dtype=jnp.int32)
    m2 = jnp.arange(h, dtype=jnp.int32)
    a1 = ang(k2[:, None] * m2[None, :], n2)
    c1, s1 = jnp.cos(a1), jnp.sin(a1)
    f1 = jnp.block([[c1, s1], [-s1, c1]])
    k1 = jnp.arange(n1, dtype=jnp.int32)
    a2 = ang(k1[None, None, :] * (k1[None, :, None] * n2 + k2[:, None, None]), n)
    c2, s2 = jnp.cos(a2), jnp.sin(a2)
    m = jnp.concatenate([jnp.concatenate([c2, s2], axis=2), jnp.concatenate([-s2, c2], axis=2)], axis=1)
    g = jnp.block([[c1.T, -s1.T], [s1.T, c1.T]]) * (1.0 / n)
    out = dict(n2=n2)
    for name, t in (("f1", f1), ("f1r", f1[:, :h]), ("m", m), ("mt", jnp.swapaxes(m, 1, 2)), ("g", g)):
        out[name + "_hi"] = t.astype(BF16)
    return out


def _mm(w, x):
    return _dot(w, x.astype(BF16))


def _row_block_matmul(w, src_ref, dst_ref, count):
    m, k = w.shape

    def body(i, carry):
        x = src_ref[pl.ds(pl.multiple_of(i * k, k), k), :]
        dst_ref[pl.ds(pl.multiple_of(i * m, m), m), :] = _mm(w, x).astype(dst_ref.dtype)
        return carry
    lax.fori_loop(0, count, body, 0, unroll=FFT_UNROLL)


def _short_conv_rows(z, w_ref, b_ref):
    l = z.shape[0]
    row = lax.broadcasted_iota(jnp.int32, z.shape, 0)
    prev = jnp.where(row == 0, 0.0, pltpu.roll(z, 1, axis=0))
    nxt = jnp.where(row == l - 1, 0.0, pltpu.roll(z, l - 1, axis=0))
    return w_ref[0:1, :] * prev + w_ref[1:2, :] * z + w_ref[2:3, :] * nxt + b_ref[...]


def _fftconv_kernel(u_ref, g_ref, uw_ref, ub_ref, gw_ref, gb_ref, kr_ref, ki_ref, bias_ref,
                    f1_ref, m_ref, mt_ref, gm_ref, o_ref, z_scr, y_scr, p_scr, q_scr, u_scr, *, u_conv):
    kc = pl.program_id(2)
    n1 = FFT_INNER
    cc = z_scr.shape[1]
    n2 = z_scr.shape[0] // n1
    h = n2 // 2

    @pl.when(kc == 0)
    def _stage1():
        z = []
        for s in range(2):
            u = u_ref[s, 0]
            if u_conv:
                u = _short_conv_rows(u.astype(F32), uw_ref, ub_ref)
                u_scr[s] = u
            z.append(jnp.swapaxes(u.astype(BF16).reshape(h, n1, cc), 0, 1))
        z_scr[...] = jnp.concatenate(z, axis=1).reshape(n1 * n2, cc)
        _row_block_matmul(f1_ref[...], z_scr, p_scr, n1)
        q_scr[...] = jnp.swapaxes(p_scr[...].reshape(n1, 2 * n2, cc), 0, 1).reshape(2 * n2 * n1, cc)

    ks = kr_ref.shape[0]
    rows = [(pl.multiple_of((kc * ks + j) * n1, n1), pl.multiple_of((n2 + kc * ks + j) * n1, n1)) for j in range(ks)]
    xs = [_mm(m_ref[j], jnp.concatenate([q_scr[pl.ds(r0, n1), :], q_scr[pl.ds(i0, n1), :]], axis=0))
          for j, (r0, i0) in enumerate(rows)]
    ys = []
    for j, x in enumerate(xs):
        xr, xi = x[:n1], x[n1:]
        kr, ki = kr_ref[j], ki_ref[j]
        ys.append(jnp.concatenate([xr * kr - xi * ki, xr * ki + xi * kr], axis=0))
    for j, (r0, i0) in enumerate(rows):
        b = _mm(mt_ref[j], ys[j]).astype(p_scr.dtype)
        p_scr[pl.ds(r0, n1), :] = b[:n1]
        p_scr[pl.ds(i0, n1), :] = b[n1:]

    @pl.when(kc == pl.num_programs(2) - 1)
    def _stage1_inv():
        q_scr[...] = jnp.swapaxes(p_scr[...].reshape(2 * n2, n1, cc), 0, 1).reshape(2 * n2 * n1, cc)
        _row_block_matmul(gm_ref[...], q_scr, y_scr, n1)
        y3 = y_scr[...].reshape(n1, n2, cc)
        for s in range(2):
            y = jnp.swapaxes(y3[:, s * h:(s + 1) * h, :], 0, 1).reshape(h * n1, cc)
            gate = _short_conv_rows(g_ref[s, 0].astype(F32), gw_ref, gb_ref)
            u = u_scr[s] if u_conv else u_ref[s, 0]
            o_ref[s, 0] = (gate * (y + u * bias_ref[...])).astype(o_ref.dtype)


def _fft_gated_long_conv(u, u_col, z_hy, g_col, short_w, short_b, kr, ki, k_col, bias, ft, out_dtype):
    b, _, l, cc = z_hy.shape
    c = HY_WIDTH
    n1, n2, ks = FFT_INNER, ft["n2"], FFT_K2_CHUNK
    p, ncc = b // 2, c // cc
    u_conv = u is None
    if u_conv:
        u = z_hy
    tok = lambda col: pl.BlockSpec((2, 1, None, l, cc), lambda q, j, k: (0, q, col * ncc + j, 0, 0))
    taps = lambda col: pl.BlockSpec((3, cc), lambda q, j, k: (0, col * ncc + j))
    row = lambda col: pl.BlockSpec((1, cc), lambda q, j, k: (0, col * ncc + j))
    filt = pl.BlockSpec((ks, n1, cc), lambda q, j, k: (k, 0, k_col * ncc + j))
    tab = pl.BlockSpec((ks, 2 * n1, 2 * n1), lambda q, j, k: (k, 0, 0))
    res = lambda a: pl.BlockSpec(a.shape, lambda q, j, k: (0,) * a.ndim)
    sb = short_b.reshape(1, -1)
    y = pl.pallas_call(
        functools.partial(_fftconv_kernel, u_conv=u_conv), name="fft_long_conv",
        grid=(p, ncc, n2 // ks),
        in_specs=[tok(u_col), tok(g_col), taps(u_col), row(u_col), taps(g_col), row(g_col), filt, filt, row(0),
                  res(ft["f1_hi"]), tab, tab, res(ft["g_hi"])],
        out_specs=tok(0),
        out_shape=jax.ShapeDtypeStruct((2, p, ncc, l, cc), out_dtype),
        scratch_shapes=[pltpu.VMEM((n1 * n2, cc), BF16), pltpu.VMEM((n1 * n2, cc), F32),
                        pltpu.VMEM((2 * n2 * n1, cc), BF16), pltpu.VMEM((2 * n2 * n1, cc), BF16),
                        pltpu.VMEM((2, l, cc) if u_conv else (2, 8, LANES), F32)],
        compiler_params=_cparams(("parallel", "parallel", "arbitrary"), 56),
    )(u.reshape((2, p) + u.shape[1:]), z_hy.reshape((2, p) + z_hy.shape[1:]), short_w, sb, short_w, sb, kr, ki,
      bias.reshape(1, c), ft["f1_hi"], ft["m_hi"], ft["mt_hi"], ft["g_hi"])
    return y.reshape(b, ncc, l, cc)


def _fftspec_kernel(a_ref, d_ref, f1_ref, m_ref, kr_ref, ki_ref, z_scr, p_scr, q_scr):
    n1 = FFT_INNER
    cc = z_scr.shape[1]
    h = z_scr.shape[0] // n1
    n2 = 2 * h
    for src_ref, out_ref, lo in ((a_ref, kr_ref, 0), (d_ref, ki_ref, n1)):
        z_scr[...] = jnp.swapaxes(src_ref[0].astype(BF16).reshape(h, n1, cc), 0, 1).reshape(n1 * h, cc)
        _row_block_matmul(f1_ref[...], z_scr, p_scr, n1)
        q_scr[...] = jnp.swapaxes(p_scr[...].reshape(n1, 2 * n2, cc), 0, 1).reshape(2 * n2 * n1, cc)

        def body(k2, carry):
            r0 = pl.multiple_of(k2 * n1, n1)
            i0 = pl.multiple_of((n2 + k2) * n1, n1)
            slab = jnp.concatenate([q_scr[pl.ds(r0, n1), :], q_scr[pl.ds(i0, n1), :]], axis=0)
            out_ref[k2] = _mm(m_ref[k2][lo:lo + n1], slab)
            return carry
        lax.fori_loop(0, n2, body, 0, unroll=FFT_UNROLL)


def _fft_filter_spectrum(ad, ft):
    _, l, n = ad.shape
    n1, n2, cc = FFT_INNER, ft["n2"], FFT_COLS
    h = n2 // 2
    res = lambda a: pl.BlockSpec(a.shape, lambda j: (0,) * a.ndim)
    out = pl.BlockSpec((n2, n1, cc), lambda j: (0, 0, j))
    tabs = [ft["f1r_hi"], ft["m_hi"]]
    return pl.pallas_call(
        _fftspec_kernel, name="fft_filter_spectrum",
        grid=(n // cc,),
        in_specs=[pl.BlockSpec((1, l, cc), lambda j: (0, 0, j)), pl.BlockSpec((1, l, cc), lambda j: (1, 0, j))]
                 + [res(t) for t in tabs],
        out_specs=[out, out],
        out_shape=[jax.ShapeDtypeStruct((n2, n1, n), F32)] * 2,
        scratch_shapes=[pltpu.VMEM((n1 * h, cc), BF16), pltpu.VMEM((2 * n2 * n1, cc), BF16),
                        pltpu.VMEM((2 * n2 * n1, cc), BF16)],
        compiler_params=_cparams(("parallel",), 56),
    )(ad, ad, *tabs)


def _softmax_weights(s, sink=None):
    m = jnp.max(s, axis=-1, keepdims=True)
    if sink is not None:
        m = jnp.maximum(m, sink)
    p = jnp.exp(s - m)
    den = jnp.sum(p, axis=-1, keepdims=True)
    if sink is not None:
        den = den + jnp.exp(sink - m)
    return p.astype(BF16), den


def _softmax_pv(s, v, sink=None):
    p, den = _softmax_weights(s, sink)
    return _dot(p, v) / den


def _qk(q, k):
    return lax.dot_general(q, k, (((1,), (1,)), ((), ())), preferred_element_type=F32)


def _half_mask(q_tile, half):
    lane = lax.broadcasted_iota(jnp.int32, q_tile.shape, 1)
    keep = (lane < HEAD_DIM) if half == 0 else (lane >= HEAD_DIM)
    return jnp.where(keep, q_tile, jnp.zeros_like(q_tile))


def _merge_halves(o_even, o_odd):
    lane = lax.broadcasted_iota(jnp.int32, o_even.shape, 1)
    return jnp.where(lane < HEAD_DIM, o_even, o_odd)


SWA_CHAIN_HEADS = 2

def _swa_kernel(sink_ref, q_ref, kp_ref, kc_ref, kn_ref, vp_ref, vc_ref, vn_ref, kx_ref, vx_ref, o_ref):
    n = pl.program_id(1)
    nb = pl.num_programs(1)
    blk = SWA_BLOCK
    g = SWA_CHAIN_HEADS
    n_ctx = kx_ref.shape[1]
    rows = g * blk
    ri = lax.broadcasted_iota(jnp.int32, (rows, 3 * blk + n_ctx), 0) & (blk - 1)
    ci = lax.broadcasted_iota(jnp.int32, (rows, 3 * blk + n_ctx), 1)
    ok_prev = (ci < blk) & (ci >= ri) & (n > 0)
    ok_cur = (ci >= blk) & (ci < 2 * blk)
    ok_next = (ci >= 2 * blk) & (ci < 3 * blk) & (ci - 2 * blk <= ri) & (n < nb - 1)
    ok = ok_prev | ok_cur | ok_next | (ci >= 3 * blk)
    bias = jnp.where(ok, 0.0, NEG_BIG)
    rsel = lax.broadcasted_iota(jnp.int32, (rows, 1), 0) >> (blk.bit_length() - 1)
    outs = [None] * SWA_HEADS
    chains = range(SWA_HEADS // g)
    group = SWA_HEADS // SWA_KV_HEADS
    lanes = [slice((c * g // group) * LANES, (c * g // group + 1) * LANES) for c in chains]
    scores, sinks = [], []
    for c in chains:
        cs = lanes[c]
        k = jnp.concatenate([kp_ref[0, :, cs], kc_ref[0, :, cs], kn_ref[0, :, cs], kx_ref[0, :, cs]], axis=0)
        q_parts = []
        sink = jnp.zeros((rows, 1), F32)
        for gi in range(g):
            h = c * g + gi
            q_parts.append(_half_mask(q_ref[0, :, (h // 2) * LANES:(h // 2 + 1) * LANES], h % 2))
            sink = jnp.where(rsel == gi, sink_ref[h], sink)
        scores.append(_qk(jnp.concatenate(q_parts, axis=0), k))
        sinks.append(sink)
    probs = [_softmax_weights(scores[c] + bias, sinks[c]) for c in chains]
    for c in chains:
        cs = lanes[c]
        v = jnp.concatenate([vp_ref[0, :, cs], vc_ref[0, :, cs], vn_ref[0, :, cs], vx_ref[0, :, cs]], axis=0)
        pw, den = probs[c]
        o = _dot(pw, v) / den
        for gi in range(g):
            outs[c * g + gi] = o[gi * blk:(gi + 1) * blk]
    for p in range(SWA_HEADS // 2):
        o_ref[0, :, p * LANES:(p + 1) * LANES] = _merge_halves(outs[2 * p], outs[2 * p + 1]).astype(o_ref.dtype)


def _window_gqa(q, k, v, kx, vx, sink):
    b, l, _ = q.shape
    nb = l // SWA_BLOCK
    n_ctx = kx.shape[1]
    kvw = k.shape[2]

    def blk(off):
        return pl.BlockSpec((1, SWA_BLOCK, kvw), lambda b, n: (b, jnp.clip(n + off, 0, nb - 1), 0))

    ctx = pl.BlockSpec((1, n_ctx, kvw), lambda b, n: (b, 0, 0))
    return pl.pallas_call(
        _swa_kernel, name="window_gqa",
        grid=(b, nb),
        in_specs=[
            pl.BlockSpec(memory_space=pltpu.SMEM),
            pl.BlockSpec((1, SWA_BLOCK, q.shape[2]), lambda b, n: (b, n, 0)),
            blk(-1), blk(0), blk(1), blk(-1), blk(0), blk(1), ctx, ctx,
        ],
        out_specs=pl.BlockSpec((1, SWA_BLOCK, q.shape[2]), lambda b, n: (b, n, 0)),
        out_shape=jax.ShapeDtypeStruct(q.shape, BF16),
        compiler_params=_cparams(("parallel", "parallel"), 32),
    )(sink, q, k, k, k, v, v, v, kx, vx)


def _na_bias_kernel(rpb_ref, o_ref):
    h = pl.program_id(0)
    nd = 2 * NA_COLS - 1
    nr = 2 * NA_MAX_ROWS - 1
    q = lax.broadcasted_iota(jnp.int32, (GRID_W, LANES), 0)
    lane = lax.broadcasted_iota(jnp.int32, (GRID_W, LANES), 1)
    kc = lane & (GRID_W - 1)
    upper = lane >= GRID_W
    cstart = jnp.clip(q - NA_COLS // 2, 0, GRID_W - NA_COLS)
    valid = (kc >= cstart) & (kc < cstart + NA_COLS)
    dc = jnp.clip(kc - q, -(NA_COLS - 1), NA_COLS - 1) + (NA_COLS - 1)
    tiles = []
    for dr in range(nr - 1):
        base0 = (h * nr + dr) * nd
        base1 = base0 + nd
        t = jnp.zeros((GRID_W, LANES), F32)
        for d in range(nd):
            val = jnp.where(upper, rpb_ref[base1 + d], rpb_ref[base0 + d])
            t = jnp.where(dc == d, val, t)
        tiles.append(jnp.where(valid, t, NEG_BIG))
    for off in range(NA_MAX_ROWS):
        for j in range(NA_MAX_ROWS // 2):
            o_ref[off, 0, :, j * LANES:(j + 1) * LANES] = tiles[2 * j - off + NA_MAX_ROWS - 1]


def _na_bias_table(rpb):
    return pl.pallas_call(
        _na_bias_kernel, name="na_bias_table",
        grid=(NA_HEADS,),
        in_specs=[pl.BlockSpec(memory_space=pltpu.SMEM)],
        out_specs=pl.BlockSpec((NA_MAX_ROWS, 1, GRID_W, NA_MAX_ROWS * GRID_W), lambda h: (0, h, 0, 0)),
        out_shape=jax.ShapeDtypeStruct((NA_MAX_ROWS, NA_HEADS, GRID_W, NA_MAX_ROWS * GRID_W), F32),
        compiler_params=_cparams(("arbitrary",), 16),
    )(rpb.reshape(-1))


NA_ROWS_PER_STEP = 8


def _na_kernel(q_ref, k_ref, v_ref, kx_ref, vx_ref, bias_ref, o_ref, *, rows):
    wr = NA_MAX_ROWS
    nloc = wr * GRID_W
    pairs = range(NA_HEADS // 2)
    lanes = [slice(p * LANES, (p + 1) * LANES) for p in pairs]
    for i in range(NA_ROWS_PER_STEP):
        r = pl.program_id(1) * NA_ROWS_PER_STEP + i
        first = jnp.clip(r - wr // 2, 0, rows - wr)
        start = pl.multiple_of(first * GRID_W, GRID_W)
        off = r - first
        qr = slice(i * GRID_W, (i + 1) * GRID_W)
        scores = []
        for p in pairs:
            q_tile = q_ref[0, qr, lanes[p]]
            q2 = jnp.concatenate([_half_mask(q_tile, 0), _half_mask(q_tile, 1)], axis=0)
            k = jnp.concatenate([k_ref[0, pl.ds(start, nloc), lanes[p]], kx_ref[0, :, lanes[p]]], axis=0)
            scores.append(_qk(q2, k))
        probs = []
        for p in pairs:
            s = scores[p]
            bias = jnp.concatenate([bias_ref[off, 2 * p], bias_ref[off, 2 * p + 1]], axis=0)
            probs.append(_softmax_weights(jnp.concatenate([s[:, :nloc] + bias, s[:, nloc:]], axis=1)))
        for p in pairs:
            pw, den = probs[p]
            v = jnp.concatenate([v_ref[0, pl.ds(start, nloc), lanes[p]], vx_ref[0, :, lanes[p]]], axis=0)
            o = _dot(pw, v) / den
            o_ref[0, qr, lanes[p]] = _merge_halves(o[:GRID_W], o[GRID_W:]).astype(o_ref.dtype)


def _neighbourhood_attention(q, k, v, kx, vx, bias_tab):
    b, l, w = q.shape
    rows = l // GRID_W
    n_ctx = kx.shape[1]
    tq = NA_ROWS_PER_STEP * GRID_W
    full = pl.BlockSpec((1, l, w), lambda b, r: (b, 0, 0))
    ctx = pl.BlockSpec((1, n_ctx, w), lambda b, r: (b, 0, 0))
    return pl.pallas_call(
        functools.partial(_na_kernel, rows=rows), name="neighbourhood_attention",
        grid=(b, rows // NA_ROWS_PER_STEP),
        in_specs=[pl.BlockSpec((1, tq, w), lambda b, r: (b, r, 0)), full, full, ctx, ctx,
                  _resident(bias_tab.shape)],
        out_specs=pl.BlockSpec((1, tq, w), lambda b, r: (b, r, 0)),
        out_shape=jax.ShapeDtypeStruct(q.shape, BF16),
        compiler_params=_cparams(("parallel", "arbitrary"), 48),
    )(q, k, v, kx, vx, bias_tab)


def _ctx_attn_kernel(sink_ref, qs_ref, ks_ref, vs_ref, qn_ref, kn_ref, vn_ref, os_ref, on_ref):
    n = qs_ref.shape[1]
    g = SWA_HEADS // SWA_KV_HEADS
    rsel = lax.broadcasted_iota(jnp.int32, (g * n, 1), 0) >> (n.bit_length() - 1)
    outs = [None] * SWA_HEADS
    for kv in range(SWA_KV_HEADS):
        cs = slice(kv * LANES, (kv + 1) * LANES)
        q_parts = []
        sink = jnp.zeros((g * n, 1), F32)
        for gi in range(g):
            h = kv * g + gi
            q_parts.append(_half_mask(qs_ref[0, :, (h // 2) * LANES:(h // 2 + 1) * LANES], h % 2))
            sink = jnp.where(rsel == gi, sink_ref[h], sink)
        o = _softmax_pv(_qk(jnp.concatenate(q_parts, axis=0), ks_ref[0, :, cs]), vs_ref[0, :, cs], sink)
        for gi in range(g):
            outs[kv * g + gi] = o[gi * n:(gi + 1) * n]
    for p in range(SWA_HEADS // 2):
        os_ref[0, :, p * LANES:(p + 1) * LANES] = _merge_halves(outs[2 * p], outs[2 * p + 1]).astype(os_ref.dtype)
    for p in range(NA_HEADS // 2):
        cs = slice(p * LANES, (p + 1) * LANES)
        q_tile = qn_ref[0, :, cs]
        q2 = jnp.concatenate([_half_mask(q_tile, 0), _half_mask(q_tile, 1)], axis=0)
        o = _softmax_pv(_qk(q2, kn_ref[0, :, cs]), vn_ref[0, :, cs])
        on_ref[0, :, cs] = _merge_halves(o[:n], o[n:]).astype(on_ref.dtype)


def _context_attention(qs, ks, vs, qn, kn, vn, sink):
    b, n, w = qs.shape
    spec = lambda a: pl.BlockSpec((1,) + a.shape[1:], lambda b: (b, 0, 0))
    return pl.pallas_call(
        _ctx_attn_kernel, name="context_attention",
        grid=(b,),
        in_specs=[pl.BlockSpec(memory_space=pltpu.SMEM)] + [spec(a) for a in (qs, ks, vs, qn, kn, vn)],
        out_specs=[spec(qs), spec(qn)],
        out_shape=[jax.ShapeDtypeStruct(qs.shape, BF16), jax.ShapeDtypeStruct(qn.shape, BF16)],
        compiler_params=_cparams(("parallel",), 32),
    )(sink, qs, ks, vs, qn, kn, vn)


def _merge_ffn_kernel(x_ref, mg_ref, yh_ref, ys_ref, yn_ref, zg_ref, wb_ref, wo_ref,
                      sh_ref, sc_ref, gt_ref, g_ref, wg_ref, wu_ref, wd_ref, o_ref, *, fc):
    d = D_MODEL
    yh = jnp.concatenate([yh_ref[0, c] for c in range(yh_ref.shape[1])], axis=1)
    ys = (yh.astype(BF16), ys_ref[0], yn_ref[0])
    m = None
    for n in range(3):
        proj = _dot(ys[n], wb_ref[n])
        term = zg_ref[0, :, n * d:(n + 1) * d].astype(F32) * proj
        m = term if m is None else m + term
    x = x_ref[0] + mg_ref[0] * _dot(m.astype(BF16), wo_ref[...])
    o_ref[0] = _swiglu_residual(x, sh_ref, sc_ref, gt_ref, g_ref, wg_ref, wu_ref, wd_ref, fc)


def _merge_and_ffn(x, mix_gate, y_hy, y_swa, y_na, gates, w_branch, w_out, layer,
                   shift, scale, gate, gain, wg, wu, wd):
    b, l, d = x.shape
    tm = min(512, l)
    tok = lambda n: pl.BlockSpec((1, tm, n), lambda b, t: (b, t, 0))
    idx = (layer, 1)
    return pl.pallas_call(
        functools.partial(_merge_ffn_kernel, fc=FFN_CHUNK), name="merge_ffn",
        grid=(b, l // tm),
        in_specs=[tok(d), _mod_spec(mix_gate),
                  pl.BlockSpec((1, HY_WIDTH // HY_CHUNK, tm, HY_CHUNK), lambda b, t: (b, 0, t, 0)),
                  tok(HY_WIDTH), tok(HY_WIDTH), tok(3 * d),
                  _layer_slab(w_branch, (layer,)), _layer_slab(w_out, (layer,)),
                  _mod_spec(shift), _mod_spec(scale), _mod_spec(gate), _resident((1, d)),
                  _layer_slab(wg, idx), _layer_slab(wu, idx), _layer_slab(wd, idx)],
        out_specs=tok(d),
        out_shape=jax.ShapeDtypeStruct(x.shape, F32),
        compiler_params=_cparams(("parallel", "parallel"), 58),
    )(x, mix_gate[0], y_hy, y_swa, y_na, gates, w_branch, w_out, shift[0], scale[0], gate[0], gain.reshape(1, d),
      wg, wu, wd)


def _dft_tables(l):
    n = 2 * l
    fb = 256 if l >= 1024 else LANES * ((l + 1 + LANES - 1) // LANES)
    fp = fb * ((l + 1 + fb - 1) // fb)
    k = jnp.arange(fp, dtype=jnp.int32)[:, None]
    t = jnp.arange(l, dtype=jnp.int32)[None, :]
    ang = ((k * t) % n).astype(F32) * (2.0 * math.pi / n)
    live = k <= l
    c = jnp.where(live, jnp.cos(ang), 0.0)
    s = jnp.where(live, jnp.sin(ang), 0.0)
    c_hi = c.astype(BF16)
    s_hi = s.astype(BF16)
    wk = jnp.where((k == 0) | (k == l), 1.0, 2.0) * jnp.where(live, 1.0 / n, 0.0)
    return dict(fb=fb, c_hi=c_hi, s_hi=s_hi, ct_hi=c_hi.T, st_hi=s_hi.T, wk=wk.astype(F32))


def _filter_features(l):
    t = jnp.linspace(0.0, 1.0, l, dtype=F32)[:, None]
    w = (2.0 * math.pi / l) * jnp.arange(l, dtype=F32)[:, None]
    f = jnp.linspace(1e-4, HY_BANDS - 1, HY_BANDS, dtype=F32)[None, :]
    z = jnp.concatenate([t, jnp.cos(f * w), -jnp.sin(f * w)], axis=-1)
    return jnp.pad(z, ((0, 0), (0, LANES - HY_EMB)))


def _decay_rates():
    max_decay = math.log(HY_TARGET) / HY_FAST_DECAY
    min_decay = math.log(HY_TARGET) / HY_SLOW_DECAY
    return jnp.abs(jnp.linspace(min_decay, max_decay, HY_WIDTH, dtype=F32))[None, :]


def _rope_tables(l):
    pos = jnp.arange(l)
    row = (pos // GRID_W).astype(F32)
    col = (pos % GRID_W).astype(F32)
    half = HEAD_DIM // 2
    inv = 1.0 / (ROPE_BASE ** (jnp.arange(0, half, 2, dtype=F32) / half))
    ar = row[:, None] * inv[None, :]
    ac = col[:, None] * inv[None, :]
    cos = jnp.concatenate([jnp.cos(ar), jnp.cos(ar), jnp.cos(ac), jnp.cos(ac)], axis=-1)
    sin = jnp.concatenate([-jnp.sin(ar), jnp.sin(ar), -jnp.sin(ac), jnp.sin(ac)], axis=-1)
    return jnp.tile(cos, (1, 2)), jnp.tile(sin, (1, 2))


def _pad_to(a, shape):
    return jnp.pad(a, [(0, s - d) for d, s in zip(a.shape, shape)])


def _use_fft(bsz, length):
    return bsz % 2 == 0 and length % (8 * FFT_INNER) == 0


def _conv_tables(bsz, length):
    return _fft_tables(length) if _use_fft(bsz, length) else _dft_tables(length)


def _hyena_branch(z_hy, short_w, short_b, feat, fparams, deltas, bias, tabs):
    ad = _hyena_filter_sums(feat, fparams, deltas)
    if "n2" in tabs:
        kr, ki = _fft_filter_spectrum(ad, tabs)
        y1 = _fft_gated_long_conv(None, 0, z_hy, 1, short_w, short_b, kr, ki, 0, bias[0], tabs, F32)
        return _fft_gated_long_conv(y1, 0, z_hy, 2, short_w, short_b, kr, ki, 1, bias[1], tabs, BF16)
    v, x1, x2 = _short_conv3(z_hy, short_w, short_b)
    kr, ki = _filter_spectrum(ad, tabs)
    conv = functools.partial(_gated_long_conv, tabs=tabs)
    y1 = conv(v, x1, kr, ki, 0, bias[0])
    y = conv(y1, x2, kr, ki, 1, bias[1])
    bsz, length, _ = y.shape
    return y.reshape(bsz, length, HY_WIDTH // HY_CHUNK, HY_CHUNK).transpose(0, 2, 1, 3)


def kernel(x, c, ctx, c_ctx, w_ada, b_ada, norm_g, ffn_w_gate, ffn_w_up, ffn_w_down,
           w_in, hy_short_w, hy_short_b, hy_pe_w0, hy_pe_b0, hy_pe_w1, hy_pe_b1,
           hy_pe_w2, hy_pe_b2, hy_pe_wout, hy_sin_freq, hy_bias,
           swa_q_gain, swa_k_gain, swa_sink, na_q_gain, na_k_gain, na_rpb,
           w_branch, w_out):
    bsz, seq, d = x.shape
    n_ctx = ctx.shape[1]
    depth = w_ada.shape[0]

    c16 = _pad_to(jnp.concatenate([c, c_ctx[None, :]], axis=0), (MOD_ROWS, d))
    mods = _adaln_mods(c16, w_ada, b_ada).reshape(depth, MOD_ROWS, N_MOD, 1, d)

    tabs_x = _conv_tables(bsz, seq)
    tabs_c = _conv_tables(bsz, n_ctx)
    feat_x = _filter_features(seq)
    feat_c = _filter_features(n_ctx)
    deltas = _decay_rates()
    cos_x, sin_x = _rope_tables(seq)
    cos_c = jnp.ones((n_ctx, LANES), F32)
    sin_c = jnp.zeros((n_ctx, LANES), F32)
    eye = jnp.arange(MXU_DIM) // HEAD_DIM
    bd = (eye[:, None] == eye[None, :]).astype(BF16)

    wg = ffn_w_gate.astype(BF16)
    wu = ffn_w_up.astype(BF16)
    wd = ffn_w_down.astype(BF16)
    wb = w_branch.astype(BF16)
    wo = w_out.astype(BF16)
    win = w_in.astype(BF16)
    kv = win[:, :, _OFF_SWA + _SWA_Q:_OFF_NA].reshape(depth, d, 2, SWA_KV_HEADS, 1, HEAD_DIM)
    win_kv = jnp.broadcast_to(kv, (depth, d, 2, SWA_KV_HEADS, 2, HEAD_DIM)).reshape(depth, d, 4 * _SWA_KV)

    xc = ctx
    for i in range(depth):
        last = i == depth - 1
        mx = lambda j: (mods, i, j, None)
        mc = lambda j: (mods, i, j, bsz)
        tile2 = lambda g: jnp.tile(g, 2 * MXU_DIM // LANES)
        head_gains = jnp.stack([tile2(swa_q_gain[i]), tile2(swa_k_gain[i]), tile2(na_q_gain[i]), tile2(na_k_gain[i])])
        fparams = (_pad_to(hy_pe_w0[i], (LANES, LANES)), _pad_to(hy_pe_b0[i][None], (1, LANES)),
                   _pad_to(hy_pe_w1[i], (LANES, LANES)), _pad_to(hy_pe_b1[i][None], (1, LANES)),
                   _pad_to(hy_pe_w2[i], (LANES, LANES)), _pad_to(hy_pe_b2[i][None], (1, LANES)),
                   _pad_to(hy_pe_wout[i], (LANES, 4 * HY_WIDTH)), _pad_to(hy_sin_freq[i][None], (1, LANES)))

        x = _ffn_half_step(x, mx(0), mx(1), mx(2), norm_g[i, 0], wg, wu, wd, (i, 0))
        xc = _ffn_half_step(xc, mc(0), mc(1), mc(2), norm_g[i, 0], wg, wu, wd, (i, 0))

        z_hy, q_s, k_s, v_s, q_n, k_n, v_n, gates = _in_projection(
            x, mx(3), mx(4), norm_g[i, 1], cos_x, sin_x, head_gains, bd, win, win_kv, i)
        zc_hy, qc_s, kc_s, vc_s, qc_n, kc_n, vc_n, gates_c = _in_projection(
            xc, mc(3), mc(4), norm_g[i, 1], cos_c, sin_c, head_gains, bd, win, win_kv, i)

        y_hy = _hyena_branch(z_hy, hy_short_w[i], hy_short_b[i], feat_x, fparams, deltas, hy_bias[i], tabs_x)
        y_swa = _window_gqa(q_s, k_s, v_s, kc_s, vc_s, swa_sink[i])
        y_na = _neighbourhood_attention(q_n, k_n, v_n, kc_n, vc_n, _na_bias_table(na_rpb[i]))
        x = _merge_and_ffn(x, mx(5), y_hy, y_swa, y_na, gates, wb, wo, i,
                           mx(6), mx(7), mx(8), norm_g[i, 2], wg, wu, wd)

        if not last:
            yc_hy = _hyena_branch(zc_hy, hy_short_w[i], hy_short_b[i], feat_c, fparams, deltas, hy_bias[i], tabs_c)
            yc_swa, yc_na = _context_attention(qc_s, kc_s, vc_s, qc_n, kc_n, vc_n, swa_sink[i])
            xc = _merge_and_ffn(xc, mc(5), yc_hy, yc_swa, yc_na, gates_c, wb, wo, i,
                                mc(6), mc(7), mc(8), norm_g[i, 2], wg, wu, wd)
    return x
```

```python
import functools
import math

import jax
import jax.numpy as jnp
from jax import lax
from jax.experimental import pallas as pl
from jax.experimental.pallas import tpu as pltpu

F32 = jnp.float32
BF16 = jnp.bfloat16

D_MODEL = 1024
DEPTH = 4
GRID_W = 64
HEAD_DIM = 64
N_MOD = 9
RMS_EPS = 1e-6

HY_WIDTH = D_MODEL // 2
HY_CHUNK = 256
HY_EMB = 33
HY_BANDS = (HY_EMB - 1) // 2
HY_FFN = 64
HY_FAST_DECAY = 0.3
HY_SLOW_DECAY = 1.5
HY_TARGET = 1e-2

SWA_HEADS = 8
SWA_KV_HEADS = 2
SWA_WINDOW = 128
SWA_BLOCK = 128
ROPE_BASE = 10000.0

NA_HEADS = 8
NA_MAX_ROWS = 8
NA_COLS = 16
NA_COL_BLOCK = 16

FFN_HIDDEN = 256 * ((8 * D_MODEL // 3 + 255) // 256)
FFN_CHUNK = 1536

LANES = 128
MXU_DIM = 256
MOD_ROWS = 16
NEG_BIG = -1e30


def _cparams(sem, vmem_mb):
    return pltpu.CompilerParams(dimension_semantics=sem, vmem_limit_bytes=vmem_mb * 1024 * 1024)


def _resident(shape):
    nd = len(shape)
    return pl.BlockSpec(shape, lambda *_: (0,) * nd, pipeline_mode=pl.Buffered(1))


def _layer_slab(arr, idx):
    rest = arr.shape[len(idx):]
    return pl.BlockSpec((None,) * len(idx) + rest, lambda *_: tuple(idx) + (0,) * len(rest),
                        pipeline_mode=pl.Buffered(1))


def _split(x):
    hi = x.astype(BF16)
    lo = (x - hi.astype(F32)).astype(BF16)
    return hi, lo


def _dot(a, b):
    return jnp.dot(a, b, preferred_element_type=F32)


def _dot3(a_hi, a_lo, b_hi, b_lo):
    return _dot(a_hi, b_hi) + _dot(a_lo, b_hi) + _dot(a_hi, b_lo)


def _dot3f(a, b):
    a_hi, a_lo = _split(a)
    b_hi, b_lo = _split(b)
    return _dot3(a_hi, a_lo, b_hi, b_lo)


def _rms_mod(x, gain, shift, scale):
    ms = jnp.mean(x * x, axis=-1, keepdims=True)
    y = x * lax.rsqrt(ms + RMS_EPS) * gain
    return y * (1.0 + scale) + shift


def _head_sumsq(x, bd):
    return [_dot(jnp.square(x[:, j * MXU_DIM:(j + 1) * MXU_DIM]).astype(BF16), bd) for j in range(x.shape[1] // MXU_DIM)]


def _head_normalize(x, sumsq, gain):
    outs = [x[:, j * MXU_DIM:(j + 1) * MXU_DIM] * lax.rsqrt(s * (1.0 / HEAD_DIM) + RMS_EPS) * gain
            for j, s in enumerate(sumsq)]
    return outs[0] if len(outs) == 1 else jnp.concatenate(outs, axis=1)


def _rope(x, cos, sin):
    lane = lax.broadcasted_iota(jnp.int32, (x.shape[0], LANES), 1)
    first = (lane & 31) < 16
    outs = []
    for j in range(x.shape[1] // LANES):
        xc = x[:, j * LANES:(j + 1) * LANES]
        partner = jnp.where(first, pltpu.roll(xc, LANES - 16, axis=1), pltpu.roll(xc, 16, axis=1))
        outs.append(xc * cos + partner * sin)
    return outs[0] if len(outs) == 1 else jnp.concatenate(outs, axis=1)


def _mods_kernel(c_ref, w_ref, b_ref, o_ref):
    c = c_ref[...]
    a = c * jax.nn.sigmoid(c)
    o_ref[0] = _dot3f(a, w_ref[0]) + b_ref[0]


def _adaln_mods(c16, w_ada, b_ada):
    depth, d, n = w_ada.shape
    nb = 1152
    return pl.pallas_call(
        _mods_kernel, name="adaln_mods",
        grid=(depth, n // nb),
        in_specs=[
            pl.BlockSpec((MOD_ROWS, d), lambda i, j: (0, 0)),
            pl.BlockSpec((1, d, nb), lambda i, j: (i, 0, j)),
            pl.BlockSpec((1, 1, nb), lambda i, j: (i, 0, j)),
        ],
        out_specs=pl.BlockSpec((1, MOD_ROWS, nb), lambda i, j: (i, 0, j)),
        out_shape=jax.ShapeDtypeStruct((depth, MOD_ROWS, n), F32),
        compiler_params=_cparams(("arbitrary", "arbitrary"), 48),
    )(c16, w_ada, b_ada.reshape(depth, 1, n))


def _swiglu_residual(x, sh_ref, sc_ref, gt_ref, g_ref, wg_ref, wu_ref, wd_ref, fc):
    hb = _rms_mod(x, g_ref[...], sh_ref[0], sc_ref[0]).astype(BF16)
    acc = None
    for f0 in range(0, FFN_HIDDEN, fc):
        f1 = min(f0 + fc, FFN_HIDDEN)
        g = _dot(hb, wg_ref[:, f0:f1])
        u = _dot(hb, wu_ref[:, f0:f1])
        a = (g * jax.nn.sigmoid(g) * u).astype(BF16)
        d = _dot(a, wd_ref[f0:f1, :])
        acc = d if acc is None else acc + d
    return x + 0.5 * gt_ref[0] * acc


def _ffn_kernel(x_ref, sh_ref, sc_ref, gt_ref, g_ref, wg_ref, wu_ref, wd_ref, o_ref, *, fc):
    o_ref[0] = _swiglu_residual(x_ref[0], sh_ref, sc_ref, gt_ref, g_ref, wg_ref, wu_ref, wd_ref, fc)


def _mod_spec(sel):
    _, layer, j, row = sel
    if row is None:
        return pl.BlockSpec((None, 1, None, 1, D_MODEL), lambda b, t: (layer, b, j, 0, 0))
    return pl.BlockSpec((None, 1, None, 1, D_MODEL), lambda b, t: (layer, row, j, 0, 0))


def _ffn_half_step(x, shift, scale, gate, gain, wg, wu, wd, idx):
    b, l, d = x.shape
    tm = min(512, l)
    return pl.pallas_call(
        functools.partial(_ffn_kernel, fc=FFN_CHUNK), name="ffn_half_step",
        grid=(b, l // tm),
        in_specs=[
            pl.BlockSpec((1, tm, d), lambda b, t: (b, t, 0)),
            _mod_spec(shift), _mod_spec(scale), _mod_spec(gate),
            _resident((1, d)),
            _layer_slab(wg, idx), _layer_slab(wu, idx), _layer_slab(wd, idx),
        ],
        out_specs=pl.BlockSpec((1, tm, d), lambda b, t: (b, t, 0)),
        out_shape=jax.ShapeDtypeStruct(x.shape, F32),
        compiler_params=_cparams(("parallel", "parallel"), 56),
    )(x, shift[0], scale[0], gate[0], gain.reshape(1, d), wg, wu, wd)


_HY_COLS = 3 * HY_WIDTH
_SWA_Q = SWA_HEADS * HEAD_DIM
_SWA_KV = SWA_KV_HEADS * HEAD_DIM
_NA_W = NA_HEADS * HEAD_DIM
_OFF_SWA = _HY_COLS
_OFF_NA = _OFF_SWA + _SWA_Q + 2 * _SWA_KV
_OFF_GATE = _OFF_NA + 3 * _NA_W


def _inproj_kernel(x_ref, sh_ref, sc_ref, g_ref, cos_ref, sin_ref, hg_ref, bd_ref, w_ref, wkv_ref,
                   zhy_ref, qs_ref, ks_ref, vs_ref, qn_ref, kn_ref, vn_ref, gt_ref):
    hb = _rms_mod(x_ref[0], g_ref[...], sh_ref[0], sc_ref[0]).astype(BF16)
    cos = cos_ref[...]
    sin = sin_ref[...]
    bd = bd_ref[...]
    scale = HEAD_DIM ** -0.5
    proj = lambda lo, n: _dot(hb, w_ref[:, lo:lo + n])
    kvw = 2 * _SWA_KV
    raw = [proj(_OFF_SWA, _SWA_Q), _dot(hb, wkv_ref[:, :kvw]), proj(_OFF_NA, _NA_W), proj(_OFF_NA + _NA_W, _NA_W)]
    gates = proj(_OFF_GATE, 3 * D_MODEL)
    sums = [_head_sumsq(r, bd) for r in raw]
    vs_ref[0] = _dot(hb, wkv_ref[:, kvw:]).astype(BF16)
    vn_ref[0] = proj(_OFF_NA + 2 * _NA_W, _NA_W).astype(BF16)
    zhy = proj(0, _HY_COLS).astype(BF16)
    for c in range(_HY_COLS // HY_CHUNK):
        zhy_ref[0, c] = zhy[:, c * HY_CHUNK:(c + 1) * HY_CHUNK]
    gt_ref[0] = jax.nn.sigmoid(gates).astype(BF16)
    qs, ks, qn, kn = [_head_normalize(r, s, hg_ref[i:i + 1, :]) for i, (r, s) in enumerate(zip(raw, sums))]
    qs_ref[0] = (_rope(qs, cos, sin) * scale).astype(BF16)
    ks_ref[0] = _rope(ks, cos, sin).astype(BF16)
    qn_ref[0] = (qn * scale).astype(BF16)
    kn_ref[0] = kn.astype(BF16)


def _in_projection(x, shift, scale, gain, cos, sin, head_gains, bd, w, w_kv_dup, layer):
    b, l, d = x.shape
    tm = min(512, l)
    kvw = 2 * _SWA_KV
    widths = [_HY_COLS, _SWA_Q, kvw, kvw, _NA_W, _NA_W, _NA_W, 3 * d]
    dtypes = [BF16] * 8
    return pl.pallas_call(
        _inproj_kernel, name="in_projection",
        grid=(b, l // tm),
        in_specs=[
            pl.BlockSpec((1, tm, d), lambda b, t: (b, t, 0)),
            _mod_spec(shift), _mod_spec(scale),
            _resident((1, d)),
            pl.BlockSpec((tm, LANES), lambda b, t: (t, 0)),
            pl.BlockSpec((tm, LANES), lambda b, t: (t, 0)),
            _resident(head_gains.shape), _resident(bd.shape), _layer_slab(w, (layer,)), _layer_slab(w_kv_dup, (layer,)),
        ],
        out_specs=[pl.BlockSpec((1, _HY_COLS // HY_CHUNK, tm, HY_CHUNK), lambda b, t: (b, 0, t, 0))]
                  + [pl.BlockSpec((1, tm, n), lambda b, t: (b, t, 0)) for n in widths[1:]],
        out_shape=[jax.ShapeDtypeStruct((b, _HY_COLS // HY_CHUNK, l, HY_CHUNK), BF16)]
                  + [jax.ShapeDtypeStruct((b, l, n), dt) for n, dt in zip(widths[1:], dtypes[1:])],
        compiler_params=_cparams(("parallel", "parallel"), 56),
    )(x, shift[0], scale[0], gain.reshape(1, d), cos, sin, head_gains, bd, w, w_kv_dup)


def _shortconv_kernel(z0_ref, z1_ref, z2_ref, w_ref, b_ref, o0_ref, o1_ref, o2_ref):
    nb, _, l, _ = z0_ref.shape
    row = lax.broadcasted_iota(jnp.int32, z0_ref.shape[2:], 0)
    for g, (z_ref, o_ref) in enumerate(((z0_ref, o0_ref), (z1_ref, o1_ref), (z2_ref, o2_ref))):
        for i in range(nb):
            z = z_ref[i, 0].astype(F32)
            prev = jnp.where(row == 0, 0.0, pltpu.roll(z, 1, axis=0))
            nxt = jnp.where(row == l - 1, 0.0, pltpu.roll(z, l - 1, axis=0))
            o_ref[i] = w_ref[g, 0:1, :] * prev + w_ref[g, 1:2, :] * z + w_ref[g, 2:3, :] * nxt + b_ref[g]


def _short_conv3(z, w, bias):
    b, _, l, cb = z.shape
    ncb = HY_WIDTH // cb
    wg = w.reshape(3, 3, HY_WIDTH).transpose(1, 0, 2)
    zin = lambda g: pl.BlockSpec((b, 1, l, cb), lambda j: (0, g * ncb + j, 0, 0))
    out = pl.BlockSpec((b, l, cb), lambda j: (0, 0, j))
    return pl.pallas_call(
        _shortconv_kernel, name="short_conv3",
        grid=(ncb,),
        in_specs=[zin(0), zin(1), zin(2),
                  pl.BlockSpec((3, 3, cb), lambda j: (0, 0, j)),
                  pl.BlockSpec((3, 1, cb), lambda j: (0, 0, j))],
        out_specs=[out, out, out],
        out_shape=[jax.ShapeDtypeStruct((b, l, HY_WIDTH), F32)] * 3,
        compiler_params=_cparams(("parallel",), 48),
    )(z, z, z, wg, bias.reshape(3, 1, HY_WIDTH))


def _filter_kernel(z_ref, w0, b0, w1, b1, w2, b2, wo, fr_ref, dl_ref, o_ref):
    z = z_ref[...]
    fr = fr_ref[...]
    a = jnp.sin(fr * (_dot3f(z, w0[...]) + b0[...]))
    a = jnp.sin(fr * (_dot3f(a, w1[...]) + b1[...]))
    a = jnp.sin(fr * (_dot3f(a, w2[...]) + b2[...]))
    hh = _dot3f(a, wo[...])
    t = z[:, 0:1]
    win = jnp.exp(-t * dl_ref[...])
    row = lax.broadcasted_iota(jnp.int32, win.shape, 0) + pl.program_id(0) * z.shape[0]
    w = HY_WIDTH
    for o in range(2):
        hp = hh[:, (2 * o) * w:(2 * o + 1) * w] * win
        hn = jnp.where(row == 0, 0.0, hh[:, (2 * o + 1) * w:(2 * o + 2) * w] * win)
        o_ref[0, :, o * w:(o + 1) * w] = hp + hn
        o_ref[1, :, o * w:(o + 1) * w] = hp - hn


def _hyena_filter_sums(zfeat, fp, deltas):
    l = zfeat.shape[0]
    tl = min(256, l)
    w0, b0, w1, b1, w2, b2, wo, fr = fp
    n = 2 * HY_WIDTH
    consts = [w0, b0, w1, b1, w2, b2, wo, fr, deltas]
    return pl.pallas_call(
        _filter_kernel, name="hyena_filter",
        grid=(l // tl,),
        in_specs=[pl.BlockSpec((tl, LANES), lambda t: (t, 0))] + [_resident(c.shape) for c in consts],
        out_specs=pl.BlockSpec((2, tl, n), lambda t: (0, t, 0)),
        out_shape=jax.ShapeDtypeStruct((2, l, n), F32),
        compiler_params=_cparams(("parallel",), 32),
    )(zfeat, *consts)


def _spectrum_kernel(a_ref, d_ref, c_ref, s_ref, kr_ref, ki_ref):
    kr_ref[...] = _dot(c_ref[...], a_ref[0].astype(BF16))
    ki_ref[...] = -_dot(s_ref[...], d_ref[0].astype(BF16))


def _filter_spectrum(ad, tabs):
    _, l, n = ad.shape
    fp = tabs["c_hi"].shape[0]
    fb = tabs["fb"]
    cb = 256
    tab = pl.BlockSpec((fb, l), lambda j, k: (k, 0))
    out = pl.BlockSpec((fb, cb), lambda j, k: (k, j))
    return pl.pallas_call(
        _spectrum_kernel, name="dense_spectrum",
        grid=(n // cb, fp // fb),
        in_specs=[pl.BlockSpec((1, l, cb), lambda j, k: (0, 0, j)),
                  pl.BlockSpec((1, l, cb), lambda j, k: (1, 0, j)), tab, tab],
        out_specs=[out, out],
        out_shape=[jax.ShapeDtypeStruct((fp, n), F32)] * 2,
        compiler_params=_cparams(("parallel", "parallel"), 48),
    )(ad, ad, tabs["c_hi"], tabs["s_hi"])


def _longconv_kernel(u_ref, g_ref, kr_ref, ki_ref, bias_ref, wk_ref, c_ref, s_ref, ct_ref, st_ref,
                     o_ref, ub_ref, acc_ref):
    kb = pl.program_id(1)
    nb, _, cb = u_ref.shape

    @pl.when(kb == 0)
    def _():
        ub_ref[...] = jnp.concatenate([u_ref[i].astype(BF16) for i in range(nb)], axis=1)
        acc_ref[...] = jnp.zeros_like(acc_ref)

    ub = ub_ref[...]
    xr = _dot(c_ref[...], ub)
    xi = -_dot(s_ref[...], ub)
    kr = jnp.tile(kr_ref[...], (1, nb))
    ki = jnp.tile(ki_ref[...], (1, nb))
    wk = wk_ref[...]
    yr = ((xr * kr - xi * ki) * wk).astype(BF16)
    yi = ((xr * ki + xi * kr) * wk).astype(BF16)
    acc_ref[...] += _dot(ct_ref[...], yr) - _dot(st_ref[...], yi)

    @pl.when(kb == pl.num_programs(1) - 1)
    def _():
        for i in range(nb):
            o_ref[i] = g_ref[i] * (acc_ref[:, i * cb:(i + 1) * cb] + u_ref[i] * bias_ref[...])


def _gated_long_conv(u_arr, g_arr, kr, ki, k_col, bias, tabs):
    b, l, _ = u_arr.shape
    cb = 256
    ncb = HY_WIDTH // cb
    fp = tabs["c_hi"].shape[0]
    fb = tabs["fb"]
    tab = pl.BlockSpec((fb, l), lambda j, k: (k, 0))
    tabt = pl.BlockSpec((l, fb), lambda j, k: (0, k))
    spec = pl.BlockSpec((fb, cb), lambda j, k: (k, k_col * ncb + j))
    tok = pl.BlockSpec((b, l, cb), lambda j, k: (0, 0, j))
    return pl.pallas_call(
        _longconv_kernel, name="dense_long_conv",
        grid=(ncb, fp // fb),
        in_specs=[tok, tok, spec, spec,
                  pl.BlockSpec((1, cb), lambda j, k: (0, j)),
                  pl.BlockSpec((fb, 1), lambda j, k: (k, 0)),
                  tab, tab, tabt, tabt],
        out_specs=tok,
        out_shape=jax.ShapeDtypeStruct((b, l, HY_WIDTH), F32),
        scratch_shapes=[pltpu.VMEM((l, b * cb), BF16), pltpu.VMEM((l, b * cb), F32)],
        compiler_params=_cparams(("parallel", "arbitrary"), 56),
    )(u_arr, g_arr, kr, ki, bias.reshape(1, HY_WIDTH), tabs["wk"],
      tabs["c_hi"], tabs["s_hi"], tabs["ct_hi"], tabs["st_hi"])


FFT_INNER = 64
FFT_COLS = HY_CHUNK
FFT_K2_CHUNK = 32
FFT_UNROLL = 64


def _fft_tables(l):
    n = 2 * l
    n1 = FFT_INNER
    n2 = n // n1
    h = n2 // 2
    ang = lambda idx, mod: (idx % mod).astype(F32) * (2.0 * math.pi / mod)
    k2 = jnp.arange(n2, dtype=jnp.int32)
    m2 = jnp.arange(h, dtype=jnp.int32)
    a1 = ang(k2[:, None] * m2[None, :], n2)
    c1, s1 = jnp.cos(a1), jnp.sin(a1)
    f1 = jnp.block([[c1, s1], [-s1, c1]])
    k1 = jnp.arange(n1, dtype=jnp.int32)
    a2 = ang(k1[None, None, :] * (k1[None, :, None] * n2 + k2[:, None, None]), n)
    c2, s2 = jnp.cos(a2), jnp.sin(a2)
    m = jnp.concatenate([jnp.concatenate([c2, s2], axis=2), jnp.concatenate([-s2, c2], axis=2)], axis=1)
    g = jnp.block([[c1.T, -s1.T], [s1.T, c1.T]]) * (1.0 / n)
    out = dict(n2=n2)
    for name, t in (("f1", f1), ("f1r", f1[:, :h]), ("m", m), ("mt", jnp.swapaxes(m, 1, 2)), ("g", g)):
        out[name + "_hi"] = t.astype(BF16)
    return out


def _mm(w, x):
    return _dot(w, x.astype(BF16))


def _row_block_matmul(w, src_ref, dst_ref, count):
    m, k = w.shape

    def body(i, carry):
        x = src_ref[pl.ds(pl.multiple_of(i * k, k), k), :]
        dst_ref[pl.ds(pl.multiple_of(i * m, m), m), :] = _mm(w, x).astype(dst_ref.dtype)
        return carry
    lax.fori_loop(0, count, body, 0, unroll=FFT_UNROLL)


def _short_conv_rows(z, w_ref, b_ref):
    l = z.shape[0]
    row = lax.broadcasted_iota(jnp.int32, z.shape, 0)
    prev = jnp.where(row == 0, 0.0, pltpu.roll(z, 1, axis=0))
    nxt = jnp.where(row == l - 1, 0.0, pltpu.roll(z, l - 1, axis=0))
    return w_ref[0:1, :] * prev + w_ref[1:2, :] * z + w_ref[2:3, :] * nxt + b_ref[...]


def _fftconv_kernel(u_ref, g_ref, uw_ref, ub_ref, gw_ref, gb_ref, kr_ref, ki_ref, bias_ref,
                    f1_ref, m_ref, mt_ref, gm_ref, o_ref, z_scr, y_scr, p_scr, q_scr, u_scr, *, u_conv):
    kc = pl.program_id(2)
    n1 = FFT_INNER
    cc = z_scr.shape[1]
    n2 = z_scr.shape[0] // n1
    h = n2 // 2

    @pl.when(kc == 0)
    def _stage1():
        z = []
        for s in range(2):
            u = u_ref[s, 0]
            if u_conv:
                u = _short_conv_rows(u.astype(F32), uw_ref, ub_ref)
                u_scr[s] = u
            z.append(jnp.swapaxes(u.astype(BF16).reshape(h, n1, cc), 0, 1))
        z_scr[...] = jnp.concatenate(z, axis=1).reshape(n1 * n2, cc)
        _row_block_matmul(f1_ref[...], z_scr, p_scr, n1)
        q_scr[...] = jnp.swapaxes(p_scr[...].reshape(n1, 2 * n2, cc), 0, 1).reshape(2 * n2 * n1, cc)

    ks = kr_ref.shape[0]
    rows = [(pl.multiple_of((kc * ks + j) * n1, n1), pl.multiple_of((n2 + kc * ks + j) * n1, n1)) for j in range(ks)]
    xs = [_mm(m_ref[j], jnp.concatenate([q_scr[pl.ds(r0, n1), :], q_scr[pl.ds(i0, n1), :]], axis=0))
          for j, (r0, i0) in enumerate(rows)]
    ys = []
    for j, x in enumerate(xs):
        xr, xi = x[:n1], x[n1:]
        kr, ki = kr_ref[j], ki_ref[j]
        ys.append(jnp.concatenate([xr * kr - xi * ki, xr * ki + xi * kr], axis=0))
    for j, (r0, i0) in enumerate(rows):
        b = _mm(mt_ref[j], ys[j]).astype(p_scr.dtype)
        p_scr[pl.ds(r0, n1), :] = b[:n1]
        p_scr[pl.ds(i0, n1), :] = b[n1:]

    @pl.when(kc == pl.num_programs(2) - 1)
    def _stage1_inv():
        q_scr[...] = jnp.swapaxes(p_scr[...].reshape(2 * n2, n1, cc), 0, 1).reshape(2 * n2 * n1, cc)
        _row_block_matmul(gm_ref[...], q_scr, y_scr, n1)
        y3 = y_scr[...].reshape(n1, n2, cc)
        for s in range(2):
            y = jnp.swapaxes(y3[:, s * h:(s + 1) * h, :], 0, 1).reshape(h * n1, cc)
            gate = _short_conv_rows(g_ref[s, 0].astype(F32), gw_ref, gb_ref)
            u = u_scr[s] if u_conv else u_ref[s, 0]
            o_ref[s, 0] = (gate * (y + u * bias_ref[...])).astype(o_ref.dtype)


def _fft_gated_long_conv(u, u_col, z_hy, g_col, short_w, short_b, kr, ki, k_col, bias, ft, out_dtype):
    b, _, l, cc = z_hy.shape
    c = HY_WIDTH
    n1, n2, ks = FFT_INNER, ft["n2"], FFT_K2_CHUNK
    p, ncc = b // 2, c // cc
    u_conv = u is None
    if u_conv:
        u = z_hy
    tok = lambda col: pl.BlockSpec((2, 1, None, l, cc), lambda q, j, k: (0, q, col * ncc + j, 0, 0))
    taps = lambda col: pl.BlockSpec((3, cc), lambda q, j, k: (0, col * ncc + j))
    row = lambda col: pl.BlockSpec((1, cc), lambda q, j, k: (0, col * ncc + j))
    filt = pl.BlockSpec((ks, n1, cc), lambda q, j, k: (k, 0, k_col * ncc + j))
    tab = pl.BlockSpec((ks, 2 * n1, 2 * n1), lambda q, j, k: (k, 0, 0))
    res = lambda a: pl.BlockSpec(a.shape, lambda q, j, k: (0,) * a.ndim)
    sb = short_b.reshape(1, -1)
    y = pl.pallas_call(
        functools.partial(_fftconv_kernel, u_conv=u_conv), name="fft_long_conv",
        grid=(p, ncc, n2 // ks),
        in_specs=[tok(u_col), tok(g_col), taps(u_col), row(u_col), taps(g_col), row(g_col), filt, filt, row(0),
                  res(ft["f1_hi"]), tab, tab, res(ft["g_hi"])],
        out_specs=tok(0),
        out_shape=jax.ShapeDtypeStruct((2, p, ncc, l, cc), out_dtype),
        scratch_shapes=[pltpu.VMEM((n1 * n2, cc), BF16), pltpu.VMEM((n1 * n2, cc), F32),
                        pltpu.VMEM((2 * n2 * n1, cc), BF16), pltpu.VMEM((2 * n2 * n1, cc), BF16),
                        pltpu.VMEM((2, l, cc) if u_conv else (2, 8, LANES), F32)],
        compiler_params=_cparams(("parallel", "parallel", "arbitrary"), 56),
    )(u.reshape((2, p) + u.shape[1:]), z_hy.reshape((2, p) + z_hy.shape[1:]), short_w, sb, short_w, sb, kr, ki,
      bias.reshape(1, c), ft["f1_hi"], ft["m_hi"], ft["mt_hi"], ft["g_hi"])
    return y.reshape(b, ncc, l, cc)


def _fftspec_kernel(a_ref, d_ref, f1_ref, m_ref, kr_ref, ki_ref, z_scr, p_scr, q_scr):
    n1 = FFT_INNER
    cc = z_scr.shape[1]
    h = z_scr.shape[0] // n1
    n2 = 2 * h
    for src_ref, out_ref, lo in ((a_ref, kr_ref, 0), (d_ref, ki_ref, n1)):
        z_scr[...] = jnp.swapaxes(src_ref[0].astype(BF16).reshape(h, n1, cc), 0, 1).reshape(n1 * h, cc)
        _row_block_matmul(f1_ref[...], z_scr, p_scr, n1)
        q_scr[...] = jnp.swapaxes(p_scr[...].reshape(n1, 2 * n2, cc), 0, 1).reshape(2 * n2 * n1, cc)

        def body(k2, carry):
            r0 = pl.multiple_of(k2 * n1, n1)
            i0 = pl.multiple_of((n2 + k2) * n1, n1)
            slab = jnp.concatenate([q_scr[pl.ds(r0, n1), :], q_scr[pl.ds(i0, n1), :]], axis=0)
            out_ref[k2] = _mm(m_ref[k2][lo:lo + n1], slab)
            return carry
        lax.fori_loop(0, n2, body, 0, unroll=FFT_UNROLL)


def _fft_filter_spectrum(ad, ft):
    _, l, n = ad.shape
    n1, n2, cc = FFT_INNER, ft["n2"], FFT_COLS
    h = n2 // 2
    res = lambda a: pl.BlockSpec(a.shape, lambda j: (0,) * a.ndim)
    out = pl.BlockSpec((n2, n1, cc), lambda j: (0, 0, j))
    tabs = [ft["f1r_hi"], ft["m_hi"]]
    return pl.pallas_call(
        _fftspec_kernel, name="fft_filter_spectrum",
        grid=(n // cc,),
        in_specs=[pl.BlockSpec((1, l, cc), lambda j: (0, 0, j)), pl.BlockSpec((1, l, cc), lambda j: (1, 0, j))]
                 + [res(t) for t in tabs],
        out_specs=[out, out],
        out_shape=[jax.ShapeDtypeStruct((n2, n1, n), F32)] * 2,
        scratch_shapes=[pltpu.VMEM((n1 * h, cc), BF16), pltpu.VMEM((2 * n2 * n1, cc), BF16),
                        pltpu.VMEM((2 * n2 * n1, cc), BF16)],
        compiler_params=_cparams(("parallel",), 56),
    )(ad, ad, *tabs)


def _softmax_weights(s, sink=None):
    m = jnp.max(s, axis=-1, keepdims=True)
    if sink is not None:
        m = jnp.maximum(m, sink)
    p = jnp.exp(s - m)
    den = jnp.sum(p, axis=-1, keepdims=True)
    if sink is not None:
        den = den + jnp.exp(sink - m)
    return p.astype(BF16), den


def _softmax_pv(s, v, sink=None):
    p, den = _softmax_weights(s, sink)
    return _dot(p, v) / den


def _qk(q, k):
    return lax.dot_general(q, k, (((1,), (1,)), ((), ())), preferred_element_type=F32)


def _half_mask(q_tile, half):
    lane = lax.broadcasted_iota(jnp.int32, q_tile.shape, 1)
    keep = (lane < HEAD_DIM) if half == 0 else (lane >= HEAD_DIM)
    return jnp.where(keep, q_tile, jnp.zeros_like(q_tile))


def _merge_halves(o_even, o_odd):
    lane = lax.broadcasted_iota(jnp.int32, o_even.shape, 1)
    return jnp.where(lane < HEAD_DIM, o_even, o_odd)


SWA_CHAIN_HEADS = 2

def _swa_kernel(sink_ref, q_ref, kp_ref, kc_ref, kn_ref, vp_ref, vc_ref, vn_ref, kx_ref, vx_ref, o_ref):
    n = pl.program_id(1)
    nb = pl.num_programs(1)
    blk = SWA_BLOCK
    g = SWA_CHAIN_HEADS
    n_ctx = kx_ref.shape[1]
    rows = g * blk
    ri = lax.broadcasted_iota(jnp.int32, (rows, 3 * blk + n_ctx), 0) & (blk - 1)
    ci = lax.broadcasted_iota(jnp.int32, (rows, 3 * blk + n_ctx), 1)
    ok_prev = (ci < blk) & (ci >= ri) & (n > 0)
    ok_cur = (ci >= blk) & (ci < 2 * blk)
    ok_next = (ci >= 2 * blk) & (ci < 3 * blk) & (ci - 2 * blk <= ri) & (n < nb - 1)
    ok = ok_prev | ok_cur | ok_next | (ci >= 3 * blk)
    bias = jnp.where(ok, 0.0, NEG_BIG)
    rsel = lax.broadcasted_iota(jnp.int32, (rows, 1), 0) >> (blk.bit_length() - 1)
    outs = [None] * SWA_HEADS
    chains = range(SWA_HEADS // g)
    group = SWA_HEADS // SWA_KV_HEADS
    lanes = [slice((c * g // group) * LANES, (c * g // group + 1) * LANES) for c in chains]
    scores, sinks = [], []
    for c in chains:
        cs = lanes[c]
        k = jnp.concatenate([kp_ref[0, :, cs], kc_ref[0, :, cs], kn_ref[0, :, cs], kx_ref[0, :, cs]], axis=0)
        q_parts = []
        sink = jnp.zeros((rows, 1), F32)
        for gi in range(g):
            h = c * g + gi
            q_parts.append(_half_mask(q_ref[0, :, (h // 2) * LANES:(h // 2 + 1) * LANES], h % 2))
            sink = jnp.where(rsel == gi, sink_ref[h], sink)
        scores.append(_qk(jnp.concatenate(q_parts, axis=0), k))
        sinks.append(sink)
    probs = [_softmax_weights(scores[c] + bias, sinks[c]) for c in chains]
    for c in chains:
        cs = lanes[c]
        v = jnp.concatenate([vp_ref[0, :, cs], vc_ref[0, :, cs], vn_ref[0, :, cs], vx_ref[0, :, cs]], axis=0)
        pw, den = probs[c]
        o = _dot(pw, v) / den
        for gi in range(g):
            outs[c * g + gi] = o[gi * blk:(gi + 1) * blk]
    for p in range(SWA_HEADS // 2):
        o_ref[0, :, p * LANES:(p + 1) * LANES] = _merge_halves(outs[2 * p], outs[2 * p + 1]).astype(o_ref.dtype)


def _window_gqa(q, k, v, kx, vx, sink):
    b, l, _ = q.shape
    nb = l // SWA_BLOCK
    n_ctx = kx.shape[1]
    kvw = k.shape[2]

    def blk(off):
        return pl.BlockSpec((1, SWA_BLOCK, kvw), lambda b, n: (b, jnp.clip(n + off, 0, nb - 1), 0))

    ctx = pl.BlockSpec((1, n_ctx, kvw), lambda b, n: (b, 0, 0))
    return pl.pallas_call(
        _swa_kernel, name="window_gqa",
        grid=(b, nb),
        in_specs=[
            pl.BlockSpec(memory_space=pltpu.SMEM),
            pl.BlockSpec((1, SWA_BLOCK, q.shape[2]), lambda b, n: (b, n, 0)),
            blk(-1), blk(0), blk(1), blk(-1), blk(0), blk(1), ctx, ctx,
        ],
        out_specs=pl.BlockSpec((1, SWA_BLOCK, q.shape[2]), lambda b, n: (b, n, 0)),
        out_shape=jax.ShapeDtypeStruct(q.shape, BF16),
        compiler_params=_cparams(("parallel", "parallel"), 32),
    )(sink, q, k, k, k, v, v, v, kx, vx)


def _na_bias_kernel(rpb_ref, o_ref):
    h = pl.program_id(0)
    nd = 2 * NA_COLS - 1
    nr = 2 * NA_MAX_ROWS - 1
    q = lax.broadcasted_iota(jnp.int32, (GRID_W, LANES), 0)
    lane = lax.broadcasted_iota(jnp.int32, (GRID_W, LANES), 1)
    kc = lane & (GRID_W - 1)
    upper = lane >= GRID_W
    cstart = jnp.clip(q - NA_COLS // 2, 0, GRID_W - NA_COLS)
    valid = (kc >= cstart) & (kc < cstart + NA_COLS)
    dc = jnp.clip(kc - q, -(NA_COLS - 1), NA_COLS - 1) + (NA_COLS - 1)
    tiles = []
    for dr in range(nr - 1):
        base0 = (h * nr + dr) * nd
        base1 = base0 + nd
        t = jnp.zeros((GRID_W, LANES), F32)
        for d in range(nd):
            val = jnp.where(upper, rpb_ref[base1 + d], rpb_ref[base0 + d])
            t = jnp.where(dc == d, val, t)
        tiles.append(jnp.where(valid, t, NEG_BIG))
    for off in range(NA_MAX_ROWS):
        for j in range(NA_MAX_ROWS // 2):
            o_ref[off, 0, :, j * LANES:(j + 1) * LANES] = tiles[2 * j - off + NA_MAX_ROWS - 1]


def _na_bias_table(rpb):
    return pl.pallas_call(
        _na_bias_kernel, name="na_bias_table",
        grid=(NA_HEADS,),
        in_specs=[pl.BlockSpec(memory_space=pltpu.SMEM)],
        out_specs=pl.BlockSpec((NA_MAX_ROWS, 1, GRID_W, NA_MAX_ROWS * GRID_W), lambda h: (0, h, 0, 0)),
        out_shape=jax.ShapeDtypeStruct((NA_MAX_ROWS, NA_HEADS, GRID_W, NA_MAX_ROWS * GRID_W), F32),
        compiler_params=_cparams(("arbitrary",), 16),
    )(rpb.reshape(-1))


NA_ROWS_PER_STEP = 8


def _na_kernel(q_ref, k_ref, v_ref, kx_ref, vx_ref, bias_ref, o_ref, *, rows):
    wr = NA_MAX_ROWS
    nloc = wr * GRID_W
    pairs = range(NA_HEADS // 2)
    lanes = [slice(p * LANES, (p + 1) * LANES) for p in pairs]
    for i in range(NA_ROWS_PER_STEP):
        r = pl.program_id(1) * NA_ROWS_PER_STEP + i
        first = jnp.clip(r - wr // 2, 0, rows - wr)
        start = pl.multiple_of(first * GRID_W, GRID_W)
        off = r - first
        qr = slice(i * GRID_W, (i + 1) * GRID_W)
        scores = []
        for p in pairs:
            q_tile = q_ref[0, qr, lanes[p]]
            q2 = jnp.concatenate([_half_mask(q_tile, 0), _half_mask(q_tile, 1)], axis=0)
            k = jnp.concatenate([k_ref[0, pl.ds(start, nloc), lanes[p]], kx_ref[0, :, lanes[p]]], axis=0)
            scores.append(_qk(q2, k))
        probs = []
        for p in pairs:
            s = scores[p]
            bias = jnp.concatenate([bias_ref[off, 2 * p], bias_ref[off, 2 * p + 1]], axis=0)
            probs.append(_softmax_weights(jnp.concatenate([s[:, :nloc] + bias, s[:, nloc:]], axis=1)))
        for p in pairs:
            pw, den = probs[p]
            v = jnp.concatenate([v_ref[0, pl.ds(start, nloc), lanes[p]], vx_ref[0, :, lanes[p]]], axis=0)
            o = _dot(pw, v) / den
            o_ref[0, qr, lanes[p]] = _merge_halves(o[:GRID_W], o[GRID_W:]).astype(o_ref.dtype)


def _neighbourhood_attention(q, k, v, kx, vx, bias_tab):
    b, l, w = q.shape
    rows = l // GRID_W
    n_ctx = kx.shape[1]
    tq = NA_ROWS_PER_STEP * GRID_W
    full = pl.BlockSpec((1, l, w), lambda b, r: (b, 0, 0))
    ctx = pl.BlockSpec((1, n_ctx, w), lambda b, r: (b, 0, 0))
    return pl.pallas_call(
        functools.partial(_na_kernel, rows=rows), name="neighbourhood_attention",
        grid=(b, rows // NA_ROWS_PER_STEP),
        in_specs=[pl.BlockSpec((1, tq, w), lambda b, r: (b, r, 0)), full, full, ctx, ctx,
                  _resident(bias_tab.shape)],
        out_specs=pl.BlockSpec((1, tq, w), lambda b, r: (b, r, 0)),
        out_shape=jax.ShapeDtypeStruct(q.shape, BF16),
        compiler_params=_cparams(("parallel", "arbitrary"), 48),
    )(q, k, v, kx, vx, bias_tab)


def _ctx_attn_kernel(sink_ref, qs_ref, ks_ref, vs_ref, qn_ref, kn_ref, vn_ref, os_ref, on_ref):
    n = qs_ref.shape[1]
    g = SWA_HEADS // SWA_KV_HEADS
    rsel = lax.broadcasted_iota(jnp.int32, (g * n, 1), 0) >> (n.bit_length() - 1)
    outs = [None] * SWA_HEADS
    for kv in range(SWA_KV_HEADS):
        cs = slice(kv * LANES, (kv + 1) * LANES)
        q_parts = []
        sink = jnp.zeros((g * n, 1), F32)
        for gi in range(g):
            h = kv * g + gi
            q_parts.append(_half_mask(qs_ref[0, :, (h // 2) * LANES:(h // 2 + 1) * LANES], h % 2))
            sink = jnp.where(rsel == gi, sink_ref[h], sink)
        o = _softmax_pv(_qk(jnp.concatenate(q_parts, axis=0), ks_ref[0, :, cs]), vs_ref[0, :, cs], sink)
        for gi in range(g):
            outs[kv * g + gi] = o[gi * n:(gi + 1) * n]
    for p in range(SWA_HEADS // 2):
        os_ref[0, :, p * LANES:(p + 1) * LANES] = _merge_halves(outs[2 * p], outs[2 * p + 1]).astype(os_ref.dtype)
    for p in range(NA_HEADS // 2):
        cs = slice(p * LANES, (p + 1) * LANES)
        q_tile = qn_ref[0, :, cs]
        q2 = jnp.concatenate([_half_mask(q_tile, 0), _half_mask(q_tile, 1)], axis=0)
        o = _softmax_pv(_qk(q2, kn_ref[0, :, cs]), vn_ref[0, :, cs])
        on_ref[0, :, cs] = _merge_halves(o[:n], o[n:]).astype(on_ref.dtype)


def _context_attention(qs, ks, vs, qn, kn, vn, sink):
    b, n, w = qs.shape
    spec = lambda a: pl.BlockSpec((1,) + a.shape[1:], lambda b: (b, 0, 0))
    return pl.pallas_call(
        _ctx_attn_kernel, name="context_attention",
        grid=(b,),
        in_specs=[pl.BlockSpec(memory_space=pltpu.SMEM)] + [spec(a) for a in (qs, ks, vs, qn, kn, vn)],
        out_specs=[spec(qs), spec(qn)],
        out_shape=[jax.ShapeDtypeStruct(qs.shape, BF16), jax.ShapeDtypeStruct(qn.shape, BF16)],
        compiler_params=_cparams(("parallel",), 32),
    )(sink, qs, ks, vs, qn, kn, vn)


def _merge_ffn_kernel(x_ref, mg_ref, yh_ref, ys_ref, yn_ref, zg_ref, wb_ref, wo_ref,
                      sh_ref, sc_ref, gt_ref, g_ref, wg_ref, wu_ref, wd_ref, o_ref, *, fc):
    d = D_MODEL
    yh = jnp.concatenate([yh_ref[0, c] for c in range(yh_ref.shape[1])], axis=1)
    ys = (yh.astype(BF16), ys_ref[0], yn_ref[0])
    m = None
    for n in range(3):
        proj = _dot(ys[n], wb_ref[n])
        term = zg_ref[0, :, n * d:(n + 1) * d].astype(F32) * proj
        m = term if m is None else m + term
    x = x_ref[0] + mg_ref[0] * _dot(m.astype(BF16), wo_ref[...])
    o_ref[0] = _swiglu_residual(x, sh_ref, sc_ref, gt_ref, g_ref, wg_ref, wu_ref, wd_ref, fc)


def _merge_and_ffn(x, mix_gate, y_hy, y_swa, y_na, gates, w_branch, w_out, layer,
                   shift, scale, gate, gain, wg, wu, wd):
    b, l, d = x.shape
    tm = min(512, l)
    tok = lambda n: pl.BlockSpec((1, tm, n), lambda b, t: (b, t, 0))
    idx = (layer, 1)
    return pl.pallas_call(
        functools.partial(_merge_ffn_kernel, fc=FFN_CHUNK), name="merge_ffn",
        grid=(b, l // tm),
        in_specs=[tok(d), _mod_spec(mix_gate),
                  pl.BlockSpec((1, HY_WIDTH // HY_CHUNK, tm, HY_CHUNK), lambda b, t: (b, 0, t, 0)),
                  tok(HY_WIDTH), tok(HY_WIDTH), tok(3 * d),
                  _layer_slab(w_branch, (layer,)), _layer_slab(w_out, (layer,)),
                  _mod_spec(shift), _mod_spec(scale), _mod_spec(gate), _resident((1, d)),
                  _layer_slab(wg, idx), _layer_slab(wu, idx), _layer_slab(wd, idx)],
        out_specs=tok(d),
        out_shape=jax.ShapeDtypeStruct(x.shape, F32),
        compiler_params=_cparams(("parallel", "parallel"), 58),
    )(x, mix_gate[0], y_hy, y_swa, y_na, gates, w_branch, w_out, shift[0], scale[0], gate[0], gain.reshape(1, d),
      wg, wu, wd)


def _dft_tables(l):
    n = 2 * l
    fb = 256 if l >= 1024 else LANES * ((l + 1 + LANES - 1) // LANES)
    fp = fb * ((l + 1 + fb - 1) // fb)
    k = jnp.arange(fp, dtype=jnp.int32)[:, None]
    t = jnp.arange(l, dtype=jnp.int32)[None, :]
    ang = ((k * t) % n).astype(F32) * (2.0 * math.pi / n)
    live = k <= l
    c = jnp.where(live, jnp.cos(ang), 0.0)
    s = jnp.where(live, jnp.sin(ang), 0.0)
    c_hi = c.astype(BF16)
    s_hi = s.astype(BF16)
    wk = jnp.where((k == 0) | (k == l), 1.0, 2.0) * jnp.where(live, 1.0 / n, 0.0)
    return dict(fb=fb, c_hi=c_hi, s_hi=s_hi, ct_hi=c_hi.T, st_hi=s_hi.T, wk=wk.astype(F32))


def _filter_features(l):
    t = jnp.linspace(0.0, 1.0, l, dtype=F32)[:, None]
    w = (2.0 * math.pi / l) * jnp.arange(l, dtype=F32)[:, None]
    f = jnp.linspace(1e-4, HY_BANDS - 1, HY_BANDS, dtype=F32)[None, :]
    z = jnp.concatenate([t, jnp.cos(f * w), -jnp.sin(f * w)], axis=-1)
    return jnp.pad(z, ((0, 0), (0, LANES - HY_EMB)))


def _decay_rates():
    max_decay = math.log(HY_TARGET) / HY_FAST_DECAY
    min_decay = math.log(HY_TARGET) / HY_SLOW_DECAY
    return jnp.abs(jnp.linspace(min_decay, max_decay, HY_WIDTH, dtype=F32))[None, :]


def _rope_tables(l):
    pos = jnp.arange(l)
    row = (pos // GRID_W).astype(F32)
    col = (pos % GRID_W).astype(F32)
    half = HEAD_DIM // 2
    inv = 1.0 / (ROPE_BASE ** (jnp.arange(0, half, 2, dtype=F32) / half))
    ar = row[:, None] * inv[None, :]
    ac = col[:, None] * inv[None, :]
    cos = jnp.concatenate([jnp.cos(ar), jnp.cos(ar), jnp.cos(ac), jnp.cos(ac)], axis=-1)
    sin = jnp.concatenate([-jnp.sin(ar), jnp.sin(ar), -jnp.sin(ac), jnp.sin(ac)], axis=-1)
    return jnp.tile(cos, (1, 2)), jnp.tile(sin, (1, 2))


def _pad_to(a, shape):
    return jnp.pad(a, [(0, s - d) for d, s in zip(a.shape, shape)])


def _use_fft(bsz, length):
    return bsz % 2 == 0 and length % (8 * FFT_INNER) == 0


def _conv_tables(bsz, length):
    return _fft_tables(length) if _use_fft(bsz, length) else _dft_tables(length)


def _hyena_branch(z_hy, short_w, short_b, feat, fparams, deltas, bias, tabs):
    ad = _hyena_filter_sums(feat, fparams, deltas)
    if "n2" in tabs:
        kr, ki = _fft_filter_spectrum(ad, tabs)
        y1 = _fft_gated_long_conv(None, 0, z_hy, 1, short_w, short_b, kr, ki, 0, bias[0], tabs, F32)
        return _fft_gated_long_conv(y1, 0, z_hy, 2, short_w, short_b, kr, ki, 1, bias[1], tabs, BF16)
    v, x1, x2 = _short_conv3(z_hy, short_w, short_b)
    kr, ki = _filter_spectrum(ad, tabs)
    conv = functools.partial(_gated_long_conv, tabs=tabs)
    y1 = conv(v, x1, kr, ki, 0, bias[0])
    y = conv(y1, x2, kr, ki, 1, bias[1])
    bsz, length, _ = y.shape
    return y.reshape(bsz, length, HY_WIDTH // HY_CHUNK, HY_CHUNK).transpose(0, 2, 1, 3)


def kernel(x, c, ctx, c_ctx, w_ada, b_ada, norm_g, ffn_w_gate, ffn_w_up, ffn_w_down,
           w_in, hy_short_w, hy_short_b, hy_pe_w0, hy_pe_b0, hy_pe_w1, hy_pe_b1,
           hy_pe_w2, hy_pe_b2, hy_pe_wout, hy_sin_freq, hy_bias,
           swa_q_gain, swa_k_gain, swa_sink, na_q_gain, na_k_gain, na_rpb,
           w_branch, w_out):
    bsz, seq, d = x.shape
    n_ctx = ctx.shape[1]
    depth = w_ada.shape[0]

    c16 = _pad_to(jnp.concatenate([c, c_ctx[None, :]], axis=0), (MOD_ROWS, d))
    mods = _adaln_mods(c16, w_ada, b_ada).reshape(depth, MOD_ROWS, N_MOD, 1, d)

    tabs_x = _conv_tables(bsz, seq)
    tabs_c = _conv_tables(bsz, n_ctx)
    feat_x = _filter_features(seq)
    feat_c = _filter_features(n_ctx)
    deltas = _decay_rates()
    cos_x, sin_x = _rope_tables(seq)
    cos_c = jnp.ones((n_ctx, LANES), F32)
    sin_c = jnp.zeros((n_ctx, LANES), F32)
    eye = jnp.arange(MXU_DIM) // HEAD_DIM
    bd = (eye[:, None] == eye[None, :]).astype(BF16)

    wg = ffn_w_gate.astype(BF16)
    wu = ffn_w_up.astype(BF16)
    wd = ffn_w_down.astype(BF16)
    wb = w_branch.astype(BF16)
    wo = w_out.astype(BF16)
    win = w_in.astype(BF16)
    kv = win[:, :, _OFF_SWA + _SWA_Q:_OFF_NA].reshape(depth, d, 2, SWA_KV_HEADS, 1, HEAD_DIM)
    win_kv = jnp.broadcast_to(kv, (depth, d, 2, SWA_KV_HEADS, 2, HEAD_DIM)).reshape(depth, d, 4 * _SWA_KV)

    xc = ctx
    for i in range(depth):
        last = i == depth - 1
        mx = lambda j: (mods, i, j, None)
        mc = lambda j: (mods, i, j, bsz)
        tile2 = lambda g: jnp.tile(g, 2 * MXU_DIM // LANES)
        head_gains = jnp.stack([tile2(swa_q_gain[i]), tile2(swa_k_gain[i]), tile2(na_q_gain[i]), tile2(na_k_gain[i])])
        fparams = (_pad_to(hy_pe_w0[i], (LANES, LANES)), _pad_to(hy_pe_b0[i][None], (1, LANES)),
                   _pad_to(hy_pe_w1[i], (LANES, LANES)), _pad_to(hy_pe_b1[i][None], (1, LANES)),
                   _pad_to(hy_pe_w2[i], (LANES, LANES)), _pad_to(hy_pe_b2[i][None], (1, LANES)),
                   _pad_to(hy_pe_wout[i], (LANES, 4 * HY_WIDTH)), _pad_to(hy_sin_freq[i][None], (1, LANES)))

        x = _ffn_half_step(x, mx(0), mx(1), mx(2), norm_g[i, 0], wg, wu, wd, (i, 0))
        xc = _ffn_half_step(xc, mc(0), mc(1), mc(2), norm_g[i, 0], wg, wu, wd, (i, 0))

        z_hy, q_s, k_s, v_s, q_n, k_n, v_n, gates = _in_projection(
            x, mx(3), mx(4), norm_g[i, 1], cos_x, sin_x, head_gains, bd, win, win_kv, i)
        zc_hy, qc_s, kc_s, vc_s, qc_n, kc_n, vc_n, gates_c = _in_projection(
            xc, mc(3), mc(4), norm_g[i, 1], cos_c, sin_c, head_gains, bd, win, win_kv, i)

        y_hy = _hyena_branch(z_hy, hy_short_w[i], hy_short_b[i], feat_x, fparams, deltas, hy_bias[i], tabs_x)
        y_swa = _window_gqa(q_s, k_s, v_s, kc_s, vc_s, swa_sink[i])
        y_na = _neighbourhood_attention(q_n, k_n, v_n, kc_n, vc_n, _na_bias_table(na_rpb[i]))
        x = _merge_and_ffn(x, mx(5), y_hy, y_swa, y_na, gates, wb, wo, i,
                           mx(6), mx(7), mx(8), norm_g[i, 2], wg, wu, wd)

        if not last:
            yc_hy = _hyena_branch(zc_hy, hy_short_w[i], hy_short_b[i], feat_c, fparams, deltas, hy_bias[i], tabs_c)
            yc_swa, yc_na = _context_attention(qc_s, kc_s, vc_s, qc_n, kc_n, vc_n, swa_sink[i])
            xc = _merge_and_ffn(xc, mc(5), yc_hy, yc_swa, yc_na, gates_c, wb, wo, i,
                                mc(6), mc(7), mc(8), norm_g[i, 2], wg, wu, wd)
    return x
```
